```python
import functools
import jax
import jax.numpy as jnp
from jax import lax
import numpy as np

D_MODEL = 1024
BATCH = 32
SEQ = 2048
DEPTH = 4

GRID_W = 64
CTX_LEN = 256
D_MIX = D_MODEL
GLA_W = D_MIX // 4
RET_W = D_MIX // 4
MLA_W = D_MIX - GLA_W - RET_W
GLA_DV = 64
GLA_HEADS = GLA_W // GLA_DV
GLA_DK = GLA_DV // 2
GLA_QK = GLA_HEADS * GLA_DK
GLA_GATE_RANK = 16
GLA_TAU = 16.0
RET_DV = 64
RET_HEADS = RET_W // RET_DV
RET_DK = RET_DV // 2
RET_QK = RET_HEADS * RET_DK
MLA_DV = 64
MLA_HEADS = MLA_W // MLA_DV
MLA_D_NOPE = 64
MLA_D_ROPE = 32
MLA_Q_RANK = D_MODEL // 4
MLA_KV_RANK = D_MODEL // 8
MLA_SCALE = (MLA_D_NOPE + MLA_D_ROPE) ** -0.5
CHUNK = 64
Q_BLOCK = 128
D_FF = 128 * ((8 * D_MODEL // 3 + 127) // 128)
CONV_W = 3
ROPE_BASE = 10000.0
EPS = 1e-6
ALPHA = (2 * DEPTH) ** 0.25
BETA = (8 * DEPTH) ** -0.25
ADA_INIT = 0.5
IN_SIZES = (GLA_QK, GLA_QK, GLA_W, 2 * GLA_GATE_RANK, GLA_W,
            RET_QK, RET_QK, RET_W, RET_W,
            MLA_Q_RANK, MLA_KV_RANK, MLA_D_ROPE)
D_IN = sum(IN_SIZES)

kernel_name = "hybrid_gla_retnet_mla_prefix_dit"


def _layer_norm(x, g, b):
    xf = x.astype(jnp.float32)
    mu = jnp.mean(xf, axis=-1, keepdims=True)
    var = jnp.mean(jnp.square(xf - mu), axis=-1, keepdims=True)
    return ((xf - mu) * lax.rsqrt(var + EPS) * g + b).astype(x.dtype)


def _rms_norm(x, g):
    xf = x.astype(jnp.float32)
    return xf * lax.rsqrt(jnp.mean(jnp.square(xf), axis=-1, keepdims=True) + EPS) * g


def _group_norm(x):
    xf = x.astype(jnp.float32)
    mu = jnp.mean(xf, axis=-1, keepdims=True)
    var = jnp.mean(jnp.square(xf - mu), axis=-1, keepdims=True)
    return (xf - mu) * lax.rsqrt(var + EPS)


def _heads(t, n):
    return t.reshape(t.shape[0], t.shape[1], n, -1)


def _merge(t):
    return t.reshape(t.shape[0], t.shape[1], -1)


def _rot_half(x, cos, sin):
    cos = cos.astype(x.dtype)
    sin = sin.astype(x.dtype)
    x1, x2 = jnp.split(x, 2, axis=-1)
    return jnp.concatenate([x1 * cos - x2 * sin, x1 * sin + x2 * cos], axis=-1)


def _axial_rope(x, row_cos, row_sin, col_cos, col_sin):
    xr, xc = jnp.split(x, 2, axis=-1)
    return jnp.concatenate([_rot_half(xr, row_cos, row_sin), _rot_half(xc, col_cos, col_sin)], axis=-1)


def _project(h, w_in):
    p = jnp.einsum('bld,de->ble', h, w_in)
    return jnp.split(p, np.cumsum(IN_SIZES)[:-1].tolist(), axis=-1)


def _to_chunks(t):
    b, l, h, d = t.shape
    return t.reshape(b, l // CHUNK, CHUNK, h, d).transpose(1, 0, 3, 2, 4)


def _from_chunks(t):
    n, b, h, c, d = t.shape
    return t.transpose(1, 0, 3, 2, 4).reshape(b, n * c, h, d)


def _gla_log_gate(lr, w2, b2):
    z = jnp.einsum('blr,re->ble', lr, w2) + b2
    return _heads(jax.nn.log_sigmoid(z.astype(jnp.float32)) / GLA_TAU, GLA_HEADS)


def _gla_scan(q, k, v, log_a, s0):
    f32 = jnp.float32
    qc, kc, vc, ac = tuple(_to_chunks(t.astype(f32)) for t in (q, k, v, log_a))
    tri = jnp.tril(jnp.ones((CHUNK, CHUNK), dtype=bool))[:, :, None]

    def step(s, inp):
        qi, ki, vi, ai = inp
        b = jnp.cumsum(ai, axis=2)
        diff = b[:, :, :, None, :] - b[:, :, None, :, :]
        decay = jnp.exp(jnp.where(tri, diff, -jnp.inf))
        att = jnp.einsum('bhtd,bhsd,bhtsd->bhts', qi, ki, decay)
        o = (jnp.einsum('bhtd,bhde->bhte', qi * jnp.exp(b), s)
             + jnp.einsum('bhts,bhse->bhte', att, vi))
        b_end = b[:, :, -1:, :]
        s = (s * jnp.exp(b_end)[:, :, 0, :, None]
             + jnp.einsum('bhsd,bhse->bhde', ki * jnp.exp(b_end - b), vi))
        return s, o

    s_fin, o = lax.scan(step, s0, (qc, kc, vc, ac))
    return _from_chunks(o), s_fin


def _ret_scan(log_g, q, k, v, s0):
    f32 = jnp.float32
    qc, kc, vc = tuple(_to_chunks(t.astype(f32)) for t in (q, k, v))
    idx = jnp.arange(CHUNK, dtype=f32)
    rel = idx[:, None] - idx[None, :]
    dmat = jnp.where(rel >= 0, jnp.exp(jnp.maximum(rel, 0.0) * log_g[:, None, None]), 0.0)
    q_dec = jnp.exp((idx + 1.0) * log_g[:, None])[..., None]
    k_dec = jnp.exp((CHUNK - 1.0 - idx) * log_g[:, None])[..., None]
    s_dec = jnp.exp(CHUNK * log_g)[:, None, None]

    def step(s, inp):
        qi, ki, vi = inp
        att = jnp.einsum('bhtd,bhsd->bhts', qi, ki) * dmat
        o = (jnp.einsum('bhtd,bhde->bhte', qi * q_dec, s)
             + jnp.einsum('bhts,bhse->bhte', att, vi))
        s = s * s_dec + jnp.einsum('bhsd,bhse->bhde', ki * k_dec, vi)
        return s, o

    s_fin, o = lax.scan(step, s0, (qc, kc, vc))
    return _from_chunks(o), s_fin


def _bidir_prefix(scan_f, scan_b, ctx_f, lat_f, ctx_b, lat_b, s0):
    flip = lambda ts: [jnp.flip(t, axis=1) for t in ts]
    oc_f, sc_f = scan_f(*ctx_f, s0)
    ol_f, _ = scan_f(*lat_f, sc_f)
    oc_b, sc_b = scan_b(*flip(ctx_b), s0)
    ol_b, _ = scan_b(*flip(lat_b), sc_b)
    return oc_f + jnp.flip(oc_b, axis=1), ol_f + jnp.flip(ol_b, axis=1)


def _mla_attend(qn, qr, kn, kr, v):
    s = (jnp.einsum('bqhd,bkhd->bhqk', qn, kn)
         + jnp.einsum('bqhr,bkr->bhqk', qr, kr)).astype(jnp.float32) * MLA_SCALE
    p = jax.nn.softmax(s, axis=-1).astype(v.dtype)
    return jnp.einsum('bhqk,bkhe->bqhe', p, v)


def _mla_blocks(qn, qr, kn, kr, v):
    b, s = qn.shape[0], qn.shape[1]
    nb = s // Q_BLOCK
    blk = lambda t: jnp.moveaxis(t.reshape(b, nb, Q_BLOCK, *t.shape[2:]), 1, 0)
    out = lax.map(lambda qs: _mla_attend(qs[0], qs[1], kn, kr, v), (blk(qn), blk(qr)))
    return jnp.moveaxis(out, 0, 1).reshape(b, s, MLA_HEADS, MLA_DV)


def _token_mixers(h_c, h_l, rope, w_in, gla_gate_w, gla_gate_b, gla_norm_g, ret_decay,
                  mla_q_norm_g, mla_kv_norm_g, mla_w_uq, mla_w_uk, mla_w_uv, need_ctx):
    ret_cos, ret_sin, row_cos, row_sin, col_cos, col_sin = rope
    b = h_l.shape[0]
    pc = _project(h_c, w_in)
    pl = _project(h_l, w_in)

    def gla_inputs(p):
        q = _heads(p[0], GLA_HEADS) * (GLA_DK ** -0.5)
        k = _heads(p[1], GLA_HEADS)
        v = _heads(p[2], GLA_HEADS)
        lr_f, lr_b = jnp.split(p[3], 2, axis=-1)
        a_f = _gla_log_gate(lr_f, gla_gate_w[0], gla_gate_b[0])
        a_b = _gla_log_gate(lr_b, gla_gate_w[1], gla_gate_b[1])
        return (q, k, v, a_f), (q, k, v, a_b)

    gc_f, gc_b = gla_inputs(pc)
    gl_f, gl_b = gla_inputs(pl)
    s0_gla = jnp.zeros((b, GLA_HEADS, GLA_DK, GLA_DV), jnp.float32)
    gla_c, gla_l = _bidir_prefix(_gla_scan, _gla_scan, gc_f, gl_f, gc_b, gl_b, s0_gla)
    gla_out = lambda o, p: _merge(_rms_norm(o, gla_norm_g) * jax.nn.silu(_heads(p[4], GLA_HEADS)))

    log_g = jax.nn.log_sigmoid(ret_decay.astype(jnp.float32))

    def ret_inputs(p, rotate):
        q = _heads(p[5], RET_HEADS)
        k = _heads(p[6], RET_HEADS) * (RET_DK ** -0.5)
        v = _heads(p[7], RET_HEADS)
        if rotate:
            q = _rot_half(q, ret_cos[:, None], ret_sin[:, None])
            k = _rot_half(k, ret_cos[:, None], ret_sin[:, None])
        return (q, k, v)

    rc = ret_inputs(pc, False)
    rl = ret_inputs(pl, True)
    s0_ret = jnp.zeros((b, RET_HEADS, RET_DK, RET_DV), jnp.float32)
    ret_c, ret_l = _bidir_prefix(functools.partial(_ret_scan, log_g[0]),
                                 functools.partial(_ret_scan, log_g[1]),
                                 rc, rl, rc, rl, s0_ret)
    ret_out = lambda o, p: _merge(_group_norm(o) * jax.nn.silu(_heads(p[8], RET_HEADS)))

    def mla_inputs(p, rotate):
        cq = _rms_norm(p[9], mla_q_norm_g)
        q = _heads(jnp.einsum('blr,re->ble', cq, mla_w_uq), MLA_HEADS)
        qn, qr = q[..., :MLA_D_NOPE], q[..., MLA_D_NOPE:]
        ckv = _rms_norm(p[10], mla_kv_norm_g)
        kn = _heads(jnp.einsum('blr,re->ble', ckv, mla_w_uk), MLA_HEADS)
        v = _heads(jnp.einsum('blr,re->ble', ckv, mla_w_uv), MLA_HEADS)
        kr = p[11]
        if rotate:
            qr = _axial_rope(qr, row_cos[:, None], row_sin[:, None], col_cos[:, None], col_sin[:, None])
            kr = _axial_rope(kr, row_cos, row_sin, col_cos, col_sin)
        return qn, qr, kn, kr, v

    qn_c, qr_c, kn_c, kr_c, v_c = mla_inputs(pc, False)
    qn_l, qr_l, kn_l, kr_l, v_l = mla_inputs(pl, True)
    mla_l = _mla_blocks(qn_l, qr_l,
                        jnp.concatenate([kn_l, kn_c], axis=1),
                        jnp.concatenate([kr_l, kr_c], axis=1),
                        jnp.concatenate([v_l, v_c], axis=1))
    m_l = jnp.concatenate([gla_out(gla_l, pl), ret_out(ret_l, pl), _merge(mla_l)], axis=-1).astype(h_l.dtype)
    if not need_ctx:
        return None, m_l
    mla_c = _mla_attend(qn_c, qr_c, kn_c, kr_c, v_c)
    m_c = jnp.concatenate([gla_out(gla_c, pc), ret_out(ret_c, pc), _merge(mla_c)], axis=-1).astype(h_c.dtype)
    return m_c, m_l


def _conv_ffn(h, w_up, conv_w, conv_b, w_down):
    u = jnp.einsum('bld,df->blf', h, w_up)
    l = u.shape[1]
    pad = CONV_W // 2
    up = jnp.pad(u, ((0, 0), (pad, pad), (0, 0)))
    u = sum(up[:, j:j + l] * conv_w[j] for j in range(CONV_W)) + conv_b
    a, g = jnp.split(u, 2, axis=-1)
    return jnp.einsum('blf,fd->bld', jax.nn.silu(a) * g, w_down)


def _fwd_setup_inputs(seed: int = 0) -> dict:
    key = jax.random.key(seed)
    ks = jax.random.split(key, 28)
    f32 = jnp.float32
    nrm = lambda k, shape, s: jax.random.normal(k, shape, f32) * s
    L = DEPTH
    ret_base = jnp.log(2.0 ** (5.0 + jnp.arange(RET_HEADS, dtype=f32)) - 1.0)
    return {
        "x": nrm(ks[0], (BATCH, SEQ, D_MODEL), 1.0),
        "c": nrm(ks[1], (BATCH, D_MODEL), 1.0),
        "ctx": nrm(ks[2], (BATCH, CTX_LEN, D_MODEL), 1.0),
        "c_ctx": nrm(ks[3], (D_MODEL,), 1.0),
        "ada_w": nrm(ks[4], (L, D_MODEL, 6 * D_MODEL), ADA_INIT * D_MODEL ** -0.5),
        "ada_b": nrm(ks[5], (L, 6 * D_MODEL), 0.01),
        "w_in": nrm(ks[6], (L, D_MODEL, D_IN), D_MODEL ** -0.5),
        "gla_gate_w": nrm(ks[7], (L, 2, GLA_GATE_RANK, GLA_QK), GLA_GATE_RANK ** -0.5),
        "gla_gate_b": nrm(ks[8], (L, 2, GLA_QK), 0.1),
        "gla_norm_g": 1.0 + nrm(ks[9], (L, GLA_DV), 0.02),
        "ret_decay": ret_base + nrm(ks[10], (L, 2, RET_HEADS), 0.1),
        "mla_q_norm_g": 1.0 + nrm(ks[11], (L, MLA_Q_RANK), 0.02),
        "mla_kv_norm_g": 1.0 + nrm(ks[12], (L, MLA_KV_RANK), 0.02),
        "mla_w_uq": nrm(ks[13], (L, MLA_Q_RANK, MLA_HEADS * (MLA_D_NOPE + MLA_D_ROPE)), MLA_Q_RANK ** -0.5),
        "mla_w_uk": nrm(ks[14], (L, MLA_KV_RANK, MLA_HEADS * MLA_D_NOPE), MLA_KV_RANK ** -0.5),
        "mla_w_uv": nrm(ks[15], (L, MLA_KV_RANK, MLA_HEADS * MLA_DV), MLA_KV_RANK ** -0.5),
        "w_out": nrm(ks[16], (L, D_MIX, D_MODEL), BETA * D_MIX ** -0.5),
        "ln1_g": 1.0 + nrm(ks[17], (L, D_MODEL), 0.02),
        "ln1_b": nrm(ks[18], (L, D_MODEL), 0.02),
        "ffn_up": nrm(ks[19], (L, D_MODEL, 2 * D_FF), D_MODEL ** -0.5),
        "ffn_conv_w": nrm(ks[20], (L, CONV_W, 2 * D_FF), CONV_W ** -0.5),
        "ffn_conv_b": nrm(ks[21], (L, 2 * D_FF), 0.01),
        "ffn_down": nrm(ks[22], (L, D_FF, D_MODEL), BETA * D_FF ** -0.5),
        "ln2_g": 1.0 + nrm(ks[23], (L, D_MODEL), 0.02),
        "ln2_b": nrm(ks[24], (L, D_MODEL), 0.02),
    }


def _fwd_reference(x, c, ctx, c_ctx, ada_w, ada_b, w_in, gla_gate_w, gla_gate_b, gla_norm_g, ret_decay,
              mla_q_norm_g, mla_kv_norm_g, mla_w_uq, mla_w_uk, mla_w_uv, w_out, ln1_g, ln1_b,
              ffn_up, ffn_conv_w, ffn_conv_b, ffn_down, ln2_g, ln2_b):
    f32 = jnp.float32
    seq = x.shape[1]
    rows_n = seq // GRID_W
    rows = jnp.repeat(jnp.arange(rows_n, dtype=f32), GRID_W)
    cols = jnp.tile(jnp.arange(GRID_W, dtype=f32), rows_n)
    pos = jnp.arange(seq, dtype=f32)
    ret_inv = 1.0 / (ROPE_BASE ** jnp.linspace(0.0, 1.0, RET_DK // 2, dtype=f32))
    ret_ang = pos[:, None] * ret_inv
    n_ax = MLA_D_ROPE // 4
    ax_inv = ROPE_BASE ** (-jnp.arange(n_ax, dtype=f32) / n_ax)
    row_ang = rows[:, None] * ax_inv
    col_ang = cols[:, None] * ax_inv
    rope = (jnp.cos(ret_ang), jnp.sin(ret_ang), jnp.cos(row_ang), jnp.sin(row_ang),
            jnp.cos(col_ang), jnp.sin(col_ang))

    for i in range(DEPTH):
        need_ctx = i < DEPTH - 1
        mod_l = jnp.einsum('bd,de->be', jax.nn.silu(c), ada_w[i]) + ada_b[i]
        mod_c = jnp.einsum('d,de->e', jax.nn.silu(c_ctx), ada_w[i]) + ada_b[i]
        sh1_l, sc1_l, g1_l, sh2_l, sc2_l, g2_l = [m[:, None, :] for m in jnp.split(mod_l, 6, axis=-1)]
        sh1_c, sc1_c, g1_c, sh2_c, sc2_c, g2_c = jnp.split(mod_c, 6, axis=-1)

        h_l = x * (1.0 + sc1_l) + sh1_l
        h_c = ctx * (1.0 + sc1_c) + sh1_c
        m_c, m_l = _token_mixers(h_c, h_l, rope, w_in[i], gla_gate_w[i], gla_gate_b[i], gla_norm_g[i],
                                 ret_decay[i], mla_q_norm_g[i], mla_kv_norm_g[i], mla_w_uq[i],
                                 mla_w_uk[i], mla_w_uv[i], need_ctx)
        x = _layer_norm(ALPHA * x + g1_l * jnp.einsum('ble,ed->bld', m_l, w_out[i]), ln1_g[i], ln1_b[i])
        f_l = _conv_ffn(x * (1.0 + sc2_l) + sh2_l, ffn_up[i], ffn_conv_w[i], ffn_conv_b[i], ffn_down[i])
        x = _layer_norm(ALPHA * x + g2_l * f_l, ln2_g[i], ln2_b[i])
        if need_ctx:
            ctx = _layer_norm(ALPHA * ctx + g1_c * jnp.einsum('ble,ed->bld', m_c, w_out[i]), ln1_g[i], ln1_b[i])
            f_c = _conv_ffn(ctx * (1.0 + sc2_c) + sh2_c, ffn_up[i], ffn_conv_w[i], ffn_conv_b[i], ffn_down[i])
            ctx = _layer_norm(ALPHA * ctx + g2_c * f_c, ln2_g[i], ln2_b[i])
    return x


import jax as _jax
import jax.numpy as _jnp

TWIN_FORMAT = 'train_step'
FWD_PARAMS = ['x', 'c', 'ctx', 'c_ctx', 'ada_w', 'ada_b', 'w_in', 'gla_gate_w', 'gla_gate_b', 'gla_norm_g', 'ret_decay', 'mla_q_norm_g', 'mla_kv_norm_g', 'mla_w_uq', 'mla_w_uk', 'mla_w_uv', 'w_out', 'ln1_g', 'ln1_b', 'ffn_up', 'ffn_conv_w', 'ffn_conv_b', 'ffn_down', 'ln2_g', 'ln2_b']
TWIN_WEIGHTS = ['c_ctx', 'ada_w', 'ada_b', 'w_in', 'gla_gate_w', 'gla_gate_b', 'gla_norm_g', 'ret_decay', 'mla_q_norm_g', 'mla_kv_norm_g', 'mla_w_uq', 'mla_w_uk', 'mla_w_uv', 'w_out', 'ln1_g', 'ln1_b', 'ffn_up', 'ffn_conv_w', 'ffn_conv_b', 'ffn_down', 'ln2_g', 'ln2_b']
TWIN_DIFF_INPUT = 'x'
TWIN_INPUTS = ['x', 'c', 'ctx', 'c_ctx', 'ada_w', 'ada_b', 'w_in', 'gla_gate_w', 'gla_gate_b', 'gla_norm_g', 'ret_decay', 'mla_q_norm_g', 'mla_kv_norm_g', 'mla_w_uq', 'mla_w_uk', 'mla_w_uv', 'w_out', 'ln1_g', 'ln1_b', 'ffn_up', 'ffn_conv_w', 'ffn_conv_b', 'ffn_down', 'ln2_g', 'ln2_b', 'loss_target', 'm_c_ctx', 'm_ada_w', 'm_ada_b', 'm_w_in', 'm_gla_gate_w', 'm_gla_gate_b', 'm_gla_norm_g', 'm_ret_decay', 'm_mla_q_norm_g', 'm_mla_kv_norm_g', 'm_mla_w_uq', 'm_mla_w_uk', 'm_mla_w_uv', 'm_w_out', 'm_ln1_g', 'm_ln1_b', 'm_ffn_up', 'm_ffn_conv_w', 'm_ffn_conv_b', 'm_ffn_down', 'm_ln2_g', 'm_ln2_b', 'v_c_ctx', 'v_ada_w', 'v_ada_b', 'v_w_in', 'v_gla_gate_w', 'v_gla_gate_b', 'v_gla_norm_g', 'v_ret_decay', 'v_mla_q_norm_g', 'v_mla_kv_norm_g', 'v_mla_w_uq', 'v_mla_w_uk', 'v_mla_w_uv', 'v_w_out', 'v_ln1_g', 'v_ln1_b', 'v_ffn_up', 'v_ffn_conv_w', 'v_ffn_conv_b', 'v_ffn_down', 'v_ln2_g', 'v_ln2_b']
TWIN_OUTPUTS = ['loss', 'grad_x', 'grad_c_ctx', 'grad_ada_w', 'grad_ada_b', 'grad_w_in', 'grad_gla_gate_w', 'grad_gla_gate_b', 'grad_gla_norm_g', 'grad_ret_decay', 'grad_mla_q_norm_g', 'grad_mla_kv_norm_g', 'grad_mla_w_uq', 'grad_mla_w_uk', 'grad_mla_w_uv', 'grad_w_out', 'grad_ln1_g', 'grad_ln1_b', 'grad_ffn_up', 'grad_ffn_conv_w', 'grad_ffn_conv_b', 'grad_ffn_down', 'grad_ln2_g', 'grad_ln2_b', 'delta_c_ctx', 'delta_ada_w', 'delta_ada_b', 'delta_w_in', 'delta_gla_gate_w', 'delta_gla_gate_b', 'delta_gla_norm_g', 'delta_ret_decay', 'delta_mla_q_norm_g', 'delta_mla_kv_norm_g', 'delta_mla_w_uq', 'delta_mla_w_uk', 'delta_mla_w_uv', 'delta_w_out', 'delta_ln1_g', 'delta_ln1_b', 'delta_ffn_up', 'delta_ffn_conv_w', 'delta_ffn_conv_b', 'delta_ffn_down', 'delta_ln2_g', 'delta_ln2_b', 'new_m_c_ctx', 'new_m_ada_w', 'new_m_ada_b', 'new_m_w_in', 'new_m_gla_gate_w', 'new_m_gla_gate_b', 'new_m_gla_norm_g', 'new_m_ret_decay', 'new_m_mla_q_norm_g', 'new_m_mla_kv_norm_g', 'new_m_mla_w_uq', 'new_m_mla_w_uk', 'new_m_mla_w_uv', 'new_m_w_out', 'new_m_ln1_g', 'new_m_ln1_b', 'new_m_ffn_up', 'new_m_ffn_conv_w', 'new_m_ffn_conv_b', 'new_m_ffn_down', 'new_m_ln2_g', 'new_m_ln2_b', 'new_v_c_ctx', 'new_v_ada_w', 'new_v_ada_b', 'new_v_w_in', 'new_v_gla_gate_w', 'new_v_gla_gate_b', 'new_v_gla_norm_g', 'new_v_ret_decay', 'new_v_mla_q_norm_g', 'new_v_mla_kv_norm_g', 'new_v_mla_w_uq', 'new_v_mla_w_uk', 'new_v_mla_w_uv', 'new_v_w_out', 'new_v_ln1_g', 'new_v_ln1_b', 'new_v_ffn_up', 'new_v_ffn_conv_w', 'new_v_ffn_conv_b', 'new_v_ffn_down', 'new_v_ln2_g', 'new_v_ln2_b']
TWIN_LEAF_KINDS = {'loss': 'loss', 'grad_x': 'grad_x', 'grad_c_ctx': 'grad_w', 'grad_ada_w': 'grad_w', 'grad_ada_b': 'grad_w', 'grad_w_in': 'grad_w', 'grad_gla_gate_w': 'grad_w', 'grad_gla_gate_b': 'grad_w', 'grad_gla_norm_g': 'grad_w', 'grad_ret_decay': 'grad_w', 'grad_mla_q_norm_g': 'grad_w', 'grad_mla_kv_norm_g': 'grad_w', 'grad_mla_w_uq': 'grad_w', 'grad_mla_w_uk': 'grad_w', 'grad_mla_w_uv': 'grad_w', 'grad_w_out': 'grad_w', 'grad_ln1_g': 'grad_w', 'grad_ln1_b': 'grad_w', 'grad_ffn_up': 'grad_w', 'grad_ffn_conv_w': 'grad_w', 'grad_ffn_conv_b': 'grad_w', 'grad_ffn_down': 'grad_w', 'grad_ln2_g': 'grad_w', 'grad_ln2_b': 'grad_w', 'delta_c_ctx': 'delta_w', 'delta_ada_w': 'delta_w', 'delta_ada_b': 'delta_w', 'delta_w_in': 'delta_w', 'delta_gla_gate_w': 'delta_w', 'delta_gla_gate_b': 'delta_w', 'delta_gla_norm_g': 'delta_w', 'delta_ret_decay': 'delta_w', 'delta_mla_q_norm_g': 'delta_w', 'delta_mla_kv_norm_g': 'delta_w', 'delta_mla_w_uq': 'delta_w', 'delta_mla_w_uk': 'delta_w', 'delta_mla_w_uv': 'delta_w', 'delta_w_out': 'delta_w', 'delta_ln1_g': 'delta_w', 'delta_ln1_b': 'delta_w', 'delta_ffn_up': 'delta_w', 'delta_ffn_conv_w': 'delta_w', 'delta_ffn_conv_b': 'delta_w', 'delta_ffn_down': 'delta_w', 'delta_ln2_g': 'delta_w', 'delta_ln2_b': 'delta_w', 'new_m_c_ctx': 'new_m', 'new_m_ada_w': 'new_m', 'new_m_ada_b': 'new_m', 'new_m_w_in': 'new_m', 'new_m_gla_gate_w': 'new_m', 'new_m_gla_gate_b': 'new_m', 'new_m_gla_norm_g': 'new_m', 'new_m_ret_decay': 'new_m', 'new_m_mla_q_norm_g': 'new_m', 'new_m_mla_kv_norm_g': 'new_m', 'new_m_mla_w_uq': 'new_m', 'new_m_mla_w_uk': 'new_m', 'new_m_mla_w_uv': 'new_m', 'new_m_w_out': 'new_m', 'new_m_ln1_g': 'new_m', 'new_m_ln1_b': 'new_m', 'new_m_ffn_up': 'new_m', 'new_m_ffn_conv_w': 'new_m', 'new_m_ffn_conv_b': 'new_m', 'new_m_ffn_down': 'new_m', 'new_m_ln2_g': 'new_m', 'new_m_ln2_b': 'new_m', 'new_v_c_ctx': 'new_v', 'new_v_ada_w': 'new_v', 'new_v_ada_b': 'new_v', 'new_v_w_in': 'new_v', 'new_v_gla_gate_w': 'new_v', 'new_v_gla_gate_b': 'new_v', 'new_v_gla_norm_g': 'new_v', 'new_v_ret_decay': 'new_v', 'new_v_mla_q_norm_g': 'new_v', 'new_v_mla_kv_norm_g': 'new_v', 'new_v_mla_w_uq': 'new_v', 'new_v_mla_w_uk': 'new_v', 'new_v_mla_w_uv': 'new_v', 'new_v_w_out': 'new_v', 'new_v_ln1_g': 'new_v', 'new_v_ln1_b': 'new_v', 'new_v_ffn_up': 'new_v', 'new_v_ffn_conv_w': 'new_v', 'new_v_ffn_conv_b': 'new_v', 'new_v_ffn_down': 'new_v', 'new_v_ln2_g': 'new_v', 'new_v_ln2_b': 'new_v'}


def _forward(args):
    return _fwd_reference(*[args[k] for k in FWD_PARAMS])


def _output_shape():
    out = _jax.eval_shape(lambda: _forward(_fwd_setup_inputs(0)))
    return out.shape, out.dtype

N_MICROBATCH = 1
ADAM_LR = 0.001
ADAM_B1 = 0.9
ADAM_B2 = 0.999
ADAM_EPS = 1e-08
ADAM_WD = 0.01
ADAM_STEP = 10
PER_EXAMPLE_BATCH_AXIS = {'x': 0, 'c': 0, 'ctx': 0, 'loss_target': 0}
SHARED_INPUTS = []
_WEIGHT_DTYPES = {'c_ctx': _jnp.float32, 'ada_w': _jnp.float32, 'ada_b': _jnp.float32, 'w_in': _jnp.float32, 'gla_gate_w': _jnp.float32, 'gla_gate_b': _jnp.float32, 'gla_norm_g': _jnp.float32, 'ret_decay': _jnp.float32, 'mla_q_norm_g': _jnp.float32, 'mla_kv_norm_g': _jnp.float32, 'mla_w_uq': _jnp.float32, 'mla_w_uk': _jnp.float32, 'mla_w_uv': _jnp.float32, 'w_out': _jnp.float32, 'ln1_g': _jnp.float32, 'ln1_b': _jnp.float32, 'ffn_up': _jnp.float32, 'ffn_conv_w': _jnp.float32, 'ffn_conv_b': _jnp.float32, 'ffn_down': _jnp.float32, 'ln2_g': _jnp.float32, 'ln2_b': _jnp.float32}
MOMENT_SCALE = {'c_ctx': 8.472362e-03, 'ada_w': 1.883217e-02, 'ada_b': 3.016223e-02, 'w_in': 1.446092e-02, 'gla_gate_w': 2.293683e-03, 'gla_gate_b': 6.059715e-03, 'gla_norm_g': 2.739126e-02, 'ret_decay': 3.432027e-02, 'mla_q_norm_g': 2.986757e-03, 'mla_kv_norm_g': 1.465217e-02, 'mla_w_uq': 1.711477e-03, 'mla_w_uk': 1.855676e-03, 'mla_w_uv': 8.316282e-03, 'w_out': 2.470911e-02, 'ln1_g': 2.018725e+00, 'ln1_b': 7.591032e-01, 'ffn_up': 8.925437e-03, 'ffn_conv_w': 8.832568e-03, 'ffn_conv_b': 8.624091e-03, 'ffn_down': 3.480278e-02, 'ln2_g': 3.214565e+01, 'ln2_b': 1.497402e+00}


def _to_microbatches(a, axis):
    t = _jnp.moveaxis(a, axis, 0)
    t = t.reshape((N_MICROBATCH, t.shape[0] // N_MICROBATCH) + t.shape[1:])
    return _jnp.moveaxis(t, 1, axis + 1)


def setup_inputs(seed: int = 0) -> dict:
    inp = _fwd_setup_inputs(seed)
    key = _jax.random.fold_in(_jax.random.key(seed), 7919)
    shape, _ = _output_shape()
    out = dict(inp)
    out["loss_target"] = _jax.random.normal(_jax.random.fold_in(key, 0), shape, _jnp.float32)
    for i, name in enumerate(TWIN_WEIGHTS):
        w = inp[name].astype(_jnp.float32)
        if MOMENT_SCALE is None:
            s = _jnp.sqrt(_jnp.mean(_jnp.square(w)) + 1e-30)
        else:
            s = MOMENT_SCALE[name]
        km, kv = _jax.random.split(_jax.random.fold_in(key, i + 1))
        out[name] = w
        out["m_" + name] = s * _jax.random.normal(km, w.shape, _jnp.float32)
        out["v_" + name] = (s * s) * _jax.random.uniform(kv, w.shape, _jnp.float32, 0.5, 1.5)
    if N_MICROBATCH > 1:
        for name, axis in PER_EXAMPLE_BATCH_AXIS.items():
            out[name] = _to_microbatches(out[name], axis)
    return {'x': out['x'], 'c': out['c'], 'ctx': out['ctx'], 'c_ctx': out['c_ctx'], 'ada_w': out['ada_w'], 'ada_b': out['ada_b'], 'w_in': out['w_in'], 'gla_gate_w': out['gla_gate_w'], 'gla_gate_b': out['gla_gate_b'], 'gla_norm_g': out['gla_norm_g'], 'ret_decay': out['ret_decay'], 'mla_q_norm_g': out['mla_q_norm_g'], 'mla_kv_norm_g': out['mla_kv_norm_g'], 'mla_w_uq': out['mla_w_uq'], 'mla_w_uk': out['mla_w_uk'], 'mla_w_uv': out['mla_w_uv'], 'w_out': out['w_out'], 'ln1_g': out['ln1_g'], 'ln1_b': out['ln1_b'], 'ffn_up': out['ffn_up'], 'ffn_conv_w': out['ffn_conv_w'], 'ffn_conv_b': out['ffn_conv_b'], 'ffn_down': out['ffn_down'], 'ln2_g': out['ln2_g'], 'ln2_b': out['ln2_b'], 'loss_target': out['loss_target'], 'm_c_ctx': out['m_c_ctx'], 'm_ada_w': out['m_ada_w'], 'm_ada_b': out['m_ada_b'], 'm_w_in': out['m_w_in'], 'm_gla_gate_w': out['m_gla_gate_w'], 'm_gla_gate_b': out['m_gla_gate_b'], 'm_gla_norm_g': out['m_gla_norm_g'], 'm_ret_decay': out['m_ret_decay'], 'm_mla_q_norm_g': out['m_mla_q_norm_g'], 'm_mla_kv_norm_g': out['m_mla_kv_norm_g'], 'm_mla_w_uq': out['m_mla_w_uq'], 'm_mla_w_uk': out['m_mla_w_uk'], 'm_mla_w_uv': out['m_mla_w_uv'], 'm_w_out': out['m_w_out'], 'm_ln1_g': out['m_ln1_g'], 'm_ln1_b': out['m_ln1_b'], 'm_ffn_up': out['m_ffn_up'], 'm_ffn_conv_w': out['m_ffn_conv_w'], 'm_ffn_conv_b': out['m_ffn_conv_b'], 'm_ffn_down': out['m_ffn_down'], 'm_ln2_g': out['m_ln2_g'], 'm_ln2_b': out['m_ln2_b'], 'v_c_ctx': out['v_c_ctx'], 'v_ada_w': out['v_ada_w'], 'v_ada_b': out['v_ada_b'], 'v_w_in': out['v_w_in'], 'v_gla_gate_w': out['v_gla_gate_w'], 'v_gla_gate_b': out['v_gla_gate_b'], 'v_gla_norm_g': out['v_gla_norm_g'], 'v_ret_decay': out['v_ret_decay'], 'v_mla_q_norm_g': out['v_mla_q_norm_g'], 'v_mla_kv_norm_g': out['v_mla_kv_norm_g'], 'v_mla_w_uq': out['v_mla_w_uq'], 'v_mla_w_uk': out['v_mla_w_uk'], 'v_mla_w_uv': out['v_mla_w_uv'], 'v_w_out': out['v_w_out'], 'v_ln1_g': out['v_ln1_g'], 'v_ln1_b': out['v_ln1_b'], 'v_ffn_up': out['v_ffn_up'], 'v_ffn_conv_w': out['v_ffn_conv_w'], 'v_ffn_conv_b': out['v_ffn_conv_b'], 'v_ffn_down': out['v_ffn_down'], 'v_ln2_g': out['v_ln2_g'], 'v_ln2_b': out['v_ln2_b']}


def _loss(weights, diff, rest, loss_target):
    with _jax.named_scope("forward"):
        args = {**rest, TWIN_DIFF_INPUT: diff, **{k: w.astype(_WEIGHT_DTYPES[k]) for k, w in weights.items()}}
        y = _forward(args)
    with _jax.named_scope("loss_head"):
        err = _jnp.square(y.astype(_jnp.float32) - loss_target)
        return 0.5 * _jnp.sum(_jnp.mean(err, axis=-1)) if err.ndim else 0.5 * err


def _adamw(w, g, m, v):
    m = ADAM_B1 * m + (1.0 - ADAM_B1) * g
    v = ADAM_B2 * v + (1.0 - ADAM_B2) * _jnp.square(g)
    m_hat = m / (1.0 - ADAM_B1 ** ADAM_STEP)
    v_hat = v / (1.0 - ADAM_B2 ** ADAM_STEP)
    delta = -ADAM_LR * (m_hat / (_jnp.sqrt(v_hat) + ADAM_EPS) + ADAM_WD * w)
    return delta, m, v


def reference(x, c, ctx, c_ctx, ada_w, ada_b, w_in, gla_gate_w, gla_gate_b, gla_norm_g, ret_decay, mla_q_norm_g, mla_kv_norm_g, mla_w_uq, mla_w_uk, mla_w_uv, w_out, ln1_g, ln1_b, ffn_up, ffn_conv_w, ffn_conv_b, ffn_down, ln2_g, ln2_b, loss_target, m_c_ctx, m_ada_w, m_ada_b, m_w_in, m_gla_gate_w, m_gla_gate_b, m_gla_norm_g, m_ret_decay, m_mla_q_norm_g, m_mla_kv_norm_g, m_mla_w_uq, m_mla_w_uk, m_mla_w_uv, m_w_out, m_ln1_g, m_ln1_b, m_ffn_up, m_ffn_conv_w, m_ffn_conv_b, m_ffn_down, m_ln2_g, m_ln2_b, v_c_ctx, v_ada_w, v_ada_b, v_w_in, v_gla_gate_w, v_gla_gate_b, v_gla_norm_g, v_ret_decay, v_mla_q_norm_g, v_mla_kv_norm_g, v_mla_w_uq, v_mla_w_uk, v_mla_w_uv, v_w_out, v_ln1_g, v_ln1_b, v_ffn_up, v_ffn_conv_w, v_ffn_conv_b, v_ffn_down, v_ln2_g, v_ln2_b):
    given = dict(x=x, c=c, ctx=ctx, c_ctx=c_ctx, ada_w=ada_w, ada_b=ada_b, w_in=w_in, gla_gate_w=gla_gate_w, gla_gate_b=gla_gate_b, gla_norm_g=gla_norm_g, ret_decay=ret_decay, mla_q_norm_g=mla_q_norm_g, mla_kv_norm_g=mla_kv_norm_g, mla_w_uq=mla_w_uq, mla_w_uk=mla_w_uk, mla_w_uv=mla_w_uv, w_out=w_out, ln1_g=ln1_g, ln1_b=ln1_b, ffn_up=ffn_up, ffn_conv_w=ffn_conv_w, ffn_conv_b=ffn_conv_b, ffn_down=ffn_down, ln2_g=ln2_g, ln2_b=ln2_b, loss_target=loss_target, m_c_ctx=m_c_ctx, m_ada_w=m_ada_w, m_ada_b=m_ada_b, m_w_in=m_w_in, m_gla_gate_w=m_gla_gate_w, m_gla_gate_b=m_gla_gate_b, m_gla_norm_g=m_gla_norm_g, m_ret_decay=m_ret_decay, m_mla_q_norm_g=m_mla_q_norm_g, m_mla_kv_norm_g=m_mla_kv_norm_g, m_mla_w_uq=m_mla_w_uq, m_mla_w_uk=m_mla_w_uk, m_mla_w_uv=m_mla_w_uv, m_w_out=m_w_out, m_ln1_g=m_ln1_g, m_ln1_b=m_ln1_b, m_ffn_up=m_ffn_up, m_ffn_conv_w=m_ffn_conv_w, m_ffn_conv_b=m_ffn_conv_b, m_ffn_down=m_ffn_down, m_ln2_g=m_ln2_g, m_ln2_b=m_ln2_b, v_c_ctx=v_c_ctx, v_ada_w=v_ada_w, v_ada_b=v_ada_b, v_w_in=v_w_in, v_gla_gate_w=v_gla_gate_w, v_gla_gate_b=v_gla_gate_b, v_gla_norm_g=v_gla_norm_g, v_ret_decay=v_ret_decay, v_mla_q_norm_g=v_mla_q_norm_g, v_mla_kv_norm_g=v_mla_kv_norm_g, v_mla_w_uq=v_mla_w_uq, v_mla_w_uk=v_mla_w_uk, v_mla_w_uv=v_mla_w_uv, v_w_out=v_w_out, v_ln1_g=v_ln1_g, v_ln1_b=v_ln1_b, v_ffn_up=v_ffn_up, v_ffn_conv_w=v_ffn_conv_w, v_ffn_conv_b=v_ffn_conv_b, v_ffn_down=v_ffn_down, v_ln2_g=v_ln2_g, v_ln2_b=v_ln2_b)
    weights = {n: given[n] for n in TWIN_WEIGHTS}
    shared = {n: given[n] for n in SHARED_INPUTS}
    per_example = {n: given[n] for n in ['x', 'c', 'ctx']}
    grad_fn = _jax.value_and_grad(_loss, argnums=(0, 1))

    def one_microbatch(ex, loss_target):
        ex = dict(ex)
        diff = ex.pop(TWIN_DIFF_INPUT)
        return grad_fn(weights, diff, {**shared, **ex}, loss_target)

    if N_MICROBATCH == 1:
        loss, (grad_w, grad_x) = one_microbatch(per_example, given["loss_target"])
    else:
        def body(carry, xs):
            loss_sum, grad_sum = carry
            l_k, (gw_k, gx_k) = one_microbatch(xs[0], xs[1])
            with _jax.named_scope("update"):
                return (loss_sum + l_k, _jax.tree.map(_jnp.add, grad_sum, gw_k)), gx_k

        init = (_jnp.zeros((), _jnp.float32), _jax.tree.map(_jnp.zeros_like, weights))
        (loss, grad_w), grad_x = _jax.lax.scan(body, init, (per_example, given["loss_target"]))
    with _jax.named_scope("update"):
        delta_w, new_m, new_v = {}, {}, {}
        for n in TWIN_WEIGHTS:
            delta_w[n], new_m[n], new_v[n] = _adamw(weights[n], grad_w[n], given["m_" + n], given["v_" + n])
    return (loss, grad_x, *[grad_w[n] for n in TWIN_WEIGHTS], *[delta_w[n] for n in TWIN_WEIGHTS],
            *[new_m[n] for n in TWIN_WEIGHTS], *[new_v[n] for n in TWIN_WEIGHTS])
```

```python
import functools

import jax
import jax.numpy as jnp
from jax import lax
from jax.experimental import pallas as pl
from jax.experimental.pallas import tpu as pltpu

F32 = jnp.float32
BF = jnp.bfloat16

D_MODEL = 1024
DEPTH = 4
GRID_W = 64
GLA_DK = 32
GLA_TAU = 16.0
RET_DK = 32
MLA_HEADS = 8
MLA_D_NOPE = 64
MLA_D_ROPE = 32
MLA_SCALE = (MLA_D_NOPE + MLA_D_ROPE) ** -0.5
D_FF = 2816
ROPE_BASE = 10000.0
EPS = 1e-6
ALPHA = (2 * DEPTH) ** 0.25
ADAM_LR, ADAM_B1, ADAM_B2, ADAM_EPS, ADAM_WD, ADAM_STEP = 0.001, 0.9, 0.999, 1e-08, 0.01, 10

ROW_TILE = 256
CHUNK = 64
GATE_ROWS = 64
LANES = 128

C_GQ, C_GK, C_GV, C_GG, C_RQ, C_RK, C_RV, C_RG, C_CQ, C_CKV, C_LK = 0, 128, 256, 512, 768, 896, 1024, 1280, 1536, 1792, 1920
D_INP = 2048
D_IN = 1984


def _dg(a, b, ca, cb):
    return lax.dot_general(a.astype(BF), b.astype(BF), (((ca,), (cb,)), ((), ())), preferred_element_type=F32)


@jax.custom_vjp
def mm_nn(a, b):
    return _dg(a, b, 1, 0)


@jax.custom_vjp
def mm_nt(a, b):
    return _dg(a, b, 1, 1)


@jax.custom_vjp
def mm_tn(a, b):
    return _dg(a, b, 0, 0)


mm_nn.defvjp(lambda a, b: (_dg(a, b, 1, 0), (a, b)),
             lambda r, g: (mm_nt(g, r[1]).astype(r[0].dtype), mm_tn(r[0], g).astype(r[1].dtype)))
mm_nt.defvjp(lambda a, b: (_dg(a, b, 1, 1), (a, b)),
             lambda r, g: (mm_nn(g, r[1]).astype(r[0].dtype), mm_tn(g, r[0]).astype(r[1].dtype)))
mm_tn.defvjp(lambda a, b: (_dg(a, b, 0, 0), (a, b)),
             lambda r, g: (mm_nt(r[1], g).astype(r[0].dtype), mm_nn(r[0], g).astype(r[1].dtype)))


def _split3(x):
    h = x.astype(BF)
    r = x - h.astype(F32)
    m = r.astype(BF)
    lo = (r - m.astype(F32)).astype(BF)
    return h, m, lo


def _exact(x, mat, left):
    h, m, lo = _split3(x)
    if left:
        d = lambda t: lax.dot_general(mat, t, (((1,), (0,)), ((), ())), preferred_element_type=F32)
    else:
        d = lambda t: lax.dot_general(t, mat, (((1,), (0,)), ((), ())), preferred_element_type=F32)
    return (d(lo) + d(m)) + d(h)


def _iota(shape, axis):
    return lax.broadcasted_iota(jnp.int32, shape, axis)


def _tri(n, upper):
    r, c = _iota((n, n), 0), _iota((n, n), 1)
    return jnp.where((c >= r) if upper else (r >= c), 1.0, 0.0).astype(BF)


@functools.partial(jax.custom_vjp, nondiff_argnums=(1,))
def cumsum_rows(x, upper):
    return _exact(x, _tri(x.shape[0], upper), True)


cumsum_rows.defvjp(lambda x, upper: (cumsum_rows(x, upper), None),
                   lambda upper, r, g: (cumsum_rows(g, not upper),))


def _seg(n, w):
    shift = w.bit_length() - 1
    r, c = _iota((n, n), 0), _iota((n, n), 1)
    return jnp.where(lax.shift_right_logical(r, shift) == lax.shift_right_logical(c, shift), 1.0, 0.0).astype(BF)


@functools.partial(jax.custom_vjp, nondiff_argnums=(1,))
def seg_sum(x, w):
    return _exact(x, _seg(x.shape[1], w), False)


seg_sum.defvjp(lambda x, w: (seg_sum(x, w), None), lambda w, r, g: (seg_sum(g, w),))


def _place_mat(transpose):
    shape = (8 * LANES, LANES) if transpose else (LANES, 8 * LANES)
    r, c = _iota(shape, 0), _iota(shape, 1)
    src, dst = (c, r) if transpose else (r, c)
    dl = jnp.bitwise_and(dst, LANES - 1)
    ok = (dl >= 64) & (dl < 96) & (src == dl - 32)
    return jnp.where(ok, 1.0, 0.0).astype(BF)


@jax.custom_vjp
def place_kr(x):
    return _exact(x, _place_mat(False), False)


place_kr.defvjp(lambda x: (place_kr(x), None), lambda r, g: (_exact(g, _place_mat(True), False),))


@functools.partial(jax.custom_vjp, nondiff_argnums=(1,))
def lane_roll(x, s):
    return pltpu.roll(x, s, 1)


lane_roll.defvjp(lambda x, s: (pltpu.roll(x, s, 1), None),
                 lambda s, r, g: (pltpu.roll(g, (g.shape[1] - s) % g.shape[1], 1),))


def rope(x, tab, d):
    cos, sa, sb = tab
    return x * cos + lane_roll(x, LANES - d) * sa + lane_roll(x, d) * sb


def silu(x):
    return x * jax.nn.sigmoid(x)


def log_sigmoid(z):
    return jnp.minimum(z, 0.0) - jnp.log(1.0 + jnp.exp(-jnp.abs(z)))


def layer_norm(x, g, b):
    mu = jnp.mean(x, axis=-1, keepdims=True)
    xc = x - mu
    var = jnp.mean(xc * xc, axis=-1, keepdims=True)
    return xc * lax.rsqrt(var + EPS) * g + b


def matmul(a, b, mode, out_dtype, tm, tn, tk, name):
    if mode == "nn":
        (m, k), (k2, n) = a.shape, b.shape
        a_spec = pl.BlockSpec((tm, tk), lambda i, j, kk: (i, kk))
        b_spec = pl.BlockSpec((tk, tn), lambda i, j, kk: (kk, j))
        ca, cb = 1, 0
    elif mode == "nt":
        (m, k), (n, k2) = a.shape, b.shape
        a_spec = pl.BlockSpec((tm, tk), lambda i, j, kk: (i, kk))
        b_spec = pl.BlockSpec((tn, tk), lambda i, j, kk: (j, kk))
        ca, cb = 1, 1
    else:
        (k, m), (k2, n) = a.shape, b.shape
        a_spec = pl.BlockSpec((tk, tm), lambda i, j, kk: (kk, i))
        b_spec = pl.BlockSpec((tk, tn), lambda i, j, kk: (kk, j))
        ca, cb = 0, 0
    assert k == k2 and m % tm == 0 and n % tn == 0 and k % tk == 0, (name, a.shape, b.shape, tm, tn, tk)
    nk = k // tk

    def body(a_ref, b_ref, o_ref, *acc):
        part = _dg(a_ref[...], b_ref[...], ca, cb)
        if nk == 1:
            o_ref[...] = part.astype(o_ref.dtype)
            return
        acc_ref, = acc
        kk = pl.program_id(2)

        @pl.when(kk == 0)
        def _():
            acc_ref[...] = part

        @pl.when(kk > 0)
        def _():
            acc_ref[...] += part

        @pl.when(kk == nk - 1)
        def _():
            o_ref[...] = acc_ref[...].astype(o_ref.dtype)

    return pl.pallas_call(
        body, name=name, grid=(m // tm, n // tn, nk),
        in_specs=[a_spec, b_spec], out_specs=pl.BlockSpec((tm, tn), lambda i, j, kk: (i, j)),
        out_shape=jax.ShapeDtypeStruct((m, n), out_dtype),
        scratch_shapes=[] if nk == 1 else [pltpu.VMEM((tm, tn), F32)],
    )(a, b)


def _stage_specs(rows, tps, consts, ws):
    specs, args = [], []
    for arr, width, cb in rows:
        specs.append(pl.BlockSpec((ROW_TILE, width), functools.partial(lambda i, cb: (i, cb), cb=cb)))
        args.append(arr)
    for arr in tps:
        specs.append(pl.BlockSpec((1, 1, arr.shape[2]), lambda i: (i, 0, 0)))
        args.append(arr)
    for arr, period in consts:
        specs.append(pl.BlockSpec((ROW_TILE, arr.shape[1]), functools.partial(lambda i, p: (i % p, 0), p=period)))
        args.append(arr)
    for arr in ws:
        specs.append(pl.BlockSpec(arr.shape, functools.partial(lambda i, nd: (0,) * nd, nd=arr.ndim)))
        args.append(arr)
    return specs, args


def _stage_load(refs, n_rows, n_tps, n_consts, n_ws):
    it = iter(refs)
    rows = [next(it)[...].astype(F32) for _ in range(n_rows)]
    tps = [next(it)[0].astype(F32) for _ in range(n_tps)]
    consts = [next(it)[...].astype(F32) for _ in range(n_consts)]
    ws = [next(it)[...].astype(F32) for _ in range(n_ws)]
    return rows, tps, consts, ws


def stage_fwd(fn, rows, tps, consts, ws, outs, name):
    n_tiles = rows[0][0].shape[0] // ROW_TILE
    specs, args = _stage_specs(rows, tps, consts, ws)
    counts = (len(rows), len(tps), len(consts), len(ws))

    def body(*refs):
        r, t, c, w = _stage_load(refs[:sum(counts)], *counts)
        res = fn(r, t, c, w)
        for o_ref, o in zip(refs[sum(counts):], res):
            o_ref[...] = o.astype(o_ref.dtype)

    res = pl.pallas_call(
        body, name=name, grid=(n_tiles,), in_specs=specs,
        out_specs=[pl.BlockSpec((ROW_TILE, wd), lambda i: (i, 0)) for wd, _ in outs],
        out_shape=[jax.ShapeDtypeStruct((n_tiles * ROW_TILE, wd), dt) for wd, dt in outs],
    )(*args)
    return list(res)


def stage_bwd(fn, rows, tps, consts, ws, cts, row_grads, name):
    n_tiles = rows[0][0].shape[0] // ROW_TILE
    specs, args = _stage_specs(rows, tps, consts, ws)
    counts = (len(rows), len(tps), len(consts), len(ws))
    n_in = sum(counts)
    for ct in cts:
        specs.append(pl.BlockSpec((ROW_TILE, ct.shape[1]), lambda i: (i, 0)))
        args.append(ct)
    want = [k for k, dt in enumerate(row_grads) if dt is not None]
    out_specs = [pl.BlockSpec((ROW_TILE, rows[k][1]), lambda i: (i, 0)) for k in want]
    out_shape = [jax.ShapeDtypeStruct((n_tiles * ROW_TILE, rows[k][1]), row_grads[k]) for k in want]
    out_specs += [pl.BlockSpec((1, 1, a.shape[2]), lambda i: (i, 0, 0)) for a in tps]
    out_shape += [jax.ShapeDtypeStruct((n_tiles, 1, a.shape[2]), F32) for a in tps]
    out_specs += [pl.BlockSpec(a.shape, functools.partial(lambda i, nd: (0,) * nd, nd=a.ndim)) for a in ws]
    out_shape += [jax.ShapeDtypeStruct(a.shape, F32) for a in ws]

    def body(*refs):
        r, t, c, w = _stage_load(refs[:n_in], *counts)
        g = [ref[...].astype(F32) for ref in refs[n_in:n_in + len(cts)]]
        _, vjp = jax.vjp(lambda r_, t_, w_: fn(r_, t_, c, w_), r, t, w)
        dr, dt, dw = vjp(g)
        o = iter(refs[n_in + len(cts):])
        for k in want:
            ref = next(o)
            ref[...] = dr[k].astype(ref.dtype)
        for v in dt:
            next(o)[0] = v
        first = pl.program_id(0) == 0
        for v in dw:
            ref = next(o)

            @pl.when(first)
            def _():
                ref[...] = v

            @pl.when(jnp.logical_not(first))
            def _():
                ref[...] += v

    res = pl.pallas_call(body, name=name, grid=(n_tiles,), in_specs=specs, out_specs=out_specs, out_shape=out_shape)(*args)
    res = list(res)
    drows = [None] * len(rows)
    for k in want:
        drows[k] = res.pop(0)
    dtps = [res.pop(0) for _ in tps]
    dws = [res.pop(0) for _ in ws]
    return drows, dtps, dws


def fn_modulate(rows, tps, consts, ws):
    (x,), (sc, sh) = rows, tps
    return [x * (1.0 + sc) + sh]


def fn_post(rows, tps, consts, ws):
    (x, a), (g,), (lng, lnb) = rows, tps, ws
    return [layer_norm(ALPHA * x + g * a, lng, lnb)]


def fn_mix(rows, tps, consts, ws):
    ogf, ogb, orf, orb, pg, pr, mo = rows
    gng, = ws
    og = ogf + ogb
    out_g = og * lax.rsqrt(seg_sum(og * og, 64) * (1.0 / 64) + EPS) * gng * silu(pg)
    o = orf + orb
    oc = o - seg_sum(o, 64) * (1.0 / 64)
    out_r = oc * lax.rsqrt(seg_sum(oc * oc, 64) * (1.0 / 64) + EPS) * silu(pr)
    return [jnp.concatenate([out_g, out_r, mo], axis=-1)]


def fn_mla_prep(rows, tps, consts, ws):
    pq, pkv, plk = rows
    gq, gkv, wuq, wuk, wuv = ws
    qtab, ktab = consts[0:3], consts[3:6]
    cq = pq * lax.rsqrt(jnp.mean(pq * pq, axis=-1, keepdims=True) + EPS) * gq
    qp = mm_nn(cq, wuq)
    q = jnp.concatenate([rope(qp[:, h * LANES:(h + 1) * LANES], qtab, 8) for h in range(MLA_HEADS)], axis=-1)
    ckv = pkv * lax.rsqrt(jnp.mean(pkv * pkv, axis=-1, keepdims=True) + EPS) * gkv
    k = mm_nn(ckv, wuk) + place_kr(rope(plk, ktab, 8))
    v = mm_nn(ckv, wuv)
    return [q, k, v]


def fn_assemble(rows, tps, consts, ws):
    gq, gk, gv, gg, rq, rk, rv, rg, cq, ckv, lk1, lk2 = rows
    return [jnp.concatenate([gq, gk, gv, gg, rq, rk, rv, rg, cq, ckv, lk1 + lk2], axis=-1)]


def _head_masks():
    hm = (lax.shift_right_logical(_iota((4, 1, LANES), 2), 5) == _iota((4, 1, LANES), 0)).astype(F32)
    vm = (lax.shift_right_logical(_iota((4, 1, 256), 2), 6) == _iota((4, 1, 256), 0)).astype(F32)
    bd = (lax.shift_right_logical(_iota((256, LANES), 0), 6) == lax.shift_right_logical(_iota((256, LANES), 1), 5)).astype(F32)
    return hm, vm, bd


def chunk_step(s, q, k, v, la, upper):
    hm, vm, bd = _head_masks()
    t, u = _iota((4 * CHUNK, CHUNK), 0), _iota((4 * CHUNK, CHUNK), 1)
    t = jnp.bitwise_and(t, CHUNK - 1)
    causal = (u >= t) if upper else (t >= u)
    b = cumsum_rows(la, upper)
    bend = jnp.sum(la, axis=0, keepdims=True)
    half = 0.5 * bend
    qd = q * jnp.exp(b - half)
    kd = k * jnp.exp(half - b)
    qe = (qd[None] * hm).reshape(4 * CHUNK, LANES)
    att = jnp.where(causal, mm_nt(qe, kd), 0.0)
    o_intra = (mm_nn(att, v).reshape(4, CHUNK, 256) * vm).sum(0)
    o = mm_nt(q * jnp.exp(b), s) + o_intra
    s_new = (s * jnp.exp(bend) + mm_tn(v, k * jnp.exp(bend - b))) * bd
    return o, s_new


def scan_step(sg, sr, q, k, v, lrk, rq, rk, rv, gw, gb, rdec, tab, upper):
    la_g = log_sigmoid(mm_nn(lrk, gw) + gb) * (1.0 / GLA_TAU)
    og, sg2 = chunk_step(sg, q * GLA_DK ** -0.5, k, v, la_g, upper)
    la_r = jnp.broadcast_to(log_sigmoid(rdec), (CHUNK, LANES))
    orr, sr2 = chunk_step(sr, rope(rq, tab, 16), rope(rk * RET_DK ** -0.5, tab, 16), rv, la_r, upper)
    return og, orr, sg2, sr2


def _chunk_of(n, ncc, nch, reverse):
    if not reverse:
        return n
    return jnp.where(n < ncc, ncc - 1 - n, nch - 1 + ncc - n)


def _scan_in_specs(p, tabs, gw, gb, rdec, nch, cidx):
    def blk(width, cb):
        return pl.BlockSpec((CHUNK, width), lambda bb, m: (bb * nch + cidx(m), cb))

    specs = [blk(128, C_GQ // 128), blk(128, C_GK // 128), blk(256, C_GV // 256), blk(128, C_LK // 128),
             blk(128, C_RQ // 128), blk(128, C_RK // 128), blk(256, C_RV // 256)]
    args = [p] * 7
    for t in tabs:
        specs.append(pl.BlockSpec((CHUNK, LANES), lambda bb, m: (cidx(m), 0)))
        args.append(t)
    for w in (gw, gb, rdec):
        specs.append(pl.BlockSpec(w.shape, lambda bb, m: (0, 0)))
        args.append(w)
    return specs, args


def scan_fwd(p, tabs, gw, gb, rdec, nb, ncc, nch, reverse, name):
    cidx = lambda n: _chunk_of(n, ncc, nch, reverse)
    specs, args = _scan_in_specs(p, tabs, gw, gb, rdec, nch, cidx)

    def body(q, k, v, lrk, rq, rk, rv, tc, ta, tb, gw_r, gb_r, rd_r, og_r, or_r, sgo_r, sro_r, sg, sr):
        @pl.when(pl.program_id(1) == 0)
        def _():
            sg[...] = jnp.zeros_like(sg)
            sr[...] = jnp.zeros_like(sr)

        sgo_r[0] = sg[...]
        sro_r[0] = sr[...]
        ld = lambda r: r[...].astype(F32)
        og, orr, sg2, sr2 = scan_step(sg[...], sr[...], ld(q), ld(k), ld(v), ld(lrk), ld(rq), ld(rk), ld(rv),
                                      ld(gw_r), ld(gb_r), ld(rd_r), (ld(tc), ld(ta), ld(tb)), reverse)
        og_r[...] = og
        or_r[...] = orr
        sg[...] = sg2
        sr[...] = sr2

    t = p.shape[0]
    row_out = pl.BlockSpec((CHUNK, 256), lambda bb, n: (bb * nch + cidx(n), 0))
    st_out = pl.BlockSpec((1, 256, LANES), lambda bb, n: (bb * nch + n, 0, 0))
    return pl.pallas_call(
        body, name=name, grid=(nb, nch), in_specs=specs, out_specs=[row_out, row_out, st_out, st_out],
        out_shape=[jax.ShapeDtypeStruct((t, 256), F32)] * 2 + [jax.ShapeDtypeStruct((nb * nch, 256, LANES), F32)] * 2,
        scratch_shapes=[pltpu.VMEM((256, LANES), F32)] * 2,
    )(*args)


def scan_bwd(p, tabs, gw, gb, rdec, sg_in, sr_in, dog, dor, prev, nb, ncc, nch, reverse, name):
    step = lambda m: nch - 1 - m
    cidx = lambda m: _chunk_of(step(m), ncc, nch, reverse)
    specs, args = _scan_in_specs(p, tabs, gw, gb, rdec, nch, cidx)
    st_spec = pl.BlockSpec((1, 256, LANES), lambda bb, m: (bb * nch + step(m), 0, 0))
    specs += [st_spec, st_spec]
    args += [sg_in, sr_in]
    row = lambda width: pl.BlockSpec((CHUNK, width), lambda bb, m: (bb * nch + cidx(m), 0))
    specs += [row(256), row(256)]
    args += [dog, dor]
    widths = (128, 128, 256, 128, 128, 128, 256)
    if prev is not None:
        specs += [row(wd) for wd in widths]
        args += list(prev)
    n_prev = 0 if prev is None else 7
    t = p.shape[0]

    def body(*refs):
        (q, k, v, lrk, rq, rk, rv, tc, ta, tb, gw_r, gb_r, rd_r, sgi, sri, dog_r, dor_r), rest = refs[:17], refs[17:]
        prev_r, rest = rest[:n_prev], rest[n_prev:]
        outs, (dgw_r, dgb_r, drd_r, dsg, dsr) = rest[:7], rest[7:]
        first = (pl.program_id(0) == 0) & (pl.program_id(1) == 0)

        @pl.when(pl.program_id(1) == 0)
        def _():
            dsg[...] = jnp.zeros_like(dsg)
            dsr[...] = jnp.zeros_like(dsr)

        ld = lambda r: r[...].astype(F32)
        tab = (ld(tc), ld(ta), ld(tb))
        prim = (sgi[0], sri[0], ld(q), ld(k), ld(v), ld(lrk), ld(rq), ld(rk), ld(rv), ld(gw_r), ld(gb_r), ld(rd_r))
        _, vjp = jax.vjp(lambda *a: scan_step(*a, tab, reverse), *prim)
        g = vjp((ld(dog_r), ld(dor_r), dsg[...], dsr[...]))
        dsg[...] = g[0]
        dsr[...] = g[1]
        for j in range(7):
            val = g[2 + j]
            if n_prev:
                val = val + prev_r[j][...]
            outs[j][...] = val
        for ref, val in ((dgw_r, g[9]), (dgb_r, g[10]), (drd_r, g[11])):
            @pl.when(first)
            def _():
                ref[...] = val

            @pl.when(jnp.logical_not(first))
            def _():
                ref[...] += val

    wspec = lambda w: pl.BlockSpec(w.shape, lambda bb, m: (0, 0))
    res = pl.pallas_call(
        body, name=name, grid=(nb, nch), in_specs=specs,
        out_specs=[row(wd) for wd in widths] + [wspec(gw), wspec(gb), wspec(rdec)],
        out_shape=[jax.ShapeDtypeStruct((t, wd), F32) for wd in widths]
        + [jax.ShapeDtypeStruct(w.shape, F32) for w in (gw, gb, rdec)],
        scratch_shapes=[pltpu.VMEM((256, LANES), F32)] * 2,
    )(*args)
    return tuple(res[:7]), res[7], res[8], res[9]


def attend(q, k, v):
    s = mm_nt(q, k) * MLA_SCALE
    p = jnp.exp(s - jnp.max(s, axis=-1, keepdims=True))
    return mm_nn(p / jnp.sum(p, axis=-1, keepdims=True), v)


def _attn_tiles(lc, lt):
    nct = lc // ROW_TILE
    return nct, (lt - lc) // ROW_TILE


def mla_fwd(q, k, v, nb, lc, lt, name):
    nct, nlt = _attn_tiles(lc, lt)

    def body(q_ref, k_ref, v_ref, o_ref):
        def tile(r0, nk):
            rows = pl.ds(r0, ROW_TILE)
            outs = []
            for j in range(2):
                outs.append(attend(q_ref[rows, j * LANES:(j + 1) * LANES], k_ref[0:nk, j * LANES:(j + 1) * LANES],
                                   v_ref[0:nk, j * 64:(j + 1) * 64]))
            o_ref[rows, :] = jnp.concatenate(outs, axis=-1)

        for i in range(nct):
            tile(i * ROW_TILE, lc)

        def lat(i, carry):
            tile(pl.multiple_of(lc + i * ROW_TILE, ROW_TILE), lt)
            return carry

        lax.fori_loop(0, nlt, lat, 0)

    return pl.pallas_call(
        body, name=name, grid=(nb, MLA_HEADS // 2),
        in_specs=[pl.BlockSpec((lt, 2 * LANES), lambda b, h: (b, h)), pl.BlockSpec((lt, 2 * LANES), lambda b, h: (b, h)),
                  pl.BlockSpec((lt, LANES), lambda b, h: (b, h))],
        out_specs=pl.BlockSpec((lt, LANES), lambda b, h: (b, h)),
        out_shape=jax.ShapeDtypeStruct((nb * lt, MLA_HEADS * 64), F32),
    )(q, k, v)


def mla_bwd(q, k, v, do, nb, lc, lt, name):
    nct, nlt = _attn_tiles(lc, lt)

    def body(q_ref, k_ref, v_ref, do_ref, dq_ref, dk_ref, dv_ref, dka, dva):
        dka[...] = jnp.zeros_like(dka)
        dva[...] = jnp.zeros_like(dva)

        def tile(r0, nk):
            rows = pl.ds(r0, ROW_TILE)
            dqs = []
            for j in range(2):
                prim = (q_ref[rows, j * LANES:(j + 1) * LANES].astype(F32), k_ref[0:nk, j * LANES:(j + 1) * LANES].astype(F32),
                        v_ref[0:nk, j * 64:(j + 1) * 64].astype(F32))
                _, vjp = jax.vjp(attend, *prim)
                dq, dk, dv = vjp(do_ref[rows, j * 64:(j + 1) * 64])
                dqs.append(dq)
                dka[j, 0:nk, :] += dk
                dva[j, 0:nk, :] += dv
            dq_ref[rows, :] = jnp.concatenate(dqs, axis=-1)

        for i in range(nct):
            tile(i * ROW_TILE, lc)

        def lat(i, carry):
            tile(pl.multiple_of(lc + i * ROW_TILE, ROW_TILE), lt)
            return carry

        lax.fori_loop(0, nlt, lat, 0)
        dk_ref[...] = jnp.concatenate([dka[0], dka[1]], axis=-1)
        dv_ref[...] = jnp.concatenate([dva[0], dva[1]], axis=-1)

    t = nb * lt
    return pl.pallas_call(
        body, name=name, grid=(nb, MLA_HEADS // 2),
        in_specs=[pl.BlockSpec((lt, 2 * LANES), lambda b, h: (b, h)), pl.BlockSpec((lt, 2 * LANES), lambda b, h: (b, h)),
                  pl.BlockSpec((lt, LANES), lambda b, h: (b, h)), pl.BlockSpec((lt, LANES), lambda b, h: (b, h))],
        out_specs=[pl.BlockSpec((lt, 2 * LANES), lambda b, h: (b, h)), pl.BlockSpec((lt, 2 * LANES), lambda b, h: (b, h)),
                   pl.BlockSpec((lt, LANES), lambda b, h: (b, h))],
        out_shape=[jax.ShapeDtypeStruct((t, MLA_HEADS * LANES), F32), jax.ShapeDtypeStruct((t, MLA_HEADS * LANES), F32),
                   jax.ShapeDtypeStruct((t, MLA_HEADS * 64), F32)],
        scratch_shapes=[pltpu.VMEM((2, lt, LANES), F32), pltpu.VMEM((2, lt, 64), F32)],
    )(q, k, v, do)


def _gate_specs(u, per_batch, lc):
    g8 = GATE_ROWS // 8
    n8 = u.shape[0] // 8
    width = u.shape[1]
    main = pl.BlockSpec((GATE_ROWS, width), lambda i: (i, 0))
    prev = pl.BlockSpec((8, width), lambda i: (jnp.maximum(i * g8 - 1, 0), 0))
    nxt = pl.BlockSpec((8, width), lambda i: (jnp.minimum((i + 1) * g8, n8 - 1), 0))
    return main, prev, nxt


def _seg_edges(per_batch, lc):
    j = pl.program_id(0) % (per_batch // GATE_ROWS)
    first = (j == 0) | (j == lc // GATE_ROWS)
    last = (j == lc // GATE_ROWS - 1) | (j == per_batch // GATE_ROWS - 1)
    return first, last


def _shifted(x, prev_ref, next_ref, first, last):
    rows = _iota(x.shape, 0)
    before = jnp.where(first, 0.0, prev_ref[7:8, :].astype(F32))
    after = jnp.where(last, 0.0, next_ref[0:1, :].astype(F32))
    xm = jnp.where(rows == 0, before, pltpu.roll(x, 1, 0))
    xp = jnp.where(rows == x.shape[0] - 1, after, pltpu.roll(x, x.shape[0] - 1, 0))
    return xm, xp


def gate_fwd(u, cw, cb, per_batch, lc, name):
    main, prev, nxt = _gate_specs(u, per_batch, lc)
    f = u.shape[1] // 2

    def body(u_ref, p_ref, n_ref, w_ref, b_ref, act_ref):
        first, last = _seg_edges(per_batch, lc)
        x = u_ref[...]
        xm, xp = _shifted(x, p_ref, n_ref, first, last)
        c = w_ref[0:1, :] * xm + w_ref[1:2, :] * x + w_ref[2:3, :] * xp + b_ref[...]
        act_ref[...] = (silu(c[:, :f]) * c[:, f:]).astype(act_ref.dtype)

    return pl.pallas_call(
        body, name=name, grid=(u.shape[0] // GATE_ROWS,),
        in_specs=[main, prev, nxt, pl.BlockSpec(cw.shape, lambda i: (0, 0)), pl.BlockSpec(cb.shape, lambda i: (0, 0))],
        out_specs=pl.BlockSpec((GATE_ROWS, f), lambda i: (i, 0)),
        out_shape=jax.ShapeDtypeStruct((u.shape[0], f), BF),
    )(u, u, u, cw, cb)


def gate_bwd(u, cw, cb, dact, per_batch, lc, name):
    main, prev, nxt = _gate_specs(u, per_batch, lc)
    f = u.shape[1] // 2

    def body(u_ref, p_ref, n_ref, w_ref, b_ref, da_ref, dc_ref, dw_ref):
        first, last = _seg_edges(per_batch, lc)
        x = u_ref[...]
        xm, xp = _shifted(x, p_ref, n_ref, first, last)
        c = w_ref[0:1, :] * xm + w_ref[1:2, :] * x + w_ref[2:3, :] * xp + b_ref[...]
        a, g = c[:, :f], c[:, f:]
        sg = jax.nn.sigmoid(a)
        da = da_ref[...]
        dc = jnp.concatenate([da * g * (sg * (1.0 + a * (1.0 - sg))), da * (a * sg)], axis=-1)
        dc_ref[...] = dc
        part = jnp.concatenate([jnp.sum(xm * dc, axis=0, keepdims=True), jnp.sum(x * dc, axis=0, keepdims=True),
                                jnp.sum(xp * dc, axis=0, keepdims=True), jnp.sum(dc, axis=0, keepdims=True),
                                jnp.zeros((4, 2 * f), F32)], axis=0)

        @pl.when(pl.program_id(0) == 0)
        def _():
            dw_ref[...] = part

        @pl.when(pl.program_id(0) > 0)
        def _():
            dw_ref[...] += part

    return pl.pallas_call(
        body, name=name, grid=(u.shape[0] // GATE_ROWS,),
        in_specs=[main, prev, nxt, pl.BlockSpec(cw.shape, lambda i: (0, 0)), pl.BlockSpec(cb.shape, lambda i: (0, 0)),
                  pl.BlockSpec((GATE_ROWS, f), lambda i: (i, 0))],
        out_specs=[main, pl.BlockSpec((8, 2 * f), lambda i: (0, 0))],
        out_shape=[jax.ShapeDtypeStruct(u.shape, F32), jax.ShapeDtypeStruct((8, 2 * f), F32)],
    )(u, u, u, cw, cb, dact)


def conv_transpose(dc, cw, per_batch, lc, name):
    main, prev, nxt = _gate_specs(dc, per_batch, lc)

    def body(d_ref, p_ref, n_ref, w_ref, du_ref):
        first, last = _seg_edges(per_batch, lc)
        x = d_ref[...]
        xm, xp = _shifted(x, p_ref, n_ref, first, last)
        du_ref[...] = (w_ref[0:1, :] * xp + w_ref[1:2, :] * x + w_ref[2:3, :] * xm).astype(du_ref.dtype)

    return pl.pallas_call(
        body, name=name, grid=(dc.shape[0] // GATE_ROWS,),
        in_specs=[main, prev, nxt, pl.BlockSpec(cw.shape, lambda i: (0, 0))],
        out_specs=main, out_shape=jax.ShapeDtypeStruct(dc.shape, BF),
    )(dc, dc, dc, cw)


def loss_head(x, target, tiles_per_batch, ctx_tiles, name):
    n_tiles = x.shape[0] // ROW_TILE
    lat_tiles = tiles_per_batch - ctx_tiles

    def tgt_idx(i):
        j = i % tiles_per_batch
        return jnp.where(j < ctx_tiles, 0, (i // tiles_per_batch) * lat_tiles + j - ctx_tiles), 0

    def body(x_ref, t_ref, dx_ref, l_ref):
        lat = (pl.program_id(0) % tiles_per_batch >= ctx_tiles).astype(F32)
        err = (x_ref[...] - t_ref[...]) * lat
        dx_ref[...] = err * (1.0 / D_MODEL)
        l_ref[...] = jnp.full(l_ref.shape, 0.5 / D_MODEL * jnp.sum(err * err), F32)

    return pl.pallas_call(
        body, name=name, grid=(n_tiles,),
        in_specs=[pl.BlockSpec((ROW_TILE, D_MODEL), lambda i: (i, 0)), pl.BlockSpec((ROW_TILE, D_MODEL), tgt_idx)],
        out_specs=[pl.BlockSpec((ROW_TILE, D_MODEL), lambda i: (i, 0)), pl.BlockSpec((1, 8, LANES), lambda i: (i, 0, 0))],
        out_shape=[jax.ShapeDtypeStruct(x.shape, F32), jax.ShapeDtypeStruct((n_tiles, 8, LANES), F32)],
    )(x, target)


def adamw(w, g, m, v, name):
    rows, cols = w.shape
    tr = rows
    for cand in (512, 256, 128, 64, 32, 16, 8):
        if rows % cand == 0 and cand * cols * 4 <= (1 << 20):
            tr = cand
            break

    def body(w_ref, g_ref, m_ref, v_ref, d_ref, mo_ref, vo_ref):
        gg = g_ref[...]
        m2 = ADAM_B1 * m_ref[...] + (1.0 - ADAM_B1) * gg
        v2 = ADAM_B2 * v_ref[...] + (1.0 - ADAM_B2) * (gg * gg)
        m_hat = m2 / (1.0 - ADAM_B1 ** ADAM_STEP)
        v_hat = v2 / (1.0 - ADAM_B2 ** ADAM_STEP)
        d_ref[...] = -ADAM_LR * (m_hat / (jnp.sqrt(v_hat) + ADAM_EPS) + ADAM_WD * w_ref[...])
        mo_ref[...] = m2
        vo_ref[...] = v2

    spec = pl.BlockSpec((tr, cols), lambda i: (i, 0))
    return pl.pallas_call(body, name=name, grid=(rows // tr,), in_specs=[spec] * 4, out_specs=[spec] * 3,
                          out_shape=[jax.ShapeDtypeStruct(w.shape, F32)] * 3)(w, g, m, v)


def fn_post_mod(rows, tps, consts, ws):
    (x, a), (g, sc, sh), (lng, lnb) = rows, tps, ws
    y = layer_norm(ALPHA * x + g * a, lng, lnb)
    return [y, y * (1.0 + sc) + sh]


def _pick(n, cands):
    for c in cands:
        if n % c == 0:
            return c
    return n


def rope_tables(lc, l):
    pos = jnp.arange(l, dtype=F32)
    ret_inv = 1.0 / (ROPE_BASE ** jnp.linspace(0.0, 1.0, RET_DK // 2, dtype=F32))
    ang = pos[:, None] * ret_inv
    rc, rs = jnp.cos(ang), jnp.sin(ang)
    n_ax = MLA_D_ROPE // 4
    ax_inv = ROPE_BASE ** (-jnp.arange(n_ax, dtype=F32) / n_ax)
    rows_n = l // GRID_W
    rows = jnp.repeat(jnp.arange(rows_n, dtype=F32), GRID_W)
    cols = jnp.tile(jnp.arange(GRID_W, dtype=F32), rows_n)
    ra, ca = rows[:, None] * ax_inv, cols[:, None] * ax_inv
    rwc, rws, clc, cls = jnp.cos(ra), jnp.sin(ra), jnp.cos(ca), jnp.sin(ca)
    one = lambda n: jnp.ones((l, n), F32)
    zero = lambda n: jnp.zeros((l, n), F32)
    cat = lambda parts: jnp.concatenate(parts, axis=1)

    def with_ctx(tab, is_cos):
        head = jnp.ones((lc, LANES), F32) if is_cos else jnp.zeros((lc, LANES), F32)
        return jnp.concatenate([head, tab], axis=0)

    ret = (cat([rc, rc] * 4), cat([-rs, zero(16)] * 4), cat([zero(16), rs] * 4))
    ax_c = [rwc, rwc, clc, clc]
    ax_a = [-rws, zero(8), -cls, zero(8)]
    ax_b = [zero(8), rws, zero(8), cls]
    qt = (cat([one(64)] + ax_c + [one(32)]), cat([zero(64)] + ax_a + [zero(32)]), cat([zero(64)] + ax_b + [zero(32)]))
    kt = (cat([one(32)] + ax_c + [one(64)]), cat([zero(32)] + ax_a + [zero(64)]), cat([zero(32)] + ax_b + [zero(64)]))
    fix = lambda t3: tuple(with_ctx(t, k == 0) for k, t in enumerate(t3))
    return fix(ret), fix(qt), fix(kt)


_IN_ORDER = ((0, 128), (128, 256), (256, 512), (544, 800), (800, 928), (928, 1056), (1056, 1312), (1312, 1568),
             (1568, 1824), (1824, 1952), (512, 544), (1952, 1984))


def permute_w_in(w):
    parts = [w[:, a:b] for a, b in _IN_ORDER] + [jnp.zeros((w.shape[0], D_INP - D_IN), w.dtype)]
    return jnp.concatenate(parts, axis=1)


def unpermute_w_in(g):
    out, at = {}, 0
    for a, b in _IN_ORDER:
        out[a] = g[:, at:at + b - a]
        at += b - a
    return jnp.concatenate([out[a] for a in sorted(out)], axis=1)


def layer_weights(w, l):
    f = lambda a: a.astype(F32)
    r = {}
    r["win"] = permute_w_in(w["w_in"][l]).astype(BF)
    r["wout"] = w["w_out"][l].astype(BF)
    r["wup"] = w["ffn_up"][l].astype(BF)
    r["wdown"] = w["ffn_down"][l].astype(BF)
    uq = w["mla_w_uq"][l].reshape(256, MLA_HEADS, 96)
    r["wuq"] = jnp.pad(uq, ((0, 0), (0, 0), (0, 32))).reshape(256, 8 * LANES).astype(BF)
    uk = w["mla_w_uk"][l].reshape(128, MLA_HEADS, 64)
    r["wuk"] = jnp.pad(uk, ((0, 0), (0, 0), (0, 64))).reshape(128, 8 * LANES).astype(BF)
    r["wuv"] = w["mla_w_uv"][l].astype(BF)
    gw = f(w["gla_gate_w"][l])
    z16 = jnp.zeros((16, LANES), F32)
    z96 = jnp.zeros((96, LANES), F32)
    r["gw"] = (jnp.concatenate([gw[0], z16, z96], axis=0), jnp.concatenate([z16, gw[1], z96], axis=0))
    r["gb"] = tuple(f(w["gla_gate_b"][l][d]).reshape(1, LANES) for d in range(2))
    r["rdec"] = tuple(jnp.repeat(f(w["ret_decay"][l][d]), 32).reshape(1, LANES) for d in range(2))
    r["gng"] = jnp.tile(f(w["gla_norm_g"][l]), 4).reshape(1, 256)
    r["gq"] = f(w["mla_q_norm_g"][l]).reshape(1, 256)
    r["gkv"] = f(w["mla_kv_norm_g"][l]).reshape(1, 128)
    for n in ("ln1_g", "ln1_b", "ln2_g", "ln2_b"):
        r[n] = f(w[n][l]).reshape(1, D_MODEL)
    r["cw"] = f(w["ffn_conv_w"][l])
    r["cb"] = f(w["ffn_conv_b"][l]).reshape(1, 2 * D_FF)
    return r


def tile_params(mod_l, nb, nct, nlt):
    m6 = mod_l.reshape(8, 6, D_MODEL)
    out = []
    for j in range(6):
        parts = []
        for b in range(nb):
            parts.append(jnp.broadcast_to(m6[4, j], (nct, 1, D_MODEL)))
            parts.append(jnp.broadcast_to(m6[b, j], (nlt, 1, D_MODEL)))
        out.append(jnp.concatenate(parts, axis=0))
    return out


def tile_param_grads(dts, nb, nct, nlt):
    cols = []
    for dt in dts:
        d = dt.reshape(nb, nct + nlt, D_MODEL)
        lat = jnp.sum(d[:, nct:], axis=1)
        ctx = jnp.sum(d[:, :nct], axis=(0, 1))
        cols.append(jnp.concatenate([lat, jnp.zeros((4 - nb, D_MODEL), F32), ctx[None], jnp.zeros((3, D_MODEL), F32)], axis=0))
    return jnp.stack(cols, axis=1).reshape(8, 6 * D_MODEL)


def layer_forward(x, h1, tp, tp_next, lw, tabs, dims, tag):
    nb, lc, lt = dims
    t = x.shape[0]
    nbt = lt // ROW_TILE
    ncc, nch = lc // CHUNK, lt // CHUNK
    tm = _pick(t, (1024, 768, 512, 256))
    ret_tab, q_tab, k_tab = tabs
    full = lambda a: (a, a.shape[1], 0)
    p = matmul(h1, lw["win"], "nn", F32, tm, 1024, 1024, "proj_in")
    ogf, orf, sgf, srf = scan_fwd(p, ret_tab, lw["gw"][0], lw["gb"][0], lw["rdec"][0], nb, ncc, nch, False, "scan_fwd_f")
    ogb, orb, sgb, srb = scan_fwd(p, ret_tab, lw["gw"][1], lw["gb"][1], lw["rdec"][1], nb, ncc, nch, True, "scan_fwd_b")
    prep_rows = [(p, 256, C_CQ // 256), (p, 128, C_CKV // 128), (p, 128, C_LK // 128)]
    prep_consts = [(a, nbt) for a in q_tab + k_tab]
    prep_ws = [lw["gq"], lw["gkv"], lw["wuq"], lw["wuk"], lw["wuv"]]
    q, k, v = stage_fwd(fn_mla_prep, prep_rows, [], prep_consts, prep_ws, [(1024, BF), (1024, BF), (512, BF)], "mla_prep")
    mo = mla_fwd(q, k, v, nb, lc, lt, "mla_attn")
    mix_rows = [full(ogf), full(ogb), full(orf), full(orb), (p, 256, C_GG // 256), (p, 256, C_RG // 256), full(mo)]
    m, = stage_fwd(fn_mix, mix_rows, [], [], [lw["gng"]], [(1024, BF)], "mix")
    a = matmul(m, lw["wout"], "nn", F32, tm, 1024, 1024, "proj_out")
    x1, h2 = stage_fwd(fn_post_mod, [full(x), full(a)], [tp[2], tp[4], tp[3]], [], [lw["ln1_g"], lw["ln1_b"]],
                       [(1024, F32), (1024, BF)], "post1")
    u = matmul(h2, lw["wup"], "nn", F32, tm, 1408, 1024, "ffn_up")
    act = gate_fwd(u, lw["cw"], lw["cb"], lt, lc, "ffn_gate")
    f = matmul(act, lw["wdown"], "nn", F32, tm, 1024, 1408, "ffn_down")
    if tp_next is None:
        x2, = stage_fwd(fn_post, [full(x1), full(f)], [tp[5]], [], [lw["ln2_g"], lw["ln2_b"]], [(1024, F32)], "post2_last")
        h1n = None
    else:
        x2, h1n = stage_fwd(fn_post_mod, [full(x1), full(f)], [tp[5], tp_next[0], tp_next[1]], [],
                            [lw["ln2_g"], lw["ln2_b"]], [(1024, F32), (1024, BF)], "post2")
    res = dict(x=x, h1=h1, p=p, ogf=ogf, orf=orf, sgf=sgf, srf=srf, ogb=ogb, orb=orb, sgb=sgb, srb=srb, q=q, k=k, v=v,
               mo=mo, m=m, a=a, x1=x1, h2=h2, u=u, act=act, f=f, mix_rows=mix_rows, prep_rows=prep_rows,
               prep_consts=prep_consts, prep_ws=prep_ws)
    return x2, h1n, res


def layer_backward(dx2, dh1n, res, tp, tp_next, lw, tabs, dims):
    nb, lc, lt = dims
    r = res
    t = dx2.shape[0]
    ncc, nch = lc // CHUNK, lt // CHUNK
    tm = _pick(t, (1024, 768, 512, 256))
    tkr = _pick(t, (2304, 1536, 1024, 768, 512))
    ret_tab = tabs[0]
    full = lambda a: (a, a.shape[1], 0)
    g = {}
    if tp_next is None:
        (dx1a, df), (dg2,), (g["ln2_g"], g["ln2_b"]) = stage_bwd(
            fn_post, [full(r["x1"]), full(r["f"])], [tp[5]], [], [lw["ln2_g"], lw["ln2_b"]], [dx2], [F32, BF], "post2_last_bwd")
        dnext = None
    else:
        (dx1a, df), (dg2, dsc1n, dsh1n), (g["ln2_g"], g["ln2_b"]) = stage_bwd(
            fn_post_mod, [full(r["x1"]), full(r["f"])], [tp[5], tp_next[0], tp_next[1]], [], [lw["ln2_g"], lw["ln2_b"]],
            [dx2, dh1n], [F32, BF], "post2_bwd")
        dnext = (dsc1n, dsh1n)
    dact = matmul(df, lw["wdown"], "nt", F32, tm, 1408, 1024, "ffn_down_dx")
    g["ffn_down"] = matmul(r["act"], df, "tn", F32, 1408, 1024, tkr, "ffn_down_dw")
    dc, dcw = gate_bwd(r["u"], lw["cw"], lw["cb"], dact, lt, lc, "ffn_gate_bwd")
    g["ffn_conv_w"], g["ffn_conv_b"] = dcw[0:3], dcw[3]
    du = conv_transpose(dc, lw["cw"], lt, lc, "ffn_conv_t")
    dh2 = matmul(du, lw["wup"], "nt", F32, tm, 1024, 1408, "ffn_up_dx")
    g["ffn_up"] = matmul(r["h2"], du, "tn", F32, 1024, 1408, tkr, "ffn_up_dw")
    (dxa, da), (dg1, dsc2, dsh2), (g["ln1_g"], g["ln1_b"]) = stage_bwd(
        fn_post_mod, [full(r["x"]), full(r["a"])], [tp[2], tp[4], tp[3]], [], [lw["ln1_g"], lw["ln1_b"]],
        [dx1a, dh2], [F32, BF], "post1_bwd")
    dm = matmul(da, lw["wout"], "nt", F32, tm, 1024, 1024, "proj_out_dx")
    g["w_out"] = matmul(r["m"], da, "tn", F32, 1024, 1024, tkr, "proj_out_dw")
    (dog, _, dor, _, dpg, dpr, dmo), _, (dgng,) = stage_bwd(
        fn_mix, r["mix_rows"], [], [], [lw["gng"]], [dm], [F32, None, F32, None, F32, F32, F32], "mix_bwd")
    g["gla_norm_g"] = jnp.sum(dgng.reshape(4, 64), axis=0)
    dq, dk, dv = mla_bwd(r["q"], r["k"], r["v"], dmo, nb, lc, lt, "mla_attn_bwd")
    (dpq, dpkv, dplk), _, (dgq, dgkv, dwuq, dwuk, dwuv) = stage_bwd(
        fn_mla_prep, r["prep_rows"], [], r["prep_consts"], r["prep_ws"], [dq, dk, dv], [F32, F32, F32], "mla_prep_bwd")
    g["mla_q_norm_g"], g["mla_kv_norm_g"] = dgq.reshape(256), dgkv.reshape(128)
    g["mla_w_uq"] = dwuq.reshape(256, MLA_HEADS, LANES)[:, :, :96].reshape(256, MLA_HEADS * 96)
    g["mla_w_uk"] = dwuk.reshape(128, MLA_HEADS, LANES)[:, :, :64].reshape(128, MLA_HEADS * 64)
    g["mla_w_uv"] = dwuv
    s7, dgw0, dgb0, drd0 = scan_bwd(r["p"], ret_tab, lw["gw"][0], lw["gb"][0], lw["rdec"][0], r["sgf"], r["srf"], dog, dor,
                                    None, nb, ncc, nch, False, "scan_bwd_f")
    s7, dgw1, dgb1, drd1 = scan_bwd(r["p"], ret_tab, lw["gw"][1], lw["gb"][1], lw["rdec"][1], r["sgb"], r["srb"], dog, dor,
                                    s7, nb, ncc, nch, True, "scan_bwd_b")
    g["gla_gate_w"] = jnp.stack([dgw0[0:16], dgw1[16:32]])
    g["gla_gate_b"] = jnp.stack([dgb0[0], dgb1[0]])
    g["ret_decay"] = jnp.stack([jnp.sum(drd0.reshape(4, 32), axis=1), jnp.sum(drd1.reshape(4, 32), axis=1)])
    gq_, gk_, gv_, glrk, rq_, rk_, rv_ = s7
    pieces = [gq_, gk_, gv_, dpg, rq_, rk_, rv_, dpr, dpq, dpkv, glrk, dplk]
    dp, = stage_fwd(fn_assemble, [full(a) for a in pieces], [], [], [], [(D_INP, BF)], "dproj_assemble")
    dh1 = matmul(dp, lw["win"], "nt", F32, tm, 1024, 1024, "proj_in_dx")
    g["w_in"] = unpermute_w_in(matmul(r["h1"], dp, "tn", F32, 1024, 1024, tkr, "proj_in_dw"))
    for n in ("ln1_g", "ln1_b", "ln2_g", "ln2_b"):
        g[n] = g[n].reshape(D_MODEL)
    dtp = [None, None, dg1, dsh2, dsc2, dg2]
    return dxa, dh1, dtp, dnext, g


def local_step(x, c, ctx, c_ctx, w, loss_target):
    nb, l, _ = x.shape
    lc = ctx.shape[1]
    lt = lc + l
    dims = (nb, lc, lt)
    nct, nlt = lc // ROW_TILE, l // ROW_TILE
    tabs = rope_tables(lc, l)
    x0 = jnp.concatenate([ctx, x], axis=1).reshape(nb * lt, D_MODEL)
    s8 = jnp.concatenate([silu(c), jnp.zeros((4 - nb, D_MODEL), F32), silu(c_ctx)[None], jnp.zeros((3, D_MODEL), F32)], axis=0)
    lws, tps = [], []
    for i in range(DEPTH):
        mod = matmul(s8, w["ada_w"][i].astype(BF), "nn", F32, 8, 1536, 1024, "ada_mod") + w["ada_b"][i].astype(F32)[None]
        tps.append(tile_params(mod, nb, nct, nlt))
        lws.append(layer_weights(w, i))
    h1, = stage_fwd(fn_modulate, [(x0, D_MODEL, 0)], [tps[0][1], tps[0][0]], [], [], [(D_MODEL, BF)], "mod_in")
    xs, ress = x0, []
    for i in range(DEPTH):
        tpn = None if i == DEPTH - 1 else (tps[i + 1][1], tps[i + 1][0])
        xs, h1, res = layer_forward(xs, h1, tps[i], tpn, lws[i], tabs, dims, i)
        ress.append(res)
    dx, lparts = loss_head(xs, loss_target.reshape(nb * l, D_MODEL), nct + nlt, nct, "loss_head")
    loss = jnp.sum(lparts[:, 0, 0])
    grads = [None] * DEPTH
    dtps = [None] * DEPTH
    dh1 = None
    for i in reversed(range(DEPTH)):
        tpn = None if i == DEPTH - 1 else (tps[i + 1][1], tps[i + 1][0])
        dx, dh1, dtp, dn, grads[i] = layer_backward(dx, dh1, ress[i], tps[i], tpn, lws[i], tabs, dims)
        if dn is not None:
            dtps[i + 1][1], dtps[i + 1][0] = dn
        dtps[i] = dtp
    (dx0b,), (dsc1, dsh1), _ = stage_bwd(fn_modulate, [(x0, D_MODEL, 0)], [tps[0][1], tps[0][0]], [], [], [dh1], [F32], "mod_in_bwd")
    dtps[0][1], dtps[0][0] = dsc1, dsh1
    grad_x = (dx + dx0b).reshape(nb, lt, D_MODEL)[:, lc:]
    dmod = jnp.stack([tile_param_grads(d, nb, nct, nlt) for d in dtps])
    gw = {n: jnp.stack([grads[i][n] for i in range(DEPTH)]) for n in grads[0]}
    return loss, grad_x, gw, dmod, s8


MESH_IDS = pl.DeviceIdType.MESH
ANY = pl.BlockSpec(memory_space=pl.ANY)


def _place():
    return lax.axis_index("x"), lax.axis_index("y"), lax.axis_index("c")


def all_gather8(block, name):
    def body(x_ref, out_ref, send_sems, recv_sems, local_sem):
        x, y, c = _place()
        me, sibling = (x, y, c), (x, y, 1 - c)
        chips = [(1 - x, y), (x, 1 - y), (1 - x, 1 - y)]

        def slot(px, py, pc):
            return out_ref.at[4 * px + 2 * py + pc]

        def copy(k, blk, to, src=None):
            return pltpu.make_async_remote_copy(
                src_ref=slot(*blk) if src is None else src, dst_ref=slot(*blk),
                send_sem=send_sems.at[k], recv_sem=recv_sems.at[k], device_id=to, device_id_type=MESH_IDS)

        mine = pltpu.make_async_copy(x_ref, slot(*me), local_sem)
        mine.start()
        first = [copy(0, me, sibling, src=x_ref)]
        first += [copy(1 + j, me, (*chip, c), src=x_ref) for j, chip in enumerate(chips)]
        for cp in first:
            cp.start()
        passed = [copy(4 + j, (*chip, c), sibling) for j, chip in enumerate(chips)]
        for j, chip in enumerate(chips):
            copy(1 + j, (*chip, c), me).wait_recv()
            passed[j].start()
        copy(0, sibling, me).wait_recv()
        for j, chip in enumerate(chips):
            copy(4 + j, (*chip, 1 - c), me).wait_recv()
        for cp in first + passed:
            cp.wait_send()
        mine.wait()

    return pl.pallas_call(
        body, name=name, out_shape=jax.ShapeDtypeStruct((8,) + block.shape, block.dtype),
        in_specs=[ANY], out_specs=ANY,
        scratch_shapes=[pltpu.SemaphoreType.DMA((7,)), pltpu.SemaphoreType.DMA((7,)), pltpu.SemaphoreType.DMA],
    )(block)


def swap_cores(block, name):
    def body(x_ref, out_ref, send_sem, recv_sem):
        x, y, c = _place()
        cp = pltpu.make_async_remote_copy(src_ref=x_ref, dst_ref=out_ref, send_sem=send_sem, recv_sem=recv_sem,
                                          device_id=(x, y, 1 - c), device_id_type=MESH_IDS)
        cp.start()
        cp.wait()

    return pl.pallas_call(
        body, name=name, out_shape=jax.ShapeDtypeStruct(block.shape, block.dtype), in_specs=[ANY], out_specs=ANY,
        scratch_shapes=[pltpu.SemaphoreType.DMA, pltpu.SemaphoreType.DMA],
    )(block)


def gather_cores(block, name):
    def body(x_ref, out_ref, send_sem, recv_sem, local_sem):
        x, y, c = _place()
        mine = pltpu.make_async_copy(x_ref, out_ref.at[c], local_sem)
        mine.start()
        cp = pltpu.make_async_remote_copy(src_ref=x_ref, dst_ref=out_ref.at[c], send_sem=send_sem, recv_sem=recv_sem,
                                          device_id=(x, y, 1 - c), device_id_type=MESH_IDS)
        cp.start()
        pltpu.make_async_remote_copy(src_ref=x_ref, dst_ref=out_ref.at[1 - c], send_sem=send_sem, recv_sem=recv_sem,
                                     device_id=(x, y, 1 - c), device_id_type=MESH_IDS).wait_recv()
        cp.wait_send()
        mine.wait()

    return pl.pallas_call(
        body, name=name, out_shape=jax.ShapeDtypeStruct((2,) + block.shape, block.dtype), in_specs=[ANY], out_specs=ANY,
        scratch_shapes=[pltpu.SemaphoreType.DMA, pltpu.SemaphoreType.DMA, pltpu.SemaphoreType.DMA],
    )(block)


def exchange_chips(parts, name):
    def body(p_ref, out_ref, send_sems, recv_sems, local_sem):
        x, y, c = _place()
        jm = 2 * x + y
        chips = [(1 - x, y), (x, 1 - y), (1 - x, 1 - y)]
        mine = pltpu.make_async_copy(p_ref.at[jm], out_ref.at[jm], local_sem)
        mine.start()
        cps = []
        for k, (px, py) in enumerate(chips):
            cps.append(pltpu.make_async_remote_copy(
                src_ref=p_ref.at[2 * px + py], dst_ref=out_ref.at[jm], send_sem=send_sems.at[k], recv_sem=recv_sems.at[k],
                device_id=(px, py, c), device_id_type=MESH_IDS))
        for cp in cps:
            cp.start()
        for k, (px, py) in enumerate(chips):
            pltpu.make_async_remote_copy(
                src_ref=p_ref.at[jm], dst_ref=out_ref.at[2 * px + py], send_sem=send_sems.at[k], recv_sem=recv_sems.at[k],
                device_id=(px, py, c), device_id_type=MESH_IDS).wait_recv()
        for cp in cps:
            cp.wait_send()
        mine.wait()

    return pl.pallas_call(
        body, name=name, out_shape=jax.ShapeDtypeStruct(parts.shape, parts.dtype), in_specs=[ANY], out_specs=ANY,
        scratch_shapes=[pltpu.SemaphoreType.DMA((3,)), pltpu.SemaphoreType.DMA((3,)), pltpu.SemaphoreType.DMA],
    )(parts)


def _row_tile(rows, cols, itemsize=4, limit=1 << 20):
    for cand in (2048, 1024, 512, 256, 128, 64, 32, 16):
        if rows % cand == 0 and cand * cols * itemsize <= limit:
            return cand
    return rows


def add_halves(a, b, name):
    _, m, n = a.shape
    tr = _row_tile(m, n)

    def body(a_ref, b_ref, s_ref, sb_ref):
        s = a_ref[...] + b_ref[...].astype(F32)
        s_ref[...] = s
        sb_ref[...] = s.astype(BF)

    spec = pl.BlockSpec((1, tr, n), lambda j, i: (j, i, 0))
    return pl.pallas_call(body, name=name, grid=(4, m // tr), in_specs=[spec, spec], out_specs=[spec, spec],
                          out_shape=[jax.ShapeDtypeStruct(a.shape, F32), jax.ShapeDtypeStruct(a.shape, BF)])(a, b)


def sum_slots(a, name):
    s, m, n = a.shape
    tr = _row_tile(m, n, limit=(1 << 21) // s)

    def body(a_ref, o_ref):
        acc = a_ref[0].astype(F32)
        for j in range(1, s):
            acc = acc + a_ref[j].astype(F32)
        o_ref[...] = acc

    return pl.pallas_call(body, name=name, grid=(m // tr,), in_specs=[pl.BlockSpec((s, tr, n), lambda i: (0, i, 0))],
                          out_specs=pl.BlockSpec((tr, n), lambda i: (i, 0)), out_shape=jax.ShapeDtypeStruct((m, n), F32))(a)


COL_SHARDED = ("ada_w", "w_in", "mla_w_uq", "mla_w_uk", "mla_w_uv", "ffn_up", "ffn_conv_w")
ROW_SHARDED = ("w_out", "ffn_down")
GATHERED = ("ada_w", "w_in", "mla_w_uq", "mla_w_uk", "mla_w_uv", "w_out", "ffn_up", "ffn_down", "ffn_conv_w")
REDUCED = ("w_in", "mla_w_uq", "mla_w_uk", "mla_w_uv", "w_out", "ffn_up", "ffn_down", "ffn_conv_w")
SMALL = ("ada_b", "gla_gate_w", "gla_gate_b", "gla_norm_g", "ret_decay", "mla_q_norm_g", "mla_kv_norm_g",
         "ln1_g", "ln1_b", "ffn_conv_b", "ln2_g", "ln2_b")
WEIGHTS = ("c_ctx", "ada_w", "ada_b", "w_in", "gla_gate_w", "gla_gate_b", "gla_norm_g", "ret_decay", "mla_q_norm_g",
           "mla_kv_norm_g", "mla_w_uq", "mla_w_uk", "mla_w_uv", "w_out", "ln1_g", "ln1_b", "ffn_up", "ffn_conv_w",
           "ffn_conv_b", "ffn_down", "ln2_g", "ln2_b")
PACK = 16 * LANES


def _pad_flat(v, n):
    return jnp.concatenate([v, jnp.zeros((n - v.shape[0],), v.dtype)]) if n > v.shape[0] else v


def _to_bf16_flat(a):
    if a.dtype == BF:
        return a.reshape(-1)
    return lax.bitcast_convert_type(a.astype(F32), BF).reshape(-1)


def gather_weights(shards, c):
    flats, sizes = [], []
    for n in GATHERED:
        a = shards[n] if n == "ffn_conv_w" else shards[n].astype(BF)
        fl = _to_bf16_flat(a)
        flats.append(fl)
        sizes.append(fl.shape[0])
    total = sum(sizes)
    half = -(-total // (2 * PACK)) * PACK
    flat = _pad_flat(jnp.concatenate(flats), 2 * half).reshape(2, half // LANES, LANES)
    mine = lax.dynamic_index_in_dim(flat, c, axis=0, keepdims=False)
    got = all_gather8(mine, "gather_weights").reshape(4, 2 * half)
    out, at = {}, 0
    for n, sz in zip(GATHERED, sizes):
        piece = got[:, at:at + sz]
        at += sz
        shp = shards[n].shape
        if n == "ffn_conv_w":
            piece = lax.bitcast_convert_type(piece.reshape(4, sz // 2, 2), F32)
        piece = piece.reshape((4,) + shp)
        if n in ROW_SHARDED:
            full = jnp.moveaxis(piece, 0, 1).reshape(shp[0], 4 * shp[1], shp[2])
        else:
            full = jnp.moveaxis(piece, 0, -2).reshape(shp[:-1] + (4 * shp[-1],))
        out[n] = full
    return out


def reduce_gradients(gw, c):
    chunks, sizes, shapes = [], [], {}
    for n in REDUCED:
        g = gw[n]
        if n in ROW_SHARDED:
            parts = g.reshape(g.shape[0], 4, g.shape[1] // 4, g.shape[2])
            parts = jnp.moveaxis(parts, 1, 0)
        else:
            parts = g.reshape(g.shape[:-1] + (4, g.shape[-1] // 4))
            parts = jnp.moveaxis(parts, -2, 0)
        shapes[n] = parts.shape[1:]
        chunks.append(parts.reshape(4, -1))
        sizes.append(chunks[-1].shape[1])
    total = sum(sizes)
    half = -(-total // (2 * PACK)) * PACK
    flat = jnp.concatenate(chunks + [jnp.zeros((4, 2 * half - total), F32)], axis=1).reshape(4, 2, half // LANES, LANES)
    keep = lax.dynamic_index_in_dim(flat, c, axis=1, keepdims=False)
    give = lax.dynamic_index_in_dim(flat, 1 - c, axis=1, keepdims=False).astype(BF)
    got = swap_cores(give, "grad_swap_cores")
    _, part_bf = add_halves(keep, got, "grad_add_cores")
    landed = exchange_chips(part_bf, "grad_exchange_chips")
    mine = sum_slots(landed, "grad_sum_chips")
    both = gather_cores(mine, "grad_gather_cores").reshape(2 * half)
    out, at = {}, 0
    for n, sz in zip(REDUCED, sizes):
        out[n] = both[at:at + sz].reshape(shapes[n])
        at += sz
    return out


def _pack_small(d, names):
    flat = jnp.concatenate([d[n].astype(F32).reshape(-1) for n in names])
    n = -(-flat.shape[0] // PACK) * PACK
    return _pad_flat(flat, n).reshape(n // LANES, LANES)


def _unpack_small(buf, like, names):
    flat, out, at = buf.reshape(-1), {}, 0
    for n in names:
        sz = like[n].size
        out[n] = flat[at:at + sz].reshape(like[n].shape)
        at += sz
    return out


def kernel(x, c, ctx, c_ctx, ada_w, ada_b, w_in, gla_gate_w, gla_gate_b, gla_norm_g, ret_decay, mla_q_norm_g, mla_kv_norm_g, mla_w_uq, mla_w_uk, mla_w_uv, w_out, ln1_g, ln1_b, ffn_up, ffn_conv_w, ffn_conv_b, ffn_down, ln2_g, ln2_b, loss_target, m_c_ctx, m_ada_w, m_ada_b, m_w_in, m_gla_gate_w, m_gla_gate_b, m_gla_norm_g, m_ret_decay, m_mla_q_norm_g, m_mla_kv_norm_g, m_mla_w_uq, m_mla_w_uk, m_mla_w_uv, m_w_out, m_ln1_g, m_ln1_b, m_ffn_up, m_ffn_conv_w, m_ffn_conv_b, m_ffn_down, m_ln2_g, m_ln2_b, v_c_ctx, v_ada_w, v_ada_b, v_w_in, v_gla_gate_w, v_gla_gate_b, v_gla_norm_g, v_ret_decay, v_mla_q_norm_g, v_mla_kv_norm_g, v_mla_w_uq, v_mla_w_uk, v_mla_w_uv, v_w_out, v_ln1_g, v_ln1_b, v_ffn_up, v_ffn_conv_w, v_ffn_conv_b, v_ffn_down, v_ln2_g, v_ln2_b):
    w = dict(c_ctx=c_ctx, ada_w=ada_w, ada_b=ada_b, w_in=w_in, gla_gate_w=gla_gate_w, gla_gate_b=gla_gate_b, gla_norm_g=gla_norm_g, ret_decay=ret_decay, mla_q_norm_g=mla_q_norm_g, mla_kv_norm_g=mla_kv_norm_g, mla_w_uq=mla_w_uq, mla_w_uk=mla_w_uk, mla_w_uv=mla_w_uv, w_out=w_out, ln1_g=ln1_g, ln1_b=ln1_b, ffn_up=ffn_up, ffn_conv_w=ffn_conv_w, ffn_conv_b=ffn_conv_b, ffn_down=ffn_down, ln2_g=ln2_g, ln2_b=ln2_b)
    m = dict(c_ctx=m_c_ctx, ada_w=m_ada_w, ada_b=m_ada_b, w_in=m_w_in, gla_gate_w=m_gla_gate_w, gla_gate_b=m_gla_gate_b, gla_norm_g=m_gla_norm_g, ret_decay=m_ret_decay, mla_q_norm_g=m_mla_q_norm_g, mla_kv_norm_g=m_mla_kv_norm_g, mla_w_uq=m_mla_w_uq, mla_w_uk=m_mla_w_uk, mla_w_uv=m_mla_w_uv, w_out=m_w_out, ln1_g=m_ln1_g, ln1_b=m_ln1_b, ffn_up=m_ffn_up, ffn_conv_w=m_ffn_conv_w, ffn_conv_b=m_ffn_conv_b, ffn_down=m_ffn_down, ln2_g=m_ln2_g, ln2_b=m_ln2_b)
    v = dict(c_ctx=v_c_ctx, ada_w=v_ada_w, ada_b=v_ada_b, w_in=v_w_in, gla_gate_w=v_gla_gate_w, gla_gate_b=v_gla_gate_b, gla_norm_g=v_gla_norm_g, ret_decay=v_ret_decay, mla_q_norm_g=v_mla_q_norm_g, mla_kv_norm_g=v_mla_kv_norm_g, mla_w_uq=v_mla_w_uq, mla_w_uk=v_mla_w_uk, mla_w_uv=v_mla_w_uv, w_out=v_w_out, ln1_g=v_ln1_g, ln1_b=v_ln1_b, ffn_up=v_ffn_up, ffn_conv_w=v_ffn_conv_w, ffn_conv_b=v_ffn_conv_b, ffn_down=v_ffn_down, ln2_g=v_ln2_g, ln2_b=v_ln2_b)
    xi, yi, ci = _place()
    chip = 2 * xi + yi

    full = dict(w)
    full.update(gather_weights({n: w[n] for n in GATHERED}, ci))
    loss, grad_x, gw, dmod, s8 = local_step(x, c, ctx, c_ctx, {n: full[n] for n in WEIGHTS if n != "c_ctx"}, loss_target)
    loss = lax.psum(loss, ("x", "y", "c"))

    dsil = jnp.zeros((8, D_MODEL), F32)
    for i in range(DEPTH):
        dsil = dsil + matmul(dmod[i], full["ada_w"][i], "nt", F32, 8, 1024, 1536, "ada_dsilu")

    grads = reduce_gradients(gw, ci)

    small = {n: gw[n] for n in SMALL if n != "ada_b"}
    small.update(dmod=dmod, s8=s8, dsil=dsil[4])
    names = tuple(small)
    packed = _pack_small(small, names)
    everyone = all_gather8(packed, "gather_small")
    summed = _unpack_small(sum_slots(everyone, "sum_small"), small, names)
    for n in SMALL:
        if n != "ada_b":
            grads[n] = summed[n]
    grads["ada_b"] = jnp.sum(summed["dmod"][:, :5], axis=1)
    sg = jax.nn.sigmoid(c_ctx)
    grads["c_ctx"] = summed["dsil"] * (sg * (1.0 + c_ctx * (1.0 - sg)))
    per_dev = [_unpack_small(everyone[d], small, names) for d in range(8)]
    s_all = jnp.concatenate([p["s8"] for p in per_dev], axis=0)
    d_all = jnp.concatenate([p["dmod"] for p in per_dev], axis=1)
    cols = ada_w.shape[2]
    g_ada = []
    for i in range(DEPTH):
        d_mine = lax.dynamic_slice_in_dim(d_all[i], chip * cols, cols, axis=1)
        g_ada.append(matmul(s_all, d_mine, "tn", F32, 1024, cols, 64, "ada_dw"))
    grads["ada_w"] = jnp.stack(g_ada)

    delta, new_m, new_v = {}, {}, {}
    for n in GATHERED:
        shp = w[n].shape
        v2 = lambda a: a.reshape(-1, shp[-1])
        d_, m_, v_ = adamw(v2(w[n]), v2(grads[n]), v2(m[n]), v2(v[n]), "adamw_" + n)
        delta[n], new_m[n], new_v[n] = d_.reshape(shp), m_.reshape(shp), v_.reshape(shp)
    rep = tuple(n for n in WEIGHTS if n not in GATHERED)
    pk = lambda d: _pack_small({n: d[n] for n in rep}, rep)
    d_, m_, v_ = adamw(pk(w), pk(grads), pk(m), pk(v), "adamw_small")
    like = {n: w[n] for n in rep}
    delta.update(_unpack_small(d_, like, rep))
    new_m.update(_unpack_small(m_, like, rep))
    new_v.update(_unpack_small(v_, like, rep))
    grads = {n: grads[n].reshape(w[n].shape) for n in WEIGHTS}
    return (loss, grad_x, *[grads[n] for n in WEIGHTS], *[delta[n] for n in WEIGHTS], *[new_m[n] for n in WEIGHTS],
            *[new_v[n] for n in WEIGHTS])
```

```python
import functools

import jax
import jax.numpy as jnp
from jax import lax
from jax.experimental import pallas as pl
from jax.experimental.pallas import tpu as pltpu

F32 = jnp.float32
BF = jnp.bfloat16

D_MODEL = 1024
DEPTH = 4
GRID_W = 64
GLA_DK = 32
GLA_TAU = 16.0
RET_DK = 32
MLA_HEADS = 8
MLA_D_NOPE = 64
MLA_D_ROPE = 32
MLA_SCALE = (MLA_D_NOPE + MLA_D_ROPE) ** -0.5
D_FF = 2816
ROPE_BASE = 10000.0
EPS = 1e-6
ALPHA = (2 * DEPTH) ** 0.25
ADAM_LR, ADAM_B1, ADAM_B2, ADAM_EPS, ADAM_WD, ADAM_STEP = 0.001, 0.9, 0.999, 1e-08, 0.01, 10

ROW_TILE = 256
CHUNK = 64
GATE_ROWS = 64
LANES = 128

C_GQ, C_GK, C_GV, C_GG, C_RQ, C_RK, C_RV, C_RG, C_CQ, C_CKV, C_LK = 0, 128, 256, 512, 768, 896, 1024, 1280, 1536, 1792, 1920
D_INP = 2048
D_IN = 1984


def _dg(a, b, ca, cb):
    return lax.dot_general(a.astype(BF), b.astype(BF), (((ca,), (cb,)), ((), ())), preferred_element_type=F32)


@jax.custom_vjp
def mm_nn(a, b):
    return _dg(a, b, 1, 0)


@jax.custom_vjp
def mm_nt(a, b):
    return _dg(a, b, 1, 1)


@jax.custom_vjp
def mm_tn(a, b):
    return _dg(a, b, 0, 0)


mm_nn.defvjp(lambda a, b: (_dg(a, b, 1, 0), (a, b)),
             lambda r, g: (mm_nt(g, r[1]).astype(r[0].dtype), mm_tn(r[0], g).astype(r[1].dtype)))
mm_nt.defvjp(lambda a, b: (_dg(a, b, 1, 1), (a, b)),
             lambda r, g: (mm_nn(g, r[1]).astype(r[0].dtype), mm_tn(g, r[0]).astype(r[1].dtype)))
mm_tn.defvjp(lambda a, b: (_dg(a, b, 0, 0), (a, b)),
             lambda r, g: (mm_nt(r[1], g).astype(r[0].dtype), mm_nn(r[0], g).astype(r[1].dtype)))


def _split3(x):
    h = x.astype(BF)
    r = x - h.astype(F32)
    m = r.astype(BF)
    lo = (r - m.astype(F32)).astype(BF)
    return h, m, lo


def _exact(x, mat, left):
    h, m, lo = _split3(x)
    if left:
        d = lambda t: lax.dot_general(mat, t, (((1,), (0,)), ((), ())), preferred_element_type=F32)
    else:
        d = lambda t: lax.dot_general(t, mat, (((1,), (0,)), ((), ())), preferred_element_type=F32)
    return (d(lo) + d(m)) + d(h)


def _iota(shape, axis):
    return lax.broadcasted_iota(jnp.int32, shape, axis)


def _tri(n, upper):
    r, c = _iota((n, n), 0), _iota((n, n), 1)
    return jnp.where((c >= r) if upper else (r >= c), 1.0, 0.0).astype(BF)


@functools.partial(jax.custom_vjp, nondiff_argnums=(1,))
def cumsum_rows(x, upper):
    return _exact(x, _tri(x.shape[0], upper), True)


cumsum_rows.defvjp(lambda x, upper: (cumsum_rows(x, upper), None),
                   lambda upper, r, g: (cumsum_rows(g, not upper),))


def _seg(n, w):
    shift = w.bit_length() - 1
    r, c = _iota((n, n), 0), _iota((n, n), 1)
    return jnp.where(lax.shift_right_logical(r, shift) == lax.shift_right_logical(c, shift), 1.0, 0.0).astype(BF)


@functools.partial(jax.custom_vjp, nondiff_argnums=(1,))
def seg_sum(x, w):
    return _exact(x, _seg(x.shape[1], w), False)


seg_sum.defvjp(lambda x, w: (seg_sum(x, w), None), lambda w, r, g: (seg_sum(g, w),))


def _place_mat(transpose):
    shape = (8 * LANES, LANES) if transpose else (LANES, 8 * LANES)
    r, c = _iota(shape, 0), _iota(shape, 1)
    src, dst = (c, r) if transpose else (r, c)
    dl = jnp.bitwise_and(dst, LANES - 1)
    ok = (dl >= 64) & (dl < 96) & (src == dl - 32)
    return jnp.where(ok, 1.0, 0.0).astype(BF)


@jax.custom_vjp
def place_kr(x):
    return _exact(x, _place_mat(False), False)


place_kr.defvjp(lambda x: (place_kr(x), None), lambda r, g: (_exact(g, _place_mat(True), False),))


@functools.partial(jax.custom_vjp, nondiff_argnums=(1,))
def lane_roll(x, s):
    return pltpu.roll(x, s, 1)


lane_roll.defvjp(lambda x, s: (pltpu.roll(x, s, 1), None),
                 lambda s, r, g: (pltpu.roll(g, (g.shape[1] - s) % g.shape[1], 1),))


def rope(x, tab, d):
    cos, sa, sb = tab
    return x * cos + lane_roll(x, LANES - d) * sa + lane_roll(x, d) * sb


def silu(x):
    return x * jax.nn.sigmoid(x)


def log_sigmoid(z):
    return jnp.minimum(z, 0.0) - jnp.log(1.0 + jnp.exp(-jnp.abs(z)))


def layer_norm(x, g, b):
    mu = jnp.mean(x, axis=-1, keepdims=True)
    xc = x - mu
    var = jnp.mean(xc * xc, axis=-1, keepdims=True)
    return xc * lax.rsqrt(var + EPS) * g + b


def matmul(a, b, mode, out_dtype, tm, tn, tk, name):
    if mode == "nn":
        (m, k), (k2, n) = a.shape, b.shape
        a_spec = pl.BlockSpec((tm, tk), lambda i, j, kk: (i, kk))
        b_spec = pl.BlockSpec((tk, tn), lambda i, j, kk: (kk, j))
        ca, cb = 1, 0
    elif mode == "nt":
        (m, k), (n, k2) = a.shape, b.shape
        a_spec = pl.BlockSpec((tm, tk), lambda i, j, kk: (i, kk))
        b_spec = pl.BlockSpec((tn, tk), lambda i, j, kk: (j, kk))
        ca, cb = 1, 1
    else:
        (k, m), (k2, n) = a.shape, b.shape
        a_spec = pl.BlockSpec((tk, tm), lambda i, j, kk: (kk, i))
        b_spec = pl.BlockSpec((tk, tn), lambda i, j, kk: (kk, j))
        ca, cb = 0, 0
    assert k == k2 and m % tm == 0 and n % tn == 0 and k % tk == 0, (name, a.shape, b.shape, tm, tn, tk)
    nk = k // tk

    def body(a_ref, b_ref, o_ref, *acc):
        part = _dg(a_ref[...], b_ref[...], ca, cb)
        if nk == 1:
            o_ref[...] = part.astype(o_ref.dtype)
            return
        acc_ref, = acc
        kk = pl.program_id(2)

        @pl.when(kk == 0)
        def _():
            acc_ref[...] = part

        @pl.when(kk > 0)
        def _():
            acc_ref[...] += part

        @pl.when(kk == nk - 1)
        def _():
            o_ref[...] = acc_ref[...].astype(o_ref.dtype)

    return pl.pallas_call(
        body, name=name, grid=(m // tm, n // tn, nk),
        in_specs=[a_spec, b_spec], out_specs=pl.BlockSpec((tm, tn), lambda i, j, kk: (i, j)),
        out_shape=jax.ShapeDtypeStruct((m, n), out_dtype),
        scratch_shapes=[] if nk == 1 else [pltpu.VMEM((tm, tn), F32)],
    )(a, b)


def _stage_specs(rows, tps, consts, ws):
    specs, args = [], []
    for arr, width, cb in rows:
        specs.append(pl.BlockSpec((ROW_TILE, width), functools.partial(lambda i, cb: (i, cb), cb=cb)))
        args.append(arr)
    for arr in tps:
        specs.append(pl.BlockSpec((1, 1, arr.shape[2]), lambda i: (i, 0, 0)))
        args.append(arr)
    for arr, period in consts:
        specs.append(pl.BlockSpec((ROW_TILE, arr.shape[1]), functools.partial(lambda i, p: (i % p, 0), p=period)))
        args.append(arr)
    for arr in ws:
        specs.append(pl.BlockSpec(arr.shape, functools.partial(lambda i, nd: (0,) * nd, nd=arr.ndim)))
        args.append(arr)
    return specs, args


def _stage_load(refs, n_rows, n_tps, n_consts, n_ws):
    it = iter(refs)
    rows = [next(it)[...].astype(F32) for _ in range(n_rows)]
    tps = [next(it)[0].astype(F32) for _ in range(n_tps)]
    consts = [next(it)[...].astype(F32) for _ in range(n_consts)]
    ws = [next(it)[...].astype(F32) for _ in range(n_ws)]
    return rows, tps, consts, ws


def stage_fwd(fn, rows, tps, consts, ws, outs, name):
    n_tiles = rows[0][0].shape[0] // ROW_TILE
    specs, args = _stage_specs(rows, tps, consts, ws)
    counts = (len(rows), len(tps), len(consts), len(ws))

    def body(*refs):
        r, t, c, w = _stage_load(refs[:sum(counts)], *counts)
        res = fn(r, t, c, w)
        for o_ref, o in zip(refs[sum(counts):], res):
            o_ref[...] = o.astype(o_ref.dtype)

    res = pl.pallas_call(
        body, name=name, grid=(n_tiles,), in_specs=specs,
        out_specs=[pl.BlockSpec((ROW_TILE, wd), lambda i: (i, 0)) for wd, _ in outs],
        out_shape=[jax.ShapeDtypeStruct((n_tiles * ROW_TILE, wd), dt) for wd, dt in outs],
    )(*args)
    return list(res)


def stage_bwd(fn, rows, tps, consts, ws, cts, row_grads, name):
    n_tiles = rows[0][0].shape[0] // ROW_TILE
    specs, args = _stage_specs(rows, tps, consts, ws)
    counts = (len(rows), len(tps), len(consts), len(ws))
    n_in = sum(counts)
    for ct in cts:
        specs.append(pl.BlockSpec((ROW_TILE, ct.shape[1]), lambda i: (i, 0)))
        args.append(ct)
    want = [k for k, dt in enumerate(row_grads) if dt is not None]
    out_specs = [pl.BlockSpec((ROW_TILE, rows[k][1]), lambda i: (i, 0)) for k in want]
    out_shape = [jax.ShapeDtypeStruct((n_tiles * ROW_TILE, rows[k][1]), row_grads[k]) for k in want]
    out_specs += [pl.BlockSpec((1, 1, a.shape[2]), lambda i: (i, 0, 0)) for a in tps]
    out_shape += [jax.ShapeDtypeStruct((n_tiles, 1, a.shape[2]), F32) for a in tps]
    out_specs += [pl.BlockSpec(a.shape, functools.partial(lambda i, nd: (0,) * nd, nd=a.ndim)) for a in ws]
    out_shape += [jax.ShapeDtypeStruct(a.shape, F32) for a in ws]

    def body(*refs):
        r, t, c, w = _stage_load(refs[:n_in], *counts)
        g = [ref[...].astype(F32) for ref in refs[n_in:n_in + len(cts)]]
        _, vjp = jax.vjp(lambda r_, t_, w_: fn(r_, t_, c, w_), r, t, w)
        dr, dt, dw = vjp(g)
        o = iter(refs[n_in + len(cts):])
        for k in want:
            ref = next(o)
            ref[...] = dr[k].astype(ref.dtype)
        for v in dt:
            next(o)[0] = v
        first = pl.program_id(0) == 0
        for v in dw:
            ref = next(o)

            @pl.when(first)
            def _():
                ref[...] = v

            @pl.when(jnp.logical_not(first))
            def _():
                ref[...] += v

    res = pl.pallas_call(body, name=name, grid=(n_tiles,), in_specs=specs, out_specs=out_specs, out_shape=out_shape)(*args)
    res = list(res)
    drows = [None] * len(rows)
    for k in want:
        drows[k] = res.pop(0)
    dtps = [res.pop(0) for _ in tps]
    dws = [res.pop(0) for _ in ws]
    return drows, dtps, dws


def fn_modulate(rows, tps, consts, ws):
    (x,), (sc, sh) = rows, tps
    return [x * (1.0 + sc) + sh]


def fn_post(rows, tps, consts, ws):
    (x, a), (g,), (lng, lnb) = rows, tps, ws
    return [layer_norm(ALPHA * x + g * a, lng, lnb)]


def fn_mix(rows, tps, consts, ws):
    ogf, ogb, orf, orb, pg, pr, mo = rows
    gng, = ws
    og = ogf + ogb
    out_g = og * lax.rsqrt(seg_sum(og * og, 64) * (1.0 / 64) + EPS) * gng * silu(pg)
    o = orf + orb
    oc = o - seg_sum(o, 64) * (1.0 / 64)
    out_r = oc * lax.rsqrt(seg_sum(oc * oc, 64) * (1.0 / 64) + EPS) * silu(pr)
    return [jnp.concatenate([out_g, out_r, mo], axis=-1)]


def fn_mla_prep(rows, tps, consts, ws):
    pq, pkv, plk = rows
    gq, gkv, wuq, wuk, wuv = ws
    qtab, ktab = consts[0:3], consts[3:6]
    cq = pq * lax.rsqrt(jnp.mean(pq * pq, axis=-1, keepdims=True) + EPS) * gq
    qp = mm_nn(cq, wuq)
    q = jnp.concatenate([rope(qp[:, h * LANES:(h + 1) * LANES], qtab, 8) for h in range(MLA_HEADS)], axis=-1)
    ckv = pkv * lax.rsqrt(jnp.mean(pkv * pkv, axis=-1, keepdims=True) + EPS) * gkv
    k = mm_nn(ckv, wuk) + place_kr(rope(plk, ktab, 8))
    v = mm_nn(ckv, wuv)
    return [q, k, v]


def fn_assemble(rows, tps, consts, ws):
    gq, gk, gv, gg, rq, rk, rv, rg, cq, ckv, lk1, lk2 = rows
    return [jnp.concatenate([gq, gk, gv, gg, rq, rk, rv, rg, cq, ckv, lk1 + lk2], axis=-1)]


def _head_masks():
    hm = (lax.shift_right_logical(_iota((4, 1, LANES), 2), 5) == _iota((4, 1, LANES), 0)).astype(F32)
    vm = (lax.shift_right_logical(_iota((4, 1, 256), 2), 6) == _iota((4, 1, 256), 0)).astype(F32)
    bd = (lax.shift_right_logical(_iota((256, LANES), 0), 6) == lax.shift_right_logical(_iota((256, LANES), 1), 5)).astype(F32)
    return hm, vm, bd


def chunk_step(s, q, k, v, la, upper):
    hm, vm, bd = _head_masks()
    t, u = _iota((4 * CHUNK, CHUNK), 0), _iota((4 * CHUNK, CHUNK), 1)
    t = jnp.bitwise_and(t, CHUNK - 1)
    causal = (u >= t) if upper else (t >= u)
    b = cumsum_rows(la, upper)
    bend = jnp.sum(la, axis=0, keepdims=True)
    half = 0.5 * bend
    qd = q * jnp.exp(b - half)
    kd = k * jnp.exp(half - b)
    qe = (qd[None] * hm).reshape(4 * CHUNK, LANES)
    att = jnp.where(causal, mm_nt(qe, kd), 0.0)
    o_intra = (mm_nn(att, v).reshape(4, CHUNK, 256) * vm).sum(0)
    o = mm_nt(q * jnp.exp(b), s) + o_intra
    s_new = (s * jnp.exp(bend) + mm_tn(v, k * jnp.exp(bend - b))) * bd
    return o, s_new


def scan_step(sg, sr, q, k, v, lrk, rq, rk, rv, gw, gb, rdec, tab, upper):
    la_g = log_sigmoid(mm_nn(lrk, gw) + gb) * (1.0 / GLA_TAU)
    og, sg2 = chunk_step(sg, q * GLA_DK ** -0.5, k, v, la_g, upper)
    la_r = jnp.broadcast_to(log_sigmoid(rdec), (CHUNK, LANES))
    orr, sr2 = chunk_step(sr, rope(rq, tab, 16), rope(rk * RET_DK ** -0.5, tab, 16), rv, la_r, upper)
    return og, orr, sg2, sr2


def _chunk_of(n, ncc, nch, reverse):
    if not reverse:
        return n
    return jnp.where(n < ncc, ncc - 1 - n, nch - 1 + ncc - n)


def _scan_in_specs(p, tabs, gw, gb, rdec, nch, cidx):
    def blk(width, cb):
        return pl.BlockSpec((CHUNK, width), lambda bb, m: (bb * nch + cidx(m), cb))

    specs = [blk(128, C_GQ // 128), blk(128, C_GK // 128), blk(256, C_GV // 256), blk(128, C_LK // 128),
             blk(128, C_RQ // 128), blk(128, C_RK // 128), blk(256, C_RV // 256)]
    args = [p] * 7
    for t in tabs:
        specs.append(pl.BlockSpec((CHUNK, LANES), lambda bb, m: (cidx(m), 0)))
        args.append(t)
    for w in (gw, gb, rdec):
        specs.append(pl.BlockSpec(w.shape, lambda bb, m: (0, 0)))
        args.append(w)
    return specs, args


def scan_fwd(p, tabs, gw, gb, rdec, nb, ncc, nch, reverse, name):
    cidx = lambda n: _chunk_of(n, ncc, nch, reverse)
    specs, args = _scan_in_specs(p, tabs, gw, gb, rdec, nch, cidx)

    def body(q, k, v, lrk, rq, rk, rv, tc, ta, tb, gw_r, gb_r, rd_r, og_r, or_r, sgo_r, sro_r, sg, sr):
        @pl.when(pl.program_id(1) == 0)
        def _():
            sg[...] = jnp.zeros_like(sg)
            sr[...] = jnp.zeros_like(sr)

        sgo_r[0] = sg[...]
        sro_r[0] = sr[...]
        ld = lambda r: r[...].astype(F32)
        og, orr, sg2, sr2 = scan_step(sg[...], sr[...], ld(q), ld(k), ld(v), ld(lrk), ld(rq), ld(rk), ld(rv),
                                      ld(gw_r), ld(gb_r), ld(rd_r), (ld(tc), ld(ta), ld(tb)), reverse)
        og_r[...] = og
        or_r[...] = orr
        sg[...] = sg2
        sr[...] = sr2

    t = p.shape[0]
    row_out = pl.BlockSpec((CHUNK, 256), lambda bb, n: (bb * nch + cidx(n), 0))
    st_out = pl.BlockSpec((1, 256, LANES), lambda bb, n: (bb * nch + n, 0, 0))
    return pl.pallas_call(
        body, name=name, grid=(nb, nch), in_specs=specs, out_specs=[row_out, row_out, st_out, st_out],
        out_shape=[jax.ShapeDtypeStruct((t, 256), F32)] * 2 + [jax.ShapeDtypeStruct((nb * nch, 256, LANES), F32)] * 2,
        scratch_shapes=[pltpu.VMEM((256, LANES), F32)] * 2,
    )(*args)


def scan_bwd(p, tabs, gw, gb, rdec, sg_in, sr_in, dog, dor, prev, nb, ncc, nch, reverse, name):
    step = lambda m: nch - 1 - m
    cidx = lambda m: _chunk_of(step(m), ncc, nch, reverse)
    specs, args = _scan_in_specs(p, tabs, gw, gb, rdec, nch, cidx)
    st_spec = pl.BlockSpec((1, 256, LANES), lambda bb, m: (bb * nch + step(m), 0, 0))
    specs += [st_spec, st_spec]
    args += [sg_in, sr_in]
    row = lambda width: pl.BlockSpec((CHUNK, width), lambda bb, m: (bb * nch + cidx(m), 0))
    specs += [row(256), row(256)]
    args += [dog, dor]
    widths = (128, 128, 256, 128, 128, 128, 256)
    if prev is not None:
        specs += [row(wd) for wd in widths]
        args += list(prev)
    n_prev = 0 if prev is None else 7
    t = p.shape[0]

    def body(*refs):
        (q, k, v, lrk, rq, rk, rv, tc, ta, tb, gw_r, gb_r, rd_r, sgi, sri, dog_r, dor_r), rest = refs[:17], refs[17:]
        prev_r, rest = rest[:n_prev], rest[n_prev:]
        outs, (dgw_r, dgb_r, drd_r, dsg, dsr) = rest[:7], rest[7:]
        first = (pl.program_id(0) == 0) & (pl.program_id(1) == 0)

        @pl.when(pl.program_id(1) == 0)
        def _():
            dsg[...] = jnp.zeros_like(dsg)
            dsr[...] = jnp.zeros_like(dsr)

        ld = lambda r: r[...].astype(F32)
        tab = (ld(tc), ld(ta), ld(tb))
        prim = (sgi[0], sri[0], ld(q), ld(k), ld(v), ld(lrk), ld(rq), ld(rk), ld(rv), ld(gw_r), ld(gb_r), ld(rd_r))
        _, vjp = jax.vjp(lambda *a: scan_step(*a, tab, reverse), *prim)
        g = vjp((ld(dog_r), ld(dor_r), dsg[...], dsr[...]))
        dsg[...] = g[0]
        dsr[...] = g[1]
        for j in range(7):
            val = g[2 + j]
            if n_prev:
                val = val + prev_r[j][...]
            outs[j][...] = val
        for ref, val in ((dgw_r, g[9]), (dgb_r, g[10]), (drd_r, g[11])):
            @pl.when(first)
            def _():
                ref[...] = val

            @pl.when(jnp.logical_not(first))
            def _():
                ref[...] += val

    wspec = lambda w: pl.BlockSpec(w.shape, lambda bb, m: (0, 0))
    res = pl.pallas_call(
        body, name=name, grid=(nb, nch), in_specs=specs,
        out_specs=[row(wd) for wd in widths] + [wspec(gw), wspec(gb), wspec(rdec)],
        out_shape=[jax.ShapeDtypeStruct((t, wd), F32) for wd in widths]
        + [jax.ShapeDtypeStruct(w.shape, F32) for w in (gw, gb, rdec)],
        scratch_shapes=[pltpu.VMEM((256, LANES), F32)] * 2,
    )(*args)
    return tuple(res[:7]), res[7], res[8], res[9]


def attend(q, k, v):
    s = mm_nt(q, k) * MLA_SCALE
    p = jnp.exp(s - jnp.max(s, axis=-1, keepdims=True))
    return mm_nn(p / jnp.sum(p, axis=-1, keepdims=True), v)


def _attn_tiles(lc, lt):
    nct = lc // ROW_TILE
    return nct, (lt - lc) // ROW_TILE


def mla_fwd(q, k, v, nb, lc, lt, name):
    nct, nlt = _attn_tiles(lc, lt)

    def body(q_ref, k_ref, v_ref, o_ref):
        def tile(r0, nk):
            rows = pl.ds(r0, ROW_TILE)
            outs = []
            for j in range(2):
                outs.append(attend(q_ref[rows, j * LANES:(j + 1) * LANES], k_ref[0:nk, j * LANES:(j + 1) * LANES],
                                   v_ref[0:nk, j * 64:(j + 1) * 64]))
            o_ref[rows, :] = jnp.concatenate(outs, axis=-1)

        for i in range(nct):
            tile(i * ROW_TILE, lc)

        def lat(i, carry):
            tile(pl.multiple_of(lc + i * ROW_TILE, ROW_TILE), lt)
            return carry

        lax.fori_loop(0, nlt, lat, 0)

    return pl.pallas_call(
        body, name=name, grid=(nb, MLA_HEADS // 2),
        in_specs=[pl.BlockSpec((lt, 2 * LANES), lambda b, h: (b, h)), pl.BlockSpec((lt, 2 * LANES), lambda b, h: (b, h)),
                  pl.BlockSpec((lt, LANES), lambda b, h: (b, h))],
        out_specs=pl.BlockSpec((lt, LANES), lambda b, h: (b, h)),
        out_shape=jax.ShapeDtypeStruct((nb * lt, MLA_HEADS * 64), F32),
    )(q, k, v)


def mla_bwd(q, k, v, do, nb, lc, lt, name):
    nct, nlt = _attn_tiles(lc, lt)

    def body(q_ref, k_ref, v_ref, do_ref, dq_ref, dk_ref, dv_ref, dka, dva):
        dka[...] = jnp.zeros_like(dka)
        dva[...] = jnp.zeros_like(dva)

        def tile(r0, nk):
            rows = pl.ds(r0, ROW_TILE)
            dqs = []
            for j in range(2):
                prim = (q_ref[rows, j * LANES:(j + 1) * LANES].astype(F32), k_ref[0:nk, j * LANES:(j + 1) * LANES].astype(F32),
                        v_ref[0:nk, j * 64:(j + 1) * 64].astype(F32))
                _, vjp = jax.vjp(attend, *prim)
                dq, dk, dv = vjp(do_ref[rows, j * 64:(j + 1) * 64])
                dqs.append(dq)
                dka[j, 0:nk, :] += dk
                dva[j, 0:nk, :] += dv
            dq_ref[rows, :] = jnp.concatenate(dqs, axis=-1)

        for i in range(nct):
            tile(i * ROW_TILE, lc)

        def lat(i, carry):
            tile(pl.multiple_of(lc + i * ROW_TILE, ROW_TILE), lt)
            return carry

        lax.fori_loop(0, nlt, lat, 0)
        dk_ref[...] = jnp.concatenate([dka[0], dka[1]], axis=-1)
        dv_ref[...] = jnp.concatenate([dva[0], dva[1]], axis=-1)

    t = nb * lt
    return pl.pallas_call(
        body, name=name, grid=(nb, MLA_HEADS // 2),
        in_specs=[pl.BlockSpec((lt, 2 * LANES), lambda b, h: (b, h)), pl.BlockSpec((lt, 2 * LANES), lambda b, h: (b, h)),
                  pl.BlockSpec((lt, LANES), lambda b, h: (b, h)), pl.BlockSpec((lt, LANES), lambda b, h: (b, h))],
        out_specs=[pl.BlockSpec((lt, 2 * LANES), lambda b, h: (b, h)), pl.BlockSpec((lt, 2 * LANES), lambda b, h: (b, h)),
                   pl.BlockSpec((lt, LANES), lambda b, h: (b, h))],
        out_shape=[jax.ShapeDtypeStruct((t, MLA_HEADS * LANES), F32), jax.ShapeDtypeStruct((t, MLA_HEADS * LANES), F32),
                   jax.ShapeDtypeStruct((t, MLA_HEADS * 64), F32)],
        scratch_shapes=[pltpu.VMEM((2, lt, LANES), F32), pltpu.VMEM((2, lt, 64), F32)],
    )(q, k, v, do)


def _gate_specs(u, per_batch, lc):
    g8 = GATE_ROWS // 8
    n8 = u.shape[0] // 8
    width = u.shape[1]
    main = pl.BlockSpec((GATE_ROWS, width), lambda i: (i, 0))
    prev = pl.BlockSpec((8, width), lambda i: (jnp.maximum(i * g8 - 1, 0), 0))
    nxt = pl.BlockSpec((8, width), lambda i: (jnp.minimum((i + 1) * g8, n8 - 1), 0))
    return main, prev, nxt


def _seg_edges(per_batch, lc):
    j = pl.program_id(0) % (per_batch // GATE_ROWS)
    first = (j == 0) | (j == lc // GATE_ROWS)
    last = (j == lc // GATE_ROWS - 1) | (j == per_batch // GATE_ROWS - 1)
    return first, last


def _shifted(x, prev_ref, next_ref, first, last):
    rows = _iota(x.shape, 0)
    before = jnp.where(first, 0.0, prev_ref[7:8, :].astype(F32))
    after = jnp.where(last, 0.0, next_ref[0:1, :].astype(F32))
    xm = jnp.where(rows == 0, before, pltpu.roll(x, 1, 0))
    xp = jnp.where(rows == x.shape[0] - 1, after, pltpu.roll(x, x.shape[0] - 1, 0))
    return xm, xp


def gate_fwd(u, cw, cb, per_batch, lc, name):
    main, prev, nxt = _gate_specs(u, per_batch, lc)
    f = u.shape[1] // 2

    def body(u_ref, p_ref, n_ref, w_ref, b_ref, act_ref):
        first, last = _seg_edges(per_batch, lc)
        x = u_ref[...]
        xm, xp = _shifted(x, p_ref, n_ref, first, last)
        c = w_ref[0:1, :] * xm + w_ref[1:2, :] * x + w_ref[2:3, :] * xp + b_ref[...]
        act_ref[...] = (silu(c[:, :f]) * c[:, f:]).astype(act_ref.dtype)

    return pl.pallas_call(
        body, name=name, grid=(u.shape[0] // GATE_ROWS,),
        in_specs=[main, prev, nxt, pl.BlockSpec(cw.shape, lambda i: (0, 0)), pl.BlockSpec(cb.shape, lambda i: (0, 0))],
        out_specs=pl.BlockSpec((GATE_ROWS, f), lambda i: (i, 0)),
        out_shape=jax.ShapeDtypeStruct((u.shape[0], f), BF),
    )(u, u, u, cw, cb)


def gate_bwd(u, cw, cb, dact, per_batch, lc, name):
    main, prev, nxt = _gate_specs(u, per_batch, lc)
    f = u.shape[1] // 2

    def body(u_ref, p_ref, n_ref, w_ref, b_ref, da_ref, dc_ref, dw_ref):
        first, last = _seg_edges(per_batch, lc)
        x = u_ref[...]
        xm, xp = _shifted(x, p_ref, n_ref, first, last)
        c = w_ref[0:1, :] * xm + w_ref[1:2, :] * x + w_ref[2:3, :] * xp + b_ref[...]
        a, g = c[:, :f], c[:, f:]
        sg = jax.nn.sigmoid(a)
        da = da_ref[...]
        dc = jnp.concatenate([da * g * (sg * (1.0 + a * (1.0 - sg))), da * (a * sg)], axis=-1)
        dc_ref[...] = dc
        part = jnp.concatenate([jnp.sum(xm * dc, axis=0, keepdims=True), jnp.sum(x * dc, axis=0, keepdims=True),
                                jnp.sum(xp * dc, axis=0, keepdims=True), jnp.sum(dc, axis=0, keepdims=True),
                                jnp.zeros((4, 2 * f), F32)], axis=0)

        @pl.when(pl.program_id(0) == 0)
        def _():
            dw_ref[...] = part

        @pl.when(pl.program_id(0) > 0)
        def _():
            dw_ref[...] += part

    return pl.pallas_call(
        body, name=name, grid=(u.shape[0] // GATE_ROWS,),
        in_specs=[main, prev, nxt, pl.BlockSpec(cw.shape, lambda i: (0, 0)), pl.BlockSpec(cb.shape, lambda i: (0, 0)),
                  pl.BlockSpec((GATE_ROWS, f), lambda i: (i, 0))],
        out_specs=[main, pl.BlockSpec((8, 2 * f), lambda i: (0, 0))],
        out_shape=[jax.ShapeDtypeStruct(u.shape, F32), jax.ShapeDtypeStruct((8, 2 * f), F32)],
    )(u, u, u, cw, cb, dact)


def conv_transpose(dc, cw, per_batch, lc, name):
    main, prev, nxt = _gate_specs(dc, per_batch, lc)

    def body(d_ref, p_ref, n_ref, w_ref, du_ref):
        first, last = _seg_edges(per_batch, lc)
        x = d_ref[...]
        xm, xp = _shifted(x, p_ref, n_ref, first, last)
        du_ref[...] = (w_ref[0:1, :] * xp + w_ref[1:2, :] * x + w_ref[2:3, :] * xm).astype(du_ref.dtype)

    return pl.pallas_call(
        body, name=name, grid=(dc.shape[0] // GATE_ROWS,),
        in_specs=[main, prev, nxt, pl.BlockSpec(cw.shape, lambda i: (0, 0))],
        out_specs=main, out_shape=jax.ShapeDtypeStruct(dc.shape, BF),
    )(dc, dc, dc, cw)


def loss_head(x, target, tiles_per_batch, ctx_tiles, name):
    n_tiles = x.shape[0] // ROW_TILE
    lat_tiles = tiles_per_batch - ctx_tiles

    def tgt_idx(i):
        j = i % tiles_per_batch
        return jnp.where(j < ctx_tiles, 0, (i // tiles_per_batch) * lat_tiles + j - ctx_tiles), 0

    def body(x_ref, t_ref, dx_ref, l_ref):
        lat = (pl.program_id(0) % tiles_per_batch >= ctx_tiles).astype(F32)
        err = (x_ref[...] - t_ref[...]) * lat
        dx_ref[...] = err * (1.0 / D_MODEL)
        l_ref[...] = jnp.full(l_ref.shape, 0.5 / D_MODEL * jnp.sum(err * err), F32)

    return pl.pallas_call(
        body, name=name, grid=(n_tiles,),
        in_specs=[pl.BlockSpec((ROW_TILE, D_MODEL), lambda i: (i, 0)), pl.BlockSpec((ROW_TILE, D_MODEL), tgt_idx)],
        out_specs=[pl.BlockSpec((ROW_TILE, D_MODEL), lambda i: (i, 0)), pl.BlockSpec((1, 8, LANES), lambda i: (i, 0, 0))],
        out_shape=[jax.ShapeDtypeStruct(x.shape, F32), jax.ShapeDtypeStruct((n_tiles, 8, LANES), F32)],
    )(x, target)


def adamw(w, g, m, v, name):
    rows, cols = w.shape
    tr = rows
    for cand in (512, 256, 128, 64, 32, 16, 8):
        if rows % cand == 0 and cand * cols * 4 <= (1 << 20):
            tr = cand
            break

    def body(w_ref, g_ref, m_ref, v_ref, d_ref, mo_ref, vo_ref):
        gg = g_ref[...]
        m2 = ADAM_B1 * m_ref[...] + (1.0 - ADAM_B1) * gg
        v2 = ADAM_B2 * v_ref[...] + (1.0 - ADAM_B2) * (gg * gg)
        m_hat = m2 / (1.0 - ADAM_B1 ** ADAM_STEP)
        v_hat = v2 / (1.0 - ADAM_B2 ** ADAM_STEP)
        d_ref[...] = -ADAM_LR * (m_hat / (jnp.sqrt(v_hat) + ADAM_EPS) + ADAM_WD * w_ref[...])
        mo_ref[...] = m2
        vo_ref[...] = v2

    spec = pl.BlockSpec((tr, cols), lambda i: (i, 0))
    return pl.pallas_call(body, name=name, grid=(rows // tr,), in_specs=[spec] * 4, out_specs=[spec] * 3,
                          out_shape=[jax.ShapeDtypeStruct(w.shape, F32)] * 3)(w, g, m, v)


def fn_post_mod(rows, tps, consts, ws):
    (x, a), (g, sc, sh), (lng, lnb) = rows, tps, ws
    y = layer_norm(ALPHA * x + g * a, lng, lnb)
    return [y, y * (1.0 + sc) + sh]


def _pick(n, cands):
    for c in cands:
        if n % c == 0:
            return c
    return n


def rope_tables(lc, l):
    pos = jnp.arange(l, dtype=F32)
    ret_inv = 1.0 / (ROPE_BASE ** jnp.linspace(0.0, 1.0, RET_DK // 2, dtype=F32))
    ang = pos[:, None] * ret_inv
    rc, rs = jnp.cos(ang), jnp.sin(ang)
    n_ax = MLA_D_ROPE // 4
    ax_inv = ROPE_BASE ** (-jnp.arange(n_ax, dtype=F32) / n_ax)
    rows_n = l // GRID_W
    rows = jnp.repeat(jnp.arange(rows_n, dtype=F32), GRID_W)
    cols = jnp.tile(jnp.arange(GRID_W, dtype=F32), rows_n)
    ra, ca = rows[:, None] * ax_inv, cols[:, None] * ax_inv
    rwc, rws, clc, cls = jnp.cos(ra), jnp.sin(ra), jnp.cos(ca), jnp.sin(ca)
    one = lambda n: jnp.ones((l, n), F32)
    zero = lambda n: jnp.zeros((l, n), F32)
    cat = lambda parts: jnp.concatenate(parts, axis=1)

    def with_ctx(tab, is_cos):
        head = jnp.ones((lc, LANES), F32) if is_cos else jnp.zeros((lc, LANES), F32)
        return jnp.concatenate([head, tab], axis=0)

    ret = (cat([rc, rc] * 4), cat([-rs, zero(16)] * 4), cat([zero(16), rs] * 4))
    ax_c = [rwc, rwc, clc, clc]
    ax_a = [-rws, zero(8), -cls, zero(8)]
    ax_b = [zero(8), rws, zero(8), cls]
    qt = (cat([one(64)] + ax_c + [one(32)]), cat([zero(64)] + ax_a + [zero(32)]), cat([zero(64)] + ax_b + [zero(32)]))
    kt = (cat([one(32)] + ax_c + [one(64)]), cat([zero(32)] + ax_a + [zero(64)]), cat([zero(32)] + ax_b + [zero(64)]))
    fix = lambda t3: tuple(with_ctx(t, k == 0) for k, t in enumerate(t3))
    return fix(ret), fix(qt), fix(kt)


_IN_ORDER = ((0, 128), (128, 256), (256, 512), (544, 800), (800, 928), (928, 1056), (1056, 1312), (1312, 1568),
             (1568, 1824), (1824, 1952), (512, 544), (1952, 1984))


def permute_w_in(w):
    parts = [w[:, a:b] for a, b in _IN_ORDER] + [jnp.zeros((w.shape[0], D_INP - D_IN), w.dtype)]
    return jnp.concatenate(parts, axis=1)


def unpermute_w_in(g):
    out, at = {}, 0
    for a, b in _IN_ORDER:
        out[a] = g[:, at:at + b - a]
        at += b - a
    return jnp.concatenate([out[a] for a in sorted(out)], axis=1)


def layer_weights(w, l):
    f = lambda a: a.astype(F32)
    r = {}
    r["win"] = permute_w_in(w["w_in"][l]).astype(BF)
    r["wout"] = w["w_out"][l].astype(BF)
    r["wup"] = w["ffn_up"][l].astype(BF)
    r["wdown"] = w["ffn_down"][l].astype(BF)
    uq = w["mla_w_uq"][l].reshape(256, MLA_HEADS, 96)
    r["wuq"] = jnp.pad(uq, ((0, 0), (0, 0), (0, 32))).reshape(256, 8 * LANES).astype(BF)
    uk = w["mla_w_uk"][l].reshape(128, MLA_HEADS, 64)
    r["wuk"] = jnp.pad(uk, ((0, 0), (0, 0), (0, 64))).reshape(128, 8 * LANES).astype(BF)
    r["wuv"] = w["mla_w_uv"][l].astype(BF)
    gw = f(w["gla_gate_w"][l])
    z16 = jnp.zeros((16, LANES), F32)
    z96 = jnp.zeros((96, LANES), F32)
    r["gw"] = (jnp.concatenate([gw[0], z16, z96], axis=0), jnp.concatenate([z16, gw[1], z96], axis=0))
    r["gb"] = tuple(f(w["gla_gate_b"][l][d]).reshape(1, LANES) for d in range(2))
    r["rdec"] = tuple(jnp.repeat(f(w["ret_decay"][l][d]), 32).reshape(1, LANES) for d in range(2))
    r["gng"] = jnp.tile(f(w["gla_norm_g"][l]), 4).reshape(1, 256)
    r["gq"] = f(w["mla_q_norm_g"][l]).reshape(1, 256)
    r["gkv"] = f(w["mla_kv_norm_g"][l]).reshape(1, 128)
    for n in ("ln1_g", "ln1_b", "ln2_g", "ln2_b"):
        r[n] = f(w[n][l]).reshape(1, D_MODEL)
    r["cw"] = f(w["ffn_conv_w"][l])
    r["cb"] = f(w["ffn_conv_b"][l]).reshape(1, 2 * D_FF)
    return r


def tile_params(mod_l, nb, nct, nlt):
    m6 = mod_l.reshape(8, 6, D_MODEL)
    out = []
    for j in range(6):
        parts = []
        for b in range(nb):
            parts.append(jnp.broadcast_to(m6[4, j], (nct, 1, D_MODEL)))
            parts.append(jnp.broadcast_to(m6[b, j], (nlt, 1, D_MODEL)))
        out.append(jnp.concatenate(parts, axis=0))
    return out


def tile_param_grads(dts, nb, nct, nlt):
    cols = []
    for dt in dts:
        d = dt.reshape(nb, nct + nlt, D_MODEL)
        lat = jnp.sum(d[:, nct:], axis=1)
        ctx = jnp.sum(d[:, :nct], axis=(0, 1))
        cols.append(jnp.concatenate([lat, jnp.zeros((4 - nb, D_MODEL), F32), ctx[None], jnp.zeros((3, D_MODEL), F32)], axis=0))
    return jnp.stack(cols, axis=1).reshape(8, 6 * D_MODEL)


def layer_forward(x, h1, tp, tp_next, lw, tabs, dims, tag):
    nb, lc, lt = dims
    t = x.shape[0]
    nbt = lt // ROW_TILE
    ncc, nch = lc // CHUNK, lt // CHUNK
    tm = _pick(t, (1024, 768, 512, 256))
    ret_tab, q_tab, k_tab = tabs
    full = lambda a: (a, a.shape[1], 0)
    p = matmul(h1, lw["win"], "nn", F32, tm, 1024, 1024, "proj_in")
    ogf, orf, sgf, srf = scan_fwd(p, ret_tab, lw["gw"][0], lw["gb"][0], lw["rdec"][0], nb, ncc, nch, False, "scan_fwd_f")
    ogb, orb, sgb, srb = scan_fwd(p, ret_tab, lw["gw"][1], lw["gb"][1], lw["rdec"][1], nb, ncc, nch, True, "scan_fwd_b")
    prep_rows = [(p, 256, C_CQ // 256), (p, 128, C_CKV // 128), (p, 128, C_LK // 128)]
    prep_consts = [(a, nbt) for a in q_tab + k_tab]
    prep_ws = [lw["gq"], lw["gkv"], lw["wuq"], lw["wuk"], lw["wuv"]]
    q, k, v = stage_fwd(fn_mla_prep, prep_rows, [], prep_consts, prep_ws, [(1024, BF), (1024, BF), (512, BF)], "mla_prep")
    mo = mla_fwd(q, k, v, nb, lc, lt, "mla_attn")
    mix_rows = [full(ogf), full(ogb), full(orf), full(orb), (p, 256, C_GG // 256), (p, 256, C_RG // 256), full(mo)]
    m, = stage_fwd(fn_mix, mix_rows, [], [], [lw["gng"]], [(1024, BF)], "mix")
    a = matmul(m, lw["wout"], "nn", F32, tm, 1024, 1024, "proj_out")
    x1, h2 = stage_fwd(fn_post_mod, [full(x), full(a)], [tp[2], tp[4], tp[3]], [], [lw["ln1_g"], lw["ln1_b"]],
                       [(1024, F32), (1024, BF)], "post1")
    u = matmul(h2, lw["wup"], "nn", F32, tm, 1408, 1024, "ffn_up")
    act = gate_fwd(u, lw["cw"], lw["cb"], lt, lc, "ffn_gate")
    f = matmul(act, lw["wdown"], "nn", F32, tm, 1024, 1408, "ffn_down")
    if tp_next is None:
        x2, = stage_fwd(fn_post, [full(x1), full(f)], [tp[5]], [], [lw["ln2_g"], lw["ln2_b"]], [(1024, F32)], "post2_last")
        h1n = None
    else:
        x2, h1n = stage_fwd(fn_post_mod, [full(x1), full(f)], [tp[5], tp_next[0], tp_next[1]], [],
                            [lw["ln2_g"], lw["ln2_b"]], [(1024, F32), (1024, BF)], "post2")
    res = dict(x=x, h1=h1, p=p, ogf=ogf, orf=orf, sgf=sgf, srf=srf, ogb=ogb, orb=orb, sgb=sgb, srb=srb, q=q, k=k, v=v,
               mo=mo, m=m, a=a, x1=x1, h2=h2, u=u, act=act, f=f, mix_rows=mix_rows, prep_rows=prep_rows,
               prep_consts=prep_consts, prep_ws=prep_ws)
    return x2, h1n, res


def layer_backward(dx2, dh1n, res, tp, tp_next, lw, tabs, dims):
    nb, lc, lt = dims
    r = res
    t = dx2.shape[0]
    ncc, nch = lc // CHUNK, lt // CHUNK
    tm = _pick(t, (1024, 768, 512, 256))
    tkr = _pick(t, (2304, 1536, 1024, 768, 512))
    ret_tab = tabs[0]
    full = lambda a: (a, a.shape[1], 0)
    g = {}
    if tp_next is None:
        (dx1a, df), (dg2,), (g["ln2_g"], g["ln2_b"]) = stage_bwd(
            fn_post, [full(r["x1"]), full(r["f"])], [tp[5]], [], [lw["ln2_g"], lw["ln2_b"]], [dx2], [F32, BF], "post2_last_bwd")
        dnext = None
    else:
        (dx1a, df), (dg2, dsc1n, dsh1n), (g["ln2_g"], g["ln2_b"]) = stage_bwd(
            fn_post_mod, [full(r["x1"]), full(r["f"])], [tp[5], tp_next[0], tp_next[1]], [], [lw["ln2_g"], lw["ln2_b"]],
            [dx2, dh1n], [F32, BF], "post2_bwd")
        dnext = (dsc1n, dsh1n)
    dact = matmul(df, lw["wdown"], "nt", F32, tm, 1408, 1024, "ffn_down_dx")
    g["ffn_down"] = matmul(r["act"], df, "tn", F32, 1408, 1024, tkr, "ffn_down_dw")
    dc, dcw = gate_bwd(r["u"], lw["cw"], lw["cb"], dact, lt, lc, "ffn_gate_bwd")
    g["ffn_conv_w"], g["ffn_conv_b"] = dcw[0:3], dcw[3]
    du = conv_transpose(dc, lw["cw"], lt, lc, "ffn_conv_t")
    dh2 = matmul(du, lw["wup"], "nt", F32, tm, 1024, 1408, "ffn_up_dx")
    g["ffn_up"] = matmul(r["h2"], du, "tn", F32, 1024, 1408, tkr, "ffn_up_dw")
    (dxa, da), (dg1, dsc2, dsh2), (g["ln1_g"], g["ln1_b"]) = stage_bwd(
        fn_post_mod, [full(r["x"]), full(r["a"])], [tp[2], tp[4], tp[3]], [], [lw["ln1_g"], lw["ln1_b"]],
        [dx1a, dh2], [F32, BF], "post1_bwd")
    dm = matmul(da, lw["wout"], "nt", F32, tm, 1024, 1024, "proj_out_dx")
    g["w_out"] = matmul(r["m"], da, "tn", F32, 1024, 1024, tkr, "proj_out_dw")
    (dog, _, dor, _, dpg, dpr, dmo), _, (dgng,) = stage_bwd(
        fn_mix, r["mix_rows"], [], [], [lw["gng"]], [dm], [F32, None, F32, None, F32, F32, F32], "mix_bwd")
    g["gla_norm_g"] = jnp.sum(dgng.reshape(4, 64), axis=0)
    dq, dk, dv = mla_bwd(r["q"], r["k"], r["v"], dmo, nb, lc, lt, "mla_attn_bwd")
    (dpq, dpkv, dplk), _, (dgq, dgkv, dwuq, dwuk, dwuv) = stage_bwd(
        fn_mla_prep, r["prep_rows"], [], r["prep_consts"], r["prep_ws"], [dq, dk, dv], [F32, F32, F32], "mla_prep_bwd")
    g["mla_q_norm_g"], g["mla_kv_norm_g"] = dgq.reshape(256), dgkv.reshape(128)
    g["mla_w_uq"] = dwuq.reshape(256, MLA_HEADS, LANES)[:, :, :96].reshape(256, MLA_HEADS * 96)
    g["mla_w_uk"] = dwuk.reshape(128, MLA_HEADS, LANES)[:, :, :64].reshape(128, MLA_HEADS * 64)
    g["mla_w_uv"] = dwuv
    s7, dgw0, dgb0, drd0 = scan_bwd(r["p"], ret_tab, lw["gw"][0], lw["gb"][0], lw["rdec"][0], r["sgf"], r["srf"], dog, dor,
                                    None, nb, ncc, nch, False, "scan_bwd_f")
    s7, dgw1, dgb1, drd1 = scan_bwd(r["p"], ret_tab, lw["gw"][1], lw["gb"][1], lw["rdec"][1], r["sgb"], r["srb"], dog, dor,
                                    s7, nb, ncc, nch, True, "scan_bwd_b")
    g["gla_gate_w"] = jnp.stack([dgw0[0:16], dgw1[16:32]])
    g["gla_gate_b"] = jnp.stack([dgb0[0], dgb1[0]])
    g["ret_decay"] = jnp.stack([jnp.sum(drd0.reshape(4, 32), axis=1), jnp.sum(drd1.reshape(4, 32), axis=1)])
    gq_, gk_, gv_, glrk, rq_, rk_, rv_ = s7
    pieces = [gq_, gk_, gv_, dpg, rq_, rk_, rv_, dpr, dpq, dpkv, glrk, dplk]
    dp, = stage_fwd(fn_assemble, [full(a) for a in pieces], [], [], [], [(D_INP, BF)], "dproj_assemble")
    dh1 = matmul(dp, lw["win"], "nt", F32, tm, 1024, 1024, "proj_in_dx")
    g["w_in"] = unpermute_w_in(matmul(r["h1"], dp, "tn", F32, 1024, 1024, tkr, "proj_in_dw"))
    for n in ("ln1_g", "ln1_b", "ln2_g", "ln2_b"):
        g[n] = g[n].reshape(D_MODEL)
    dtp = [None, None, dg1, dsh2, dsc2, dg2]
    return dxa, dh1, dtp, dnext, g


def local_step(x, c, ctx, c_ctx, w, loss_target):
    nb, l, _ = x.shape
    lc = ctx.shape[1]
    lt = lc + l
    dims = (nb, lc, lt)
    nct, nlt = lc // ROW_TILE, l // ROW_TILE
    tabs = rope_tables(lc, l)
    x0 = jnp.concatenate([ctx, x], axis=1).reshape(nb * lt, D_MODEL)
    s8 = jnp.concatenate([silu(c), jnp.zeros((4 - nb, D_MODEL), F32), silu(c_ctx)[None], jnp.zeros((3, D_MODEL), F32)], axis=0)
    lws, tps = [], []
    for i in range(DEPTH):
        mod = matmul(s8, w["ada_w"][i].astype(BF), "nn", F32, 8, 1536, 1024, "ada_mod") + w["ada_b"][i].astype(F32)[None]
        tps.append(tile_params(mod, nb, nct, nlt))
        lws.append(layer_weights(w, i))
    h1, = stage_fwd(fn_modulate, [(x0, D_MODEL, 0)], [tps[0][1], tps[0][0]], [], [], [(D_MODEL, BF)], "mod_in")
    xs, ress = x0, []
    for i in range(DEPTH):
        tpn = None if i == DEPTH - 1 else (tps[i + 1][1], tps[i + 1][0])
        xs, h1, res = layer_forward(xs, h1, tps[i], tpn, lws[i], tabs, dims, i)
        ress.append(res)
    dx, lparts = loss_head(xs, loss_target.reshape(nb * l, D_MODEL), nct + nlt, nct, "loss_head")
    loss = jnp.sum(lparts[:, 0, 0])
    grads = [None] * DEPTH
    dtps = [None] * DEPTH
    dh1 = None
    for i in reversed(range(DEPTH)):
        tpn = None if i == DEPTH - 1 else (tps[i + 1][1], tps[i + 1][0])
        dx, dh1, dtp, dn, grads[i] = layer_backward(dx, dh1, ress[i], tps[i], tpn, lws[i], tabs, dims)
        if dn is not None:
            dtps[i + 1][1], dtps[i + 1][0] = dn
        dtps[i] = dtp
    (dx0b,), (dsc1, dsh1), _ = stage_bwd(fn_modulate, [(x0, D_MODEL, 0)], [tps[0][1], tps[0][0]], [], [], [dh1], [F32], "mod_in_bwd")
    dtps[0][1], dtps[0][0] = dsc1, dsh1
    grad_x = (dx + dx0b).reshape(nb, lt, D_MODEL)[:, lc:]
    dmod = jnp.stack([tile_param_grads(d, nb, nct, nlt) for d in dtps])
    gw = {n: jnp.stack([grads[i][n] for i in range(DEPTH)]) for n in grads[0]}
    return loss, grad_x, gw, dmod, s8


MESH_IDS = pl.DeviceIdType.MESH
ANY = pl.BlockSpec(memory_space=pl.ANY)


def _place():
    return lax.axis_index("x"), lax.axis_index("y"), lax.axis_index("c")


def _dma_sems(n, per):
    return pltpu.SemaphoreType.DMA((n, per))


def all_gather8(blocks, name):
    n = len(blocks)

    def body(*refs):
        x_refs, out_refs, (send_sems, recv_sems, local_sems) = refs[:n], refs[n:2 * n], refs[2 * n:]
        x, y, c = _place()
        me, sibling = (x, y, c), (x, y, 1 - c)
        chips = [(1 - x, y), (x, 1 - y), (1 - x, 1 - y)]

        def copy(a, k, blk, to, own=False):
            slot = out_refs[a].at[4 * blk[0] + 2 * blk[1] + blk[2]]
            return pltpu.make_async_remote_copy(
                src_ref=x_refs[a] if own else slot, dst_ref=slot,
                send_sem=send_sems.at[a, k], recv_sem=recv_sems.at[a, k], device_id=to, device_id_type=MESH_IDS)

        mine = [pltpu.make_async_copy(x_refs[a], out_refs[a].at[4 * x + 2 * y + c], local_sems.at[a, 0]) for a in range(n)]
        first = []
        for a in range(n):
            mine[a].start()
            first.append(copy(a, 0, me, sibling, own=True))
            first += [copy(a, 1 + j, me, (*chip, c), own=True) for j, chip in enumerate(chips)]
        for cp in first:
            cp.start()
        passed = []
        for j, chip in enumerate(chips):
            for a in range(n):
                copy(a, 1 + j, (*chip, c), me).wait_recv()
                passed.append(copy(a, 4 + j, (*chip, c), sibling))
                passed[-1].start()
        for a in range(n):
            copy(a, 0, sibling, me).wait_recv()
            for j, chip in enumerate(chips):
                copy(a, 4 + j, (*chip, 1 - c), me).wait_recv()
        for cp in first + passed:
            cp.wait_send()
        for cp in mine:
            cp.wait()

    return pl.pallas_call(
        body, name=name, out_shape=[jax.ShapeDtypeStruct((8,) + b.shape, b.dtype) for b in blocks],
        in_specs=[ANY] * n, out_specs=[ANY] * n,
        scratch_shapes=[_dma_sems(n, 7), _dma_sems(n, 7), _dma_sems(n, 1)],
    )(*blocks)


def swap_cores(blocks, name):
    n = len(blocks)

    def body(*refs):
        x_refs, out_refs, (send_sems, recv_sems) = refs[:n], refs[n:2 * n], refs[2 * n:]
        x, y, c = _place()
        cps = [pltpu.make_async_remote_copy(src_ref=x_refs[a], dst_ref=out_refs[a], send_sem=send_sems.at[a, 0],
                                            recv_sem=recv_sems.at[a, 0], device_id=(x, y, 1 - c), device_id_type=MESH_IDS)
               for a in range(n)]
        for cp in cps:
            cp.start()
        for cp in cps:
            cp.wait()

    return pl.pallas_call(
        body, name=name, out_shape=[jax.ShapeDtypeStruct(b.shape, b.dtype) for b in blocks],
        in_specs=[ANY] * n, out_specs=[ANY] * n, scratch_shapes=[_dma_sems(n, 1), _dma_sems(n, 1)],
    )(*blocks)


def gather_cores(blocks, name):
    n = len(blocks)

    def body(*refs):
        x_refs, out_refs, (send_sems, recv_sems, local_sems) = refs[:n], refs[n:2 * n], refs[2 * n:]
        x, y, c = _place()
        mine = [pltpu.make_async_copy(x_refs[a], out_refs[a].at[c], local_sems.at[a, 0]) for a in range(n)]
        cps = [pltpu.make_async_remote_copy(src_ref=x_refs[a], dst_ref=out_refs[a].at[c], send_sem=send_sems.at[a, 0],
                                            recv_sem=recv_sems.at[a, 0], device_id=(x, y, 1 - c), device_id_type=MESH_IDS)
               for a in range(n)]
        for cp in mine + cps:
            cp.start()
        for a in range(n):
            pltpu.make_async_remote_copy(src_ref=x_refs[a], dst_ref=out_refs[a].at[1 - c], send_sem=send_sems.at[a, 0],
                                         recv_sem=recv_sems.at[a, 0], device_id=(x, y, 1 - c),
                                         device_id_type=MESH_IDS).wait_recv()
        for cp in cps:
            cp.wait_send()
        for cp in mine:
            cp.wait()

    return pl.pallas_call(
        body, name=name, out_shape=[jax.ShapeDtypeStruct((2,) + b.shape, b.dtype) for b in blocks],
        in_specs=[ANY] * n, out_specs=[ANY] * n, scratch_shapes=[_dma_sems(n, 1), _dma_sems(n, 1), _dma_sems(n, 1)],
    )(*blocks)


def exchange_chips(parts, name):
    n = len(parts)

    def body(*refs):
        p_refs, out_refs, (send_sems, recv_sems, local_sems) = refs[:n], refs[n:2 * n], refs[2 * n:]
        x, y, c = _place()
        jm = 2 * x + y
        chips = [(1 - x, y), (x, 1 - y), (1 - x, 1 - y)]
        mine = [pltpu.make_async_copy(p_refs[a].at[jm], out_refs[a].at[jm], local_sems.at[a, 0]) for a in range(n)]
        cps = []
        for k, (px, py) in enumerate(chips):
            for a in range(n):
                cps.append(pltpu.make_async_remote_copy(
                    src_ref=p_refs[a].at[2 * px + py], dst_ref=out_refs[a].at[jm], send_sem=send_sems.at[a, k],
                    recv_sem=recv_sems.at[a, k], device_id=(px, py, c), device_id_type=MESH_IDS))
        for cp in mine + cps:
            cp.start()
        for k, (px, py) in enumerate(chips):
            for a in range(n):
                pltpu.make_async_remote_copy(
                    src_ref=p_refs[a].at[jm], dst_ref=out_refs[a].at[2 * px + py], send_sem=send_sems.at[a, k],
                    recv_sem=recv_sems.at[a, k], device_id=(px, py, c), device_id_type=MESH_IDS).wait_recv()
        for cp in cps:
            cp.wait_send()
        for cp in mine:
            cp.wait()

    return pl.pallas_call(
        body, name=name, out_shape=[jax.ShapeDtypeStruct(p.shape, p.dtype) for p in parts],
        in_specs=[ANY] * n, out_specs=[ANY] * n, scratch_shapes=[_dma_sems(n, 3), _dma_sems(n, 3), _dma_sems(n, 1)],
    )(*parts)


def _row_tile(rows, cols, itemsize=4, limit=1 << 21):
    for cand in (2048, 1024, 512, 256, 128, 64, 32, 16):
        if rows % cand == 0 and cand * cols * itemsize <= limit:
            return cand
    return rows


def add_halves(a, b, kind, name):
    if kind == "col":
        _, k, n = a.shape
        n4 = n // 4
        tr = _row_tile(k, n4)
        in_spec = pl.BlockSpec((1, tr, n4), lambda j, h, i: (h, i, j))
        out_spec = pl.BlockSpec((1, 1, tr, n4), lambda j, h, i: (j, h, i, 0))
        grid, out_shape = (4, 2, k // tr), (4, 2, k, n4)
    elif kind == "row":
        _, k, n = a.shape
        k4 = k // 4
        tr = _row_tile(k4, n)
        nt = k4 // tr
        in_spec = pl.BlockSpec((1, tr, n), lambda j, h, i: (h, j * nt + i, 0))
        out_spec = pl.BlockSpec((1, 1, tr, n), lambda j, h, i: (j, h, i, 0))
        grid, out_shape = (4, 2, nt), (4, 2, k4, n)
    else:
        _, _, k, n = a.shape
        tr = _row_tile(k, n)
        in_spec = out_spec = pl.BlockSpec((1, 1, tr, n), lambda j, h, i: (j, h, i, 0))
        grid, out_shape = (4, 2, k // tr), a.shape

    def body(a_ref, b_ref, s_ref):
        s_ref[...] = (a_ref[...] + b_ref[...].astype(F32)).astype(BF).reshape(s_ref.shape)

    return pl.pallas_call(body, name=name, grid=grid, in_specs=[in_spec, in_spec], out_specs=out_spec,
                          out_shape=jax.ShapeDtypeStruct(out_shape, BF))(a, b)


def sum_slots(a, name):
    s, m, k, n = a.shape
    tr = _row_tile(k, n, limit=(1 << 22) // s)

    def body(a_ref, o_ref):
        acc = a_ref[0].astype(F32)
        for j in range(1, s):
            acc = acc + a_ref[j].astype(F32)
        o_ref[...] = acc

    return pl.pallas_call(body, name=name, grid=(m, k // tr), in_specs=[pl.BlockSpec((s, 1, tr, n), lambda h, i: (0, h, i, 0))],
                          out_specs=pl.BlockSpec((1, tr, n), lambda h, i: (h, i, 0)),
                          out_shape=jax.ShapeDtypeStruct((m, k, n), F32))(a)


def sum_small(arrays, name):
    n = len(arrays)

    def body(*refs):
        for a_ref, o_ref in zip(refs[:n], refs[n:]):
            acc = a_ref[0]
            for j in range(1, 8):
                acc = acc + a_ref[j]
            o_ref[...] = acc

    return pl.pallas_call(body, name=name, out_shape=[jax.ShapeDtypeStruct(a.shape[1:], F32) for a in arrays])(*arrays)


COL_SHARDED = ("ada_w", "w_in", "mla_w_uq", "mla_w_uk", "mla_w_uv", "ffn_up", "ffn_conv_w")
ROW_SHARDED = ("w_out", "ffn_down")
GATHERED = ("ada_w", "w_in", "mla_w_uq", "mla_w_uk", "mla_w_uv", "w_out", "ffn_up", "ffn_down", "ffn_conv_w")
REDUCED = ("w_in", "mla_w_uq", "mla_w_uk", "mla_w_uv", "w_out", "ffn_up", "ffn_down")
SMALL = ("ada_b", "gla_gate_w", "gla_gate_b", "gla_norm_g", "ret_decay", "mla_q_norm_g", "mla_kv_norm_g",
         "ln1_g", "ln1_b", "ffn_conv_b", "ln2_g", "ln2_b")
WEIGHTS = ("c_ctx", "ada_w", "ada_b", "w_in", "gla_gate_w", "gla_gate_b", "gla_norm_g", "ret_decay", "mla_q_norm_g",
           "mla_kv_norm_g", "mla_w_uq", "mla_w_uk", "mla_w_uv", "w_out", "ln1_g", "ln1_b", "ffn_up", "ffn_conv_w",
           "ffn_conv_b", "ffn_down", "ln2_g", "ln2_b")
PACK = 16 * LANES
HALF_LAYERS = DEPTH // 2


def _pad_flat(v, n):
    return jnp.concatenate([v, jnp.zeros((n - v.shape[0],), v.dtype)]) if n > v.shape[0] else v


def _my_layers(a, c):
    return lax.dynamic_slice_in_dim(a, HALF_LAYERS * c, HALF_LAYERS, axis=0)


def gather_weights(shards, c):
    blocks = [_my_layers(shards[n], c) if n == "ffn_conv_w" else _my_layers(shards[n], c).astype(BF) for n in GATHERED]
    got = all_gather8(blocks, "gather_weights")
    out = {}
    for n, g in zip(GATHERED, got):
        _, _, k, n4 = g.shape
        g = g.reshape(4, 2, HALF_LAYERS, k, n4)
        if n in ROW_SHARDED:
            out[n] = jnp.transpose(g, (1, 2, 0, 3, 4)).reshape(DEPTH, 4 * k, n4)
        else:
            out[n] = jnp.transpose(g, (1, 2, 3, 0, 4)).reshape(DEPTH, k, 4 * n4)
    return out


def reduce_gradients(gw, c):
    keep, give, kinds = [], [], []
    for n in REDUCED:
        g = gw[n]
        kind = "row" if n in ROW_SHARDED else ("col" if (g.shape[2] // 4) % LANES == 0 else "pre")
        if kind == "pre":
            g = jnp.transpose(g.reshape(DEPTH, g.shape[1], 4, g.shape[2] // 4), (2, 0, 1, 3))
            keep.append(lax.dynamic_slice_in_dim(g, HALF_LAYERS * c, HALF_LAYERS, axis=1))
            give.append(lax.dynamic_slice_in_dim(g, HALF_LAYERS * (1 - c), HALF_LAYERS, axis=1).astype(BF))
        else:
            keep.append(_my_layers(g, c))
            give.append(_my_layers(g, 1 - c).astype(BF))
        kinds.append(kind)
    got = swap_cores(give, "grad_swap_cores")
    parts = [add_halves(a, b, kind, "grad_add_cores_" + n) for n, a, b, kind in zip(REDUCED, keep, got, kinds)]
    landed = exchange_chips(parts, "grad_exchange_chips")
    mine = [sum_slots(a, "grad_sum_chips_" + n) for n, a in zip(REDUCED, landed)]
    both = gather_cores(mine, "grad_gather_cores")
    return {n: b.reshape((DEPTH,) + b.shape[2:]) for n, b in zip(REDUCED, both)}


def _pack_small(d, names):
    flat = jnp.concatenate([d[n].astype(F32).reshape(-1) for n in names])
    n = -(-flat.shape[0] // PACK) * PACK
    return _pad_flat(flat, n).reshape(n // LANES, LANES)


def _unpack_small(buf, like, names):
    flat, out, at = buf.reshape(-1), {}, 0
    for n in names:
        sz = like[n].size
        out[n] = flat[at:at + sz].reshape(like[n].shape)
        at += sz
    return out


def kernel(x, c, ctx, c_ctx, ada_w, ada_b, w_in, gla_gate_w, gla_gate_b, gla_norm_g, ret_decay, mla_q_norm_g, mla_kv_norm_g, mla_w_uq, mla_w_uk, mla_w_uv, w_out, ln1_g, ln1_b, ffn_up, ffn_conv_w, ffn_conv_b, ffn_down, ln2_g, ln2_b, loss_target, m_c_ctx, m_ada_w, m_ada_b, m_w_in, m_gla_gate_w, m_gla_gate_b, m_gla_norm_g, m_ret_decay, m_mla_q_norm_g, m_mla_kv_norm_g, m_mla_w_uq, m_mla_w_uk, m_mla_w_uv, m_w_out, m_ln1_g, m_ln1_b, m_ffn_up, m_ffn_conv_w, m_ffn_conv_b, m_ffn_down, m_ln2_g, m_ln2_b, v_c_ctx, v_ada_w, v_ada_b, v_w_in, v_gla_gate_w, v_gla_gate_b, v_gla_norm_g, v_ret_decay, v_mla_q_norm_g, v_mla_kv_norm_g, v_mla_w_uq, v_mla_w_uk, v_mla_w_uv, v_w_out, v_ln1_g, v_ln1_b, v_ffn_up, v_ffn_conv_w, v_ffn_conv_b, v_ffn_down, v_ln2_g, v_ln2_b):
    w = dict(c_ctx=c_ctx, ada_w=ada_w, ada_b=ada_b, w_in=w_in, gla_gate_w=gla_gate_w, gla_gate_b=gla_gate_b, gla_norm_g=gla_norm_g, ret_decay=ret_decay, mla_q_norm_g=mla_q_norm_g, mla_kv_norm_g=mla_kv_norm_g, mla_w_uq=mla_w_uq, mla_w_uk=mla_w_uk, mla_w_uv=mla_w_uv, w_out=w_out, ln1_g=ln1_g, ln1_b=ln1_b, ffn_up=ffn_up, ffn_conv_w=ffn_conv_w, ffn_conv_b=ffn_conv_b, ffn_down=ffn_down, ln2_g=ln2_g, ln2_b=ln2_b)
    m = dict(c_ctx=m_c_ctx, ada_w=m_ada_w, ada_b=m_ada_b, w_in=m_w_in, gla_gate_w=m_gla_gate_w, gla_gate_b=m_gla_gate_b, gla_norm_g=m_gla_norm_g, ret_decay=m_ret_decay, mla_q_norm_g=m_mla_q_norm_g, mla_kv_norm_g=m_mla_kv_norm_g, mla_w_uq=m_mla_w_uq, mla_w_uk=m_mla_w_uk, mla_w_uv=m_mla_w_uv, w_out=m_w_out, ln1_g=m_ln1_g, ln1_b=m_ln1_b, ffn_up=m_ffn_up, ffn_conv_w=m_ffn_conv_w, ffn_conv_b=m_ffn_conv_b, ffn_down=m_ffn_down, ln2_g=m_ln2_g, ln2_b=m_ln2_b)
    v = dict(c_ctx=v_c_ctx, ada_w=v_ada_w, ada_b=v_ada_b, w_in=v_w_in, gla_gate_w=v_gla_gate_w, gla_gate_b=v_gla_gate_b, gla_norm_g=v_gla_norm_g, ret_decay=v_ret_decay, mla_q_norm_g=v_mla_q_norm_g, mla_kv_norm_g=v_mla_kv_norm_g, mla_w_uq=v_mla_w_uq, mla_w_uk=v_mla_w_uk, mla_w_uv=v_mla_w_uv, w_out=v_w_out, ln1_g=v_ln1_g, ln1_b=v_ln1_b, ffn_up=v_ffn_up, ffn_conv_w=v_ffn_conv_w, ffn_conv_b=v_ffn_conv_b, ffn_down=v_ffn_down, ln2_g=v_ln2_g, ln2_b=v_ln2_b)
    xi, yi, ci = _place()
    chip = 2 * xi + yi

    full = dict(w)
    full.update(gather_weights({n: w[n] for n in GATHERED}, ci))
    loss, grad_x, gw, dmod, s8 = local_step(x, c, ctx, c_ctx, {n: full[n] for n in WEIGHTS if n != "c_ctx"}, loss_target)
    loss = lax.psum(loss, ("x", "y", "c"))

    dsil = jnp.zeros((8, D_MODEL), F32)
    for i in range(DEPTH):
        dsil = dsil + matmul(dmod[i], full["ada_w"][i], "nt", F32, 8, 1024, 1536, "ada_dsilu")

    grads = reduce_gradients(gw, ci)

    small = {n: gw[n] for n in SMALL if n != "ada_b"}
    small.update(dsil=dsil[4])
    names = tuple(small)
    conv_g = gw["ffn_conv_w"].reshape(DEPTH * 3, 2 * D_FF)
    ev_small, ev_dmod, ev_s8, ev_conv = all_gather8(
        [_pack_small(small, names), dmod.reshape(DEPTH * 8, 6 * D_MODEL), s8, conv_g], "gather_small")
    sm_small, sm_dmod, sm_conv = sum_small([ev_small, ev_dmod, ev_conv], "sum_small")
    summed = _unpack_small(sm_small, small, names)
    for n in SMALL:
        if n != "ada_b":
            grads[n] = summed[n]
    grads["ada_b"] = jnp.sum(sm_dmod.reshape(DEPTH, 8, 6 * D_MODEL)[:, :5], axis=1)
    sg = jax.nn.sigmoid(c_ctx)
    grads["c_ctx"] = summed["dsil"] * (sg * (1.0 + c_ctx * (1.0 - sg)))
    ccols = ffn_conv_w.shape[2]
    grads["ffn_conv_w"] = lax.dynamic_slice_in_dim(sm_conv.reshape(DEPTH, 3, 2 * D_FF), chip * ccols, ccols, axis=2)
    s_all = ev_s8.reshape(64, D_MODEL)
    d_all = jnp.transpose(ev_dmod.reshape(8, DEPTH, 8, 6 * D_MODEL), (1, 0, 2, 3)).reshape(DEPTH, 64, 6 * D_MODEL)
    cols = ada_w.shape[2]
    g_ada = []
    for i in range(DEPTH):
        d_mine = lax.dynamic_slice_in_dim(d_all[i], chip * cols, cols, axis=1)
        g_ada.append(matmul(s_all, d_mine, "tn", F32, 1024, cols, 64, "ada_dw"))
    grads["ada_w"] = jnp.stack(g_ada)

    delta, new_m, new_v = {}, {}, {}
    for n in GATHERED:
        shp = w[n].shape
        v2 = lambda a: a.reshape(-1, shp[-1])
        d_, m_, v_ = adamw(v2(w[n]), v2(grads[n]), v2(m[n]), v2(v[n]), "adamw_" + n)
        delta[n], new_m[n], new_v[n] = d_.reshape(shp), m_.reshape(shp), v_.reshape(shp)
    rep = tuple(n for n in WEIGHTS if n not in GATHERED)
    pk = lambda d: _pack_small({n: d[n] for n in rep}, rep)
    d_, m_, v_ = adamw(pk(w), pk(grads), pk(m), pk(v), "adamw_small")
    like = {n: w[n] for n in rep}
    delta.update(_unpack_small(d_, like, rep))
    new_m.update(_unpack_small(m_, like, rep))
    new_v.update(_unpack_small(v_, like, rep))
    grads = {n: grads[n].reshape(w[n].shape) for n in WEIGHTS}
    return (loss, grad_x, *[grads[n] for n in WEIGHTS], *[delta[n] for n in WEIGHTS], *[new_m[n] for n in WEIGHTS],
            *[new_v[n] for n in WEIGHTS])
```

```python
import functools

import jax
import jax.numpy as jnp
from jax import lax
from jax.experimental import pallas as pl
from jax.experimental.pallas import tpu as pltpu

F32 = jnp.float32
BF = jnp.bfloat16

D_MODEL = 1024
DEPTH = 4
GRID_W = 64
GLA_DK = 32
GLA_TAU = 16.0
RET_DK = 32
MLA_HEADS = 8
MLA_D_NOPE = 64
MLA_D_ROPE = 32
MLA_SCALE = (MLA_D_NOPE + MLA_D_ROPE) ** -0.5
D_FF = 2816
ROPE_BASE = 10000.0
EPS = 1e-6
ALPHA = (2 * DEPTH) ** 0.25
ADAM_LR, ADAM_B1, ADAM_B2, ADAM_EPS, ADAM_WD, ADAM_STEP = 0.001, 0.9, 0.999, 1e-08, 0.01, 10

ROW_TILE = 256
CHUNK = 64
GATE_ROWS = 64
LANES = 128

C_GQ, C_GK, C_GV, C_GG, C_RQ, C_RK, C_RV, C_RG, C_CQ, C_CKV, C_LK = 0, 128, 256, 512, 768, 896, 1024, 1280, 1536, 1792, 1920
D_INP = 2048
D_IN = 1984


def _dg(a, b, ca, cb):
    return lax.dot_general(a.astype(BF), b.astype(BF), (((ca,), (cb,)), ((), ())), preferred_element_type=F32)


@jax.custom_vjp
def mm_nn(a, b):
    return _dg(a, b, 1, 0)


@jax.custom_vjp
def mm_nt(a, b):
    return _dg(a, b, 1, 1)


@jax.custom_vjp
def mm_tn(a, b):
    return _dg(a, b, 0, 0)


mm_nn.defvjp(lambda a, b: (_dg(a, b, 1, 0), (a, b)),
             lambda r, g: (mm_nt(g, r[1]).astype(r[0].dtype), mm_tn(r[0], g).astype(r[1].dtype)))
mm_nt.defvjp(lambda a, b: (_dg(a, b, 1, 1), (a, b)),
             lambda r, g: (mm_nn(g, r[1]).astype(r[0].dtype), mm_tn(g, r[0]).astype(r[1].dtype)))
mm_tn.defvjp(lambda a, b: (_dg(a, b, 0, 0), (a, b)),
             lambda r, g: (mm_nt(r[1], g).astype(r[0].dtype), mm_nn(r[0], g).astype(r[1].dtype)))


def _split3(x):
    h = x.astype(BF)
    r = x - h.astype(F32)
    m = r.astype(BF)
    lo = (r - m.astype(F32)).astype(BF)
    return h, m, lo


def _exact(x, mat, left):
    h, m, lo = _split3(x)
    if left:
        d = lambda t: lax.dot_general(mat, t, (((1,), (0,)), ((), ())), preferred_element_type=F32)
    else:
        d = lambda t: lax.dot_general(t, mat, (((1,), (0,)), ((), ())), preferred_element_type=F32)
    return (d(lo) + d(m)) + d(h)


def _iota(shape, axis):
    return lax.broadcasted_iota(jnp.int32, shape, axis)


def _tri(n, upper):
    r, c = _iota((n, n), 0), _iota((n, n), 1)
    return jnp.where((c >= r) if upper else (r >= c), 1.0, 0.0).astype(BF)


@functools.partial(jax.custom_vjp, nondiff_argnums=(1,))
def cumsum_rows(x, upper):
    return _exact(x, _tri(x.shape[0], upper), True)


cumsum_rows.defvjp(lambda x, upper: (cumsum_rows(x, upper), None),
                   lambda upper, r, g: (cumsum_rows(g, not upper),))


def _seg(n, w):
    shift = w.bit_length() - 1
    r, c = _iota((n, n), 0), _iota((n, n), 1)
    return jnp.where(lax.shift_right_logical(r, shift) == lax.shift_right_logical(c, shift), 1.0, 0.0).astype(BF)


@functools.partial(jax.custom_vjp, nondiff_argnums=(1,))
def seg_sum(x, w):
    return _exact(x, _seg(x.shape[1], w), False)


seg_sum.defvjp(lambda x, w: (seg_sum(x, w), None), lambda w, r, g: (seg_sum(g, w),))


def _place_mat(transpose):
    shape = (8 * LANES, LANES) if transpose else (LANES, 8 * LANES)
    r, c = _iota(shape, 0), _iota(shape, 1)
    src, dst = (c, r) if transpose else (r, c)
    dl = jnp.bitwise_and(dst, LANES - 1)
    ok = (dl >= 64) & (dl < 96) & (src == dl - 32)
    return jnp.where(ok, 1.0, 0.0).astype(BF)


@jax.custom_vjp
def place_kr(x):
    return _exact(x, _place_mat(False), False)


place_kr.defvjp(lambda x: (place_kr(x), None), lambda r, g: (_exact(g, _place_mat(True), False),))


@functools.partial(jax.custom_vjp, nondiff_argnums=(1,))
def lane_roll(x, s):
    return pltpu.roll(x, s, 1)


lane_roll.defvjp(lambda x, s: (pltpu.roll(x, s, 1), None),
                 lambda s, r, g: (pltpu.roll(g, (g.shape[1] - s) % g.shape[1], 1),))


def rope(x, tab, d):
    cos, sa, sb = tab
    return x * cos + lane_roll(x, LANES - d) * sa + lane_roll(x, d) * sb


def silu(x):
    return x * jax.nn.sigmoid(x)


def log_sigmoid(z):
    return jnp.minimum(z, 0.0) - jnp.log(1.0 + jnp.exp(-jnp.abs(z)))


def layer_norm(x, g, b):
    mu = jnp.mean(x, axis=-1, keepdims=True)
    xc = x - mu
    var = jnp.mean(xc * xc, axis=-1, keepdims=True)
    return xc * lax.rsqrt(var + EPS) * g + b


def matmul(a, b, mode, out_dtype, tm, tn, tk, name):
    if mode == "nn":
        (m, k), (k2, n) = a.shape, b.shape
        a_spec = pl.BlockSpec((tm, tk), lambda i, j, kk: (i, kk))
        b_spec = pl.BlockSpec((tk, tn), lambda i, j, kk: (kk, j))
        ca, cb = 1, 0
    elif mode == "nt":
        (m, k), (n, k2) = a.shape, b.shape
        a_spec = pl.BlockSpec((tm, tk), lambda i, j, kk: (i, kk))
        b_spec = pl.BlockSpec((tn, tk), lambda i, j, kk: (j, kk))
        ca, cb = 1, 1
    else:
        (k, m), (k2, n) = a.shape, b.shape
        a_spec = pl.BlockSpec((tk, tm), lambda i, j, kk: (kk, i))
        b_spec = pl.BlockSpec((tk, tn), lambda i, j, kk: (kk, j))
        ca, cb = 0, 0
    assert k == k2 and m % tm == 0 and n % tn == 0 and k % tk == 0, (name, a.shape, b.shape, tm, tn, tk)
    nk = k // tk

    def body(a_ref, b_ref, o_ref, *acc):
        part = _dg(a_ref[...], b_ref[...], ca, cb)
        if nk == 1:
            o_ref[...] = part.astype(o_ref.dtype)
            return
        acc_ref, = acc
        kk = pl.program_id(2)

        @pl.when(kk == 0)
        def _():
            acc_ref[...] = part

        @pl.when(kk > 0)
        def _():
            acc_ref[...] += part

        @pl.when(kk == nk - 1)
        def _():
            o_ref[...] = acc_ref[...].astype(o_ref.dtype)

    return pl.pallas_call(
        body, name=name, grid=(m // tm, n // tn, nk),
        in_specs=[a_spec, b_spec], out_specs=pl.BlockSpec((tm, tn), lambda i, j, kk: (i, j)),
        out_shape=jax.ShapeDtypeStruct((m, n), out_dtype),
        scratch_shapes=[] if nk == 1 else [pltpu.VMEM((tm, tn), F32)],
    )(a, b)


def _stage_specs(rows, tps, consts, ws):
    specs, args = [], []
    for arr, width, cb in rows:
        specs.append(pl.BlockSpec((ROW_TILE, width), functools.partial(lambda i, cb: (i, cb), cb=cb)))
        args.append(arr)
    for arr in tps:
        specs.append(pl.BlockSpec((1, 1, arr.shape[2]), lambda i: (i, 0, 0)))
        args.append(arr)
    for arr, period in consts:
        specs.append(pl.BlockSpec((ROW_TILE, arr.shape[1]), functools.partial(lambda i, p: (i % p, 0), p=period)))
        args.append(arr)
    for arr in ws:
        specs.append(pl.BlockSpec(arr.shape, functools.partial(lambda i, nd: (0,) * nd, nd=arr.ndim)))
        args.append(arr)
    return specs, args


def _stage_load(refs, n_rows, n_tps, n_consts, n_ws):
    it = iter(refs)
    rows = [next(it)[...].astype(F32) for _ in range(n_rows)]
    tps = [next(it)[0].astype(F32) for _ in range(n_tps)]
    consts = [next(it)[...].astype(F32) for _ in range(n_consts)]
    ws = [next(it)[...].astype(F32) for _ in range(n_ws)]
    return rows, tps, consts, ws


def stage_fwd(fn, rows, tps, consts, ws, outs, name):
    n_tiles = rows[0][0].shape[0] // ROW_TILE
    specs, args = _stage_specs(rows, tps, consts, ws)
    counts = (len(rows), len(tps), len(consts), len(ws))

    def body(*refs):
        r, t, c, w = _stage_load(refs[:sum(counts)], *counts)
        res = fn(r, t, c, w)
        for o_ref, o in zip(refs[sum(counts):], res):
            o_ref[...] = o.astype(o_ref.dtype)

    res = pl.pallas_call(
        body, name=name, grid=(n_tiles,), in_specs=specs,
        out_specs=[pl.BlockSpec((ROW_TILE, wd), lambda i: (i, 0)) for wd, _ in outs],
        out_shape=[jax.ShapeDtypeStruct((n_tiles * ROW_TILE, wd), dt) for wd, dt in outs],
    )(*args)
    return list(res)


def stage_bwd(fn, rows, tps, consts, ws, cts, row_grads, name):
    n_tiles = rows[0][0].shape[0] // ROW_TILE
    specs, args = _stage_specs(rows, tps, consts, ws)
    counts = (len(rows), len(tps), len(consts), len(ws))
    n_in = sum(counts)
    for ct in cts:
        specs.append(pl.BlockSpec((ROW_TILE, ct.shape[1]), lambda i: (i, 0)))
        args.append(ct)
    want = [k for k, dt in enumerate(row_grads) if dt is not None]
    out_specs = [pl.BlockSpec((ROW_TILE, rows[k][1]), lambda i: (i, 0)) for k in want]
    out_shape = [jax.ShapeDtypeStruct((n_tiles * ROW_TILE, rows[k][1]), row_grads[k]) for k in want]
    out_specs += [pl.BlockSpec((1, 1, a.shape[2]), lambda i: (i, 0, 0)) for a in tps]
    out_shape += [jax.ShapeDtypeStruct((n_tiles, 1, a.shape[2]), F32) for a in tps]
    out_specs += [pl.BlockSpec(a.shape, functools.partial(lambda i, nd: (0,) * nd, nd=a.ndim)) for a in ws]
    out_shape += [jax.ShapeDtypeStruct(a.shape, F32) for a in ws]

    def body(*refs):
        r, t, c, w = _stage_load(refs[:n_in], *counts)
        g = [ref[...].astype(F32) for ref in refs[n_in:n_in + len(cts)]]
        _, vjp = jax.vjp(lambda r_, t_, w_: fn(r_, t_, c, w_), r, t, w)
        dr, dt, dw = vjp(g)
        o = iter(refs[n_in + len(cts):])
        for k in want:
            ref = next(o)
            ref[...] = dr[k].astype(ref.dtype)
        for v in dt:
            next(o)[0] = v
        first = pl.program_id(0) == 0
        for v in dw:
            ref = next(o)

            @pl.when(first)
            def _():
                ref[...] = v

            @pl.when(jnp.logical_not(first))
            def _():
                ref[...] += v

    res = pl.pallas_call(body, name=name, grid=(n_tiles,), in_specs=specs, out_specs=out_specs, out_shape=out_shape)(*args)
    res = list(res)
    drows = [None] * len(rows)
    for k in want:
        drows[k] = res.pop(0)
    dtps = [res.pop(0) for _ in tps]
    dws = [res.pop(0) for _ in ws]
    return drows, dtps, dws


def fn_modulate(rows, tps, consts, ws):
    (x,), (sc, sh) = rows, tps
    return [x * (1.0 + sc) + sh]


def fn_post(rows, tps, consts, ws):
    (x, a), (g,), (lng, lnb) = rows, tps, ws
    return [layer_norm(ALPHA * x + g * a, lng, lnb)]


def fn_mix(rows, tps, consts, ws):
    ogf, ogb, orf, orb, pg, pr, mo = rows
    gng, = ws
    og = ogf + ogb
    out_g = og * lax.rsqrt(seg_sum(og * og, 64) * (1.0 / 64) + EPS) * gng * silu(pg)
    o = orf + orb
    oc = o - seg_sum(o, 64) * (1.0 / 64)
    out_r = oc * lax.rsqrt(seg_sum(oc * oc, 64) * (1.0 / 64) + EPS) * silu(pr)
    return [jnp.concatenate([out_g, out_r, mo], axis=-1)]


def fn_mla_prep(rows, tps, consts, ws):
    pq, pkv, plk = rows
    gq, gkv, wuq, wuk, wuv = ws
    qtab, ktab = consts[0:3], consts[3:6]
    cq = pq * lax.rsqrt(jnp.mean(pq * pq, axis=-1, keepdims=True) + EPS) * gq
    qp = mm_nn(cq, wuq)
    q = jnp.concatenate([rope(qp[:, h * LANES:(h + 1) * LANES], qtab, 8) for h in range(MLA_HEADS)], axis=-1)
    ckv = pkv * lax.rsqrt(jnp.mean(pkv * pkv, axis=-1, keepdims=True) + EPS) * gkv
    k = mm_nn(ckv, wuk) + place_kr(rope(plk, ktab, 8))
    v = mm_nn(ckv, wuv)
    return [q, k, v]


def fn_assemble(rows, tps, consts, ws):
    gq, gk, gv, gg, rq, rk, rv, rg, cq, ckv, lk1, lk2 = rows
    return [jnp.concatenate([gq, gk, gv, gg, rq, rk, rv, rg, cq, ckv, lk1 + lk2], axis=-1)]


def _head_masks():
    hm = (lax.shift_right_logical(_iota((4, 1, LANES), 2), 5) == _iota((4, 1, LANES), 0)).astype(F32)
    vm = (lax.shift_right_logical(_iota((4, 1, 256), 2), 6) == _iota((4, 1, 256), 0)).astype(F32)
    bd = (lax.shift_right_logical(_iota((256, LANES), 0), 6) == lax.shift_right_logical(_iota((256, LANES), 1), 5)).astype(F32)
    return hm, vm, bd


def chunk_step(s, q, k, v, la, upper):
    hm, vm, bd = _head_masks()
    t, u = _iota((4 * CHUNK, CHUNK), 0), _iota((4 * CHUNK, CHUNK), 1)
    t = jnp.bitwise_and(t, CHUNK - 1)
    causal = (u >= t) if upper else (t >= u)
    b = cumsum_rows(la, upper)
    bend = jnp.sum(la, axis=0, keepdims=True)
    half = 0.5 * bend
    qd = q * jnp.exp(b - half)
    kd = k * jnp.exp(half - b)
    qe = (qd[None] * hm).reshape(4 * CHUNK, LANES)
    att = jnp.where(causal, mm_nt(qe, kd), 0.0)
    o_intra = (mm_nn(att, v).reshape(4, CHUNK, 256) * vm).sum(0)
    o = mm_nt(q * jnp.exp(b), s) + o_intra
    s_new = (s * jnp.exp(bend) + mm_tn(v, k * jnp.exp(bend - b))) * bd
    return o, s_new


def scan_step(sg, sr, q, k, v, lrk, rq, rk, rv, gw, gb, rdec, tab, upper):
    la_g = log_sigmoid(mm_nn(lrk, gw) + gb) * (1.0 / GLA_TAU)
    og, sg2 = chunk_step(sg, q * GLA_DK ** -0.5, k, v, la_g, upper)
    la_r = jnp.broadcast_to(log_sigmoid(rdec), (CHUNK, LANES))
    orr, sr2 = chunk_step(sr, rope(rq, tab, 16), rope(rk * RET_DK ** -0.5, tab, 16), rv, la_r, upper)
    return og, orr, sg2, sr2


def _chunk_of(n, ncc, nch, reverse):
    if not reverse:
        return n
    return jnp.where(n < ncc, ncc - 1 - n, nch - 1 + ncc - n)


def _scan_in_specs(p3, tabs, gw, gb, rdec, cidx):
    nb = p3.shape[0]

    def blk(width, cb):
        return pl.BlockSpec((nb, CHUNK, width), lambda m: (0, cidx(m), cb))

    specs = [blk(128, C_GQ // 128), blk(128, C_GK // 128), blk(256, C_GV // 256), blk(128, C_LK // 128),
             blk(128, C_RQ // 128), blk(128, C_RK // 128), blk(256, C_RV // 256)]
    args = [p3] * 7
    for t in tabs:
        specs.append(pl.BlockSpec((CHUNK, LANES), lambda m: (cidx(m), 0)))
        args.append(t)
    for w in (gw, gb, rdec):
        specs.append(pl.BlockSpec(w.shape, lambda m: (0, 0)))
        args.append(w)
    return specs, args


def scan_fwd(p, tabs, gw, gb, rdec, nb, ncc, nch, reverse, name):
    cidx = lambda n: _chunk_of(n, ncc, nch, reverse)
    t = p.shape[0]
    specs, args = _scan_in_specs(p.reshape(nb, t // nb, p.shape[1]), tabs, gw, gb, rdec, cidx)

    def body(q, k, v, lrk, rq, rk, rv, tc, ta, tb, gw_r, gb_r, rd_r, og_r, or_r, sgo_r, sro_r, sg, sr):
        @pl.when(pl.program_id(0) == 0)
        def _():
            sg[...] = jnp.zeros_like(sg)
            sr[...] = jnp.zeros_like(sr)

        sgo_r[0] = sg[...]
        sro_r[0] = sr[...]
        ld = lambda r: r[...].astype(F32)
        tab, gw_, gb_, rd_ = (ld(tc), ld(ta), ld(tb)), ld(gw_r), ld(gb_r), ld(rd_r)
        for b in range(nb):
            lb = lambda r: r[b].astype(F32)
            og, orr, sg2, sr2 = scan_step(sg[b], sr[b], lb(q), lb(k), lb(v), lb(lrk), lb(rq), lb(rk), lb(rv),
                                          gw_, gb_, rd_, tab, reverse)
            og_r[b] = og
            or_r[b] = orr
            sg[b] = sg2
            sr[b] = sr2

    row_out = pl.BlockSpec((nb, CHUNK, 256), lambda n: (0, cidx(n), 0))
    st_out = pl.BlockSpec((1, nb, 256, LANES), lambda n: (n, 0, 0, 0))
    og, orr, sgs, srs = pl.pallas_call(
        body, name=name, grid=(nch,), in_specs=specs, out_specs=[row_out, row_out, st_out, st_out],
        out_shape=[jax.ShapeDtypeStruct((nb, t // nb, 256), F32)] * 2 + [jax.ShapeDtypeStruct((nch, nb, 256, LANES), F32)] * 2,
        scratch_shapes=[pltpu.VMEM((nb, 256, LANES), F32)] * 2,
    )(*args)
    return og.reshape(t, 256), orr.reshape(t, 256), sgs, srs


def scan_bwd(p, tabs, gw, gb, rdec, sg_in, sr_in, dog, dor, prev, nb, ncc, nch, reverse, name):
    step = lambda m: nch - 1 - m
    cidx = lambda m: _chunk_of(step(m), ncc, nch, reverse)
    t = p.shape[0]
    lt = t // nb
    specs, args = _scan_in_specs(p.reshape(nb, lt, p.shape[1]), tabs, gw, gb, rdec, cidx)
    st_spec = pl.BlockSpec((1, nb, 256, LANES), lambda m: (step(m), 0, 0, 0))
    specs += [st_spec, st_spec]
    args += [sg_in, sr_in]
    row = lambda width: pl.BlockSpec((nb, CHUNK, width), lambda m: (0, cidx(m), 0))
    specs += [row(256), row(256)]
    args += [dog.reshape(nb, lt, 256), dor.reshape(nb, lt, 256)]
    widths = (128, 128, 256, 128, 128, 128, 256)
    if prev is not None:
        specs += [row(wd) for wd in widths]
        args += [a.reshape(nb, lt, a.shape[1]) for a in prev]
    n_prev = 0 if prev is None else 7

    def body(*refs):
        (q, k, v, lrk, rq, rk, rv, tc, ta, tb, gw_r, gb_r, rd_r, sgi, sri, dog_r, dor_r), rest = refs[:17], refs[17:]
        prev_r, rest = rest[:n_prev], rest[n_prev:]
        outs, (dgw_r, dgb_r, drd_r, dsg, dsr) = rest[:7], rest[7:]
        first = pl.program_id(0) == 0

        @pl.when(first)
        def _():
            dsg[...] = jnp.zeros_like(dsg)
            dsr[...] = jnp.zeros_like(dsr)

        ld = lambda r: r[...].astype(F32)
        tab, gw_, gb_, rd_ = (ld(tc), ld(ta), ld(tb)), ld(gw_r), ld(gb_r), ld(rd_r)
        wsum = None
        for b in range(nb):
            lb = lambda r: r[b].astype(F32)
            prim = (sgi[0, b], sri[0, b], lb(q), lb(k), lb(v), lb(lrk), lb(rq), lb(rk), lb(rv), gw_, gb_, rd_)
            _, vjp = jax.vjp(lambda *a: scan_step(*a, tab, reverse), *prim)
            g = vjp((lb(dog_r), lb(dor_r), dsg[b], dsr[b]))
            dsg[b] = g[0]
            dsr[b] = g[1]
            for j in range(7):
                val = g[2 + j]
                if n_prev:
                    val = val + prev_r[j][b]
                outs[j][b] = val
            wsum = g[9:12] if wsum is None else tuple(a + c for a, c in zip(wsum, g[9:12]))
        for ref, val in zip((dgw_r, dgb_r, drd_r), wsum):
            @pl.when(first)
            def _():
                ref[...] = val

            @pl.when(jnp.logical_not(first))
            def _():
                ref[...] += val

    wspec = lambda w: pl.BlockSpec(w.shape, lambda m: (0, 0))
    res = pl.pallas_call(
        body, name=name, grid=(nch,), in_specs=specs,
        out_specs=[row(wd) for wd in widths] + [wspec(gw), wspec(gb), wspec(rdec)],
        out_shape=[jax.ShapeDtypeStruct((nb, lt, wd), F32) for wd in widths]
        + [jax.ShapeDtypeStruct(w.shape, F32) for w in (gw, gb, rdec)],
        scratch_shapes=[pltpu.VMEM((nb, 256, LANES), F32)] * 2,
    )(*args)
    return tuple(a.reshape(t, a.shape[2]) for a in res[:7]), res[7], res[8], res[9]


def attend(q, k, v):
    s = mm_nt(q, k) * MLA_SCALE
    p = jnp.exp(s - jnp.max(s, axis=-1, keepdims=True))
    return mm_nn(p * (1.0 / jnp.sum(p, axis=-1, keepdims=True)), v)


def _attn_tiles(lc, lt):
    nct = lc // ROW_TILE
    return nct, (lt - lc) // ROW_TILE


def mla_fwd(q, k, v, nb, lc, lt, name):
    nct, nlt = _attn_tiles(lc, lt)

    def body(q_ref, k_ref, v_ref, o_ref):
        def tile(r0, nk):
            rows = pl.ds(r0, ROW_TILE)
            outs = []
            for j in range(2):
                outs.append(attend(q_ref[rows, j * LANES:(j + 1) * LANES], k_ref[0:nk, j * LANES:(j + 1) * LANES],
                                   v_ref[0:nk, j * 64:(j + 1) * 64]))
            o_ref[rows, :] = jnp.concatenate(outs, axis=-1)

        for i in range(nct):
            tile(i * ROW_TILE, lc)

        def lat(i, carry):
            tile(pl.multiple_of(lc + i * ROW_TILE, ROW_TILE), lt)
            return carry

        lax.fori_loop(0, nlt, lat, 0)

    return pl.pallas_call(
        body, name=name, grid=(nb, MLA_HEADS // 2),
        in_specs=[pl.BlockSpec((lt, 2 * LANES), lambda b, h: (b, h)), pl.BlockSpec((lt, 2 * LANES), lambda b, h: (b, h)),
                  pl.BlockSpec((lt, LANES), lambda b, h: (b, h))],
        out_specs=pl.BlockSpec((lt, LANES), lambda b, h: (b, h)),
        out_shape=jax.ShapeDtypeStruct((nb * lt, MLA_HEADS * 64), F32),
    )(q, k, v)


def mla_bwd(q, k, v, do, nb, lc, lt, name):
    nct, nlt = _attn_tiles(lc, lt)

    def body(q_ref, k_ref, v_ref, do_ref, dq_ref, dk_ref, dv_ref, dka, dva):
        dka[...] = jnp.zeros_like(dka)
        dva[...] = jnp.zeros_like(dva)

        def tile(r0, nk):
            rows = pl.ds(r0, ROW_TILE)
            dqs = []
            for j in range(2):
                prim = (q_ref[rows, j * LANES:(j + 1) * LANES].astype(F32), k_ref[0:nk, j * LANES:(j + 1) * LANES].astype(F32),
                        v_ref[0:nk, j * 64:(j + 1) * 64].astype(F32))
                _, vjp = jax.vjp(attend, *prim)
                dq, dk, dv = vjp(do_ref[rows, j * 64:(j + 1) * 64])
                dqs.append(dq)
                dka[j, 0:nk, :] += dk
                dva[j, 0:nk, :] += dv
            dq_ref[rows, :] = jnp.concatenate(dqs, axis=-1)

        for i in range(nct):
            tile(i * ROW_TILE, lc)

        def lat(i, carry):
            tile(pl.multiple_of(lc + i * ROW_TILE, ROW_TILE), lt)
            return carry

        lax.fori_loop(0, nlt, lat, 0)
        dk_ref[...] = jnp.concatenate([dka[0], dka[1]], axis=-1)
        dv_ref[...] = jnp.concatenate([dva[0], dva[1]], axis=-1)

    t = nb * lt
    return pl.pallas_call(
        body, name=name, grid=(nb, MLA_HEADS // 2),
        in_specs=[pl.BlockSpec((lt, 2 * LANES), lambda b, h: (b, h)), pl.BlockSpec((lt, 2 * LANES), lambda b, h: (b, h)),
                  pl.BlockSpec((lt, LANES), lambda b, h: (b, h)), pl.BlockSpec((lt, LANES), lambda b, h: (b, h))],
        out_specs=[pl.BlockSpec((lt, 2 * LANES), lambda b, h: (b, h)), pl.BlockSpec((lt, 2 * LANES), lambda b, h: (b, h)),
                   pl.BlockSpec((lt, LANES), lambda b, h: (b, h))],
        out_shape=[jax.ShapeDtypeStruct((t, MLA_HEADS * LANES), F32), jax.ShapeDtypeStruct((t, MLA_HEADS * LANES), F32),
                   jax.ShapeDtypeStruct((t, MLA_HEADS * 64), F32)],
        scratch_shapes=[pltpu.VMEM((2, lt, LANES), F32), pltpu.VMEM((2, lt, 64), F32)],
    )(q, k, v, do)


def _gate_specs(u, per_batch, lc):
    g8 = GATE_ROWS // 8
    n8 = u.shape[0] // 8
    width = u.shape[1]
    main = pl.BlockSpec((GATE_ROWS, width), lambda i: (i, 0))
    prev = pl.BlockSpec((8, width), lambda i: (jnp.maximum(i * g8 - 1, 0), 0))
    nxt = pl.BlockSpec((8, width), lambda i: (jnp.minimum((i + 1) * g8, n8 - 1), 0))
    return main, prev, nxt


def _seg_edges(per_batch, lc):
    j = pl.program_id(0) % (per_batch // GATE_ROWS)
    first = (j == 0) | (j == lc // GATE_ROWS)
    last = (j == lc // GATE_ROWS - 1) | (j == per_batch // GATE_ROWS - 1)
    return first, last


def _shifted(x, prev_ref, next_ref, first, last):
    rows = _iota(x.shape, 0)
    before = jnp.where(first, 0.0, prev_ref[7:8, :].astype(F32))
    after = jnp.where(last, 0.0, next_ref[0:1, :].astype(F32))
    xm = jnp.where(rows == 0, before, pltpu.roll(x, 1, 0))
    xp = jnp.where(rows == x.shape[0] - 1, after, pltpu.roll(x, x.shape[0] - 1, 0))
    return xm, xp


def gate_fwd(u, cw, cb, per_batch, lc, name):
    main, prev, nxt = _gate_specs(u, per_batch, lc)
    f = u.shape[1] // 2

    def body(u_ref, p_ref, n_ref, w_ref, b_ref, act_ref):
        first, last = _seg_edges(per_batch, lc)
        x = u_ref[...]
        xm, xp = _shifted(x, p_ref, n_ref, first, last)
        c = w_ref[0:1, :] * xm + w_ref[1:2, :] * x + w_ref[2:3, :] * xp + b_ref[...]
        act_ref[...] = (silu(c[:, :f]) * c[:, f:]).astype(act_ref.dtype)

    return pl.pallas_call(
        body, name=name, grid=(u.shape[0] // GATE_ROWS,),
        in_specs=[main, prev, nxt, pl.BlockSpec(cw.shape, lambda i: (0, 0)), pl.BlockSpec(cb.shape, lambda i: (0, 0))],
        out_specs=pl.BlockSpec((GATE_ROWS, f), lambda i: (i, 0)),
        out_shape=jax.ShapeDtypeStruct((u.shape[0], f), BF),
    )(u, u, u, cw, cb)


def gate_bwd(u, cw, cb, dact, per_batch, lc, name):
    main, prev, nxt = _gate_specs(u, per_batch, lc)
    f = u.shape[1] // 2

    def body(u_ref, p_ref, n_ref, w_ref, b_ref, da_ref, dc_ref, dw_ref):
        first, last = _seg_edges(per_batch, lc)
        x = u_ref[...]
        xm, xp = _shifted(x, p_ref, n_ref, first, last)
        c = w_ref[0:1, :] * xm + w_ref[1:2, :] * x + w_ref[2:3, :] * xp + b_ref[...]
        a, g = c[:, :f], c[:, f:]
        sg = jax.nn.sigmoid(a)
        da = da_ref[...]
        dc = jnp.concatenate([da * g * (sg * (1.0 + a * (1.0 - sg))), da * (a * sg)], axis=-1)
        dc_ref[...] = dc
        part = jnp.concatenate([jnp.sum(xm * dc, axis=0, keepdims=True), jnp.sum(x * dc, axis=0, keepdims=True),
                                jnp.sum(xp * dc, axis=0, keepdims=True), jnp.sum(dc, axis=0, keepdims=True),
                                jnp.zeros((4, 2 * f), F32)], axis=0)

        @pl.when(pl.program_id(0) == 0)
        def _():
            dw_ref[...] = part

        @pl.when(pl.program_id(0) > 0)
        def _():
            dw_ref[...] += part

    return pl.pallas_call(
        body, name=name, grid=(u.shape[0] // GATE_ROWS,),
        in_specs=[main, prev, nxt, pl.BlockSpec(cw.shape, lambda i: (0, 0)), pl.BlockSpec(cb.shape, lambda i: (0, 0)),
                  pl.BlockSpec((GATE_ROWS, f), lambda i: (i, 0))],
        out_specs=[main, pl.BlockSpec((8, 2 * f), lambda i: (0, 0))],
        out_shape=[jax.ShapeDtypeStruct(u.shape, F32), jax.ShapeDtypeStruct((8, 2 * f), F32)],
    )(u, u, u, cw, cb, dact)


def conv_transpose(dc, cw, per_batch, lc, name):
    main, prev, nxt = _gate_specs(dc, per_batch, lc)

    def body(d_ref, p_ref, n_ref, w_ref, du_ref):
        first, last = _seg_edges(per_batch, lc)
        x = d_ref[...]
        xm, xp = _shifted(x, p_ref, n_ref, first, last)
        du_ref[...] = (w_ref[0:1, :] * xp + w_ref[1:2, :] * x + w_ref[2:3, :] * xm).astype(du_ref.dtype)

    return pl.pallas_call(
        body, name=name, grid=(dc.shape[0] // GATE_ROWS,),
        in_specs=[main, prev, nxt, pl.BlockSpec(cw.shape, lambda i: (0, 0))],
        out_specs=main, out_shape=jax.ShapeDtypeStruct(dc.shape, BF),
    )(dc, dc, dc, cw)


def loss_head(x, target, tiles_per_batch, ctx_tiles, name):
    n_tiles = x.shape[0] // ROW_TILE
    lat_tiles = tiles_per_batch - ctx_tiles

    def tgt_idx(i):
        j = i % tiles_per_batch
        return jnp.where(j < ctx_tiles, 0, (i // tiles_per_batch) * lat_tiles + j - ctx_tiles), 0

    def body(x_ref, t_ref, dx_ref, l_ref):
        lat = (pl.program_id(0) % tiles_per_batch >= ctx_tiles).astype(F32)
        err = (x_ref[...] - t_ref[...]) * lat
        dx_ref[...] = err * (1.0 / D_MODEL)
        l_ref[...] = jnp.full(l_ref.shape, 0.5 / D_MODEL * jnp.sum(err * err), F32)

    return pl.pallas_call(
        body, name=name, grid=(n_tiles,),
        in_specs=[pl.BlockSpec((ROW_TILE, D_MODEL), lambda i: (i, 0)), pl.BlockSpec((ROW_TILE, D_MODEL), tgt_idx)],
        out_specs=[pl.BlockSpec((ROW_TILE, D_MODEL), lambda i: (i, 0)), pl.BlockSpec((1, 8, LANES), lambda i: (i, 0, 0))],
        out_shape=[jax.ShapeDtypeStruct(x.shape, F32), jax.ShapeDtypeStruct((n_tiles, 8, LANES), F32)],
    )(x, target)


def adamw(w, g, m, v, name):
    rows, cols = w.shape
    tr = rows
    for cand in (512, 256, 128, 64, 32, 16, 8):
        if rows % cand == 0 and cand * cols * 4 <= (1 << 20):
            tr = cand
            break

    def body(w_ref, g_ref, m_ref, v_ref, d_ref, mo_ref, vo_ref):
        gg = g_ref[...]
        m2 = ADAM_B1 * m_ref[...] + (1.0 - ADAM_B1) * gg
        v2 = ADAM_B2 * v_ref[...] + (1.0 - ADAM_B2) * (gg * gg)
        m_hat = m2 / (1.0 - ADAM_B1 ** ADAM_STEP)
        v_hat = v2 / (1.0 - ADAM_B2 ** ADAM_STEP)
        d_ref[...] = -ADAM_LR * (m_hat / (jnp.sqrt(v_hat) + ADAM_EPS) + ADAM_WD * w_ref[...])
        mo_ref[...] = m2
        vo_ref[...] = v2

    spec = pl.BlockSpec((tr, cols), lambda i: (i, 0))
    return pl.pallas_call(body, name=name, grid=(rows // tr,), in_specs=[spec] * 4, out_specs=[spec] * 3,
                          out_shape=[jax.ShapeDtypeStruct(w.shape, F32)] * 3)(w, g, m, v)


def fn_post_mod(rows, tps, consts, ws):
    (x, a), (g, sc, sh), (lng, lnb) = rows, tps, ws
    y = layer_norm(ALPHA * x + g * a, lng, lnb)
    return [y, y * (1.0 + sc) + sh]


def _pick(n, cands):
    for c in cands:
        if n % c == 0:
            return c
    return n


def rope_tables(lc, l):
    pos = jnp.arange(l, dtype=F32)
    ret_inv = 1.0 / (ROPE_BASE ** jnp.linspace(0.0, 1.0, RET_DK // 2, dtype=F32))
    ang = pos[:, None] * ret_inv
    rc, rs = jnp.cos(ang), jnp.sin(ang)
    n_ax = MLA_D_ROPE // 4
    ax_inv = ROPE_BASE ** (-jnp.arange(n_ax, dtype=F32) / n_ax)
    rows_n = l // GRID_W
    rows = jnp.repeat(jnp.arange(rows_n, dtype=F32), GRID_W)
    cols = jnp.tile(jnp.arange(GRID_W, dtype=F32), rows_n)
    ra, ca = rows[:, None] * ax_inv, cols[:, None] * ax_inv
    rwc, rws, clc, cls = jnp.cos(ra), jnp.sin(ra), jnp.cos(ca), jnp.sin(ca)
    one = lambda n: jnp.ones((l, n), F32)
    zero = lambda n: jnp.zeros((l, n), F32)
    cat = lambda parts: jnp.concatenate(parts, axis=1)

    def with_ctx(tab, is_cos):
        head = jnp.ones((lc, LANES), F32) if is_cos else jnp.zeros((lc, LANES), F32)
        return jnp.concatenate([head, tab], axis=0)

    ret = (cat([rc, rc] * 4), cat([-rs, zero(16)] * 4), cat([zero(16), rs] * 4))
    ax_c = [rwc, rwc, clc, clc]
    ax_a = [-rws, zero(8), -cls, zero(8)]
    ax_b = [zero(8), rws, zero(8), cls]
    qt = (cat([one(64)] + ax_c + [one(32)]), cat([zero(64)] + ax_a + [zero(32)]), cat([zero(64)] + ax_b + [zero(32)]))
    kt = (cat([one(32)] + ax_c + [one(64)]), cat([zero(32)] + ax_a + [zero(64)]), cat([zero(32)] + ax_b + [zero(64)]))
    fix = lambda t3: tuple(with_ctx(t, k == 0) for k, t in enumerate(t3))
    return fix(ret), fix(qt), fix(kt)


_IN_ORDER = ((0, 128), (128, 256), (256, 512), (544, 800), (800, 928), (928, 1056), (1056, 1312), (1312, 1568),
             (1568, 1824), (1824, 1952), (512, 544), (1952, 1984))


def permute_w_in(w):
    parts = [w[:, a:b] for a, b in _IN_ORDER] + [jnp.zeros((w.shape[0], D_INP - D_IN), w.dtype)]
    return jnp.concatenate(parts, axis=1)


def unpermute_w_in(g):
    out, at = {}, 0
    for a, b in _IN_ORDER:
        out[a] = g[:, at:at + b - a]
        at += b - a
    return jnp.concatenate([out[a] for a in sorted(out)], axis=1)


def layer_weights(w, l):
    f = lambda a: a.astype(F32)
    r = {}
    r["win"] = permute_w_in(w["w_in"][l]).astype(BF)
    r["wout"] = w["w_out"][l].astype(BF)
    r["wup"] = w["ffn_up"][l].astype(BF)
    r["wdown"] = w["ffn_down"][l].astype(BF)
    uq = w["mla_w_uq"][l].reshape(256, MLA_HEADS, 96)
    r["wuq"] = jnp.pad(uq, ((0, 0), (0, 0), (0, 32))).reshape(256, 8 * LANES).astype(BF)
    uk = w["mla_w_uk"][l].reshape(128, MLA_HEADS, 64)
    r["wuk"] = jnp.pad(uk, ((0, 0), (0, 0), (0, 64))).reshape(128, 8 * LANES).astype(BF)
    r["wuv"] = w["mla_w_uv"][l].astype(BF)
    gw = f(w["gla_gate_w"][l])
    z16 = jnp.zeros((16, LANES), F32)
    z96 = jnp.zeros((96, LANES), F32)
    r["gw"] = (jnp.concatenate([gw[0], z16, z96], axis=0), jnp.concatenate([z16, gw[1], z96], axis=0))
    r["gb"] = tuple(f(w["gla_gate_b"][l][d]).reshape(1, LANES) for d in range(2))
    r["rdec"] = tuple(jnp.repeat(f(w["ret_decay"][l][d]), 32).reshape(1, LANES) for d in range(2))
    r["gng"] = jnp.tile(f(w["gla_norm_g"][l]), 4).reshape(1, 256)
    r["gq"] = f(w["mla_q_norm_g"][l]).reshape(1, 256)
    r["gkv"] = f(w["mla_kv_norm_g"][l]).reshape(1, 128)
    for n in ("ln1_g", "ln1_b", "ln2_g", "ln2_b"):
        r[n] = f(w[n][l]).reshape(1, D_MODEL)
    r["cw"] = f(w["ffn_conv_w"][l])
    r["cb"] = f(w["ffn_conv_b"][l]).reshape(1, 2 * D_FF)
    return r


def tile_params(mod_l, nb, nct, nlt):
    m6 = mod_l.reshape(8, 6, D_MODEL)
    out = []
    for j in range(6):
        parts = []
        for b in range(nb):
            parts.append(jnp.broadcast_to(m6[4, j], (nct, 1, D_MODEL)))
            parts.append(jnp.broadcast_to(m6[b, j], (nlt, 1, D_MODEL)))
        out.append(jnp.concatenate(parts, axis=0))
    return out


def tile_param_grads(dts, nb, nct, nlt):
    cols = []
    for dt in dts:
        d = dt.reshape(nb, nct + nlt, D_MODEL)
        lat = jnp.sum(d[:, nct:], axis=1)
        ctx = jnp.sum(d[:, :nct], axis=(0, 1))
        cols.append(jnp.concatenate([lat, jnp.zeros((4 - nb, D_MODEL), F32), ctx[None], jnp.zeros((3, D_MODEL), F32)], axis=0))
    return jnp.stack(cols, axis=1).reshape(8, 6 * D_MODEL)


def layer_forward(x, h1, tp, tp_next, lw, tabs, dims, tag):
    nb, lc, lt = dims
    t = x.shape[0]
    nbt = lt // ROW_TILE
    ncc, nch = lc // CHUNK, lt // CHUNK
    tm = _pick(t, (1024, 768, 512, 256))
    ret_tab, q_tab, k_tab = tabs
    full = lambda a: (a, a.shape[1], 0)
    p = matmul(h1, lw["win"], "nn", F32, tm, 1024, 1024, "proj_in")
    ogf, orf, sgf, srf = scan_fwd(p, ret_tab, lw["gw"][0], lw["gb"][0], lw["rdec"][0], nb, ncc, nch, False, "scan_fwd_f")
    ogb, orb, sgb, srb = scan_fwd(p, ret_tab, lw["gw"][1], lw["gb"][1], lw["rdec"][1], nb, ncc, nch, True, "scan_fwd_b")
    prep_rows = [(p, 256, C_CQ // 256), (p, 128, C_CKV // 128), (p, 128, C_LK // 128)]
    prep_consts = [(a, nbt) for a in q_tab + k_tab]
    prep_ws = [lw["gq"], lw["gkv"], lw["wuq"], lw["wuk"], lw["wuv"]]
    q, k, v = stage_fwd(fn_mla_prep, prep_rows, [], prep_consts, prep_ws, [(1024, BF), (1024, BF), (512, BF)], "mla_prep")
    mo = mla_fwd(q, k, v, nb, lc, lt, "mla_attn")
    mix_rows = [full(ogf), full(ogb), full(orf), full(orb), (p, 256, C_GG // 256), (p, 256, C_RG // 256), full(mo)]
    m, = stage_fwd(fn_mix, mix_rows, [], [], [lw["gng"]], [(1024, BF)], "mix")
    a = matmul(m, lw["wout"], "nn", F32, tm, 1024, 1024, "proj_out")
    x1, h2 = stage_fwd(fn_post_mod, [full(x), full(a)], [tp[2], tp[4], tp[3]], [], [lw["ln1_g"], lw["ln1_b"]],
                       [(1024, F32), (1024, BF)], "post1")
    u = matmul(h2, lw["wup"], "nn", F32, tm, 1408, 1024, "ffn_up")
    act = gate_fwd(u, lw["cw"], lw["cb"], lt, lc, "ffn_gate")
    f = matmul(act, lw["wdown"], "nn", F32, tm, 1024, 1408, "ffn_down")
    if tp_next is None:
        x2, = stage_fwd(fn_post, [full(x1), full(f)], [tp[5]], [], [lw["ln2_g"], lw["ln2_b"]], [(1024, F32)], "post2_last")
        h1n = None
    else:
        x2, h1n = stage_fwd(fn_post_mod, [full(x1), full(f)], [tp[5], tp_next[0], tp_next[1]], [],
                            [lw["ln2_g"], lw["ln2_b"]], [(1024, F32), (1024, BF)], "post2")
    res = dict(x=x, h1=h1, p=p, ogf=ogf, orf=orf, sgf=sgf, srf=srf, ogb=ogb, orb=orb, sgb=sgb, srb=srb, q=q, k=k, v=v,
               mo=mo, m=m, a=a, x1=x1, h2=h2, u=u, act=act, f=f, mix_rows=mix_rows, prep_rows=prep_rows,
               prep_consts=prep_consts, prep_ws=prep_ws)
    return x2, h1n, res


def layer_backward(dx2, dh1n, res, tp, tp_next, lw, tabs, dims):
    nb, lc, lt = dims
    r = res
    t = dx2.shape[0]
    ncc, nch = lc // CHUNK, lt // CHUNK
    tm = _pick(t, (1024, 768, 512, 256))
    tkr = _pick(t, (2304, 1536, 1024, 768, 512))
    ret_tab = tabs[0]
    full = lambda a: (a, a.shape[1], 0)
    g = {}
    if tp_next is None:
        (dx1a, df), (dg2,), (g["ln2_g"], g["ln2_b"]) = stage_bwd(
            fn_post, [full(r["x1"]), full(r["f"])], [tp[5]], [], [lw["ln2_g"], lw["ln2_b"]], [dx2], [F32, BF], "post2_last_bwd")
        dnext = None
    else:
        (dx1a, df), (dg2, dsc1n, dsh1n), (g["ln2_g"], g["ln2_b"]) = stage_bwd(
            fn_post_mod, [full(r["x1"]), full(r["f"])], [tp[5], tp_next[0], tp_next[1]], [], [lw["ln2_g"], lw["ln2_b"]],
            [dx2, dh1n], [F32, BF], "post2_bwd")
        dnext = (dsc1n, dsh1n)
    dact = matmul(df, lw["wdown"], "nt", F32, tm, 1408, 1024, "ffn_down_dx")
    g["ffn_down"] = matmul(r["act"], df, "tn", F32, 1408, 1024, tkr, "ffn_down_dw")
    dc, dcw = gate_bwd(r["u"], lw["cw"], lw["cb"], dact, lt, lc, "ffn_gate_bwd")
    g["ffn_conv_w"], g["ffn_conv_b"] = dcw[0:3], dcw[3]
    du = conv_transpose(dc, lw["cw"], lt, lc, "ffn_conv_t")
    dh2 = matmul(du, lw["wup"], "nt", F32, tm, 1024, 1408, "ffn_up_dx")
    g["ffn_up"] = matmul(r["h2"], du, "tn", F32, 1024, 1408, tkr, "ffn_up_dw")
    (dxa, da), (dg1, dsc2, dsh2), (g["ln1_g"], g["ln1_b"]) = stage_bwd(
        fn_post_mod, [full(r["x"]), full(r["a"])], [tp[2], tp[4], tp[3]], [], [lw["ln1_g"], lw["ln1_b"]],
        [dx1a, dh2], [F32, BF], "post1_bwd")
    dm = matmul(da, lw["wout"], "nt", F32, tm, 1024, 1024, "proj_out_dx")
    g["w_out"] = matmul(r["m"], da, "tn", F32, 1024, 1024, tkr, "proj_out_dw")
    (dog, _, dor, _, dpg, dpr, dmo), _, (dgng,) = stage_bwd(
        fn_mix, r["mix_rows"], [], [], [lw["gng"]], [dm], [F32, None, F32, None, F32, F32, F32], "mix_bwd")
    g["gla_norm_g"] = jnp.sum(dgng.reshape(4, 64), axis=0)
    dq, dk, dv = mla_bwd(r["q"], r["k"], r["v"], dmo, nb, lc, lt, "mla_attn_bwd")
    (dpq, dpkv, dplk), _, (dgq, dgkv, dwuq, dwuk, dwuv) = stage_bwd(
        fn_mla_prep, r["prep_rows"], [], r["prep_consts"], r["prep_ws"], [dq, dk, dv], [F32, F32, F32], "mla_prep_bwd")
    g["mla_q_norm_g"], g["mla_kv_norm_g"] = dgq.reshape(256), dgkv.reshape(128)
    g["mla_w_uq"] = dwuq.reshape(256, MLA_HEADS, LANES)[:, :, :96].reshape(256, MLA_HEADS * 96)
    g["mla_w_uk"] = dwuk.reshape(128, MLA_HEADS, LANES)[:, :, :64].reshape(128, MLA_HEADS * 64)
    g["mla_w_uv"] = dwuv
    s7, dgw0, dgb0, drd0 = scan_bwd(r["p"], ret_tab, lw["gw"][0], lw["gb"][0], lw["rdec"][0], r["sgf"], r["srf"], dog, dor,
                                    None, nb, ncc, nch, False, "scan_bwd_f")
    s7, dgw1, dgb1, drd1 = scan_bwd(r["p"], ret_tab, lw["gw"][1], lw["gb"][1], lw["rdec"][1], r["sgb"], r["srb"], dog, dor,
                                    s7, nb, ncc, nch, True, "scan_bwd_b")
    g["gla_gate_w"] = jnp.stack([dgw0[0:16], dgw1[16:32]])
    g["gla_gate_b"] = jnp.stack([dgb0[0], dgb1[0]])
    g["ret_decay"] = jnp.stack([jnp.sum(drd0.reshape(4, 32), axis=1), jnp.sum(drd1.reshape(4, 32), axis=1)])
    gq_, gk_, gv_, glrk, rq_, rk_, rv_ = s7
    pieces = [gq_, gk_, gv_, dpg, rq_, rk_, rv_, dpr, dpq, dpkv, glrk, dplk]
    dp, = stage_fwd(fn_assemble, [full(a) for a in pieces], [], [], [], [(D_INP, BF)], "dproj_assemble")
    dh1 = matmul(dp, lw["win"], "nt", F32, tm, 1024, 1024, "proj_in_dx")
    g["w_in"] = unpermute_w_in(matmul(r["h1"], dp, "tn", F32, 1024, 1024, tkr, "proj_in_dw"))
    for n in ("ln1_g", "ln1_b", "ln2_g", "ln2_b"):
        g[n] = g[n].reshape(D_MODEL)
    dtp = [None, None, dg1, dsh2, dsc2, dg2]
    return dxa, dh1, dtp, dnext, g


def local_step(x, c, ctx, c_ctx, w, loss_target):
    nb, l, _ = x.shape
    lc = ctx.shape[1]
    lt = lc + l
    dims = (nb, lc, lt)
    nct, nlt = lc // ROW_TILE, l // ROW_TILE
    tabs = rope_tables(lc, l)
    x0 = jnp.concatenate([ctx, x], axis=1).reshape(nb * lt, D_MODEL)
    s8 = jnp.concatenate([silu(c), jnp.zeros((4 - nb, D_MODEL), F32), silu(c_ctx)[None], jnp.zeros((3, D_MODEL), F32)], axis=0)
    lws, tps = [], []
    for i in range(DEPTH):
        mod = matmul(s8, w["ada_w"][i].astype(BF), "nn", F32, 8, 1536, 1024, "ada_mod") + w["ada_b"][i].astype(F32)[None]
        tps.append(tile_params(mod, nb, nct, nlt))
        lws.append(layer_weights(w, i))
    h1, = stage_fwd(fn_modulate, [(x0, D_MODEL, 0)], [tps[0][1], tps[0][0]], [], [], [(D_MODEL, BF)], "mod_in")
    xs, ress = x0, []
    for i in range(DEPTH):
        tpn = None if i == DEPTH - 1 else (tps[i + 1][1], tps[i + 1][0])
        xs, h1, res = layer_forward(xs, h1, tps[i], tpn, lws[i], tabs, dims, i)
        ress.append(res)
    dx, lparts = loss_head(xs, loss_target.reshape(nb * l, D_MODEL), nct + nlt, nct, "loss_head")
    loss = jnp.sum(lparts[:, 0, 0])
    grads = [None] * DEPTH
    dtps = [None] * DEPTH
    dh1 = None
    for i in reversed(range(DEPTH)):
        tpn = None if i == DEPTH - 1 else (tps[i + 1][1], tps[i + 1][0])
        dx, dh1, dtp, dn, grads[i] = layer_backward(dx, dh1, ress[i], tps[i], tpn, lws[i], tabs, dims)
        if dn is not None:
            dtps[i + 1][1], dtps[i + 1][0] = dn
        dtps[i] = dtp
    (dx0b,), (dsc1, dsh1), _ = stage_bwd(fn_modulate, [(x0, D_MODEL, 0)], [tps[0][1], tps[0][0]], [], [], [dh1], [F32], "mod_in_bwd")
    dtps[0][1], dtps[0][0] = dsc1, dsh1
    grad_x = (dx + dx0b).reshape(nb, lt, D_MODEL)[:, lc:]
    dmod = jnp.stack([tile_param_grads(d, nb, nct, nlt) for d in dtps])
    gw = {n: jnp.stack([grads[i][n] for i in range(DEPTH)]) for n in grads[0]}
    return loss, grad_x, gw, dmod, s8


MESH_IDS = pl.DeviceIdType.MESH
ANY = pl.BlockSpec(memory_space=pl.ANY)


def _place():
    return lax.axis_index("x"), lax.axis_index("y"), lax.axis_index("c")


def _dma_sems(n, per):
    return pltpu.SemaphoreType.DMA((n, per))


def all_gather8(blocks, name):
    n = len(blocks)

    def body(*refs):
        x_refs, out_refs, (send_sems, recv_sems, local_sems) = refs[:n], refs[n:2 * n], refs[2 * n:]
        x, y, c = _place()
        me, sibling = (x, y, c), (x, y, 1 - c)
        chips = [(1 - x, y), (x, 1 - y), (1 - x, 1 - y)]

        def copy(a, k, blk, to, own=False):
            slot = out_refs[a].at[4 * blk[0] + 2 * blk[1] + blk[2]]
            return pltpu.make_async_remote_copy(
                src_ref=x_refs[a] if own else slot, dst_ref=slot,
                send_sem=send_sems.at[a, k], recv_sem=recv_sems.at[a, k], device_id=to, device_id_type=MESH_IDS)

        mine = [pltpu.make_async_copy(x_refs[a], out_refs[a].at[4 * x + 2 * y + c], local_sems.at[a, 0]) for a in range(n)]
        first = []
        for a in range(n):
            mine[a].start()
            first.append(copy(a, 0, me, sibling, own=True))
            first += [copy(a, 1 + j, me, (*chip, c), own=True) for j, chip in enumerate(chips)]
        for cp in first:
            cp.start()
        passed = []
        for j, chip in enumerate(chips):
            for a in range(n):
                copy(a, 1 + j, (*chip, c), me).wait_recv()
                passed.append(copy(a, 4 + j, (*chip, c), sibling))
                passed[-1].start()
        for a in range(n):
            copy(a, 0, sibling, me).wait_recv()
            for j, chip in enumerate(chips):
                copy(a, 4 + j, (*chip, 1 - c), me).wait_recv()
        for cp in first + passed:
            cp.wait_send()
        for cp in mine:
            cp.wait()

    return pl.pallas_call(
        body, name=name, out_shape=[jax.ShapeDtypeStruct((8,) + b.shape, b.dtype) for b in blocks],
        in_specs=[ANY] * n, out_specs=[ANY] * n,
        scratch_shapes=[_dma_sems(n, 7), _dma_sems(n, 7), _dma_sems(n, 1)],
    )(*blocks)


def swap_cores(blocks, name):
    n = len(blocks)

    def body(*refs):
        x_refs, out_refs, (send_sems, recv_sems) = refs[:n], refs[n:2 * n], refs[2 * n:]
        x, y, c = _place()
        cps = [pltpu.make_async_remote_copy(src_ref=x_refs[a], dst_ref=out_refs[a], send_sem=send_sems.at[a, 0],
                                            recv_sem=recv_sems.at[a, 0], device_id=(x, y, 1 - c), device_id_type=MESH_IDS)
               for a in range(n)]
        for cp in cps:
            cp.start()
        for cp in cps:
            cp.wait()

    return pl.pallas_call(
        body, name=name, out_shape=[jax.ShapeDtypeStruct(b.shape, b.dtype) for b in blocks],
        in_specs=[ANY] * n, out_specs=[ANY] * n, scratch_shapes=[_dma_sems(n, 1), _dma_sems(n, 1)],
    )(*blocks)


def exchange_chips(parts, name):
    n = len(parts)

    def body(*refs):
        p_refs, out_refs, (send_sems, recv_sems, local_sems) = refs[:n], refs[n:2 * n], refs[2 * n:]
        x, y, c = _place()
        jm = 2 * x + y
        chips = [(1 - x, y), (x, 1 - y), (1 - x, 1 - y)]
        mine = [pltpu.make_async_copy(p_refs[a].at[jm], out_refs[a].at[jm], local_sems.at[a, 0]) for a in range(n)]
        cps = []
        for k, (px, py) in enumerate(chips):
            for a in range(n):
                cps.append(pltpu.make_async_remote_copy(
                    src_ref=p_refs[a].at[2 * px + py], dst_ref=out_refs[a].at[jm], send_sem=send_sems.at[a, k],
                    recv_sem=recv_sems.at[a, k], device_id=(px, py, c), device_id_type=MESH_IDS))
        for cp in mine + cps:
            cp.start()
        for k, (px, py) in enumerate(chips):
            for a in range(n):
                pltpu.make_async_remote_copy(
                    src_ref=p_refs[a].at[jm], dst_ref=out_refs[a].at[2 * px + py], send_sem=send_sems.at[a, k],
                    recv_sem=recv_sems.at[a, k], device_id=(px, py, c), device_id_type=MESH_IDS).wait_recv()
        for cp in cps:
            cp.wait_send()
        for cp in mine:
            cp.wait()

    return pl.pallas_call(
        body, name=name, out_shape=[jax.ShapeDtypeStruct(p.shape, p.dtype) for p in parts],
        in_specs=[ANY] * n, out_specs=[ANY] * n, scratch_shapes=[_dma_sems(n, 3), _dma_sems(n, 3), _dma_sems(n, 1)],
    )(*parts)


def _row_tile(rows, cols, itemsize=4, limit=1 << 21):
    for cand in (2048, 1024, 512, 256, 128, 64, 32, 16):
        if rows % cand == 0 and cand * cols * itemsize <= limit:
            return cand
    return rows


def add_halves(a, b, kind, name):
    if kind == "col":
        _, k, n = a.shape
        n4 = n // 4
        tr = _row_tile(k, n4)
        in_spec = pl.BlockSpec((1, tr, n4), lambda j, h, i: (h, i, j))
        out_spec = pl.BlockSpec((1, 1, tr, n4), lambda j, h, i: (j, h, i, 0))
        grid, out_shape = (4, 2, k // tr), (4, 2, k, n4)
    elif kind == "row":
        _, k, n = a.shape
        k4 = k // 4
        tr = _row_tile(k4, n)
        nt = k4 // tr
        in_spec = pl.BlockSpec((1, tr, n), lambda j, h, i: (h, j * nt + i, 0))
        out_spec = pl.BlockSpec((1, 1, tr, n), lambda j, h, i: (j, h, i, 0))
        grid, out_shape = (4, 2, nt), (4, 2, k4, n)
    else:
        _, _, k, n = a.shape
        tr = _row_tile(k, n)
        in_spec = out_spec = pl.BlockSpec((1, 1, tr, n), lambda j, h, i: (j, h, i, 0))
        grid, out_shape = (4, 2, k // tr), a.shape

    def body(a_ref, b_ref, s_ref):
        s_ref[...] = (a_ref[...] + b_ref[...].astype(F32)).astype(BF).reshape(s_ref.shape)

    return pl.pallas_call(body, name=name, grid=grid, in_specs=[in_spec, in_spec], out_specs=out_spec,
                          out_shape=jax.ShapeDtypeStruct(out_shape, BF))(a, b)


def sum_slots(a, name):
    s, m, k, n = a.shape
    tr = _row_tile(k, n, limit=(1 << 22) // s)

    def body(a_ref, o_ref):
        acc = a_ref[0].astype(F32)
        for j in range(1, s):
            acc = acc + a_ref[j].astype(F32)
        o_ref[...] = acc

    return pl.pallas_call(body, name=name, grid=(m, k // tr), in_specs=[pl.BlockSpec((s, 1, tr, n), lambda h, i: (0, h, i, 0))],
                          out_specs=pl.BlockSpec((1, tr, n), lambda h, i: (h, i, 0)),
                          out_shape=jax.ShapeDtypeStruct((m, k, n), F32))(a)


def sum_small(arrays, name):
    n = len(arrays)

    def body(*refs):
        for a_ref, o_ref in zip(refs[:n], refs[n:]):
            acc = a_ref[0]
            for j in range(1, 8):
                acc = acc + a_ref[j]
            o_ref[...] = acc

    return pl.pallas_call(body, name=name, out_shape=[jax.ShapeDtypeStruct(a.shape[1:], F32) for a in arrays])(*arrays)


COL_SHARDED = ("ada_w", "w_in", "mla_w_uq", "mla_w_uk", "mla_w_uv", "ffn_up", "ffn_conv_w")
ROW_SHARDED = ("w_out", "ffn_down")
GATHERED = ("ada_w", "w_in", "mla_w_uq", "mla_w_uk", "mla_w_uv", "w_out", "ffn_up", "ffn_down", "ffn_conv_w")
REDUCED = ("w_in", "mla_w_uq", "mla_w_uk", "mla_w_uv", "w_out", "ffn_up", "ffn_down")
SMALL = ("ada_b", "gla_gate_w", "gla_gate_b", "gla_norm_g", "ret_decay", "mla_q_norm_g", "mla_kv_norm_g",
         "ln1_g", "ln1_b", "ffn_conv_b", "ln2_g", "ln2_b")
WEIGHTS = ("c_ctx", "ada_w", "ada_b", "w_in", "gla_gate_w", "gla_gate_b", "gla_norm_g", "ret_decay", "mla_q_norm_g",
           "mla_kv_norm_g", "mla_w_uq", "mla_w_uk", "mla_w_uv", "w_out", "ln1_g", "ln1_b", "ffn_up", "ffn_conv_w",
           "ffn_conv_b", "ffn_down", "ln2_g", "ln2_b")
PACK = 16 * LANES
HALF_LAYERS = DEPTH // 2


def _pad_flat(v, n):
    return jnp.concatenate([v, jnp.zeros((n - v.shape[0],), v.dtype)]) if n > v.shape[0] else v


def _my_layers(a, c):
    return lax.dynamic_slice_in_dim(a, HALF_LAYERS * c, HALF_LAYERS, axis=0)


def gather_weights(shards, c):
    blocks = [_my_layers(shards[n], c) if n == "ffn_conv_w" else _my_layers(shards[n], c).astype(BF) for n in GATHERED]
    got = all_gather8(blocks, "gather_weights")
    out = {}
    for n, g in zip(GATHERED, got):
        _, _, k, n4 = g.shape
        g = g.reshape(4, 2, HALF_LAYERS, k, n4)
        if n in ROW_SHARDED:
            out[n] = jnp.transpose(g, (1, 2, 0, 3, 4)).reshape(DEPTH, 4 * k, n4)
        else:
            out[n] = jnp.transpose(g, (1, 2, 3, 0, 4)).reshape(DEPTH, k, 4 * n4)
    return out


def reduce_gradients(gw, c):
    keep, give, kinds = [], [], []
    for n in REDUCED:
        g = gw[n]
        kind = "row" if n in ROW_SHARDED else ("col" if (g.shape[2] // 4) % LANES == 0 else "pre")
        if kind == "pre":
            g = jnp.transpose(g.reshape(DEPTH, g.shape[1], 4, g.shape[2] // 4), (2, 0, 1, 3))
            keep.append(lax.dynamic_slice_in_dim(g, HALF_LAYERS * c, HALF_LAYERS, axis=1))
            give.append(lax.dynamic_slice_in_dim(g, HALF_LAYERS * (1 - c), HALF_LAYERS, axis=1).astype(BF))
        else:
            keep.append(_my_layers(g, c))
            give.append(_my_layers(g, 1 - c).astype(BF))
        kinds.append(kind)
    got = swap_cores(give, "grad_swap_cores")
    parts = [add_halves(a, b, kind, "grad_add_cores_" + n) for n, a, b, kind in zip(REDUCED, keep, got, kinds)]
    landed = exchange_chips(parts, "grad_exchange_chips")
    mine = [sum_slots(a, "grad_sum_chips_" + n) for n, a in zip(REDUCED, landed)]
    theirs = swap_cores(mine, "grad_swap_back")
    out = {}
    for n, a, b in zip(REDUCED, mine, theirs):
        out[n] = jnp.concatenate([jnp.where(c == 0, a, b), jnp.where(c == 0, b, a)], axis=0)
    return out


def _pack_small(d, names):
    flat = jnp.concatenate([d[n].astype(F32).reshape(-1) for n in names])
    n = -(-flat.shape[0] // PACK) * PACK
    return _pad_flat(flat, n).reshape(n // LANES, LANES)


def _unpack_small(buf, like, names):
    flat, out, at = buf.reshape(-1), {}, 0
    for n in names:
        sz = like[n].size
        out[n] = flat[at:at + sz].reshape(like[n].shape)
        at += sz
    return out


def kernel(x, c, ctx, c_ctx, ada_w, ada_b, w_in, gla_gate_w, gla_gate_b, gla_norm_g, ret_decay, mla_q_norm_g, mla_kv_norm_g, mla_w_uq, mla_w_uk, mla_w_uv, w_out, ln1_g, ln1_b, ffn_up, ffn_conv_w, ffn_conv_b, ffn_down, ln2_g, ln2_b, loss_target, m_c_ctx, m_ada_w, m_ada_b, m_w_in, m_gla_gate_w, m_gla_gate_b, m_gla_norm_g, m_ret_decay, m_mla_q_norm_g, m_mla_kv_norm_g, m_mla_w_uq, m_mla_w_uk, m_mla_w_uv, m_w_out, m_ln1_g, m_ln1_b, m_ffn_up, m_ffn_conv_w, m_ffn_conv_b, m_ffn_down, m_ln2_g, m_ln2_b, v_c_ctx, v_ada_w, v_ada_b, v_w_in, v_gla_gate_w, v_gla_gate_b, v_gla_norm_g, v_ret_decay, v_mla_q_norm_g, v_mla_kv_norm_g, v_mla_w_uq, v_mla_w_uk, v_mla_w_uv, v_w_out, v_ln1_g, v_ln1_b, v_ffn_up, v_ffn_conv_w, v_ffn_conv_b, v_ffn_down, v_ln2_g, v_ln2_b):
    w = dict(c_ctx=c_ctx, ada_w=ada_w, ada_b=ada_b, w_in=w_in, gla_gate_w=gla_gate_w, gla_gate_b=gla_gate_b, gla_norm_g=gla_norm_g, ret_decay=ret_decay, mla_q_norm_g=mla_q_norm_g, mla_kv_norm_g=mla_kv_norm_g, mla_w_uq=mla_w_uq, mla_w_uk=mla_w_uk, mla_w_uv=mla_w_uv, w_out=w_out, ln1_g=ln1_g, ln1_b=ln1_b, ffn_up=ffn_up, ffn_conv_w=ffn_conv_w, ffn_conv_b=ffn_conv_b, ffn_down=ffn_down, ln2_g=ln2_g, ln2_b=ln2_b)
    m = dict(c_ctx=m_c_ctx, ada_w=m_ada_w, ada_b=m_ada_b, w_in=m_w_in, gla_gate_w=m_gla_gate_w, gla_gate_b=m_gla_gate_b, gla_norm_g=m_gla_norm_g, ret_decay=m_ret_decay, mla_q_norm_g=m_mla_q_norm_g, mla_kv_norm_g=m_mla_kv_norm_g, mla_w_uq=m_mla_w_uq, mla_w_uk=m_mla_w_uk, mla_w_uv=m_mla_w_uv, w_out=m_w_out, ln1_g=m_ln1_g, ln1_b=m_ln1_b, ffn_up=m_ffn_up, ffn_conv_w=m_ffn_conv_w, ffn_conv_b=m_ffn_conv_b, ffn_down=m_ffn_down, ln2_g=m_ln2_g, ln2_b=m_ln2_b)
    v = dict(c_ctx=v_c_ctx, ada_w=v_ada_w, ada_b=v_ada_b, w_in=v_w_in, gla_gate_w=v_gla_gate_w, gla_gate_b=v_gla_gate_b, gla_norm_g=v_gla_norm_g, ret_decay=v_ret_decay, mla_q_norm_g=v_mla_q_norm_g, mla_kv_norm_g=v_mla_kv_norm_g, mla_w_uq=v_mla_w_uq, mla_w_uk=v_mla_w_uk, mla_w_uv=v_mla_w_uv, w_out=v_w_out, ln1_g=v_ln1_g, ln1_b=v_ln1_b, ffn_up=v_ffn_up, ffn_conv_w=v_ffn_conv_w, ffn_conv_b=v_ffn_conv_b, ffn_down=v_ffn_down, ln2_g=v_ln2_g, ln2_b=v_ln2_b)
    xi, yi, ci = _place()
    chip = 2 * xi + yi

    full = dict(w)
    full.update(gather_weights({n: w[n] for n in GATHERED}, ci))
    loss, grad_x, gw, dmod, s8 = local_step(x, c, ctx, c_ctx, {n: full[n] for n in WEIGHTS if n != "c_ctx"}, loss_target)
    loss = lax.psum(loss, ("x", "y", "c"))

    dsil = jnp.zeros((8, D_MODEL), F32)
    for i in range(DEPTH):
        dsil = dsil + matmul(dmod[i], full["ada_w"][i], "nt", F32, 8, 1024, 1536, "ada_dsilu")

    grads = reduce_gradients(gw, ci)

    small = {n: gw[n] for n in SMALL if n != "ada_b"}
    small.update(dsil=dsil[4])
    names = tuple(small)
    conv_g = gw["ffn_conv_w"].reshape(DEPTH * 3, 2 * D_FF)
    ev_small, ev_dmod, ev_s8, ev_conv = all_gather8(
        [_pack_small(small, names), dmod.reshape(DEPTH * 8, 6 * D_MODEL), s8, conv_g], "gather_small")
    sm_small, sm_dmod, sm_conv = sum_small([ev_small, ev_dmod, ev_conv], "sum_small")
    summed = _unpack_small(sm_small, small, names)
    for n in SMALL:
        if n != "ada_b":
            grads[n] = summed[n]
    grads["ada_b"] = jnp.sum(sm_dmod.reshape(DEPTH, 8, 6 * D_MODEL)[:, :5], axis=1)
    sg = jax.nn.sigmoid(c_ctx)
    grads["c_ctx"] = summed["dsil"] * (sg * (1.0 + c_ctx * (1.0 - sg)))
    ccols = ffn_conv_w.shape[2]
    grads["ffn_conv_w"] = lax.dynamic_slice_in_dim(sm_conv.reshape(DEPTH, 3, 2 * D_FF), chip * ccols, ccols, axis=2)
    s_all = ev_s8.reshape(64, D_MODEL)
    d_all = jnp.transpose(ev_dmod.reshape(8, DEPTH, 8, 6 * D_MODEL), (1, 0, 2, 3)).reshape(DEPTH, 64, 6 * D_MODEL)
    cols = ada_w.shape[2]
    g_ada = []
    for i in range(DEPTH):
        d_mine = lax.dynamic_slice_in_dim(d_all[i], chip * cols, cols, axis=1)
        g_ada.append(matmul(s_all, d_mine, "tn", F32, 1024, cols, 64, "ada_dw"))
    grads["ada_w"] = jnp.stack(g_ada)

    delta, new_m, new_v = {}, {}, {}
    for n in GATHERED:
        shp = w[n].shape
        v2 = lambda a: a.reshape(-1, shp[-1])
        d_, m_, v_ = adamw(v2(w[n]), v2(grads[n]), v2(m[n]), v2(v[n]), "adamw_" + n)
        delta[n], new_m[n], new_v[n] = d_.reshape(shp), m_.reshape(shp), v_.reshape(shp)
    rep = tuple(n for n in WEIGHTS if n not in GATHERED)
    pk = lambda d: _pack_small({n: d[n] for n in rep}, rep)
    d_, m_, v_ = adamw(pk(w), pk(grads), pk(m), pk(v), "adamw_small")
    like = {n: w[n] for n in rep}
    delta.update(_unpack_small(d_, like, rep))
    new_m.update(_unpack_small(m_, like, rep))
    new_v.update(_unpack_small(v_, like, rep))
    grads = {n: grads[n].reshape(w[n].shape) for n in WEIGHTS}
    return (loss, grad_x, *[grads[n] for n in WEIGHTS], *[delta[n] for n in WEIGHTS], *[new_m[n] for n in WEIGHTS],
            *[new_v[n] for n in WEIGHTS])
```

```python
import functools

import jax
import jax.numpy as jnp
from jax import lax
from jax.experimental import pallas as pl
from jax.experimental.pallas import tpu as pltpu

F32 = jnp.float32
BF = jnp.bfloat16

D_MODEL = 1024
DEPTH = 4
GRID_W = 64
GLA_DK = 32
GLA_TAU = 16.0
RET_DK = 32
MLA_HEADS = 8
MLA_D_NOPE = 64
MLA_D_ROPE = 32
MLA_SCALE = (MLA_D_NOPE + MLA_D_ROPE) ** -0.5
D_FF = 2816
ROPE_BASE = 10000.0
EPS = 1e-6
ALPHA = (2 * DEPTH) ** 0.25
ADAM_LR, ADAM_B1, ADAM_B2, ADAM_EPS, ADAM_WD, ADAM_STEP = 0.001, 0.9, 0.999, 1e-08, 0.01, 10

ROW_TILE = 256
CHUNK = 64
GATE_ROWS = 64
LANES = 128

C_GQ, C_GK, C_GV, C_GG, C_RQ, C_RK, C_RV, C_RG, C_CQ, C_CKV, C_LK = 0, 128, 256, 512, 768, 896, 1024, 1280, 1536, 1792, 1920
D_INP = 2048
D_IN = 1984


def _dg(a, b, ca, cb):
    return lax.dot_general(a.astype(BF), b.astype(BF), (((ca,), (cb,)), ((), ())), preferred_element_type=F32)


@jax.custom_vjp
def mm_nn(a, b):
    return _dg(a, b, 1, 0)


@jax.custom_vjp
def mm_nt(a, b):
    return _dg(a, b, 1, 1)


@jax.custom_vjp
def mm_tn(a, b):
    return _dg(a, b, 0, 0)


mm_nn.defvjp(lambda a, b: (_dg(a, b, 1, 0), (a, b)),
             lambda r, g: (mm_nt(g, r[1]).astype(r[0].dtype), mm_tn(r[0], g).astype(r[1].dtype)))
mm_nt.defvjp(lambda a, b: (_dg(a, b, 1, 1), (a, b)),
             lambda r, g: (mm_nn(g, r[1]).astype(r[0].dtype), mm_tn(g, r[0]).astype(r[1].dtype)))
mm_tn.defvjp(lambda a, b: (_dg(a, b, 0, 0), (a, b)),
             lambda r, g: (mm_nt(r[1], g).astype(r[0].dtype), mm_nn(r[0], g).astype(r[1].dtype)))


def _split3(x):
    h = x.astype(BF)
    r = x - h.astype(F32)
    m = r.astype(BF)
    lo = (r - m.astype(F32)).astype(BF)
    return h, m, lo


def _exact(x, mat, left):
    h, m, lo = _split3(x)
    if left:
        d = lambda t: lax.dot_general(mat, t, (((1,), (0,)), ((), ())), preferred_element_type=F32)
    else:
        d = lambda t: lax.dot_general(t, mat, (((1,), (0,)), ((), ())), preferred_element_type=F32)
    return (d(lo) + d(m)) + d(h)


def _iota(shape, axis):
    return lax.broadcasted_iota(jnp.int32, shape, axis)


def _tri(n, upper):
    r, c = _iota((n, n), 0), _iota((n, n), 1)
    return jnp.where((c >= r) if upper else (r >= c), 1.0, 0.0).astype(BF)


@functools.partial(jax.custom_vjp, nondiff_argnums=(1,))
def cumsum_rows(x, upper):
    return _exact(x, _tri(x.shape[0], upper), True)


cumsum_rows.defvjp(lambda x, upper: (cumsum_rows(x, upper), None),
                   lambda upper, r, g: (cumsum_rows(g, not upper),))


def _seg(n, w):
    shift = w.bit_length() - 1
    r, c = _iota((n, n), 0), _iota((n, n), 1)
    return jnp.where(lax.shift_right_logical(r, shift) == lax.shift_right_logical(c, shift), 1.0, 0.0).astype(BF)


@functools.partial(jax.custom_vjp, nondiff_argnums=(1,))
def seg_sum(x, w):
    return _exact(x, _seg(x.shape[1], w), False)


seg_sum.defvjp(lambda x, w: (seg_sum(x, w), None), lambda w, r, g: (seg_sum(g, w),))


def _place_mat(transpose):
    shape = (8 * LANES, LANES) if transpose else (LANES, 8 * LANES)
    r, c = _iota(shape, 0), _iota(shape, 1)
    src, dst = (c, r) if transpose else (r, c)
    dl = jnp.bitwise_and(dst, LANES - 1)
    ok = (dl >= 64) & (dl < 96) & (src == dl - 32)
    return jnp.where(ok, 1.0, 0.0).astype(BF)


@jax.custom_vjp
def place_kr(x):
    return _exact(x, _place_mat(False), False)


place_kr.defvjp(lambda x: (place_kr(x), None), lambda r, g: (_exact(g, _place_mat(True), False),))


@functools.partial(jax.custom_vjp, nondiff_argnums=(1,))
def lane_roll(x, s):
    return pltpu.roll(x, s, 1)


lane_roll.defvjp(lambda x, s: (pltpu.roll(x, s, 1), None),
                 lambda s, r, g: (pltpu.roll(g, (g.shape[1] - s) % g.shape[1], 1),))


def rope(x, tab, d):
    cos, sa, sb = tab
    return x * cos + lane_roll(x, LANES - d) * sa + lane_roll(x, d) * sb


def silu(x):
    return x * jax.nn.sigmoid(x)


def log_sigmoid(z):
    return jnp.minimum(z, 0.0) - jnp.log(1.0 + jnp.exp(-jnp.abs(z)))


def layer_norm(x, g, b):
    mu = jnp.mean(x, axis=-1, keepdims=True)
    xc = x - mu
    var = jnp.mean(xc * xc, axis=-1, keepdims=True)
    return xc * lax.rsqrt(var + EPS) * g + b


def matmul(a, b, mode, out_dtype, tm, tn, tk, name):
    if mode == "nn":
        (m, k), (k2, n) = a.shape, b.shape
        a_spec = pl.BlockSpec((tm, tk), lambda i, j, kk: (i, kk))
        b_spec = pl.BlockSpec((tk, tn), lambda i, j, kk: (kk, j))
        ca, cb = 1, 0
    elif mode == "nt":
        (m, k), (n, k2) = a.shape, b.shape
        a_spec = pl.BlockSpec((tm, tk), lambda i, j, kk: (i, kk))
        b_spec = pl.BlockSpec((tn, tk), lambda i, j, kk: (j, kk))
        ca, cb = 1, 1
    else:
        (k, m), (k2, n) = a.shape, b.shape
        a_spec = pl.BlockSpec((tk, tm), lambda i, j, kk: (kk, i))
        b_spec = pl.BlockSpec((tk, tn), lambda i, j, kk: (kk, j))
        ca, cb = 0, 0
    assert k == k2 and m % tm == 0 and n % tn == 0 and k % tk == 0, (name, a.shape, b.shape, tm, tn, tk)
    nk = k // tk

    def body(a_ref, b_ref, o_ref, *acc):
        part = _dg(a_ref[...], b_ref[...], ca, cb)
        if nk == 1:
            o_ref[...] = part.astype(o_ref.dtype)
            return
        acc_ref, = acc
        kk = pl.program_id(2)

        @pl.when(kk == 0)
        def _():
            acc_ref[...] = part

        @pl.when(kk > 0)
        def _():
            acc_ref[...] += part

        @pl.when(kk == nk - 1)
        def _():
            o_ref[...] = acc_ref[...].astype(o_ref.dtype)

    return pl.pallas_call(
        body, name=name, grid=(m // tm, n // tn, nk),
        in_specs=[a_spec, b_spec], out_specs=pl.BlockSpec((tm, tn), lambda i, j, kk: (i, j)),
        out_shape=jax.ShapeDtypeStruct((m, n), out_dtype),
        scratch_shapes=[] if nk == 1 else [pltpu.VMEM((tm, tn), F32)],
    )(a, b)


def _stage_specs(rows, tps, consts, ws):
    specs, args = [], []
    for arr, width, cb in rows:
        specs.append(pl.BlockSpec((ROW_TILE, width), functools.partial(lambda i, cb: (i, cb), cb=cb)))
        args.append(arr)
    for arr in tps:
        specs.append(pl.BlockSpec((1, 1, arr.shape[2]), lambda i: (i, 0, 0)))
        args.append(arr)
    for arr, period in consts:
        specs.append(pl.BlockSpec((ROW_TILE, arr.shape[1]), functools.partial(lambda i, p: (i % p, 0), p=period)))
        args.append(arr)
    for arr in ws:
        specs.append(pl.BlockSpec(arr.shape, functools.partial(lambda i, nd: (0,) * nd, nd=arr.ndim)))
        args.append(arr)
    return specs, args


def _stage_load(refs, n_rows, n_tps, n_consts, n_ws):
    it = iter(refs)
    rows = [next(it)[...].astype(F32) for _ in range(n_rows)]
    tps = [next(it)[0].astype(F32) for _ in range(n_tps)]
    consts = [next(it)[...].astype(F32) for _ in range(n_consts)]
    ws = [next(it)[...].astype(F32) for _ in range(n_ws)]
    return rows, tps, consts, ws


def stage_fwd(fn, rows, tps, consts, ws, outs, name):
    n_tiles = rows[0][0].shape[0] // ROW_TILE
    specs, args = _stage_specs(rows, tps, consts, ws)
    counts = (len(rows), len(tps), len(consts), len(ws))

    def body(*refs):
        r, t, c, w = _stage_load(refs[:sum(counts)], *counts)
        res = fn(r, t, c, w)
        for o_ref, o in zip(refs[sum(counts):], res):
            o_ref[...] = o.astype(o_ref.dtype)

    res = pl.pallas_call(
        body, name=name, grid=(n_tiles,), in_specs=specs,
        out_specs=[pl.BlockSpec((ROW_TILE, wd), lambda i: (i, 0)) for wd, _ in outs],
        out_shape=[jax.ShapeDtypeStruct((n_tiles * ROW_TILE, wd), dt) for wd, dt in outs],
    )(*args)
    return list(res)


def stage_bwd(fn, rows, tps, consts, ws, cts, row_grads, name):
    n_tiles = rows[0][0].shape[0] // ROW_TILE
    specs, args = _stage_specs(rows, tps, consts, ws)
    counts = (len(rows), len(tps), len(consts), len(ws))
    n_in = sum(counts)
    for ct in cts:
        specs.append(pl.BlockSpec((ROW_TILE, ct.shape[1]), lambda i: (i, 0)))
        args.append(ct)
    want = [k for k, dt in enumerate(row_grads) if dt is not None]
    out_specs = [pl.BlockSpec((ROW_TILE, rows[k][1]), lambda i: (i, 0)) for k in want]
    out_shape = [jax.ShapeDtypeStruct((n_tiles * ROW_TILE, rows[k][1]), row_grads[k]) for k in want]
    out_specs += [pl.BlockSpec((1, 1, a.shape[2]), lambda i: (i, 0, 0)) for a in tps]
    out_shape += [jax.ShapeDtypeStruct((n_tiles, 1, a.shape[2]), F32) for a in tps]
    out_specs += [pl.BlockSpec(a.shape, functools.partial(lambda i, nd: (0,) * nd, nd=a.ndim)) for a in ws]
    out_shape += [jax.ShapeDtypeStruct(a.shape, F32) for a in ws]

    def body(*refs):
        r, t, c, w = _stage_load(refs[:n_in], *counts)
        g = [ref[...].astype(F32) for ref in refs[n_in:n_in + len(cts)]]
        _, vjp = jax.vjp(lambda r_, t_, w_: fn(r_, t_, c, w_), r, t, w)
        dr, dt, dw = vjp(g)
        o = iter(refs[n_in + len(cts):])
        for k in want:
            ref = next(o)
            ref[...] = dr[k].astype(ref.dtype)
        for v in dt:
            next(o)[0] = v
        first = pl.program_id(0) == 0
        for v in dw:
            ref = next(o)

            @pl.when(first)
            def _():
                ref[...] = v

            @pl.when(jnp.logical_not(first))
            def _():
                ref[...] += v

    res = pl.pallas_call(body, name=name, grid=(n_tiles,), in_specs=specs, out_specs=out_specs, out_shape=out_shape)(*args)
    res = list(res)
    drows = [None] * len(rows)
    for k in want:
        drows[k] = res.pop(0)
    dtps = [res.pop(0) for _ in tps]
    dws = [res.pop(0) for _ in ws]
    return drows, dtps, dws


def fn_modulate(rows, tps, consts, ws):
    (x,), (sc, sh) = rows, tps
    return [x * (1.0 + sc) + sh]


def fn_post(rows, tps, consts, ws):
    (x, a), (g,), (lng, lnb) = rows, tps, ws
    return [layer_norm(ALPHA * x + g * a, lng, lnb)]


def fn_mix(rows, tps, consts, ws):
    ogf, ogb, orf, orb, pg, pr, mo = rows
    gng, = ws
    og = ogf + ogb
    out_g = og * lax.rsqrt(seg_sum(og * og, 64) * (1.0 / 64) + EPS) * gng * silu(pg)
    o = orf + orb
    oc = o - seg_sum(o, 64) * (1.0 / 64)
    out_r = oc * lax.rsqrt(seg_sum(oc * oc, 64) * (1.0 / 64) + EPS) * silu(pr)
    return [jnp.concatenate([out_g, out_r, mo], axis=-1)]


def fn_mla_prep(rows, tps, consts, ws):
    pq, pkv, plk = rows
    gq, gkv, wuq, wuk, wuv = ws
    qtab, ktab = consts[0:3], consts[3:6]
    cq = pq * lax.rsqrt(jnp.mean(pq * pq, axis=-1, keepdims=True) + EPS) * gq
    qp = mm_nn(cq, wuq)
    q = jnp.concatenate([rope(qp[:, h * LANES:(h + 1) * LANES], qtab, 8) for h in range(MLA_HEADS)], axis=-1)
    ckv = pkv * lax.rsqrt(jnp.mean(pkv * pkv, axis=-1, keepdims=True) + EPS) * gkv
    k = mm_nn(ckv, wuk) + place_kr(rope(plk, ktab, 8))
    v = mm_nn(ckv, wuv)
    return [q, k, v]


def fn_assemble(rows, tps, consts, ws):
    gq, gk, gv, gg, rq, rk, rv, rg, cq, ckv, lk1, lk2 = rows
    return [jnp.concatenate([gq, gk, gv, gg, rq, rk, rv, rg, cq, ckv, lk1 + lk2], axis=-1)]


def _head_masks():
    hm = (lax.shift_right_logical(_iota((4, 1, LANES), 2), 5) == _iota((4, 1, LANES), 0)).astype(F32)
    vm = (lax.shift_right_logical(_iota((4, 1, 256), 2), 6) == _iota((4, 1, 256), 0)).astype(F32)
    bd = (lax.shift_right_logical(_iota((256, LANES), 0), 6) == lax.shift_right_logical(_iota((256, LANES), 1), 5)).astype(F32)
    return hm, vm, bd


def chunk_step(s, q, k, v, la, upper):
    hm, vm, bd = _head_masks()
    t, u = _iota((4 * CHUNK, CHUNK), 0), _iota((4 * CHUNK, CHUNK), 1)
    t = jnp.bitwise_and(t, CHUNK - 1)
    causal = (u >= t) if upper else (t >= u)
    b = cumsum_rows(la, upper)
    bend = jnp.sum(la, axis=0, keepdims=True)
    half = 0.5 * bend
    qd = q * jnp.exp(b - half)
    kd = k * jnp.exp(half - b)
    qe = (qd[None] * hm).reshape(4 * CHUNK, LANES)
    att = jnp.where(causal, mm_nt(qe, kd), 0.0)
    o_intra = (mm_nn(att, v).reshape(4, CHUNK, 256) * vm).sum(0)
    o = mm_nt(q * jnp.exp(b), s) + o_intra
    s_new = (s * jnp.exp(bend) + mm_tn(v, k * jnp.exp(bend - b))) * bd
    return o, s_new


def scan_step(sg, sr, q, k, v, lrk, rq, rk, rv, gw, gb, rdec, tab, upper):
    la_g = log_sigmoid(mm_nn(lrk, gw) + gb) * (1.0 / GLA_TAU)
    og, sg2 = chunk_step(sg, q * GLA_DK ** -0.5, k, v, la_g, upper)
    la_r = jnp.broadcast_to(log_sigmoid(rdec), (CHUNK, LANES))
    orr, sr2 = chunk_step(sr, rope(rq, tab, 16), rope(rk * RET_DK ** -0.5, tab, 16), rv, la_r, upper)
    return og, orr, sg2, sr2


def _chunk_of(n, ncc, nch, reverse):
    if not reverse:
        return n
    return jnp.where(n < ncc, ncc - 1 - n, nch - 1 + ncc - n)


def _scan_in_specs(p3, tabs, gw, gb, rdec, cidx):
    nb = p3.shape[0]

    def blk(width, cb):
        return pl.BlockSpec((nb, CHUNK, width), lambda m: (0, cidx(m), cb))

    specs = [blk(128, C_GQ // 128), blk(128, C_GK // 128), blk(256, C_GV // 256), blk(128, C_LK // 128),
             blk(128, C_RQ // 128), blk(128, C_RK // 128), blk(256, C_RV // 256)]
    args = [p3] * 7
    for t in tabs:
        specs.append(pl.BlockSpec((CHUNK, LANES), lambda m: (cidx(m), 0)))
        args.append(t)
    for w in (gw, gb, rdec):
        specs.append(pl.BlockSpec(w.shape, lambda m: (0, 0)))
        args.append(w)
    return specs, args


def scan_fwd(p, tabs, gw, gb, rdec, nb, ncc, nch, reverse, name):
    cidx = lambda n: _chunk_of(n, ncc, nch, reverse)
    t = p.shape[0]
    specs, args = _scan_in_specs(p.reshape(nb, t // nb, p.shape[1]), tabs, gw, gb, rdec, cidx)

    def body(q, k, v, lrk, rq, rk, rv, tc, ta, tb, gw_r, gb_r, rd_r, og_r, or_r, sgo_r, sro_r, sg, sr):
        @pl.when(pl.program_id(0) == 0)
        def _():
            sg[...] = jnp.zeros_like(sg)
            sr[...] = jnp.zeros_like(sr)

        sgo_r[0] = sg[...]
        sro_r[0] = sr[...]
        ld = lambda r: r[...].astype(F32)
        tab, gw_, gb_, rd_ = (ld(tc), ld(ta), ld(tb)), ld(gw_r), ld(gb_r), ld(rd_r)
        for b in range(nb):
            lb = lambda r: r[b].astype(F32)
            og, orr, sg2, sr2 = scan_step(sg[b], sr[b], lb(q), lb(k), lb(v), lb(lrk), lb(rq), lb(rk), lb(rv),
                                          gw_, gb_, rd_, tab, reverse)
            og_r[b] = og
            or_r[b] = orr
            sg[b] = sg2
            sr[b] = sr2

    row_out = pl.BlockSpec((nb, CHUNK, 256), lambda n: (0, cidx(n), 0))
    st_out = pl.BlockSpec((1, nb, 256, LANES), lambda n: (n, 0, 0, 0))
    og, orr, sgs, srs = pl.pallas_call(
        body, name=name, grid=(nch,), in_specs=specs, out_specs=[row_out, row_out, st_out, st_out],
        out_shape=[jax.ShapeDtypeStruct((nb, t // nb, 256), F32)] * 2 + [jax.ShapeDtypeStruct((nch, nb, 256, LANES), F32)] * 2,
        scratch_shapes=[pltpu.VMEM((nb, 256, LANES), F32)] * 2,
    )(*args)
    return og.reshape(t, 256), orr.reshape(t, 256), sgs, srs


def scan_bwd(p, tabs, gw, gb, rdec, sg_in, sr_in, dog, dor, prev, nb, ncc, nch, reverse, name):
    step = lambda m: nch - 1 - m
    cidx = lambda m: _chunk_of(step(m), ncc, nch, reverse)
    t = p.shape[0]
    lt = t // nb
    specs, args = _scan_in_specs(p.reshape(nb, lt, p.shape[1]), tabs, gw, gb, rdec, cidx)
    st_spec = pl.BlockSpec((1, nb, 256, LANES), lambda m: (step(m), 0, 0, 0))
    specs += [st_spec, st_spec]
    args += [sg_in, sr_in]
    row = lambda width: pl.BlockSpec((nb, CHUNK, width), lambda m: (0, cidx(m), 0))
    specs += [row(256), row(256)]
    args += [dog.reshape(nb, lt, 256), dor.reshape(nb, lt, 256)]
    widths = (128, 128, 256, 128, 128, 128, 256)
    if prev is not None:
        specs += [row(wd) for wd in widths]
        args += [a.reshape(nb, lt, a.shape[1]) for a in prev]
    n_prev = 0 if prev is None else 7

    def body(*refs):
        (q, k, v, lrk, rq, rk, rv, tc, ta, tb, gw_r, gb_r, rd_r, sgi, sri, dog_r, dor_r), rest = refs[:17], refs[17:]
        prev_r, rest = rest[:n_prev], rest[n_prev:]
        outs, (dgw_r, dgb_r, drd_r, dsg, dsr) = rest[:7], rest[7:]
        first = pl.program_id(0) == 0

        @pl.when(first)
        def _():
            dsg[...] = jnp.zeros_like(dsg)
            dsr[...] = jnp.zeros_like(dsr)

        ld = lambda r: r[...].astype(F32)
        tab, gw_, gb_, rd_ = (ld(tc), ld(ta), ld(tb)), ld(gw_r), ld(gb_r), ld(rd_r)
        wsum = None
        for b in range(nb):
            lb = lambda r: r[b].astype(F32)
            prim = (sgi[0, b], sri[0, b], lb(q), lb(k), lb(v), lb(lrk), lb(rq), lb(rk), lb(rv), gw_, gb_, rd_)
            _, vjp = jax.vjp(lambda *a: scan_step(*a, tab, reverse), *prim)
            g = vjp((lb(dog_r), lb(dor_r), dsg[b], dsr[b]))
            dsg[b] = g[0]
            dsr[b] = g[1]
            for j in range(7):
                val = g[2 + j]
                if n_prev:
                    val = val + prev_r[j][b]
                outs[j][b] = val
            wsum = g[9:12] if wsum is None else tuple(a + c for a, c in zip(wsum, g[9:12]))
        for ref, val in zip((dgw_r, dgb_r, drd_r), wsum):
            @pl.when(first)
            def _():
                ref[...] = val

            @pl.when(jnp.logical_not(first))
            def _():
                ref[...] += val

    wspec = lambda w: pl.BlockSpec(w.shape, lambda m: (0, 0))
    res = pl.pallas_call(
        body, name=name, grid=(nch,), in_specs=specs,
        out_specs=[row(wd) for wd in widths] + [wspec(gw), wspec(gb), wspec(rdec)],
        out_shape=[jax.ShapeDtypeStruct((nb, lt, wd), F32) for wd in widths]
        + [jax.ShapeDtypeStruct(w.shape, F32) for w in (gw, gb, rdec)],
        scratch_shapes=[pltpu.VMEM((nb, 256, LANES), F32)] * 2,
    )(*args)
    return tuple(a.reshape(t, a.shape[2]) for a in res[:7]), res[7], res[8], res[9]


def _attn_tiles(lc, lt):
    nct = lc // ROW_TILE
    return nct, (lt - lc) // ROW_TILE


def _attn_loop(tile, lc, lt):
    nct, nlt = _attn_tiles(lc, lt)
    for i in range(nct):
        tile(i * ROW_TILE, lc)

    def lat(i, carry):
        tile(pl.multiple_of(lc + i * ROW_TILE, ROW_TILE), lt)
        return carry

    lax.fori_loop(0, nlt, lat, 0)


def mla_fwd(q, k, v, nb, lc, lt, name):
    def body(q_ref, k_ref, v_ref, o_ref, lse_ref):
        def tile(r0, nk):
            rows = pl.ds(r0, ROW_TILE)
            lane = _iota((ROW_TILE, LANES), 1)
            outs, lse = [], jnp.zeros((ROW_TILE, LANES), F32)
            for j in range(2):
                s = _dg(q_ref[rows, j * LANES:(j + 1) * LANES], k_ref[0:nk, j * LANES:(j + 1) * LANES], 1, 1) * MLA_SCALE
                m = jnp.max(s, axis=-1, keepdims=True)
                p = jnp.exp(s - m)
                l = jnp.sum(p, axis=-1, keepdims=True)
                outs.append(_dg(p * (1.0 / l), v_ref[0:nk, j * 64:(j + 1) * 64], 1, 0))
                lse = jnp.where(lane == j, m + jnp.log(l), lse)
            o_ref[rows, :] = jnp.concatenate(outs, axis=-1)
            lse_ref[rows, :] = lse

        _attn_loop(tile, lc, lt)

    pair = lambda width: pl.BlockSpec((lt, width), lambda b, h: (b, h))
    return pl.pallas_call(
        body, name=name, grid=(nb, MLA_HEADS // 2), in_specs=[pair(2 * LANES), pair(2 * LANES), pair(LANES)],
        out_specs=[pair(LANES), pair(LANES)],
        out_shape=[jax.ShapeDtypeStruct((nb * lt, MLA_HEADS * 64), F32), jax.ShapeDtypeStruct((nb * lt, MLA_HEADS // 2 * LANES), F32)],
    )(q, k, v)


def mla_bwd(q, k, v, o, lse, do, nb, lc, lt, name):
    def body(q_ref, k_ref, v_ref, o_ref, lse_ref, do_ref, dq_ref, dk_ref, dv_ref, dka, dva):
        dka[...] = jnp.zeros_like(dka)
        dva[...] = jnp.zeros_like(dva)

        def tile(r0, nk):
            rows = pl.ds(r0, ROW_TILE)
            dqs = []
            for j in range(2):
                qj, kj = q_ref[rows, j * LANES:(j + 1) * LANES], k_ref[0:nk, j * LANES:(j + 1) * LANES]
                vj, doj = v_ref[0:nk, j * 64:(j + 1) * 64], do_ref[rows, j * 64:(j + 1) * 64]
                p = jnp.exp(_dg(qj, kj, 1, 1) * MLA_SCALE - lse_ref[rows, j:j + 1])
                dsum = jnp.sum(doj * o_ref[rows, j * 64:(j + 1) * 64], axis=-1, keepdims=True)
                ds = p * (_dg(doj, vj, 1, 1) - dsum) * MLA_SCALE
                dqs.append(_dg(ds, kj, 1, 0))
                dka[j, 0:nk, :] += _dg(ds, qj, 0, 0)
                dva[j, 0:nk, :] += _dg(p, doj, 0, 0)
            dq_ref[rows, :] = jnp.concatenate(dqs, axis=-1)

        _attn_loop(tile, lc, lt)
        dk_ref[...] = jnp.concatenate([dka[0], dka[1]], axis=-1)
        dv_ref[...] = jnp.concatenate([dva[0], dva[1]], axis=-1)

    t = nb * lt
    pair = lambda width: pl.BlockSpec((lt, width), lambda b, h: (b, h))
    return pl.pallas_call(
        body, name=name, grid=(nb, MLA_HEADS // 2),
        in_specs=[pair(2 * LANES), pair(2 * LANES), pair(LANES), pair(LANES), pair(LANES), pair(LANES)],
        out_specs=[pair(2 * LANES), pair(2 * LANES), pair(LANES)],
        out_shape=[jax.ShapeDtypeStruct((t, MLA_HEADS * LANES), F32), jax.ShapeDtypeStruct((t, MLA_HEADS * LANES), F32),
                   jax.ShapeDtypeStruct((t, MLA_HEADS * 64), F32)],
        scratch_shapes=[pltpu.VMEM((2, lt, LANES), F32), pltpu.VMEM((2, lt, 64), F32)],
    )(q, k, v, o, lse, do)


def _gate_specs(u, per_batch, lc):
    g8 = GATE_ROWS // 8
    n8 = u.shape[0] // 8
    width = u.shape[1]
    main = pl.BlockSpec((GATE_ROWS, width), lambda i: (i, 0))
    prev = pl.BlockSpec((8, width), lambda i: (jnp.maximum(i * g8 - 1, 0), 0))
    nxt = pl.BlockSpec((8, width), lambda i: (jnp.minimum((i + 1) * g8, n8 - 1), 0))
    return main, prev, nxt


def _seg_edges(per_batch, lc):
    j = pl.program_id(0) % (per_batch // GATE_ROWS)
    first = (j == 0) | (j == lc // GATE_ROWS)
    last = (j == lc // GATE_ROWS - 1) | (j == per_batch // GATE_ROWS - 1)
    return first, last


def _shifted(x, prev_ref, next_ref, first, last):
    rows = _iota(x.shape, 0)
    before = jnp.where(first, 0.0, prev_ref[7:8, :].astype(F32))
    after = jnp.where(last, 0.0, next_ref[0:1, :].astype(F32))
    xm = jnp.where(rows == 0, before, pltpu.roll(x, 1, 0))
    xp = jnp.where(rows == x.shape[0] - 1, after, pltpu.roll(x, x.shape[0] - 1, 0))
    return xm, xp


def gate_fwd(u, cw, cb, per_batch, lc, name):
    main, prev, nxt = _gate_specs(u, per_batch, lc)
    f = u.shape[1] // 2

    def body(u_ref, p_ref, n_ref, w_ref, b_ref, act_ref):
        first, last = _seg_edges(per_batch, lc)
        x = u_ref[...]
        xm, xp = _shifted(x, p_ref, n_ref, first, last)
        c = w_ref[0:1, :] * xm + w_ref[1:2, :] * x + w_ref[2:3, :] * xp + b_ref[...]
        act_ref[...] = (silu(c[:, :f]) * c[:, f:]).astype(act_ref.dtype)

    return pl.pallas_call(
        body, name=name, grid=(u.shape[0] // GATE_ROWS,),
        in_specs=[main, prev, nxt, pl.BlockSpec(cw.shape, lambda i: (0, 0)), pl.BlockSpec(cb.shape, lambda i: (0, 0))],
        out_specs=pl.BlockSpec((GATE_ROWS, f), lambda i: (i, 0)),
        out_shape=jax.ShapeDtypeStruct((u.shape[0], f), BF),
    )(u, u, u, cw, cb)


def gate_bwd(u, cw, cb, dact, per_batch, lc, name):
    main, prev, nxt = _gate_specs(u, per_batch, lc)
    f = u.shape[1] // 2

    def body(u_ref, p_ref, n_ref, w_ref, b_ref, da_ref, dc_ref, dw_ref):
        first, last = _seg_edges(per_batch, lc)
        x = u_ref[...]
        xm, xp = _shifted(x, p_ref, n_ref, first, last)
        c = w_ref[0:1, :] * xm + w_ref[1:2, :] * x + w_ref[2:3, :] * xp + b_ref[...]
        a, g = c[:, :f], c[:, f:]
        sg = jax.nn.sigmoid(a)
        da = da_ref[...]
        dc = jnp.concatenate([da * g * (sg * (1.0 + a * (1.0 - sg))), da * (a * sg)], axis=-1)
        dc_ref[...] = dc
        part = jnp.concatenate([jnp.sum(xm * dc, axis=0, keepdims=True), jnp.sum(x * dc, axis=0, keepdims=True),
                                jnp.sum(xp * dc, axis=0, keepdims=True), jnp.sum(dc, axis=0, keepdims=True),
                                jnp.zeros((4, 2 * f), F32)], axis=0)

        @pl.when(pl.program_id(0) == 0)
        def _():
            dw_ref[...] = part

        @pl.when(pl.program_id(0) > 0)
        def _():
            dw_ref[...] += part

    return pl.pallas_call(
        body, name=name, grid=(u.shape[0] // GATE_ROWS,),
        in_specs=[main, prev, nxt, pl.BlockSpec(cw.shape, lambda i: (0, 0)), pl.BlockSpec(cb.shape, lambda i: (0, 0)),
                  pl.BlockSpec((GATE_ROWS, f), lambda i: (i, 0))],
        out_specs=[main, pl.BlockSpec((8, 2 * f), lambda i: (0, 0))],
        out_shape=[jax.ShapeDtypeStruct(u.shape, F32), jax.ShapeDtypeStruct((8, 2 * f), F32)],
    )(u, u, u, cw, cb, dact)


def conv_transpose(dc, cw, per_batch, lc, name):
    main, prev, nxt = _gate_specs(dc, per_batch, lc)

    def body(d_ref, p_ref, n_ref, w_ref, du_ref):
        first, last = _seg_edges(per_batch, lc)
        x = d_ref[...]
        xm, xp = _shifted(x, p_ref, n_ref, first, last)
        du_ref[...] = (w_ref[0:1, :] * xp + w_ref[1:2, :] * x + w_ref[2:3, :] * xm).astype(du_ref.dtype)

    return pl.pallas_call(
        body, name=name, grid=(dc.shape[0] // GATE_ROWS,),
        in_specs=[main, prev, nxt, pl.BlockSpec(cw.shape, lambda i: (0, 0))],
        out_specs=main, out_shape=jax.ShapeDtypeStruct(dc.shape, BF),
    )(dc, dc, dc, cw)


def loss_head(x, target, tiles_per_batch, ctx_tiles, name):
    n_tiles = x.shape[0] // ROW_TILE
    lat_tiles = tiles_per_batch - ctx_tiles

    def tgt_idx(i):
        j = i % tiles_per_batch
        return jnp.where(j < ctx_tiles, 0, (i // tiles_per_batch) * lat_tiles + j - ctx_tiles), 0

    def body(x_ref, t_ref, dx_ref, l_ref):
        lat = (pl.program_id(0) % tiles_per_batch >= ctx_tiles).astype(F32)
        err = (x_ref[...] - t_ref[...]) * lat
        dx_ref[...] = err * (1.0 / D_MODEL)
        l_ref[...] = jnp.full(l_ref.shape, 0.5 / D_MODEL * jnp.sum(err * err), F32)

    return pl.pallas_call(
        body, name=name, grid=(n_tiles,),
        in_specs=[pl.BlockSpec((ROW_TILE, D_MODEL), lambda i: (i, 0)), pl.BlockSpec((ROW_TILE, D_MODEL), tgt_idx)],
        out_specs=[pl.BlockSpec((ROW_TILE, D_MODEL), lambda i: (i, 0)), pl.BlockSpec((1, 8, LANES), lambda i: (i, 0, 0))],
        out_shape=[jax.ShapeDtypeStruct(x.shape, F32), jax.ShapeDtypeStruct((n_tiles, 8, LANES), F32)],
    )(x, target)


def adamw(w, g, m, v, name):
    rows, cols = w.shape
    tr = rows
    for cand in (512, 256, 128, 64, 32, 16, 8):
        if rows % cand == 0 and cand * cols * 4 <= (1 << 20):
            tr = cand
            break

    def body(w_ref, g_ref, m_ref, v_ref, d_ref, mo_ref, vo_ref):
        gg = g_ref[...]
        m2 = ADAM_B1 * m_ref[...] + (1.0 - ADAM_B1) * gg
        v2 = ADAM_B2 * v_ref[...] + (1.0 - ADAM_B2) * (gg * gg)
        m_hat = m2 / (1.0 - ADAM_B1 ** ADAM_STEP)
        v_hat = v2 / (1.0 - ADAM_B2 ** ADAM_STEP)
        d_ref[...] = -ADAM_LR * (m_hat / (jnp.sqrt(v_hat) + ADAM_EPS) + ADAM_WD * w_ref[...])
        mo_ref[...] = m2
        vo_ref[...] = v2

    spec = pl.BlockSpec((tr, cols), lambda i: (i, 0))
    return pl.pallas_call(body, name=name, grid=(rows // tr,), in_specs=[spec] * 4, out_specs=[spec] * 3,
                          out_shape=[jax.ShapeDtypeStruct(w.shape, F32)] * 3)(w, g, m, v)


def fn_post_mod(rows, tps, consts, ws):
    (x, a), (g, sc, sh), (lng, lnb) = rows, tps, ws
    y = layer_norm(ALPHA * x + g * a, lng, lnb)
    return [y, y * (1.0 + sc) + sh]


def _pick(n, cands):
    for c in cands:
        if n % c == 0:
            return c
    return n


def rope_tables(lc, l):
    pos = jnp.arange(l, dtype=F32)
    ret_inv = 1.0 / (ROPE_BASE ** jnp.linspace(0.0, 1.0, RET_DK // 2, dtype=F32))
    ang = pos[:, None] * ret_inv
    rc, rs = jnp.cos(ang), jnp.sin(ang)
    n_ax = MLA_D_ROPE // 4
    ax_inv = ROPE_BASE ** (-jnp.arange(n_ax, dtype=F32) / n_ax)
    rows_n = l // GRID_W
    rows = jnp.repeat(jnp.arange(rows_n, dtype=F32), GRID_W)
    cols = jnp.tile(jnp.arange(GRID_W, dtype=F32), rows_n)
    ra, ca = rows[:, None] * ax_inv, cols[:, None] * ax_inv
    rwc, rws, clc, cls = jnp.cos(ra), jnp.sin(ra), jnp.cos(ca), jnp.sin(ca)
    one = lambda n: jnp.ones((l, n), F32)
    zero = lambda n: jnp.zeros((l, n), F32)
    cat = lambda parts: jnp.concatenate(parts, axis=1)

    def with_ctx(tab, is_cos):
        head = jnp.ones((lc, LANES), F32) if is_cos else jnp.zeros((lc, LANES), F32)
        return jnp.concatenate([head, tab], axis=0)

    ret = (cat([rc, rc] * 4), cat([-rs, zero(16)] * 4), cat([zero(16), rs] * 4))
    ax_c = [rwc, rwc, clc, clc]
    ax_a = [-rws, zero(8), -cls, zero(8)]
    ax_b = [zero(8), rws, zero(8), cls]
    qt = (cat([one(64)] + ax_c + [one(32)]), cat([zero(64)] + ax_a + [zero(32)]), cat([zero(64)] + ax_b + [zero(32)]))
    kt = (cat([one(32)] + ax_c + [one(64)]), cat([zero(32)] + ax_a + [zero(64)]), cat([zero(32)] + ax_b + [zero(64)]))
    fix = lambda t3: tuple(with_ctx(t, k == 0) for k, t in enumerate(t3))
    return fix(ret), fix(qt), fix(kt)


_IN_ORDER = ((0, 128), (128, 256), (256, 512), (544, 800), (800, 928), (928, 1056), (1056, 1312), (1312, 1568),
             (1568, 1824), (1824, 1952), (512, 544), (1952, 1984))


def permute_w_in(w):
    parts = [w[:, a:b] for a, b in _IN_ORDER] + [jnp.zeros((w.shape[0], D_INP - D_IN), w.dtype)]
    return jnp.concatenate(parts, axis=1)


def unpermute_w_in(g):
    out, at = {}, 0
    for a, b in _IN_ORDER:
        out[a] = g[:, at:at + b - a]
        at += b - a
    return jnp.concatenate([out[a] for a in sorted(out)], axis=1)


def layer_weights(w, l):
    f = lambda a: a.astype(F32)
    r = {}
    r["win"] = permute_w_in(w["w_in"][l]).astype(BF)
    r["wout"] = w["w_out"][l].astype(BF)
    r["wup"] = w["ffn_up"][l].astype(BF)
    r["wdown"] = w["ffn_down"][l].astype(BF)
    uq = w["mla_w_uq"][l].reshape(256, MLA_HEADS, 96)
    r["wuq"] = jnp.pad(uq, ((0, 0), (0, 0), (0, 32))).reshape(256, 8 * LANES).astype(BF)
    uk = w["mla_w_uk"][l].reshape(128, MLA_HEADS, 64)
    r["wuk"] = jnp.pad(uk, ((0, 0), (0, 0), (0, 64))).reshape(128, 8 * LANES).astype(BF)
    r["wuv"] = w["mla_w_uv"][l].astype(BF)
    gw = f(w["gla_gate_w"][l])
    z16 = jnp.zeros((16, LANES), F32)
    z96 = jnp.zeros((96, LANES), F32)
    r["gw"] = (jnp.concatenate([gw[0], z16, z96], axis=0), jnp.concatenate([z16, gw[1], z96], axis=0))
    r["gb"] = tuple(f(w["gla_gate_b"][l][d]).reshape(1, LANES) for d in range(2))
    r["rdec"] = tuple(jnp.repeat(f(w["ret_decay"][l][d]), 32).reshape(1, LANES) for d in range(2))
    r["gng"] = jnp.tile(f(w["gla_norm_g"][l]), 4).reshape(1, 256)
    r["gq"] = f(w["mla_q_norm_g"][l]).reshape(1, 256)
    r["gkv"] = f(w["mla_kv_norm_g"][l]).reshape(1, 128)
    for n in ("ln1_g", "ln1_b", "ln2_g", "ln2_b"):
        r[n] = f(w[n][l]).reshape(1, D_MODEL)
    r["cw"] = f(w["ffn_conv_w"][l])
    r["cb"] = f(w["ffn_conv_b"][l]).reshape(1, 2 * D_FF)
    return r


def tile_params(mod_l, nb, nct, nlt):
    m6 = mod_l.reshape(8, 6, D_MODEL)
    out = []
    for j in range(6):
        parts = []
        for b in range(nb):
            parts.append(jnp.broadcast_to(m6[4, j], (nct, 1, D_MODEL)))
            parts.append(jnp.broadcast_to(m6[b, j], (nlt, 1, D_MODEL)))
        out.append(jnp.concatenate(parts, axis=0))
    return out


def tile_param_grads(dts, nb, nct, nlt):
    cols = []
    for dt in dts:
        d = dt.reshape(nb, nct + nlt, D_MODEL)
        lat = jnp.sum(d[:, nct:], axis=1)
        ctx = jnp.sum(d[:, :nct], axis=(0, 1))
        cols.append(jnp.concatenate([lat, jnp.zeros((4 - nb, D_MODEL), F32), ctx[None], jnp.zeros((3, D_MODEL), F32)], axis=0))
    return jnp.stack(cols, axis=1).reshape(8, 6 * D_MODEL)


def layer_forward(x, h1, tp, tp_next, lw, tabs, dims, tag):
    nb, lc, lt = dims
    t = x.shape[0]
    nbt = lt // ROW_TILE
    ncc, nch = lc // CHUNK, lt // CHUNK
    tm = _pick(t, (1024, 768, 512, 256))
    ret_tab, q_tab, k_tab = tabs
    full = lambda a: (a, a.shape[1], 0)
    p = matmul(h1, lw["win"], "nn", F32, tm, 1024, 1024, "proj_in")
    ogf, orf, sgf, srf = scan_fwd(p, ret_tab, lw["gw"][0], lw["gb"][0], lw["rdec"][0], nb, ncc, nch, False, "scan_fwd_f")
    ogb, orb, sgb, srb = scan_fwd(p, ret_tab, lw["gw"][1], lw["gb"][1], lw["rdec"][1], nb, ncc, nch, True, "scan_fwd_b")
    prep_rows = [(p, 256, C_CQ // 256), (p, 128, C_CKV // 128), (p, 128, C_LK // 128)]
    prep_consts = [(a, nbt) for a in q_tab + k_tab]
    prep_ws = [lw["gq"], lw["gkv"], lw["wuq"], lw["wuk"], lw["wuv"]]
    q, k, v = stage_fwd(fn_mla_prep, prep_rows, [], prep_consts, prep_ws, [(1024, BF), (1024, BF), (512, BF)], "mla_prep")
    mo, lse = mla_fwd(q, k, v, nb, lc, lt, "mla_attn")
    mix_rows = [full(ogf), full(ogb), full(orf), full(orb), (p, 256, C_GG // 256), (p, 256, C_RG // 256), full(mo)]
    m, = stage_fwd(fn_mix, mix_rows, [], [], [lw["gng"]], [(1024, BF)], "mix")
    a = matmul(m, lw["wout"], "nn", F32, tm, 1024, 1024, "proj_out")
    x1, h2 = stage_fwd(fn_post_mod, [full(x), full(a)], [tp[2], tp[4], tp[3]], [], [lw["ln1_g"], lw["ln1_b"]],
                       [(1024, F32), (1024, BF)], "post1")
    u = matmul(h2, lw["wup"], "nn", F32, tm, 1408, 1024, "ffn_up")
    act = gate_fwd(u, lw["cw"], lw["cb"], lt, lc, "ffn_gate")
    f = matmul(act, lw["wdown"], "nn", F32, tm, 1024, 1408, "ffn_down")
    if tp_next is None:
        x2, = stage_fwd(fn_post, [full(x1), full(f)], [tp[5]], [], [lw["ln2_g"], lw["ln2_b"]], [(1024, F32)], "post2_last")
        h1n = None
    else:
        x2, h1n = stage_fwd(fn_post_mod, [full(x1), full(f)], [tp[5], tp_next[0], tp_next[1]], [],
                            [lw["ln2_g"], lw["ln2_b"]], [(1024, F32), (1024, BF)], "post2")
    res = dict(x=x, h1=h1, p=p, ogf=ogf, orf=orf, sgf=sgf, srf=srf, ogb=ogb, orb=orb, sgb=sgb, srb=srb, q=q, k=k, v=v,
               mo=mo, lse=lse, m=m, a=a, x1=x1, h2=h2, u=u, act=act, f=f, mix_rows=mix_rows, prep_rows=prep_rows,
               prep_consts=prep_consts, prep_ws=prep_ws)
    return x2, h1n, res


def layer_backward(dx2, dh1n, res, tp, tp_next, lw, tabs, dims):
    nb, lc, lt = dims
    r = res
    t = dx2.shape[0]
    ncc, nch = lc // CHUNK, lt // CHUNK
    tm = _pick(t, (1024, 768, 512, 256))
    tkr = _pick(t, (2304, 1536, 1024, 768, 512))
    ret_tab = tabs[0]
    full = lambda a: (a, a.shape[1], 0)
    g = {}
    if tp_next is None:
        (dx1a, df), (dg2,), (g["ln2_g"], g["ln2_b"]) = stage_bwd(
            fn_post, [full(r["x1"]), full(r["f"])], [tp[5]], [], [lw["ln2_g"], lw["ln2_b"]], [dx2], [F32, BF], "post2_last_bwd")
        dnext = None
    else:
        (dx1a, df), (dg2, dsc1n, dsh1n), (g["ln2_g"], g["ln2_b"]) = stage_bwd(
            fn_post_mod, [full(r["x1"]), full(r["f"])], [tp[5], tp_next[0], tp_next[1]], [], [lw["ln2_g"], lw["ln2_b"]],
            [dx2, dh1n], [F32, BF], "post2_bwd")
        dnext = (dsc1n, dsh1n)
    dact = matmul(df, lw["wdown"], "nt", F32, tm, 1408, 1024, "ffn_down_dx")
    g["ffn_down"] = matmul(r["act"], df, "tn", F32, 1408, 1024, tkr, "ffn_down_dw")
    dc, dcw = gate_bwd(r["u"], lw["cw"], lw["cb"], dact, lt, lc, "ffn_gate_bwd")
    g["ffn_conv_w"], g["ffn_conv_b"] = dcw[0:3], dcw[3]
    du = conv_transpose(dc, lw["cw"], lt, lc, "ffn_conv_t")
    dh2 = matmul(du, lw["wup"], "nt", F32, tm, 1024, 1408, "ffn_up_dx")
    g["ffn_up"] = matmul(r["h2"], du, "tn", F32, 1024, 1408, tkr, "ffn_up_dw")
    (dxa, da), (dg1, dsc2, dsh2), (g["ln1_g"], g["ln1_b"]) = stage_bwd(
        fn_post_mod, [full(r["x"]), full(r["a"])], [tp[2], tp[4], tp[3]], [], [lw["ln1_g"], lw["ln1_b"]],
        [dx1a, dh2], [F32, BF], "post1_bwd")
    dm = matmul(da, lw["wout"], "nt", F32, tm, 1024, 1024, "proj_out_dx")
    g["w_out"] = matmul(r["m"], da, "tn", F32, 1024, 1024, tkr, "proj_out_dw")
    (dog, _, dor, _, dpg, dpr, dmo), _, (dgng,) = stage_bwd(
        fn_mix, r["mix_rows"], [], [], [lw["gng"]], [dm], [F32, None, F32, None, F32, F32, F32], "mix_bwd")
    g["gla_norm_g"] = jnp.sum(dgng.reshape(4, 64), axis=0)
    dq, dk, dv = mla_bwd(r["q"], r["k"], r["v"], r["mo"], r["lse"], dmo, nb, lc, lt, "mla_attn_bwd")
    (dpq, dpkv, dplk), _, (dgq, dgkv, dwuq, dwuk, dwuv) = stage_bwd(
        fn_mla_prep, r["prep_rows"], [], r["prep_consts"], r["prep_ws"], [dq, dk, dv], [F32, F32, F32], "mla_prep_bwd")
    g["mla_q_norm_g"], g["mla_kv_norm_g"] = dgq.reshape(256), dgkv.reshape(128)
    g["mla_w_uq"] = dwuq.reshape(256, MLA_HEADS, LANES)[:, :, :96].reshape(256, MLA_HEADS * 96)
    g["mla_w_uk"] = dwuk.reshape(128, MLA_HEADS, LANES)[:, :, :64].reshape(128, MLA_HEADS * 64)
    g["mla_w_uv"] = dwuv
    s7, dgw0, dgb0, drd0 = scan_bwd(r["p"], ret_tab, lw["gw"][0], lw["gb"][0], lw["rdec"][0], r["sgf"], r["srf"], dog, dor,
                                    None, nb, ncc, nch, False, "scan_bwd_f")
    s7, dgw1, dgb1, drd1 = scan_bwd(r["p"], ret_tab, lw["gw"][1], lw["gb"][1], lw["rdec"][1], r["sgb"], r["srb"], dog, dor,
                                    s7, nb, ncc, nch, True, "scan_bwd_b")
    g["gla_gate_w"] = jnp.stack([dgw0[0:16], dgw1[16:32]])
    g["gla_gate_b"] = jnp.stack([dgb0[0], dgb1[0]])
    g["ret_decay"] = jnp.stack([jnp.sum(drd0.reshape(4, 32), axis=1), jnp.sum(drd1.reshape(4, 32), axis=1)])
    gq_, gk_, gv_, glrk, rq_, rk_, rv_ = s7
    pieces = [gq_, gk_, gv_, dpg, rq_, rk_, rv_, dpr, dpq, dpkv, glrk, dplk]
    dp, = stage_fwd(fn_assemble, [full(a) for a in pieces], [], [], [], [(D_INP, BF)], "dproj_assemble")
    dh1 = matmul(dp, lw["win"], "nt", F32, tm, 1024, 1024, "proj_in_dx")
    g["w_in"] = unpermute_w_in(matmul(r["h1"], dp, "tn", F32, 1024, 1024, tkr, "proj_in_dw"))
    for n in ("ln1_g", "ln1_b", "ln2_g", "ln2_b"):
        g[n] = g[n].reshape(D_MODEL)
    dtp = [None, None, dg1, dsh2, dsc2, dg2]
    return dxa, dh1, dtp, dnext, g


def local_step(x, c, ctx, c_ctx, w, loss_target):
    nb, l, _ = x.shape
    lc = ctx.shape[1]
    lt = lc + l
    dims = (nb, lc, lt)
    nct, nlt = lc // ROW_TILE, l // ROW_TILE
    tabs = rope_tables(lc, l)
    x0 = jnp.concatenate([ctx, x], axis=1).reshape(nb * lt, D_MODEL)
    s8 = jnp.concatenate([silu(c), jnp.zeros((4 - nb, D_MODEL), F32), silu(c_ctx)[None], jnp.zeros((3, D_MODEL), F32)], axis=0)
    lws, tps = [], []
    for i in range(DEPTH):
        mod = matmul(s8, w["ada_w"][i].astype(BF), "nn", F32, 8, 1536, 1024, "ada_mod") + w["ada_b"][i].astype(F32)[None]
        tps.append(tile_params(mod, nb, nct, nlt))
        lws.append(layer_weights(w, i))
    h1, = stage_fwd(fn_modulate, [(x0, D_MODEL, 0)], [tps[0][1], tps[0][0]], [], [], [(D_MODEL, BF)], "mod_in")
    xs, ress = x0, []
    for i in range(DEPTH):
        tpn = None if i == DEPTH - 1 else (tps[i + 1][1], tps[i + 1][0])
        xs, h1, res = layer_forward(xs, h1, tps[i], tpn, lws[i], tabs, dims, i)
        ress.append(res)
    dx, lparts = loss_head(xs, loss_target.reshape(nb * l, D_MODEL), nct + nlt, nct, "loss_head")
    loss = jnp.sum(lparts[:, 0, 0])
    grads = [None] * DEPTH
    dtps = [None] * DEPTH
    dh1 = None
    for i in reversed(range(DEPTH)):
        tpn = None if i == DEPTH - 1 else (tps[i + 1][1], tps[i + 1][0])
        dx, dh1, dtp, dn, grads[i] = layer_backward(dx, dh1, ress[i], tps[i], tpn, lws[i], tabs, dims)
        if dn is not None:
            dtps[i + 1][1], dtps[i + 1][0] = dn
        dtps[i] = dtp
    (dx0b,), (dsc1, dsh1), _ = stage_bwd(fn_modulate, [(x0, D_MODEL, 0)], [tps[0][1], tps[0][0]], [], [], [dh1], [F32], "mod_in_bwd")
    dtps[0][1], dtps[0][0] = dsc1, dsh1
    grad_x = (dx + dx0b).reshape(nb, lt, D_MODEL)[:, lc:]
    dmod = jnp.stack([tile_param_grads(d, nb, nct, nlt) for d in dtps])
    gw = {n: jnp.stack([grads[i][n] for i in range(DEPTH)]) for n in grads[0]}
    return loss, grad_x, gw, dmod, s8


MESH_IDS = pl.DeviceIdType.MESH
ANY = pl.BlockSpec(memory_space=pl.ANY)


def _place():
    return lax.axis_index("x"), lax.axis_index("y"), lax.axis_index("c")


def _dma_sems(n, per):
    return pltpu.SemaphoreType.DMA((n, per))


def all_gather8(blocks, name):
    n = len(blocks)

    def body(*refs):
        x_refs, out_refs, (send_sems, recv_sems, local_sems) = refs[:n], refs[n:2 * n], refs[2 * n:]
        x, y, c = _place()
        me, sibling = (x, y, c), (x, y, 1 - c)
        chips = [(1 - x, y), (x, 1 - y), (1 - x, 1 - y)]

        def copy(a, k, blk, to, own=False):
            slot = out_refs[a].at[4 * blk[0] + 2 * blk[1] + blk[2]]
            return pltpu.make_async_remote_copy(
                src_ref=x_refs[a] if own else slot, dst_ref=slot,
                send_sem=send_sems.at[a, k], recv_sem=recv_sems.at[a, k], device_id=to, device_id_type=MESH_IDS)

        mine = [pltpu.make_async_copy(x_refs[a], out_refs[a].at[4 * x + 2 * y + c], local_sems.at[a, 0]) for a in range(n)]
        first = []
        for a in range(n):
            mine[a].start()
            first.append(copy(a, 0, me, sibling, own=True))
            first += [copy(a, 1 + j, me, (*chip, c), own=True) for j, chip in enumerate(chips)]
        for cp in first:
            cp.start()
        passed = []
        for j, chip in enumerate(chips):
            for a in range(n):
                copy(a, 1 + j, (*chip, c), me).wait_recv()
                passed.append(copy(a, 4 + j, (*chip, c), sibling))
                passed[-1].start()
        for a in range(n):
            copy(a, 0, sibling, me).wait_recv()
            for j, chip in enumerate(chips):
                copy(a, 4 + j, (*chip, 1 - c), me).wait_recv()
        for cp in first + passed:
            cp.wait_send()
        for cp in mine:
            cp.wait()

    return pl.pallas_call(
        body, name=name, out_shape=[jax.ShapeDtypeStruct((8,) + b.shape, b.dtype) for b in blocks],
        in_specs=[ANY] * n, out_specs=[ANY] * n,
        scratch_shapes=[_dma_sems(n, 7), _dma_sems(n, 7), _dma_sems(n, 1)],
    )(*blocks)


def swap_cores(blocks, name):
    n = len(blocks)

    def body(*refs):
        x_refs, out_refs, (send_sems, recv_sems) = refs[:n], refs[n:2 * n], refs[2 * n:]
        x, y, c = _place()
        cps = [pltpu.make_async_remote_copy(src_ref=x_refs[a], dst_ref=out_refs[a], send_sem=send_sems.at[a, 0],
                                            recv_sem=recv_sems.at[a, 0], device_id=(x, y, 1 - c), device_id_type=MESH_IDS)
               for a in range(n)]
        for cp in cps:
            cp.start()
        for cp in cps:
            cp.wait()

    return pl.pallas_call(
        body, name=name, out_shape=[jax.ShapeDtypeStruct(b.shape, b.dtype) for b in blocks],
        in_specs=[ANY] * n, out_specs=[ANY] * n, scratch_shapes=[_dma_sems(n, 1), _dma_sems(n, 1)],
    )(*blocks)


def exchange_chips(parts, name):
    n = len(parts)

    def body(*refs):
        p_refs, out_refs, (send_sems, recv_sems, local_sems) = refs[:n], refs[n:2 * n], refs[2 * n:]
        x, y, c = _place()
        jm = 2 * x + y
        chips = [(1 - x, y), (x, 1 - y), (1 - x, 1 - y)]
        mine = [pltpu.make_async_copy(p_refs[a].at[jm], out_refs[a].at[jm], local_sems.at[a, 0]) for a in range(n)]
        cps = []
        for k, (px, py) in enumerate(chips):
            for a in range(n):
                cps.append(pltpu.make_async_remote_copy(
                    src_ref=p_refs[a].at[2 * px + py], dst_ref=out_refs[a].at[jm], send_sem=send_sems.at[a, k],
                    recv_sem=recv_sems.at[a, k], device_id=(px, py, c), device_id_type=MESH_IDS))
        for cp in mine + cps:
            cp.start()
        for k, (px, py) in enumerate(chips):
            for a in range(n):
                pltpu.make_async_remote_copy(
                    src_ref=p_refs[a].at[jm], dst_ref=out_refs[a].at[2 * px + py], send_sem=send_sems.at[a, k],
                    recv_sem=recv_sems.at[a, k], device_id=(px, py, c), device_id_type=MESH_IDS).wait_recv()
        for cp in cps:
            cp.wait_send()
        for cp in mine:
            cp.wait()

    return pl.pallas_call(
        body, name=name, out_shape=[jax.ShapeDtypeStruct(p.shape, p.dtype) for p in parts],
        in_specs=[ANY] * n, out_specs=[ANY] * n, scratch_shapes=[_dma_sems(n, 3), _dma_sems(n, 3), _dma_sems(n, 1)],
    )(*parts)


def _row_tile(rows, cols, itemsize=4, limit=1 << 21):
    for cand in (2048, 1024, 512, 256, 128, 64, 32, 16):
        if rows % cand == 0 and cand * cols * itemsize <= limit:
            return cand
    return rows


def add_halves(a, b, kind, name):
    if kind == "col":
        _, k, n = a.shape
        n4 = n // 4
        tr = _row_tile(k, n4)
        in_spec = pl.BlockSpec((1, tr, n4), lambda j, h, i: (h, i, j))
        out_spec = pl.BlockSpec((1, 1, tr, n4), lambda j, h, i: (j, h, i, 0))
        grid, out_shape = (4, 2, k // tr), (4, 2, k, n4)
    elif kind == "row":
        _, k, n = a.shape
        k4 = k // 4
        tr = _row_tile(k4, n)
        nt = k4 // tr
        in_spec = pl.BlockSpec((1, tr, n), lambda j, h, i: (h, j * nt + i, 0))
        out_spec = pl.BlockSpec((1, 1, tr, n), lambda j, h, i: (j, h, i, 0))
        grid, out_shape = (4, 2, nt), (4, 2, k4, n)
    else:
        _, _, k, n = a.shape
        tr = _row_tile(k, n)
        in_spec = out_spec = pl.BlockSpec((1, 1, tr, n), lambda j, h, i: (j, h, i, 0))
        grid, out_shape = (4, 2, k // tr), a.shape

    def body(a_ref, b_ref, s_ref):
        s_ref[...] = (a_ref[...] + b_ref[...].astype(F32)).astype(BF).reshape(s_ref.shape)

    return pl.pallas_call(body, name=name, grid=grid, in_specs=[in_spec, in_spec], out_specs=out_spec,
                          out_shape=jax.ShapeDtypeStruct(out_shape, BF))(a, b)


def sum_slots(a, name):
    s, m, k, n = a.shape
    tr = _row_tile(k, n, limit=(1 << 22) // s)

    def body(a_ref, o_ref):
        acc = a_ref[0].astype(F32)
        for j in range(1, s):
            acc = acc + a_ref[j].astype(F32)
        o_ref[...] = acc

    return pl.pallas_call(body, name=name, grid=(m, k // tr), in_specs=[pl.BlockSpec((s, 1, tr, n), lambda h, i: (0, h, i, 0))],
                          out_specs=pl.BlockSpec((1, tr, n), lambda h, i: (h, i, 0)),
                          out_shape=jax.ShapeDtypeStruct((m, k, n), F32))(a)


def sum_small(arrays, name):
    n = len(arrays)

    def body(*refs):
        for a_ref, o_ref in zip(refs[:n], refs[n:]):
            acc = a_ref[0]
            for j in range(1, 8):
                acc = acc + a_ref[j]
            o_ref[...] = acc

    return pl.pallas_call(body, name=name, out_shape=[jax.ShapeDtypeStruct(a.shape[1:], F32) for a in arrays])(*arrays)


COL_SHARDED = ("ada_w", "w_in", "mla_w_uq", "mla_w_uk", "mla_w_uv", "ffn_up", "ffn_conv_w")
ROW_SHARDED = ("w_out", "ffn_down")
GATHERED = ("ada_w", "w_in", "mla_w_uq", "mla_w_uk", "mla_w_uv", "w_out", "ffn_up", "ffn_down", "ffn_conv_w")
REDUCED = ("w_in", "mla_w_uq", "mla_w_uk", "mla_w_uv", "w_out", "ffn_up", "ffn_down")
SMALL = ("ada_b", "gla_gate_w", "gla_gate_b", "gla_norm_g", "ret_decay", "mla_q_norm_g", "mla_kv_norm_g",
         "ln1_g", "ln1_b", "ffn_conv_b", "ln2_g", "ln2_b")
WEIGHTS = ("c_ctx", "ada_w", "ada_b", "w_in", "gla_gate_w", "gla_gate_b", "gla_norm_g", "ret_decay", "mla_q_norm_g",
           "mla_kv_norm_g", "mla_w_uq", "mla_w_uk", "mla_w_uv", "w_out", "ln1_g", "ln1_b", "ffn_up", "ffn_conv_w",
           "ffn_conv_b", "ffn_down", "ln2_g", "ln2_b")
PACK = 16 * LANES
HALF_LAYERS = DEPTH // 2


def _pad_flat(v, n):
    return jnp.concatenate([v, jnp.zeros((n - v.shape[0],), v.dtype)]) if n > v.shape[0] else v


def _my_layers(a, c):
    return lax.dynamic_slice_in_dim(a, HALF_LAYERS * c, HALF_LAYERS, axis=0)


def gather_weights(shards, c):
    blocks = [_my_layers(shards[n], c) if n == "ffn_conv_w" else _my_layers(shards[n], c).astype(BF) for n in GATHERED]
    got = all_gather8(blocks, "gather_weights")
    out = {}
    for n, g in zip(GATHERED, got):
        _, _, k, n4 = g.shape
        g = g.reshape(4, 2, HALF_LAYERS, k, n4)
        if n in ROW_SHARDED:
            out[n] = jnp.transpose(g, (1, 2, 0, 3, 4)).reshape(DEPTH, 4 * k, n4)
        else:
            out[n] = jnp.transpose(g, (1, 2, 3, 0, 4)).reshape(DEPTH, k, 4 * n4)
    return out


def reduce_gradients(gw, c):
    keep, give, kinds = [], [], []
    for n in REDUCED:
        g = gw[n]
        kind = "row" if n in ROW_SHARDED else ("col" if (g.shape[2] // 4) % LANES == 0 else "pre")
        if kind == "pre":
            g = jnp.transpose(g.reshape(DEPTH, g.shape[1], 4, g.shape[2] // 4), (2, 0, 1, 3))
            keep.append(lax.dynamic_slice_in_dim(g, HALF_LAYERS * c, HALF_LAYERS, axis=1))
            give.append(lax.dynamic_slice_in_dim(g, HALF_LAYERS * (1 - c), HALF_LAYERS, axis=1).astype(BF))
        else:
            keep.append(_my_layers(g, c))
            give.append(_my_layers(g, 1 - c).astype(BF))
        kinds.append(kind)
    got = swap_cores(give, "grad_swap_cores")
    parts = [add_halves(a, b, kind, "grad_add_cores_" + n) for n, a, b, kind in zip(REDUCED, keep, got, kinds)]
    landed = exchange_chips(parts, "grad_exchange_chips")
    mine = [sum_slots(a, "grad_sum_chips_" + n) for n, a in zip(REDUCED, landed)]
    theirs = swap_cores(mine, "grad_swap_back")
    out = {}
    for n, a, b in zip(REDUCED, mine, theirs):
        out[n] = jnp.concatenate([jnp.where(c == 0, a, b), jnp.where(c == 0, b, a)], axis=0)
    return out


def _pack_small(d, names):
    flat = jnp.concatenate([d[n].astype(F32).reshape(-1) for n in names])
    n = -(-flat.shape[0] // PACK) * PACK
    return _pad_flat(flat, n).reshape(n // LANES, LANES)


def _unpack_small(buf, like, names):
    flat, out, at = buf.reshape(-1), {}, 0
    for n in names:
        sz = like[n].size
        out[n] = flat[at:at + sz].reshape(like[n].shape)
        at += sz
    return out


def kernel(x, c, ctx, c_ctx, ada_w, ada_b, w_in, gla_gate_w, gla_gate_b, gla_norm_g, ret_decay, mla_q_norm_g, mla_kv_norm_g, mla_w_uq, mla_w_uk, mla_w_uv, w_out, ln1_g, ln1_b, ffn_up, ffn_conv_w, ffn_conv_b, ffn_down, ln2_g, ln2_b, loss_target, m_c_ctx, m_ada_w, m_ada_b, m_w_in, m_gla_gate_w, m_gla_gate_b, m_gla_norm_g, m_ret_decay, m_mla_q_norm_g, m_mla_kv_norm_g, m_mla_w_uq, m_mla_w_uk, m_mla_w_uv, m_w_out, m_ln1_g, m_ln1_b, m_ffn_up, m_ffn_conv_w, m_ffn_conv_b, m_ffn_down, m_ln2_g, m_ln2_b, v_c_ctx, v_ada_w, v_ada_b, v_w_in, v_gla_gate_w, v_gla_gate_b, v_gla_norm_g, v_ret_decay, v_mla_q_norm_g, v_mla_kv_norm_g, v_mla_w_uq, v_mla_w_uk, v_mla_w_uv, v_w_out, v_ln1_g, v_ln1_b, v_ffn_up, v_ffn_conv_w, v_ffn_conv_b, v_ffn_down, v_ln2_g, v_ln2_b):
    w = dict(c_ctx=c_ctx, ada_w=ada_w, ada_b=ada_b, w_in=w_in, gla_gate_w=gla_gate_w, gla_gate_b=gla_gate_b, gla_norm_g=gla_norm_g, ret_decay=ret_decay, mla_q_norm_g=mla_q_norm_g, mla_kv_norm_g=mla_kv_norm_g, mla_w_uq=mla_w_uq, mla_w_uk=mla_w_uk, mla_w_uv=mla_w_uv, w_out=w_out, ln1_g=ln1_g, ln1_b=ln1_b, ffn_up=ffn_up, ffn_conv_w=ffn_conv_w, ffn_conv_b=ffn_conv_b, ffn_down=ffn_down, ln2_g=ln2_g, ln2_b=ln2_b)
    m = dict(c_ctx=m_c_ctx, ada_w=m_ada_w, ada_b=m_ada_b, w_in=m_w_in, gla_gate_w=m_gla_gate_w, gla_gate_b=m_gla_gate_b, gla_norm_g=m_gla_norm_g, ret_decay=m_ret_decay, mla_q_norm_g=m_mla_q_norm_g, mla_kv_norm_g=m_mla_kv_norm_g, mla_w_uq=m_mla_w_uq, mla_w_uk=m_mla_w_uk, mla_w_uv=m_mla_w_uv, w_out=m_w_out, ln1_g=m_ln1_g, ln1_b=m_ln1_b, ffn_up=m_ffn_up, ffn_conv_w=m_ffn_conv_w, ffn_conv_b=m_ffn_conv_b, ffn_down=m_ffn_down, ln2_g=m_ln2_g, ln2_b=m_ln2_b)
    v = dict(c_ctx=v_c_ctx, ada_w=v_ada_w, ada_b=v_ada_b, w_in=v_w_in, gla_gate_w=v_gla_gate_w, gla_gate_b=v_gla_gate_b, gla_norm_g=v_gla_norm_g, ret_decay=v_ret_decay, mla_q_norm_g=v_mla_q_norm_g, mla_kv_norm_g=v_mla_kv_norm_g, mla_w_uq=v_mla_w_uq, mla_w_uk=v_mla_w_uk, mla_w_uv=v_mla_w_uv, w_out=v_w_out, ln1_g=v_ln1_g, ln1_b=v_ln1_b, ffn_up=v_ffn_up, ffn_conv_w=v_ffn_conv_w, ffn_conv_b=v_ffn_conv_b, ffn_down=v_ffn_down, ln2_g=v_ln2_g, ln2_b=v_ln2_b)
    xi, yi, ci = _place()
    chip = 2 * xi + yi

    full = dict(w)
    full.update(gather_weights({n: w[n] for n in GATHERED}, ci))
    loss, grad_x, gw, dmod, s8 = local_step(x, c, ctx, c_ctx, {n: full[n] for n in WEIGHTS if n != "c_ctx"}, loss_target)
    loss = lax.psum(loss, ("x", "y", "c"))

    dsil = jnp.zeros((8, D_MODEL), F32)
    for i in range(DEPTH):
        dsil = dsil + matmul(dmod[i], full["ada_w"][i], "nt", F32, 8, 1024, 1536, "ada_dsilu")

    grads = reduce_gradients(gw, ci)

    small = {n: gw[n] for n in SMALL if n != "ada_b"}
    small.update(dsil=dsil[4])
    names = tuple(small)
    conv_g = gw["ffn_conv_w"].reshape(DEPTH * 3, 2 * D_FF)
    ev_small, ev_dmod, ev_s8, ev_conv = all_gather8(
        [_pack_small(small, names), dmod.reshape(DEPTH * 8, 6 * D_MODEL), s8, conv_g], "gather_small")
    sm_small, sm_dmod, sm_conv = sum_small([ev_small, ev_dmod, ev_conv], "sum_small")
    summed = _unpack_small(sm_small, small, names)
    for n in SMALL:
        if n != "ada_b":
            grads[n] = summed[n]
    grads["ada_b"] = jnp.sum(sm_dmod.reshape(DEPTH, 8, 6 * D_MODEL)[:, :5], axis=1)
    sg = jax.nn.sigmoid(c_ctx)
    grads["c_ctx"] = summed["dsil"] * (sg * (1.0 + c_ctx * (1.0 - sg)))
    ccols = ffn_conv_w.shape[2]
    grads["ffn_conv_w"] = lax.dynamic_slice_in_dim(sm_conv.reshape(DEPTH, 3, 2 * D_FF), chip * ccols, ccols, axis=2)
    s_all = ev_s8.reshape(64, D_MODEL)
    d_all = jnp.transpose(ev_dmod.reshape(8, DEPTH, 8, 6 * D_MODEL), (1, 0, 2, 3)).reshape(DEPTH, 64, 6 * D_MODEL)
    cols = ada_w.shape[2]
    g_ada = []
    for i in range(DEPTH):
        d_mine = lax.dynamic_slice_in_dim(d_all[i], chip * cols, cols, axis=1)
        g_ada.append(matmul(s_all, d_mine, "tn", F32, 1024, cols, 64, "ada_dw"))
    grads["ada_w"] = jnp.stack(g_ada)

    delta, new_m, new_v = {}, {}, {}
    for n in GATHERED:
        shp = w[n].shape
        v2 = lambda a: a.reshape(-1, shp[-1])
        d_, m_, v_ = adamw(v2(w[n]), v2(grads[n]), v2(m[n]), v2(v[n]), "adamw_" + n)
        delta[n], new_m[n], new_v[n] = d_.reshape(shp), m_.reshape(shp), v_.reshape(shp)
    rep = tuple(n for n in WEIGHTS if n not in GATHERED)
    pk = lambda d: _pack_small({n: d[n] for n in rep}, rep)
    d_, m_, v_ = adamw(pk(w), pk(grads), pk(m), pk(v), "adamw_small")
    like = {n: w[n] for n in rep}
    delta.update(_unpack_small(d_, like, rep))
    new_m.update(_unpack_small(m_, like, rep))
    new_v.update(_unpack_small(v_, like, rep))
    grads = {n: grads[n].reshape(w[n].shape) for n in WEIGHTS}
    return (loss, grad_x, *[grads[n] for n in WEIGHTS], *[delta[n] for n in WEIGHTS], *[new_m[n] for n in WEIGHTS],
            *[new_v[n] for n in WEIGHTS])
```

```python
import functools

import jax
import jax.numpy as jnp
from jax import lax
from jax.experimental import pallas as pl
from jax.experimental.pallas import tpu as pltpu

F32 = jnp.float32
BF = jnp.bfloat16

D_MODEL = 1024
DEPTH = 4
GRID_W = 64
GLA_DK = 32
GLA_TAU = 16.0
RET_DK = 32
MLA_HEADS = 8
MLA_D_NOPE = 64
MLA_D_ROPE = 32
MLA_SCALE = (MLA_D_NOPE + MLA_D_ROPE) ** -0.5
D_FF = 2816
ROPE_BASE = 10000.0
EPS = 1e-6
ALPHA = (2 * DEPTH) ** 0.25
ADAM_LR, ADAM_B1, ADAM_B2, ADAM_EPS, ADAM_WD, ADAM_STEP = 0.001, 0.9, 0.999, 1e-08, 0.01, 10

ROW_TILE = 256
CHUNK = 64
GATE_ROWS = 128
LANES = 128

C_GQ, C_GK, C_GV, C_GG, C_RQ, C_RK, C_RV, C_RG, C_CQ, C_CKV, C_LK = 0, 128, 256, 512, 768, 896, 1024, 1280, 1536, 1792, 1920
D_INP = 2048
D_IN = 1984


def _dg(a, b, ca, cb):
    return lax.dot_general(a.astype(BF), b.astype(BF), (((ca,), (cb,)), ((), ())), preferred_element_type=F32)


@jax.custom_vjp
def mm_nn(a, b):
    return _dg(a, b, 1, 0)


@jax.custom_vjp
def mm_nt(a, b):
    return _dg(a, b, 1, 1)


@jax.custom_vjp
def mm_tn(a, b):
    return _dg(a, b, 0, 0)


mm_nn.defvjp(lambda a, b: (_dg(a, b, 1, 0), (a, b)),
             lambda r, g: (mm_nt(g, r[1]).astype(r[0].dtype), mm_tn(r[0], g).astype(r[1].dtype)))
mm_nt.defvjp(lambda a, b: (_dg(a, b, 1, 1), (a, b)),
             lambda r, g: (mm_nn(g, r[1]).astype(r[0].dtype), mm_tn(g, r[0]).astype(r[1].dtype)))
mm_tn.defvjp(lambda a, b: (_dg(a, b, 0, 0), (a, b)),
             lambda r, g: (mm_nt(r[1], g).astype(r[0].dtype), mm_nn(r[0], g).astype(r[1].dtype)))


def _split3(x):
    h = x.astype(BF)
    r = x - h.astype(F32)
    m = r.astype(BF)
    lo = (r - m.astype(F32)).astype(BF)
    return h, m, lo


def _exact(x, mat, left):
    h, m, lo = _split3(x)
    if left:
        d = lambda t: lax.dot_general(mat, t, (((1,), (0,)), ((), ())), preferred_element_type=F32)
    else:
        d = lambda t: lax.dot_general(t, mat, (((1,), (0,)), ((), ())), preferred_element_type=F32)
    return (d(lo) + d(m)) + d(h)


def _iota(shape, axis):
    return lax.broadcasted_iota(jnp.int32, shape, axis)


def _tri(n, upper):
    r, c = _iota((n, n), 0), _iota((n, n), 1)
    return jnp.where((c >= r) if upper else (r >= c), 1.0, 0.0).astype(BF)


@functools.partial(jax.custom_vjp, nondiff_argnums=(1,))
def cumsum_rows(x, upper):
    return _exact(x, _tri(x.shape[0], upper), True)


cumsum_rows.defvjp(lambda x, upper: (cumsum_rows(x, upper), None),
                   lambda upper, r, g: (cumsum_rows(g, not upper),))


def _seg(n, w):
    shift = w.bit_length() - 1
    r, c = _iota((n, n), 0), _iota((n, n), 1)
    return jnp.where(lax.shift_right_logical(r, shift) == lax.shift_right_logical(c, shift), 1.0, 0.0).astype(BF)


@functools.partial(jax.custom_vjp, nondiff_argnums=(1,))
def seg_sum(x, w):
    return _exact(x, _seg(x.shape[1], w), False)


seg_sum.defvjp(lambda x, w: (seg_sum(x, w), None), lambda w, r, g: (seg_sum(g, w),))


def _place_mat(transpose):
    shape = (8 * LANES, LANES) if transpose else (LANES, 8 * LANES)
    r, c = _iota(shape, 0), _iota(shape, 1)
    src, dst = (c, r) if transpose else (r, c)
    dl = jnp.bitwise_and(dst, LANES - 1)
    ok = (dl >= 64) & (dl < 96) & (src == dl - 32)
    return jnp.where(ok, 1.0, 0.0).astype(BF)


@jax.custom_vjp
def place_kr(x):
    return _exact(x, _place_mat(False), False)


place_kr.defvjp(lambda x: (place_kr(x), None), lambda r, g: (_exact(g, _place_mat(True), False),))


@functools.partial(jax.custom_vjp, nondiff_argnums=(1,))
def lane_roll(x, s):
    return pltpu.roll(x, s, 1)


lane_roll.defvjp(lambda x, s: (pltpu.roll(x, s, 1), None),
                 lambda s, r, g: (pltpu.roll(g, (g.shape[1] - s) % g.shape[1], 1),))


def rope(x, tab, d):
    cos, sa, sb = tab
    return x * cos + lane_roll(x, LANES - d) * sa + lane_roll(x, d) * sb


def silu(x):
    return x * jax.nn.sigmoid(x)


def log_sigmoid(z):
    return jnp.minimum(z, 0.0) - jnp.log(1.0 + jnp.exp(-jnp.abs(z)))


def layer_norm(x, g, b):
    mu = jnp.mean(x, axis=-1, keepdims=True)
    xc = x - mu
    var = jnp.mean(xc * xc, axis=-1, keepdims=True)
    return xc * lax.rsqrt(var + EPS) * g + b


def matmul(a, b, mode, out_dtype, tm, tn, tk, name, b_outer=False):
    ij = (lambda f: (lambda g0, g1, kk: f(g1, g0, kk))) if b_outer else (lambda f: f)
    if mode == "nn":
        (m, k), (k2, n) = a.shape, b.shape
        a_spec = pl.BlockSpec((tm, tk), ij(lambda i, j, kk: (i, kk)))
        b_spec = pl.BlockSpec((tk, tn), ij(lambda i, j, kk: (kk, j)))
        ca, cb = 1, 0
    elif mode == "nt":
        (m, k), (n, k2) = a.shape, b.shape
        a_spec = pl.BlockSpec((tm, tk), ij(lambda i, j, kk: (i, kk)))
        b_spec = pl.BlockSpec((tn, tk), ij(lambda i, j, kk: (j, kk)))
        ca, cb = 1, 1
    else:
        (k, m), (k2, n) = a.shape, b.shape
        a_spec = pl.BlockSpec((tk, tm), ij(lambda i, j, kk: (kk, i)))
        b_spec = pl.BlockSpec((tk, tn), ij(lambda i, j, kk: (kk, j)))
        ca, cb = 0, 0
    assert k == k2 and m % tm == 0 and n % tn == 0 and k % tk == 0, (name, a.shape, b.shape, tm, tn, tk)
    nk = k // tk
    grid = (n // tn, m // tm, nk) if b_outer else (m // tm, n // tn, nk)

    def body(a_ref, b_ref, o_ref, *acc):
        part = _dg(a_ref[...], b_ref[...], ca, cb)
        if nk == 1:
            o_ref[...] = part.astype(o_ref.dtype)
            return
        acc_ref, = acc
        kk = pl.program_id(2)

        @pl.when(kk == 0)
        def _():
            acc_ref[...] = part

        @pl.when(kk > 0)
        def _():
            acc_ref[...] += part

        @pl.when(kk == nk - 1)
        def _():
            o_ref[...] = acc_ref[...].astype(o_ref.dtype)

    return pl.pallas_call(
        body, name=name, grid=grid,
        in_specs=[a_spec, b_spec], out_specs=pl.BlockSpec((tm, tn), ij(lambda i, j, kk: (i, j))),
        out_shape=jax.ShapeDtypeStruct((m, n), out_dtype),
        scratch_shapes=[] if nk == 1 else [pltpu.VMEM((tm, tn), F32)],
    )(a, b)


def _stage_specs(rows, tps, consts, ws):
    specs, args = [], []
    for arr, width, cb in rows:
        specs.append(pl.BlockSpec((ROW_TILE, width), functools.partial(lambda i, cb: (i, cb), cb=cb)))
        args.append(arr)
    for arr in tps:
        specs.append(pl.BlockSpec((1, 1, arr.shape[2]), lambda i: (i, 0, 0)))
        args.append(arr)
    for arr, period in consts:
        specs.append(pl.BlockSpec((ROW_TILE, arr.shape[1]), functools.partial(lambda i, p: (i % p, 0), p=period)))
        args.append(arr)
    for arr in ws:
        specs.append(pl.BlockSpec(arr.shape, functools.partial(lambda i, nd: (0,) * nd, nd=arr.ndim)))
        args.append(arr)
    return specs, args


def _stage_load(refs, n_rows, n_tps, n_consts, n_ws):
    it = iter(refs)
    rows = [next(it)[...].astype(F32) for _ in range(n_rows)]
    tps = [next(it)[0].astype(F32) for _ in range(n_tps)]
    consts = [next(it)[...].astype(F32) for _ in range(n_consts)]
    ws = [next(it)[...].astype(F32) for _ in range(n_ws)]
    return rows, tps, consts, ws


def stage_fwd(fn, rows, tps, consts, ws, outs, name):
    n_tiles = rows[0][0].shape[0] // ROW_TILE
    specs, args = _stage_specs(rows, tps, consts, ws)
    counts = (len(rows), len(tps), len(consts), len(ws))

    def body(*refs):
        r, t, c, w = _stage_load(refs[:sum(counts)], *counts)
        res = fn(r, t, c, w)
        for o_ref, o in zip(refs[sum(counts):], res):
            o_ref[...] = o.astype(o_ref.dtype)

    res = pl.pallas_call(
        body, name=name, grid=(n_tiles,), in_specs=specs,
        out_specs=[pl.BlockSpec((ROW_TILE, wd), lambda i: (i, 0)) for wd, _ in outs],
        out_shape=[jax.ShapeDtypeStruct((n_tiles * ROW_TILE, wd), dt) for wd, dt in outs],
    )(*args)
    return list(res)


def stage_bwd(fn, rows, tps, consts, ws, cts, row_grads, name):
    n_tiles = rows[0][0].shape[0] // ROW_TILE
    specs, args = _stage_specs(rows, tps, consts, ws)
    counts = (len(rows), len(tps), len(consts), len(ws))
    n_in = sum(counts)
    for ct in cts:
        specs.append(pl.BlockSpec((ROW_TILE, ct.shape[1]), lambda i: (i, 0)))
        args.append(ct)
    want = [k for k, dt in enumerate(row_grads) if dt is not None]
    out_specs = [pl.BlockSpec((ROW_TILE, rows[k][1]), lambda i: (i, 0)) for k in want]
    out_shape = [jax.ShapeDtypeStruct((n_tiles * ROW_TILE, rows[k][1]), row_grads[k]) for k in want]
    out_specs += [pl.BlockSpec((1, 1, a.shape[2]), lambda i: (i, 0, 0)) for a in tps]
    out_shape += [jax.ShapeDtypeStruct((n_tiles, 1, a.shape[2]), F32) for a in tps]
    out_specs += [pl.BlockSpec(a.shape, functools.partial(lambda i, nd: (0,) * nd, nd=a.ndim)) for a in ws]
    out_shape += [jax.ShapeDtypeStruct(a.shape, F32) for a in ws]

    def body(*refs):
        r, t, c, w = _stage_load(refs[:n_in], *counts)
        g = [ref[...].astype(F32) for ref in refs[n_in:n_in + len(cts)]]
        _, vjp = jax.vjp(lambda r_, t_, w_: fn(r_, t_, c, w_), r, t, w)
        dr, dt, dw = vjp(g)
        o = iter(refs[n_in + len(cts):])
        for k in want:
            ref = next(o)
            ref[...] = dr[k].astype(ref.dtype)
        for v in dt:
            next(o)[0] = v
        first = pl.program_id(0) == 0
        for v in dw:
            ref = next(o)

            @pl.when(first)
            def _():
                ref[...] = v

            @pl.when(jnp.logical_not(first))
            def _():
                ref[...] += v

    res = pl.pallas_call(body, name=name, grid=(n_tiles,), in_specs=specs, out_specs=out_specs, out_shape=out_shape)(*args)
    res = list(res)
    drows = [None] * len(rows)
    for k in want:
        drows[k] = res.pop(0)
    dtps = [res.pop(0) for _ in tps]
    dws = [res.pop(0) for _ in ws]
    return drows, dtps, dws


def fn_modulate(rows, tps, consts, ws):
    (x,), (sc, sh) = rows, tps
    return [x * (1.0 + sc) + sh]


def fn_post(rows, tps, consts, ws):
    (x, a), (g,), (lng, lnb) = rows, tps, ws
    return [layer_norm(ALPHA * x + g * a, lng, lnb)]


def fn_mix(rows, tps, consts, ws):
    ogf, ogb, orf, orb, pg, pr, mo = rows
    gng, = ws
    og = ogf + ogb
    out_g = og * lax.rsqrt(seg_sum(og * og, 64) * (1.0 / 64) + EPS) * gng * silu(pg)
    o = orf + orb
    oc = o - seg_sum(o, 64) * (1.0 / 64)
    out_r = oc * lax.rsqrt(seg_sum(oc * oc, 64) * (1.0 / 64) + EPS) * silu(pr)
    return [jnp.concatenate([out_g, out_r, mo], axis=-1)]


def fn_mla_prep(rows, tps, consts, ws):
    pq, pkv, plk = rows
    gq, gkv, wuq, wuk, wuv = ws
    qtab, ktab = consts[0:3], consts[3:6]
    cq = pq * lax.rsqrt(jnp.mean(pq * pq, axis=-1, keepdims=True) + EPS) * gq
    qp = mm_nn(cq, wuq)
    q = jnp.concatenate([rope(qp[:, h * LANES:(h + 1) * LANES], qtab, 8) for h in range(MLA_HEADS)], axis=-1)
    ckv = pkv * lax.rsqrt(jnp.mean(pkv * pkv, axis=-1, keepdims=True) + EPS) * gkv
    k = mm_nn(ckv, wuk) + place_kr(rope(plk, ktab, 8))
    v = mm_nn(ckv, wuv)
    return [q, k, v]


def fn_assemble(rows, tps, consts, ws):
    gq, gk, gv, gg, rq, rk, rv, rg, cq, ckv, lk1, lk2 = rows
    return [jnp.concatenate([gq, gk, gv, gg, rq, rk, rv, rg, cq, ckv, lk1 + lk2], axis=-1)]


def _head_masks():
    hm = (lax.shift_right_logical(_iota((4, 1, LANES), 2), 5) == _iota((4, 1, LANES), 0)).astype(F32)
    vm = (lax.shift_right_logical(_iota((4, 1, 256), 2), 6) == _iota((4, 1, 256), 0)).astype(F32)
    bd = (lax.shift_right_logical(_iota((256, LANES), 0), 6) == lax.shift_right_logical(_iota((256, LANES), 1), 5)).astype(F32)
    return hm, vm, bd


def chunk_step(s, q, k, v, la, upper):
    hm, vm, bd = _head_masks()
    t, u = _iota((4 * CHUNK, CHUNK), 0), _iota((4 * CHUNK, CHUNK), 1)
    t = jnp.bitwise_and(t, CHUNK - 1)
    causal = (u >= t) if upper else (t >= u)
    b = cumsum_rows(la, upper)
    bend = jnp.sum(la, axis=0, keepdims=True)
    half = 0.5 * bend
    qd = q * jnp.exp(b - half)
    kd = k * jnp.exp(half - b)
    qe = (qd[None] * hm).reshape(4 * CHUNK, LANES)
    att = jnp.where(causal, mm_nt(qe, kd), 0.0)
    o_intra = (mm_nn(att, v).reshape(4, CHUNK, 256) * vm).sum(0)
    o = mm_nt(q * jnp.exp(b), s) + o_intra
    s_new = (s * jnp.exp(bend) + mm_tn(v, k * jnp.exp(bend - b))) * bd
    return o, s_new


def scan_step(sg, sr, q, k, v, lrk, rq, rk, rv, gw, gb, rdec, tab, upper):
    la_g = log_sigmoid(mm_nn(lrk, gw) + gb) * (1.0 / GLA_TAU)
    og, sg2 = chunk_step(sg, q * GLA_DK ** -0.5, k, v, la_g, upper)
    la_r = jnp.broadcast_to(log_sigmoid(rdec), (CHUNK, LANES))
    orr, sr2 = chunk_step(sr, rope(rq, tab, 16), rope(rk * RET_DK ** -0.5, tab, 16), rv, la_r, upper)
    return og, orr, sg2, sr2


def _chunk_of(n, ncc, nch, reverse):
    if not reverse:
        return n
    return jnp.where(n < ncc, ncc - 1 - n, nch - 1 + ncc - n)


def _scan_in_specs(p3, tabs, gw, gb, rdec, cidx):
    nb = p3.shape[0]

    def blk(width, cb):
        return pl.BlockSpec((nb, CHUNK, width), lambda m: (0, cidx(m), cb))

    specs = [blk(128, C_GQ // 128), blk(128, C_GK // 128), blk(256, C_GV // 256), blk(128, C_LK // 128),
             blk(128, C_RQ // 128), blk(128, C_RK // 128), blk(256, C_RV // 256)]
    args = [p3] * 7
    for t in tabs:
        specs.append(pl.BlockSpec((CHUNK, LANES), lambda m: (cidx(m), 0)))
        args.append(t)
    for w in (gw, gb, rdec):
        specs.append(pl.BlockSpec(w.shape, lambda m: (0, 0)))
        args.append(w)
    return specs, args


def scan_fwd(p, tabs, gw, gb, rdec, nb, ncc, nch, reverse, name):
    cidx = lambda n: _chunk_of(n, ncc, nch, reverse)
    t = p.shape[0]
    specs, args = _scan_in_specs(p.reshape(nb, t // nb, p.shape[1]), tabs, gw, gb, rdec, cidx)

    def body(q, k, v, lrk, rq, rk, rv, tc, ta, tb, gw_r, gb_r, rd_r, og_r, or_r, sgo_r, sro_r, sg, sr):
        @pl.when(pl.program_id(0) == 0)
        def _():
            sg[...] = jnp.zeros_like(sg)
            sr[...] = jnp.zeros_like(sr)

        sgo_r[0] = sg[...]
        sro_r[0] = sr[...]
        ld = lambda r: r[...].astype(F32)
        tab, gw_, gb_, rd_ = (ld(tc), ld(ta), ld(tb)), ld(gw_r), ld(gb_r), ld(rd_r)
        for b in range(nb):
            lb = lambda r: r[b].astype(F32)
            og, orr, sg2, sr2 = scan_step(sg[b], sr[b], lb(q), lb(k), lb(v), lb(lrk), lb(rq), lb(rk), lb(rv),
                                          gw_, gb_, rd_, tab, reverse)
            og_r[b] = og
            or_r[b] = orr
            sg[b] = sg2
            sr[b] = sr2

    row_out = pl.BlockSpec((nb, CHUNK, 256), lambda n: (0, cidx(n), 0))
    st_out = pl.BlockSpec((1, nb, 256, LANES), lambda n: (n, 0, 0, 0))
    og, orr, sgs, srs = pl.pallas_call(
        body, name=name, grid=(nch,), in_specs=specs, out_specs=[row_out, row_out, st_out, st_out],
        out_shape=[jax.ShapeDtypeStruct((nb, t // nb, 256), F32)] * 2 + [jax.ShapeDtypeStruct((nch, nb, 256, LANES), F32)] * 2,
        scratch_shapes=[pltpu.VMEM((nb, 256, LANES), F32)] * 2,
    )(*args)
    return og.reshape(t, 256), orr.reshape(t, 256), sgs, srs


def scan_bwd(p, tabs, gw, gb, rdec, sg_in, sr_in, dog, dor, prev, nb, ncc, nch, reverse, name):
    step = lambda m: nch - 1 - m
    cidx = lambda m: _chunk_of(step(m), ncc, nch, reverse)
    t = p.shape[0]
    lt = t // nb
    specs, args = _scan_in_specs(p.reshape(nb, lt, p.shape[1]), tabs, gw, gb, rdec, cidx)
    st_spec = pl.BlockSpec((1, nb, 256, LANES), lambda m: (step(m), 0, 0, 0))
    specs += [st_spec, st_spec]
    args += [sg_in, sr_in]
    row = lambda width: pl.BlockSpec((nb, CHUNK, width), lambda m: (0, cidx(m), 0))
    specs += [row(256), row(256)]
    args += [dog.reshape(nb, lt, 256), dor.reshape(nb, lt, 256)]
    widths = (128, 128, 256, 128, 128, 128, 256)
    if prev is not None:
        specs += [row(wd) for wd in widths]
        args += [a.reshape(nb, lt, a.shape[1]) for a in prev]
    n_prev = 0 if prev is None else 7

    def body(*refs):
        (q, k, v, lrk, rq, rk, rv, tc, ta, tb, gw_r, gb_r, rd_r, sgi, sri, dog_r, dor_r), rest = refs[:17], refs[17:]
        prev_r, rest = rest[:n_prev], rest[n_prev:]
        outs, (dgw_r, dgb_r, drd_r, dsg, dsr) = rest[:7], rest[7:]
        first = pl.program_id(0) == 0

        @pl.when(first)
        def _():
            dsg[...] = jnp.zeros_like(dsg)
            dsr[...] = jnp.zeros_like(dsr)

        ld = lambda r: r[...].astype(F32)
        tab, gw_, gb_, rd_ = (ld(tc), ld(ta), ld(tb)), ld(gw_r), ld(gb_r), ld(rd_r)
        wsum = None
        for b in range(nb):
            lb = lambda r: r[b].astype(F32)
            prim = (sgi[0, b], sri[0, b], lb(q), lb(k), lb(v), lb(lrk), lb(rq), lb(rk), lb(rv), gw_, gb_, rd_)
            _, vjp = jax.vjp(lambda *a: scan_step(*a, tab, reverse), *prim)
            g = vjp((lb(dog_r), lb(dor_r), dsg[b], dsr[b]))
            dsg[b] = g[0]
            dsr[b] = g[1]
            for j in range(7):
                val = g[2 + j]
                if n_prev:
                    val = val + prev_r[j][b]
                outs[j][b] = val
            wsum = g[9:12] if wsum is None else tuple(a + c for a, c in zip(wsum, g[9:12]))
        for ref, val in zip((dgw_r, dgb_r, drd_r), wsum):
            @pl.when(first)
            def _():
                ref[...] = val

            @pl.when(jnp.logical_not(first))
            def _():
                ref[...] += val

    wspec = lambda w: pl.BlockSpec(w.shape, lambda m: (0, 0))
    res = pl.pallas_call(
        body, name=name, grid=(nch,), in_specs=specs,
        out_specs=[row(wd) for wd in widths] + [wspec(gw), wspec(gb), wspec(rdec)],
        out_shape=[jax.ShapeDtypeStruct((nb, lt, wd), F32) for wd in widths]
        + [jax.ShapeDtypeStruct(w.shape, F32) for w in (gw, gb, rdec)],
        scratch_shapes=[pltpu.VMEM((nb, 256, LANES), F32)] * 2,
    )(*args)
    return tuple(a.reshape(t, a.shape[2]) for a in res[:7]), res[7], res[8], res[9]


def _attn_tiles(lc, lt):
    nct = lc // ROW_TILE
    return nct, (lt - lc) // ROW_TILE


def _attn_loop(tile, lc, lt):
    nct, nlt = _attn_tiles(lc, lt)
    for i in range(nct):
        tile(i * ROW_TILE, lc)

    def lat(i, carry):
        tile(pl.multiple_of(lc + i * ROW_TILE, ROW_TILE), lt)
        return carry

    lax.fori_loop(0, nlt, lat, 0)


def mla_fwd(q, k, v, nb, lc, lt, name):
    def body(q_ref, k_ref, v_ref, o_ref, lse_ref):
        def tile(r0, nk):
            rows = pl.ds(r0, ROW_TILE)
            lane = _iota((ROW_TILE, LANES), 1)
            outs, lse = [], jnp.zeros((ROW_TILE, LANES), F32)
            for j in range(2):
                s = _dg(q_ref[rows, j * LANES:(j + 1) * LANES], k_ref[0:nk, j * LANES:(j + 1) * LANES], 1, 1) * MLA_SCALE
                m = jnp.max(s, axis=-1, keepdims=True)
                p = jnp.exp(s - m)
                l = jnp.sum(p, axis=-1, keepdims=True)
                outs.append(_dg(p * (1.0 / l), v_ref[0:nk, j * 64:(j + 1) * 64], 1, 0))
                lse = jnp.where(lane == j, m + jnp.log(l), lse)
            o_ref[rows, :] = jnp.concatenate(outs, axis=-1)
            lse_ref[rows, :] = lse

        _attn_loop(tile, lc, lt)

    pair = lambda width: pl.BlockSpec((lt, width), lambda b, h: (b, h))
    return pl.pallas_call(
        body, name=name, grid=(nb, MLA_HEADS // 2), in_specs=[pair(2 * LANES), pair(2 * LANES), pair(LANES)],
        out_specs=[pair(LANES), pair(LANES)],
        out_shape=[jax.ShapeDtypeStruct((nb * lt, MLA_HEADS * 64), F32), jax.ShapeDtypeStruct((nb * lt, MLA_HEADS // 2 * LANES), F32)],
    )(q, k, v)


def mla_bwd(q, k, v, o, lse, do, nb, lc, lt, name):
    def body(q_ref, k_ref, v_ref, o_ref, lse_ref, do_ref, dq_ref, dk_ref, dv_ref, dka, dva):
        dka[...] = jnp.zeros_like(dka)
        dva[...] = jnp.zeros_like(dva)

        def tile(r0, nk):
            rows = pl.ds(r0, ROW_TILE)
            dqs = []
            for j in range(2):
                qj, kj = q_ref[rows, j * LANES:(j + 1) * LANES], k_ref[0:nk, j * LANES:(j + 1) * LANES]
                vj, doj = v_ref[0:nk, j * 64:(j + 1) * 64], do_ref[rows, j * 64:(j + 1) * 64]
                p = jnp.exp(_dg(qj, kj, 1, 1) * MLA_SCALE - lse_ref[rows, j:j + 1])
                dsum = jnp.sum(doj * o_ref[rows, j * 64:(j + 1) * 64], axis=-1, keepdims=True)
                ds = p * (_dg(doj, vj, 1, 1) - dsum) * MLA_SCALE
                dqs.append(_dg(ds, kj, 1, 0))
                dka[j, 0:nk, :] += _dg(ds, qj, 0, 0)
                dva[j, 0:nk, :] += _dg(p, doj, 0, 0)
            dq_ref[rows, :] = jnp.concatenate(dqs, axis=-1)

        _attn_loop(tile, lc, lt)
        dk_ref[...] = jnp.concatenate([dka[0], dka[1]], axis=-1)
        dv_ref[...] = jnp.concatenate([dva[0], dva[1]], axis=-1)

    t = nb * lt
    pair = lambda width: pl.BlockSpec((lt, width), lambda b, h: (b, h))
    return pl.pallas_call(
        body, name=name, grid=(nb, MLA_HEADS // 2),
        in_specs=[pair(2 * LANES), pair(2 * LANES), pair(LANES), pair(LANES), pair(LANES), pair(LANES)],
        out_specs=[pair(2 * LANES), pair(2 * LANES), pair(LANES)],
        out_shape=[jax.ShapeDtypeStruct((t, MLA_HEADS * LANES), F32), jax.ShapeDtypeStruct((t, MLA_HEADS * LANES), F32),
                   jax.ShapeDtypeStruct((t, MLA_HEADS * 64), F32)],
        scratch_shapes=[pltpu.VMEM((2, lt, LANES), F32), pltpu.VMEM((2, lt, 64), F32)],
    )(q, k, v, o, lse, do)


HALO = 16


def _gate_specs(u, per_batch, lc):
    gh = GATE_ROWS // HALO
    nh = u.shape[0] // HALO
    width = u.shape[1]
    main = pl.BlockSpec((GATE_ROWS, width), lambda i: (i, 0))
    prev = pl.BlockSpec((HALO, width), lambda i: (jnp.maximum(i * gh - 1, 0), 0))
    nxt = pl.BlockSpec((HALO, width), lambda i: (jnp.minimum((i + 1) * gh, nh - 1), 0))
    return main, prev, nxt


def _seg_edges(per_batch, lc):
    j = pl.program_id(0) % (per_batch // GATE_ROWS)
    first = (j == 0) | (j == lc // GATE_ROWS)
    last = (j == lc // GATE_ROWS - 1) | (j == per_batch // GATE_ROWS - 1)
    return first, last


def _shifted(x, prev_ref, next_ref, first, last):
    rows = _iota(x.shape, 0)
    before = jnp.where(first, 0.0, prev_ref[HALO - 1:HALO, :].astype(F32))
    after = jnp.where(last, 0.0, next_ref[0:1, :].astype(F32))
    xm = jnp.where(rows == 0, before, pltpu.roll(x, 1, 0))
    xp = jnp.where(rows == x.shape[0] - 1, after, pltpu.roll(x, x.shape[0] - 1, 0))
    return xm, xp


def gate_fwd(u, cw, cb, per_batch, lc, name):
    main, prev, nxt = _gate_specs(u, per_batch, lc)
    f = u.shape[1] // 2

    def body(u_ref, p_ref, n_ref, w_ref, b_ref, act_ref):
        first, last = _seg_edges(per_batch, lc)
        x = u_ref[...].astype(F32)
        xm, xp = _shifted(x, p_ref, n_ref, first, last)
        c = w_ref[0:1, :] * xm + w_ref[1:2, :] * x + w_ref[2:3, :] * xp + b_ref[...]
        act_ref[...] = (silu(c[:, :f]) * c[:, f:]).astype(act_ref.dtype)

    return pl.pallas_call(
        body, name=name, grid=(u.shape[0] // GATE_ROWS,),
        in_specs=[main, prev, nxt, pl.BlockSpec(cw.shape, lambda i: (0, 0)), pl.BlockSpec(cb.shape, lambda i: (0, 0))],
        out_specs=pl.BlockSpec((GATE_ROWS, f), lambda i: (i, 0)),
        out_shape=jax.ShapeDtypeStruct((u.shape[0], f), BF),
    )(u, u, u, cw, cb)


def gate_bwd(u, cw, cb, dact, per_batch, lc, name):
    main, prev, nxt = _gate_specs(u, per_batch, lc)
    f = u.shape[1] // 2

    def body(u_ref, p_ref, n_ref, w_ref, b_ref, da_ref, dc_ref, dw_ref):
        first, last = _seg_edges(per_batch, lc)
        x = u_ref[...].astype(F32)
        xm, xp = _shifted(x, p_ref, n_ref, first, last)
        c = w_ref[0:1, :] * xm + w_ref[1:2, :] * x + w_ref[2:3, :] * xp + b_ref[...]
        a, g = c[:, :f], c[:, f:]
        sg = jax.nn.sigmoid(a)
        da = da_ref[...]
        dc = jnp.concatenate([da * g * (sg * (1.0 + a * (1.0 - sg))), da * (a * sg)], axis=-1)
        dc_ref[...] = dc.astype(dc_ref.dtype)
        part = jnp.concatenate([jnp.sum(xm * dc, axis=0, keepdims=True), jnp.sum(x * dc, axis=0, keepdims=True),
                                jnp.sum(xp * dc, axis=0, keepdims=True), jnp.sum(dc, axis=0, keepdims=True),
                                jnp.zeros((4, 2 * f), F32)], axis=0)

        @pl.when(pl.program_id(0) == 0)
        def _():
            dw_ref[...] = part

        @pl.when(pl.program_id(0) > 0)
        def _():
            dw_ref[...] += part

    return pl.pallas_call(
        body, name=name, grid=(u.shape[0] // GATE_ROWS,),
        in_specs=[main, prev, nxt, pl.BlockSpec(cw.shape, lambda i: (0, 0)), pl.BlockSpec(cb.shape, lambda i: (0, 0)),
                  pl.BlockSpec((GATE_ROWS, f), lambda i: (i, 0))],
        out_specs=[main, pl.BlockSpec((8, 2 * f), lambda i: (0, 0))],
        out_shape=[jax.ShapeDtypeStruct(u.shape, BF), jax.ShapeDtypeStruct((8, 2 * f), F32)],
    )(u, u, u, cw, cb, dact)


def conv_transpose(dc, cw, per_batch, lc, name):
    main, prev, nxt = _gate_specs(dc, per_batch, lc)

    def body(d_ref, p_ref, n_ref, w_ref, du_ref):
        first, last = _seg_edges(per_batch, lc)
        x = d_ref[...].astype(F32)
        xm, xp = _shifted(x, p_ref, n_ref, first, last)
        du_ref[...] = (w_ref[0:1, :] * xp + w_ref[1:2, :] * x + w_ref[2:3, :] * xm).astype(du_ref.dtype)

    return pl.pallas_call(
        body, name=name, grid=(dc.shape[0] // GATE_ROWS,),
        in_specs=[main, prev, nxt, pl.BlockSpec(cw.shape, lambda i: (0, 0))],
        out_specs=main, out_shape=jax.ShapeDtypeStruct(dc.shape, BF),
    )(dc, dc, dc, cw)


def loss_head(x, target, tiles_per_batch, ctx_tiles, name):
    n_tiles = x.shape[0] // ROW_TILE
    lat_tiles = tiles_per_batch - ctx_tiles

    def tgt_idx(i):
        j = i % tiles_per_batch
        return jnp.where(j < ctx_tiles, 0, (i // tiles_per_batch) * lat_tiles + j - ctx_tiles), 0

    def body(x_ref, t_ref, dx_ref, l_ref):
        lat = (pl.program_id(0) % tiles_per_batch >= ctx_tiles).astype(F32)
        err = (x_ref[...] - t_ref[...]) * lat
        dx_ref[...] = err * (1.0 / D_MODEL)
        l_ref[...] = jnp.full(l_ref.shape, 0.5 / D_MODEL * jnp.sum(err * err), F32)

    return pl.pallas_call(
        body, name=name, grid=(n_tiles,),
        in_specs=[pl.BlockSpec((ROW_TILE, D_MODEL), lambda i: (i, 0)), pl.BlockSpec((ROW_TILE, D_MODEL), tgt_idx)],
        out_specs=[pl.BlockSpec((ROW_TILE, D_MODEL), lambda i: (i, 0)), pl.BlockSpec((1, 8, LANES), lambda i: (i, 0, 0))],
        out_shape=[jax.ShapeDtypeStruct(x.shape, F32), jax.ShapeDtypeStruct((n_tiles, 8, LANES), F32)],
    )(x, target)


def adamw(w, g, m, v, name):
    rows, cols = w.shape
    tr = rows
    for cand in (512, 256, 128, 64, 32, 16, 8):
        if rows % cand == 0 and cand * cols * 4 <= (1 << 20):
            tr = cand
            break

    def body(w_ref, g_ref, m_ref, v_ref, d_ref, mo_ref, vo_ref):
        gg = g_ref[...]
        m2 = ADAM_B1 * m_ref[...] + (1.0 - ADAM_B1) * gg
        v2 = ADAM_B2 * v_ref[...] + (1.0 - ADAM_B2) * (gg * gg)
        m_hat = m2 / (1.0 - ADAM_B1 ** ADAM_STEP)
        v_hat = v2 / (1.0 - ADAM_B2 ** ADAM_STEP)
        d_ref[...] = -ADAM_LR * (m_hat / (jnp.sqrt(v_hat) + ADAM_EPS) + ADAM_WD * w_ref[...])
        mo_ref[...] = m2
        vo_ref[...] = v2

    spec = pl.BlockSpec((tr, cols), lambda i: (i, 0))
    return pl.pallas_call(body, name=name, grid=(rows // tr,), in_specs=[spec] * 4, out_specs=[spec] * 3,
                          out_shape=[jax.ShapeDtypeStruct(w.shape, F32)] * 3)(w, g, m, v)


def fn_post_mod(rows, tps, consts, ws):
    (x, a), (g, sc, sh), (lng, lnb) = rows, tps, ws
    y = layer_norm(ALPHA * x + g * a, lng, lnb)
    return [y, y * (1.0 + sc) + sh]


def _pick(n, cands):
    for c in cands:
        if n % c == 0:
            return c
    return n


def rope_tables(lc, l):
    pos = jnp.arange(l, dtype=F32)
    ret_inv = 1.0 / (ROPE_BASE ** jnp.linspace(0.0, 1.0, RET_DK // 2, dtype=F32))
    ang = pos[:, None] * ret_inv
    rc, rs = jnp.cos(ang), jnp.sin(ang)
    n_ax = MLA_D_ROPE // 4
    ax_inv = ROPE_BASE ** (-jnp.arange(n_ax, dtype=F32) / n_ax)
    rows_n = l // GRID_W
    rows = jnp.repeat(jnp.arange(rows_n, dtype=F32), GRID_W)
    cols = jnp.tile(jnp.arange(GRID_W, dtype=F32), rows_n)
    ra, ca = rows[:, None] * ax_inv, cols[:, None] * ax_inv
    rwc, rws, clc, cls = jnp.cos(ra), jnp.sin(ra), jnp.cos(ca), jnp.sin(ca)
    one = lambda n: jnp.ones((l, n), F32)
    zero = lambda n: jnp.zeros((l, n), F32)
    cat = lambda parts: jnp.concatenate(parts, axis=1)

    def with_ctx(tab, is_cos):
        head = jnp.ones((lc, LANES), F32) if is_cos else jnp.zeros((lc, LANES), F32)
        return jnp.concatenate([head, tab], axis=0)

    ret = (cat([rc, rc] * 4), cat([-rs, zero(16)] * 4), cat([zero(16), rs] * 4))
    ax_c = [rwc, rwc, clc, clc]
    ax_a = [-rws, zero(8), -cls, zero(8)]
    ax_b = [zero(8), rws, zero(8), cls]
    qt = (cat([one(64)] + ax_c + [one(32)]), cat([zero(64)] + ax_a + [zero(32)]), cat([zero(64)] + ax_b + [zero(32)]))
    kt = (cat([one(32)] + ax_c + [one(64)]), cat([zero(32)] + ax_a + [zero(64)]), cat([zero(32)] + ax_b + [zero(64)]))
    fix = lambda t3: tuple(with_ctx(t, k == 0) for k, t in enumerate(t3))
    return fix(ret), fix(qt), fix(kt)


_IN_ORDER = ((0, 128), (128, 256), (256, 512), (544, 800), (800, 928), (928, 1056), (1056, 1312), (1312, 1568),
             (1568, 1824), (1824, 1952), (512, 544), (1952, 1984))


def permute_w_in(w):
    parts = [w[:, a:b] for a, b in _IN_ORDER] + [jnp.zeros((w.shape[0], D_INP - D_IN), w.dtype)]
    return jnp.concatenate(parts, axis=1)


def unpermute_w_in(g):
    out, at = {}, 0
    for a, b in _IN_ORDER:
        out[a] = g[:, at:at + b - a]
        at += b - a
    return jnp.concatenate([out[a] for a in sorted(out)], axis=1)


def layer_weights(w, l):
    f = lambda a: a.astype(F32)
    r = {}
    r["win"] = permute_w_in(w["w_in"][l]).astype(BF)
    r["wout"] = w["w_out"][l].astype(BF)
    r["wup"] = w["ffn_up"][l].astype(BF)
    r["wdown"] = w["ffn_down"][l].astype(BF)
    uq = w["mla_w_uq"][l].reshape(256, MLA_HEADS, 96)
    r["wuq"] = jnp.pad(uq, ((0, 0), (0, 0), (0, 32))).reshape(256, 8 * LANES).astype(BF)
    uk = w["mla_w_uk"][l].reshape(128, MLA_HEADS, 64)
    r["wuk"] = jnp.pad(uk, ((0, 0), (0, 0), (0, 64))).reshape(128, 8 * LANES).astype(BF)
    r["wuv"] = w["mla_w_uv"][l].astype(BF)
    gw = f(w["gla_gate_w"][l])
    z16 = jnp.zeros((16, LANES), F32)
    z96 = jnp.zeros((96, LANES), F32)
    r["gw"] = (jnp.concatenate([gw[0], z16, z96], axis=0), jnp.concatenate([z16, gw[1], z96], axis=0))
    r["gb"] = tuple(f(w["gla_gate_b"][l][d]).reshape(1, LANES) for d in range(2))
    r["rdec"] = tuple(jnp.repeat(f(w["ret_decay"][l][d]), 32).reshape(1, LANES) for d in range(2))
    r["gng"] = jnp.tile(f(w["gla_norm_g"][l]), 4).reshape(1, 256)
    r["gq"] = f(w["mla_q_norm_g"][l]).reshape(1, 256)
    r["gkv"] = f(w["mla_kv_norm_g"][l]).reshape(1, 128)
    for n in ("ln1_g", "ln1_b", "ln2_g", "ln2_b"):
        r[n] = f(w[n][l]).reshape(1, D_MODEL)
    r["cw"] = f(w["ffn_conv_w"][l])
    r["cb"] = f(w["ffn_conv_b"][l]).reshape(1, 2 * D_FF)
    return r


def tile_params(mod_l, nb, nct, nlt):
    m6 = mod_l.reshape(8, 6, D_MODEL)
    out = []
    for j in range(6):
        parts = []
        for b in range(nb):
            parts.append(jnp.broadcast_to(m6[4, j], (nct, 1, D_MODEL)))
            parts.append(jnp.broadcast_to(m6[b, j], (nlt, 1, D_MODEL)))
        out.append(jnp.concatenate(parts, axis=0))
    return out


def tile_param_grads(dts, nb, nct, nlt):
    cols = []
    for dt in dts:
        d = dt.reshape(nb, nct + nlt, D_MODEL)
        lat = jnp.sum(d[:, nct:], axis=1)
        ctx = jnp.sum(d[:, :nct], axis=(0, 1))
        cols.append(jnp.concatenate([lat, jnp.zeros((4 - nb, D_MODEL), F32), ctx[None], jnp.zeros((3, D_MODEL), F32)], axis=0))
    return jnp.stack(cols, axis=1).reshape(8, 6 * D_MODEL)


def layer_forward(x, h1, tp, tp_next, lw, tabs, dims, tag):
    nb, lc, lt = dims
    t = x.shape[0]
    nbt = lt // ROW_TILE
    ncc, nch = lc // CHUNK, lt // CHUNK
    tm = _pick(t, (1024, 768, 512, 256))
    ret_tab, q_tab, k_tab = tabs
    full = lambda a: (a, a.shape[1], 0)
    p = matmul(h1, lw["win"], "nn", F32, tm, D_INP, 1024, "proj_in")
    ogf, orf, sgf, srf = scan_fwd(p, ret_tab, lw["gw"][0], lw["gb"][0], lw["rdec"][0], nb, ncc, nch, False, "scan_fwd_f")
    ogb, orb, sgb, srb = scan_fwd(p, ret_tab, lw["gw"][1], lw["gb"][1], lw["rdec"][1], nb, ncc, nch, True, "scan_fwd_b")
    prep_rows = [(p, 256, C_CQ // 256), (p, 128, C_CKV // 128), (p, 128, C_LK // 128)]
    prep_consts = [(a, nbt) for a in q_tab + k_tab]
    prep_ws = [lw["gq"], lw["gkv"], lw["wuq"], lw["wuk"], lw["wuv"]]
    q, k, v = stage_fwd(fn_mla_prep, prep_rows, [], prep_consts, prep_ws, [(1024, BF), (1024, BF), (512, BF)], "mla_prep")
    mo, lse = mla_fwd(q, k, v, nb, lc, lt, "mla_attn")
    mix_rows = [full(ogf), full(ogb), full(orf), full(orb), (p, 256, C_GG // 256), (p, 256, C_RG // 256), full(mo)]
    m, = stage_fwd(fn_mix, mix_rows, [], [], [lw["gng"]], [(1024, BF)], "mix")
    a = matmul(m, lw["wout"], "nn", F32, tm, 1024, 1024, "proj_out")
    x1, h2 = stage_fwd(fn_post_mod, [full(x), full(a)], [tp[2], tp[4], tp[3]], [], [lw["ln1_g"], lw["ln1_b"]],
                       [(1024, F32), (1024, BF)], "post1")
    u = matmul(h2, lw["wup"], "nn", BF, tm, 1408, 1024, "ffn_up", b_outer=True)
    act = gate_fwd(u, lw["cw"], lw["cb"], lt, lc, "ffn_gate")
    f = matmul(act, lw["wdown"], "nn", F32, tm, 1024, D_FF, "ffn_down")
    if tp_next is None:
        x2, = stage_fwd(fn_post, [full(x1), full(f)], [tp[5]], [], [lw["ln2_g"], lw["ln2_b"]], [(1024, F32)], "post2_last")
        h1n = None
    else:
        x2, h1n = stage_fwd(fn_post_mod, [full(x1), full(f)], [tp[5], tp_next[0], tp_next[1]], [],
                            [lw["ln2_g"], lw["ln2_b"]], [(1024, F32), (1024, BF)], "post2")
    res = dict(x=x, h1=h1, p=p, ogf=ogf, orf=orf, sgf=sgf, srf=srf, ogb=ogb, orb=orb, sgb=sgb, srb=srb, q=q, k=k, v=v,
               mo=mo, lse=lse, m=m, a=a, x1=x1, h2=h2, u=u, act=act, f=f, mix_rows=mix_rows, prep_rows=prep_rows,
               prep_consts=prep_consts, prep_ws=prep_ws)
    return x2, h1n, res


def layer_backward(dx2, dh1n, res, tp, tp_next, lw, tabs, dims):
    nb, lc, lt = dims
    r = res
    t = dx2.shape[0]
    ncc, nch = lc // CHUNK, lt // CHUNK
    tm = _pick(t, (1024, 768, 512, 256))
    tkr = _pick(t, (2304, 1536, 1024, 768, 512))
    ret_tab = tabs[0]
    full = lambda a: (a, a.shape[1], 0)
    g = {}
    if tp_next is None:
        (dx1a, df), (dg2,), (g["ln2_g"], g["ln2_b"]) = stage_bwd(
            fn_post, [full(r["x1"]), full(r["f"])], [tp[5]], [], [lw["ln2_g"], lw["ln2_b"]], [dx2], [F32, BF], "post2_last_bwd")
        dnext = None
    else:
        (dx1a, df), (dg2, dsc1n, dsh1n), (g["ln2_g"], g["ln2_b"]) = stage_bwd(
            fn_post_mod, [full(r["x1"]), full(r["f"])], [tp[5], tp_next[0], tp_next[1]], [], [lw["ln2_g"], lw["ln2_b"]],
            [dx2, dh1n], [F32, BF], "post2_bwd")
        dnext = (dsc1n, dsh1n)
    dact = matmul(df, lw["wdown"], "nt", F32, tm, 1408, 1024, "ffn_down_dx", b_outer=True)
    g["ffn_down"] = matmul(r["act"], df, "tn", F32, 1408, 1024, tkr, "ffn_down_dw")
    dc, dcw = gate_bwd(r["u"], lw["cw"], lw["cb"], dact, lt, lc, "ffn_gate_bwd")
    g["ffn_conv_w"], g["ffn_conv_b"] = dcw[0:3], dcw[3]
    du = conv_transpose(dc, lw["cw"], lt, lc, "ffn_conv_t")
    dh2 = matmul(du, lw["wup"], "nt", F32, _pick(t, (512, 256)), 1024, 2 * D_FF, "ffn_up_dx")
    g["ffn_up"] = matmul(r["h2"], du, "tn", F32, 1024, 1408, tkr, "ffn_up_dw")
    (dxa, da), (dg1, dsc2, dsh2), (g["ln1_g"], g["ln1_b"]) = stage_bwd(
        fn_post_mod, [full(r["x"]), full(r["a"])], [tp[2], tp[4], tp[3]], [], [lw["ln1_g"], lw["ln1_b"]],
        [dx1a, dh2], [F32, BF], "post1_bwd")
    dm = matmul(da, lw["wout"], "nt", F32, tm, 1024, 1024, "proj_out_dx")
    g["w_out"] = matmul(r["m"], da, "tn", F32, 1024, 1024, tkr, "proj_out_dw")
    (dog, _, dor, _, dpg, dpr, dmo), _, (dgng,) = stage_bwd(
        fn_mix, r["mix_rows"], [], [], [lw["gng"]], [dm], [F32, None, F32, None, F32, F32, F32], "mix_bwd")
    g["gla_norm_g"] = jnp.sum(dgng.reshape(4, 64), axis=0)
    dq, dk, dv = mla_bwd(r["q"], r["k"], r["v"], r["mo"], r["lse"], dmo, nb, lc, lt, "mla_attn_bwd")
    (dpq, dpkv, dplk), _, (dgq, dgkv, dwuq, dwuk, dwuv) = stage_bwd(
        fn_mla_prep, r["prep_rows"], [], r["prep_consts"], r["prep_ws"], [dq, dk, dv], [F32, F32, F32], "mla_prep_bwd")
    g["mla_q_norm_g"], g["mla_kv_norm_g"] = dgq.reshape(256), dgkv.reshape(128)
    g["mla_w_uq"] = dwuq.reshape(256, MLA_HEADS, LANES)[:, :, :96].reshape(256, MLA_HEADS * 96)
    g["mla_w_uk"] = dwuk.reshape(128, MLA_HEADS, LANES)[:, :, :64].reshape(128, MLA_HEADS * 64)
    g["mla_w_uv"] = dwuv
    s7, dgw0, dgb0, drd0 = scan_bwd(r["p"], ret_tab, lw["gw"][0], lw["gb"][0], lw["rdec"][0], r["sgf"], r["srf"], dog, dor,
                                    None, nb, ncc, nch, False, "scan_bwd_f")
    s7, dgw1, dgb1, drd1 = scan_bwd(r["p"], ret_tab, lw["gw"][1], lw["gb"][1], lw["rdec"][1], r["sgb"], r["srb"], dog, dor,
                                    s7, nb, ncc, nch, True, "scan_bwd_b")
    g["gla_gate_w"] = jnp.stack([dgw0[0:16], dgw1[16:32]])
    g["gla_gate_b"] = jnp.stack([dgb0[0], dgb1[0]])
    g["ret_decay"] = jnp.stack([jnp.sum(drd0.reshape(4, 32), axis=1), jnp.sum(drd1.reshape(4, 32), axis=1)])
    gq_, gk_, gv_, glrk, rq_, rk_, rv_ = s7
    pieces = [gq_, gk_, gv_, dpg, rq_, rk_, rv_, dpr, dpq, dpkv, glrk, dplk]
    dp, = stage_fwd(fn_assemble, [full(a) for a in pieces], [], [], [], [(D_INP, BF)], "dproj_assemble")
    dh1 = matmul(dp, lw["win"], "nt", F32, tm, 1024, D_INP, "proj_in_dx")
    g["w_in"] = unpermute_w_in(matmul(r["h1"], dp, "tn", F32, 1024, 1024, tkr, "proj_in_dw"))
    for n in ("ln1_g", "ln1_b", "ln2_g", "ln2_b"):
        g[n] = g[n].reshape(D_MODEL)
    dtp = [None, None, dg1, dsh2, dsc2, dg2]
    return dxa, dh1, dtp, dnext, g


def local_step(x, c, ctx, c_ctx, w, loss_target):
    nb, l, _ = x.shape
    lc = ctx.shape[1]
    lt = lc + l
    dims = (nb, lc, lt)
    nct, nlt = lc // ROW_TILE, l // ROW_TILE
    tabs = rope_tables(lc, l)
    x0 = jnp.concatenate([ctx, x], axis=1).reshape(nb * lt, D_MODEL)
    s8 = jnp.concatenate([silu(c), jnp.zeros((4 - nb, D_MODEL), F32), silu(c_ctx)[None], jnp.zeros((3, D_MODEL), F32)], axis=0)
    lws, tps = [], []
    for i in range(DEPTH):
        mod = matmul(s8, w["ada_w"][i].astype(BF), "nn", F32, 8, 1536, 1024, "ada_mod") + w["ada_b"][i].astype(F32)[None]
        tps.append(tile_params(mod, nb, nct, nlt))
        lws.append(layer_weights(w, i))
    h1, = stage_fwd(fn_modulate, [(x0, D_MODEL, 0)], [tps[0][1], tps[0][0]], [], [], [(D_MODEL, BF)], "mod_in")
    xs, ress = x0, []
    for i in range(DEPTH):
        tpn = None if i == DEPTH - 1 else (tps[i + 1][1], tps[i + 1][0])
        xs, h1, res = layer_forward(xs, h1, tps[i], tpn, lws[i], tabs, dims, i)
        ress.append(res)
    dx, lparts = loss_head(xs, loss_target.reshape(nb * l, D_MODEL), nct + nlt, nct, "loss_head")
    loss = jnp.sum(lparts[:, 0, 0])
    grads = [None] * DEPTH
    dtps = [None] * DEPTH
    dh1 = None
    for i in reversed(range(DEPTH)):
        tpn = None if i == DEPTH - 1 else (tps[i + 1][1], tps[i + 1][0])
        dx, dh1, dtp, dn, grads[i] = layer_backward(dx, dh1, ress[i], tps[i], tpn, lws[i], tabs, dims)
        if dn is not None:
            dtps[i + 1][1], dtps[i + 1][0] = dn
        dtps[i] = dtp
    (dx0b,), (dsc1, dsh1), _ = stage_bwd(fn_modulate, [(x0, D_MODEL, 0)], [tps[0][1], tps[0][0]], [], [], [dh1], [F32], "mod_in_bwd")
    dtps[0][1], dtps[0][0] = dsc1, dsh1
    grad_x = (dx + dx0b).reshape(nb, lt, D_MODEL)[:, lc:]
    dmod = jnp.stack([tile_param_grads(d, nb, nct, nlt) for d in dtps])
    gw = {n: jnp.stack([grads[i][n] for i in range(DEPTH)]) for n in grads[0]}
    return loss, grad_x, gw, dmod, s8


MESH_IDS = pl.DeviceIdType.MESH
ANY = pl.BlockSpec(memory_space=pl.ANY)


def _place():
    return lax.axis_index("x"), lax.axis_index("y"), lax.axis_index("c")


def _dma_sems(n, per):
    return pltpu.SemaphoreType.DMA((n, per))


def all_gather8(blocks, name):
    n = len(blocks)

    def body(*refs):
        x_refs, out_refs, (send_sems, recv_sems, local_sems) = refs[:n], refs[n:2 * n], refs[2 * n:]
        x, y, c = _place()
        me, sibling = (x, y, c), (x, y, 1 - c)
        chips = [(1 - x, y), (x, 1 - y), (1 - x, 1 - y)]

        def copy(a, k, blk, to, own=False):
            slot = out_refs[a].at[4 * blk[0] + 2 * blk[1] + blk[2]]
            return pltpu.make_async_remote_copy(
                src_ref=x_refs[a] if own else slot, dst_ref=slot,
                send_sem=send_sems.at[a, k], recv_sem=recv_sems.at[a, k], device_id=to, device_id_type=MESH_IDS)

        mine = [pltpu.make_async_copy(x_refs[a], out_refs[a].at[4 * x + 2 * y + c], local_sems.at[a, 0]) for a in range(n)]
        first = []
        for a in range(n):
            mine[a].start()
            first.append(copy(a, 0, me, sibling, own=True))
            first += [copy(a, 1 + j, me, (*chip, c), own=True) for j, chip in enumerate(chips)]
        for cp in first:
            cp.start()
        passed = []
        for j, chip in enumerate(chips):
            for a in range(n):
                copy(a, 1 + j, (*chip, c), me).wait_recv()
                passed.append(copy(a, 4 + j, (*chip, c), sibling))
                passed[-1].start()
        for a in range(n):
            copy(a, 0, sibling, me).wait_recv()
            for j, chip in enumerate(chips):
                copy(a, 4 + j, (*chip, 1 - c), me).wait_recv()
        for cp in first + passed:
            cp.wait_send()
        for cp in mine:
            cp.wait()

    return pl.pallas_call(
        body, name=name, out_shape=[jax.ShapeDtypeStruct((8,) + b.shape, b.dtype) for b in blocks],
        in_specs=[ANY] * n, out_specs=[ANY] * n,
        scratch_shapes=[_dma_sems(n, 7), _dma_sems(n, 7), _dma_sems(n, 1)],
    )(*blocks)


def swap_cores(blocks, name):
    n = len(blocks)

    def body(*refs):
        x_refs, out_refs, (send_sems, recv_sems) = refs[:n], refs[n:2 * n], refs[2 * n:]
        x, y, c = _place()
        cps = [pltpu.make_async_remote_copy(src_ref=x_refs[a], dst_ref=out_refs[a], send_sem=send_sems.at[a, 0],
                                            recv_sem=recv_sems.at[a, 0], device_id=(x, y, 1 - c), device_id_type=MESH_IDS)
               for a in range(n)]
        for cp in cps:
            cp.start()
        for cp in cps:
            cp.wait()

    return pl.pallas_call(
        body, name=name, out_shape=[jax.ShapeDtypeStruct(b.shape, b.dtype) for b in blocks],
        in_specs=[ANY] * n, out_specs=[ANY] * n, scratch_shapes=[_dma_sems(n, 1), _dma_sems(n, 1)],
    )(*blocks)


def exchange_chips(parts, name):
    n = len(parts)

    def body(*refs):
        p_refs, out_refs, (send_sems, recv_sems, local_sems) = refs[:n], refs[n:2 * n], refs[2 * n:]
        x, y, c = _place()
        jm = 2 * x + y
        chips = [(1 - x, y), (x, 1 - y), (1 - x, 1 - y)]
        mine = [pltpu.make_async_copy(p_refs[a].at[jm], out_refs[a].at[jm], local_sems.at[a, 0]) for a in range(n)]
        cps = []
        for k, (px, py) in enumerate(chips):
            for a in range(n):
                cps.append(pltpu.make_async_remote_copy(
                    src_ref=p_refs[a].at[2 * px + py], dst_ref=out_refs[a].at[jm], send_sem=send_sems.at[a, k],
                    recv_sem=recv_sems.at[a, k], device_id=(px, py, c), device_id_type=MESH_IDS))
        for cp in mine + cps:
            cp.start()
        for k, (px, py) in enumerate(chips):
            for a in range(n):
                pltpu.make_async_remote_copy(
                    src_ref=p_refs[a].at[jm], dst_ref=out_refs[a].at[2 * px + py], send_sem=send_sems.at[a, k],
                    recv_sem=recv_sems.at[a, k], device_id=(px, py, c), device_id_type=MESH_IDS).wait_recv()
        for cp in cps:
            cp.wait_send()
        for cp in mine:
            cp.wait()

    return pl.pallas_call(
        body, name=name, out_shape=[jax.ShapeDtypeStruct(p.shape, p.dtype) for p in parts],
        in_specs=[ANY] * n, out_specs=[ANY] * n, scratch_shapes=[_dma_sems(n, 3), _dma_sems(n, 3), _dma_sems(n, 1)],
    )(*parts)


def _row_tile(rows, cols, itemsize=4, limit=1 << 21):
    for cand in (2048, 1024, 512, 256, 128, 64, 32, 16):
        if rows % cand == 0 and cand * cols * itemsize <= limit:
            return cand
    return rows


def add_halves(a, b, kind, name):
    if kind == "col":
        _, k, n = a.shape
        n4 = n // 4
        tr = _row_tile(k, n4)
        in_spec = pl.BlockSpec((1, tr, n4), lambda j, h, i: (h, i, j))
        out_spec = pl.BlockSpec((1, 1, tr, n4), lambda j, h, i: (j, h, i, 0))
        grid, out_shape = (4, 2, k // tr), (4, 2, k, n4)
    elif kind == "row":
        _, k, n = a.shape
        k4 = k // 4
        tr = _row_tile(k4, n)
        nt = k4 // tr
        in_spec = pl.BlockSpec((1, tr, n), lambda j, h, i: (h, j * nt + i, 0))
        out_spec = pl.BlockSpec((1, 1, tr, n), lambda j, h, i: (j, h, i, 0))
        grid, out_shape = (4, 2, nt), (4, 2, k4, n)
    else:
        _, _, k, n = a.shape
        tr = _row_tile(k, n)
        in_spec = out_spec = pl.BlockSpec((1, 1, tr, n), lambda j, h, i: (j, h, i, 0))
        grid, out_shape = (4, 2, k // tr), a.shape

    def body(a_ref, b_ref, s_ref):
        s_ref[...] = (a_ref[...] + b_ref[...].astype(F32)).astype(BF).reshape(s_ref.shape)

    return pl.pallas_call(body, name=name, grid=grid, in_specs=[in_spec, in_spec], out_specs=out_spec,
                          out_shape=jax.ShapeDtypeStruct(out_shape, BF))(a, b)


def sum_slots(a, name):
    s, m, k, n = a.shape
    tr = _row_tile(k, n, limit=(1 << 22) // s)

    def body(a_ref, o_ref):
        acc = a_ref[0].astype(F32)
        for j in range(1, s):
            acc = acc + a_ref[j].astype(F32)
        o_ref[...] = acc

    return pl.pallas_call(body, name=name, grid=(m, k // tr), in_specs=[pl.BlockSpec((s, 1, tr, n), lambda h, i: (0, h, i, 0))],
                          out_specs=pl.BlockSpec((1, tr, n), lambda h, i: (h, i, 0)),
                          out_shape=jax.ShapeDtypeStruct((m, k, n), F32))(a)


def sum_small(arrays, name):
    n = len(arrays)

    def body(*refs):
        for a_ref, o_ref in zip(refs[:n], refs[n:]):
            acc = a_ref[0]
            for j in range(1, 8):
                acc = acc + a_ref[j]
            o_ref[...] = acc

    return pl.pallas_call(body, name=name, out_shape=[jax.ShapeDtypeStruct(a.shape[1:], F32) for a in arrays])(*arrays)


COL_SHARDED = ("ada_w", "w_in", "mla_w_uq", "mla_w_uk", "mla_w_uv", "ffn_up", "ffn_conv_w")
ROW_SHARDED = ("w_out", "ffn_down")
GATHERED = ("ada_w", "w_in", "mla_w_uq", "mla_w_uk", "mla_w_uv", "w_out", "ffn_up", "ffn_down", "ffn_conv_w")
REDUCED = ("w_in", "mla_w_uq", "mla_w_uk", "mla_w_uv", "w_out", "ffn_up", "ffn_down")
SMALL = ("ada_b", "gla_gate_w", "gla_gate_b", "gla_norm_g", "ret_decay", "mla_q_norm_g", "mla_kv_norm_g",
         "ln1_g", "ln1_b", "ffn_conv_b", "ln2_g", "ln2_b")
WEIGHTS = ("c_ctx", "ada_w", "ada_b", "w_in", "gla_gate_w", "gla_gate_b", "gla_norm_g", "ret_decay", "mla_q_norm_g",
           "mla_kv_norm_g", "mla_w_uq", "mla_w_uk", "mla_w_uv", "w_out", "ln1_g", "ln1_b", "ffn_up", "ffn_conv_w",
           "ffn_conv_b", "ffn_down", "ln2_g", "ln2_b")
PACK = 16 * LANES
HALF_LAYERS = DEPTH // 2


def _pad_flat(v, n):
    return jnp.concatenate([v, jnp.zeros((n - v.shape[0],), v.dtype)]) if n > v.shape[0] else v


def _my_layers(a, c):
    return lax.dynamic_slice_in_dim(a, HALF_LAYERS * c, HALF_LAYERS, axis=0)


def gather_weights(shards, c):
    blocks = [_my_layers(shards[n], c) if n == "ffn_conv_w" else _my_layers(shards[n], c).astype(BF) for n in GATHERED]
    got = all_gather8(blocks, "gather_weights")
    out = {}
    for n, g in zip(GATHERED, got):
        _, _, k, n4 = g.shape
        g = g.reshape(4, 2, HALF_LAYERS, k, n4)
        if n in ROW_SHARDED:
            out[n] = jnp.transpose(g, (1, 2, 0, 3, 4)).reshape(DEPTH, 4 * k, n4)
        else:
            out[n] = jnp.transpose(g, (1, 2, 3, 0, 4)).reshape(DEPTH, k, 4 * n4)
    return out


def reduce_gradients(gw, c):
    keep, give, kinds = [], [], []
    for n in REDUCED:
        g = gw[n]
        kind = "row" if n in ROW_SHARDED else ("col" if (g.shape[2] // 4) % LANES == 0 else "pre")
        if kind == "pre":
            g = jnp.transpose(g.reshape(DEPTH, g.shape[1], 4, g.shape[2] // 4), (2, 0, 1, 3))
            keep.append(lax.dynamic_slice_in_dim(g, HALF_LAYERS * c, HALF_LAYERS, axis=1))
            give.append(lax.dynamic_slice_in_dim(g, HALF_LAYERS * (1 - c), HALF_LAYERS, axis=1).astype(BF))
        else:
            keep.append(_my_layers(g, c))
            give.append(_my_layers(g, 1 - c).astype(BF))
        kinds.append(kind)
    got = swap_cores(give, "grad_swap_cores")
    parts = [add_halves(a, b, kind, "grad_add_cores_" + n) for n, a, b, kind in zip(REDUCED, keep, got, kinds)]
    landed = exchange_chips(parts, "grad_exchange_chips")
    mine = [sum_slots(a, "grad_sum_chips_" + n) for n, a in zip(REDUCED, landed)]
    theirs = swap_cores(mine, "grad_swap_back")
    out = {}
    for n, a, b in zip(REDUCED, mine, theirs):
        out[n] = jnp.concatenate([jnp.where(c == 0, a, b), jnp.where(c == 0, b, a)], axis=0)
    return out


def _pack_small(d, names):
    flat = jnp.concatenate([d[n].astype(F32).reshape(-1) for n in names])
    n = -(-flat.shape[0] // PACK) * PACK
    return _pad_flat(flat, n).reshape(n // LANES, LANES)


def _unpack_small(buf, like, names):
    flat, out, at = buf.reshape(-1), {}, 0
    for n in names:
        sz = like[n].size
        out[n] = flat[at:at + sz].reshape(like[n].shape)
        at += sz
    return out


def kernel(x, c, ctx, c_ctx, ada_w, ada_b, w_in, gla_gate_w, gla_gate_b, gla_norm_g, ret_decay, mla_q_norm_g, mla_kv_norm_g, mla_w_uq, mla_w_uk, mla_w_uv, w_out, ln1_g, ln1_b, ffn_up, ffn_conv_w, ffn_conv_b, ffn_down, ln2_g, ln2_b, loss_target, m_c_ctx, m_ada_w, m_ada_b, m_w_in, m_gla_gate_w, m_gla_gate_b, m_gla_norm_g, m_ret_decay, m_mla_q_norm_g, m_mla_kv_norm_g, m_mla_w_uq, m_mla_w_uk, m_mla_w_uv, m_w_out, m_ln1_g, m_ln1_b, m_ffn_up, m_ffn_conv_w, m_ffn_conv_b, m_ffn_down, m_ln2_g, m_ln2_b, v_c_ctx, v_ada_w, v_ada_b, v_w_in, v_gla_gate_w, v_gla_gate_b, v_gla_norm_g, v_ret_decay, v_mla_q_norm_g, v_mla_kv_norm_g, v_mla_w_uq, v_mla_w_uk, v_mla_w_uv, v_w_out, v_ln1_g, v_ln1_b, v_ffn_up, v_ffn_conv_w, v_ffn_conv_b, v_ffn_down, v_ln2_g, v_ln2_b):
    w = dict(c_ctx=c_ctx, ada_w=ada_w, ada_b=ada_b, w_in=w_in, gla_gate_w=gla_gate_w, gla_gate_b=gla_gate_b, gla_norm_g=gla_norm_g, ret_decay=ret_decay, mla_q_norm_g=mla_q_norm_g, mla_kv_norm_g=mla_kv_norm_g, mla_w_uq=mla_w_uq, mla_w_uk=mla_w_uk, mla_w_uv=mla_w_uv, w_out=w_out, ln1_g=ln1_g, ln1_b=ln1_b, ffn_up=ffn_up, ffn_conv_w=ffn_conv_w, ffn_conv_b=ffn_conv_b, ffn_down=ffn_down, ln2_g=ln2_g, ln2_b=ln2_b)
    m = dict(c_ctx=m_c_ctx, ada_w=m_ada_w, ada_b=m_ada_b, w_in=m_w_in, gla_gate_w=m_gla_gate_w, gla_gate_b=m_gla_gate_b, gla_norm_g=m_gla_norm_g, ret_decay=m_ret_decay, mla_q_norm_g=m_mla_q_norm_g, mla_kv_norm_g=m_mla_kv_norm_g, mla_w_uq=m_mla_w_uq, mla_w_uk=m_mla_w_uk, mla_w_uv=m_mla_w_uv, w_out=m_w_out, ln1_g=m_ln1_g, ln1_b=m_ln1_b, ffn_up=m_ffn_up, ffn_conv_w=m_ffn_conv_w, ffn_conv_b=m_ffn_conv_b, ffn_down=m_ffn_down, ln2_g=m_ln2_g, ln2_b=m_ln2_b)
    v = dict(c_ctx=v_c_ctx, ada_w=v_ada_w, ada_b=v_ada_b, w_in=v_w_in, gla_gate_w=v_gla_gate_w, gla_gate_b=v_gla_gate_b, gla_norm_g=v_gla_norm_g, ret_decay=v_ret_decay, mla_q_norm_g=v_mla_q_norm_g, mla_kv_norm_g=v_mla_kv_norm_g, mla_w_uq=v_mla_w_uq, mla_w_uk=v_mla_w_uk, mla_w_uv=v_mla_w_uv, w_out=v_w_out, ln1_g=v_ln1_g, ln1_b=v_ln1_b, ffn_up=v_ffn_up, ffn_conv_w=v_ffn_conv_w, ffn_conv_b=v_ffn_conv_b, ffn_down=v_ffn_down, ln2_g=v_ln2_g, ln2_b=v_ln2_b)
    xi, yi, ci = _place()
    chip = 2 * xi + yi

    full = dict(w)
    full.update(gather_weights({n: w[n] for n in GATHERED}, ci))
    loss, grad_x, gw, dmod, s8 = local_step(x, c, ctx, c_ctx, {n: full[n] for n in WEIGHTS if n != "c_ctx"}, loss_target)
    loss = lax.psum(loss, ("x", "y", "c"))

    dsil = jnp.zeros((8, D_MODEL), F32)
    for i in range(DEPTH):
        dsil = dsil + matmul(dmod[i], full["ada_w"][i], "nt", F32, 8, 1024, 1536, "ada_dsilu")

    grads = reduce_gradients(gw, ci)

    small = {n: gw[n] for n in SMALL if n != "ada_b"}
    small.update(dsil=dsil[4])
    names = tuple(small)
    conv_g = gw["ffn_conv_w"].reshape(DEPTH * 3, 2 * D_FF)
    ev_small, ev_dmod, ev_s8, ev_conv = all_gather8(
        [_pack_small(small, names), dmod.reshape(DEPTH * 8, 6 * D_MODEL), s8, conv_g], "gather_small")
    sm_small, sm_dmod, sm_conv = sum_small([ev_small, ev_dmod, ev_conv], "sum_small")
    summed = _unpack_small(sm_small, small, names)
    for n in SMALL:
        if n != "ada_b":
            grads[n] = summed[n]
    grads["ada_b"] = jnp.sum(sm_dmod.reshape(DEPTH, 8, 6 * D_MODEL)[:, :5], axis=1)
    sg = jax.nn.sigmoid(c_ctx)
    grads["c_ctx"] = summed["dsil"] * (sg * (1.0 + c_ctx * (1.0 - sg)))
    ccols = ffn_conv_w.shape[2]
    grads["ffn_conv_w"] = lax.dynamic_slice_in_dim(sm_conv.reshape(DEPTH, 3, 2 * D_FF), chip * ccols, ccols, axis=2)
    s_all = ev_s8.reshape(64, D_MODEL)
    d_all = jnp.transpose(ev_dmod.reshape(8, DEPTH, 8, 6 * D_MODEL), (1, 0, 2, 3)).reshape(DEPTH, 64, 6 * D_MODEL)
    cols = ada_w.shape[2]
    g_ada = []
    for i in range(DEPTH):
        d_mine = lax.dynamic_slice_in_dim(d_all[i], chip * cols, cols, axis=1)
        g_ada.append(matmul(s_all, d_mine, "tn", F32, 1024, cols, 64, "ada_dw"))
    grads["ada_w"] = jnp.stack(g_ada)

    delta, new_m, new_v = {}, {}, {}
    for n in GATHERED:
        shp = w[n].shape
        v2 = lambda a: a.reshape(-1, shp[-1])
        d_, m_, v_ = adamw(v2(w[n]), v2(grads[n]), v2(m[n]), v2(v[n]), "adamw_" + n)
        delta[n], new_m[n], new_v[n] = d_.reshape(shp), m_.reshape(shp), v_.reshape(shp)
    rep = tuple(n for n in WEIGHTS if n not in GATHERED)
    pk = lambda d: _pack_small({n: d[n] for n in rep}, rep)
    d_, m_, v_ = adamw(pk(w), pk(grads), pk(m), pk(v), "adamw_small")
    like = {n: w[n] for n in rep}
    delta.update(_unpack_small(d_, like, rep))
    new_m.update(_unpack_small(m_, like, rep))
    new_v.update(_unpack_small(v_, like, rep))
    grads = {n: grads[n].reshape(w[n].shape) for n in WEIGHTS}
    return (loss, grad_x, *[grads[n] for n in WEIGHTS], *[delta[n] for n in WEIGHTS], *[new_m[n] for n in WEIGHTS],
            *[new_v[n] for n in WEIGHTS])
```

```python
import functools

import jax
import jax.numpy as jnp
from jax import lax
from jax.experimental import pallas as pl
from jax.experimental.pallas import tpu as pltpu

F32 = jnp.float32
BF = jnp.bfloat16

D_MODEL = 1024
DEPTH = 4
GRID_W = 64
GLA_DK = 32
GLA_TAU = 16.0
RET_DK = 32
MLA_HEADS = 8
MLA_D_NOPE = 64
MLA_D_ROPE = 32
MLA_SCALE = (MLA_D_NOPE + MLA_D_ROPE) ** -0.5
D_FF = 2816
ROPE_BASE = 10000.0
EPS = 1e-6
ALPHA = (2 * DEPTH) ** 0.25
ADAM_LR, ADAM_B1, ADAM_B2, ADAM_EPS, ADAM_WD, ADAM_STEP = 0.001, 0.9, 0.999, 1e-08, 0.01, 10

ROW_TILE = 256
CHUNK = 64
GATE_ROWS = 128
LANES = 128

C_GQ, C_GK, C_GV, C_GG, C_RQ, C_RK, C_RV, C_RG, C_CQ, C_CKV, C_LK = 0, 128, 256, 512, 768, 896, 1024, 1280, 1536, 1792, 1920
D_INP = 2048
D_IN = 1984


def _dg(a, b, ca, cb):
    return lax.dot_general(a.astype(BF), b.astype(BF), (((ca,), (cb,)), ((), ())), preferred_element_type=F32)


@jax.custom_vjp
def mm_nn(a, b):
    return _dg(a, b, 1, 0)


@jax.custom_vjp
def mm_nt(a, b):
    return _dg(a, b, 1, 1)


@jax.custom_vjp
def mm_tn(a, b):
    return _dg(a, b, 0, 0)


mm_nn.defvjp(lambda a, b: (_dg(a, b, 1, 0), (a, b)),
             lambda r, g: (mm_nt(g, r[1]).astype(r[0].dtype), mm_tn(r[0], g).astype(r[1].dtype)))
mm_nt.defvjp(lambda a, b: (_dg(a, b, 1, 1), (a, b)),
             lambda r, g: (mm_nn(g, r[1]).astype(r[0].dtype), mm_tn(g, r[0]).astype(r[1].dtype)))
mm_tn.defvjp(lambda a, b: (_dg(a, b, 0, 0), (a, b)),
             lambda r, g: (mm_nt(r[1], g).astype(r[0].dtype), mm_nn(r[0], g).astype(r[1].dtype)))


def _split3(x):
    h = x.astype(BF)
    r = x - h.astype(F32)
    m = r.astype(BF)
    lo = (r - m.astype(F32)).astype(BF)
    return h, m, lo


def _exact(x, mat, left):
    h, m, lo = _split3(x)
    if left:
        d = lambda t: lax.dot_general(mat, t, (((1,), (0,)), ((), ())), preferred_element_type=F32)
    else:
        d = lambda t: lax.dot_general(t, mat, (((1,), (0,)), ((), ())), preferred_element_type=F32)
    return (d(lo) + d(m)) + d(h)


def _iota(shape, axis):
    return lax.broadcasted_iota(jnp.int32, shape, axis)


def _tri(n, upper):
    r, c = _iota((n, n), 0), _iota((n, n), 1)
    return jnp.where((c >= r) if upper else (r >= c), 1.0, 0.0).astype(BF)


@functools.partial(jax.custom_vjp, nondiff_argnums=(1,))
def cumsum_rows(x, upper):
    return _exact(x, _tri(x.shape[0], upper), True)


cumsum_rows.defvjp(lambda x, upper: (cumsum_rows(x, upper), None),
                   lambda upper, r, g: (cumsum_rows(g, not upper),))


def _seg(n, w):
    shift = w.bit_length() - 1
    r, c = _iota((n, n), 0), _iota((n, n), 1)
    return jnp.where(lax.shift_right_logical(r, shift) == lax.shift_right_logical(c, shift), 1.0, 0.0).astype(BF)


@functools.partial(jax.custom_vjp, nondiff_argnums=(1,))
def seg_sum(x, w):
    return _exact(x, _seg(x.shape[1], w), False)


seg_sum.defvjp(lambda x, w: (seg_sum(x, w), None), lambda w, r, g: (seg_sum(g, w),))


def _place_mat(transpose):
    shape = (8 * LANES, LANES) if transpose else (LANES, 8 * LANES)
    r, c = _iota(shape, 0), _iota(shape, 1)
    src, dst = (c, r) if transpose else (r, c)
    dl = jnp.bitwise_and(dst, LANES - 1)
    ok = (dl >= 64) & (dl < 96) & (src == dl - 32)
    return jnp.where(ok, 1.0, 0.0).astype(BF)


@jax.custom_vjp
def place_kr(x):
    return _exact(x, _place_mat(False), False)


place_kr.defvjp(lambda x: (place_kr(x), None), lambda r, g: (_exact(g, _place_mat(True), False),))


@functools.partial(jax.custom_vjp, nondiff_argnums=(1,))
def lane_roll(x, s):
    return pltpu.roll(x, s, 1)


lane_roll.defvjp(lambda x, s: (pltpu.roll(x, s, 1), None),
                 lambda s, r, g: (pltpu.roll(g, (g.shape[1] - s) % g.shape[1], 1),))


def rope(x, tab, d):
    cos, sa, sb = tab
    return x * cos + lane_roll(x, LANES - d) * sa + lane_roll(x, d) * sb


def silu(x):
    return x * jax.nn.sigmoid(x)


def log_sigmoid(z):
    return jnp.minimum(z, 0.0) - jnp.log(1.0 + jnp.exp(-jnp.abs(z)))


def layer_norm(x, g, b):
    mu = jnp.mean(x, axis=-1, keepdims=True)
    xc = x - mu
    var = jnp.mean(xc * xc, axis=-1, keepdims=True)
    return xc * lax.rsqrt(var + EPS) * g + b


def matmul(a, b, mode, out_dtype, tm, tn, tk, name, b_outer=False):
    ij = (lambda f: (lambda g0, g1, kk: f(g1, g0, kk))) if b_outer else (lambda f: f)
    if mode == "nn":
        (m, k), (k2, n) = a.shape, b.shape
        a_spec = pl.BlockSpec((tm, tk), ij(lambda i, j, kk: (i, kk)))
        b_spec = pl.BlockSpec((tk, tn), ij(lambda i, j, kk: (kk, j)))
        ca, cb = 1, 0
    elif mode == "nt":
        (m, k), (n, k2) = a.shape, b.shape
        a_spec = pl.BlockSpec((tm, tk), ij(lambda i, j, kk: (i, kk)))
        b_spec = pl.BlockSpec((tn, tk), ij(lambda i, j, kk: (j, kk)))
        ca, cb = 1, 1
    else:
        (k, m), (k2, n) = a.shape, b.shape
        a_spec = pl.BlockSpec((tk, tm), ij(lambda i, j, kk: (kk, i)))
        b_spec = pl.BlockSpec((tk, tn), ij(lambda i, j, kk: (kk, j)))
        ca, cb = 0, 0
    assert k == k2 and m % tm == 0 and n % tn == 0 and k % tk == 0, (name, a.shape, b.shape, tm, tn, tk)
    nk = k // tk
    grid = (n // tn, m // tm, nk) if b_outer else (m // tm, n // tn, nk)

    def body(a_ref, b_ref, o_ref, *acc):
        part = _dg(a_ref[...], b_ref[...], ca, cb)
        if nk == 1:
            o_ref[...] = part.astype(o_ref.dtype)
            return
        acc_ref, = acc
        kk = pl.program_id(2)

        @pl.when(kk == 0)
        def _():
            acc_ref[...] = part

        @pl.when(kk > 0)
        def _():
            acc_ref[...] += part

        @pl.when(kk == nk - 1)
        def _():
            o_ref[...] = acc_ref[...].astype(o_ref.dtype)

    return pl.pallas_call(
        body, name=name, grid=grid,
        in_specs=[a_spec, b_spec], out_specs=pl.BlockSpec((tm, tn), ij(lambda i, j, kk: (i, j))),
        out_shape=jax.ShapeDtypeStruct((m, n), out_dtype),
        scratch_shapes=[] if nk == 1 else [pltpu.VMEM((tm, tn), F32)],
    )(a, b)


def _stage_specs(rows, tps, consts, ws):
    specs, args = [], []
    for arr, width, cb in rows:
        specs.append(pl.BlockSpec((ROW_TILE, width), functools.partial(lambda i, cb: (i, cb), cb=cb)))
        args.append(arr)
    for arr in tps:
        specs.append(pl.BlockSpec((1, 1, arr.shape[2]), lambda i: (i, 0, 0)))
        args.append(arr)
    for arr, period in consts:
        specs.append(pl.BlockSpec((ROW_TILE, arr.shape[1]), functools.partial(lambda i, p: (i % p, 0), p=period)))
        args.append(arr)
    for arr in ws:
        specs.append(pl.BlockSpec(arr.shape, functools.partial(lambda i, nd: (0,) * nd, nd=arr.ndim)))
        args.append(arr)
    return specs, args


def _stage_load(refs, n_rows, n_tps, n_consts, n_ws):
    it = iter(refs)
    rows = [next(it)[...].astype(F32) for _ in range(n_rows)]
    tps = [next(it)[0].astype(F32) for _ in range(n_tps)]
    consts = [next(it)[...].astype(F32) for _ in range(n_consts)]
    ws = [next(it)[...].astype(F32) for _ in range(n_ws)]
    return rows, tps, consts, ws


def stage_fwd(fn, rows, tps, consts, ws, outs, name):
    n_tiles = rows[0][0].shape[0] // ROW_TILE
    specs, args = _stage_specs(rows, tps, consts, ws)
    counts = (len(rows), len(tps), len(consts), len(ws))

    def body(*refs):
        r, t, c, w = _stage_load(refs[:sum(counts)], *counts)
        res = fn(r, t, c, w)
        for o_ref, o in zip(refs[sum(counts):], res):
            o_ref[...] = o.astype(o_ref.dtype)

    res = pl.pallas_call(
        body, name=name, grid=(n_tiles,), in_specs=specs,
        out_specs=[pl.BlockSpec((ROW_TILE, wd), lambda i: (i, 0)) for wd, _ in outs],
        out_shape=[jax.ShapeDtypeStruct((n_tiles * ROW_TILE, wd), dt) for wd, dt in outs],
    )(*args)
    return list(res)


def stage_bwd(fn, rows, tps, consts, ws, cts, row_grads, name):
    n_tiles = rows[0][0].shape[0] // ROW_TILE
    specs, args = _stage_specs(rows, tps, consts, ws)
    counts = (len(rows), len(tps), len(consts), len(ws))
    n_in = sum(counts)
    for ct in cts:
        specs.append(pl.BlockSpec((ROW_TILE, ct.shape[1]), lambda i: (i, 0)))
        args.append(ct)
    want = [k for k, dt in enumerate(row_grads) if dt is not None]
    out_specs = [pl.BlockSpec((ROW_TILE, rows[k][1]), lambda i: (i, 0)) for k in want]
    out_shape = [jax.ShapeDtypeStruct((n_tiles * ROW_TILE, rows[k][1]), row_grads[k]) for k in want]
    out_specs += [pl.BlockSpec((1, 1, a.shape[2]), lambda i: (i, 0, 0)) for a in tps]
    out_shape += [jax.ShapeDtypeStruct((n_tiles, 1, a.shape[2]), F32) for a in tps]
    out_specs += [pl.BlockSpec(a.shape, functools.partial(lambda i, nd: (0,) * nd, nd=a.ndim)) for a in ws]
    out_shape += [jax.ShapeDtypeStruct(a.shape, F32) for a in ws]

    def body(*refs):
        r, t, c, w = _stage_load(refs[:n_in], *counts)
        g = [ref[...].astype(F32) for ref in refs[n_in:n_in + len(cts)]]
        _, vjp = jax.vjp(lambda r_, t_, w_: fn(r_, t_, c, w_), r, t, w)
        dr, dt, dw = vjp(g)
        o = iter(refs[n_in + len(cts):])
        for k in want:
            ref = next(o)
            ref[...] = dr[k].astype(ref.dtype)
        for v in dt:
            next(o)[0] = v
        first = pl.program_id(0) == 0
        for v in dw:
            ref = next(o)

            @pl.when(first)
            def _():
                ref[...] = v

            @pl.when(jnp.logical_not(first))
            def _():
                ref[...] += v

    res = pl.pallas_call(body, name=name, grid=(n_tiles,), in_specs=specs, out_specs=out_specs, out_shape=out_shape)(*args)
    res = list(res)
    drows = [None] * len(rows)
    for k in want:
        drows[k] = res.pop(0)
    dtps = [res.pop(0) for _ in tps]
    dws = [res.pop(0) for _ in ws]
    return drows, dtps, dws


def fn_modulate(rows, tps, consts, ws):
    (x,), (sc, sh) = rows, tps
    return [x * (1.0 + sc) + sh]


def fn_post(rows, tps, consts, ws):
    (x, a), (g,), (lng, lnb) = rows, tps, ws
    return [layer_norm(ALPHA * x + g * a, lng, lnb)]


def fn_mix(rows, tps, consts, ws):
    ogf, ogb, orf, orb, pg, pr, mo = rows
    gng, = ws
    og = ogf + ogb
    out_g = og * lax.rsqrt(seg_sum(og * og, 64) * (1.0 / 64) + EPS) * gng * silu(pg)
    o = orf + orb
    oc = o - seg_sum(o, 64) * (1.0 / 64)
    out_r = oc * lax.rsqrt(seg_sum(oc * oc, 64) * (1.0 / 64) + EPS) * silu(pr)
    return [jnp.concatenate([out_g, out_r, mo], axis=-1)]


def fn_mla_prep(rows, tps, consts, ws):
    pq, pkv, plk = rows
    gq, gkv, wuq, wuk, wuv = ws
    qtab, ktab = consts[0:3], consts[3:6]
    cq = pq * lax.rsqrt(jnp.mean(pq * pq, axis=-1, keepdims=True) + EPS) * gq
    qp = mm_nn(cq, wuq)
    q = jnp.concatenate([rope(qp[:, h * LANES:(h + 1) * LANES], qtab, 8) * MLA_SCALE for h in range(MLA_HEADS)], axis=-1)
    ckv = pkv * lax.rsqrt(jnp.mean(pkv * pkv, axis=-1, keepdims=True) + EPS) * gkv
    k = mm_nn(ckv, wuk) + place_kr(rope(plk, ktab, 8))
    v = mm_nn(ckv, wuv)
    return [q, k, v]


def fn_assemble(rows, tps, consts, ws):
    gq, gk, gv, gg, rq, rk, rv, rg, cq, ckv, lk1, lk2 = rows
    return [jnp.concatenate([gq, gk, gv, gg, rq, rk, rv, rg, cq, ckv, lk1 + lk2], axis=-1)]


def _head_masks():
    hm = (lax.shift_right_logical(_iota((4, 1, LANES), 2), 5) == _iota((4, 1, LANES), 0)).astype(F32)
    vm = (lax.shift_right_logical(_iota((4, 1, 256), 2), 6) == _iota((4, 1, 256), 0)).astype(F32)
    bd = (lax.shift_right_logical(_iota((256, LANES), 0), 6) == lax.shift_right_logical(_iota((256, LANES), 1), 5)).astype(F32)
    return hm, vm, bd


def chunk_step(s, q, k, v, la, upper):
    hm, vm, bd = _head_masks()
    t, u = _iota((4 * CHUNK, CHUNK), 0), _iota((4 * CHUNK, CHUNK), 1)
    t = jnp.bitwise_and(t, CHUNK - 1)
    causal = (u >= t) if upper else (t >= u)
    b = cumsum_rows(la, upper)
    bend = jnp.sum(la, axis=0, keepdims=True)
    half = 0.5 * bend
    qd = q * jnp.exp(b - half)
    kd = k * jnp.exp(half - b)
    qe = (qd[None] * hm).reshape(4 * CHUNK, LANES)
    att = jnp.where(causal, mm_nt(qe, kd), 0.0)
    o_intra = (mm_nn(att, v).reshape(4, CHUNK, 256) * vm).sum(0)
    o = mm_nt(q * jnp.exp(b), s) + o_intra
    s_new = (s * jnp.exp(bend) + mm_tn(v, k * jnp.exp(bend - b))) * bd
    return o, s_new


def scan_step(sg, sr, q, k, v, lrk, rq, rk, rv, gw, gb, rdec, tab, upper):
    la_g = log_sigmoid(mm_nn(lrk, gw) + gb) * (1.0 / GLA_TAU)
    og, sg2 = chunk_step(sg, q * GLA_DK ** -0.5, k, v, la_g, upper)
    la_r = jnp.broadcast_to(log_sigmoid(rdec), (CHUNK, LANES))
    orr, sr2 = chunk_step(sr, rope(rq, tab, 16), rope(rk * RET_DK ** -0.5, tab, 16), rv, la_r, upper)
    return og, orr, sg2, sr2


def _chunk_of(n, ncc, nch, reverse):
    if not reverse:
        return n
    return jnp.where(n < ncc, ncc - 1 - n, nch - 1 + ncc - n)


def _scan_in_specs(p3, tabs, gw, gb, rdec, cidx):
    nb = p3.shape[0]

    def blk(width, cb):
        return pl.BlockSpec((nb, CHUNK, width), lambda m: (0, cidx(m), cb))

    specs = [blk(128, C_GQ // 128), blk(128, C_GK // 128), blk(256, C_GV // 256), blk(128, C_LK // 128),
             blk(128, C_RQ // 128), blk(128, C_RK // 128), blk(256, C_RV // 256)]
    args = [p3] * 7
    for t in tabs:
        specs.append(pl.BlockSpec((CHUNK, LANES), lambda m: (cidx(m), 0)))
        args.append(t)
    for w in (gw, gb, rdec):
        specs.append(pl.BlockSpec(w.shape, lambda m: (0, 0)))
        args.append(w)
    return specs, args


def scan_fwd(p, tabs, gw, gb, rdec, nb, ncc, nch, reverse, name):
    cidx = lambda n: _chunk_of(n, ncc, nch, reverse)
    t = p.shape[0]
    specs, args = _scan_in_specs(p.reshape(nb, t // nb, p.shape[1]), tabs, gw, gb, rdec, cidx)

    def body(q, k, v, lrk, rq, rk, rv, tc, ta, tb, gw_r, gb_r, rd_r, og_r, or_r, sgo_r, sro_r, sg, sr):
        @pl.when(pl.program_id(0) == 0)
        def _():
            sg[...] = jnp.zeros_like(sg)
            sr[...] = jnp.zeros_like(sr)

        sgo_r[0] = sg[...]
        sro_r[0] = sr[...]
        ld = lambda r: r[...].astype(F32)
        tab, gw_, gb_, rd_ = (ld(tc), ld(ta), ld(tb)), ld(gw_r), ld(gb_r), ld(rd_r)
        for b in range(nb):
            lb = lambda r: r[b].astype(F32)
            og, orr, sg2, sr2 = scan_step(sg[b], sr[b], lb(q), lb(k), lb(v), lb(lrk), lb(rq), lb(rk), lb(rv),
                                          gw_, gb_, rd_, tab, reverse)
            og_r[b] = og
            or_r[b] = orr
            sg[b] = sg2
            sr[b] = sr2

    row_out = pl.BlockSpec((nb, CHUNK, 256), lambda n: (0, cidx(n), 0))
    st_out = pl.BlockSpec((1, nb, 256, LANES), lambda n: (n, 0, 0, 0))
    og, orr, sgs, srs = pl.pallas_call(
        body, name=name, grid=(nch,), in_specs=specs, out_specs=[row_out, row_out, st_out, st_out],
        out_shape=[jax.ShapeDtypeStruct((nb, t // nb, 256), F32)] * 2 + [jax.ShapeDtypeStruct((nch, nb, 256, LANES), F32)] * 2,
        scratch_shapes=[pltpu.VMEM((nb, 256, LANES), F32)] * 2,
    )(*args)
    return og.reshape(t, 256), orr.reshape(t, 256), sgs, srs


def scan_bwd(p, tabs, gw, gb, rdec, sg_in, sr_in, dog, dor, prev, nb, ncc, nch, reverse, name):
    step = lambda m: nch - 1 - m
    cidx = lambda m: _chunk_of(step(m), ncc, nch, reverse)
    t = p.shape[0]
    lt = t // nb
    specs, args = _scan_in_specs(p.reshape(nb, lt, p.shape[1]), tabs, gw, gb, rdec, cidx)
    st_spec = pl.BlockSpec((1, nb, 256, LANES), lambda m: (step(m), 0, 0, 0))
    specs += [st_spec, st_spec]
    args += [sg_in, sr_in]
    row = lambda width: pl.BlockSpec((nb, CHUNK, width), lambda m: (0, cidx(m), 0))
    specs += [row(256), row(256)]
    args += [dog.reshape(nb, lt, 256), dor.reshape(nb, lt, 256)]
    widths = (128, 128, 256, 128, 128, 128, 256)
    if prev is not None:
        specs += [row(wd) for wd in widths]
        args += [a.reshape(nb, lt, a.shape[1]) for a in prev]
    n_prev = 0 if prev is None else 7

    def body(*refs):
        (q, k, v, lrk, rq, rk, rv, tc, ta, tb, gw_r, gb_r, rd_r, sgi, sri, dog_r, dor_r), rest = refs[:17], refs[17:]
        prev_r, rest = rest[:n_prev], rest[n_prev:]
        outs, (dgw_r, dgb_r, drd_r, dsg, dsr) = rest[:7], rest[7:]
        first = pl.program_id(0) == 0

        @pl.when(first)
        def _():
            dsg[...] = jnp.zeros_like(dsg)
            dsr[...] = jnp.zeros_like(dsr)

        ld = lambda r: r[...].astype(F32)
        tab, gw_, gb_, rd_ = (ld(tc), ld(ta), ld(tb)), ld(gw_r), ld(gb_r), ld(rd_r)
        wsum = None
        for b in range(nb):
            lb = lambda r: r[b].astype(F32)
            prim = (sgi[0, b], sri[0, b], lb(q), lb(k), lb(v), lb(lrk), lb(rq), lb(rk), lb(rv), gw_, gb_, rd_)
            _, vjp = jax.vjp(lambda *a: scan_step(*a, tab, reverse), *prim)
            g = vjp((lb(dog_r), lb(dor_r), dsg[b], dsr[b]))
            dsg[b] = g[0]
            dsr[b] = g[1]
            for j in range(7):
                val = g[2 + j]
                if n_prev:
                    val = val + prev_r[j][b]
                outs[j][b] = val
            wsum = g[9:12] if wsum is None else tuple(a + c for a, c in zip(wsum, g[9:12]))
        for ref, val in zip((dgw_r, dgb_r, drd_r), wsum):
            @pl.when(first)
            def _():
                ref[...] = val

            @pl.when(jnp.logical_not(first))
            def _():
                ref[...] += val

    wspec = lambda w: pl.BlockSpec(w.shape, lambda m: (0, 0))
    res = pl.pallas_call(
        body, name=name, grid=(nch,), in_specs=specs,
        out_specs=[row(wd) for wd in widths] + [wspec(gw), wspec(gb), wspec(rdec)],
        out_shape=[jax.ShapeDtypeStruct((nb, lt, wd), F32) for wd in widths]
        + [jax.ShapeDtypeStruct(w.shape, F32) for w in (gw, gb, rdec)],
        scratch_shapes=[pltpu.VMEM((nb, 256, LANES), F32)] * 2,
    )(*args)
    return tuple(a.reshape(t, a.shape[2]) for a in res[:7]), res[7], res[8], res[9]


def _attn_tiles(lc, lt):
    nct = lc // ROW_TILE
    return nct, (lt - lc) // ROW_TILE


def _attn_loop(tile, lc, lt):
    nct, nlt = _attn_tiles(lc, lt)
    for i in range(nct):
        tile(i * ROW_TILE, lc)

    def lat(i, carry):
        tile(pl.multiple_of(lc + i * ROW_TILE, ROW_TILE), lt)
        return carry

    lax.fori_loop(0, nlt, lat, 0)


def mla_fwd(q, k, v, nb, lc, lt, name, gather=()):
    ng = len(gather)

    def body(q_ref, k_ref, v_ref, *rest):
        x_refs, (o_ref, lse_ref), out_refs, sems = rest[:ng], rest[ng:ng + 2], rest[ng + 2:2 * ng + 2], rest[2 * ng + 2:]
        if ng:
            start, finish = _gather8_steps(x_refs, out_refs, *sems)
            pl.when((pl.program_id(0) == 0) & (pl.program_id(1) == 0))(start)

        def tile(r0, nk):
            rows = pl.ds(r0, ROW_TILE)
            lane = _iota((ROW_TILE, LANES), 1)
            outs, lse = [], jnp.zeros((ROW_TILE, LANES), F32)
            for j in range(2):
                s = _dg(q_ref[rows, j * LANES:(j + 1) * LANES], k_ref[0:nk, j * LANES:(j + 1) * LANES], 1, 1)
                m = jnp.max(s, axis=-1, keepdims=True)
                p = jnp.exp(s - m)
                l = jnp.sum(p, axis=-1, keepdims=True)
                outs.append(_dg(p, v_ref[0:nk, j * 64:(j + 1) * 64], 1, 0) * (1.0 / l))
                lse = jnp.where(lane == j, m + jnp.log(l), lse)
            o_ref[rows, :] = jnp.concatenate(outs, axis=-1)
            lse_ref[rows, :] = lse

        _attn_loop(tile, lc, lt)
        if ng:
            pl.when((pl.program_id(0) == nb - 1) & (pl.program_id(1) == MLA_HEADS // 2 - 1))(finish)

    pair = lambda width: pl.BlockSpec((lt, width), lambda b, h: (b, h))
    res = pl.pallas_call(
        body, name=name, grid=(nb, MLA_HEADS // 2), in_specs=[pair(2 * LANES), pair(2 * LANES), pair(LANES)] + [ANY] * ng,
        out_specs=[pair(LANES), pair(LANES)] + [ANY] * ng,
        out_shape=[jax.ShapeDtypeStruct((nb * lt, MLA_HEADS * 64), F32), jax.ShapeDtypeStruct((nb * lt, MLA_HEADS // 2 * LANES), F32)]
        + [jax.ShapeDtypeStruct((8,) + g.shape, g.dtype) for g in gather],
        scratch_shapes=[_dma_sems(ng, 7), _dma_sems(ng, 7), _dma_sems(ng, 1)] if ng else [],
    )(q, k, v, *gather)
    return res[0], res[1], list(res[2:])


def mla_bwd(q, k, v, o, lse, do, nb, lc, lt, name):
    def body(q_ref, k_ref, v_ref, o_ref, lse_ref, do_ref, dq_ref, dk_ref, dv_ref, dka, dva):
        dka[...] = jnp.zeros_like(dka)
        dva[...] = jnp.zeros_like(dva)

        def tile(r0, nk):
            rows = pl.ds(r0, ROW_TILE)
            dqs = []
            for j in range(2):
                qj, kj = q_ref[rows, j * LANES:(j + 1) * LANES], k_ref[0:nk, j * LANES:(j + 1) * LANES]
                vj, doj = v_ref[0:nk, j * 64:(j + 1) * 64], do_ref[rows, j * 64:(j + 1) * 64]
                p = jnp.exp(_dg(qj, kj, 1, 1) - lse_ref[rows, j:j + 1])
                dsum = jnp.sum(doj * o_ref[rows, j * 64:(j + 1) * 64], axis=-1, keepdims=True)
                ds = p * (_dg(doj, vj, 1, 1) - dsum)
                dqs.append(_dg(ds, kj, 1, 0))
                dka[j, 0:nk, :] += _dg(ds, qj, 0, 0)
                dva[j, 0:nk, :] += _dg(p, doj, 0, 0)
            dq_ref[rows, :] = jnp.concatenate(dqs, axis=-1)

        _attn_loop(tile, lc, lt)
        dk_ref[...] = jnp.concatenate([dka[0], dka[1]], axis=-1)
        dv_ref[...] = jnp.concatenate([dva[0], dva[1]], axis=-1)

    t = nb * lt
    pair = lambda width: pl.BlockSpec((lt, width), lambda b, h: (b, h))
    return pl.pallas_call(
        body, name=name, grid=(nb, MLA_HEADS // 2),
        in_specs=[pair(2 * LANES), pair(2 * LANES), pair(LANES), pair(LANES), pair(LANES), pair(LANES)],
        out_specs=[pair(2 * LANES), pair(2 * LANES), pair(LANES)],
        out_shape=[jax.ShapeDtypeStruct((t, MLA_HEADS * LANES), F32), jax.ShapeDtypeStruct((t, MLA_HEADS * LANES), F32),
                   jax.ShapeDtypeStruct((t, MLA_HEADS * 64), F32)],
        scratch_shapes=[pltpu.VMEM((2, lt, LANES), F32), pltpu.VMEM((2, lt, 64), F32)],
    )(q, k, v, o, lse, do)


HALO = 16


def _gate_specs(u, per_batch, lc):
    gh = GATE_ROWS // HALO
    nh = u.shape[0] // HALO
    width = u.shape[1]
    main = pl.BlockSpec((GATE_ROWS, width), lambda i: (i, 0))
    prev = pl.BlockSpec((HALO, width), lambda i: (jnp.maximum(i * gh - 1, 0), 0))
    nxt = pl.BlockSpec((HALO, width), lambda i: (jnp.minimum((i + 1) * gh, nh - 1), 0))
    return main, prev, nxt


def _seg_edges(per_batch, lc):
    j = pl.program_id(0) % (per_batch // GATE_ROWS)
    first = (j == 0) | (j == lc // GATE_ROWS)
    last = (j == lc // GATE_ROWS - 1) | (j == per_batch // GATE_ROWS - 1)
    return first, last


def _shifted(x, prev_ref, next_ref, first, last):
    rows = _iota(x.shape, 0)
    before = jnp.where(first, 0.0, prev_ref[HALO - 1:HALO, :].astype(F32))
    after = jnp.where(last, 0.0, next_ref[0:1, :].astype(F32))
    xm = jnp.where(rows == 0, before, pltpu.roll(x, 1, 0))
    xp = jnp.where(rows == x.shape[0] - 1, after, pltpu.roll(x, x.shape[0] - 1, 0))
    return xm, xp


def gate_fwd(u, cw, cb, per_batch, lc, name):
    main, prev, nxt = _gate_specs(u, per_batch, lc)
    f = u.shape[1] // 2

    def body(u_ref, p_ref, n_ref, w_ref, b_ref, act_ref):
        first, last = _seg_edges(per_batch, lc)
        x = u_ref[...].astype(F32)
        xm, xp = _shifted(x, p_ref, n_ref, first, last)
        c = w_ref[0:1, :] * xm + w_ref[1:2, :] * x + w_ref[2:3, :] * xp + b_ref[...]
        act_ref[...] = (silu(c[:, :f]) * c[:, f:]).astype(act_ref.dtype)

    return pl.pallas_call(
        body, name=name, grid=(u.shape[0] // GATE_ROWS,),
        in_specs=[main, prev, nxt, pl.BlockSpec(cw.shape, lambda i: (0, 0)), pl.BlockSpec(cb.shape, lambda i: (0, 0))],
        out_specs=pl.BlockSpec((GATE_ROWS, f), lambda i: (i, 0)),
        out_shape=jax.ShapeDtypeStruct((u.shape[0], f), BF),
    )(u, u, u, cw, cb)


def gate_bwd(u, cw, cb, dact, per_batch, lc, name):
    main, prev, nxt = _gate_specs(u, per_batch, lc)
    f = u.shape[1] // 2

    def body(u_ref, p_ref, n_ref, w_ref, b_ref, da_ref, dc_ref, dw_ref):
        first, last = _seg_edges(per_batch, lc)
        x = u_ref[...].astype(F32)
        xm, xp = _shifted(x, p_ref, n_ref, first, last)
        c = w_ref[0:1, :] * xm + w_ref[1:2, :] * x + w_ref[2:3, :] * xp + b_ref[...]
        a, g = c[:, :f], c[:, f:]
        sg = jax.nn.sigmoid(a)
        da = da_ref[...]
        dc = jnp.concatenate([da * g * (sg * (1.0 + a * (1.0 - sg))), da * (a * sg)], axis=-1)
        dc_ref[...] = dc.astype(dc_ref.dtype)
        part = jnp.concatenate([jnp.sum(xm * dc, axis=0, keepdims=True), jnp.sum(x * dc, axis=0, keepdims=True),
                                jnp.sum(xp * dc, axis=0, keepdims=True), jnp.sum(dc, axis=0, keepdims=True),
                                jnp.zeros((4, 2 * f), F32)], axis=0)

        @pl.when(pl.program_id(0) == 0)
        def _():
            dw_ref[...] = part

        @pl.when(pl.program_id(0) > 0)
        def _():
            dw_ref[...] += part

    return pl.pallas_call(
        body, name=name, grid=(u.shape[0] // GATE_ROWS,),
        in_specs=[main, prev, nxt, pl.BlockSpec(cw.shape, lambda i: (0, 0)), pl.BlockSpec(cb.shape, lambda i: (0, 0)),
                  pl.BlockSpec((GATE_ROWS, f), lambda i: (i, 0))],
        out_specs=[main, pl.BlockSpec((8, 2 * f), lambda i: (0, 0))],
        out_shape=[jax.ShapeDtypeStruct(u.shape, BF), jax.ShapeDtypeStruct((8, 2 * f), F32)],
    )(u, u, u, cw, cb, dact)


def conv_transpose(dc, cw, per_batch, lc, name):
    main, prev, nxt = _gate_specs(dc, per_batch, lc)

    def body(d_ref, p_ref, n_ref, w_ref, du_ref):
        first, last = _seg_edges(per_batch, lc)
        x = d_ref[...].astype(F32)
        xm, xp = _shifted(x, p_ref, n_ref, first, last)
        du_ref[...] = (w_ref[0:1, :] * xp + w_ref[1:2, :] * x + w_ref[2:3, :] * xm).astype(du_ref.dtype)

    return pl.pallas_call(
        body, name=name, grid=(dc.shape[0] // GATE_ROWS,),
        in_specs=[main, prev, nxt, pl.BlockSpec(cw.shape, lambda i: (0, 0))],
        out_specs=main, out_shape=jax.ShapeDtypeStruct(dc.shape, BF),
    )(dc, dc, dc, cw)


def loss_head(x, target, tiles_per_batch, ctx_tiles, name):
    n_tiles = x.shape[0] // ROW_TILE
    lat_tiles = tiles_per_batch - ctx_tiles

    def tgt_idx(i):
        j = i % tiles_per_batch
        return jnp.where(j < ctx_tiles, 0, (i // tiles_per_batch) * lat_tiles + j - ctx_tiles), 0

    def body(x_ref, t_ref, dx_ref, l_ref):
        lat = (pl.program_id(0) % tiles_per_batch >= ctx_tiles).astype(F32)
        err = (x_ref[...] - t_ref[...]) * lat
        dx_ref[...] = err * (1.0 / D_MODEL)
        l_ref[...] = jnp.full(l_ref.shape, 0.5 / D_MODEL * jnp.sum(err * err), F32)

    return pl.pallas_call(
        body, name=name, grid=(n_tiles,),
        in_specs=[pl.BlockSpec((ROW_TILE, D_MODEL), lambda i: (i, 0)), pl.BlockSpec((ROW_TILE, D_MODEL), tgt_idx)],
        out_specs=[pl.BlockSpec((ROW_TILE, D_MODEL), lambda i: (i, 0)), pl.BlockSpec((1, 8, LANES), lambda i: (i, 0, 0))],
        out_shape=[jax.ShapeDtypeStruct(x.shape, F32), jax.ShapeDtypeStruct((n_tiles, 8, LANES), F32)],
    )(x, target)


def adamw(w, g, m, v, name):
    rows, cols = w.shape
    tr = rows
    for cand in (512, 256, 128, 64, 32, 16, 8):
        if rows % cand == 0 and cand * cols * 4 <= (1 << 20):
            tr = cand
            break

    def body(w_ref, g_ref, m_ref, v_ref, d_ref, mo_ref, vo_ref):
        gg = g_ref[...]
        m2 = ADAM_B1 * m_ref[...] + (1.0 - ADAM_B1) * gg
        v2 = ADAM_B2 * v_ref[...] + (1.0 - ADAM_B2) * (gg * gg)
        m_hat = m2 / (1.0 - ADAM_B1 ** ADAM_STEP)
        v_hat = v2 / (1.0 - ADAM_B2 ** ADAM_STEP)
        d_ref[...] = -ADAM_LR * (m_hat / (jnp.sqrt(v_hat) + ADAM_EPS) + ADAM_WD * w_ref[...])
        mo_ref[...] = m2
        vo_ref[...] = v2

    spec = pl.BlockSpec((tr, cols), lambda i: (i, 0))
    return pl.pallas_call(body, name=name, grid=(rows // tr,), in_specs=[spec] * 4, out_specs=[spec] * 3,
                          out_shape=[jax.ShapeDtypeStruct(w.shape, F32)] * 3)(w, g, m, v)


def fn_post_mod(rows, tps, consts, ws):
    (x, a), (g, sc, sh), (lng, lnb) = rows, tps, ws
    y = layer_norm(ALPHA * x + g * a, lng, lnb)
    return [y, y * (1.0 + sc) + sh]


def _pick(n, cands):
    for c in cands:
        if n % c == 0:
            return c
    return n


def rope_tables(lc, l):
    pos = jnp.arange(l, dtype=F32)
    ret_inv = 1.0 / (ROPE_BASE ** jnp.linspace(0.0, 1.0, RET_DK // 2, dtype=F32))
    ang = pos[:, None] * ret_inv
    rc, rs = jnp.cos(ang), jnp.sin(ang)
    n_ax = MLA_D_ROPE // 4
    ax_inv = ROPE_BASE ** (-jnp.arange(n_ax, dtype=F32) / n_ax)
    rows_n = l // GRID_W
    rows = jnp.repeat(jnp.arange(rows_n, dtype=F32), GRID_W)
    cols = jnp.tile(jnp.arange(GRID_W, dtype=F32), rows_n)
    ra, ca = rows[:, None] * ax_inv, cols[:, None] * ax_inv
    rwc, rws, clc, cls = jnp.cos(ra), jnp.sin(ra), jnp.cos(ca), jnp.sin(ca)
    one = lambda n: jnp.ones((l, n), F32)
    zero = lambda n: jnp.zeros((l, n), F32)
    cat = lambda parts: jnp.concatenate(parts, axis=1)

    def with_ctx(tab, is_cos):
        head = jnp.ones((lc, LANES), F32) if is_cos else jnp.zeros((lc, LANES), F32)
        return jnp.concatenate([head, tab], axis=0)

    ret = (cat([rc, rc] * 4), cat([-rs, zero(16)] * 4), cat([zero(16), rs] * 4))
    ax_c = [rwc, rwc, clc, clc]
    ax_a = [-rws, zero(8), -cls, zero(8)]
    ax_b = [zero(8), rws, zero(8), cls]
    qt = (cat([one(64)] + ax_c + [one(32)]), cat([zero(64)] + ax_a + [zero(32)]), cat([zero(64)] + ax_b + [zero(32)]))
    kt = (cat([one(32)] + ax_c + [one(64)]), cat([zero(32)] + ax_a + [zero(64)]), cat([zero(32)] + ax_b + [zero(64)]))
    fix = lambda t3: tuple(with_ctx(t, k == 0) for k, t in enumerate(t3))
    return fix(ret), fix(qt), fix(kt)


_IN_ORDER = ((0, 128), (128, 256), (256, 512), (544, 800), (800, 928), (928, 1056), (1056, 1312), (1312, 1568),
             (1568, 1824), (1824, 1952), (512, 544), (1952, 1984))


def permute_w_in(w):
    parts = [w[:, a:b] for a, b in _IN_ORDER] + [jnp.zeros((w.shape[0], D_INP - D_IN), w.dtype)]
    return jnp.concatenate(parts, axis=1)


def unpermute_w_in(g):
    out, at = {}, 0
    for a, b in _IN_ORDER:
        out[a] = g[:, at:at + b - a]
        at += b - a
    return jnp.concatenate([out[a] for a in sorted(out)], axis=1)


def layer_weights(big, w, l):
    f = lambda a: a.astype(F32)
    r = {}
    r["ada_w"] = big["ada_w"].astype(BF)
    r["win"] = permute_w_in(big["w_in"]).astype(BF)
    r["wout"] = big["w_out"].astype(BF)
    r["wup"] = big["ffn_up"].astype(BF)
    r["wdown"] = big["ffn_down"].astype(BF)
    uq = big["mla_w_uq"].reshape(256, MLA_HEADS, 96)
    r["wuq"] = jnp.pad(uq, ((0, 0), (0, 0), (0, 32))).reshape(256, 8 * LANES).astype(BF)
    uk = big["mla_w_uk"].reshape(128, MLA_HEADS, 64)
    r["wuk"] = jnp.pad(uk, ((0, 0), (0, 0), (0, 64))).reshape(128, 8 * LANES).astype(BF)
    r["wuv"] = big["mla_w_uv"].astype(BF)
    gw = f(w["gla_gate_w"][l])
    z16 = jnp.zeros((16, LANES), F32)
    z96 = jnp.zeros((96, LANES), F32)
    r["gw"] = (jnp.concatenate([gw[0], z16, z96], axis=0), jnp.concatenate([z16, gw[1], z96], axis=0))
    r["gb"] = tuple(f(w["gla_gate_b"][l][d]).reshape(1, LANES) for d in range(2))
    r["rdec"] = tuple(jnp.repeat(f(w["ret_decay"][l][d]), 32).reshape(1, LANES) for d in range(2))
    r["gng"] = jnp.tile(f(w["gla_norm_g"][l]), 4).reshape(1, 256)
    r["gq"] = f(w["mla_q_norm_g"][l]).reshape(1, 256)
    r["gkv"] = f(w["mla_kv_norm_g"][l]).reshape(1, 128)
    for n in ("ln1_g", "ln1_b", "ln2_g", "ln2_b"):
        r[n] = f(w[n][l]).reshape(1, D_MODEL)
    r["cw"] = f(w["ffn_conv_w"][l])
    r["cb"] = f(w["ffn_conv_b"][l]).reshape(1, 2 * D_FF)
    return r


def tile_params(mod_l, nb, nct, nlt):
    m6 = mod_l.reshape(8, 6, D_MODEL)
    out = []
    for j in range(6):
        parts = []
        for b in range(nb):
            parts.append(jnp.broadcast_to(m6[4, j], (nct, 1, D_MODEL)))
            parts.append(jnp.broadcast_to(m6[b, j], (nlt, 1, D_MODEL)))
        out.append(jnp.concatenate(parts, axis=0))
    return out


def tile_param_grads(dts, nb, nct, nlt):
    cols = []
    for dt in dts:
        d = dt.reshape(nb, nct + nlt, D_MODEL)
        lat = jnp.sum(d[:, nct:], axis=1)
        ctx = jnp.sum(d[:, :nct], axis=(0, 1))
        cols.append(jnp.concatenate([lat, jnp.zeros((4 - nb, D_MODEL), F32), ctx[None], jnp.zeros((3, D_MODEL), F32)], axis=0))
    return jnp.stack(cols, axis=1).reshape(8, 6 * D_MODEL)


def layer_forward(x, h1, tp, lw, tabs, dims, nxt):
    nb, lc, lt = dims
    t = x.shape[0]
    nbt = lt // ROW_TILE
    ncc, nch = lc // CHUNK, lt // CHUNK
    tm = _pick(t, (1024, 768, 512, 256))
    ret_tab, q_tab, k_tab = tabs
    full = lambda a: (a, a.shape[1], 0)
    p = matmul(h1, lw["win"], "nn", F32, tm, D_INP, 1024, "proj_in")
    ogf, orf, sgf, srf = scan_fwd(p, ret_tab, lw["gw"][0], lw["gb"][0], lw["rdec"][0], nb, ncc, nch, False, "scan_fwd_f")
    ogb, orb, sgb, srb = scan_fwd(p, ret_tab, lw["gw"][1], lw["gb"][1], lw["rdec"][1], nb, ncc, nch, True, "scan_fwd_b")
    prep_rows = [(p, 256, C_CQ // 256), (p, 128, C_CKV // 128), (p, 128, C_LK // 128)]
    prep_consts = [(a, nbt) for a in q_tab + k_tab]
    prep_ws = [lw["gq"], lw["gkv"], lw["wuq"], lw["wuk"], lw["wuv"]]
    q, k, v = stage_fwd(fn_mla_prep, prep_rows, [], prep_consts, prep_ws, [(1024, BF), (1024, BF), (512, BF)], "mla_prep")
    if nxt is None:
        mo, lse, _ = mla_fwd(q, k, v, nb, lc, lt, "mla_attn_last")
        made, tp_next = None, None
    else:
        mo, lse, got = mla_fwd(q, k, v, nb, lc, lt, "mla_attn", gather=nxt[0])
        made = nxt[1](got)
        tp_next = (made[0][1], made[0][0])
    mix_rows = [full(ogf), full(ogb), full(orf), full(orb), (p, 256, C_GG // 256), (p, 256, C_RG // 256), full(mo)]
    m, = stage_fwd(fn_mix, mix_rows, [], [], [lw["gng"]], [(1024, BF)], "mix")
    a = matmul(m, lw["wout"], "nn", F32, tm, 1024, 1024, "proj_out")
    x1, h2 = stage_fwd(fn_post_mod, [full(x), full(a)], [tp[2], tp[4], tp[3]], [], [lw["ln1_g"], lw["ln1_b"]],
                       [(1024, F32), (1024, BF)], "post1")
    u = matmul(h2, lw["wup"], "nn", BF, tm, 1408, 1024, "ffn_up", b_outer=True)
    act = gate_fwd(u, lw["cw"], lw["cb"], lt, lc, "ffn_gate")
    f = matmul(act, lw["wdown"], "nn", F32, tm, 1024, D_FF, "ffn_down")
    if tp_next is None:
        x2, = stage_fwd(fn_post, [full(x1), full(f)], [tp[5]], [], [lw["ln2_g"], lw["ln2_b"]], [(1024, F32)], "post2_last")
        h1n = None
    else:
        x2, h1n = stage_fwd(fn_post_mod, [full(x1), full(f)], [tp[5], tp_next[0], tp_next[1]], [],
                            [lw["ln2_g"], lw["ln2_b"]], [(1024, F32), (1024, BF)], "post2")
    res = dict(x=x, h1=h1, p=p, ogf=ogf, orf=orf, sgf=sgf, srf=srf, ogb=ogb, orb=orb, sgb=sgb, srb=srb, q=q, k=k, v=v,
               mo=mo, lse=lse, m=m, a=a, x1=x1, h2=h2, u=u, act=act, f=f, mix_rows=mix_rows, prep_rows=prep_rows,
               prep_consts=prep_consts, prep_ws=prep_ws)
    return x2, h1n, res, made


def layer_backward(dx2, dh1n, res, tp, tp_next, lw, tabs, dims):
    nb, lc, lt = dims
    r = res
    t = dx2.shape[0]
    ncc, nch = lc // CHUNK, lt // CHUNK
    tm = _pick(t, (1024, 768, 512, 256))
    tkr = _pick(t, (2304, 1536, 1024, 768, 512))
    ret_tab = tabs[0]
    full = lambda a: (a, a.shape[1], 0)
    g = {}
    if tp_next is None:
        (dx1a, df), (dg2,), (g["ln2_g"], g["ln2_b"]) = stage_bwd(
            fn_post, [full(r["x1"]), full(r["f"])], [tp[5]], [], [lw["ln2_g"], lw["ln2_b"]], [dx2], [F32, BF], "post2_last_bwd")
        dnext = None
    else:
        (dx1a, df), (dg2, dsc1n, dsh1n), (g["ln2_g"], g["ln2_b"]) = stage_bwd(
            fn_post_mod, [full(r["x1"]), full(r["f"])], [tp[5], tp_next[0], tp_next[1]], [], [lw["ln2_g"], lw["ln2_b"]],
            [dx2, dh1n], [F32, BF], "post2_bwd")
        dnext = (dsc1n, dsh1n)
    dact = matmul(df, lw["wdown"], "nt", F32, tm, 1408, 1024, "ffn_down_dx", b_outer=True)
    g["ffn_down"] = matmul(r["act"], df, "tn", F32, 1408, 1024, tkr, "ffn_down_dw")
    dc, dcw = gate_bwd(r["u"], lw["cw"], lw["cb"], dact, lt, lc, "ffn_gate_bwd")
    g["ffn_conv_w"], g["ffn_conv_b"] = dcw[0:3], dcw[3]
    du = conv_transpose(dc, lw["cw"], lt, lc, "ffn_conv_t")
    dh2 = matmul(du, lw["wup"], "nt", F32, _pick(t, (512, 256)), 1024, 2 * D_FF, "ffn_up_dx")
    g["ffn_up"] = matmul(r["h2"], du, "tn", F32, 1024, 1408, tkr, "ffn_up_dw")
    (dxa, da), (dg1, dsc2, dsh2), (g["ln1_g"], g["ln1_b"]) = stage_bwd(
        fn_post_mod, [full(r["x"]), full(r["a"])], [tp[2], tp[4], tp[3]], [], [lw["ln1_g"], lw["ln1_b"]],
        [dx1a, dh2], [F32, BF], "post1_bwd")
    dm = matmul(da, lw["wout"], "nt", F32, tm, 1024, 1024, "proj_out_dx")
    g["w_out"] = matmul(r["m"], da, "tn", F32, 1024, 1024, tkr, "proj_out_dw")
    (dog, _, dor, _, dpg, dpr, dmo), _, (dgng,) = stage_bwd(
        fn_mix, r["mix_rows"], [], [], [lw["gng"]], [dm], [F32, None, F32, None, F32, F32, F32], "mix_bwd")
    g["gla_norm_g"] = jnp.sum(dgng.reshape(4, 64), axis=0)
    dq, dk, dv = mla_bwd(r["q"], r["k"], r["v"], r["mo"], r["lse"], dmo, nb, lc, lt, "mla_attn_bwd")
    (dpq, dpkv, dplk), _, (dgq, dgkv, dwuq, dwuk, dwuv) = stage_bwd(
        fn_mla_prep, r["prep_rows"], [], r["prep_consts"], r["prep_ws"], [dq, dk, dv], [F32, F32, F32], "mla_prep_bwd")
    g["mla_q_norm_g"], g["mla_kv_norm_g"] = dgq.reshape(256), dgkv.reshape(128)
    g["mla_w_uq"] = dwuq.reshape(256, MLA_HEADS, LANES)[:, :, :96].reshape(256, MLA_HEADS * 96)
    g["mla_w_uk"] = dwuk.reshape(128, MLA_HEADS, LANES)[:, :, :64].reshape(128, MLA_HEADS * 64)
    g["mla_w_uv"] = dwuv
    s7, dgw0, dgb0, drd0 = scan_bwd(r["p"], ret_tab, lw["gw"][0], lw["gb"][0], lw["rdec"][0], r["sgf"], r["srf"], dog, dor,
                                    None, nb, ncc, nch, False, "scan_bwd_f")
    s7, dgw1, dgb1, drd1 = scan_bwd(r["p"], ret_tab, lw["gw"][1], lw["gb"][1], lw["rdec"][1], r["sgb"], r["srb"], dog, dor,
                                    s7, nb, ncc, nch, True, "scan_bwd_b")
    g["gla_gate_w"] = jnp.stack([dgw0[0:16], dgw1[16:32]])
    g["gla_gate_b"] = jnp.stack([dgb0[0], dgb1[0]])
    g["ret_decay"] = jnp.stack([jnp.sum(drd0.reshape(4, 32), axis=1), jnp.sum(drd1.reshape(4, 32), axis=1)])
    gq_, gk_, gv_, glrk, rq_, rk_, rv_ = s7
    pieces = [gq_, gk_, gv_, dpg, rq_, rk_, rv_, dpr, dpq, dpkv, glrk, dplk]
    dp, = stage_fwd(fn_assemble, [full(a) for a in pieces], [], [], [], [(D_INP, BF)], "dproj_assemble")
    dh1 = matmul(dp, lw["win"], "nt", F32, tm, 1024, D_INP, "proj_in_dx")
    g["w_in"] = unpermute_w_in(matmul(r["h1"], dp, "tn", F32, 1024, 1024, tkr, "proj_in_dw"))
    for n in ("ln1_g", "ln1_b", "ln2_g", "ln2_b"):
        g[n] = g[n].reshape(D_MODEL)
    dtp = [None, None, dg1, dsh2, dsc2, dg2]
    return dxa, dh1, dtp, dnext, g


def local_step(x, c, ctx, c_ctx, w, loss_target, first_big, next_blocks, assemble):
    nb, l, _ = x.shape
    lc = ctx.shape[1]
    lt = lc + l
    dims = (nb, lc, lt)
    nct, nlt = lc // ROW_TILE, l // ROW_TILE
    tabs = rope_tables(lc, l)
    x0 = jnp.concatenate([ctx, x], axis=1).reshape(nb * lt, D_MODEL)
    s8 = jnp.concatenate([silu(c), jnp.zeros((4 - nb, D_MODEL), F32), silu(c_ctx)[None], jnp.zeros((3, D_MODEL), F32)], axis=0)

    def make_layer(i, big):
        lw = layer_weights(big, w, i)
        mod = matmul(s8, lw["ada_w"], "nn", F32, 8, 1536, 1024, "ada_mod") + w["ada_b"][i].astype(F32)[None]
        return tile_params(mod, nb, nct, nlt), lw

    made = make_layer(0, first_big)
    lws, tps = [], []
    h1, = stage_fwd(fn_modulate, [(x0, D_MODEL, 0)], [made[0][1], made[0][0]], [], [], [(D_MODEL, BF)], "mod_in")
    xs, ress = x0, []
    for i in range(DEPTH):
        tps.append(made[0])
        lws.append(made[1])
        nxt = None
        if i < DEPTH - 1:
            nxt = (next_blocks(i + 1), functools.partial(lambda got, j: make_layer(j, assemble(j, got)), j=i + 1))
        xs, h1, res, made = layer_forward(xs, h1, tps[i], lws[i], tabs, dims, nxt)
        ress.append(res)
    dx, lparts = loss_head(xs, loss_target.reshape(nb * l, D_MODEL), nct + nlt, nct, "loss_head")
    loss = jnp.sum(lparts[:, 0, 0])
    grads = [None] * DEPTH
    dtps = [None] * DEPTH
    dh1 = None
    for i in reversed(range(DEPTH)):
        tpn = None if i == DEPTH - 1 else (tps[i + 1][1], tps[i + 1][0])
        dx, dh1, dtp, dn, grads[i] = layer_backward(dx, dh1, ress[i], tps[i], tpn, lws[i], tabs, dims)
        if dn is not None:
            dtps[i + 1][1], dtps[i + 1][0] = dn
        dtps[i] = dtp
    (dx0b,), (dsc1, dsh1), _ = stage_bwd(fn_modulate, [(x0, D_MODEL, 0)], [tps[0][1], tps[0][0]], [], [], [dh1], [F32], "mod_in_bwd")
    dtps[0][1], dtps[0][0] = dsc1, dsh1
    grad_x = (dx + dx0b).reshape(nb, lt, D_MODEL)[:, lc:]
    dmod = jnp.stack([tile_param_grads(d, nb, nct, nlt) for d in dtps])
    gw = {n: jnp.stack([grads[i][n] for i in range(DEPTH)]) for n in grads[0]}
    return loss, grad_x, gw, dmod, s8, lws


MESH_IDS = pl.DeviceIdType.MESH
ANY = pl.BlockSpec(memory_space=pl.ANY)


def _place():
    return lax.axis_index("x"), lax.axis_index("y"), lax.axis_index("c")


def _dma_sems(n, per):
    return pltpu.SemaphoreType.DMA((n, per))


def _gather8_steps(x_refs, out_refs, send_sems, recv_sems, local_sems):
    n = len(x_refs)
    x, y, c = _place()
    me, sibling = (x, y, c), (x, y, 1 - c)
    chips = [(1 - x, y), (x, 1 - y), (1 - x, 1 - y)]

    def copy(a, k, blk, to, own=False):
        slot = out_refs[a].at[4 * blk[0] + 2 * blk[1] + blk[2]]
        return pltpu.make_async_remote_copy(
            src_ref=x_refs[a] if own else slot, dst_ref=slot,
            send_sem=send_sems.at[a, k], recv_sem=recv_sems.at[a, k], device_id=to, device_id_type=MESH_IDS)

    def local(a):
        return pltpu.make_async_copy(x_refs[a], out_refs[a].at[4 * x + 2 * y + c], local_sems.at[a, 0])

    def first_copies():
        cps = []
        for a in range(n):
            cps.append(copy(a, 0, me, sibling, own=True))
            cps += [copy(a, 1 + j, me, (*chip, c), own=True) for j, chip in enumerate(chips)]
        return cps

    def start():
        for a in range(n):
            local(a).start()
        for cp in first_copies():
            cp.start()

    def finish():
        passed = []
        for j, chip in enumerate(chips):
            for a in range(n):
                copy(a, 1 + j, (*chip, c), me).wait_recv()
                passed.append(copy(a, 4 + j, (*chip, c), sibling))
                passed[-1].start()
        for a in range(n):
            copy(a, 0, sibling, me).wait_recv()
            for j, chip in enumerate(chips):
                copy(a, 4 + j, (*chip, 1 - c), me).wait_recv()
        for cp in first_copies() + passed:
            cp.wait_send()
        for a in range(n):
            local(a).wait()

    return start, finish


def all_gather8(blocks, name):
    n = len(blocks)

    def body(*refs):
        start, finish = _gather8_steps(refs[:n], refs[n:2 * n], *refs[2 * n:])
        start()
        finish()

    return pl.pallas_call(
        body, name=name, out_shape=[jax.ShapeDtypeStruct((8,) + b.shape, b.dtype) for b in blocks],
        in_specs=[ANY] * n, out_specs=[ANY] * n,
        scratch_shapes=[_dma_sems(n, 7), _dma_sems(n, 7), _dma_sems(n, 1)],
    )(*blocks)


def swap_cores(blocks, name):
    n = len(blocks)

    def body(*refs):
        x_refs, out_refs, (send_sems, recv_sems) = refs[:n], refs[n:2 * n], refs[2 * n:]
        x, y, c = _place()
        cps = [pltpu.make_async_remote_copy(src_ref=x_refs[a], dst_ref=out_refs[a], send_sem=send_sems.at[a, 0],
                                            recv_sem=recv_sems.at[a, 0], device_id=(x, y, 1 - c), device_id_type=MESH_IDS)
               for a in range(n)]
        for cp in cps:
            cp.start()
        for cp in cps:
            cp.wait()

    return pl.pallas_call(
        body, name=name, out_shape=[jax.ShapeDtypeStruct(b.shape, b.dtype) for b in blocks],
        in_specs=[ANY] * n, out_specs=[ANY] * n, scratch_shapes=[_dma_sems(n, 1), _dma_sems(n, 1)],
    )(*blocks)


def exchange_chips(parts, name):
    n = len(parts)

    def body(*refs):
        p_refs, out_refs, (send_sems, recv_sems, local_sems) = refs[:n], refs[n:2 * n], refs[2 * n:]
        x, y, c = _place()
        jm = 2 * x + y
        chips = [(1 - x, y), (x, 1 - y), (1 - x, 1 - y)]
        mine = [pltpu.make_async_copy(p_refs[a].at[jm], out_refs[a].at[jm], local_sems.at[a, 0]) for a in range(n)]
        cps = []
        for k, (px, py) in enumerate(chips):
            for a in range(n):
                cps.append(pltpu.make_async_remote_copy(
                    src_ref=p_refs[a].at[2 * px + py], dst_ref=out_refs[a].at[jm], send_sem=send_sems.at[a, k],
                    recv_sem=recv_sems.at[a, k], device_id=(px, py, c), device_id_type=MESH_IDS))
        for cp in mine + cps:
            cp.start()
        for k, (px, py) in enumerate(chips):
            for a in range(n):
                pltpu.make_async_remote_copy(
                    src_ref=p_refs[a].at[jm], dst_ref=out_refs[a].at[2 * px + py], send_sem=send_sems.at[a, k],
                    recv_sem=recv_sems.at[a, k], device_id=(px, py, c), device_id_type=MESH_IDS).wait_recv()
        for cp in cps:
            cp.wait_send()
        for cp in mine:
            cp.wait()

    return pl.pallas_call(
        body, name=name, out_shape=[jax.ShapeDtypeStruct(p.shape, p.dtype) for p in parts],
        in_specs=[ANY] * n, out_specs=[ANY] * n, scratch_shapes=[_dma_sems(n, 3), _dma_sems(n, 3), _dma_sems(n, 1)],
    )(*parts)


def _row_tile(rows, cols, itemsize=4, limit=1 << 21):
    for cand in (2048, 1024, 512, 256, 128, 64, 32, 16):
        if rows % cand == 0 and cand * cols * itemsize <= limit:
            return cand
    return rows


def add_halves(a, b, kind, name):
    if kind == "col":
        _, k, n = a.shape
        n4 = n // 4
        tr = _row_tile(k, n4)
        in_spec = pl.BlockSpec((1, tr, n4), lambda j, h, i: (h, i, j))
        out_spec = pl.BlockSpec((1, 1, tr, n4), lambda j, h, i: (j, h, i, 0))
        grid, out_shape = (4, 2, k // tr), (4, 2, k, n4)
    elif kind == "row":
        _, k, n = a.shape
        k4 = k // 4
        tr = _row_tile(k4, n)
        nt = k4 // tr
        in_spec = pl.BlockSpec((1, tr, n), lambda j, h, i: (h, j * nt + i, 0))
        out_spec = pl.BlockSpec((1, 1, tr, n), lambda j, h, i: (j, h, i, 0))
        grid, out_shape = (4, 2, nt), (4, 2, k4, n)
    else:
        _, _, k, n = a.shape
        tr = _row_tile(k, n)
        in_spec = out_spec = pl.BlockSpec((1, 1, tr, n), lambda j, h, i: (j, h, i, 0))
        grid, out_shape = (4, 2, k // tr), a.shape

    def body(a_ref, b_ref, s_ref):
        s_ref[...] = (a_ref[...] + b_ref[...].astype(F32)).astype(BF).reshape(s_ref.shape)

    return pl.pallas_call(body, name=name, grid=grid, in_specs=[in_spec, in_spec], out_specs=out_spec,
                          out_shape=jax.ShapeDtypeStruct(out_shape, BF))(a, b)


def sum_slots(a, name):
    s, m, k, n = a.shape
    tr = _row_tile(k, n, limit=(1 << 22) // s)

    def body(a_ref, o_ref):
        acc = a_ref[0].astype(F32)
        for j in range(1, s):
            acc = acc + a_ref[j].astype(F32)
        o_ref[...] = acc

    return pl.pallas_call(body, name=name, grid=(m, k // tr), in_specs=[pl.BlockSpec((s, 1, tr, n), lambda h, i: (0, h, i, 0))],
                          out_specs=pl.BlockSpec((1, tr, n), lambda h, i: (h, i, 0)),
                          out_shape=jax.ShapeDtypeStruct((m, k, n), F32))(a)


def sum_small(arrays, name):
    n = len(arrays)

    def body(*refs):
        for a_ref, o_ref in zip(refs[:n], refs[n:]):
            acc = a_ref[0]
            for j in range(1, 8):
                acc = acc + a_ref[j]
            o_ref[...] = acc

    return pl.pallas_call(body, name=name, out_shape=[jax.ShapeDtypeStruct(a.shape[1:], F32) for a in arrays])(*arrays)


COL_SHARDED = ("ada_w", "w_in", "mla_w_uq", "mla_w_uk", "mla_w_uv", "ffn_up", "ffn_conv_w")
ROW_SHARDED = ("w_out", "ffn_down")
GATHERED = ("ada_w", "w_in", "mla_w_uq", "mla_w_uk", "mla_w_uv", "w_out", "ffn_up", "ffn_down", "ffn_conv_w")
LAYER_GATHERED = GATHERED[:-1]
REDUCED = ("w_in", "mla_w_uq", "mla_w_uk", "mla_w_uv", "w_out", "ffn_up", "ffn_down")
SMALL = ("ada_b", "gla_gate_w", "gla_gate_b", "gla_norm_g", "ret_decay", "mla_q_norm_g", "mla_kv_norm_g",
         "ln1_g", "ln1_b", "ffn_conv_b", "ln2_g", "ln2_b")
WEIGHTS = ("c_ctx", "ada_w", "ada_b", "w_in", "gla_gate_w", "gla_gate_b", "gla_norm_g", "ret_decay", "mla_q_norm_g",
           "mla_kv_norm_g", "mla_w_uq", "mla_w_uk", "mla_w_uv", "w_out", "ln1_g", "ln1_b", "ffn_up", "ffn_conv_w",
           "ffn_conv_b", "ffn_down", "ln2_g", "ln2_b")
PACK = 16 * LANES
HALF_LAYERS = DEPTH // 2


def _pad_flat(v, n):
    return jnp.concatenate([v, jnp.zeros((n - v.shape[0],), v.dtype)]) if n > v.shape[0] else v


def _my_layers(a, c):
    return lax.dynamic_slice_in_dim(a, HALF_LAYERS * c, HALF_LAYERS, axis=0)


def layer_blocks(shards, l, c):
    out = []
    for n in LAYER_GATHERED:
        a = shards[n][l]
        out.append(lax.dynamic_slice_in_dim(a, c * (a.shape[0] // 2), a.shape[0] // 2, axis=0).astype(BF))
    return out


def layer_assemble(got):
    out = {}
    for n, g in zip(LAYER_GATHERED, got):
        _, k2, n4 = g.shape
        if n in ROW_SHARDED:
            out[n] = g.reshape(8 * k2, n4)
        else:
            out[n] = jnp.transpose(g.reshape(4, 2, k2, n4), (1, 2, 0, 3)).reshape(2 * k2, 4 * n4)
    return out


def gather_conv_taps(shard, c):
    got, = all_gather8([_my_layers(shard, c)], "gather_conv_taps")
    _, _, k, n4 = got.shape
    return jnp.transpose(got.reshape(4, 2, HALF_LAYERS, k, n4), (1, 2, 3, 0, 4)).reshape(DEPTH, k, 4 * n4)


def reduce_gradients(gw, c):
    keep, give, kinds = [], [], []
    for n in REDUCED:
        g = gw[n]
        kind = "row" if n in ROW_SHARDED else ("col" if (g.shape[2] // 4) % LANES == 0 else "pre")
        if kind == "pre":
            g = jnp.transpose(g.reshape(DEPTH, g.shape[1], 4, g.shape[2] // 4), (2, 0, 1, 3))
            keep.append(lax.dynamic_slice_in_dim(g, HALF_LAYERS * c, HALF_LAYERS, axis=1))
            give.append(lax.dynamic_slice_in_dim(g, HALF_LAYERS * (1 - c), HALF_LAYERS, axis=1).astype(BF))
        else:
            keep.append(_my_layers(g, c))
            give.append(_my_layers(g, 1 - c).astype(BF))
        kinds.append(kind)
    got = swap_cores(give, "grad_swap_cores")
    parts = [add_halves(a, b, kind, "grad_add_cores_" + n) for n, a, b, kind in zip(REDUCED, keep, got, kinds)]
    landed = exchange_chips(parts, "grad_exchange_chips")
    mine = [sum_slots(a, "grad_sum_chips_" + n) for n, a in zip(REDUCED, landed)]
    theirs = swap_cores(mine, "grad_swap_back")
    out = {}
    for n, a, b in zip(REDUCED, mine, theirs):
        out[n] = jnp.concatenate([jnp.where(c == 0, a, b), jnp.where(c == 0, b, a)], axis=0)
    return out


def _pack_small(d, names):
    flat = jnp.concatenate([d[n].astype(F32).reshape(-1) for n in names])
    n = -(-flat.shape[0] // PACK) * PACK
    return _pad_flat(flat, n).reshape(n // LANES, LANES)


def _unpack_small(buf, like, names):
    flat, out, at = buf.reshape(-1), {}, 0
    for n in names:
        sz = like[n].size
        out[n] = flat[at:at + sz].reshape(like[n].shape)
        at += sz
    return out


def kernel(x, c, ctx, c_ctx, ada_w, ada_b, w_in, gla_gate_w, gla_gate_b, gla_norm_g, ret_decay, mla_q_norm_g, mla_kv_norm_g, mla_w_uq, mla_w_uk, mla_w_uv, w_out, ln1_g, ln1_b, ffn_up, ffn_conv_w, ffn_conv_b, ffn_down, ln2_g, ln2_b, loss_target, m_c_ctx, m_ada_w, m_ada_b, m_w_in, m_gla_gate_w, m_gla_gate_b, m_gla_norm_g, m_ret_decay, m_mla_q_norm_g, m_mla_kv_norm_g, m_mla_w_uq, m_mla_w_uk, m_mla_w_uv, m_w_out, m_ln1_g, m_ln1_b, m_ffn_up, m_ffn_conv_w, m_ffn_conv_b, m_ffn_down, m_ln2_g, m_ln2_b, v_c_ctx, v_ada_w, v_ada_b, v_w_in, v_gla_gate_w, v_gla_gate_b, v_gla_norm_g, v_ret_decay, v_mla_q_norm_g, v_mla_kv_norm_g, v_mla_w_uq, v_mla_w_uk, v_mla_w_uv, v_w_out, v_ln1_g, v_ln1_b, v_ffn_up, v_ffn_conv_w, v_ffn_conv_b, v_ffn_down, v_ln2_g, v_ln2_b):
    w = dict(c_ctx=c_ctx, ada_w=ada_w, ada_b=ada_b, w_in=w_in, gla_gate_w=gla_gate_w, gla_gate_b=gla_gate_b, gla_norm_g=gla_norm_g, ret_decay=ret_decay, mla_q_norm_g=mla_q_norm_g, mla_kv_norm_g=mla_kv_norm_g, mla_w_uq=mla_w_uq, mla_w_uk=mla_w_uk, mla_w_uv=mla_w_uv, w_out=w_out, ln1_g=ln1_g, ln1_b=ln1_b, ffn_up=ffn_up, ffn_conv_w=ffn_conv_w, ffn_conv_b=ffn_conv_b, ffn_down=ffn_down, ln2_g=ln2_g, ln2_b=ln2_b)
    m = dict(c_ctx=m_c_ctx, ada_w=m_ada_w, ada_b=m_ada_b, w_in=m_w_in, gla_gate_w=m_gla_gate_w, gla_gate_b=m_gla_gate_b, gla_norm_g=m_gla_norm_g, ret_decay=m_ret_decay, mla_q_norm_g=m_mla_q_norm_g, mla_kv_norm_g=m_mla_kv_norm_g, mla_w_uq=m_mla_w_uq, mla_w_uk=m_mla_w_uk, mla_w_uv=m_mla_w_uv, w_out=m_w_out, ln1_g=m_ln1_g, ln1_b=m_ln1_b, ffn_up=m_ffn_up, ffn_conv_w=m_ffn_conv_w, ffn_conv_b=m_ffn_conv_b, ffn_down=m_ffn_down, ln2_g=m_ln2_g, ln2_b=m_ln2_b)
    v = dict(c_ctx=v_c_ctx, ada_w=v_ada_w, ada_b=v_ada_b, w_in=v_w_in, gla_gate_w=v_gla_gate_w, gla_gate_b=v_gla_gate_b, gla_norm_g=v_gla_norm_g, ret_decay=v_ret_decay, mla_q_norm_g=v_mla_q_norm_g, mla_kv_norm_g=v_mla_kv_norm_g, mla_w_uq=v_mla_w_uq, mla_w_uk=v_mla_w_uk, mla_w_uv=v_mla_w_uv, w_out=v_w_out, ln1_g=v_ln1_g, ln1_b=v_ln1_b, ffn_up=v_ffn_up, ffn_conv_w=v_ffn_conv_w, ffn_conv_b=v_ffn_conv_b, ffn_down=v_ffn_down, ln2_g=v_ln2_g, ln2_b=v_ln2_b)
    xi, yi, ci = _place()
    chip = 2 * xi + yi

    whole = {n: w[n] for n in WEIGHTS if n not in GATHERED and n != "c_ctx"}
    whole["ffn_conv_w"] = gather_conv_taps(ffn_conv_w, ci)
    first_big = layer_assemble(all_gather8(layer_blocks(w, 0, ci), "gather_layer0"))
    loss, grad_x, gw, dmod, s8, lws = local_step(
        x, c, ctx, c_ctx, whole, loss_target, first_big,
        lambda l: layer_blocks(w, l, ci), lambda l, got: layer_assemble(got))
    loss = lax.psum(loss, ("x", "y", "c"))

    dsil = jnp.zeros((8, D_MODEL), F32)
    for i in range(DEPTH):
        dsil = dsil + matmul(dmod[i], lws[i]["ada_w"], "nt", F32, 8, 1024, 1536, "ada_dsilu")

    grads = reduce_gradients(gw, ci)

    small = {n: gw[n] for n in SMALL if n != "ada_b"}
    small.update(dsil=dsil[4])
    names = tuple(small)
    conv_g = gw["ffn_conv_w"].reshape(DEPTH * 3, 2 * D_FF)
    ev_small, ev_dmod, ev_s8, ev_conv = all_gather8(
        [_pack_small(small, names), dmod.reshape(DEPTH * 8, 6 * D_MODEL), s8, conv_g], "gather_small")
    sm_small, sm_dmod, sm_conv = sum_small([ev_small, ev_dmod, ev_conv], "sum_small")
    summed = _unpack_small(sm_small, small, names)
    for n in SMALL:
        if n != "ada_b":
            grads[n] = summed[n]
    grads["ada_b"] = jnp.sum(sm_dmod.reshape(DEPTH, 8, 6 * D_MODEL)[:, :5], axis=1)
    sg = jax.nn.sigmoid(c_ctx)
    grads["c_ctx"] = summed["dsil"] * (sg * (1.0 + c_ctx * (1.0 - sg)))
    ccols = ffn_conv_w.shape[2]
    grads["ffn_conv_w"] = lax.dynamic_slice_in_dim(sm_conv.reshape(DEPTH, 3, 2 * D_FF), chip * ccols, ccols, axis=2)
    s_all = ev_s8.reshape(64, D_MODEL)
    d_all = jnp.transpose(ev_dmod.reshape(8, DEPTH, 8, 6 * D_MODEL), (1, 0, 2, 3)).reshape(DEPTH, 64, 6 * D_MODEL)
    cols = ada_w.shape[2]
    g_ada = []
    for i in range(DEPTH):
        d_mine = lax.dynamic_slice_in_dim(d_all[i], chip * cols, cols, axis=1)
        g_ada.append(matmul(s_all, d_mine, "tn", F32, 1024, cols, 64, "ada_dw"))
    grads["ada_w"] = jnp.stack(g_ada)

    delta, new_m, new_v = {}, {}, {}
    for n in GATHERED:
        shp = w[n].shape
        v2 = lambda a: a.reshape(-1, shp[-1])
        d_, m_, v_ = adamw(v2(w[n]), v2(grads[n]), v2(m[n]), v2(v[n]), "adamw_" + n)
        delta[n], new_m[n], new_v[n] = d_.reshape(shp), m_.reshape(shp), v_.reshape(shp)
    rep = tuple(n for n in WEIGHTS if n not in GATHERED)
    pk = lambda d: _pack_small({n: d[n] for n in rep}, rep)
    d_, m_, v_ = adamw(pk(w), pk(grads), pk(m), pk(v), "adamw_small")
    like = {n: w[n] for n in rep}
    delta.update(_unpack_small(d_, like, rep))
    new_m.update(_unpack_small(m_, like, rep))
    new_v.update(_unpack_small(v_, like, rep))
    grads = {n: grads[n].reshape(w[n].shape) for n in WEIGHTS}
    return (loss, grad_x, *[grads[n] for n in WEIGHTS], *[delta[n] for n in WEIGHTS], *[new_m[n] for n in WEIGHTS],
            *[new_v[n] for n in WEIGHTS])
```

```python
import functools

import jax
import jax.numpy as jnp
from jax import lax
from jax.experimental import pallas as pl
from jax.experimental.pallas import tpu as pltpu

F32 = jnp.float32
BF = jnp.bfloat16

D_MODEL = 1024
DEPTH = 4
GRID_W = 64
GLA_DK = 32
GLA_TAU = 16.0
RET_DK = 32
MLA_HEADS = 8
MLA_D_NOPE = 64
MLA_D_ROPE = 32
MLA_SCALE = (MLA_D_NOPE + MLA_D_ROPE) ** -0.5
D_FF = 2816
ROPE_BASE = 10000.0
EPS = 1e-6
ALPHA = (2 * DEPTH) ** 0.25
ADAM_LR, ADAM_B1, ADAM_B2, ADAM_EPS, ADAM_WD, ADAM_STEP = 0.001, 0.9, 0.999, 1e-08, 0.01, 10

ROW_TILE = 256
CHUNK = 64
GATE_ROWS = 128
LANES = 128

C_GQ, C_GK, C_GV, C_GG, C_RQ, C_RK, C_RV, C_RG, C_CQ, C_CKV, C_LK = 0, 128, 256, 512, 768, 896, 1024, 1280, 1536, 1792, 1920
D_INP = 2048
D_IN = 1984


def _dg(a, b, ca, cb):
    return lax.dot_general(a.astype(BF), b.astype(BF), (((ca,), (cb,)), ((), ())), preferred_element_type=F32)


@jax.custom_vjp
def mm_nn(a, b):
    return _dg(a, b, 1, 0)


@jax.custom_vjp
def mm_nt(a, b):
    return _dg(a, b, 1, 1)


@jax.custom_vjp
def mm_tn(a, b):
    return _dg(a, b, 0, 0)


mm_nn.defvjp(lambda a, b: (_dg(a, b, 1, 0), (a, b)),
             lambda r, g: (mm_nt(g, r[1]).astype(r[0].dtype), mm_tn(r[0], g).astype(r[1].dtype)))
mm_nt.defvjp(lambda a, b: (_dg(a, b, 1, 1), (a, b)),
             lambda r, g: (mm_nn(g, r[1]).astype(r[0].dtype), mm_tn(g, r[0]).astype(r[1].dtype)))
mm_tn.defvjp(lambda a, b: (_dg(a, b, 0, 0), (a, b)),
             lambda r, g: (mm_nt(r[1], g).astype(r[0].dtype), mm_nn(r[0], g).astype(r[1].dtype)))


def _split3(x):
    h = x.astype(BF)
    r = x - h.astype(F32)
    m = r.astype(BF)
    lo = (r - m.astype(F32)).astype(BF)
    return h, m, lo


def _exact(x, mat, left):
    h, m, lo = _split3(x)
    if left:
        d = lambda t: lax.dot_general(mat, t, (((1,), (0,)), ((), ())), preferred_element_type=F32)
    else:
        d = lambda t: lax.dot_general(t, mat, (((1,), (0,)), ((), ())), preferred_element_type=F32)
    return (d(lo) + d(m)) + d(h)


def _iota(shape, axis):
    return lax.broadcasted_iota(jnp.int32, shape, axis)


def _tri(n, upper):
    r, c = _iota((n, n), 0), _iota((n, n), 1)
    return jnp.where((c >= r) if upper else (r >= c), 1.0, 0.0).astype(BF)


@functools.partial(jax.custom_vjp, nondiff_argnums=(1,))
def cumsum_rows(x, upper):
    return _exact(x, _tri(x.shape[0], upper), True)


cumsum_rows.defvjp(lambda x, upper: (cumsum_rows(x, upper), None),
                   lambda upper, r, g: (cumsum_rows(g, not upper),))


def _seg(n, w):
    shift = w.bit_length() - 1
    r, c = _iota((n, n), 0), _iota((n, n), 1)
    return jnp.where(lax.shift_right_logical(r, shift) == lax.shift_right_logical(c, shift), 1.0, 0.0).astype(BF)


@functools.partial(jax.custom_vjp, nondiff_argnums=(1,))
def seg_sum(x, w):
    return _exact(x, _seg(x.shape[1], w), False)


seg_sum.defvjp(lambda x, w: (seg_sum(x, w), None), lambda w, r, g: (seg_sum(g, w),))


def _place_mat(transpose):
    shape = (8 * LANES, LANES) if transpose else (LANES, 8 * LANES)
    r, c = _iota(shape, 0), _iota(shape, 1)
    src, dst = (c, r) if transpose else (r, c)
    dl = jnp.bitwise_and(dst, LANES - 1)
    ok = (dl >= 64) & (dl < 96) & (src == dl - 32)
    return jnp.where(ok, 1.0, 0.0).astype(BF)


@jax.custom_vjp
def place_kr(x):
    return _exact(x, _place_mat(False), False)


place_kr.defvjp(lambda x: (place_kr(x), None), lambda r, g: (_exact(g, _place_mat(True), False),))


@functools.partial(jax.custom_vjp, nondiff_argnums=(1,))
def lane_roll(x, s):
    return pltpu.roll(x, s, 1)


lane_roll.defvjp(lambda x, s: (pltpu.roll(x, s, 1), None),
                 lambda s, r, g: (pltpu.roll(g, (g.shape[1] - s) % g.shape[1], 1),))


def rope(x, tab, d):
    cos, sa, sb = tab
    return x * cos + lane_roll(x, LANES - d) * sa + lane_roll(x, d) * sb


def silu(x):
    return x * jax.nn.sigmoid(x)


def log_sigmoid(z):
    return jnp.minimum(z, 0.0) - jnp.log(1.0 + jnp.exp(-jnp.abs(z)))


def layer_norm(x, g, b):
    mu = jnp.mean(x, axis=-1, keepdims=True)
    xc = x - mu
    var = jnp.mean(xc * xc, axis=-1, keepdims=True)
    return xc * lax.rsqrt(var + EPS) * g + b


def matmul(a, b, mode, out_dtype, tm, tn, tk, name, b_outer=False):
    ij = (lambda f: (lambda g0, g1, kk: f(g1, g0, kk))) if b_outer else (lambda f: f)
    if mode == "nn":
        (m, k), (k2, n) = a.shape, b.shape
        a_spec = pl.BlockSpec((tm, tk), ij(lambda i, j, kk: (i, kk)))
        b_spec = pl.BlockSpec((tk, tn), ij(lambda i, j, kk: (kk, j)))
        ca, cb = 1, 0
    elif mode == "nt":
        (m, k), (n, k2) = a.shape, b.shape
        a_spec = pl.BlockSpec((tm, tk), ij(lambda i, j, kk: (i, kk)))
        b_spec = pl.BlockSpec((tn, tk), ij(lambda i, j, kk: (j, kk)))
        ca, cb = 1, 1
    else:
        (k, m), (k2, n) = a.shape, b.shape
        a_spec = pl.BlockSpec((tk, tm), ij(lambda i, j, kk: (kk, i)))
        b_spec = pl.BlockSpec((tk, tn), ij(lambda i, j, kk: (kk, j)))
        ca, cb = 0, 0
    assert k == k2 and m % tm == 0 and n % tn == 0 and k % tk == 0, (name, a.shape, b.shape, tm, tn, tk)
    nk = k // tk
    grid = (n // tn, m // tm, nk) if b_outer else (m // tm, n // tn, nk)

    def body(a_ref, b_ref, o_ref, *acc):
        part = _dg(a_ref[...], b_ref[...], ca, cb)
        if nk == 1:
            o_ref[...] = part.astype(o_ref.dtype)
            return
        acc_ref, = acc
        kk = pl.program_id(2)

        @pl.when(kk == 0)
        def _():
            acc_ref[...] = part

        @pl.when(kk > 0)
        def _():
            acc_ref[...] += part

        @pl.when(kk == nk - 1)
        def _():
            o_ref[...] = acc_ref[...].astype(o_ref.dtype)

    return pl.pallas_call(
        body, name=name, grid=grid,
        in_specs=[a_spec, b_spec], out_specs=pl.BlockSpec((tm, tn), ij(lambda i, j, kk: (i, j))),
        out_shape=jax.ShapeDtypeStruct((m, n), out_dtype),
        scratch_shapes=[] if nk == 1 else [pltpu.VMEM((tm, tn), F32)],
    )(a, b)


def _stage_specs(rows, tps, consts, ws):
    specs, args = [], []
    for arr, width, cb in rows:
        specs.append(pl.BlockSpec((ROW_TILE, width), functools.partial(lambda i, cb: (i, cb), cb=cb)))
        args.append(arr)
    for arr in tps:
        specs.append(pl.BlockSpec((1, 1, arr.shape[2]), lambda i: (i, 0, 0)))
        args.append(arr)
    for arr, period in consts:
        specs.append(pl.BlockSpec((ROW_TILE, arr.shape[1]), functools.partial(lambda i, p: (i % p, 0), p=period)))
        args.append(arr)
    for arr in ws:
        specs.append(pl.BlockSpec(arr.shape, functools.partial(lambda i, nd: (0,) * nd, nd=arr.ndim)))
        args.append(arr)
    return specs, args


def _stage_load(refs, n_rows, n_tps, n_consts, n_ws):
    it = iter(refs)
    rows = [next(it)[...].astype(F32) for _ in range(n_rows)]
    tps = [next(it)[0].astype(F32) for _ in range(n_tps)]
    consts = [next(it)[...].astype(F32) for _ in range(n_consts)]
    ws = [next(it)[...].astype(F32) for _ in range(n_ws)]
    return rows, tps, consts, ws


def stage_fwd(fn, rows, tps, consts, ws, outs, name):
    n_tiles = rows[0][0].shape[0] // ROW_TILE
    specs, args = _stage_specs(rows, tps, consts, ws)
    counts = (len(rows), len(tps), len(consts), len(ws))

    def body(*refs):
        r, t, c, w = _stage_load(refs[:sum(counts)], *counts)
        res = fn(r, t, c, w)
        for o_ref, o in zip(refs[sum(counts):], res):
            o_ref[...] = o.astype(o_ref.dtype)

    res = pl.pallas_call(
        body, name=name, grid=(n_tiles,), in_specs=specs,
        out_specs=[pl.BlockSpec((ROW_TILE, wd), lambda i: (i, 0)) for wd, _ in outs],
        out_shape=[jax.ShapeDtypeStruct((n_tiles * ROW_TILE, wd), dt) for wd, dt in outs],
    )(*args)
    return list(res)


def stage_bwd(fn, rows, tps, consts, ws, cts, row_grads, name):
    n_tiles = rows[0][0].shape[0] // ROW_TILE
    specs, args = _stage_specs(rows, tps, consts, ws)
    counts = (len(rows), len(tps), len(consts), len(ws))
    n_in = sum(counts)
    for ct in cts:
        specs.append(pl.BlockSpec((ROW_TILE, ct.shape[1]), lambda i: (i, 0)))
        args.append(ct)
    want = [k for k, dt in enumerate(row_grads) if dt is not None]
    out_specs = [pl.BlockSpec((ROW_TILE, rows[k][1]), lambda i: (i, 0)) for k in want]
    out_shape = [jax.ShapeDtypeStruct((n_tiles * ROW_TILE, rows[k][1]), row_grads[k]) for k in want]
    out_specs += [pl.BlockSpec((1, 1, a.shape[2]), lambda i: (i, 0, 0)) for a in tps]
    out_shape += [jax.ShapeDtypeStruct((n_tiles, 1, a.shape[2]), F32) for a in tps]
    out_specs += [pl.BlockSpec(a.shape, functools.partial(lambda i, nd: (0,) * nd, nd=a.ndim)) for a in ws]
    out_shape += [jax.ShapeDtypeStruct(a.shape, F32) for a in ws]

    def body(*refs):
        r, t, c, w = _stage_load(refs[:n_in], *counts)
        g = [ref[...].astype(F32) for ref in refs[n_in:n_in + len(cts)]]
        _, vjp = jax.vjp(lambda r_, t_, w_: fn(r_, t_, c, w_), r, t, w)
        dr, dt, dw = vjp(g)
        o = iter(refs[n_in + len(cts):])
        for k in want:
            ref = next(o)
            ref[...] = dr[k].astype(ref.dtype)
        for v in dt:
            next(o)[0] = v
        first = pl.program_id(0) == 0
        for v in dw:
            ref = next(o)

            @pl.when(first)
            def _():
                ref[...] = v

            @pl.when(jnp.logical_not(first))
            def _():
                ref[...] += v

    res = pl.pallas_call(body, name=name, grid=(n_tiles,), in_specs=specs, out_specs=out_specs, out_shape=out_shape)(*args)
    res = list(res)
    drows = [None] * len(rows)
    for k in want:
        drows[k] = res.pop(0)
    dtps = [res.pop(0) for _ in tps]
    dws = [res.pop(0) for _ in ws]
    return drows, dtps, dws


def fn_modulate(rows, tps, consts, ws):
    (x,), (sc, sh) = rows, tps
    return [x * (1.0 + sc) + sh]


def fn_post(rows, tps, consts, ws):
    (x, a), (g,), (lng, lnb) = rows, tps, ws
    return [layer_norm(ALPHA * x + g * a, lng, lnb)]


def fn_mix(rows, tps, consts, ws):
    ogf, ogb, orf, orb, pg, pr, mo = rows
    gng, = ws
    og = ogf + ogb
    out_g = og * lax.rsqrt(seg_sum(og * og, 64) * (1.0 / 64) + EPS) * gng * silu(pg)
    o = orf + orb
    oc = o - seg_sum(o, 64) * (1.0 / 64)
    out_r = oc * lax.rsqrt(seg_sum(oc * oc, 64) * (1.0 / 64) + EPS) * silu(pr)
    return [jnp.concatenate([out_g, out_r, mo], axis=-1)]


def fn_mla_prep(rows, tps, consts, ws):
    pq, pkv, plk = rows
    gq, gkv, wuq, wuk, wuv = ws
    qtab, ktab = consts[0:3], consts[3:6]
    cq = pq * lax.rsqrt(jnp.mean(pq * pq, axis=-1, keepdims=True) + EPS) * gq
    qp = mm_nn(cq, wuq)
    q = jnp.concatenate([rope(qp[:, h * LANES:(h + 1) * LANES], qtab, 8) * MLA_SCALE for h in range(MLA_HEADS)], axis=-1)
    ckv = pkv * lax.rsqrt(jnp.mean(pkv * pkv, axis=-1, keepdims=True) + EPS) * gkv
    k = mm_nn(ckv, wuk) + place_kr(rope(plk, ktab, 8))
    v = mm_nn(ckv, wuv)
    return [q, k, v]


def fn_assemble(rows, tps, consts, ws):
    gq, gk, gv, gg, rq, rk, rv, rg, cq, ckv, lk1, lk2 = rows
    return [jnp.concatenate([gq, gk, gv, gg, rq, rk, rv, rg, cq, ckv, lk1 + lk2], axis=-1)]


def _head_masks():
    hm = (lax.shift_right_logical(_iota((4, 1, LANES), 2), 5) == _iota((4, 1, LANES), 0)).astype(F32)
    vm = (lax.shift_right_logical(_iota((4, 1, 256), 2), 6) == _iota((4, 1, 256), 0)).astype(F32)
    bd = (lax.shift_right_logical(_iota((256, LANES), 0), 6) == lax.shift_right_logical(_iota((256, LANES), 1), 5)).astype(F32)
    return hm, vm, bd


def chunk_step(s, q, k, v, la, upper):
    hm, vm, bd = _head_masks()
    t, u = _iota((4 * CHUNK, CHUNK), 0), _iota((4 * CHUNK, CHUNK), 1)
    t = jnp.bitwise_and(t, CHUNK - 1)
    causal = (u >= t) if upper else (t >= u)
    b = cumsum_rows(la, upper)
    bend = jnp.sum(la, axis=0, keepdims=True)
    half = 0.5 * bend
    qd = q * jnp.exp(b - half)
    kd = k * jnp.exp(half - b)
    qe = (qd[None] * hm).reshape(4 * CHUNK, LANES)
    att = jnp.where(causal, mm_nt(qe, kd), 0.0)
    o_intra = (mm_nn(att, v).reshape(4, CHUNK, 256) * vm).sum(0)
    o = mm_nt(q * jnp.exp(b), s) + o_intra
    s_new = (s * jnp.exp(bend) + mm_tn(v, k * jnp.exp(bend - b))) * bd
    return o, s_new


def scan_step(sg, sr, q, k, v, lrk, rq, rk, rv, gw, gb, rdec, tab, upper):
    la_g = log_sigmoid(mm_nn(lrk, gw) + gb) * (1.0 / GLA_TAU)
    og, sg2 = chunk_step(sg, q * GLA_DK ** -0.5, k, v, la_g, upper)
    la_r = jnp.broadcast_to(log_sigmoid(rdec), (CHUNK, LANES))
    orr, sr2 = chunk_step(sr, rope(rq, tab, 16), rope(rk * RET_DK ** -0.5, tab, 16), rv, la_r, upper)
    return og, orr, sg2, sr2


def _chunk_of(n, ncc, nch, reverse):
    if not reverse:
        return n
    return jnp.where(n < ncc, ncc - 1 - n, nch - 1 + ncc - n)


def _scan_in_specs(p3, tabs, gw, gb, rdec, cidx):
    nb = p3.shape[0]

    def blk(width, cb):
        return pl.BlockSpec((nb, CHUNK, width), lambda m: (0, cidx(m), cb))

    specs = [blk(128, C_GQ // 128), blk(128, C_GK // 128), blk(256, C_GV // 256), blk(128, C_LK // 128),
             blk(128, C_RQ // 128), blk(128, C_RK // 128), blk(256, C_RV // 256)]
    args = [p3] * 7
    for t in tabs:
        specs.append(pl.BlockSpec((CHUNK, LANES), lambda m: (cidx(m), 0)))
        args.append(t)
    for w in (gw, gb, rdec):
        specs.append(pl.BlockSpec(w.shape, lambda m: (0, 0)))
        args.append(w)
    return specs, args


def scan_fwd(p, tabs, gw, gb, rdec, nb, ncc, nch, reverse, name):
    cidx = lambda n: _chunk_of(n, ncc, nch, reverse)
    t = p.shape[0]
    specs, args = _scan_in_specs(p.reshape(nb, t // nb, p.shape[1]), tabs, gw, gb, rdec, cidx)

    def body(q, k, v, lrk, rq, rk, rv, tc, ta, tb, gw_r, gb_r, rd_r, og_r, or_r, sgo_r, sro_r, sg, sr):
        @pl.when(pl.program_id(0) == 0)
        def _():
            sg[...] = jnp.zeros_like(sg)
            sr[...] = jnp.zeros_like(sr)

        sgo_r[0] = sg[...]
        sro_r[0] = sr[...]
        ld = lambda r: r[...].astype(F32)
        tab, gw_, gb_, rd_ = (ld(tc), ld(ta), ld(tb)), ld(gw_r), ld(gb_r), ld(rd_r)
        for b in range(nb):
            lb = lambda r: r[b].astype(F32)
            og, orr, sg2, sr2 = scan_step(sg[b], sr[b], lb(q), lb(k), lb(v), lb(lrk), lb(rq), lb(rk), lb(rv),
                                          gw_, gb_, rd_, tab, reverse)
            og_r[b] = og
            or_r[b] = orr
            sg[b] = sg2
            sr[b] = sr2

    row_out = pl.BlockSpec((nb, CHUNK, 256), lambda n: (0, cidx(n), 0))
    st_out = pl.BlockSpec((1, nb, 256, LANES), lambda n: (n, 0, 0, 0))
    og, orr, sgs, srs = pl.pallas_call(
        body, name=name, grid=(nch,), in_specs=specs, out_specs=[row_out, row_out, st_out, st_out],
        out_shape=[jax.ShapeDtypeStruct((nb, t // nb, 256), F32)] * 2 + [jax.ShapeDtypeStruct((nch, nb, 256, LANES), F32)] * 2,
        scratch_shapes=[pltpu.VMEM((nb, 256, LANES), F32)] * 2,
    )(*args)
    return og.reshape(t, 256), orr.reshape(t, 256), sgs, srs


def scan_bwd(p, tabs, gw, gb, rdec, sg_in, sr_in, dog, dor, prev, nb, ncc, nch, reverse, name):
    step = lambda m: nch - 1 - m
    cidx = lambda m: _chunk_of(step(m), ncc, nch, reverse)
    t = p.shape[0]
    lt = t // nb
    specs, args = _scan_in_specs(p.reshape(nb, lt, p.shape[1]), tabs, gw, gb, rdec, cidx)
    st_spec = pl.BlockSpec((1, nb, 256, LANES), lambda m: (step(m), 0, 0, 0))
    specs += [st_spec, st_spec]
    args += [sg_in, sr_in]
    row = lambda width: pl.BlockSpec((nb, CHUNK, width), lambda m: (0, cidx(m), 0))
    specs += [row(256), row(256)]
    args += [dog.reshape(nb, lt, 256), dor.reshape(nb, lt, 256)]
    widths = (128, 128, 256, 128, 128, 128, 256)
    if prev is not None:
        specs += [row(wd) for wd in widths]
        args += [a.reshape(nb, lt, a.shape[1]) for a in prev]
    n_prev = 0 if prev is None else 7

    def body(*refs):
        (q, k, v, lrk, rq, rk, rv, tc, ta, tb, gw_r, gb_r, rd_r, sgi, sri, dog_r, dor_r), rest = refs[:17], refs[17:]
        prev_r, rest = rest[:n_prev], rest[n_prev:]
        outs, (dgw_r, dgb_r, drd_r, dsg, dsr) = rest[:7], rest[7:]
        first = pl.program_id(0) == 0

        @pl.when(first)
        def _():
            dsg[...] = jnp.zeros_like(dsg)
            dsr[...] = jnp.zeros_like(dsr)

        ld = lambda r: r[...].astype(F32)
        tab, gw_, gb_, rd_ = (ld(tc), ld(ta), ld(tb)), ld(gw_r), ld(gb_r), ld(rd_r)
        wsum = None
        for b in range(nb):
            lb = lambda r: r[b].astype(F32)
            prim = (sgi[0, b], sri[0, b], lb(q), lb(k), lb(v), lb(lrk), lb(rq), lb(rk), lb(rv), gw_, gb_, rd_)
            _, vjp = jax.vjp(lambda *a: scan_step(*a, tab, reverse), *prim)
            g = vjp((lb(dog_r), lb(dor_r), dsg[b], dsr[b]))
            dsg[b] = g[0]
            dsr[b] = g[1]
            for j in range(7):
                val = g[2 + j]
                if n_prev:
                    val = val + prev_r[j][b]
                outs[j][b] = val
            wsum = g[9:12] if wsum is None else tuple(a + c for a, c in zip(wsum, g[9:12]))
        for ref, val in zip((dgw_r, dgb_r, drd_r), wsum):
            @pl.when(first)
            def _():
                ref[...] = val

            @pl.when(jnp.logical_not(first))
            def _():
                ref[...] += val

    wspec = lambda w: pl.BlockSpec(w.shape, lambda m: (0, 0))
    res = pl.pallas_call(
        body, name=name, grid=(nch,), in_specs=specs,
        out_specs=[row(wd) for wd in widths] + [wspec(gw), wspec(gb), wspec(rdec)],
        out_shape=[jax.ShapeDtypeStruct((nb, lt, wd), F32) for wd in widths]
        + [jax.ShapeDtypeStruct(w.shape, F32) for w in (gw, gb, rdec)],
        scratch_shapes=[pltpu.VMEM((nb, 256, LANES), F32)] * 2,
    )(*args)
    return tuple(a.reshape(t, a.shape[2]) for a in res[:7]), res[7], res[8], res[9]


def _attn_tiles(lc, lt):
    nct = lc // ROW_TILE
    return nct, (lt - lc) // ROW_TILE


def _attn_loop(tile, lc, lt):
    nct, nlt = _attn_tiles(lc, lt)
    for i in range(nct):
        tile(i * ROW_TILE, lc)

    def lat(i, carry):
        tile(pl.multiple_of(lc + i * ROW_TILE, ROW_TILE), lt)
        return carry

    lax.fori_loop(0, nlt, lat, 0)


def mla_fwd(q, k, v, nb, lc, lt, name, gather=()):
    ng = len(gather)

    def body(q_ref, k_ref, v_ref, *rest):
        x_refs, (o_ref, lse_ref), out_refs, sems = rest[:ng], rest[ng:ng + 2], rest[ng + 2:2 * ng + 2], rest[2 * ng + 2:]
        if ng:
            start, finish = _gather8_steps(x_refs, out_refs, *sems)
            pl.when((pl.program_id(0) == 0) & (pl.program_id(1) == 0))(start)

        def tile(r0, nk):
            rows = pl.ds(r0, ROW_TILE)
            lane = _iota((ROW_TILE, LANES), 1)
            outs, lse = [], jnp.zeros((ROW_TILE, LANES), F32)
            for j in range(2):
                s = _dg(q_ref[rows, j * LANES:(j + 1) * LANES], k_ref[0:nk, j * LANES:(j + 1) * LANES], 1, 1)
                m = jnp.max(s, axis=-1, keepdims=True)
                p = jnp.exp(s - m)
                l = jnp.sum(p, axis=-1, keepdims=True)
                outs.append(_dg(p, v_ref[0:nk, j * 64:(j + 1) * 64], 1, 0) * (1.0 / l))
                lse = jnp.where(lane == j, m + jnp.log(l), lse)
            o_ref[rows, :] = jnp.concatenate(outs, axis=-1)
            lse_ref[rows, :] = lse

        _attn_loop(tile, lc, lt)
        if ng:
            pl.when((pl.program_id(0) == nb - 1) & (pl.program_id(1) == MLA_HEADS // 2 - 1))(finish)

    pair = lambda width: pl.BlockSpec((lt, width), lambda b, h: (b, h))
    res = pl.pallas_call(
        body, name=name, grid=(nb, MLA_HEADS // 2), in_specs=[pair(2 * LANES), pair(2 * LANES), pair(LANES)] + [ANY] * ng,
        out_specs=[pair(LANES), pair(LANES)] + [ANY] * ng,
        out_shape=[jax.ShapeDtypeStruct((nb * lt, MLA_HEADS * 64), F32), jax.ShapeDtypeStruct((nb * lt, MLA_HEADS // 2 * LANES), F32)]
        + [jax.ShapeDtypeStruct((8,) + g.shape, g.dtype) for g in gather],
        scratch_shapes=[_dma_sems(ng, 7), _dma_sems(ng, 7), _dma_sems(ng, 1)] if ng else [],
    )(q, k, v, *gather)
    return res[0], res[1], list(res[2:])


def mla_bwd(q, k, v, o, lse, do, nb, lc, lt, name, exchange=()):
    ne = len(exchange)

    def body(q_ref, k_ref, v_ref, o_ref, lse_ref, do_ref, *rest):
        p_refs, (dq_ref, dk_ref, dv_ref), rest = rest[:ne], rest[ne:ne + 3], rest[ne + 3:]
        out_refs, (dka, dva), sems = rest[:ne], rest[ne:ne + 2], rest[ne + 2:]
        if ne:
            start, finish = _exchange_steps(p_refs, out_refs, *sems)
            pl.when((pl.program_id(0) == 0) & (pl.program_id(1) == 0))(start)
        dka[...] = jnp.zeros_like(dka)
        dva[...] = jnp.zeros_like(dva)

        def tile(r0, nk):
            rows = pl.ds(r0, ROW_TILE)
            dqs = []
            for j in range(2):
                qj, kj = q_ref[rows, j * LANES:(j + 1) * LANES], k_ref[0:nk, j * LANES:(j + 1) * LANES]
                vj, doj = v_ref[0:nk, j * 64:(j + 1) * 64], do_ref[rows, j * 64:(j + 1) * 64]
                p = jnp.exp(_dg(qj, kj, 1, 1) - lse_ref[rows, j:j + 1])
                dsum = jnp.sum(doj * o_ref[rows, j * 64:(j + 1) * 64], axis=-1, keepdims=True)
                ds = p * (_dg(doj, vj, 1, 1) - dsum)
                dqs.append(_dg(ds, kj, 1, 0))
                dka[j, 0:nk, :] += _dg(ds, qj, 0, 0)
                dva[j, 0:nk, :] += _dg(p, doj, 0, 0)
            dq_ref[rows, :] = jnp.concatenate(dqs, axis=-1)

        _attn_loop(tile, lc, lt)
        dk_ref[...] = jnp.concatenate([dka[0], dka[1]], axis=-1)
        dv_ref[...] = jnp.concatenate([dva[0], dva[1]], axis=-1)
        if ne:
            pl.when((pl.program_id(0) == nb - 1) & (pl.program_id(1) == MLA_HEADS // 2 - 1))(finish)

    t = nb * lt
    pair = lambda width: pl.BlockSpec((lt, width), lambda b, h: (b, h))
    res = pl.pallas_call(
        body, name=name, grid=(nb, MLA_HEADS // 2),
        in_specs=[pair(2 * LANES), pair(2 * LANES), pair(LANES), pair(LANES), pair(LANES), pair(LANES)] + [ANY] * ne,
        out_specs=[pair(2 * LANES), pair(2 * LANES), pair(LANES)] + [ANY] * ne,
        out_shape=[jax.ShapeDtypeStruct((t, MLA_HEADS * LANES), F32), jax.ShapeDtypeStruct((t, MLA_HEADS * LANES), F32),
                   jax.ShapeDtypeStruct((t, MLA_HEADS * 64), F32)] + [jax.ShapeDtypeStruct(e.shape, e.dtype) for e in exchange],
        scratch_shapes=[pltpu.VMEM((2, lt, LANES), F32), pltpu.VMEM((2, lt, 64), F32)]
        + ([_dma_sems(ne, 3), _dma_sems(ne, 3), _dma_sems(ne, 1)] if ne else []),
    )(q, k, v, o, lse, do, *exchange)
    return res[0], res[1], res[2], list(res[3:])


HALO = 16


def _gate_specs(u, per_batch, lc):
    gh = GATE_ROWS // HALO
    nh = u.shape[0] // HALO
    width = u.shape[1]
    main = pl.BlockSpec((GATE_ROWS, width), lambda i: (i, 0))
    prev = pl.BlockSpec((HALO, width), lambda i: (jnp.maximum(i * gh - 1, 0), 0))
    nxt = pl.BlockSpec((HALO, width), lambda i: (jnp.minimum((i + 1) * gh, nh - 1), 0))
    return main, prev, nxt


def _seg_edges(per_batch, lc):
    j = pl.program_id(0) % (per_batch // GATE_ROWS)
    first = (j == 0) | (j == lc // GATE_ROWS)
    last = (j == lc // GATE_ROWS - 1) | (j == per_batch // GATE_ROWS - 1)
    return first, last


def _shifted(x, prev_ref, next_ref, first, last):
    rows = _iota(x.shape, 0)
    before = jnp.where(first, 0.0, prev_ref[HALO - 1:HALO, :].astype(F32))
    after = jnp.where(last, 0.0, next_ref[0:1, :].astype(F32))
    xm = jnp.where(rows == 0, before, pltpu.roll(x, 1, 0))
    xp = jnp.where(rows == x.shape[0] - 1, after, pltpu.roll(x, x.shape[0] - 1, 0))
    return xm, xp


def gate_fwd(u, cw, cb, per_batch, lc, name):
    main, prev, nxt = _gate_specs(u, per_batch, lc)
    f = u.shape[1] // 2

    def body(u_ref, p_ref, n_ref, w_ref, b_ref, act_ref):
        first, last = _seg_edges(per_batch, lc)
        x = u_ref[...].astype(F32)
        xm, xp = _shifted(x, p_ref, n_ref, first, last)
        c = w_ref[0:1, :] * xm + w_ref[1:2, :] * x + w_ref[2:3, :] * xp + b_ref[...]
        act_ref[...] = (silu(c[:, :f]) * c[:, f:]).astype(act_ref.dtype)

    return pl.pallas_call(
        body, name=name, grid=(u.shape[0] // GATE_ROWS,),
        in_specs=[main, prev, nxt, pl.BlockSpec(cw.shape, lambda i: (0, 0)), pl.BlockSpec(cb.shape, lambda i: (0, 0))],
        out_specs=pl.BlockSpec((GATE_ROWS, f), lambda i: (i, 0)),
        out_shape=jax.ShapeDtypeStruct((u.shape[0], f), BF),
    )(u, u, u, cw, cb)


def gate_bwd(u, cw, cb, dact, per_batch, lc, name):
    main, prev, nxt = _gate_specs(u, per_batch, lc)
    f = u.shape[1] // 2

    def body(u_ref, p_ref, n_ref, w_ref, b_ref, da_ref, dc_ref, dw_ref):
        first, last = _seg_edges(per_batch, lc)
        x = u_ref[...].astype(F32)
        xm, xp = _shifted(x, p_ref, n_ref, first, last)
        c = w_ref[0:1, :] * xm + w_ref[1:2, :] * x + w_ref[2:3, :] * xp + b_ref[...]
        a, g = c[:, :f], c[:, f:]
        sg = jax.nn.sigmoid(a)
        da = da_ref[...]
        dc = jnp.concatenate([da * g * (sg * (1.0 + a * (1.0 - sg))), da * (a * sg)], axis=-1)
        dc_ref[...] = dc.astype(dc_ref.dtype)
        part = jnp.concatenate([jnp.sum(xm * dc, axis=0, keepdims=True), jnp.sum(x * dc, axis=0, keepdims=True),
                                jnp.sum(xp * dc, axis=0, keepdims=True), jnp.sum(dc, axis=0, keepdims=True),
                                jnp.zeros((4, 2 * f), F32)], axis=0)

        @pl.when(pl.program_id(0) == 0)
        def _():
            dw_ref[...] = part

        @pl.when(pl.program_id(0) > 0)
        def _():
            dw_ref[...] += part

    return pl.pallas_call(
        body, name=name, grid=(u.shape[0] // GATE_ROWS,),
        in_specs=[main, prev, nxt, pl.BlockSpec(cw.shape, lambda i: (0, 0)), pl.BlockSpec(cb.shape, lambda i: (0, 0)),
                  pl.BlockSpec((GATE_ROWS, f), lambda i: (i, 0))],
        out_specs=[main, pl.BlockSpec((8, 2 * f), lambda i: (0, 0))],
        out_shape=[jax.ShapeDtypeStruct(u.shape, BF), jax.ShapeDtypeStruct((8, 2 * f), F32)],
    )(u, u, u, cw, cb, dact)


def conv_transpose(dc, cw, per_batch, lc, name):
    main, prev, nxt = _gate_specs(dc, per_batch, lc)

    def body(d_ref, p_ref, n_ref, w_ref, du_ref):
        first, last = _seg_edges(per_batch, lc)
        x = d_ref[...].astype(F32)
        xm, xp = _shifted(x, p_ref, n_ref, first, last)
        du_ref[...] = (w_ref[0:1, :] * xp + w_ref[1:2, :] * x + w_ref[2:3, :] * xm).astype(du_ref.dtype)

    return pl.pallas_call(
        body, name=name, grid=(dc.shape[0] // GATE_ROWS,),
        in_specs=[main, prev, nxt, pl.BlockSpec(cw.shape, lambda i: (0, 0))],
        out_specs=main, out_shape=jax.ShapeDtypeStruct(dc.shape, BF),
    )(dc, dc, dc, cw)


def loss_head(x, target, tiles_per_batch, ctx_tiles, name):
    n_tiles = x.shape[0] // ROW_TILE
    lat_tiles = tiles_per_batch - ctx_tiles

    def tgt_idx(i):
        j = i % tiles_per_batch
        return jnp.where(j < ctx_tiles, 0, (i // tiles_per_batch) * lat_tiles + j - ctx_tiles), 0

    def body(x_ref, t_ref, dx_ref, l_ref):
        lat = (pl.program_id(0) % tiles_per_batch >= ctx_tiles).astype(F32)
        err = (x_ref[...] - t_ref[...]) * lat
        dx_ref[...] = err * (1.0 / D_MODEL)
        l_ref[...] = jnp.full(l_ref.shape, 0.5 / D_MODEL * jnp.sum(err * err), F32)

    return pl.pallas_call(
        body, name=name, grid=(n_tiles,),
        in_specs=[pl.BlockSpec((ROW_TILE, D_MODEL), lambda i: (i, 0)), pl.BlockSpec((ROW_TILE, D_MODEL), tgt_idx)],
        out_specs=[pl.BlockSpec((ROW_TILE, D_MODEL), lambda i: (i, 0)), pl.BlockSpec((1, 8, LANES), lambda i: (i, 0, 0))],
        out_shape=[jax.ShapeDtypeStruct(x.shape, F32), jax.ShapeDtypeStruct((n_tiles, 8, LANES), F32)],
    )(x, target)


def adamw(w, g, m, v, name):
    rows, cols = w.shape
    tr = rows
    for cand in (512, 256, 128, 64, 32, 16, 8):
        if rows % cand == 0 and cand * cols * 4 <= (1 << 20):
            tr = cand
            break

    def body(w_ref, g_ref, m_ref, v_ref, d_ref, mo_ref, vo_ref):
        gg = g_ref[...]
        m2 = ADAM_B1 * m_ref[...] + (1.0 - ADAM_B1) * gg
        v2 = ADAM_B2 * v_ref[...] + (1.0 - ADAM_B2) * (gg * gg)
        m_hat = m2 / (1.0 - ADAM_B1 ** ADAM_STEP)
        v_hat = v2 / (1.0 - ADAM_B2 ** ADAM_STEP)
        d_ref[...] = -ADAM_LR * (m_hat / (jnp.sqrt(v_hat) + ADAM_EPS) + ADAM_WD * w_ref[...])
        mo_ref[...] = m2
        vo_ref[...] = v2

    spec = pl.BlockSpec((tr, cols), lambda i: (i, 0))
    return pl.pallas_call(body, name=name, grid=(rows // tr,), in_specs=[spec] * 4, out_specs=[spec] * 3,
                          out_shape=[jax.ShapeDtypeStruct(w.shape, F32)] * 3)(w, g, m, v)


def fn_post_mod(rows, tps, consts, ws):
    (x, a), (g, sc, sh), (lng, lnb) = rows, tps, ws
    y = layer_norm(ALPHA * x + g * a, lng, lnb)
    return [y, y * (1.0 + sc) + sh]


def _pick(n, cands):
    for c in cands:
        if n % c == 0:
            return c
    return n


def rope_tables(lc, l):
    pos = jnp.arange(l, dtype=F32)
    ret_inv = 1.0 / (ROPE_BASE ** jnp.linspace(0.0, 1.0, RET_DK // 2, dtype=F32))
    ang = pos[:, None] * ret_inv
    rc, rs = jnp.cos(ang), jnp.sin(ang)
    n_ax = MLA_D_ROPE // 4
    ax_inv = ROPE_BASE ** (-jnp.arange(n_ax, dtype=F32) / n_ax)
    rows_n = l // GRID_W
    rows = jnp.repeat(jnp.arange(rows_n, dtype=F32), GRID_W)
    cols = jnp.tile(jnp.arange(GRID_W, dtype=F32), rows_n)
    ra, ca = rows[:, None] * ax_inv, cols[:, None] * ax_inv
    rwc, rws, clc, cls = jnp.cos(ra), jnp.sin(ra), jnp.cos(ca), jnp.sin(ca)
    one = lambda n: jnp.ones((l, n), F32)
    zero = lambda n: jnp.zeros((l, n), F32)
    cat = lambda parts: jnp.concatenate(parts, axis=1)

    def with_ctx(tab, is_cos):
        head = jnp.ones((lc, LANES), F32) if is_cos else jnp.zeros((lc, LANES), F32)
        return jnp.concatenate([head, tab], axis=0)

    ret = (cat([rc, rc] * 4), cat([-rs, zero(16)] * 4), cat([zero(16), rs] * 4))
    ax_c = [rwc, rwc, clc, clc]
    ax_a = [-rws, zero(8), -cls, zero(8)]
    ax_b = [zero(8), rws, zero(8), cls]
    qt = (cat([one(64)] + ax_c + [one(32)]), cat([zero(64)] + ax_a + [zero(32)]), cat([zero(64)] + ax_b + [zero(32)]))
    kt = (cat([one(32)] + ax_c + [one(64)]), cat([zero(32)] + ax_a + [zero(64)]), cat([zero(32)] + ax_b + [zero(64)]))
    fix = lambda t3: tuple(with_ctx(t, k == 0) for k, t in enumerate(t3))
    return fix(ret), fix(qt), fix(kt)


_IN_ORDER = ((0, 128), (128, 256), (256, 512), (544, 800), (800, 928), (928, 1056), (1056, 1312), (1312, 1568),
             (1568, 1824), (1824, 1952), (512, 544), (1952, 1984))


def permute_w_in(w):
    parts = [w[:, a:b] for a, b in _IN_ORDER] + [jnp.zeros((w.shape[0], D_INP - D_IN), w.dtype)]
    return jnp.concatenate(parts, axis=1)


def unpermute_w_in(g):
    out, at = {}, 0
    for a, b in _IN_ORDER:
        out[a] = g[:, at:at + b - a]
        at += b - a
    return jnp.concatenate([out[a] for a in sorted(out)], axis=1)


def layer_weights(big, w, l):
    f = lambda a: a.astype(F32)
    r = {}
    r["ada_w"] = big["ada_w"].astype(BF)
    r["win"] = permute_w_in(big["w_in"]).astype(BF)
    r["wout"] = big["w_out"].astype(BF)
    r["wup"] = big["ffn_up"].astype(BF)
    r["wdown"] = big["ffn_down"].astype(BF)
    uq = big["mla_w_uq"].reshape(256, MLA_HEADS, 96)
    r["wuq"] = jnp.pad(uq, ((0, 0), (0, 0), (0, 32))).reshape(256, 8 * LANES).astype(BF)
    uk = big["mla_w_uk"].reshape(128, MLA_HEADS, 64)
    r["wuk"] = jnp.pad(uk, ((0, 0), (0, 0), (0, 64))).reshape(128, 8 * LANES).astype(BF)
    r["wuv"] = big["mla_w_uv"].astype(BF)
    gw = f(w["gla_gate_w"][l])
    z16 = jnp.zeros((16, LANES), F32)
    z96 = jnp.zeros((96, LANES), F32)
    r["gw"] = (jnp.concatenate([gw[0], z16, z96], axis=0), jnp.concatenate([z16, gw[1], z96], axis=0))
    r["gb"] = tuple(f(w["gla_gate_b"][l][d]).reshape(1, LANES) for d in range(2))
    r["rdec"] = tuple(jnp.repeat(f(w["ret_decay"][l][d]), 32).reshape(1, LANES) for d in range(2))
    r["gng"] = jnp.tile(f(w["gla_norm_g"][l]), 4).reshape(1, 256)
    r["gq"] = f(w["mla_q_norm_g"][l]).reshape(1, 256)
    r["gkv"] = f(w["mla_kv_norm_g"][l]).reshape(1, 128)
    for n in ("ln1_g", "ln1_b", "ln2_g", "ln2_b"):
        r[n] = f(w[n][l]).reshape(1, D_MODEL)
    r["cw"] = f(w["ffn_conv_w"][l])
    r["cb"] = f(w["ffn_conv_b"][l]).reshape(1, 2 * D_FF)
    return r


def tile_params(mod_l, nb, nct, nlt):
    m6 = mod_l.reshape(8, 6, D_MODEL)
    out = []
    for j in range(6):
        parts = []
        for b in range(nb):
            parts.append(jnp.broadcast_to(m6[4, j], (nct, 1, D_MODEL)))
            parts.append(jnp.broadcast_to(m6[b, j], (nlt, 1, D_MODEL)))
        out.append(jnp.concatenate(parts, axis=0))
    return out


def tile_param_grads(dts, nb, nct, nlt):
    cols = []
    for dt in dts:
        d = dt.reshape(nb, nct + nlt, D_MODEL)
        lat = jnp.sum(d[:, nct:], axis=1)
        ctx = jnp.sum(d[:, :nct], axis=(0, 1))
        cols.append(jnp.concatenate([lat, jnp.zeros((4 - nb, D_MODEL), F32), ctx[None], jnp.zeros((3, D_MODEL), F32)], axis=0))
    return jnp.stack(cols, axis=1).reshape(8, 6 * D_MODEL)


def layer_forward(x, h1, tp, lw, tabs, dims, nxt):
    nb, lc, lt = dims
    t = x.shape[0]
    nbt = lt // ROW_TILE
    ncc, nch = lc // CHUNK, lt // CHUNK
    tm = _pick(t, (1024, 768, 512, 256))
    ret_tab, q_tab, k_tab = tabs
    full = lambda a: (a, a.shape[1], 0)
    p = matmul(h1, lw["win"], "nn", F32, tm, D_INP, 1024, "proj_in")
    ogf, orf, sgf, srf = scan_fwd(p, ret_tab, lw["gw"][0], lw["gb"][0], lw["rdec"][0], nb, ncc, nch, False, "scan_fwd_f")
    ogb, orb, sgb, srb = scan_fwd(p, ret_tab, lw["gw"][1], lw["gb"][1], lw["rdec"][1], nb, ncc, nch, True, "scan_fwd_b")
    prep_rows = [(p, 256, C_CQ // 256), (p, 128, C_CKV // 128), (p, 128, C_LK // 128)]
    prep_consts = [(a, nbt) for a in q_tab + k_tab]
    prep_ws = [lw["gq"], lw["gkv"], lw["wuq"], lw["wuk"], lw["wuv"]]
    q, k, v = stage_fwd(fn_mla_prep, prep_rows, [], prep_consts, prep_ws, [(1024, BF), (1024, BF), (512, BF)], "mla_prep")
    if nxt is None:
        mo, lse, _ = mla_fwd(q, k, v, nb, lc, lt, "mla_attn_last")
        made, tp_next = None, None
    else:
        mo, lse, got = mla_fwd(q, k, v, nb, lc, lt, "mla_attn", gather=nxt[0])
        made = nxt[1](got)
        tp_next = (made[0][1], made[0][0])
    mix_rows = [full(ogf), full(ogb), full(orf), full(orb), (p, 256, C_GG // 256), (p, 256, C_RG // 256), full(mo)]
    m, = stage_fwd(fn_mix, mix_rows, [], [], [lw["gng"]], [(1024, BF)], "mix")
    a = matmul(m, lw["wout"], "nn", F32, tm, 1024, 1024, "proj_out")
    x1, h2 = stage_fwd(fn_post_mod, [full(x), full(a)], [tp[2], tp[4], tp[3]], [], [lw["ln1_g"], lw["ln1_b"]],
                       [(1024, F32), (1024, BF)], "post1")
    u = matmul(h2, lw["wup"], "nn", BF, tm, 1408, 1024, "ffn_up", b_outer=True)
    act = gate_fwd(u, lw["cw"], lw["cb"], lt, lc, "ffn_gate")
    f = matmul(act, lw["wdown"], "nn", F32, tm, 1024, D_FF, "ffn_down")
    if tp_next is None:
        x2, = stage_fwd(fn_post, [full(x1), full(f)], [tp[5]], [], [lw["ln2_g"], lw["ln2_b"]], [(1024, F32)], "post2_last")
        h1n = None
    else:
        x2, h1n = stage_fwd(fn_post_mod, [full(x1), full(f)], [tp[5], tp_next[0], tp_next[1]], [],
                            [lw["ln2_g"], lw["ln2_b"]], [(1024, F32), (1024, BF)], "post2")
    res = dict(x=x, h1=h1, p=p, ogf=ogf, orf=orf, sgf=sgf, srf=srf, ogb=ogb, orb=orb, sgb=sgb, srb=srb, q=q, k=k, v=v,
               mo=mo, lse=lse, m=m, a=a, x1=x1, h2=h2, u=u, act=act, f=f, mix_rows=mix_rows, prep_rows=prep_rows,
               prep_consts=prep_consts, prep_ws=prep_ws)
    return x2, h1n, res, made


def layer_backward(dx2, dh1n, res, tp, tp_next, lw, tabs, dims, exchange=()):
    nb, lc, lt = dims
    r = res
    t = dx2.shape[0]
    ncc, nch = lc // CHUNK, lt // CHUNK
    tm = _pick(t, (1024, 768, 512, 256))
    tkr = _pick(t, (2304, 1536, 1024, 768, 512))
    ret_tab = tabs[0]
    full = lambda a: (a, a.shape[1], 0)
    g = {}
    if tp_next is None:
        (dx1a, df), (dg2,), (g["ln2_g"], g["ln2_b"]) = stage_bwd(
            fn_post, [full(r["x1"]), full(r["f"])], [tp[5]], [], [lw["ln2_g"], lw["ln2_b"]], [dx2], [F32, BF], "post2_last_bwd")
        dnext = None
    else:
        (dx1a, df), (dg2, dsc1n, dsh1n), (g["ln2_g"], g["ln2_b"]) = stage_bwd(
            fn_post_mod, [full(r["x1"]), full(r["f"])], [tp[5], tp_next[0], tp_next[1]], [], [lw["ln2_g"], lw["ln2_b"]],
            [dx2, dh1n], [F32, BF], "post2_bwd")
        dnext = (dsc1n, dsh1n)
    dact = matmul(df, lw["wdown"], "nt", F32, tm, 1408, 1024, "ffn_down_dx", b_outer=True)
    g["ffn_down"] = matmul(r["act"], df, "tn", F32, 1408, 1024, tkr, "ffn_down_dw")
    dc, dcw = gate_bwd(r["u"], lw["cw"], lw["cb"], dact, lt, lc, "ffn_gate_bwd")
    g["ffn_conv_w"], g["ffn_conv_b"] = dcw[0:3], dcw[3]
    du = conv_transpose(dc, lw["cw"], lt, lc, "ffn_conv_t")
    dh2 = matmul(du, lw["wup"], "nt", F32, _pick(t, (512, 256)), 1024, 2 * D_FF, "ffn_up_dx")
    g["ffn_up"] = matmul(r["h2"], du, "tn", F32, 1024, 1408, tkr, "ffn_up_dw")
    (dxa, da), (dg1, dsc2, dsh2), (g["ln1_g"], g["ln1_b"]) = stage_bwd(
        fn_post_mod, [full(r["x"]), full(r["a"])], [tp[2], tp[4], tp[3]], [], [lw["ln1_g"], lw["ln1_b"]],
        [dx1a, dh2], [F32, BF], "post1_bwd")
    dm = matmul(da, lw["wout"], "nt", F32, tm, 1024, 1024, "proj_out_dx")
    g["w_out"] = matmul(r["m"], da, "tn", F32, 1024, 1024, tkr, "proj_out_dw")
    (dog, _, dor, _, dpg, dpr, dmo), _, (dgng,) = stage_bwd(
        fn_mix, r["mix_rows"], [], [], [lw["gng"]], [dm], [F32, None, F32, None, F32, F32, F32], "mix_bwd")
    g["gla_norm_g"] = jnp.sum(dgng.reshape(4, 64), axis=0)
    dq, dk, dv, landed = mla_bwd(r["q"], r["k"], r["v"], r["mo"], r["lse"], dmo, nb, lc, lt,
                                 "mla_attn_bwd" if exchange else "mla_attn_bwd_first", exchange=exchange)
    (dpq, dpkv, dplk), _, (dgq, dgkv, dwuq, dwuk, dwuv) = stage_bwd(
        fn_mla_prep, r["prep_rows"], [], r["prep_consts"], r["prep_ws"], [dq, dk, dv], [F32, F32, F32], "mla_prep_bwd")
    g["mla_q_norm_g"], g["mla_kv_norm_g"] = dgq.reshape(256), dgkv.reshape(128)
    g["mla_w_uq"] = dwuq.reshape(256, MLA_HEADS, LANES)[:, :, :96].reshape(256, MLA_HEADS * 96)
    g["mla_w_uk"] = dwuk.reshape(128, MLA_HEADS, LANES)[:, :, :64].reshape(128, MLA_HEADS * 64)
    g["mla_w_uv"] = dwuv
    s7, dgw0, dgb0, drd0 = scan_bwd(r["p"], ret_tab, lw["gw"][0], lw["gb"][0], lw["rdec"][0], r["sgf"], r["srf"], dog, dor,
                                    None, nb, ncc, nch, False, "scan_bwd_f")
    s7, dgw1, dgb1, drd1 = scan_bwd(r["p"], ret_tab, lw["gw"][1], lw["gb"][1], lw["rdec"][1], r["sgb"], r["srb"], dog, dor,
                                    s7, nb, ncc, nch, True, "scan_bwd_b")
    g["gla_gate_w"] = jnp.stack([dgw0[0:16], dgw1[16:32]])
    g["gla_gate_b"] = jnp.stack([dgb0[0], dgb1[0]])
    g["ret_decay"] = jnp.stack([jnp.sum(drd0.reshape(4, 32), axis=1), jnp.sum(drd1.reshape(4, 32), axis=1)])
    gq_, gk_, gv_, glrk, rq_, rk_, rv_ = s7
    pieces = [gq_, gk_, gv_, dpg, rq_, rk_, rv_, dpr, dpq, dpkv, glrk, dplk]
    dp, = stage_fwd(fn_assemble, [full(a) for a in pieces], [], [], [], [(D_INP, BF)], "dproj_assemble")
    dh1 = matmul(dp, lw["win"], "nt", F32, tm, 1024, D_INP, "proj_in_dx")
    g["w_in"] = unpermute_w_in(matmul(r["h1"], dp, "tn", F32, 1024, 1024, tkr, "proj_in_dw"))
    for n in ("ln1_g", "ln1_b", "ln2_g", "ln2_b"):
        g[n] = g[n].reshape(D_MODEL)
    dtp = [None, None, dg1, dsh2, dsc2, dg2]
    return dxa, dh1, dtp, dnext, g, landed


def local_step(x, c, ctx, c_ctx, w, loss_target, first_big, next_blocks, assemble, reduce_hooks=None):
    nb, l, _ = x.shape
    lc = ctx.shape[1]
    lt = lc + l
    dims = (nb, lc, lt)
    nct, nlt = lc // ROW_TILE, l // ROW_TILE
    tabs = rope_tables(lc, l)
    x0 = jnp.concatenate([ctx, x], axis=1).reshape(nb * lt, D_MODEL)
    s8 = jnp.concatenate([silu(c), jnp.zeros((4 - nb, D_MODEL), F32), silu(c_ctx)[None], jnp.zeros((3, D_MODEL), F32)], axis=0)

    def make_layer(i, big):
        lw = layer_weights(big, w, i)
        mod = matmul(s8, lw["ada_w"], "nn", F32, 8, 1536, 1024, "ada_mod") + w["ada_b"][i].astype(F32)[None]
        return tile_params(mod, nb, nct, nlt), lw

    made = make_layer(0, first_big)
    lws, tps = [], []
    h1, = stage_fwd(fn_modulate, [(x0, D_MODEL, 0)], [made[0][1], made[0][0]], [], [], [(D_MODEL, BF)], "mod_in")
    xs, ress = x0, []
    for i in range(DEPTH):
        tps.append(made[0])
        lws.append(made[1])
        nxt = None
        if i < DEPTH - 1:
            nxt = (next_blocks(i + 1), functools.partial(lambda got, j: make_layer(j, assemble(j, got)), j=i + 1))
        xs, h1, res, made = layer_forward(xs, h1, tps[i], lws[i], tabs, dims, nxt)
        ress.append(res)
    dx, lparts = loss_head(xs, loss_target.reshape(nb * l, D_MODEL), nct + nlt, nct, "loss_head")
    loss = jnp.sum(lparts[:, 0, 0])
    grads = [None] * DEPTH
    dtps = [None] * DEPTH
    reduced = [None] * DEPTH
    dh1, parts = None, ()
    for i in reversed(range(DEPTH)):
        tpn = None if i == DEPTH - 1 else (tps[i + 1][1], tps[i + 1][0])
        dx, dh1, dtp, dn, grads[i], landed = layer_backward(dx, dh1, ress[i], tps[i], tpn, lws[i], tabs, dims, parts)
        if dn is not None:
            dtps[i + 1][1], dtps[i + 1][0] = dn
        dtps[i] = dtp
        if reduce_hooks is not None:
            if parts:
                reduced[i + 1] = reduce_hooks[1](landed)
            parts = reduce_hooks[0]({n: grads[i].pop(n) for n in REDUCED})
    if reduce_hooks is not None:
        reduced[0] = reduce_hooks[1](exchange_chips(parts, "grad_exchange_last"))
    (dx0b,), (dsc1, dsh1), _ = stage_bwd(fn_modulate, [(x0, D_MODEL, 0)], [tps[0][1], tps[0][0]], [], [], [dh1], [F32], "mod_in_bwd")
    dtps[0][1], dtps[0][0] = dsc1, dsh1
    grad_x = (dx + dx0b).reshape(nb, lt, D_MODEL)[:, lc:]
    dmod = jnp.stack([tile_param_grads(d, nb, nct, nlt) for d in dtps])
    gw = {n: jnp.stack([grads[i][n] for i in range(DEPTH)]) for n in grads[0]}
    return loss, grad_x, gw, dmod, s8, lws, reduced


MESH_IDS = pl.DeviceIdType.MESH
ANY = pl.BlockSpec(memory_space=pl.ANY)


def _place():
    return lax.axis_index("x"), lax.axis_index("y"), lax.axis_index("c")


def _dma_sems(n, per):
    return pltpu.SemaphoreType.DMA((n, per))


def _gather8_steps(x_refs, out_refs, send_sems, recv_sems, local_sems):
    n = len(x_refs)
    x, y, c = _place()
    me, sibling = (x, y, c), (x, y, 1 - c)
    chips = [(1 - x, y), (x, 1 - y), (1 - x, 1 - y)]

    def copy(a, k, blk, to, own=False):
        slot = out_refs[a].at[4 * blk[0] + 2 * blk[1] + blk[2]]
        return pltpu.make_async_remote_copy(
            src_ref=x_refs[a] if own else slot, dst_ref=slot,
            send_sem=send_sems.at[a, k], recv_sem=recv_sems.at[a, k], device_id=to, device_id_type=MESH_IDS)

    def local(a):
        return pltpu.make_async_copy(x_refs[a], out_refs[a].at[4 * x + 2 * y + c], local_sems.at[a, 0])

    def first_copies():
        cps = []
        for a in range(n):
            cps.append(copy(a, 0, me, sibling, own=True))
            cps += [copy(a, 1 + j, me, (*chip, c), own=True) for j, chip in enumerate(chips)]
        return cps

    def start():
        for a in range(n):
            local(a).start()
        for cp in first_copies():
            cp.start()

    def finish():
        passed = []
        for j, chip in enumerate(chips):
            for a in range(n):
                copy(a, 1 + j, (*chip, c), me).wait_recv()
                passed.append(copy(a, 4 + j, (*chip, c), sibling))
                passed[-1].start()
        for a in range(n):
            copy(a, 0, sibling, me).wait_recv()
            for j, chip in enumerate(chips):
                copy(a, 4 + j, (*chip, 1 - c), me).wait_recv()
        for cp in first_copies() + passed:
            cp.wait_send()
        for a in range(n):
            local(a).wait()

    return start, finish


def all_gather8(blocks, name):
    n = len(blocks)

    def body(*refs):
        start, finish = _gather8_steps(refs[:n], refs[n:2 * n], *refs[2 * n:])
        start()
        finish()

    return pl.pallas_call(
        body, name=name, out_shape=[jax.ShapeDtypeStruct((8,) + b.shape, b.dtype) for b in blocks],
        in_specs=[ANY] * n, out_specs=[ANY] * n,
        scratch_shapes=[_dma_sems(n, 7), _dma_sems(n, 7), _dma_sems(n, 1)],
    )(*blocks)


def swap_cores(blocks, name):
    n = len(blocks)

    def body(*refs):
        x_refs, out_refs, (send_sems, recv_sems) = refs[:n], refs[n:2 * n], refs[2 * n:]
        x, y, c = _place()
        cps = [pltpu.make_async_remote_copy(src_ref=x_refs[a], dst_ref=out_refs[a], send_sem=send_sems.at[a, 0],
                                            recv_sem=recv_sems.at[a, 0], device_id=(x, y, 1 - c), device_id_type=MESH_IDS)
               for a in range(n)]
        for cp in cps:
            cp.start()
        for cp in cps:
            cp.wait()

    return pl.pallas_call(
        body, name=name, out_shape=[jax.ShapeDtypeStruct(b.shape, b.dtype) for b in blocks],
        in_specs=[ANY] * n, out_specs=[ANY] * n, scratch_shapes=[_dma_sems(n, 1), _dma_sems(n, 1)],
    )(*blocks)


def _exchange_steps(p_refs, out_refs, send_sems, recv_sems, local_sems):
    n = len(p_refs)
    x, y, c = _place()
    jm = 2 * x + y
    chips = [(1 - x, y), (x, 1 - y), (1 - x, 1 - y)]

    def local(a):
        return pltpu.make_async_copy(p_refs[a].at[jm], out_refs[a].at[jm], local_sems.at[a, 0])

    def sends():
        return [pltpu.make_async_remote_copy(
            src_ref=p_refs[a].at[2 * px + py], dst_ref=out_refs[a].at[jm], send_sem=send_sems.at[a, k],
            recv_sem=recv_sems.at[a, k], device_id=(px, py, c), device_id_type=MESH_IDS)
            for k, (px, py) in enumerate(chips) for a in range(n)]

    def start():
        for a in range(n):
            local(a).start()
        for cp in sends():
            cp.start()

    def finish():
        for k, (px, py) in enumerate(chips):
            for a in range(n):
                pltpu.make_async_remote_copy(
                    src_ref=p_refs[a].at[jm], dst_ref=out_refs[a].at[2 * px + py], send_sem=send_sems.at[a, k],
                    recv_sem=recv_sems.at[a, k], device_id=(px, py, c), device_id_type=MESH_IDS).wait_recv()
        for cp in sends():
            cp.wait_send()
        for a in range(n):
            local(a).wait()

    return start, finish


def exchange_chips(parts, name):
    n = len(parts)

    def body(*refs):
        start, finish = _exchange_steps(refs[:n], refs[n:2 * n], *refs[2 * n:])
        start()
        finish()

    return pl.pallas_call(
        body, name=name, out_shape=[jax.ShapeDtypeStruct(p.shape, p.dtype) for p in parts],
        in_specs=[ANY] * n, out_specs=[ANY] * n, scratch_shapes=[_dma_sems(n, 3), _dma_sems(n, 3), _dma_sems(n, 1)],
    )(*parts)


def _row_tile(rows, cols, itemsize=4, limit=1 << 21):
    for cand in (2048, 1024, 512, 256, 128, 64, 32, 16):
        if rows % cand == 0 and cand * cols * itemsize <= limit:
            return cand
    return rows


def add_halves(a, b, kind, name):
    if kind == "col":
        m, k, n = a.shape
        n4 = n // 4
        tr = _row_tile(k, n4)
        in_spec = pl.BlockSpec((1, tr, n4), lambda j, h, i: (h, i, j))
        out_spec = pl.BlockSpec((1, 1, tr, n4), lambda j, h, i: (j, h, i, 0))
        grid, out_shape = (4, m, k // tr), (4, m, k, n4)
    elif kind == "row":
        m, k, n = a.shape
        k4 = k // 4
        tr = _row_tile(k4, n)
        nt = k4 // tr
        in_spec = pl.BlockSpec((1, tr, n), lambda j, h, i: (h, j * nt + i, 0))
        out_spec = pl.BlockSpec((1, 1, tr, n), lambda j, h, i: (j, h, i, 0))
        grid, out_shape = (4, m, nt), (4, m, k4, n)
    else:
        _, m, k, n = a.shape
        tr = _row_tile(k, n)
        in_spec = out_spec = pl.BlockSpec((1, 1, tr, n), lambda j, h, i: (j, h, i, 0))
        grid, out_shape = (4, m, k // tr), a.shape

    def body(a_ref, b_ref, s_ref):
        s_ref[...] = (a_ref[...] + b_ref[...].astype(F32)).astype(BF).reshape(s_ref.shape)

    return pl.pallas_call(body, name=name, grid=grid, in_specs=[in_spec, in_spec], out_specs=out_spec,
                          out_shape=jax.ShapeDtypeStruct(out_shape, BF))(a, b)


def sum_slots(a, name):
    s, m, k, n = a.shape
    tr = _row_tile(k, n, limit=(1 << 22) // s)

    def body(a_ref, o_ref):
        acc = a_ref[0].astype(F32)
        for j in range(1, s):
            acc = acc + a_ref[j].astype(F32)
        o_ref[...] = acc

    return pl.pallas_call(body, name=name, grid=(m, k // tr), in_specs=[pl.BlockSpec((s, 1, tr, n), lambda h, i: (0, h, i, 0))],
                          out_specs=pl.BlockSpec((1, tr, n), lambda h, i: (h, i, 0)),
                          out_shape=jax.ShapeDtypeStruct((m, k, n), F32))(a)


def sum_small(arrays, name):
    n = len(arrays)

    def body(*refs):
        for a_ref, o_ref in zip(refs[:n], refs[n:]):
            acc = a_ref[0]
            for j in range(1, 8):
                acc = acc + a_ref[j]
            o_ref[...] = acc

    return pl.pallas_call(body, name=name, out_shape=[jax.ShapeDtypeStruct(a.shape[1:], F32) for a in arrays])(*arrays)


COL_SHARDED = ("ada_w", "w_in", "mla_w_uq", "mla_w_uk", "mla_w_uv", "ffn_up", "ffn_conv_w")
ROW_SHARDED = ("w_out", "ffn_down")
GATHERED = ("ada_w", "w_in", "mla_w_uq", "mla_w_uk", "mla_w_uv", "w_out", "ffn_up", "ffn_down", "ffn_conv_w")
LAYER_GATHERED = GATHERED[:-1]
REDUCED = ("w_in", "mla_w_uq", "mla_w_uk", "mla_w_uv", "w_out", "ffn_up", "ffn_down")
SMALL = ("ada_b", "gla_gate_w", "gla_gate_b", "gla_norm_g", "ret_decay", "mla_q_norm_g", "mla_kv_norm_g",
         "ln1_g", "ln1_b", "ffn_conv_b", "ln2_g", "ln2_b")
WEIGHTS = ("c_ctx", "ada_w", "ada_b", "w_in", "gla_gate_w", "gla_gate_b", "gla_norm_g", "ret_decay", "mla_q_norm_g",
           "mla_kv_norm_g", "mla_w_uq", "mla_w_uk", "mla_w_uv", "w_out", "ln1_g", "ln1_b", "ffn_up", "ffn_conv_w",
           "ffn_conv_b", "ffn_down", "ln2_g", "ln2_b")
PACK = 16 * LANES
HALF_LAYERS = DEPTH // 2


def _pad_flat(v, n):
    return jnp.concatenate([v, jnp.zeros((n - v.shape[0],), v.dtype)]) if n > v.shape[0] else v


def _my_layers(a, c):
    return lax.dynamic_slice_in_dim(a, HALF_LAYERS * c, HALF_LAYERS, axis=0)


def layer_blocks(shards, l, c):
    out = []
    for n in LAYER_GATHERED:
        a = shards[n][l]
        out.append(lax.dynamic_slice_in_dim(a, c * (a.shape[0] // 2), a.shape[0] // 2, axis=0).astype(BF))
    return out


def layer_assemble(got):
    out = {}
    for n, g in zip(LAYER_GATHERED, got):
        _, k2, n4 = g.shape
        if n in ROW_SHARDED:
            out[n] = g.reshape(8 * k2, n4)
        else:
            out[n] = jnp.transpose(g.reshape(4, 2, k2, n4), (1, 2, 0, 3)).reshape(2 * k2, 4 * n4)
    return out


def gather_conv_taps(shard, c):
    got, = all_gather8([_my_layers(shard, c)], "gather_conv_taps")
    _, _, k, n4 = got.shape
    return jnp.transpose(got.reshape(4, 2, HALF_LAYERS, k, n4), (1, 2, 3, 0, 4)).reshape(DEPTH, k, 4 * n4)


def _reduce_kind(n, g):
    return "row" if n in ROW_SHARDED else ("col" if (g.shape[-1] // 4) % LANES == 0 else "pre")


def reduce_begin(g, c):
    keep, give, kinds = [], [], []
    for n in REDUCED:
        a, kind = g[n], _reduce_kind(n, g[n])
        if kind == "pre":
            a = jnp.transpose(a.reshape(a.shape[0], 4, a.shape[1] // 4), (1, 0, 2))
        axis = a.ndim - 1 if kind == "row" else a.ndim - 2
        half = a.shape[axis] // 2
        lead = (slice(None), None) if kind == "pre" else (None,)
        keep.append(lax.dynamic_slice_in_dim(a, half * c, half, axis=axis)[lead])
        give.append(lax.dynamic_slice_in_dim(a, half * (1 - c), half, axis=axis)[lead].astype(BF))
        kinds.append(kind)
    got = swap_cores(give, "grad_swap_cores")
    return [add_halves(a, b, kind, "grad_add_cores_" + n) for n, a, b, kind in zip(REDUCED, keep, got, kinds)]


def reduce_end(landed):
    return [sum_slots(a, "grad_sum_chips_" + n) for n, a in zip(REDUCED, landed)]


def reduce_assemble(layers, c):
    mine = [jnp.concatenate([layers[l][k] for l in range(DEPTH)], axis=0) for k in range(len(REDUCED))]
    theirs = swap_cores(mine, "grad_swap_back")
    out = {}
    for n, a, b in zip(REDUCED, mine, theirs):
        out[n] = jnp.concatenate([jnp.where(c == 0, a, b), jnp.where(c == 0, b, a)], axis=2 if n in ROW_SHARDED else 1)
    return out


def _pack_small(d, names):
    flat = jnp.concatenate([d[n].astype(F32).reshape(-1) for n in names])
    n = -(-flat.shape[0] // PACK) * PACK
    return _pad_flat(flat, n).reshape(n // LANES, LANES)


def _unpack_small(buf, like, names):
    flat, out, at = buf.reshape(-1), {}, 0
    for n in names:
        sz = like[n].size
        out[n] = flat[at:at + sz].reshape(like[n].shape)
        at += sz
    return out


def kernel(x, c, ctx, c_ctx, ada_w, ada_b, w_in, gla_gate_w, gla_gate_b, gla_norm_g, ret_decay, mla_q_norm_g, mla_kv_norm_g, mla_w_uq, mla_w_uk, mla_w_uv, w_out, ln1_g, ln1_b, ffn_up, ffn_conv_w, ffn_conv_b, ffn_down, ln2_g, ln2_b, loss_target, m_c_ctx, m_ada_w, m_ada_b, m_w_in, m_gla_gate_w, m_gla_gate_b, m_gla_norm_g, m_ret_decay, m_mla_q_norm_g, m_mla_kv_norm_g, m_mla_w_uq, m_mla_w_uk, m_mla_w_uv, m_w_out, m_ln1_g, m_ln1_b, m_ffn_up, m_ffn_conv_w, m_ffn_conv_b, m_ffn_down, m_ln2_g, m_ln2_b, v_c_ctx, v_ada_w, v_ada_b, v_w_in, v_gla_gate_w, v_gla_gate_b, v_gla_norm_g, v_ret_decay, v_mla_q_norm_g, v_mla_kv_norm_g, v_mla_w_uq, v_mla_w_uk, v_mla_w_uv, v_w_out, v_ln1_g, v_ln1_b, v_ffn_up, v_ffn_conv_w, v_ffn_conv_b, v_ffn_down, v_ln2_g, v_ln2_b):
    w = dict(c_ctx=c_ctx, ada_w=ada_w, ada_b=ada_b, w_in=w_in, gla_gate_w=gla_gate_w, gla_gate_b=gla_gate_b, gla_norm_g=gla_norm_g, ret_decay=ret_decay, mla_q_norm_g=mla_q_norm_g, mla_kv_norm_g=mla_kv_norm_g, mla_w_uq=mla_w_uq, mla_w_uk=mla_w_uk, mla_w_uv=mla_w_uv, w_out=w_out, ln1_g=ln1_g, ln1_b=ln1_b, ffn_up=ffn_up, ffn_conv_w=ffn_conv_w, ffn_conv_b=ffn_conv_b, ffn_down=ffn_down, ln2_g=ln2_g, ln2_b=ln2_b)
    m = dict(c_ctx=m_c_ctx, ada_w=m_ada_w, ada_b=m_ada_b, w_in=m_w_in, gla_gate_w=m_gla_gate_w, gla_gate_b=m_gla_gate_b, gla_norm_g=m_gla_norm_g, ret_decay=m_ret_decay, mla_q_norm_g=m_mla_q_norm_g, mla_kv_norm_g=m_mla_kv_norm_g, mla_w_uq=m_mla_w_uq, mla_w_uk=m_mla_w_uk, mla_w_uv=m_mla_w_uv, w_out=m_w_out, ln1_g=m_ln1_g, ln1_b=m_ln1_b, ffn_up=m_ffn_up, ffn_conv_w=m_ffn_conv_w, ffn_conv_b=m_ffn_conv_b, ffn_down=m_ffn_down, ln2_g=m_ln2_g, ln2_b=m_ln2_b)
    v = dict(c_ctx=v_c_ctx, ada_w=v_ada_w, ada_b=v_ada_b, w_in=v_w_in, gla_gate_w=v_gla_gate_w, gla_gate_b=v_gla_gate_b, gla_norm_g=v_gla_norm_g, ret_decay=v_ret_decay, mla_q_norm_g=v_mla_q_norm_g, mla_kv_norm_g=v_mla_kv_norm_g, mla_w_uq=v_mla_w_uq, mla_w_uk=v_mla_w_uk, mla_w_uv=v_mla_w_uv, w_out=v_w_out, ln1_g=v_ln1_g, ln1_b=v_ln1_b, ffn_up=v_ffn_up, ffn_conv_w=v_ffn_conv_w, ffn_conv_b=v_ffn_conv_b, ffn_down=v_ffn_down, ln2_g=v_ln2_g, ln2_b=v_ln2_b)
    xi, yi, ci = _place()
    chip = 2 * xi + yi

    whole = {n: w[n] for n in WEIGHTS if n not in GATHERED and n != "c_ctx"}
    whole["ffn_conv_w"] = gather_conv_taps(ffn_conv_w, ci)
    first_big = layer_assemble(all_gather8(layer_blocks(w, 0, ci), "gather_layer0"))
    loss, grad_x, gw, dmod, s8, lws, reduced = local_step(
        x, c, ctx, c_ctx, whole, loss_target, first_big,
        lambda l: layer_blocks(w, l, ci), lambda l, got: layer_assemble(got),
        (lambda g: reduce_begin(g, ci), reduce_end))
    loss = lax.psum(loss, ("x", "y", "c"))

    dsil = jnp.zeros((8, D_MODEL), F32)
    for i in range(DEPTH):
        dsil = dsil + matmul(dmod[i], lws[i]["ada_w"], "nt", F32, 8, 1024, 1536, "ada_dsilu")

    grads = reduce_assemble(reduced, ci)

    small = {n: gw[n] for n in SMALL if n != "ada_b"}
    small.update(dsil=dsil[4])
    names = tuple(small)
    conv_g = gw["ffn_conv_w"].reshape(DEPTH * 3, 2 * D_FF)
    ev_small, ev_dmod, ev_s8, ev_conv = all_gather8(
        [_pack_small(small, names), dmod.reshape(DEPTH * 8, 6 * D_MODEL), s8, conv_g], "gather_small")
    sm_small, sm_dmod, sm_conv = sum_small([ev_small, ev_dmod, ev_conv], "sum_small")
    summed = _unpack_small(sm_small, small, names)
    for n in SMALL:
        if n != "ada_b":
            grads[n] = summed[n]
    grads["ada_b"] = jnp.sum(sm_dmod.reshape(DEPTH, 8, 6 * D_MODEL)[:, :5], axis=1)
    sg = jax.nn.sigmoid(c_ctx)
    grads["c_ctx"] = summed["dsil"] * (sg * (1.0 + c_ctx * (1.0 - sg)))
    ccols = ffn_conv_w.shape[2]
    grads["ffn_conv_w"] = lax.dynamic_slice_in_dim(sm_conv.reshape(DEPTH, 3, 2 * D_FF), chip * ccols, ccols, axis=2)
    s_all = ev_s8.reshape(64, D_MODEL)
    d_all = jnp.transpose(ev_dmod.reshape(8, DEPTH, 8, 6 * D_MODEL), (1, 0, 2, 3)).reshape(DEPTH, 64, 6 * D_MODEL)
    cols = ada_w.shape[2]
    g_ada = []
    for i in range(DEPTH):
        d_mine = lax.dynamic_slice_in_dim(d_all[i], chip * cols, cols, axis=1)
        g_ada.append(matmul(s_all, d_mine, "tn", F32, 1024, cols, 64, "ada_dw"))
    grads["ada_w"] = jnp.stack(g_ada)

    delta, new_m, new_v = {}, {}, {}
    for n in GATHERED:
        shp = w[n].shape
        v2 = lambda a: a.reshape(-1, shp[-1])
        d_, m_, v_ = adamw(v2(w[n]), v2(grads[n]), v2(m[n]), v2(v[n]), "adamw_" + n)
        delta[n], new_m[n], new_v[n] = d_.reshape(shp), m_.reshape(shp), v_.reshape(shp)
    rep = tuple(n for n in WEIGHTS if n not in GATHERED)
    pk = lambda d: _pack_small({n: d[n] for n in rep}, rep)
    d_, m_, v_ = adamw(pk(w), pk(grads), pk(m), pk(v), "adamw_small")
    like = {n: w[n] for n in rep}
    delta.update(_unpack_small(d_, like, rep))
    new_m.update(_unpack_small(m_, like, rep))
    new_v.update(_unpack_small(v_, like, rep))
    grads = {n: grads[n].reshape(w[n].shape) for n in WEIGHTS}
    return (loss, grad_x, *[grads[n] for n in WEIGHTS], *[delta[n] for n in WEIGHTS], *[new_m[n] for n in WEIGHTS],
            *[new_v[n] for n in WEIGHTS])
```

```python
import functools

import jax
import jax.numpy as jnp
from jax import lax
from jax.experimental import pallas as pl
from jax.experimental.pallas import tpu as pltpu

F32 = jnp.float32
BF = jnp.bfloat16

D_MODEL = 1024
DEPTH = 4
GRID_W = 64
GLA_DK = 32
GLA_TAU = 16.0
RET_DK = 32
MLA_HEADS = 8
MLA_D_NOPE = 64
MLA_D_ROPE = 32
MLA_SCALE = (MLA_D_NOPE + MLA_D_ROPE) ** -0.5
D_FF = 2816
ROPE_BASE = 10000.0
EPS = 1e-6
ALPHA = (2 * DEPTH) ** 0.25
ADAM_LR, ADAM_B1, ADAM_B2, ADAM_EPS, ADAM_WD, ADAM_STEP = 0.001, 0.9, 0.999, 1e-08, 0.01, 10

ROW_TILE = 256
CHUNK = 64
GATE_ROWS = 128
LANES = 128

C_GQ, C_GK, C_GV, C_GG, C_RQ, C_RK, C_RV, C_RG, C_CQ, C_CKV, C_LK = 0, 128, 256, 512, 768, 896, 1024, 1280, 1536, 1792, 1920
D_INP = 2048
D_IN = 1984


def _dg(a, b, ca, cb):
    return lax.dot_general(a.astype(BF), b.astype(BF), (((ca,), (cb,)), ((), ())), preferred_element_type=F32)


@jax.custom_vjp
def mm_nn(a, b):
    return _dg(a, b, 1, 0)


@jax.custom_vjp
def mm_nt(a, b):
    return _dg(a, b, 1, 1)


@jax.custom_vjp
def mm_tn(a, b):
    return _dg(a, b, 0, 0)


mm_nn.defvjp(lambda a, b: (_dg(a, b, 1, 0), (a, b)),
             lambda r, g: (mm_nt(g, r[1]).astype(r[0].dtype), mm_tn(r[0], g).astype(r[1].dtype)))
mm_nt.defvjp(lambda a, b: (_dg(a, b, 1, 1), (a, b)),
             lambda r, g: (mm_nn(g, r[1]).astype(r[0].dtype), mm_tn(g, r[0]).astype(r[1].dtype)))
mm_tn.defvjp(lambda a, b: (_dg(a, b, 0, 0), (a, b)),
             lambda r, g: (mm_nt(r[1], g).astype(r[0].dtype), mm_nn(r[0], g).astype(r[1].dtype)))


def _split3(x):
    h = x.astype(BF)
    r = x - h.astype(F32)
    m = r.astype(BF)
    lo = (r - m.astype(F32)).astype(BF)
    return h, m, lo


def _exact(x, mat, left):
    h, m, lo = _split3(x)
    if left:
        d = lambda t: lax.dot_general(mat, t, (((1,), (0,)), ((), ())), preferred_element_type=F32)
    else:
        d = lambda t: lax.dot_general(t, mat, (((1,), (0,)), ((), ())), preferred_element_type=F32)
    return (d(lo) + d(m)) + d(h)


def _iota(shape, axis):
    return lax.broadcasted_iota(jnp.int32, shape, axis)


def _tri(n, upper):
    r, c = _iota((n, n), 0), _iota((n, n), 1)
    return jnp.where((c >= r) if upper else (r >= c), 1.0, 0.0).astype(BF)


@functools.partial(jax.custom_vjp, nondiff_argnums=(1,))
def cumsum_rows(x, upper):
    return _exact(x, _tri(x.shape[0], upper), True)


cumsum_rows.defvjp(lambda x, upper: (cumsum_rows(x, upper), None),
                   lambda upper, r, g: (cumsum_rows(g, not upper),))


def _seg(n, w):
    shift = w.bit_length() - 1
    r, c = _iota((n, n), 0), _iota((n, n), 1)
    return jnp.where(lax.shift_right_logical(r, shift) == lax.shift_right_logical(c, shift), 1.0, 0.0).astype(BF)


@functools.partial(jax.custom_vjp, nondiff_argnums=(1,))
def seg_sum(x, w):
    return _exact(x, _seg(x.shape[1], w), False)


seg_sum.defvjp(lambda x, w: (seg_sum(x, w), None), lambda w, r, g: (seg_sum(g, w),))


def _place_mat(transpose):
    shape = (8 * LANES, LANES) if transpose else (LANES, 8 * LANES)
    r, c = _iota(shape, 0), _iota(shape, 1)
    src, dst = (c, r) if transpose else (r, c)
    dl = jnp.bitwise_and(dst, LANES - 1)
    ok = (dl >= 64) & (dl < 96) & (src == dl - 32)
    return jnp.where(ok, 1.0, 0.0).astype(BF)


@jax.custom_vjp
def place_kr(x):
    return _exact(x, _place_mat(False), False)


place_kr.defvjp(lambda x: (place_kr(x), None), lambda r, g: (_exact(g, _place_mat(True), False),))


@functools.partial(jax.custom_vjp, nondiff_argnums=(1,))
def lane_roll(x, s):
    return pltpu.roll(x, s, 1)


lane_roll.defvjp(lambda x, s: (pltpu.roll(x, s, 1), None),
                 lambda s, r, g: (pltpu.roll(g, (g.shape[1] - s) % g.shape[1], 1),))


def rope(x, tab, d):
    cos, sa, sb = tab
    return x * cos + lane_roll(x, LANES - d) * sa + lane_roll(x, d) * sb


def silu(x):
    return x * jax.nn.sigmoid(x)


def log_sigmoid(z):
    return jnp.minimum(z, 0.0) - jnp.log(1.0 + jnp.exp(-jnp.abs(z)))


def layer_norm(x, g, b):
    mu = jnp.mean(x, axis=-1, keepdims=True)
    xc = x - mu
    var = jnp.mean(xc * xc, axis=-1, keepdims=True)
    return xc * lax.rsqrt(var + EPS) * g + b


def matmul(a, b, mode, out_dtype, tm, tn, tk, name, b_outer=False):
    ij = (lambda f: (lambda g0, g1, kk: f(g1, g0, kk))) if b_outer else (lambda f: f)
    if mode == "nn":
        (m, k), (k2, n) = a.shape, b.shape
        a_spec = pl.BlockSpec((tm, tk), ij(lambda i, j, kk: (i, kk)))
        b_spec = pl.BlockSpec((tk, tn), ij(lambda i, j, kk: (kk, j)))
        ca, cb = 1, 0
    elif mode == "nt":
        (m, k), (n, k2) = a.shape, b.shape
        a_spec = pl.BlockSpec((tm, tk), ij(lambda i, j, kk: (i, kk)))
        b_spec = pl.BlockSpec((tn, tk), ij(lambda i, j, kk: (j, kk)))
        ca, cb = 1, 1
    else:
        (k, m), (k2, n) = a.shape, b.shape
        a_spec = pl.BlockSpec((tk, tm), ij(lambda i, j, kk: (kk, i)))
        b_spec = pl.BlockSpec((tk, tn), ij(lambda i, j, kk: (kk, j)))
        ca, cb = 0, 0
    assert k == k2 and m % tm == 0 and n % tn == 0 and k % tk == 0, (name, a.shape, b.shape, tm, tn, tk)
    nk = k // tk
    grid = (n // tn, m // tm, nk) if b_outer else (m // tm, n // tn, nk)

    def body(a_ref, b_ref, o_ref, *acc):
        part = _dg(a_ref[...], b_ref[...], ca, cb)
        if nk == 1:
            o_ref[...] = part.astype(o_ref.dtype)
            return
        acc_ref, = acc
        kk = pl.program_id(2)

        @pl.when(kk == 0)
        def _():
            acc_ref[...] = part

        @pl.when(kk > 0)
        def _():
            acc_ref[...] += part

        @pl.when(kk == nk - 1)
        def _():
            o_ref[...] = acc_ref[...].astype(o_ref.dtype)

    return pl.pallas_call(
        body, name=name, grid=grid,
        in_specs=[a_spec, b_spec], out_specs=pl.BlockSpec((tm, tn), ij(lambda i, j, kk: (i, j))),
        out_shape=jax.ShapeDtypeStruct((m, n), out_dtype),
        scratch_shapes=[] if nk == 1 else [pltpu.VMEM((tm, tn), F32)],
    )(a, b)


def _stage_specs(rows, tps, consts, ws):
    specs, args = [], []
    for arr, width, cb in rows:
        specs.append(pl.BlockSpec((ROW_TILE, width), functools.partial(lambda i, cb: (i, cb), cb=cb)))
        args.append(arr)
    for arr in tps:
        specs.append(pl.BlockSpec((1, 1, arr.shape[2]), lambda i: (i, 0, 0)))
        args.append(arr)
    for arr, period in consts:
        specs.append(pl.BlockSpec((ROW_TILE, arr.shape[1]), functools.partial(lambda i, p: (i % p, 0), p=period)))
        args.append(arr)
    for arr in ws:
        specs.append(pl.BlockSpec(arr.shape, functools.partial(lambda i, nd: (0,) * nd, nd=arr.ndim)))
        args.append(arr)
    return specs, args


def _stage_load(refs, n_rows, n_tps, n_consts, n_ws):
    it = iter(refs)
    rows = [next(it)[...].astype(F32) for _ in range(n_rows)]
    tps = [next(it)[0].astype(F32) for _ in range(n_tps)]
    consts = [next(it)[...].astype(F32) for _ in range(n_consts)]
    ws = [next(it)[...].astype(F32) for _ in range(n_ws)]
    return rows, tps, consts, ws


def stage_fwd(fn, rows, tps, consts, ws, outs, name):
    n_tiles = rows[0][0].shape[0] // ROW_TILE
    specs, args = _stage_specs(rows, tps, consts, ws)
    counts = (len(rows), len(tps), len(consts), len(ws))

    def body(*refs):
        r, t, c, w = _stage_load(refs[:sum(counts)], *counts)
        res = fn(r, t, c, w)
        for o_ref, o in zip(refs[sum(counts):], res):
            o_ref[...] = o.astype(o_ref.dtype)

    res = pl.pallas_call(
        body, name=name, grid=(n_tiles,), in_specs=specs,
        out_specs=[pl.BlockSpec((ROW_TILE, wd), lambda i: (i, 0)) for wd, _ in outs],
        out_shape=[jax.ShapeDtypeStruct((n_tiles * ROW_TILE, wd), dt) for wd, dt in outs],
    )(*args)
    return list(res)


def stage_bwd(fn, rows, tps, consts, ws, cts, row_grads, name):
    n_tiles = rows[0][0].shape[0] // ROW_TILE
    specs, args = _stage_specs(rows, tps, consts, ws)
    counts = (len(rows), len(tps), len(consts), len(ws))
    n_in = sum(counts)
    for ct in cts:
        specs.append(pl.BlockSpec((ROW_TILE, ct.shape[1]), lambda i: (i, 0)))
        args.append(ct)
    want = [k for k, dt in enumerate(row_grads) if dt is not None]
    out_specs = [pl.BlockSpec((ROW_TILE, rows[k][1]), lambda i: (i, 0)) for k in want]
    out_shape = [jax.ShapeDtypeStruct((n_tiles * ROW_TILE, rows[k][1]), row_grads[k]) for k in want]
    out_specs += [pl.BlockSpec((1, 1, a.shape[2]), lambda i: (i, 0, 0)) for a in tps]
    out_shape += [jax.ShapeDtypeStruct((n_tiles, 1, a.shape[2]), F32) for a in tps]
    out_specs += [pl.BlockSpec(a.shape, functools.partial(lambda i, nd: (0,) * nd, nd=a.ndim)) for a in ws]
    out_shape += [jax.ShapeDtypeStruct(a.shape, F32) for a in ws]

    def body(*refs):
        r, t, c, w = _stage_load(refs[:n_in], *counts)
        g = [ref[...].astype(F32) for ref in refs[n_in:n_in + len(cts)]]
        _, vjp = jax.vjp(lambda r_, t_, w_: fn(r_, t_, c, w_), r, t, w)
        dr, dt, dw = vjp(g)
        o = iter(refs[n_in + len(cts):])
        for k in want:
            ref = next(o)
            ref[...] = dr[k].astype(ref.dtype)
        for v in dt:
            next(o)[0] = v
        first = pl.program_id(0) == 0
        for v in dw:
            ref = next(o)

            @pl.when(first)
            def _():
                ref[...] = v

            @pl.when(jnp.logical_not(first))
            def _():
                ref[...] += v

    res = pl.pallas_call(body, name=name, grid=(n_tiles,), in_specs=specs, out_specs=out_specs, out_shape=out_shape)(*args)
    res = list(res)
    drows = [None] * len(rows)
    for k in want:
        drows[k] = res.pop(0)
    dtps = [res.pop(0) for _ in tps]
    dws = [res.pop(0) for _ in ws]
    return drows, dtps, dws


def fn_modulate(rows, tps, consts, ws):
    (x,), (sc, sh) = rows, tps
    return [x * (1.0 + sc) + sh]


def fn_post(rows, tps, consts, ws):
    (x, a), (g,), (lng, lnb) = rows, tps, ws
    return [layer_norm(ALPHA * x + g * a, lng, lnb)]


def fn_mix(rows, tps, consts, ws):
    ogf, ogb, orf, orb, pg, pr, mo = rows
    gng, = ws
    og = ogf + ogb
    out_g = og * lax.rsqrt(seg_sum(og * og, 64) * (1.0 / 64) + EPS) * gng * silu(pg)
    o = orf + orb
    oc = o - seg_sum(o, 64) * (1.0 / 64)
    out_r = oc * lax.rsqrt(seg_sum(oc * oc, 64) * (1.0 / 64) + EPS) * silu(pr)
    return [jnp.concatenate([out_g, out_r, mo], axis=-1)]


def fn_mla_prep(rows, tps, consts, ws):
    pq, pkv, plk = rows
    gq, gkv, wuq, wuk, wuv = ws
    qtab, ktab = consts[0:3], consts[3:6]
    cq = pq * lax.rsqrt(jnp.mean(pq * pq, axis=-1, keepdims=True) + EPS) * gq
    qp = mm_nn(cq, wuq)
    q = jnp.concatenate([rope(qp[:, h * LANES:(h + 1) * LANES], qtab, 8) * MLA_SCALE for h in range(MLA_HEADS)], axis=-1)
    ckv = pkv * lax.rsqrt(jnp.mean(pkv * pkv, axis=-1, keepdims=True) + EPS) * gkv
    k = mm_nn(ckv, wuk) + place_kr(rope(plk, ktab, 8))
    v = mm_nn(ckv, wuv)
    return [q, k, v]


def fn_assemble(rows, tps, consts, ws):
    gq, gk, gv, gg, rq, rk, rv, rg, cq, ckv, lk1, lk2 = rows
    return [jnp.concatenate([gq, gk, gv, gg, rq, rk, rv, rg, cq, ckv, lk1 + lk2], axis=-1)]


def _head_masks():
    hm = (lax.shift_right_logical(_iota((4, 1, LANES), 2), 5) == _iota((4, 1, LANES), 0)).astype(F32)
    vm = (lax.shift_right_logical(_iota((4, 1, 256), 2), 6) == _iota((4, 1, 256), 0)).astype(F32)
    bd = (lax.shift_right_logical(_iota((256, LANES), 0), 6) == lax.shift_right_logical(_iota((256, LANES), 1), 5)).astype(F32)
    return hm, vm, bd


def chunk_step(s, q, k, v, la, upper):
    hm, vm, bd = _head_masks()
    t, u = _iota((4 * CHUNK, CHUNK), 0), _iota((4 * CHUNK, CHUNK), 1)
    t = jnp.bitwise_and(t, CHUNK - 1)
    causal = (u >= t) if upper else (t >= u)
    b = cumsum_rows(la, upper)
    bend = jnp.sum(la, axis=0, keepdims=True)
    half = 0.5 * bend
    qd = q * jnp.exp(b - half)
    kd = k * jnp.exp(half - b)
    qe = (qd[None] * hm).reshape(4 * CHUNK, LANES)
    att = jnp.where(causal, mm_nt(qe, kd), 0.0)
    o_intra = (mm_nn(att, v).reshape(4, CHUNK, 256) * vm).sum(0)
    o = mm_nt(q * jnp.exp(b), s) + o_intra
    s_new = (s * jnp.exp(bend) + mm_tn(v, k * jnp.exp(bend - b))) * bd
    return o, s_new


def scan_step(sg, sr, q, k, v, lrk, rq, rk, rv, gw, gb, rdec, tab, upper):
    la_g = log_sigmoid(mm_nn(lrk, gw) + gb) * (1.0 / GLA_TAU)
    og, sg2 = chunk_step(sg, q * GLA_DK ** -0.5, k, v, la_g, upper)
    la_r = jnp.broadcast_to(log_sigmoid(rdec), (CHUNK, LANES))
    orr, sr2 = chunk_step(sr, rope(rq, tab, 16), rope(rk * RET_DK ** -0.5, tab, 16), rv, la_r, upper)
    return og, orr, sg2, sr2


def _chunk_of(n, ncc, nch, reverse):
    if not reverse:
        return n
    return jnp.where(n < ncc, ncc - 1 - n, nch - 1 + ncc - n)


def _scan_in_specs(p3, tabs, gw, gb, rdec, cidx):
    nb = p3.shape[0]

    def blk(width, cb):
        return pl.BlockSpec((nb, CHUNK, width), lambda m: (0, cidx(m), cb))

    specs = [blk(128, C_GQ // 128), blk(128, C_GK // 128), blk(256, C_GV // 256), blk(128, C_LK // 128),
             blk(128, C_RQ // 128), blk(128, C_RK // 128), blk(256, C_RV // 256)]
    args = [p3] * 7
    for t in tabs:
        specs.append(pl.BlockSpec((CHUNK, LANES), lambda m: (cidx(m), 0)))
        args.append(t)
    for w in (gw, gb, rdec):
        specs.append(pl.BlockSpec(w.shape, lambda m: (0, 0)))
        args.append(w)
    return specs, args


def scan_fwd(p, tabs, gw, gb, rdec, nb, ncc, nch, reverse, name):
    cidx = lambda n: _chunk_of(n, ncc, nch, reverse)
    t = p.shape[0]
    specs, args = _scan_in_specs(p.reshape(nb, t // nb, p.shape[1]), tabs, gw, gb, rdec, cidx)

    def body(q, k, v, lrk, rq, rk, rv, tc, ta, tb, gw_r, gb_r, rd_r, og_r, or_r, sgo_r, sro_r, sg, sr):
        @pl.when(pl.program_id(0) == 0)
        def _():
            sg[...] = jnp.zeros_like(sg)
            sr[...] = jnp.zeros_like(sr)

        sgo_r[0] = sg[...]
        sro_r[0] = sr[...]
        ld = lambda r: r[...].astype(F32)
        tab, gw_, gb_, rd_ = (ld(tc), ld(ta), ld(tb)), ld(gw_r), ld(gb_r), ld(rd_r)
        for b in range(nb):
            lb = lambda r: r[b].astype(F32)
            og, orr, sg2, sr2 = scan_step(sg[b], sr[b], lb(q), lb(k), lb(v), lb(lrk), lb(rq), lb(rk), lb(rv),
                                          gw_, gb_, rd_, tab, reverse)
            og_r[b] = og
            or_r[b] = orr
            sg[b] = sg2
            sr[b] = sr2

    row_out = pl.BlockSpec((nb, CHUNK, 256), lambda n: (0, cidx(n), 0))
    st_out = pl.BlockSpec((1, nb, 256, LANES), lambda n: (n, 0, 0, 0))
    og, orr, sgs, srs = pl.pallas_call(
        body, name=name, grid=(nch,), in_specs=specs, out_specs=[row_out, row_out, st_out, st_out],
        out_shape=[jax.ShapeDtypeStruct((nb, t // nb, 256), F32)] * 2 + [jax.ShapeDtypeStruct((nch, nb, 256, LANES), F32)] * 2,
        scratch_shapes=[pltpu.VMEM((nb, 256, LANES), F32)] * 2,
    )(*args)
    return og.reshape(t, 256), orr.reshape(t, 256), sgs, srs


def scan_bwd(p, tabs, gw, gb, rdec, sg_in, sr_in, dog, dor, prev, nb, ncc, nch, reverse, name):
    step = lambda m: nch - 1 - m
    cidx = lambda m: _chunk_of(step(m), ncc, nch, reverse)
    t = p.shape[0]
    lt = t // nb
    specs, args = _scan_in_specs(p.reshape(nb, lt, p.shape[1]), tabs, gw, gb, rdec, cidx)
    st_spec = pl.BlockSpec((1, nb, 256, LANES), lambda m: (step(m), 0, 0, 0))
    specs += [st_spec, st_spec]
    args += [sg_in, sr_in]
    row = lambda width: pl.BlockSpec((nb, CHUNK, width), lambda m: (0, cidx(m), 0))
    specs += [row(256), row(256)]
    args += [dog.reshape(nb, lt, 256), dor.reshape(nb, lt, 256)]
    widths = (128, 128, 256, 128, 128, 128, 256)
    if prev is not None:
        specs += [row(wd) for wd in widths]
        args += [a.reshape(nb, lt, a.shape[1]) for a in prev]
    n_prev = 0 if prev is None else 7

    def body(*refs):
        (q, k, v, lrk, rq, rk, rv, tc, ta, tb, gw_r, gb_r, rd_r, sgi, sri, dog_r, dor_r), rest = refs[:17], refs[17:]
        prev_r, rest = rest[:n_prev], rest[n_prev:]
        outs, (dgw_r, dgb_r, drd_r, dsg, dsr) = rest[:7], rest[7:]
        first = pl.program_id(0) == 0

        @pl.when(first)
        def _():
            dsg[...] = jnp.zeros_like(dsg)
            dsr[...] = jnp.zeros_like(dsr)

        ld = lambda r: r[...].astype(F32)
        tab, gw_, gb_, rd_ = (ld(tc), ld(ta), ld(tb)), ld(gw_r), ld(gb_r), ld(rd_r)
        wsum = None
        for b in range(nb):
            lb = lambda r: r[b].astype(F32)
            prim = (sgi[0, b], sri[0, b], lb(q), lb(k), lb(v), lb(lrk), lb(rq), lb(rk), lb(rv), gw_, gb_, rd_)
            _, vjp = jax.vjp(lambda *a: scan_step(*a, tab, reverse), *prim)
            g = vjp((lb(dog_r), lb(dor_r), dsg[b], dsr[b]))
            dsg[b] = g[0]
            dsr[b] = g[1]
            for j in range(7):
                val = g[2 + j]
                if n_prev:
                    val = val + prev_r[j][b]
                outs[j][b] = val
            wsum = g[9:12] if wsum is None else tuple(a + c for a, c in zip(wsum, g[9:12]))
        for ref, val in zip((dgw_r, dgb_r, drd_r), wsum):
            @pl.when(first)
            def _():
                ref[...] = val

            @pl.when(jnp.logical_not(first))
            def _():
                ref[...] += val

    wspec = lambda w: pl.BlockSpec(w.shape, lambda m: (0, 0))
    res = pl.pallas_call(
        body, name=name, grid=(nch,), in_specs=specs,
        out_specs=[row(wd) for wd in widths] + [wspec(gw), wspec(gb), wspec(rdec)],
        out_shape=[jax.ShapeDtypeStruct((nb, lt, wd), F32) for wd in widths]
        + [jax.ShapeDtypeStruct(w.shape, F32) for w in (gw, gb, rdec)],
        scratch_shapes=[pltpu.VMEM((nb, 256, LANES), F32)] * 2,
    )(*args)
    return tuple(a.reshape(t, a.shape[2]) for a in res[:7]), res[7], res[8], res[9]


def _attn_tiles(lc, lt):
    nct = lc // ROW_TILE
    return nct, (lt - lc) // ROW_TILE


def _attn_loop(tile, lc, lt):
    nct, nlt = _attn_tiles(lc, lt)
    for i in range(nct):
        tile(i * ROW_TILE, lc)

    def lat(i, carry):
        tile(pl.multiple_of(lc + i * ROW_TILE, ROW_TILE), lt)
        return carry

    lax.fori_loop(0, nlt, lat, 0)


def mla_fwd(q, k, v, nb, lc, lt, name, gather=()):
    ng = len(gather)

    def body(q_ref, k_ref, v_ref, *rest):
        x_refs, (o_ref, lse_ref), out_refs, sems = rest[:ng], rest[ng:ng + 2], rest[ng + 2:2 * ng + 2], rest[2 * ng + 2:]
        if ng:
            start, finish = _gather8_steps(x_refs, out_refs, *sems)
            pl.when((pl.program_id(0) == 0) & (pl.program_id(1) == 0))(start)

        def tile(r0, nk):
            rows = pl.ds(r0, ROW_TILE)
            lane = _iota((ROW_TILE, LANES), 1)
            outs, lse = [], jnp.zeros((ROW_TILE, LANES), F32)
            for j in range(2):
                s = _dg(q_ref[rows, j * LANES:(j + 1) * LANES], k_ref[0:nk, j * LANES:(j + 1) * LANES], 1, 1)
                m = jnp.max(s, axis=-1, keepdims=True)
                p = jnp.exp(s - m)
                l = jnp.sum(p, axis=-1, keepdims=True)
                outs.append(_dg(p, v_ref[0:nk, j * 64:(j + 1) * 64], 1, 0) * (1.0 / l))
                lse = jnp.where(lane == j, m + jnp.log(l), lse)
            o_ref[rows, :] = jnp.concatenate(outs, axis=-1)
            lse_ref[rows, :] = lse

        _attn_loop(tile, lc, lt)
        if ng:
            pl.when((pl.program_id(0) == nb - 1) & (pl.program_id(1) == MLA_HEADS // 2 - 1))(finish)

    pair = lambda width: pl.BlockSpec((lt, width), lambda b, h: (b, h))
    res = pl.pallas_call(
        body, name=name, grid=(nb, MLA_HEADS // 2), in_specs=[pair(2 * LANES), pair(2 * LANES), pair(LANES)] + [ANY] * ng,
        out_specs=[pair(LANES), pair(LANES)] + [ANY] * ng,
        out_shape=[jax.ShapeDtypeStruct((nb * lt, MLA_HEADS * 64), F32), jax.ShapeDtypeStruct((nb * lt, MLA_HEADS // 2 * LANES), F32)]
        + [jax.ShapeDtypeStruct((8,) + g.shape, g.dtype) for g in gather],
        scratch_shapes=[_dma_sems(ng, 7), _dma_sems(ng, 7), _dma_sems(ng, 1)] if ng else [],
    )(q, k, v, *gather)
    return res[0], res[1], list(res[2:])


def mla_bwd(q, k, v, o, lse, do, nb, lc, lt, name, exchange=()):
    ne = len(exchange)

    def body(q_ref, k_ref, v_ref, o_ref, lse_ref, do_ref, *rest):
        p_refs, (dq_ref, dk_ref, dv_ref), rest = rest[:ne], rest[ne:ne + 3], rest[ne + 3:]
        out_refs, (dka, dva), sems = rest[:ne], rest[ne:ne + 2], rest[ne + 2:]
        if ne:
            start, finish = _exchange_steps(p_refs, out_refs, *sems)
            pl.when((pl.program_id(0) == 0) & (pl.program_id(1) == 0))(start)
        dka[...] = jnp.zeros_like(dka)
        dva[...] = jnp.zeros_like(dva)

        def tile(r0, nk):
            rows = pl.ds(r0, ROW_TILE)
            dqs = []
            for j in range(2):
                qj, kj = q_ref[rows, j * LANES:(j + 1) * LANES], k_ref[0:nk, j * LANES:(j + 1) * LANES]
                vj, doj = v_ref[0:nk, j * 64:(j + 1) * 64], do_ref[rows, j * 64:(j + 1) * 64]
                p = jnp.exp(_dg(qj, kj, 1, 1) - lse_ref[rows, j:j + 1])
                dsum = jnp.sum(doj * o_ref[rows, j * 64:(j + 1) * 64], axis=-1, keepdims=True)
                ds = p * (_dg(doj, vj, 1, 1) - dsum)
                dqs.append(_dg(ds, kj, 1, 0))
                dka[j, 0:nk, :] += _dg(ds, qj, 0, 0)
                dva[j, 0:nk, :] += _dg(p, doj, 0, 0)
            dq_ref[rows, :] = jnp.concatenate(dqs, axis=-1)

        _attn_loop(tile, lc, lt)
        dk_ref[...] = jnp.concatenate([dka[0], dka[1]], axis=-1)
        dv_ref[...] = jnp.concatenate([dva[0], dva[1]], axis=-1)
        if ne:
            pl.when((pl.program_id(0) == nb - 1) & (pl.program_id(1) == MLA_HEADS // 2 - 1))(finish)

    t = nb * lt
    pair = lambda width: pl.BlockSpec((lt, width), lambda b, h: (b, h))
    res = pl.pallas_call(
        body, name=name, grid=(nb, MLA_HEADS // 2),
        in_specs=[pair(2 * LANES), pair(2 * LANES), pair(LANES), pair(LANES), pair(LANES), pair(LANES)] + [ANY] * ne,
        out_specs=[pair(2 * LANES), pair(2 * LANES), pair(LANES)] + [ANY] * ne,
        out_shape=[jax.ShapeDtypeStruct((t, MLA_HEADS * LANES), F32), jax.ShapeDtypeStruct((t, MLA_HEADS * LANES), F32),
                   jax.ShapeDtypeStruct((t, MLA_HEADS * 64), F32)] + [jax.ShapeDtypeStruct(e.shape, e.dtype) for e in exchange],
        scratch_shapes=[pltpu.VMEM((2, lt, LANES), F32), pltpu.VMEM((2, lt, 64), F32)]
        + ([_dma_sems(ne, 3), _dma_sems(ne, 3), _dma_sems(ne, 1)] if ne else []),
    )(q, k, v, o, lse, do, *exchange)
    return res[0], res[1], res[2], list(res[3:])


HALO = 16


def _gate_specs(u, per_batch, lc):
    gh = GATE_ROWS // HALO
    nh = u.shape[0] // HALO
    width = u.shape[1]
    main = pl.BlockSpec((GATE_ROWS, width), lambda i: (i, 0))
    prev = pl.BlockSpec((HALO, width), lambda i: (jnp.maximum(i * gh - 1, 0), 0))
    nxt = pl.BlockSpec((HALO, width), lambda i: (jnp.minimum((i + 1) * gh, nh - 1), 0))
    return main, prev, nxt


def _seg_edges(per_batch, lc):
    j = pl.program_id(0) % (per_batch // GATE_ROWS)
    first = (j == 0) | (j == lc // GATE_ROWS)
    last = (j == lc // GATE_ROWS - 1) | (j == per_batch // GATE_ROWS - 1)
    return first, last


def _shifted(x, prev_ref, next_ref, first, last):
    rows = _iota(x.shape, 0)
    before = jnp.where(first, 0.0, prev_ref[HALO - 1:HALO, :].astype(F32))
    after = jnp.where(last, 0.0, next_ref[0:1, :].astype(F32))
    xm = jnp.where(rows == 0, before, pltpu.roll(x, 1, 0))
    xp = jnp.where(rows == x.shape[0] - 1, after, pltpu.roll(x, x.shape[0] - 1, 0))
    return xm, xp


def _whole(a):
    return pl.BlockSpec(a.shape, lambda i: (0,) * a.ndim)


def gate_fwd(u, cw, cb, wdown, per_batch, lc, name):
    main, prev, nxt = _gate_specs(u, per_batch, lc)
    f = u.shape[1] // 2

    def body(u_ref, p_ref, n_ref, w_ref, b_ref, wd_ref, act_ref, f_ref):
        first, last = _seg_edges(per_batch, lc)
        x = u_ref[...].astype(F32)
        xm, xp = _shifted(x, p_ref, n_ref, first, last)
        c = w_ref[0:1, :] * xm + w_ref[1:2, :] * x + w_ref[2:3, :] * xp + b_ref[...]
        act = silu(c[:, :f]) * c[:, f:]
        act_ref[...] = act.astype(act_ref.dtype)
        f_ref[...] = _dg(act, wd_ref[...], 1, 0)

    return pl.pallas_call(
        body, name=name, grid=(u.shape[0] // GATE_ROWS,),
        in_specs=[main, prev, nxt, _whole(cw), _whole(cb), _whole(wdown)],
        out_specs=[pl.BlockSpec((GATE_ROWS, f), lambda i: (i, 0)), pl.BlockSpec((GATE_ROWS, wdown.shape[1]), lambda i: (i, 0))],
        out_shape=[jax.ShapeDtypeStruct((u.shape[0], f), BF), jax.ShapeDtypeStruct((u.shape[0], wdown.shape[1]), F32)],
    )(u, u, u, cw, cb, wdown)


def gate_bwd(u, cw, cb, df, wdown, per_batch, lc, name):
    main, prev, nxt = _gate_specs(u, per_batch, lc)
    f = u.shape[1] // 2

    def body(u_ref, p_ref, n_ref, w_ref, b_ref, df_ref, wd_ref, dc_ref, dw_ref):
        first, last = _seg_edges(per_batch, lc)
        x = u_ref[...].astype(F32)
        xm, xp = _shifted(x, p_ref, n_ref, first, last)
        c = w_ref[0:1, :] * xm + w_ref[1:2, :] * x + w_ref[2:3, :] * xp + b_ref[...]
        a, g = c[:, :f], c[:, f:]
        sg = jax.nn.sigmoid(a)
        da = _dg(df_ref[...], wd_ref[...], 1, 1)
        dc = jnp.concatenate([da * g * (sg * (1.0 + a * (1.0 - sg))), da * (a * sg)], axis=-1)
        dc_ref[...] = dc.astype(dc_ref.dtype)
        part = jnp.concatenate([jnp.sum(xm * dc, axis=0, keepdims=True), jnp.sum(x * dc, axis=0, keepdims=True),
                                jnp.sum(xp * dc, axis=0, keepdims=True), jnp.sum(dc, axis=0, keepdims=True),
                                jnp.zeros((4, 2 * f), F32)], axis=0)

        @pl.when(pl.program_id(0) == 0)
        def _():
            dw_ref[...] = part

        @pl.when(pl.program_id(0) > 0)
        def _():
            dw_ref[...] += part

    return pl.pallas_call(
        body, name=name, grid=(u.shape[0] // GATE_ROWS,),
        in_specs=[main, prev, nxt, _whole(cw), _whole(cb), pl.BlockSpec((GATE_ROWS, df.shape[1]), lambda i: (i, 0)), _whole(wdown)],
        out_specs=[main, pl.BlockSpec((8, 2 * f), lambda i: (0, 0))],
        out_shape=[jax.ShapeDtypeStruct(u.shape, BF), jax.ShapeDtypeStruct((8, 2 * f), F32)],
    )(u, u, u, cw, cb, df, wdown)


def conv_transpose(dc, cw, wup, per_batch, lc, name):
    main, prev, nxt = _gate_specs(dc, per_batch, lc)

    def body(d_ref, p_ref, n_ref, w_ref, wu_ref, du_ref, dh_ref):
        first, last = _seg_edges(per_batch, lc)
        x = d_ref[...].astype(F32)
        xm, xp = _shifted(x, p_ref, n_ref, first, last)
        du = w_ref[0:1, :] * xp + w_ref[1:2, :] * x + w_ref[2:3, :] * xm
        du_ref[...] = du.astype(du_ref.dtype)
        dh_ref[...] = _dg(du, wu_ref[...], 1, 1)

    return pl.pallas_call(
        body, name=name, grid=(dc.shape[0] // GATE_ROWS,),
        in_specs=[main, prev, nxt, _whole(cw), _whole(wup)],
        out_specs=[main, pl.BlockSpec((GATE_ROWS, wup.shape[0]), lambda i: (i, 0))],
        out_shape=[jax.ShapeDtypeStruct(dc.shape, BF), jax.ShapeDtypeStruct((dc.shape[0], wup.shape[0]), F32)],
    )(dc, dc, dc, cw, wup)


def loss_head(x, target, tiles_per_batch, ctx_tiles, name):
    n_tiles = x.shape[0] // ROW_TILE
    lat_tiles = tiles_per_batch - ctx_tiles

    def tgt_idx(i):
        j = i % tiles_per_batch
        return jnp.where(j < ctx_tiles, 0, (i // tiles_per_batch) * lat_tiles + j - ctx_tiles), 0

    def body(x_ref, t_ref, dx_ref, l_ref):
        lat = (pl.program_id(0) % tiles_per_batch >= ctx_tiles).astype(F32)
        err = (x_ref[...] - t_ref[...]) * lat
        dx_ref[...] = err * (1.0 / D_MODEL)
        l_ref[...] = jnp.full(l_ref.shape, 0.5 / D_MODEL * jnp.sum(err * err), F32)

    return pl.pallas_call(
        body, name=name, grid=(n_tiles,),
        in_specs=[pl.BlockSpec((ROW_TILE, D_MODEL), lambda i: (i, 0)), pl.BlockSpec((ROW_TILE, D_MODEL), tgt_idx)],
        out_specs=[pl.BlockSpec((ROW_TILE, D_MODEL), lambda i: (i, 0)), pl.BlockSpec((1, 8, LANES), lambda i: (i, 0, 0))],
        out_shape=[jax.ShapeDtypeStruct(x.shape, F32), jax.ShapeDtypeStruct((n_tiles, 8, LANES), F32)],
    )(x, target)


def adamw(w, g, m, v, name):
    rows, cols = w.shape
    tr = rows
    for cand in (512, 256, 128, 64, 32, 16, 8):
        if rows % cand == 0 and cand * cols * 4 <= (1 << 20):
            tr = cand
            break

    def body(w_ref, g_ref, m_ref, v_ref, d_ref, mo_ref, vo_ref):
        gg = g_ref[...]
        m2 = ADAM_B1 * m_ref[...] + (1.0 - ADAM_B1) * gg
        v2 = ADAM_B2 * v_ref[...] + (1.0 - ADAM_B2) * (gg * gg)
        m_hat = m2 / (1.0 - ADAM_B1 ** ADAM_STEP)
        v_hat = v2 / (1.0 - ADAM_B2 ** ADAM_STEP)
        d_ref[...] = -ADAM_LR * (m_hat / (jnp.sqrt(v_hat) + ADAM_EPS) + ADAM_WD * w_ref[...])
        mo_ref[...] = m2
        vo_ref[...] = v2

    spec = pl.BlockSpec((tr, cols), lambda i: (i, 0))
    return pl.pallas_call(body, name=name, grid=(rows // tr,), in_specs=[spec] * 4, out_specs=[spec] * 3,
                          out_shape=[jax.ShapeDtypeStruct(w.shape, F32)] * 3)(w, g, m, v)


def fn_post_mod(rows, tps, consts, ws):
    (x, a), (g, sc, sh), (lng, lnb) = rows, tps, ws
    y = layer_norm(ALPHA * x + g * a, lng, lnb)
    return [y, y * (1.0 + sc) + sh]


def _pick(n, cands):
    for c in cands:
        if n % c == 0:
            return c
    return n


def rope_tables(lc, l):
    pos = jnp.arange(l, dtype=F32)
    ret_inv = 1.0 / (ROPE_BASE ** jnp.linspace(0.0, 1.0, RET_DK // 2, dtype=F32))
    ang = pos[:, None] * ret_inv
    rc, rs = jnp.cos(ang), jnp.sin(ang)
    n_ax = MLA_D_ROPE // 4
    ax_inv = ROPE_BASE ** (-jnp.arange(n_ax, dtype=F32) / n_ax)
    rows_n = l // GRID_W
    rows = jnp.repeat(jnp.arange(rows_n, dtype=F32), GRID_W)
    cols = jnp.tile(jnp.arange(GRID_W, dtype=F32), rows_n)
    ra, ca = rows[:, None] * ax_inv, cols[:, None] * ax_inv
    rwc, rws, clc, cls = jnp.cos(ra), jnp.sin(ra), jnp.cos(ca), jnp.sin(ca)
    one = lambda n: jnp.ones((l, n), F32)
    zero = lambda n: jnp.zeros((l, n), F32)
    cat = lambda parts: jnp.concatenate(parts, axis=1)

    def with_ctx(tab, is_cos):
        head = jnp.ones((lc, LANES), F32) if is_cos else jnp.zeros((lc, LANES), F32)
        return jnp.concatenate([head, tab], axis=0)

    ret = (cat([rc, rc] * 4), cat([-rs, zero(16)] * 4), cat([zero(16), rs] * 4))
    ax_c = [rwc, rwc, clc, clc]
    ax_a = [-rws, zero(8), -cls, zero(8)]
    ax_b = [zero(8), rws, zero(8), cls]
    qt = (cat([one(64)] + ax_c + [one(32)]), cat([zero(64)] + ax_a + [zero(32)]), cat([zero(64)] + ax_b + [zero(32)]))
    kt = (cat([one(32)] + ax_c + [one(64)]), cat([zero(32)] + ax_a + [zero(64)]), cat([zero(32)] + ax_b + [zero(64)]))
    fix = lambda t3: tuple(with_ctx(t, k == 0) for k, t in enumerate(t3))
    return fix(ret), fix(qt), fix(kt)


_IN_ORDER = ((0, 128), (128, 256), (256, 512), (544, 800), (800, 928), (928, 1056), (1056, 1312), (1312, 1568),
             (1568, 1824), (1824, 1952), (512, 544), (1952, 1984))


def permute_w_in(w):
    parts = [w[:, a:b] for a, b in _IN_ORDER] + [jnp.zeros((w.shape[0], D_INP - D_IN), w.dtype)]
    return jnp.concatenate(parts, axis=1)


def unpermute_w_in(g):
    out, at = {}, 0
    for a, b in _IN_ORDER:
        out[a] = g[:, at:at + b - a]
        at += b - a
    return jnp.concatenate([out[a] for a in sorted(out)], axis=1)


def layer_weights(big, w, l):
    f = lambda a: a.astype(F32)
    r = {}
    r["ada_w"] = big["ada_w"].astype(BF)
    r["win"] = permute_w_in(big["w_in"]).astype(BF)
    r["wout"] = big["w_out"].astype(BF)
    r["wup"] = big["ffn_up"].astype(BF)
    r["wdown"] = big["ffn_down"].astype(BF)
    uq = big["mla_w_uq"].reshape(256, MLA_HEADS, 96)
    r["wuq"] = jnp.pad(uq, ((0, 0), (0, 0), (0, 32))).reshape(256, 8 * LANES).astype(BF)
    uk = big["mla_w_uk"].reshape(128, MLA_HEADS, 64)
    r["wuk"] = jnp.pad(uk, ((0, 0), (0, 0), (0, 64))).reshape(128, 8 * LANES).astype(BF)
    r["wuv"] = big["mla_w_uv"].astype(BF)
    gw = f(w["gla_gate_w"][l])
    z16 = jnp.zeros((16, LANES), F32)
    z96 = jnp.zeros((96, LANES), F32)
    r["gw"] = (jnp.concatenate([gw[0], z16, z96], axis=0), jnp.concatenate([z16, gw[1], z96], axis=0))
    r["gb"] = tuple(f(w["gla_gate_b"][l][d]).reshape(1, LANES) for d in range(2))
    r["rdec"] = tuple(jnp.repeat(f(w["ret_decay"][l][d]), 32).reshape(1, LANES) for d in range(2))
    r["gng"] = jnp.tile(f(w["gla_norm_g"][l]), 4).reshape(1, 256)
    r["gq"] = f(w["mla_q_norm_g"][l]).reshape(1, 256)
    r["gkv"] = f(w["mla_kv_norm_g"][l]).reshape(1, 128)
    for n in ("ln1_g", "ln1_b", "ln2_g", "ln2_b"):
        r[n] = f(w[n][l]).reshape(1, D_MODEL)
    r["cw"] = f(w["ffn_conv_w"][l])
    r["cb"] = f(w["ffn_conv_b"][l]).reshape(1, 2 * D_FF)
    return r


def tile_params(mod_l, nb, nct, nlt):
    m6 = mod_l.reshape(8, 6, D_MODEL)
    out = []
    for j in range(6):
        parts = []
        for b in range(nb):
            parts.append(jnp.broadcast_to(m6[4, j], (nct, 1, D_MODEL)))
            parts.append(jnp.broadcast_to(m6[b, j], (nlt, 1, D_MODEL)))
        out.append(jnp.concatenate(parts, axis=0))
    return out


def tile_param_grads(dts, nb, nct, nlt):
    cols = []
    for dt in dts:
        d = dt.reshape(nb, nct + nlt, D_MODEL)
        lat = jnp.sum(d[:, nct:], axis=1)
        ctx = jnp.sum(d[:, :nct], axis=(0, 1))
        cols.append(jnp.concatenate([lat, jnp.zeros((4 - nb, D_MODEL), F32), ctx[None], jnp.zeros((3, D_MODEL), F32)], axis=0))
    return jnp.stack(cols, axis=1).reshape(8, 6 * D_MODEL)


def layer_forward(x, h1, tp, lw, tabs, dims, nxt):
    nb, lc, lt = dims
    t = x.shape[0]
    nbt = lt // ROW_TILE
    ncc, nch = lc // CHUNK, lt // CHUNK
    tm = _pick(t, (1024, 768, 512, 256))
    ret_tab, q_tab, k_tab = tabs
    full = lambda a: (a, a.shape[1], 0)
    p = matmul(h1, lw["win"], "nn", F32, tm, D_INP, 1024, "proj_in")
    ogf, orf, sgf, srf = scan_fwd(p, ret_tab, lw["gw"][0], lw["gb"][0], lw["rdec"][0], nb, ncc, nch, False, "scan_fwd_f")
    ogb, orb, sgb, srb = scan_fwd(p, ret_tab, lw["gw"][1], lw["gb"][1], lw["rdec"][1], nb, ncc, nch, True, "scan_fwd_b")
    prep_rows = [(p, 256, C_CQ // 256), (p, 128, C_CKV // 128), (p, 128, C_LK // 128)]
    prep_consts = [(a, nbt) for a in q_tab + k_tab]
    prep_ws = [lw["gq"], lw["gkv"], lw["wuq"], lw["wuk"], lw["wuv"]]
    q, k, v = stage_fwd(fn_mla_prep, prep_rows, [], prep_consts, prep_ws, [(1024, BF), (1024, BF), (512, BF)], "mla_prep")
    if nxt is None:
        mo, lse, _ = mla_fwd(q, k, v, nb, lc, lt, "mla_attn_last")
        made, tp_next = None, None
    else:
        mo, lse, got = mla_fwd(q, k, v, nb, lc, lt, "mla_attn", gather=nxt[0])
        made = nxt[1](got)
        tp_next = (made[0][1], made[0][0])
    mix_rows = [full(ogf), full(ogb), full(orf), full(orb), (p, 256, C_GG // 256), (p, 256, C_RG // 256), full(mo)]
    m, = stage_fwd(fn_mix, mix_rows, [], [], [lw["gng"]], [(1024, BF)], "mix")
    a = matmul(m, lw["wout"], "nn", F32, tm, 1024, 1024, "proj_out")
    x1, h2 = stage_fwd(fn_post_mod, [full(x), full(a)], [tp[2], tp[4], tp[3]], [], [lw["ln1_g"], lw["ln1_b"]],
                       [(1024, F32), (1024, BF)], "post1")
    u = matmul(h2, lw["wup"], "nn", BF, tm, 1408, 1024, "ffn_up", b_outer=True)
    act, f = gate_fwd(u, lw["cw"], lw["cb"], lw["wdown"], lt, lc, "ffn_gate_down")
    if tp_next is None:
        x2, = stage_fwd(fn_post, [full(x1), full(f)], [tp[5]], [], [lw["ln2_g"], lw["ln2_b"]], [(1024, F32)], "post2_last")
        h1n = None
    else:
        x2, h1n = stage_fwd(fn_post_mod, [full(x1), full(f)], [tp[5], tp_next[0], tp_next[1]], [],
                            [lw["ln2_g"], lw["ln2_b"]], [(1024, F32), (1024, BF)], "post2")
    res = dict(x=x, h1=h1, p=p, ogf=ogf, orf=orf, sgf=sgf, srf=srf, ogb=ogb, orb=orb, sgb=sgb, srb=srb, q=q, k=k, v=v,
               mo=mo, lse=lse, m=m, a=a, x1=x1, h2=h2, u=u, act=act, f=f, mix_rows=mix_rows, prep_rows=prep_rows,
               prep_consts=prep_consts, prep_ws=prep_ws)
    return x2, h1n, res, made


def layer_backward(dx2, dh1n, res, tp, tp_next, lw, tabs, dims, exchange=()):
    nb, lc, lt = dims
    r = res
    t = dx2.shape[0]
    ncc, nch = lc // CHUNK, lt // CHUNK
    tm = _pick(t, (1024, 768, 512, 256))
    tkr = _pick(t, (2304, 1536, 1024, 768, 512))
    ret_tab = tabs[0]
    full = lambda a: (a, a.shape[1], 0)
    g = {}
    if tp_next is None:
        (dx1a, df), (dg2,), (g["ln2_g"], g["ln2_b"]) = stage_bwd(
            fn_post, [full(r["x1"]), full(r["f"])], [tp[5]], [], [lw["ln2_g"], lw["ln2_b"]], [dx2], [F32, BF], "post2_last_bwd")
        dnext = None
    else:
        (dx1a, df), (dg2, dsc1n, dsh1n), (g["ln2_g"], g["ln2_b"]) = stage_bwd(
            fn_post_mod, [full(r["x1"]), full(r["f"])], [tp[5], tp_next[0], tp_next[1]], [], [lw["ln2_g"], lw["ln2_b"]],
            [dx2, dh1n], [F32, BF], "post2_bwd")
        dnext = (dsc1n, dsh1n)
    g["ffn_down"] = matmul(r["act"], df, "tn", F32, 1408, 1024, tkr, "ffn_down_dw")
    dc, dcw = gate_bwd(r["u"], lw["cw"], lw["cb"], df, lw["wdown"], lt, lc, "ffn_down_dx_gate_bwd")
    g["ffn_conv_w"], g["ffn_conv_b"] = dcw[0:3], dcw[3]
    du, dh2 = conv_transpose(dc, lw["cw"], lw["wup"], lt, lc, "ffn_conv_t_up_dx")
    g["ffn_up"] = matmul(r["h2"], du, "tn", F32, 1024, 1408, tkr, "ffn_up_dw")
    (dxa, da), (dg1, dsc2, dsh2), (g["ln1_g"], g["ln1_b"]) = stage_bwd(
        fn_post_mod, [full(r["x"]), full(r["a"])], [tp[2], tp[4], tp[3]], [], [lw["ln1_g"], lw["ln1_b"]],
        [dx1a, dh2], [F32, BF], "post1_bwd")
    dm = matmul(da, lw["wout"], "nt", F32, tm, 1024, 1024, "proj_out_dx")
    g["w_out"] = matmul(r["m"], da, "tn", F32, 1024, 1024, tkr, "proj_out_dw")
    (dog, _, dor, _, dpg, dpr, dmo), _, (dgng,) = stage_bwd(
        fn_mix, r["mix_rows"], [], [], [lw["gng"]], [dm], [F32, None, F32, None, F32, F32, F32], "mix_bwd")
    g["gla_norm_g"] = jnp.sum(dgng.reshape(4, 64), axis=0)
    dq, dk, dv, landed = mla_bwd(r["q"], r["k"], r["v"], r["mo"], r["lse"], dmo, nb, lc, lt,
                                 "mla_attn_bwd" if exchange else "mla_attn_bwd_first", exchange=exchange)
    (dpq, dpkv, dplk), _, (dgq, dgkv, dwuq, dwuk, dwuv) = stage_bwd(
        fn_mla_prep, r["prep_rows"], [], r["prep_consts"], r["prep_ws"], [dq, dk, dv], [F32, F32, F32], "mla_prep_bwd")
    g["mla_q_norm_g"], g["mla_kv_norm_g"] = dgq.reshape(256), dgkv.reshape(128)
    g["mla_w_uq"] = dwuq.reshape(256, MLA_HEADS, LANES)[:, :, :96].reshape(256, MLA_HEADS * 96)
    g["mla_w_uk"] = dwuk.reshape(128, MLA_HEADS, LANES)[:, :, :64].reshape(128, MLA_HEADS * 64)
    g["mla_w_uv"] = dwuv
    s7, dgw0, dgb0, drd0 = scan_bwd(r["p"], ret_tab, lw["gw"][0], lw["gb"][0], lw["rdec"][0], r["sgf"], r["srf"], dog, dor,
                                    None, nb, ncc, nch, False, "scan_bwd_f")
    s7, dgw1, dgb1, drd1 = scan_bwd(r["p"], ret_tab, lw["gw"][1], lw["gb"][1], lw["rdec"][1], r["sgb"], r["srb"], dog, dor,
                                    s7, nb, ncc, nch, True, "scan_bwd_b")
    g["gla_gate_w"] = jnp.stack([dgw0[0:16], dgw1[16:32]])
    g["gla_gate_b"] = jnp.stack([dgb0[0], dgb1[0]])
    g["ret_decay"] = jnp.stack([jnp.sum(drd0.reshape(4, 32), axis=1), jnp.sum(drd1.reshape(4, 32), axis=1)])
    gq_, gk_, gv_, glrk, rq_, rk_, rv_ = s7
    pieces = [gq_, gk_, gv_, dpg, rq_, rk_, rv_, dpr, dpq, dpkv, glrk, dplk]
    dp, = stage_fwd(fn_assemble, [full(a) for a in pieces], [], [], [], [(D_INP, BF)], "dproj_assemble")
    dh1 = matmul(dp, lw["win"], "nt", F32, tm, 1024, D_INP, "proj_in_dx")
    g["w_in"] = unpermute_w_in(matmul(r["h1"], dp, "tn", F32, 1024, 1024, tkr, "proj_in_dw"))
    for n in ("ln1_g", "ln1_b", "ln2_g", "ln2_b"):
        g[n] = g[n].reshape(D_MODEL)
    dtp = [None, None, dg1, dsh2, dsc2, dg2]
    return dxa, dh1, dtp, dnext, g, landed


def local_step(x, c, ctx, c_ctx, w, loss_target, first_big, next_blocks, assemble, reduce_hooks=None):
    nb, l, _ = x.shape
    lc = ctx.shape[1]
    lt = lc + l
    dims = (nb, lc, lt)
    nct, nlt = lc // ROW_TILE, l // ROW_TILE
    tabs = rope_tables(lc, l)
    x0 = jnp.concatenate([ctx, x], axis=1).reshape(nb * lt, D_MODEL)
    s8 = jnp.concatenate([silu(c), jnp.zeros((4 - nb, D_MODEL), F32), silu(c_ctx)[None], jnp.zeros((3, D_MODEL), F32)], axis=0)

    def make_layer(i, big):
        lw = layer_weights(big, w, i)
        mod = matmul(s8, lw["ada_w"], "nn", F32, 8, 1536, 1024, "ada_mod") + w["ada_b"][i].astype(F32)[None]
        return tile_params(mod, nb, nct, nlt), lw

    made = make_layer(0, first_big)
    lws, tps = [], []
    h1, = stage_fwd(fn_modulate, [(x0, D_MODEL, 0)], [made[0][1], made[0][0]], [], [], [(D_MODEL, BF)], "mod_in")
    xs, ress = x0, []
    for i in range(DEPTH):
        tps.append(made[0])
        lws.append(made[1])
        nxt = None
        if i < DEPTH - 1:
            nxt = (next_blocks(i + 1), functools.partial(lambda got, j: make_layer(j, assemble(j, got)), j=i + 1))
        xs, h1, res, made = layer_forward(xs, h1, tps[i], lws[i], tabs, dims, nxt)
        ress.append(res)
    dx, lparts = loss_head(xs, loss_target.reshape(nb * l, D_MODEL), nct + nlt, nct, "loss_head")
    loss = jnp.sum(lparts[:, 0, 0])
    grads = [None] * DEPTH
    dtps = [None] * DEPTH
    reduced = [None] * DEPTH
    dh1, parts = None, ()
    for i in reversed(range(DEPTH)):
        tpn = None if i == DEPTH - 1 else (tps[i + 1][1], tps[i + 1][0])
        dx, dh1, dtp, dn, grads[i], landed = layer_backward(dx, dh1, ress[i], tps[i], tpn, lws[i], tabs, dims, parts)
        if dn is not None:
            dtps[i + 1][1], dtps[i + 1][0] = dn
        dtps[i] = dtp
        if reduce_hooks is not None:
            if parts:
                reduced[i + 1] = reduce_hooks[1](landed)
            parts = reduce_hooks[0]({n: grads[i].pop(n) for n in REDUCED})
    if reduce_hooks is not None:
        reduced[0] = reduce_hooks[1](exchange_chips(parts, "grad_exchange_last"))
    (dx0b,), (dsc1, dsh1), _ = stage_bwd(fn_modulate, [(x0, D_MODEL, 0)], [tps[0][1], tps[0][0]], [], [], [dh1], [F32], "mod_in_bwd")
    dtps[0][1], dtps[0][0] = dsc1, dsh1
    grad_x = (dx + dx0b).reshape(nb, lt, D_MODEL)[:, lc:]
    dmod = jnp.stack([tile_param_grads(d, nb, nct, nlt) for d in dtps])
    gw = {n: jnp.stack([grads[i][n] for i in range(DEPTH)]) for n in grads[0]}
    return loss, grad_x, gw, dmod, s8, lws, reduced


MESH_IDS = pl.DeviceIdType.MESH
ANY = pl.BlockSpec(memory_space=pl.ANY)


def _place():
    return lax.axis_index("x"), lax.axis_index("y"), lax.axis_index("c")


def _dma_sems(n, per):
    return pltpu.SemaphoreType.DMA((n, per))


def _gather8_steps(x_refs, out_refs, send_sems, recv_sems, local_sems):
    n = len(x_refs)
    x, y, c = _place()
    me, sibling = (x, y, c), (x, y, 1 - c)
    chips = [(1 - x, y), (x, 1 - y), (1 - x, 1 - y)]

    def copy(a, k, blk, to, own=False):
        slot = out_refs[a].at[4 * blk[0] + 2 * blk[1] + blk[2]]
        return pltpu.make_async_remote_copy(
            src_ref=x_refs[a] if own else slot, dst_ref=slot,
            send_sem=send_sems.at[a, k], recv_sem=recv_sems.at[a, k], device_id=to, device_id_type=MESH_IDS)

    def local(a):
        return pltpu.make_async_copy(x_refs[a], out_refs[a].at[4 * x + 2 * y + c], local_sems.at[a, 0])

    def first_copies():
        cps = []
        for a in range(n):
            cps.append(copy(a, 0, me, sibling, own=True))
            cps += [copy(a, 1 + j, me, (*chip, c), own=True) for j, chip in enumerate(chips)]
        return cps

    def start():
        for a in range(n):
            local(a).start()
        for cp in first_copies():
            cp.start()

    def finish():
        passed = []
        for j, chip in enumerate(chips):
            for a in range(n):
                copy(a, 1 + j, (*chip, c), me).wait_recv()
                passed.append(copy(a, 4 + j, (*chip, c), sibling))
                passed[-1].start()
        for a in range(n):
            copy(a, 0, sibling, me).wait_recv()
            for j, chip in enumerate(chips):
                copy(a, 4 + j, (*chip, 1 - c), me).wait_recv()
        for cp in first_copies() + passed:
            cp.wait_send()
        for a in range(n):
            local(a).wait()

    return start, finish


def all_gather8(blocks, name):
    n = len(blocks)

    def body(*refs):
        start, finish = _gather8_steps(refs[:n], refs[n:2 * n], *refs[2 * n:])
        start()
        finish()

    return pl.pallas_call(
        body, name=name, out_shape=[jax.ShapeDtypeStruct((8,) + b.shape, b.dtype) for b in blocks],
        in_specs=[ANY] * n, out_specs=[ANY] * n,
        scratch_shapes=[_dma_sems(n, 7), _dma_sems(n, 7), _dma_sems(n, 1)],
    )(*blocks)


def swap_cores(blocks, name):
    n = len(blocks)

    def body(*refs):
        x_refs, out_refs, (send_sems, recv_sems) = refs[:n], refs[n:2 * n], refs[2 * n:]
        x, y, c = _place()
        cps = [pltpu.make_async_remote_copy(src_ref=x_refs[a], dst_ref=out_refs[a], send_sem=send_sems.at[a, 0],
                                            recv_sem=recv_sems.at[a, 0], device_id=(x, y, 1 - c), device_id_type=MESH_IDS)
               for a in range(n)]
        for cp in cps:
            cp.start()
        for cp in cps:
            cp.wait()

    return pl.pallas_call(
        body, name=name, out_shape=[jax.ShapeDtypeStruct(b.shape, b.dtype) for b in blocks],
        in_specs=[ANY] * n, out_specs=[ANY] * n, scratch_shapes=[_dma_sems(n, 1), _dma_sems(n, 1)],
    )(*blocks)


def _exchange_steps(p_refs, out_refs, send_sems, recv_sems, local_sems):
    n = len(p_refs)
    x, y, c = _place()
    jm = 2 * x + y
    chips = [(1 - x, y), (x, 1 - y), (1 - x, 1 - y)]

    def local(a):
        return pltpu.make_async_copy(p_refs[a].at[jm], out_refs[a].at[jm], local_sems.at[a, 0])

    def sends():
        return [pltpu.make_async_remote_copy(
            src_ref=p_refs[a].at[2 * px + py], dst_ref=out_refs[a].at[jm], send_sem=send_sems.at[a, k],
            recv_sem=recv_sems.at[a, k], device_id=(px, py, c), device_id_type=MESH_IDS)
            for k, (px, py) in enumerate(chips) for a in range(n)]

    def start():
        for a in range(n):
            local(a).start()
        for cp in sends():
            cp.start()

    def finish():
        for k, (px, py) in enumerate(chips):
            for a in range(n):
                pltpu.make_async_remote_copy(
                    src_ref=p_refs[a].at[jm], dst_ref=out_refs[a].at[2 * px + py], send_sem=send_sems.at[a, k],
                    recv_sem=recv_sems.at[a, k], device_id=(px, py, c), device_id_type=MESH_IDS).wait_recv()
        for cp in sends():
            cp.wait_send()
        for a in range(n):
            local(a).wait()

    return start, finish


def exchange_chips(parts, name):
    n = len(parts)

    def body(*refs):
        start, finish = _exchange_steps(refs[:n], refs[n:2 * n], *refs[2 * n:])
        start()
        finish()

    return pl.pallas_call(
        body, name=name, out_shape=[jax.ShapeDtypeStruct(p.shape, p.dtype) for p in parts],
        in_specs=[ANY] * n, out_specs=[ANY] * n, scratch_shapes=[_dma_sems(n, 3), _dma_sems(n, 3), _dma_sems(n, 1)],
    )(*parts)


def _row_tile(rows, cols, itemsize=4, limit=1 << 21):
    for cand in (2048, 1024, 512, 256, 128, 64, 32, 16):
        if rows % cand == 0 and cand * cols * itemsize <= limit:
            return cand
    return rows


def add_halves(a, b, kind, name):
    if kind == "col":
        m, k, n = a.shape
        n4 = n // 4
        tr = _row_tile(k, n4)
        in_spec = pl.BlockSpec((1, tr, n4), lambda j, h, i: (h, i, j))
        out_spec = pl.BlockSpec((1, 1, tr, n4), lambda j, h, i: (j, h, i, 0))
        grid, out_shape = (4, m, k // tr), (4, m, k, n4)
    elif kind == "row":
        m, k, n = a.shape
        k4 = k // 4
        tr = _row_tile(k4, n)
        nt = k4 // tr
        in_spec = pl.BlockSpec((1, tr, n), lambda j, h, i: (h, j * nt + i, 0))
        out_spec = pl.BlockSpec((1, 1, tr, n), lambda j, h, i: (j, h, i, 0))
        grid, out_shape = (4, m, nt), (4, m, k4, n)
    else:
        _, m, k, n = a.shape
        tr = _row_tile(k, n)
        in_spec = out_spec = pl.BlockSpec((1, 1, tr, n), lambda j, h, i: (j, h, i, 0))
        grid, out_shape = (4, m, k // tr), a.shape

    def body(a_ref, b_ref, s_ref):
        s_ref[...] = (a_ref[...] + b_ref[...].astype(F32)).astype(BF).reshape(s_ref.shape)

    return pl.pallas_call(body, name=name, grid=grid, in_specs=[in_spec, in_spec], out_specs=out_spec,
                          out_shape=jax.ShapeDtypeStruct(out_shape, BF))(a, b)


def sum_slots(a, name):
    s, m, k, n = a.shape
    tr = _row_tile(k, n, limit=(1 << 22) // s)

    def body(a_ref, o_ref):
        acc = a_ref[0].astype(F32)
        for j in range(1, s):
            acc = acc + a_ref[j].astype(F32)
        o_ref[...] = acc

    return pl.pallas_call(body, name=name, grid=(m, k // tr), in_specs=[pl.BlockSpec((s, 1, tr, n), lambda h, i: (0, h, i, 0))],
                          out_specs=pl.BlockSpec((1, tr, n), lambda h, i: (h, i, 0)),
                          out_shape=jax.ShapeDtypeStruct((m, k, n), F32))(a)


def sum_small(arrays, name):
    n = len(arrays)

    def body(*refs):
        for a_ref, o_ref in zip(refs[:n], refs[n:]):
            acc = a_ref[0]
            for j in range(1, 8):
                acc = acc + a_ref[j]
            o_ref[...] = acc

    return pl.pallas_call(body, name=name, out_shape=[jax.ShapeDtypeStruct(a.shape[1:], F32) for a in arrays])(*arrays)


COL_SHARDED = ("ada_w", "w_in", "mla_w_uq", "mla_w_uk", "mla_w_uv", "ffn_up", "ffn_conv_w")
ROW_SHARDED = ("w_out", "ffn_down")
GATHERED = ("ada_w", "w_in", "mla_w_uq", "mla_w_uk", "mla_w_uv", "w_out", "ffn_up", "ffn_down", "ffn_conv_w")
LAYER_GATHERED = GATHERED[:-1]
REDUCED = ("w_in", "mla_w_uq", "mla_w_uk", "mla_w_uv", "w_out", "ffn_up", "ffn_down")
SMALL = ("ada_b", "gla_gate_w", "gla_gate_b", "gla_norm_g", "ret_decay", "mla_q_norm_g", "mla_kv_norm_g",
         "ln1_g", "ln1_b", "ffn_conv_b", "ln2_g", "ln2_b")
WEIGHTS = ("c_ctx", "ada_w", "ada_b", "w_in", "gla_gate_w", "gla_gate_b", "gla_norm_g", "ret_decay", "mla_q_norm_g",
           "mla_kv_norm_g", "mla_w_uq", "mla_w_uk", "mla_w_uv", "w_out", "ln1_g", "ln1_b", "ffn_up", "ffn_conv_w",
           "ffn_conv_b", "ffn_down", "ln2_g", "ln2_b")
PACK = 16 * LANES
HALF_LAYERS = DEPTH // 2


def _pad_flat(v, n):
    return jnp.concatenate([v, jnp.zeros((n - v.shape[0],), v.dtype)]) if n > v.shape[0] else v


def _my_layers(a, c):
    return lax.dynamic_slice_in_dim(a, HALF_LAYERS * c, HALF_LAYERS, axis=0)


def layer_blocks(shards, l, c):
    out = []
    for n in LAYER_GATHERED:
        a = shards[n][l]
        out.append(lax.dynamic_slice_in_dim(a, c * (a.shape[0] // 2), a.shape[0] // 2, axis=0).astype(BF))
    return out


def layer_assemble(got):
    out = {}
    for n, g in zip(LAYER_GATHERED, got):
        _, k2, n4 = g.shape
        if n in ROW_SHARDED:
            out[n] = g.reshape(8 * k2, n4)
        else:
            out[n] = jnp.transpose(g.reshape(4, 2, k2, n4), (1, 2, 0, 3)).reshape(2 * k2, 4 * n4)
    return out


def gather_conv_taps(shard, c):
    got, = all_gather8([_my_layers(shard, c)], "gather_conv_taps")
    _, _, k, n4 = got.shape
    return jnp.transpose(got.reshape(4, 2, HALF_LAYERS, k, n4), (1, 2, 3, 0, 4)).reshape(DEPTH, k, 4 * n4)


def _reduce_kind(n, g):
    return "row" if n in ROW_SHARDED else ("col" if (g.shape[-1] // 4) % LANES == 0 else "pre")


def reduce_begin(g, c):
    keep, give, kinds = [], [], []
    for n in REDUCED:
        a, kind = g[n], _reduce_kind(n, g[n])
        if kind == "pre":
            a = jnp.transpose(a.reshape(a.shape[0], 4, a.shape[1] // 4), (1, 0, 2))
        axis = a.ndim - 1 if kind == "row" else a.ndim - 2
        half = a.shape[axis] // 2
        lead = (slice(None), None) if kind == "pre" else (None,)
        keep.append(lax.dynamic_slice_in_dim(a, half * c, half, axis=axis)[lead])
        give.append(lax.dynamic_slice_in_dim(a, half * (1 - c), half, axis=axis)[lead].astype(BF))
        kinds.append(kind)
    got = swap_cores(give, "grad_swap_cores")
    return [add_halves(a, b, kind, "grad_add_cores_" + n) for n, a, b, kind in zip(REDUCED, keep, got, kinds)]


def reduce_end(landed):
    return [sum_slots(a, "grad_sum_chips_" + n) for n, a in zip(REDUCED, landed)]


def reduce_assemble(layers, c):
    mine = [jnp.concatenate([layers[l][k] for l in range(DEPTH)], axis=0) for k in range(len(REDUCED))]
    theirs = swap_cores(mine, "grad_swap_back")
    out = {}
    for n, a, b in zip(REDUCED, mine, theirs):
        out[n] = jnp.concatenate([jnp.where(c == 0, a, b), jnp.where(c == 0, b, a)], axis=2 if n in ROW_SHARDED else 1)
    return out


def _pack_small(d, names):
    flat = jnp.concatenate([d[n].astype(F32).reshape(-1) for n in names])
    n = -(-flat.shape[0] // PACK) * PACK
    return _pad_flat(flat, n).reshape(n // LANES, LANES)


def _unpack_small(buf, like, names):
    flat, out, at = buf.reshape(-1), {}, 0
    for n in names:
        sz = like[n].size
        out[n] = flat[at:at + sz].reshape(like[n].shape)
        at += sz
    return out


def kernel(x, c, ctx, c_ctx, ada_w, ada_b, w_in, gla_gate_w, gla_gate_b, gla_norm_g, ret_decay, mla_q_norm_g, mla_kv_norm_g, mla_w_uq, mla_w_uk, mla_w_uv, w_out, ln1_g, ln1_b, ffn_up, ffn_conv_w, ffn_conv_b, ffn_down, ln2_g, ln2_b, loss_target, m_c_ctx, m_ada_w, m_ada_b, m_w_in, m_gla_gate_w, m_gla_gate_b, m_gla_norm_g, m_ret_decay, m_mla_q_norm_g, m_mla_kv_norm_g, m_mla_w_uq, m_mla_w_uk, m_mla_w_uv, m_w_out, m_ln1_g, m_ln1_b, m_ffn_up, m_ffn_conv_w, m_ffn_conv_b, m_ffn_down, m_ln2_g, m_ln2_b, v_c_ctx, v_ada_w, v_ada_b, v_w_in, v_gla_gate_w, v_gla_gate_b, v_gla_norm_g, v_ret_decay, v_mla_q_norm_g, v_mla_kv_norm_g, v_mla_w_uq, v_mla_w_uk, v_mla_w_uv, v_w_out, v_ln1_g, v_ln1_b, v_ffn_up, v_ffn_conv_w, v_ffn_conv_b, v_ffn_down, v_ln2_g, v_ln2_b):
    w = dict(c_ctx=c_ctx, ada_w=ada_w, ada_b=ada_b, w_in=w_in, gla_gate_w=gla_gate_w, gla_gate_b=gla_gate_b, gla_norm_g=gla_norm_g, ret_decay=ret_decay, mla_q_norm_g=mla_q_norm_g, mla_kv_norm_g=mla_kv_norm_g, mla_w_uq=mla_w_uq, mla_w_uk=mla_w_uk, mla_w_uv=mla_w_uv, w_out=w_out, ln1_g=ln1_g, ln1_b=ln1_b, ffn_up=ffn_up, ffn_conv_w=ffn_conv_w, ffn_conv_b=ffn_conv_b, ffn_down=ffn_down, ln2_g=ln2_g, ln2_b=ln2_b)
    m = dict(c_ctx=m_c_ctx, ada_w=m_ada_w, ada_b=m_ada_b, w_in=m_w_in, gla_gate_w=m_gla_gate_w, gla_gate_b=m_gla_gate_b, gla_norm_g=m_gla_norm_g, ret_decay=m_ret_decay, mla_q_norm_g=m_mla_q_norm_g, mla_kv_norm_g=m_mla_kv_norm_g, mla_w_uq=m_mla_w_uq, mla_w_uk=m_mla_w_uk, mla_w_uv=m_mla_w_uv, w_out=m_w_out, ln1_g=m_ln1_g, ln1_b=m_ln1_b, ffn_up=m_ffn_up, ffn_conv_w=m_ffn_conv_w, ffn_conv_b=m_ffn_conv_b, ffn_down=m_ffn_down, ln2_g=m_ln2_g, ln2_b=m_ln2_b)
    v = dict(c_ctx=v_c_ctx, ada_w=v_ada_w, ada_b=v_ada_b, w_in=v_w_in, gla_gate_w=v_gla_gate_w, gla_gate_b=v_gla_gate_b, gla_norm_g=v_gla_norm_g, ret_decay=v_ret_decay, mla_q_norm_g=v_mla_q_norm_g, mla_kv_norm_g=v_mla_kv_norm_g, mla_w_uq=v_mla_w_uq, mla_w_uk=v_mla_w_uk, mla_w_uv=v_mla_w_uv, w_out=v_w_out, ln1_g=v_ln1_g, ln1_b=v_ln1_b, ffn_up=v_ffn_up, ffn_conv_w=v_ffn_conv_w, ffn_conv_b=v_ffn_conv_b, ffn_down=v_ffn_down, ln2_g=v_ln2_g, ln2_b=v_ln2_b)
    xi, yi, ci = _place()
    chip = 2 * xi + yi

    whole = {n: w[n] for n in WEIGHTS if n not in GATHERED and n != "c_ctx"}
    whole["ffn_conv_w"] = gather_conv_taps(ffn_conv_w, ci)
    first_big = layer_assemble(all_gather8(layer_blocks(w, 0, ci), "gather_layer0"))
    loss, grad_x, gw, dmod, s8, lws, reduced = local_step(
        x, c, ctx, c_ctx, whole, loss_target, first_big,
        lambda l: layer_blocks(w, l, ci), lambda l, got: layer_assemble(got),
        (lambda g: reduce_begin(g, ci), reduce_end))
    loss = lax.psum(loss, ("x", "y", "c"))

    dsil = jnp.zeros((8, D_MODEL), F32)
    for i in range(DEPTH):
        dsil = dsil + matmul(dmod[i], lws[i]["ada_w"], "nt", F32, 8, 1024, 1536, "ada_dsilu")

    grads = reduce_assemble(reduced, ci)

    small = {n: gw[n] for n in SMALL if n != "ada_b"}
    small.update(dsil=dsil[4])
    names = tuple(small)
    conv_g = gw["ffn_conv_w"].reshape(DEPTH * 3, 2 * D_FF)
    ev_small, ev_dmod, ev_s8, ev_conv = all_gather8(
        [_pack_small(small, names), dmod.reshape(DEPTH * 8, 6 * D_MODEL), s8, conv_g], "gather_small")
    sm_small, sm_dmod, sm_conv = sum_small([ev_small, ev_dmod, ev_conv], "sum_small")
    summed = _unpack_small(sm_small, small, names)
    for n in SMALL:
        if n != "ada_b":
            grads[n] = summed[n]
    grads["ada_b"] = jnp.sum(sm_dmod.reshape(DEPTH, 8, 6 * D_MODEL)[:, :5], axis=1)
    sg = jax.nn.sigmoid(c_ctx)
    grads["c_ctx"] = summed["dsil"] * (sg * (1.0 + c_ctx * (1.0 - sg)))
    ccols = ffn_conv_w.shape[2]
    grads["ffn_conv_w"] = lax.dynamic_slice_in_dim(sm_conv.reshape(DEPTH, 3, 2 * D_FF), chip * ccols, ccols, axis=2)
    s_all = ev_s8.reshape(64, D_MODEL)
    d_all = jnp.transpose(ev_dmod.reshape(8, DEPTH, 8, 6 * D_MODEL), (1, 0, 2, 3)).reshape(DEPTH, 64, 6 * D_MODEL)
    cols = ada_w.shape[2]
    g_ada = []
    for i in range(DEPTH):
        d_mine = lax.dynamic_slice_in_dim(d_all[i], chip * cols, cols, axis=1)
        g_ada.append(matmul(s_all, d_mine, "tn", F32, 1024, cols, 64, "ada_dw"))
    grads["ada_w"] = jnp.stack(g_ada)

    delta, new_m, new_v = {}, {}, {}
    for n in GATHERED:
        shp = w[n].shape
        v2 = lambda a: a.reshape(-1, shp[-1])
        d_, m_, v_ = adamw(v2(w[n]), v2(grads[n]), v2(m[n]), v2(v[n]), "adamw_" + n)
        delta[n], new_m[n], new_v[n] = d_.reshape(shp), m_.reshape(shp), v_.reshape(shp)
    rep = tuple(n for n in WEIGHTS if n not in GATHERED)
    pk = lambda d: _pack_small({n: d[n] for n in rep}, rep)
    d_, m_, v_ = adamw(pk(w), pk(grads), pk(m), pk(v), "adamw_small")
    like = {n: w[n] for n in rep}
    delta.update(_unpack_small(d_, like, rep))
    new_m.update(_unpack_small(m_, like, rep))
    new_v.update(_unpack_small(v_, like, rep))
    grads = {n: grads[n].reshape(w[n].shape) for n in WEIGHTS}
    return (loss, grad_x, *[grads[n] for n in WEIGHTS], *[delta[n] for n in WEIGHTS], *[new_m[n] for n in WEIGHTS],
            *[new_v[n] for n in WEIGHTS])
```

```python
import functools

import jax
import jax.numpy as jnp
from jax import lax
from jax.experimental import pallas as pl
from jax.experimental.pallas import tpu as pltpu

F32 = jnp.float32
BF = jnp.bfloat16

D_MODEL = 1024
DEPTH = 4
GRID_W = 64
GLA_DK = 32
GLA_TAU = 16.0
RET_DK = 32
MLA_HEADS = 8
MLA_D_NOPE = 64
MLA_D_ROPE = 32
MLA_SCALE = (MLA_D_NOPE + MLA_D_ROPE) ** -0.5
D_FF = 2816
ROPE_BASE = 10000.0
EPS = 1e-6
ALPHA = (2 * DEPTH) ** 0.25
ADAM_LR, ADAM_B1, ADAM_B2, ADAM_EPS, ADAM_WD, ADAM_STEP = 0.001, 0.9, 0.999, 1e-08, 0.01, 10

ROW_TILE = 256
CHUNK = 64
GATE_ROWS = 128
LANES = 128

C_GQ, C_GK, C_GV, C_GG, C_RQ, C_RK, C_RV, C_RG, C_CQ, C_CKV, C_LK = 0, 128, 256, 512, 768, 896, 1024, 1280, 1536, 1792, 1920
D_INP = 2048
D_IN = 1984


def _dg(a, b, ca, cb):
    return lax.dot_general(a.astype(BF), b.astype(BF), (((ca,), (cb,)), ((), ())), preferred_element_type=F32)


@jax.custom_vjp
def mm_nn(a, b):
    return _dg(a, b, 1, 0)


@jax.custom_vjp
def mm_nt(a, b):
    return _dg(a, b, 1, 1)


@jax.custom_vjp
def mm_tn(a, b):
    return _dg(a, b, 0, 0)


mm_nn.defvjp(lambda a, b: (_dg(a, b, 1, 0), (a, b)),
             lambda r, g: (mm_nt(g, r[1]).astype(r[0].dtype), mm_tn(r[0], g).astype(r[1].dtype)))
mm_nt.defvjp(lambda a, b: (_dg(a, b, 1, 1), (a, b)),
             lambda r, g: (mm_nn(g, r[1]).astype(r[0].dtype), mm_tn(g, r[0]).astype(r[1].dtype)))
mm_tn.defvjp(lambda a, b: (_dg(a, b, 0, 0), (a, b)),
             lambda r, g: (mm_nt(r[1], g).astype(r[0].dtype), mm_nn(r[0], g).astype(r[1].dtype)))


def _split3(x):
    h = x.astype(BF)
    r = x - h.astype(F32)
    m = r.astype(BF)
    lo = (r - m.astype(F32)).astype(BF)
    return h, m, lo


def _exact(x, mat, left):
    h, m, lo = _split3(x)
    if left:
        d = lambda t: lax.dot_general(mat, t, (((1,), (0,)), ((), ())), preferred_element_type=F32)
    else:
        d = lambda t: lax.dot_general(t, mat, (((1,), (0,)), ((), ())), preferred_element_type=F32)
    return (d(lo) + d(m)) + d(h)


def _iota(shape, axis):
    return lax.broadcasted_iota(jnp.int32, shape, axis)


def _tri(n, upper):
    r, c = _iota((n, n), 0), _iota((n, n), 1)
    return jnp.where((c >= r) if upper else (r >= c), 1.0, 0.0).astype(BF)


@functools.partial(jax.custom_vjp, nondiff_argnums=(1,))
def cumsum_rows(x, upper):
    return _exact(x, _tri(x.shape[0], upper), True)


cumsum_rows.defvjp(lambda x, upper: (cumsum_rows(x, upper), None),
                   lambda upper, r, g: (cumsum_rows(g, not upper),))


def _seg(n, w):
    shift = w.bit_length() - 1
    r, c = _iota((n, n), 0), _iota((n, n), 1)
    return jnp.where(lax.shift_right_logical(r, shift) == lax.shift_right_logical(c, shift), 1.0, 0.0).astype(BF)


@functools.partial(jax.custom_vjp, nondiff_argnums=(1,))
def seg_sum(x, w):
    return _exact(x, _seg(x.shape[1], w), False)


seg_sum.defvjp(lambda x, w: (seg_sum(x, w), None), lambda w, r, g: (seg_sum(g, w),))


def _place_mat(transpose):
    shape = (8 * LANES, LANES) if transpose else (LANES, 8 * LANES)
    r, c = _iota(shape, 0), _iota(shape, 1)
    src, dst = (c, r) if transpose else (r, c)
    dl = jnp.bitwise_and(dst, LANES - 1)
    ok = (dl >= 64) & (dl < 96) & (src == dl - 32)
    return jnp.where(ok, 1.0, 0.0).astype(BF)


@jax.custom_vjp
def place_kr(x):
    return _exact(x, _place_mat(False), False)


place_kr.defvjp(lambda x: (place_kr(x), None), lambda r, g: (_exact(g, _place_mat(True), False),))


@functools.partial(jax.custom_vjp, nondiff_argnums=(1,))
def lane_roll(x, s):
    return pltpu.roll(x, s, 1)


lane_roll.defvjp(lambda x, s: (pltpu.roll(x, s, 1), None),
                 lambda s, r, g: (pltpu.roll(g, (g.shape[1] - s) % g.shape[1], 1),))


def rope(x, tab, d):
    cos, sa, sb = tab
    return x * cos + lane_roll(x, LANES - d) * sa + lane_roll(x, d) * sb


def silu(x):
    return x * jax.nn.sigmoid(x)


def log_sigmoid(z):
    return jnp.minimum(z, 0.0) - jnp.log(1.0 + jnp.exp(-jnp.abs(z)))


def layer_norm(x, g, b):
    mu = jnp.mean(x, axis=-1, keepdims=True)
    xc = x - mu
    var = jnp.mean(xc * xc, axis=-1, keepdims=True)
    return xc * lax.rsqrt(var + EPS) * g + b


def matmul(a, b, mode, out_dtype, tm, tn, tk, name, b_outer=False):
    ij = (lambda f: (lambda g0, g1, kk: f(g1, g0, kk))) if b_outer else (lambda f: f)
    if mode == "nn":
        (m, k), (k2, n) = a.shape, b.shape
        a_spec = pl.BlockSpec((tm, tk), ij(lambda i, j, kk: (i, kk)))
        b_spec = pl.BlockSpec((tk, tn), ij(lambda i, j, kk: (kk, j)))
        ca, cb = 1, 0
    elif mode == "nt":
        (m, k), (n, k2) = a.shape, b.shape
        a_spec = pl.BlockSpec((tm, tk), ij(lambda i, j, kk: (i, kk)))
        b_spec = pl.BlockSpec((tn, tk), ij(lambda i, j, kk: (j, kk)))
        ca, cb = 1, 1
    else:
        (k, m), (k2, n) = a.shape, b.shape
        a_spec = pl.BlockSpec((tk, tm), ij(lambda i, j, kk: (kk, i)))
        b_spec = pl.BlockSpec((tk, tn), ij(lambda i, j, kk: (kk, j)))
        ca, cb = 0, 0
    assert k == k2 and m % tm == 0 and n % tn == 0 and k % tk == 0, (name, a.shape, b.shape, tm, tn, tk)
    nk = k // tk
    grid = (n // tn, m // tm, nk) if b_outer else (m // tm, n // tn, nk)

    def body(a_ref, b_ref, o_ref, *acc):
        part = _dg(a_ref[...], b_ref[...], ca, cb)
        if nk == 1:
            o_ref[...] = part.astype(o_ref.dtype)
            return
        acc_ref, = acc
        kk = pl.program_id(2)

        @pl.when(kk == 0)
        def _():
            acc_ref[...] = part

        @pl.when(kk > 0)
        def _():
            acc_ref[...] += part

        @pl.when(kk == nk - 1)
        def _():
            o_ref[...] = acc_ref[...].astype(o_ref.dtype)

    return pl.pallas_call(
        body, name=name, grid=grid,
        in_specs=[a_spec, b_spec], out_specs=pl.BlockSpec((tm, tn), ij(lambda i, j, kk: (i, j))),
        out_shape=jax.ShapeDtypeStruct((m, n), out_dtype),
        scratch_shapes=[] if nk == 1 else [pltpu.VMEM((tm, tn), F32)],
    )(a, b)


def _stage_specs(rows, tps, consts, ws):
    specs, args = [], []
    for arr, width, cb in rows:
        specs.append(pl.BlockSpec((ROW_TILE, width), functools.partial(lambda i, cb: (i, cb), cb=cb)))
        args.append(arr)
    for arr in tps:
        specs.append(pl.BlockSpec((1, 1, arr.shape[2]), lambda i: (i, 0, 0)))
        args.append(arr)
    for arr, period in consts:
        specs.append(pl.BlockSpec((ROW_TILE, arr.shape[1]), functools.partial(lambda i, p: (i % p, 0), p=period)))
        args.append(arr)
    for arr in ws:
        specs.append(pl.BlockSpec(arr.shape, functools.partial(lambda i, nd: (0,) * nd, nd=arr.ndim)))
        args.append(arr)
    return specs, args


def _stage_load(refs, n_rows, n_tps, n_consts, n_ws):
    it = iter(refs)
    rows = [next(it)[...].astype(F32) for _ in range(n_rows)]
    tps = [next(it)[0].astype(F32) for _ in range(n_tps)]
    consts = [next(it)[...].astype(F32) for _ in range(n_consts)]
    ws = [next(it)[...].astype(F32) for _ in range(n_ws)]
    return rows, tps, consts, ws


def stage_fwd(fn, rows, tps, consts, ws, outs, name):
    n_tiles = rows[0][0].shape[0] // ROW_TILE
    specs, args = _stage_specs(rows, tps, consts, ws)
    counts = (len(rows), len(tps), len(consts), len(ws))

    def body(*refs):
        r, t, c, w = _stage_load(refs[:sum(counts)], *counts)
        res = fn(r, t, c, w)
        for o_ref, o in zip(refs[sum(counts):], res):
            o_ref[...] = o.astype(o_ref.dtype)

    res = pl.pallas_call(
        body, name=name, grid=(n_tiles,), in_specs=specs,
        out_specs=[pl.BlockSpec((ROW_TILE, wd), lambda i: (i, 0)) for wd, _ in outs],
        out_shape=[jax.ShapeDtypeStruct((n_tiles * ROW_TILE, wd), dt) for wd, dt in outs],
    )(*args)
    return list(res)


def stage_bwd(fn, rows, tps, consts, ws, cts, row_grads, name):
    n_tiles = rows[0][0].shape[0] // ROW_TILE
    specs, args = _stage_specs(rows, tps, consts, ws)
    counts = (len(rows), len(tps), len(consts), len(ws))
    n_in = sum(counts)
    for ct in cts:
        specs.append(pl.BlockSpec((ROW_TILE, ct.shape[1]), lambda i: (i, 0)))
        args.append(ct)
    want = [k for k, dt in enumerate(row_grads) if dt is not None]
    out_specs = [pl.BlockSpec((ROW_TILE, rows[k][1]), lambda i: (i, 0)) for k in want]
    out_shape = [jax.ShapeDtypeStruct((n_tiles * ROW_TILE, rows[k][1]), row_grads[k]) for k in want]
    out_specs += [pl.BlockSpec((1, 1, a.shape[2]), lambda i: (i, 0, 0)) for a in tps]
    out_shape += [jax.ShapeDtypeStruct((n_tiles, 1, a.shape[2]), F32) for a in tps]
    out_specs += [pl.BlockSpec(a.shape, functools.partial(lambda i, nd: (0,) * nd, nd=a.ndim)) for a in ws]
    out_shape += [jax.ShapeDtypeStruct(a.shape, F32) for a in ws]

    def body(*refs):
        r, t, c, w = _stage_load(refs[:n_in], *counts)
        g = [ref[...].astype(F32) for ref in refs[n_in:n_in + len(cts)]]
        _, vjp = jax.vjp(lambda r_, t_, w_: fn(r_, t_, c, w_), r, t, w)
        dr, dt, dw = vjp(g)
        o = iter(refs[n_in + len(cts):])
        for k in want:
            ref = next(o)
            ref[...] = dr[k].astype(ref.dtype)
        for v in dt:
            next(o)[0] = v
        first = pl.program_id(0) == 0
        for v in dw:
            ref = next(o)

            @pl.when(first)
            def _():
                ref[...] = v

            @pl.when(jnp.logical_not(first))
            def _():
                ref[...] += v

    res = pl.pallas_call(body, name=name, grid=(n_tiles,), in_specs=specs, out_specs=out_specs, out_shape=out_shape)(*args)
    res = list(res)
    drows = [None] * len(rows)
    for k in want:
        drows[k] = res.pop(0)
    dtps = [res.pop(0) for _ in tps]
    dws = [res.pop(0) for _ in ws]
    return drows, dtps, dws


def fn_modulate(rows, tps, consts, ws):
    (x,), (sc, sh) = rows, tps
    return [x * (1.0 + sc) + sh]


def fn_post(rows, tps, consts, ws):
    (x, a), (g,), (lng, lnb) = rows, tps, ws
    return [layer_norm(ALPHA * x + g * a, lng, lnb)]


def fn_mix(rows, tps, consts, ws):
    ogf, ogb, orf, orb, pg, pr, mo = rows
    gng, = ws
    og = ogf + ogb
    out_g = og * lax.rsqrt(seg_sum(og * og, 64) * (1.0 / 64) + EPS) * gng * silu(pg)
    o = orf + orb
    oc = o - seg_sum(o, 64) * (1.0 / 64)
    out_r = oc * lax.rsqrt(seg_sum(oc * oc, 64) * (1.0 / 64) + EPS) * silu(pr)
    return [jnp.concatenate([out_g, out_r, mo], axis=-1)]


def fn_mla_prep(rows, tps, consts, ws):
    pq, pkv, plk = rows
    gq, gkv, wuq, wuk, wuv = ws
    qtab, ktab = consts[0:3], consts[3:6]
    cq = pq * lax.rsqrt(jnp.mean(pq * pq, axis=-1, keepdims=True) + EPS) * gq
    qp = mm_nn(cq, wuq)
    q = jnp.concatenate([rope(qp[:, h * LANES:(h + 1) * LANES], qtab, 8) * MLA_SCALE for h in range(MLA_HEADS)], axis=-1)
    ckv = pkv * lax.rsqrt(jnp.mean(pkv * pkv, axis=-1, keepdims=True) + EPS) * gkv
    k = mm_nn(ckv, wuk) + place_kr(rope(plk, ktab, 8))
    v = mm_nn(ckv, wuv)
    return [q, k, v]


def fn_assemble(rows, tps, consts, ws):
    gq, gk, gv, gg, rq, rk, rv, rg, cq, ckv, lk1, lk2 = rows
    return [jnp.concatenate([gq, gk, gv, gg, rq, rk, rv, rg, cq, ckv, lk1 + lk2], axis=-1)]


def _head_masks():
    hm = (lax.shift_right_logical(_iota((4, 1, LANES), 2), 5) == _iota((4, 1, LANES), 0)).astype(F32)
    vm = (lax.shift_right_logical(_iota((4, 1, 256), 2), 6) == _iota((4, 1, 256), 0)).astype(F32)
    bd = (lax.shift_right_logical(_iota((256, LANES), 0), 6) == lax.shift_right_logical(_iota((256, LANES), 1), 5)).astype(F32)
    return hm, vm, bd


def chunk_step(s, q, k, v, la, upper):
    hm, vm, bd = _head_masks()
    t, u = _iota((4 * CHUNK, CHUNK), 0), _iota((4 * CHUNK, CHUNK), 1)
    t = jnp.bitwise_and(t, CHUNK - 1)
    causal = (u >= t) if upper else (t >= u)
    if la.shape[0] == 1:
        pos = _iota((CHUNK, LANES), 0)
        b = ((CHUNK - pos) if upper else (pos + 1)).astype(F32) * la
        bend = float(CHUNK) * la
    else:
        b = cumsum_rows(la, upper)
        bend = jnp.sum(la, axis=0, keepdims=True)
    half = 0.5 * bend
    qd = q * jnp.exp(b - half)
    kd = k * jnp.exp(half - b)
    qe = (qd[None] * hm).reshape(4 * CHUNK, LANES)
    att = jnp.where(causal, mm_nt(qe, kd), 0.0)
    o_intra = (mm_nn(att, v).reshape(4, CHUNK, 256) * vm).sum(0)
    o = mm_nt(q * jnp.exp(b), s) + o_intra
    s_new = (s * jnp.exp(bend) + mm_tn(v, k * jnp.exp(bend - b))) * bd
    return o, s_new


def scan_step(sg, sr, q, k, v, lrk, rq, rk, rv, gw, gb, rdec, tab, upper):
    la_g = log_sigmoid(mm_nn(lrk, gw) + gb) * (1.0 / GLA_TAU)
    og, sg2 = chunk_step(sg, q * GLA_DK ** -0.5, k, v, la_g, upper)
    la_r = log_sigmoid(rdec)
    orr, sr2 = chunk_step(sr, rope(rq, tab, 16), rope(rk * RET_DK ** -0.5, tab, 16), rv, la_r, upper)
    return og, orr, sg2, sr2


def _chunk_of(n, ncc, nch, reverse):
    if not reverse:
        return n
    return jnp.where(n < ncc, ncc - 1 - n, nch - 1 + ncc - n)


def _scan_in_specs(p3, tabs, gw, gb, rdec, cidx):
    nb = p3.shape[0]

    def blk(width, cb):
        return pl.BlockSpec((nb, CHUNK, width), lambda m: (0, cidx(m), cb))

    specs = [blk(128, C_GQ // 128), blk(128, C_GK // 128), blk(256, C_GV // 256), blk(128, C_LK // 128),
             blk(128, C_RQ // 128), blk(128, C_RK // 128), blk(256, C_RV // 256)]
    args = [p3] * 7
    for t in tabs:
        specs.append(pl.BlockSpec((CHUNK, LANES), lambda m: (cidx(m), 0)))
        args.append(t)
    for w in (gw, gb, rdec):
        specs.append(pl.BlockSpec(w.shape, lambda m: (0, 0)))
        args.append(w)
    return specs, args


def scan_fwd(p, tabs, gw, gb, rdec, nb, ncc, nch, reverse, name):
    cidx = lambda n: _chunk_of(n, ncc, nch, reverse)
    t = p.shape[0]
    specs, args = _scan_in_specs(p.reshape(nb, t // nb, p.shape[1]), tabs, gw, gb, rdec, cidx)

    def body(q, k, v, lrk, rq, rk, rv, tc, ta, tb, gw_r, gb_r, rd_r, og_r, or_r, sgo_r, sro_r, sg, sr):
        @pl.when(pl.program_id(0) == 0)
        def _():
            sg[...] = jnp.zeros_like(sg)
            sr[...] = jnp.zeros_like(sr)

        sgo_r[0] = sg[...]
        sro_r[0] = sr[...]
        ld = lambda r: r[...].astype(F32)
        tab, gw_, gb_, rd_ = (ld(tc), ld(ta), ld(tb)), ld(gw_r), ld(gb_r), ld(rd_r)
        for b in range(nb):
            lb = lambda r: r[b].astype(F32)
            og, orr, sg2, sr2 = scan_step(sg[b], sr[b], lb(q), lb(k), lb(v), lb(lrk), lb(rq), lb(rk), lb(rv),
                                          gw_, gb_, rd_, tab, reverse)
            og_r[b] = og
            or_r[b] = orr
            sg[b] = sg2
            sr[b] = sr2

    row_out = pl.BlockSpec((nb, CHUNK, 256), lambda n: (0, cidx(n), 0))
    st_out = pl.BlockSpec((1, nb, 256, LANES), lambda n: (n, 0, 0, 0))
    og, orr, sgs, srs = pl.pallas_call(
        body, name=name, grid=(nch,), in_specs=specs, out_specs=[row_out, row_out, st_out, st_out],
        out_shape=[jax.ShapeDtypeStruct((nb, t // nb, 256), F32)] * 2 + [jax.ShapeDtypeStruct((nch, nb, 256, LANES), F32)] * 2,
        scratch_shapes=[pltpu.VMEM((nb, 256, LANES), F32)] * 2,
    )(*args)
    return og.reshape(t, 256), orr.reshape(t, 256), sgs, srs


def scan_bwd(p, tabs, gw, gb, rdec, sg_in, sr_in, dog, dor, prev, nb, ncc, nch, reverse, name):
    step = lambda m: nch - 1 - m
    cidx = lambda m: _chunk_of(step(m), ncc, nch, reverse)
    t = p.shape[0]
    lt = t // nb
    specs, args = _scan_in_specs(p.reshape(nb, lt, p.shape[1]), tabs, gw, gb, rdec, cidx)
    st_spec = pl.BlockSpec((1, nb, 256, LANES), lambda m: (step(m), 0, 0, 0))
    specs += [st_spec, st_spec]
    args += [sg_in, sr_in]
    row = lambda width: pl.BlockSpec((nb, CHUNK, width), lambda m: (0, cidx(m), 0))
    specs += [row(256), row(256)]
    args += [dog.reshape(nb, lt, 256), dor.reshape(nb, lt, 256)]
    widths = (128, 128, 256, 128, 128, 128, 256)
    if prev is not None:
        specs += [row(wd) for wd in widths]
        args += [a.reshape(nb, lt, a.shape[1]) for a in prev]
    n_prev = 0 if prev is None else 7

    def body(*refs):
        (q, k, v, lrk, rq, rk, rv, tc, ta, tb, gw_r, gb_r, rd_r, sgi, sri, dog_r, dor_r), rest = refs[:17], refs[17:]
        prev_r, rest = rest[:n_prev], rest[n_prev:]
        outs, (dgw_r, dgb_r, drd_r, dsg, dsr) = rest[:7], rest[7:]
        first = pl.program_id(0) == 0

        @pl.when(first)
        def _():
            dsg[...] = jnp.zeros_like(dsg)
            dsr[...] = jnp.zeros_like(dsr)

        ld = lambda r: r[...].astype(F32)
        tab, gw_, gb_, rd_ = (ld(tc), ld(ta), ld(tb)), ld(gw_r), ld(gb_r), ld(rd_r)
        wsum = None
        for b in range(nb):
            lb = lambda r: r[b].astype(F32)
            prim = (sgi[0, b], sri[0, b], lb(q), lb(k), lb(v), lb(lrk), lb(rq), lb(rk), lb(rv), gw_, gb_, rd_)
            _, vjp = jax.vjp(lambda *a: scan_step(*a, tab, reverse), *prim)
            g = vjp((lb(dog_r), lb(dor_r), dsg[b], dsr[b]))
            dsg[b] = g[0]
            dsr[b] = g[1]
            for j in range(7):
                val = g[2 + j]
                if n_prev:
                    val = val + prev_r[j][b]
                outs[j][b] = val
            wsum = g[9:12] if wsum is None else tuple(a + c for a, c in zip(wsum, g[9:12]))
        for ref, val in zip((dgw_r, dgb_r, drd_r), wsum):
            @pl.when(first)
            def _():
                ref[...] = val

            @pl.when(jnp.logical_not(first))
            def _():
                ref[...] += val

    wspec = lambda w: pl.BlockSpec(w.shape, lambda m: (0, 0))
    res = pl.pallas_call(
        body, name=name, grid=(nch,), in_specs=specs,
        out_specs=[row(wd) for wd in widths] + [wspec(gw), wspec(gb), wspec(rdec)],
        out_shape=[jax.ShapeDtypeStruct((nb, lt, wd), F32) for wd in widths]
        + [jax.ShapeDtypeStruct(w.shape, F32) for w in (gw, gb, rdec)],
        scratch_shapes=[pltpu.VMEM((nb, 256, LANES), F32)] * 2,
    )(*args)
    return tuple(a.reshape(t, a.shape[2]) for a in res[:7]), res[7], res[8], res[9]


def _attn_tiles(lc, lt):
    nct = lc // ROW_TILE
    return nct, (lt - lc) // ROW_TILE


def _attn_loop(tile, lc, lt):
    nct, nlt = _attn_tiles(lc, lt)
    for i in range(nct):
        tile(i * ROW_TILE, lc)

    def lat(i, carry):
        tile(pl.multiple_of(lc + i * ROW_TILE, ROW_TILE), lt)
        return carry

    lax.fori_loop(0, nlt, lat, 0)


def mla_fwd(q, k, v, nb, lc, lt, name, gather=()):
    ng = len(gather)

    def body(q_ref, k_ref, v_ref, *rest):
        x_refs, (o_ref, lse_ref), out_refs, sems = rest[:ng], rest[ng:ng + 2], rest[ng + 2:2 * ng + 2], rest[2 * ng + 2:]
        if ng:
            start, finish = _gather8_steps(x_refs, out_refs, *sems)
            pl.when((pl.program_id(0) == 0) & (pl.program_id(1) == 0))(start)

        def tile(r0, nk):
            rows = pl.ds(r0, ROW_TILE)
            lane = _iota((ROW_TILE, LANES), 1)
            outs, lse = [], jnp.zeros((ROW_TILE, LANES), F32)
            for j in range(2):
                s = _dg(q_ref[rows, j * LANES:(j + 1) * LANES], k_ref[0:nk, j * LANES:(j + 1) * LANES], 1, 1)
                m = jnp.max(s, axis=-1, keepdims=True)
                p = jnp.exp(s - m)
                l = jnp.sum(p, axis=-1, keepdims=True)
                outs.append(_dg(p, v_ref[0:nk, j * 64:(j + 1) * 64], 1, 0) * (1.0 / l))
                lse = jnp.where(lane == j, m + jnp.log(l), lse)
            o_ref[rows, :] = jnp.concatenate(outs, axis=-1)
            lse_ref[rows, :] = lse

        _attn_loop(tile, lc, lt)
        if ng:
            pl.when((pl.program_id(0) == nb - 1) & (pl.program_id(1) == MLA_HEADS // 2 - 1))(finish)

    pair = lambda width: pl.BlockSpec((lt, width), lambda b, h: (b, h))
    res = pl.pallas_call(
        body, name=name, grid=(nb, MLA_HEADS // 2), in_specs=[pair(2 * LANES), pair(2 * LANES), pair(LANES)] + [ANY] * ng,
        out_specs=[pair(LANES), pair(LANES)] + [ANY] * ng,
        out_shape=[jax.ShapeDtypeStruct((nb * lt, MLA_HEADS * 64), F32), jax.ShapeDtypeStruct((nb * lt, MLA_HEADS // 2 * LANES), F32)]
        + [jax.ShapeDtypeStruct((8,) + g.shape, g.dtype) for g in gather],
        scratch_shapes=[_dma_sems(ng, 7), _dma_sems(ng, 7), _dma_sems(ng, 1)] if ng else [],
    )(q, k, v, *gather)
    return res[0], res[1], list(res[2:])


def mla_bwd(q, k, v, o, lse, do, nb, lc, lt, name, exchange=()):
    ne = len(exchange)

    def body(q_ref, k_ref, v_ref, o_ref, lse_ref, do_ref, *rest):
        p_refs, (dq_ref, dk_ref, dv_ref), rest = rest[:ne], rest[ne:ne + 3], rest[ne + 3:]
        out_refs, (dka, dva), sems = rest[:ne], rest[ne:ne + 2], rest[ne + 2:]
        if ne:
            start, finish = _exchange_steps(p_refs, out_refs, *sems)
            pl.when((pl.program_id(0) == 0) & (pl.program_id(1) == 0))(start)
        dka[...] = jnp.zeros_like(dka)
        dva[...] = jnp.zeros_like(dva)

        def tile(r0, nk):
            rows = pl.ds(r0, ROW_TILE)
            dqs = []
            for j in range(2):
                qj, kj = q_ref[rows, j * LANES:(j + 1) * LANES], k_ref[0:nk, j * LANES:(j + 1) * LANES]
                vj, doj = v_ref[0:nk, j * 64:(j + 1) * 64], do_ref[rows, j * 64:(j + 1) * 64]
                p = jnp.exp(_dg(qj, kj, 1, 1) - lse_ref[rows, j:j + 1])
                dsum = jnp.sum(doj * o_ref[rows, j * 64:(j + 1) * 64], axis=-1, keepdims=True)
                ds = p * (_dg(doj, vj, 1, 1) - dsum)
                dqs.append(_dg(ds, kj, 1, 0))
                dka[j, 0:nk, :] += _dg(ds, qj, 0, 0)
                dva[j, 0:nk, :] += _dg(p, doj, 0, 0)
            dq_ref[rows, :] = jnp.concatenate(dqs, axis=-1)

        _attn_loop(tile, lc, lt)
        dk_ref[...] = jnp.concatenate([dka[0], dka[1]], axis=-1)
        dv_ref[...] = jnp.concatenate([dva[0], dva[1]], axis=-1)
        if ne:
            pl.when((pl.program_id(0) == nb - 1) & (pl.program_id(1) == MLA_HEADS // 2 - 1))(finish)

    t = nb * lt
    pair = lambda width: pl.BlockSpec((lt, width), lambda b, h: (b, h))
    res = pl.pallas_call(
        body, name=name, grid=(nb, MLA_HEADS // 2),
        in_specs=[pair(2 * LANES), pair(2 * LANES), pair(LANES), pair(LANES), pair(LANES), pair(LANES)] + [ANY] * ne,
        out_specs=[pair(2 * LANES), pair(2 * LANES), pair(LANES)] + [ANY] * ne,
        out_shape=[jax.ShapeDtypeStruct((t, MLA_HEADS * LANES), F32), jax.ShapeDtypeStruct((t, MLA_HEADS * LANES), F32),
                   jax.ShapeDtypeStruct((t, MLA_HEADS * 64), F32)] + [jax.ShapeDtypeStruct(e.shape, e.dtype) for e in exchange],
        scratch_shapes=[pltpu.VMEM((2, lt, LANES), F32), pltpu.VMEM((2, lt, 64), F32)]
        + ([_dma_sems(ne, 3), _dma_sems(ne, 3), _dma_sems(ne, 1)] if ne else []),
    )(q, k, v, o, lse, do, *exchange)
    return res[0], res[1], res[2], list(res[3:])


HALO = 16


def _gate_specs(u, per_batch, lc):
    gh = GATE_ROWS // HALO
    nh = u.shape[0] // HALO
    width = u.shape[1]
    main = pl.BlockSpec((GATE_ROWS, width), lambda i: (i, 0))
    prev = pl.BlockSpec((HALO, width), lambda i: (jnp.maximum(i * gh - 1, 0), 0))
    nxt = pl.BlockSpec((HALO, width), lambda i: (jnp.minimum((i + 1) * gh, nh - 1), 0))
    return main, prev, nxt


def _seg_edges(per_batch, lc):
    j = pl.program_id(0) % (per_batch // GATE_ROWS)
    first = (j == 0) | (j == lc // GATE_ROWS)
    last = (j == lc // GATE_ROWS - 1) | (j == per_batch // GATE_ROWS - 1)
    return first, last


def _shifted(x, prev_ref, next_ref, first, last):
    rows = _iota(x.shape, 0)
    before = jnp.where(first, 0.0, prev_ref[HALO - 1:HALO, :].astype(F32))
    after = jnp.where(last, 0.0, next_ref[0:1, :].astype(F32))
    xm = jnp.where(rows == 0, before, pltpu.roll(x, 1, 0))
    xp = jnp.where(rows == x.shape[0] - 1, after, pltpu.roll(x, x.shape[0] - 1, 0))
    return xm, xp


def _whole(a):
    return pl.BlockSpec(a.shape, lambda i: (0,) * a.ndim)


def gate_fwd(u, cw, cb, wdown, per_batch, lc, name):
    main, prev, nxt = _gate_specs(u, per_batch, lc)
    f = u.shape[1] // 2

    def body(u_ref, p_ref, n_ref, w_ref, b_ref, wd_ref, act_ref, f_ref):
        first, last = _seg_edges(per_batch, lc)
        x = u_ref[...].astype(F32)
        xm, xp = _shifted(x, p_ref, n_ref, first, last)
        c = w_ref[0:1, :] * xm + w_ref[1:2, :] * x + w_ref[2:3, :] * xp + b_ref[...]
        act = silu(c[:, :f]) * c[:, f:]
        act_ref[...] = act.astype(act_ref.dtype)
        f_ref[...] = _dg(act, wd_ref[...], 1, 0)

    return pl.pallas_call(
        body, name=name, grid=(u.shape[0] // GATE_ROWS,),
        in_specs=[main, prev, nxt, _whole(cw), _whole(cb), _whole(wdown)],
        out_specs=[pl.BlockSpec((GATE_ROWS, f), lambda i: (i, 0)), pl.BlockSpec((GATE_ROWS, wdown.shape[1]), lambda i: (i, 0))],
        out_shape=[jax.ShapeDtypeStruct((u.shape[0], f), BF), jax.ShapeDtypeStruct((u.shape[0], wdown.shape[1]), F32)],
    )(u, u, u, cw, cb, wdown)


def gate_bwd(u, cw, cb, dact, per_batch, lc, name):
    main, prev, nxt = _gate_specs(u, per_batch, lc)
    f = u.shape[1] // 2

    def body(u_ref, p_ref, n_ref, w_ref, b_ref, da_ref, dc_ref, dw_ref):
        first, last = _seg_edges(per_batch, lc)
        x = u_ref[...].astype(F32)
        xm, xp = _shifted(x, p_ref, n_ref, first, last)
        c = w_ref[0:1, :] * xm + w_ref[1:2, :] * x + w_ref[2:3, :] * xp + b_ref[...]
        a, g = c[:, :f], c[:, f:]
        sg = jax.nn.sigmoid(a)
        da = da_ref[...]
        dc = jnp.concatenate([da * g * (sg * (1.0 + a * (1.0 - sg))), da * (a * sg)], axis=-1)
        dc_ref[...] = dc.astype(dc_ref.dtype)
        part = jnp.concatenate([jnp.sum(xm * dc, axis=0, keepdims=True), jnp.sum(x * dc, axis=0, keepdims=True),
                                jnp.sum(xp * dc, axis=0, keepdims=True), jnp.sum(dc, axis=0, keepdims=True),
                                jnp.zeros((4, 2 * f), F32)], axis=0)

        @pl.when(pl.program_id(0) == 0)
        def _():
            dw_ref[...] = part

        @pl.when(pl.program_id(0) > 0)
        def _():
            dw_ref[...] += part

    return pl.pallas_call(
        body, name=name, grid=(u.shape[0] // GATE_ROWS,),
        in_specs=[main, prev, nxt, _whole(cw), _whole(cb), pl.BlockSpec((GATE_ROWS, f), lambda i: (i, 0))],
        out_specs=[main, pl.BlockSpec((8, 2 * f), lambda i: (0, 0))],
        out_shape=[jax.ShapeDtypeStruct(u.shape, BF), jax.ShapeDtypeStruct((8, 2 * f), F32)],
    )(u, u, u, cw, cb, dact)


def conv_transpose(dc, cw, per_batch, lc, name):
    main, prev, nxt = _gate_specs(dc, per_batch, lc)

    def body(d_ref, p_ref, n_ref, w_ref, du_ref):
        first, last = _seg_edges(per_batch, lc)
        x = d_ref[...].astype(F32)
        xm, xp = _shifted(x, p_ref, n_ref, first, last)
        du_ref[...] = (w_ref[0:1, :] * xp + w_ref[1:2, :] * x + w_ref[2:3, :] * xm).astype(du_ref.dtype)

    return pl.pallas_call(
        body, name=name, grid=(dc.shape[0] // GATE_ROWS,),
        in_specs=[main, prev, nxt, _whole(cw)],
        out_specs=main, out_shape=jax.ShapeDtypeStruct(dc.shape, BF),
    )(dc, dc, dc, cw)


def loss_head(x, target, tiles_per_batch, ctx_tiles, name):
    n_tiles = x.shape[0] // ROW_TILE
    lat_tiles = tiles_per_batch - ctx_tiles

    def tgt_idx(i):
        j = i % tiles_per_batch
        return jnp.where(j < ctx_tiles, 0, (i // tiles_per_batch) * lat_tiles + j - ctx_tiles), 0

    def body(x_ref, t_ref, dx_ref, l_ref):
        lat = (pl.program_id(0) % tiles_per_batch >= ctx_tiles).astype(F32)
        err = (x_ref[...] - t_ref[...]) * lat
        dx_ref[...] = err * (1.0 / D_MODEL)
        l_ref[...] = jnp.full(l_ref.shape, 0.5 / D_MODEL * jnp.sum(err * err), F32)

    return pl.pallas_call(
        body, name=name, grid=(n_tiles,),
        in_specs=[pl.BlockSpec((ROW_TILE, D_MODEL), lambda i: (i, 0)), pl.BlockSpec((ROW_TILE, D_MODEL), tgt_idx)],
        out_specs=[pl.BlockSpec((ROW_TILE, D_MODEL), lambda i: (i, 0)), pl.BlockSpec((1, 8, LANES), lambda i: (i, 0, 0))],
        out_shape=[jax.ShapeDtypeStruct(x.shape, F32), jax.ShapeDtypeStruct((n_tiles, 8, LANES), F32)],
    )(x, target)


def adamw(w, g, m, v, name):
    rows, cols = w.shape
    tr = rows
    for cand in (512, 256, 128, 64, 32, 16, 8):
        if rows % cand == 0 and cand * cols * 4 <= (1 << 20):
            tr = cand
            break

    def body(w_ref, g_ref, m_ref, v_ref, d_ref, mo_ref, vo_ref):
        gg = g_ref[...]
        m2 = ADAM_B1 * m_ref[...] + (1.0 - ADAM_B1) * gg
        v2 = ADAM_B2 * v_ref[...] + (1.0 - ADAM_B2) * (gg * gg)
        m_hat = m2 / (1.0 - ADAM_B1 ** ADAM_STEP)
        v_hat = v2 / (1.0 - ADAM_B2 ** ADAM_STEP)
        d_ref[...] = -ADAM_LR * (m_hat / (jnp.sqrt(v_hat) + ADAM_EPS) + ADAM_WD * w_ref[...])
        mo_ref[...] = m2
        vo_ref[...] = v2

    spec = pl.BlockSpec((tr, cols), lambda i: (i, 0))
    return pl.pallas_call(body, name=name, grid=(rows // tr,), in_specs=[spec] * 4, out_specs=[spec] * 3,
                          out_shape=[jax.ShapeDtypeStruct(w.shape, F32)] * 3)(w, g, m, v)


def fn_post_mod(rows, tps, consts, ws):
    (x, a), (g, sc, sh), (lng, lnb) = rows, tps, ws
    y = layer_norm(ALPHA * x + g * a, lng, lnb)
    return [y, y * (1.0 + sc) + sh]


def _pick(n, cands):
    for c in cands:
        if n % c == 0:
            return c
    return n


def rope_tables(lc, l):
    pos = jnp.arange(l, dtype=F32)
    ret_inv = 1.0 / (ROPE_BASE ** jnp.linspace(0.0, 1.0, RET_DK // 2, dtype=F32))
    ang = pos[:, None] * ret_inv
    rc, rs = jnp.cos(ang), jnp.sin(ang)
    n_ax = MLA_D_ROPE // 4
    ax_inv = ROPE_BASE ** (-jnp.arange(n_ax, dtype=F32) / n_ax)
    rows_n = l // GRID_W
    rows = jnp.repeat(jnp.arange(rows_n, dtype=F32), GRID_W)
    cols = jnp.tile(jnp.arange(GRID_W, dtype=F32), rows_n)
    ra, ca = rows[:, None] * ax_inv, cols[:, None] * ax_inv
    rwc, rws, clc, cls = jnp.cos(ra), jnp.sin(ra), jnp.cos(ca), jnp.sin(ca)
    one = lambda n: jnp.ones((l, n), F32)
    zero = lambda n: jnp.zeros((l, n), F32)
    cat = lambda parts: jnp.concatenate(parts, axis=1)

    def with_ctx(tab, is_cos):
        head = jnp.ones((lc, LANES), F32) if is_cos else jnp.zeros((lc, LANES), F32)
        return jnp.concatenate([head, tab], axis=0)

    ret = (cat([rc, rc] * 4), cat([-rs, zero(16)] * 4), cat([zero(16), rs] * 4))
    ax_c = [rwc, rwc, clc, clc]
    ax_a = [-rws, zero(8), -cls, zero(8)]
    ax_b = [zero(8), rws, zero(8), cls]
    qt = (cat([one(64)] + ax_c + [one(32)]), cat([zero(64)] + ax_a + [zero(32)]), cat([zero(64)] + ax_b + [zero(32)]))
    kt = (cat([one(32)] + ax_c + [one(64)]), cat([zero(32)] + ax_a + [zero(64)]), cat([zero(32)] + ax_b + [zero(64)]))
    fix = lambda t3: tuple(with_ctx(t, k == 0) for k, t in enumerate(t3))
    return fix(ret), fix(qt), fix(kt)


_IN_ORDER = ((0, 128), (128, 256), (256, 512), (544, 800), (800, 928), (928, 1056), (1056, 1312), (1312, 1568),
             (1568, 1824), (1824, 1952), (512, 544), (1952, 1984))


def permute_w_in(w):
    parts = [w[:, a:b] for a, b in _IN_ORDER] + [jnp.zeros((w.shape[0], D_INP - D_IN), w.dtype)]
    return jnp.concatenate(parts, axis=1)


def unpermute_w_in(g):
    out, at = {}, 0
    for a, b in _IN_ORDER:
        out[a] = g[:, at:at + b - a]
        at += b - a
    return jnp.concatenate([out[a] for a in sorted(out)], axis=1)


def layer_weights(big, w, l):
    f = lambda a: a.astype(F32)
    r = {}
    r["ada_w"] = big["ada_w"].astype(BF)
    r["win"] = permute_w_in(big["w_in"]).astype(BF)
    r["wout"] = big["w_out"].astype(BF)
    r["wup"] = big["ffn_up"].astype(BF)
    r["wdown"] = big["ffn_down"].astype(BF)
    uq = big["mla_w_uq"].reshape(256, MLA_HEADS, 96)
    r["wuq"] = jnp.pad(uq, ((0, 0), (0, 0), (0, 32))).reshape(256, 8 * LANES).astype(BF)
    uk = big["mla_w_uk"].reshape(128, MLA_HEADS, 64)
    r["wuk"] = jnp.pad(uk, ((0, 0), (0, 0), (0, 64))).reshape(128, 8 * LANES).astype(BF)
    r["wuv"] = big["mla_w_uv"].astype(BF)
    gw = f(w["gla_gate_w"][l])
    z16 = jnp.zeros((16, LANES), F32)
    z96 = jnp.zeros((96, LANES), F32)
    r["gw"] = (jnp.concatenate([gw[0], z16, z96], axis=0), jnp.concatenate([z16, gw[1], z96], axis=0))
    r["gb"] = tuple(f(w["gla_gate_b"][l][d]).reshape(1, LANES) for d in range(2))
    r["rdec"] = tuple(jnp.repeat(f(w["ret_decay"][l][d]), 32).reshape(1, LANES) for d in range(2))
    r["gng"] = jnp.tile(f(w["gla_norm_g"][l]), 4).reshape(1, 256)
    r["gq"] = f(w["mla_q_norm_g"][l]).reshape(1, 256)
    r["gkv"] = f(w["mla_kv_norm_g"][l]).reshape(1, 128)
    for n in ("ln1_g", "ln1_b", "ln2_g", "ln2_b"):
        r[n] = f(w[n][l]).reshape(1, D_MODEL)
    r["cw"] = f(w["ffn_conv_w"][l])
    r["cb"] = f(w["ffn_conv_b"][l]).reshape(1, 2 * D_FF)
    return r


def tile_params(mod_l, nb, nct, nlt):
    m6 = mod_l.reshape(8, 6, D_MODEL)
    out = []
    for j in range(6):
        parts = []
        for b in range(nb):
            parts.append(jnp.broadcast_to(m6[4, j], (nct, 1, D_MODEL)))
            parts.append(jnp.broadcast_to(m6[b, j], (nlt, 1, D_MODEL)))
        out.append(jnp.concatenate(parts, axis=0))
    return out


def tile_param_grads(dts, nb, nct, nlt):
    cols = []
    for dt in dts:
        d = dt.reshape(nb, nct + nlt, D_MODEL)
        lat = jnp.sum(d[:, nct:], axis=1)
        ctx = jnp.sum(d[:, :nct], axis=(0, 1))
        cols.append(jnp.concatenate([lat, jnp.zeros((4 - nb, D_MODEL), F32), ctx[None], jnp.zeros((3, D_MODEL), F32)], axis=0))
    return jnp.stack(cols, axis=1).reshape(8, 6 * D_MODEL)


def layer_forward(x, h1, tp, lw, tabs, dims, nxt):
    nb, lc, lt = dims
    t = x.shape[0]
    nbt = lt // ROW_TILE
    ncc, nch = lc // CHUNK, lt // CHUNK
    tm = _pick(t, (1024, 768, 512, 256))
    ret_tab, q_tab, k_tab = tabs
    full = lambda a: (a, a.shape[1], 0)
    p = matmul(h1, lw["win"], "nn", F32, tm, D_INP, 1024, "proj_in")
    ogf, orf, sgf, srf = scan_fwd(p, ret_tab, lw["gw"][0], lw["gb"][0], lw["rdec"][0], nb, ncc, nch, False, "scan_fwd_f")
    ogb, orb, sgb, srb = scan_fwd(p, ret_tab, lw["gw"][1], lw["gb"][1], lw["rdec"][1], nb, ncc, nch, True, "scan_fwd_b")
    prep_rows = [(p, 256, C_CQ // 256), (p, 128, C_CKV // 128), (p, 128, C_LK // 128)]
    prep_consts = [(a, nbt) for a in q_tab + k_tab]
    prep_ws = [lw["gq"], lw["gkv"], lw["wuq"], lw["wuk"], lw["wuv"]]
    q, k, v = stage_fwd(fn_mla_prep, prep_rows, [], prep_consts, prep_ws, [(1024, BF), (1024, BF), (512, BF)], "mla_prep")
    if nxt is None:
        mo, lse, _ = mla_fwd(q, k, v, nb, lc, lt, "mla_attn_last")
        made, tp_next = None, None
    else:
        mo, lse, got = mla_fwd(q, k, v, nb, lc, lt, "mla_attn", gather=nxt[0])
        made = nxt[1](got)
        tp_next = (made[0][1], made[0][0])
    mix_rows = [full(ogf), full(ogb), full(orf), full(orb), (p, 256, C_GG // 256), (p, 256, C_RG // 256), full(mo)]
    m, = stage_fwd(fn_mix, mix_rows, [], [], [lw["gng"]], [(1024, BF)], "mix")
    a = matmul(m, lw["wout"], "nn", F32, tm, 1024, 1024, "proj_out")
    x1, h2 = stage_fwd(fn_post_mod, [full(x), full(a)], [tp[2], tp[4], tp[3]], [], [lw["ln1_g"], lw["ln1_b"]],
                       [(1024, F32), (1024, BF)], "post1")
    u = matmul(h2, lw["wup"], "nn", BF, tm, 1408, 1024, "ffn_up", b_outer=True)
    act, f = gate_fwd(u, lw["cw"], lw["cb"], lw["wdown"], lt, lc, "ffn_gate_down")
    if tp_next is None:
        x2, = stage_fwd(fn_post, [full(x1), full(f)], [tp[5]], [], [lw["ln2_g"], lw["ln2_b"]], [(1024, F32)], "post2_last")
        h1n = None
    else:
        x2, h1n = stage_fwd(fn_post_mod, [full(x1), full(f)], [tp[5], tp_next[0], tp_next[1]], [],
                            [lw["ln2_g"], lw["ln2_b"]], [(1024, F32), (1024, BF)], "post2")
    res = dict(x=x, h1=h1, p=p, ogf=ogf, orf=orf, sgf=sgf, srf=srf, ogb=ogb, orb=orb, sgb=sgb, srb=srb, q=q, k=k, v=v,
               mo=mo, lse=lse, m=m, a=a, x1=x1, h2=h2, u=u, act=act, f=f, mix_rows=mix_rows, prep_rows=prep_rows,
               prep_consts=prep_consts, prep_ws=prep_ws)
    return x2, h1n, res, made


def layer_backward(dx2, dh1n, res, tp, tp_next, lw, tabs, dims, exchange=()):
    nb, lc, lt = dims
    r = res
    t = dx2.shape[0]
    ncc, nch = lc // CHUNK, lt // CHUNK
    tm = _pick(t, (1024, 768, 512, 256))
    tkr = _pick(t, (2304, 1536, 1024, 768, 512))
    ret_tab = tabs[0]
    full = lambda a: (a, a.shape[1], 0)
    g = {}
    if tp_next is None:
        (dx1a, df), (dg2,), (g["ln2_g"], g["ln2_b"]) = stage_bwd(
            fn_post, [full(r["x1"]), full(r["f"])], [tp[5]], [], [lw["ln2_g"], lw["ln2_b"]], [dx2], [F32, BF], "post2_last_bwd")
        dnext = None
    else:
        (dx1a, df), (dg2, dsc1n, dsh1n), (g["ln2_g"], g["ln2_b"]) = stage_bwd(
            fn_post_mod, [full(r["x1"]), full(r["f"])], [tp[5], tp_next[0], tp_next[1]], [], [lw["ln2_g"], lw["ln2_b"]],
            [dx2, dh1n], [F32, BF], "post2_bwd")
        dnext = (dsc1n, dsh1n)
    g["ffn_down"] = matmul(r["act"], df, "tn", F32, 1408, 1024, tkr, "ffn_down_dw")
    dact = matmul(df, lw["wdown"], "nt", F32, tm, 1408, 1024, "ffn_down_dx", b_outer=True)
    dc, dcw = gate_bwd(r["u"], lw["cw"], lw["cb"], dact, lt, lc, "ffn_gate_bwd")
    g["ffn_conv_w"], g["ffn_conv_b"] = dcw[0:3], dcw[3]
    du = conv_transpose(dc, lw["cw"], lt, lc, "ffn_conv_t")
    dh2 = matmul(du, lw["wup"], "nt", F32, _pick(t, (512, 256)), 1024, 2 * D_FF, "ffn_up_dx")
    g["ffn_up"] = matmul(r["h2"], du, "tn", F32, 1024, 1408, tkr, "ffn_up_dw")
    (dxa, da), (dg1, dsc2, dsh2), (g["ln1_g"], g["ln1_b"]) = stage_bwd(
        fn_post_mod, [full(r["x"]), full(r["a"])], [tp[2], tp[4], tp[3]], [], [lw["ln1_g"], lw["ln1_b"]],
        [dx1a, dh2], [F32, BF], "post1_bwd")
    dm = matmul(da, lw["wout"], "nt", F32, tm, 1024, 1024, "proj_out_dx")
    g["w_out"] = matmul(r["m"], da, "tn", F32, 1024, 1024, tkr, "proj_out_dw")
    (dog, _, dor, _, dpg, dpr, dmo), _, (dgng,) = stage_bwd(
        fn_mix, r["mix_rows"], [], [], [lw["gng"]], [dm], [F32, None, F32, None, F32, F32, F32], "mix_bwd")
    g["gla_norm_g"] = jnp.sum(dgng.reshape(4, 64), axis=0)
    dq, dk, dv, landed = mla_bwd(r["q"], r["k"], r["v"], r["mo"], r["lse"], dmo, nb, lc, lt,
                                 "mla_attn_bwd" if exchange else "mla_attn_bwd_first", exchange=exchange)
    (dpq, dpkv, dplk), _, (dgq, dgkv, dwuq, dwuk, dwuv) = stage_bwd(
        fn_mla_prep, r["prep_rows"], [], r["prep_consts"], r["prep_ws"], [dq, dk, dv], [F32, F32, F32], "mla_prep_bwd")
    g["mla_q_norm_g"], g["mla_kv_norm_g"] = dgq.reshape(256), dgkv.reshape(128)
    g["mla_w_uq"] = dwuq.reshape(256, MLA_HEADS, LANES)[:, :, :96].reshape(256, MLA_HEADS * 96)
    g["mla_w_uk"] = dwuk.reshape(128, MLA_HEADS, LANES)[:, :, :64].reshape(128, MLA_HEADS * 64)
    g["mla_w_uv"] = dwuv
    s7, dgw0, dgb0, drd0 = scan_bwd(r["p"], ret_tab, lw["gw"][0], lw["gb"][0], lw["rdec"][0], r["sgf"], r["srf"], dog, dor,
                                    None, nb, ncc, nch, False, "scan_bwd_f")
    s7, dgw1, dgb1, drd1 = scan_bwd(r["p"], ret_tab, lw["gw"][1], lw["gb"][1], lw["rdec"][1], r["sgb"], r["srb"], dog, dor,
                                    s7, nb, ncc, nch, True, "scan_bwd_b")
    g["gla_gate_w"] = jnp.stack([dgw0[0:16], dgw1[16:32]])
    g["gla_gate_b"] = jnp.stack([dgb0[0], dgb1[0]])
    g["ret_decay"] = jnp.stack([jnp.sum(drd0.reshape(4, 32), axis=1), jnp.sum(drd1.reshape(4, 32), axis=1)])
    gq_, gk_, gv_, glrk, rq_, rk_, rv_ = s7
    pieces = [gq_, gk_, gv_, dpg, rq_, rk_, rv_, dpr, dpq, dpkv, glrk, dplk]
    dp, = stage_fwd(fn_assemble, [full(a) for a in pieces], [], [], [], [(D_INP, BF)], "dproj_assemble")
    dh1 = matmul(dp, lw["win"], "nt", F32, tm, 1024, D_INP, "proj_in_dx")
    g["w_in"] = unpermute_w_in(matmul(r["h1"], dp, "tn", F32, 1024, 1024, tkr, "proj_in_dw"))
    for n in ("ln1_g", "ln1_b", "ln2_g", "ln2_b"):
        g[n] = g[n].reshape(D_MODEL)
    dtp = [None, None, dg1, dsh2, dsc2, dg2]
    return dxa, dh1, dtp, dnext, g, landed


def local_step(x, c, ctx, c_ctx, w, loss_target, first_big, next_blocks, assemble, reduce_hooks=None):
    nb, l, _ = x.shape
    lc = ctx.shape[1]
    lt = lc + l
    dims = (nb, lc, lt)
    nct, nlt = lc // ROW_TILE, l // ROW_TILE
    tabs = rope_tables(lc, l)
    x0 = jnp.concatenate([ctx, x], axis=1).reshape(nb * lt, D_MODEL)
    s8 = jnp.concatenate([silu(c), jnp.zeros((4 - nb, D_MODEL), F32), silu(c_ctx)[None], jnp.zeros((3, D_MODEL), F32)], axis=0)

    def make_layer(i, big):
        lw = layer_weights(big, w, i)
        mod = matmul(s8, lw["ada_w"], "nn", F32, 8, 1536, 1024, "ada_mod") + w["ada_b"][i].astype(F32)[None]
        return tile_params(mod, nb, nct, nlt), lw

    made = make_layer(0, first_big)
    lws, tps = [], []
    h1, = stage_fwd(fn_modulate, [(x0, D_MODEL, 0)], [made[0][1], made[0][0]], [], [], [(D_MODEL, BF)], "mod_in")
    xs, ress = x0, []
    for i in range(DEPTH):
        tps.append(made[0])
        lws.append(made[1])
        nxt = None
        if i < DEPTH - 1:
            nxt = (next_blocks(i + 1), functools.partial(lambda got, j: make_layer(j, assemble(j, got)), j=i + 1))
        xs, h1, res, made = layer_forward(xs, h1, tps[i], lws[i], tabs, dims, nxt)
        ress.append(res)
    dx, lparts = loss_head(xs, loss_target.reshape(nb * l, D_MODEL), nct + nlt, nct, "loss_head")
    loss = jnp.sum(lparts[:, 0, 0])
    grads = [None] * DEPTH
    dtps = [None] * DEPTH
    reduced = [None] * DEPTH
    dh1, parts = None, ()
    for i in reversed(range(DEPTH)):
        tpn = None if i == DEPTH - 1 else (tps[i + 1][1], tps[i + 1][0])
        dx, dh1, dtp, dn, grads[i], landed = layer_backward(dx, dh1, ress[i], tps[i], tpn, lws[i], tabs, dims, parts)
        if dn is not None:
            dtps[i + 1][1], dtps[i + 1][0] = dn
        dtps[i] = dtp
        if reduce_hooks is not None:
            if parts:
                reduced[i + 1] = reduce_hooks[1](landed)
            parts = reduce_hooks[0]({n: grads[i].pop(n) for n in REDUCED})
    if reduce_hooks is not None:
        reduced[0] = reduce_hooks[1](exchange_chips(parts, "grad_exchange_last"))
    (dx0b,), (dsc1, dsh1), _ = stage_bwd(fn_modulate, [(x0, D_MODEL, 0)], [tps[0][1], tps[0][0]], [], [], [dh1], [F32], "mod_in_bwd")
    dtps[0][1], dtps[0][0] = dsc1, dsh1
    grad_x = (dx + dx0b).reshape(nb, lt, D_MODEL)[:, lc:]
    dmod = jnp.stack([tile_param_grads(d, nb, nct, nlt) for d in dtps])
    gw = {n: jnp.stack([grads[i][n] for i in range(DEPTH)]) for n in grads[0]}
    return loss, grad_x, gw, dmod, s8, lws, reduced


MESH_IDS = pl.DeviceIdType.MESH
ANY = pl.BlockSpec(memory_space=pl.ANY)


def _place():
    return lax.axis_index("x"), lax.axis_index("y"), lax.axis_index("c")


def _dma_sems(n, per):
    return pltpu.SemaphoreType.DMA((n, per))


def _gather8_steps(x_refs, out_refs, send_sems, recv_sems, local_sems):
    n = len(x_refs)
    x, y, c = _place()
    me, sibling = (x, y, c), (x, y, 1 - c)
    chips = [(1 - x, y), (x, 1 - y), (1 - x, 1 - y)]

    def copy(a, k, blk, to, own=False):
        slot = out_refs[a].at[4 * blk[0] + 2 * blk[1] + blk[2]]
        return pltpu.make_async_remote_copy(
            src_ref=x_refs[a] if own else slot, dst_ref=slot,
            send_sem=send_sems.at[a, k], recv_sem=recv_sems.at[a, k], device_id=to, device_id_type=MESH_IDS)

    def local(a):
        return pltpu.make_async_copy(x_refs[a], out_refs[a].at[4 * x + 2 * y + c], local_sems.at[a, 0])

    def first_copies():
        cps = []
        for a in range(n):
            cps.append(copy(a, 0, me, sibling, own=True))
            cps += [copy(a, 1 + j, me, (*chip, c), own=True) for j, chip in enumerate(chips)]
        return cps

    def start():
        for a in range(n):
            local(a).start()
        for cp in first_copies():
            cp.start()

    def finish():
        passed = []
        for j, chip in enumerate(chips):
            for a in range(n):
                copy(a, 1 + j, (*chip, c), me).wait_recv()
                passed.append(copy(a, 4 + j, (*chip, c), sibling))
                passed[-1].start()
        for a in range(n):
            copy(a, 0, sibling, me).wait_recv()
            for j, chip in enumerate(chips):
                copy(a, 4 + j, (*chip, 1 - c), me).wait_recv()
        for cp in first_copies() + passed:
            cp.wait_send()
        for a in range(n):
            local(a).wait()

    return start, finish


def all_gather8(blocks, name):
    n = len(blocks)

    def body(*refs):
        start, finish = _gather8_steps(refs[:n], refs[n:2 * n], *refs[2 * n:])
        start()
        finish()

    return pl.pallas_call(
        body, name=name, out_shape=[jax.ShapeDtypeStruct((8,) + b.shape, b.dtype) for b in blocks],
        in_specs=[ANY] * n, out_specs=[ANY] * n,
        scratch_shapes=[_dma_sems(n, 7), _dma_sems(n, 7), _dma_sems(n, 1)],
    )(*blocks)


def swap_cores(blocks, name):
    n = len(blocks)

    def body(*refs):
        x_refs, out_refs, (send_sems, recv_sems) = refs[:n], refs[n:2 * n], refs[2 * n:]
        x, y, c = _place()
        cps = [pltpu.make_async_remote_copy(src_ref=x_refs[a], dst_ref=out_refs[a], send_sem=send_sems.at[a, 0],
                                            recv_sem=recv_sems.at[a, 0], device_id=(x, y, 1 - c), device_id_type=MESH_IDS)
               for a in range(n)]
        for cp in cps:
            cp.start()
        for cp in cps:
            cp.wait()

    return pl.pallas_call(
        body, name=name, out_shape=[jax.ShapeDtypeStruct(b.shape, b.dtype) for b in blocks],
        in_specs=[ANY] * n, out_specs=[ANY] * n, scratch_shapes=[_dma_sems(n, 1), _dma_sems(n, 1)],
    )(*blocks)


def _exchange_steps(p_refs, out_refs, send_sems, recv_sems, local_sems):
    n = len(p_refs)
    x, y, c = _place()
    jm = 2 * x + y
    chips = [(1 - x, y), (x, 1 - y), (1 - x, 1 - y)]

    def local(a):
        return pltpu.make_async_copy(p_refs[a].at[jm], out_refs[a].at[jm], local_sems.at[a, 0])

    def sends():
        return [pltpu.make_async_remote_copy(
            src_ref=p_refs[a].at[2 * px + py], dst_ref=out_refs[a].at[jm], send_sem=send_sems.at[a, k],
            recv_sem=recv_sems.at[a, k], device_id=(px, py, c), device_id_type=MESH_IDS)
            for k, (px, py) in enumerate(chips) for a in range(n)]

    def start():
        for a in range(n):
            local(a).start()
        for cp in sends():
            cp.start()

    def finish():
        for k, (px, py) in enumerate(chips):
            for a in range(n):
                pltpu.make_async_remote_copy(
                    src_ref=p_refs[a].at[jm], dst_ref=out_refs[a].at[2 * px + py], send_sem=send_sems.at[a, k],
                    recv_sem=recv_sems.at[a, k], device_id=(px, py, c), device_id_type=MESH_IDS).wait_recv()
        for cp in sends():
            cp.wait_send()
        for a in range(n):
            local(a).wait()

    return start, finish


def exchange_chips(parts, name):
    n = len(parts)

    def body(*refs):
        start, finish = _exchange_steps(refs[:n], refs[n:2 * n], *refs[2 * n:])
        start()
        finish()

    return pl.pallas_call(
        body, name=name, out_shape=[jax.ShapeDtypeStruct(p.shape, p.dtype) for p in parts],
        in_specs=[ANY] * n, out_specs=[ANY] * n, scratch_shapes=[_dma_sems(n, 3), _dma_sems(n, 3), _dma_sems(n, 1)],
    )(*parts)


def _row_tile(rows, cols, itemsize=4, limit=1 << 21):
    top = min(rows, limit // (cols * itemsize)) // 16 * 16
    for cand in range(top, 0, -16):
        if rows % cand == 0:
            return cand
    return rows


def add_halves(a, b, kind, name):
    if kind == "col":
        m, k, n = a.shape
        n4 = n // 4
        tr = _row_tile(k, n4)
        in_spec = pl.BlockSpec((1, tr, n4), lambda j, h, i: (h, i, j))
        out_spec = pl.BlockSpec((1, 1, tr, n4), lambda j, h, i: (j, h, i, 0))
        grid, out_shape = (4, m, k // tr), (4, m, k, n4)
    elif kind == "row":
        m, k, n = a.shape
        k4 = k // 4
        tr = _row_tile(k4, n)
        nt = k4 // tr
        in_spec = pl.BlockSpec((1, tr, n), lambda j, h, i: (h, j * nt + i, 0))
        out_spec = pl.BlockSpec((1, 1, tr, n), lambda j, h, i: (j, h, i, 0))
        grid, out_shape = (4, m, nt), (4, m, k4, n)
    else:
        _, m, k, n = a.shape
        tr = _row_tile(k, n)
        in_spec = out_spec = pl.BlockSpec((1, 1, tr, n), lambda j, h, i: (j, h, i, 0))
        grid, out_shape = (4, m, k // tr), a.shape

    def body(a_ref, b_ref, s_ref):
        s_ref[...] = (a_ref[...] + b_ref[...].astype(F32)).astype(BF).reshape(s_ref.shape)

    return pl.pallas_call(body, name=name, grid=grid, in_specs=[in_spec, in_spec], out_specs=out_spec,
                          out_shape=jax.ShapeDtypeStruct(out_shape, BF))(a, b)


def sum_slots(a, name):
    s, m, k, n = a.shape
    tr = _row_tile(k, n, limit=(1 << 22) // s)

    def body(a_ref, o_ref):
        acc = a_ref[0].astype(F32)
        for j in range(1, s):
            acc = acc + a_ref[j].astype(F32)
        o_ref[...] = acc

    return pl.pallas_call(body, name=name, grid=(m, k // tr), in_specs=[pl.BlockSpec((s, 1, tr, n), lambda h, i: (0, h, i, 0))],
                          out_specs=pl.BlockSpec((1, tr, n), lambda h, i: (h, i, 0)),
                          out_shape=jax.ShapeDtypeStruct((m, k, n), F32))(a)


def sum_small(arrays, name):
    n = len(arrays)

    def body(*refs):
        for a_ref, o_ref in zip(refs[:n], refs[n:]):
            acc = a_ref[0]
            for j in range(1, 8):
                acc = acc + a_ref[j]
            o_ref[...] = acc

    return pl.pallas_call(body, name=name, out_shape=[jax.ShapeDtypeStruct(a.shape[1:], F32) for a in arrays])(*arrays)


COL_SHARDED = ("ada_w", "w_in", "mla_w_uq", "mla_w_uk", "mla_w_uv", "ffn_up", "ffn_conv_w")
ROW_SHARDED = ("w_out", "ffn_down")
GATHERED = ("ada_w", "w_in", "mla_w_uq", "mla_w_uk", "mla_w_uv", "w_out", "ffn_up", "ffn_down", "ffn_conv_w")
LAYER_GATHERED = GATHERED[:-1]
REDUCED = ("w_in", "mla_w_uq", "mla_w_uk", "mla_w_uv", "w_out", "ffn_up", "ffn_down")
SMALL = ("ada_b", "gla_gate_w", "gla_gate_b", "gla_norm_g", "ret_decay", "mla_q_norm_g", "mla_kv_norm_g",
         "ln1_g", "ln1_b", "ffn_conv_b", "ln2_g", "ln2_b")
WEIGHTS = ("c_ctx", "ada_w", "ada_b", "w_in", "gla_gate_w", "gla_gate_b", "gla_norm_g", "ret_decay", "mla_q_norm_g",
           "mla_kv_norm_g", "mla_w_uq", "mla_w_uk", "mla_w_uv", "w_out", "ln1_g", "ln1_b", "ffn_up", "ffn_conv_w",
           "ffn_conv_b", "ffn_down", "ln2_g", "ln2_b")
PACK = 16 * LANES
HALF_LAYERS = DEPTH // 2


def _pad_flat(v, n):
    return jnp.concatenate([v, jnp.zeros((n - v.shape[0],), v.dtype)]) if n > v.shape[0] else v


def _my_layers(a, c):
    return lax.dynamic_slice_in_dim(a, HALF_LAYERS * c, HALF_LAYERS, axis=0)


def layer_blocks(shards, l, c):
    out = []
    for n in LAYER_GATHERED:
        a = shards[n][l]
        out.append(lax.dynamic_slice_in_dim(a, c * (a.shape[0] // 2), a.shape[0] // 2, axis=0).astype(BF))
    return out


def layer_assemble(got):
    out = {}
    for n, g in zip(LAYER_GATHERED, got):
        _, k2, n4 = g.shape
        if n in ROW_SHARDED:
            out[n] = g.reshape(8 * k2, n4)
        else:
            out[n] = jnp.transpose(g.reshape(4, 2, k2, n4), (1, 2, 0, 3)).reshape(2 * k2, 4 * n4)
    return out


def gather_conv_taps(shard, c):
    got, = all_gather8([_my_layers(shard, c)], "gather_conv_taps")
    _, _, k, n4 = got.shape
    return jnp.transpose(got.reshape(4, 2, HALF_LAYERS, k, n4), (1, 2, 3, 0, 4)).reshape(DEPTH, k, 4 * n4)


def _reduce_kind(n, g):
    return "row" if n in ROW_SHARDED else ("col" if (g.shape[-1] // 4) % LANES == 0 else "pre")


def reduce_begin(g, c):
    keep, give, kinds = [], [], []
    for n in REDUCED:
        a, kind = g[n], _reduce_kind(n, g[n])
        if kind == "pre":
            a = jnp.transpose(a.reshape(a.shape[0], 4, a.shape[1] // 4), (1, 0, 2))
        axis = a.ndim - 1 if kind == "row" else a.ndim - 2
        half = a.shape[axis] // 2
        lead = (slice(None), None) if kind == "pre" else (None,)
        keep.append(lax.dynamic_slice_in_dim(a, half * c, half, axis=axis)[lead])
        give.append(lax.dynamic_slice_in_dim(a, half * (1 - c), half, axis=axis)[lead].astype(BF))
        kinds.append(kind)
    got = swap_cores(give, "grad_swap_cores")
    return [add_halves(a, b, kind, "grad_add_cores_" + n) for n, a, b, kind in zip(REDUCED, keep, got, kinds)]


def reduce_end(landed):
    return [sum_slots(a, "grad_sum_chips_" + n) for n, a in zip(REDUCED, landed)]


def reduce_assemble(layers, c):
    mine = [jnp.concatenate([layers[l][k] for l in range(DEPTH)], axis=0) for k in range(len(REDUCED))]
    theirs = swap_cores(mine, "grad_swap_back")
    out = {}
    for n, a, b in zip(REDUCED, mine, theirs):
        out[n] = jnp.concatenate([jnp.where(c == 0, a, b), jnp.where(c == 0, b, a)], axis=2 if n in ROW_SHARDED else 1)
    return out


def _pack_small(d, names):
    flat = jnp.concatenate([d[n].astype(F32).reshape(-1) for n in names])
    n = -(-flat.shape[0] // PACK) * PACK
    return _pad_flat(flat, n).reshape(n // LANES, LANES)


def _unpack_small(buf, like, names):
    flat, out, at = buf.reshape(-1), {}, 0
    for n in names:
        sz = like[n].size
        out[n] = flat[at:at + sz].reshape(like[n].shape)
        at += sz
    return out


def kernel(x, c, ctx, c_ctx, ada_w, ada_b, w_in, gla_gate_w, gla_gate_b, gla_norm_g, ret_decay, mla_q_norm_g, mla_kv_norm_g, mla_w_uq, mla_w_uk, mla_w_uv, w_out, ln1_g, ln1_b, ffn_up, ffn_conv_w, ffn_conv_b, ffn_down, ln2_g, ln2_b, loss_target, m_c_ctx, m_ada_w, m_ada_b, m_w_in, m_gla_gate_w, m_gla_gate_b, m_gla_norm_g, m_ret_decay, m_mla_q_norm_g, m_mla_kv_norm_g, m_mla_w_uq, m_mla_w_uk, m_mla_w_uv, m_w_out, m_ln1_g, m_ln1_b, m_ffn_up, m_ffn_conv_w, m_ffn_conv_b, m_ffn_down, m_ln2_g, m_ln2_b, v_c_ctx, v_ada_w, v_ada_b, v_w_in, v_gla_gate_w, v_gla_gate_b, v_gla_norm_g, v_ret_decay, v_mla_q_norm_g, v_mla_kv_norm_g, v_mla_w_uq, v_mla_w_uk, v_mla_w_uv, v_w_out, v_ln1_g, v_ln1_b, v_ffn_up, v_ffn_conv_w, v_ffn_conv_b, v_ffn_down, v_ln2_g, v_ln2_b):
    w = dict(c_ctx=c_ctx, ada_w=ada_w, ada_b=ada_b, w_in=w_in, gla_gate_w=gla_gate_w, gla_gate_b=gla_gate_b, gla_norm_g=gla_norm_g, ret_decay=ret_decay, mla_q_norm_g=mla_q_norm_g, mla_kv_norm_g=mla_kv_norm_g, mla_w_uq=mla_w_uq, mla_w_uk=mla_w_uk, mla_w_uv=mla_w_uv, w_out=w_out, ln1_g=ln1_g, ln1_b=ln1_b, ffn_up=ffn_up, ffn_conv_w=ffn_conv_w, ffn_conv_b=ffn_conv_b, ffn_down=ffn_down, ln2_g=ln2_g, ln2_b=ln2_b)
    m = dict(c_ctx=m_c_ctx, ada_w=m_ada_w, ada_b=m_ada_b, w_in=m_w_in, gla_gate_w=m_gla_gate_w, gla_gate_b=m_gla_gate_b, gla_norm_g=m_gla_norm_g, ret_decay=m_ret_decay, mla_q_norm_g=m_mla_q_norm_g, mla_kv_norm_g=m_mla_kv_norm_g, mla_w_uq=m_mla_w_uq, mla_w_uk=m_mla_w_uk, mla_w_uv=m_mla_w_uv, w_out=m_w_out, ln1_g=m_ln1_g, ln1_b=m_ln1_b, ffn_up=m_ffn_up, ffn_conv_w=m_ffn_conv_w, ffn_conv_b=m_ffn_conv_b, ffn_down=m_ffn_down, ln2_g=m_ln2_g, ln2_b=m_ln2_b)
    v = dict(c_ctx=v_c_ctx, ada_w=v_ada_w, ada_b=v_ada_b, w_in=v_w_in, gla_gate_w=v_gla_gate_w, gla_gate_b=v_gla_gate_b, gla_norm_g=v_gla_norm_g, ret_decay=v_ret_decay, mla_q_norm_g=v_mla_q_norm_g, mla_kv_norm_g=v_mla_kv_norm_g, mla_w_uq=v_mla_w_uq, mla_w_uk=v_mla_w_uk, mla_w_uv=v_mla_w_uv, w_out=v_w_out, ln1_g=v_ln1_g, ln1_b=v_ln1_b, ffn_up=v_ffn_up, ffn_conv_w=v_ffn_conv_w, ffn_conv_b=v_ffn_conv_b, ffn_down=v_ffn_down, ln2_g=v_ln2_g, ln2_b=v_ln2_b)
    xi, yi, ci = _place()
    chip = 2 * xi + yi

    whole = {n: w[n] for n in WEIGHTS if n not in GATHERED and n != "c_ctx"}
    whole["ffn_conv_w"] = gather_conv_taps(ffn_conv_w, ci)
    first_big = layer_assemble(all_gather8(layer_blocks(w, 0, ci), "gather_layer0"))
    loss, grad_x, gw, dmod, s8, lws, reduced = local_step(
        x, c, ctx, c_ctx, whole, loss_target, first_big,
        lambda l: layer_blocks(w, l, ci), lambda l, got: layer_assemble(got),
        (lambda g: reduce_begin(g, ci), reduce_end))
    loss = lax.psum(loss, ("x", "y", "c"))

    dsil = jnp.zeros((8, D_MODEL), F32)
    for i in range(DEPTH):
        dsil = dsil + matmul(dmod[i], lws[i]["ada_w"], "nt", F32, 8, 1024, 1536, "ada_dsilu")

    grads = reduce_assemble(reduced, ci)

    small = {n: gw[n] for n in SMALL if n != "ada_b"}
    small.update(dsil=dsil[4])
    names = tuple(small)
    conv_g = gw["ffn_conv_w"].reshape(DEPTH * 3, 2 * D_FF)
    ev_small, ev_dmod, ev_s8, ev_conv = all_gather8(
        [_pack_small(small, names), dmod.reshape(DEPTH * 8, 6 * D_MODEL), s8, conv_g], "gather_small")
    sm_small, sm_dmod, sm_conv = sum_small([ev_small, ev_dmod, ev_conv], "sum_small")
    summed = _unpack_small(sm_small, small, names)
    for n in SMALL:
        if n != "ada_b":
            grads[n] = summed[n]
    grads["ada_b"] = jnp.sum(sm_dmod.reshape(DEPTH, 8, 6 * D_MODEL)[:, :5], axis=1)
    sg = jax.nn.sigmoid(c_ctx)
    grads["c_ctx"] = summed["dsil"] * (sg * (1.0 + c_ctx * (1.0 - sg)))
    ccols = ffn_conv_w.shape[2]
    grads["ffn_conv_w"] = lax.dynamic_slice_in_dim(sm_conv.reshape(DEPTH, 3, 2 * D_FF), chip * ccols, ccols, axis=2)
    s_all = ev_s8.reshape(64, D_MODEL)
    d_all = jnp.transpose(ev_dmod.reshape(8, DEPTH, 8, 6 * D_MODEL), (1, 0, 2, 3)).reshape(DEPTH, 64, 6 * D_MODEL)
    cols = ada_w.shape[2]
    g_ada = []
    for i in range(DEPTH):
        d_mine = lax.dynamic_slice_in_dim(d_all[i], chip * cols, cols, axis=1)
        g_ada.append(matmul(s_all, d_mine, "tn", F32, 1024, cols, 64, "ada_dw"))
    grads["ada_w"] = jnp.stack(g_ada)

    delta, new_m, new_v = {}, {}, {}
    for n in GATHERED:
        shp = w[n].shape
        v2 = lambda a: a.reshape(-1, shp[-1])
        d_, m_, v_ = adamw(v2(w[n]), v2(grads[n]), v2(m[n]), v2(v[n]), "adamw_" + n)
        delta[n], new_m[n], new_v[n] = d_.reshape(shp), m_.reshape(shp), v_.reshape(shp)
    rep = tuple(n for n in WEIGHTS if n not in GATHERED)
    pk = lambda d: _pack_small({n: d[n] for n in rep}, rep)
    d_, m_, v_ = adamw(pk(w), pk(grads), pk(m), pk(v), "adamw_small")
    like = {n: w[n] for n in rep}
    delta.update(_unpack_small(d_, like, rep))
    new_m.update(_unpack_small(m_, like, rep))
    new_v.update(_unpack_small(v_, like, rep))
    grads = {n: grads[n].reshape(w[n].shape) for n in WEIGHTS}
    return (loss, grad_x, *[grads[n] for n in WEIGHTS], *[delta[n] for n in WEIGHTS], *[new_m[n] for n in WEIGHTS],
            *[new_v[n] for n in WEIGHTS])
```

```python
import functools

import jax
import jax.numpy as jnp
from jax import lax
from jax.experimental import pallas as pl
from jax.experimental.pallas import tpu as pltpu

F32 = jnp.float32
BF = jnp.bfloat16

D_MODEL = 1024
DEPTH = 4
GRID_W = 64
GLA_DK = 32
GLA_TAU = 16.0
RET_DK = 32
MLA_HEADS = 8
MLA_D_NOPE = 64
MLA_D_ROPE = 32
MLA_SCALE = (MLA_D_NOPE + MLA_D_ROPE) ** -0.5
D_FF = 2816
ROPE_BASE = 10000.0
EPS = 1e-6
ALPHA = (2 * DEPTH) ** 0.25
ADAM_LR, ADAM_B1, ADAM_B2, ADAM_EPS, ADAM_WD, ADAM_STEP = 0.001, 0.9, 0.999, 1e-08, 0.01, 10

ROW_TILE = 256
CHUNK = 64
GATE_ROWS = 128
ATTN_ROWS_FWD = 512
ATTN_ROWS_BWD = 512
LANES = 128

C_GQ, C_GK, C_GV, C_GG, C_RQ, C_RK, C_RV, C_RG, C_CQ, C_CKV, C_LK = 0, 128, 256, 512, 768, 896, 1024, 1280, 1536, 1792, 1920
D_INP = 2048
D_IN = 1984


def _dg(a, b, ca, cb):
    return lax.dot_general(a.astype(BF), b.astype(BF), (((ca,), (cb,)), ((), ())), preferred_element_type=F32)


@jax.custom_vjp
def mm_nn(a, b):
    return _dg(a, b, 1, 0)


@jax.custom_vjp
def mm_nt(a, b):
    return _dg(a, b, 1, 1)


@jax.custom_vjp
def mm_tn(a, b):
    return _dg(a, b, 0, 0)


mm_nn.defvjp(lambda a, b: (_dg(a, b, 1, 0), (a, b)),
             lambda r, g: (mm_nt(g, r[1]).astype(r[0].dtype), mm_tn(r[0], g).astype(r[1].dtype)))
mm_nt.defvjp(lambda a, b: (_dg(a, b, 1, 1), (a, b)),
             lambda r, g: (mm_nn(g, r[1]).astype(r[0].dtype), mm_tn(g, r[0]).astype(r[1].dtype)))
mm_tn.defvjp(lambda a, b: (_dg(a, b, 0, 0), (a, b)),
             lambda r, g: (mm_nt(r[1], g).astype(r[0].dtype), mm_nn(r[0], g).astype(r[1].dtype)))


def _split3(x):
    h = x.astype(BF)
    r = x - h.astype(F32)
    m = r.astype(BF)
    lo = (r - m.astype(F32)).astype(BF)
    return h, m, lo


def _exact(x, mat, left):
    h, m, lo = _split3(x)
    if left:
        d = lambda t: lax.dot_general(mat, t, (((1,), (0,)), ((), ())), preferred_element_type=F32)
    else:
        d = lambda t: lax.dot_general(t, mat, (((1,), (0,)), ((), ())), preferred_element_type=F32)
    return (d(lo) + d(m)) + d(h)


def _iota(shape, axis):
    return lax.broadcasted_iota(jnp.int32, shape, axis)


def _tri(n, upper):
    r, c = _iota((n, n), 0), _iota((n, n), 1)
    return jnp.where((c >= r) if upper else (r >= c), 1.0, 0.0).astype(BF)


@functools.partial(jax.custom_vjp, nondiff_argnums=(1,))
def cumsum_rows(x, upper):
    return _exact(x, _tri(x.shape[0], upper), True)


cumsum_rows.defvjp(lambda x, upper: (cumsum_rows(x, upper), None),
                   lambda upper, r, g: (cumsum_rows(g, not upper),))


def _seg(n, w):
    shift = w.bit_length() - 1
    r, c = _iota((n, n), 0), _iota((n, n), 1)
    return jnp.where(lax.shift_right_logical(r, shift) == lax.shift_right_logical(c, shift), 1.0, 0.0).astype(BF)


@functools.partial(jax.custom_vjp, nondiff_argnums=(1,))
def seg_sum(x, w):
    return _exact(x, _seg(x.shape[1], w), False)


seg_sum.defvjp(lambda x, w: (seg_sum(x, w), None), lambda w, r, g: (seg_sum(g, w),))


def _place_mat(transpose):
    shape = (8 * LANES, LANES) if transpose else (LANES, 8 * LANES)
    r, c = _iota(shape, 0), _iota(shape, 1)
    src, dst = (c, r) if transpose else (r, c)
    dl = jnp.bitwise_and(dst, LANES - 1)
    ok = (dl >= 64) & (dl < 96) & (src == dl - 32)
    return jnp.where(ok, 1.0, 0.0).astype(BF)


@jax.custom_vjp
def place_kr(x):
    return _exact(x, _place_mat(False), False)


place_kr.defvjp(lambda x: (place_kr(x), None), lambda r, g: (_exact(g, _place_mat(True), False),))


@functools.partial(jax.custom_vjp, nondiff_argnums=(1,))
def lane_roll(x, s):
    return pltpu.roll(x, s, 1)


lane_roll.defvjp(lambda x, s: (pltpu.roll(x, s, 1), None),
                 lambda s, r, g: (pltpu.roll(g, (g.shape[1] - s) % g.shape[1], 1),))


def rope(x, tab, d):
    cos, sa, sb = tab
    return x * cos + lane_roll(x, LANES - d) * sa + lane_roll(x, d) * sb


def silu(x):
    return x * jax.nn.sigmoid(x)


def log_sigmoid(z):
    return jnp.minimum(z, 0.0) - jnp.log(1.0 + jnp.exp(-jnp.abs(z)))


def layer_norm(x, g, b):
    mu = jnp.mean(x, axis=-1, keepdims=True)
    xc = x - mu
    var = jnp.mean(xc * xc, axis=-1, keepdims=True)
    return xc * lax.rsqrt(var + EPS) * g + b


def matmul(a, b, mode, out_dtype, tm, tn, tk, name, b_outer=False):
    ij = (lambda f: (lambda g0, g1, kk: f(g1, g0, kk))) if b_outer else (lambda f: f)
    if mode == "nn":
        (m, k), (k2, n) = a.shape, b.shape
        a_spec = pl.BlockSpec((tm, tk), ij(lambda i, j, kk: (i, kk)))
        b_spec = pl.BlockSpec((tk, tn), ij(lambda i, j, kk: (kk, j)))
        ca, cb = 1, 0
    elif mode == "nt":
        (m, k), (n, k2) = a.shape, b.shape
        a_spec = pl.BlockSpec((tm, tk), ij(lambda i, j, kk: (i, kk)))
        b_spec = pl.BlockSpec((tn, tk), ij(lambda i, j, kk: (j, kk)))
        ca, cb = 1, 1
    else:
        (k, m), (k2, n) = a.shape, b.shape
        a_spec = pl.BlockSpec((tk, tm), ij(lambda i, j, kk: (kk, i)))
        b_spec = pl.BlockSpec((tk, tn), ij(lambda i, j, kk: (kk, j)))
        ca, cb = 0, 0
    assert k == k2 and m % tm == 0 and n % tn == 0 and k % tk == 0, (name, a.shape, b.shape, tm, tn, tk)
    nk = k // tk
    grid = (n // tn, m // tm, nk) if b_outer else (m // tm, n // tn, nk)

    def body(a_ref, b_ref, o_ref, *acc):
        part = _dg(a_ref[...], b_ref[...], ca, cb)
        if nk == 1:
            o_ref[...] = part.astype(o_ref.dtype)
            return
        acc_ref, = acc
        kk = pl.program_id(2)

        @pl.when(kk == 0)
        def _():
            acc_ref[...] = part

        @pl.when(kk > 0)
        def _():
            acc_ref[...] += part

        @pl.when(kk == nk - 1)
        def _():
            o_ref[...] = acc_ref[...].astype(o_ref.dtype)

    return pl.pallas_call(
        body, name=name, grid=grid,
        in_specs=[a_spec, b_spec], out_specs=pl.BlockSpec((tm, tn), ij(lambda i, j, kk: (i, j))),
        out_shape=jax.ShapeDtypeStruct((m, n), out_dtype),
        scratch_shapes=[] if nk == 1 else [pltpu.VMEM((tm, tn), F32)],
    )(a, b)


def _stage_specs(rows, tps, consts, ws):
    specs, args = [], []
    for arr, width, cb in rows:
        specs.append(pl.BlockSpec((ROW_TILE, width), functools.partial(lambda i, cb: (i, cb), cb=cb)))
        args.append(arr)
    for arr in tps:
        specs.append(pl.BlockSpec((1, 1, arr.shape[2]), lambda i: (i, 0, 0)))
        args.append(arr)
    for arr, period in consts:
        specs.append(pl.BlockSpec((ROW_TILE, arr.shape[1]), functools.partial(lambda i, p: (i % p, 0), p=period)))
        args.append(arr)
    for arr in ws:
        specs.append(pl.BlockSpec(arr.shape, functools.partial(lambda i, nd: (0,) * nd, nd=arr.ndim)))
        args.append(arr)
    return specs, args


def _stage_load(refs, n_rows, n_tps, n_consts, n_ws):
    it = iter(refs)
    rows = [next(it)[...].astype(F32) for _ in range(n_rows)]
    tps = [next(it)[0].astype(F32) for _ in range(n_tps)]
    consts = [next(it)[...].astype(F32) for _ in range(n_consts)]
    ws = [next(it)[...].astype(F32) for _ in range(n_ws)]
    return rows, tps, consts, ws


def stage_fwd(fn, rows, tps, consts, ws, outs, name):
    n_tiles = rows[0][0].shape[0] // ROW_TILE
    specs, args = _stage_specs(rows, tps, consts, ws)
    counts = (len(rows), len(tps), len(consts), len(ws))

    def body(*refs):
        r, t, c, w = _stage_load(refs[:sum(counts)], *counts)
        res = fn(r, t, c, w)
        for o_ref, o in zip(refs[sum(counts):], res):
            o_ref[...] = o.astype(o_ref.dtype)

    res = pl.pallas_call(
        body, name=name, grid=(n_tiles,), in_specs=specs,
        out_specs=[pl.BlockSpec((ROW_TILE, wd), lambda i: (i, 0)) for wd, _ in outs],
        out_shape=[jax.ShapeDtypeStruct((n_tiles * ROW_TILE, wd), dt) for wd, dt in outs],
    )(*args)
    return list(res)


def stage_bwd(fn, rows, tps, consts, ws, cts, row_grads, name):
    n_tiles = rows[0][0].shape[0] // ROW_TILE
    specs, args = _stage_specs(rows, tps, consts, ws)
    counts = (len(rows), len(tps), len(consts), len(ws))
    n_in = sum(counts)
    for ct in cts:
        specs.append(pl.BlockSpec((ROW_TILE, ct.shape[1]), lambda i: (i, 0)))
        args.append(ct)
    want = [k for k, dt in enumerate(row_grads) if dt is not None]
    out_specs = [pl.BlockSpec((ROW_TILE, rows[k][1]), lambda i: (i, 0)) for k in want]
    out_shape = [jax.ShapeDtypeStruct((n_tiles * ROW_TILE, rows[k][1]), row_grads[k]) for k in want]
    out_specs += [pl.BlockSpec((1, 1, a.shape[2]), lambda i: (i, 0, 0)) for a in tps]
    out_shape += [jax.ShapeDtypeStruct((n_tiles, 1, a.shape[2]), F32) for a in tps]
    out_specs += [pl.BlockSpec(a.shape, functools.partial(lambda i, nd: (0,) * nd, nd=a.ndim)) for a in ws]
    out_shape += [jax.ShapeDtypeStruct(a.shape, F32) for a in ws]

    def body(*refs):
        r, t, c, w = _stage_load(refs[:n_in], *counts)
        g = [ref[...].astype(F32) for ref in refs[n_in:n_in + len(cts)]]
        _, vjp = jax.vjp(lambda r_, t_, w_: fn(r_, t_, c, w_), r, t, w)
        dr, dt, dw = vjp(g)
        o = iter(refs[n_in + len(cts):])
        for k in want:
            ref = next(o)
            ref[...] = dr[k].astype(ref.dtype)
        for v in dt:
            next(o)[0] = v
        first = pl.program_id(0) == 0
        for v in dw:
            ref = next(o)

            @pl.when(first)
            def _():
                ref[...] = v

            @pl.when(jnp.logical_not(first))
            def _():
                ref[...] += v

    res = pl.pallas_call(body, name=name, grid=(n_tiles,), in_specs=specs, out_specs=out_specs, out_shape=out_shape)(*args)
    res = list(res)
    drows = [None] * len(rows)
    for k in want:
        drows[k] = res.pop(0)
    dtps = [res.pop(0) for _ in tps]
    dws = [res.pop(0) for _ in ws]
    return drows, dtps, dws


def fn_modulate(rows, tps, consts, ws):
    (x,), (sc, sh) = rows, tps
    return [x * (1.0 + sc) + sh]


def fn_post(rows, tps, consts, ws):
    (x, a), (g,), (lng, lnb) = rows, tps, ws
    return [layer_norm(ALPHA * x + g * a, lng, lnb)]


def fn_mix(rows, tps, consts, ws):
    ogf, ogb, orf, orb, pg, pr, mo = rows
    gng, = ws
    og = ogf + ogb
    out_g = og * lax.rsqrt(seg_sum(og * og, 64) * (1.0 / 64) + EPS) * gng * silu(pg)
    o = orf + orb
    oc = o - seg_sum(o, 64) * (1.0 / 64)
    out_r = oc * lax.rsqrt(seg_sum(oc * oc, 64) * (1.0 / 64) + EPS) * silu(pr)
    return [jnp.concatenate([out_g, out_r, mo], axis=-1)]


def fn_mla_prep(rows, tps, consts, ws):
    pq, pkv, plk = rows
    gq, gkv, wuq, wuk, wuv = ws
    qtab, ktab = consts[0:3], consts[3:6]
    cq = pq * lax.rsqrt(jnp.mean(pq * pq, axis=-1, keepdims=True) + EPS) * gq
    qp = mm_nn(cq, wuq)
    q = jnp.concatenate([rope(qp[:, h * LANES:(h + 1) * LANES], qtab, 8) * MLA_SCALE for h in range(MLA_HEADS)], axis=-1)
    ckv = pkv * lax.rsqrt(jnp.mean(pkv * pkv, axis=-1, keepdims=True) + EPS) * gkv
    k = mm_nn(ckv, wuk) + place_kr(rope(plk, ktab, 8))
    v = mm_nn(ckv, wuv)
    return [q, k, v]


def fn_assemble(rows, tps, consts, ws):
    gq, gk, gv, gg, rq, rk, rv, rg, cq, ckv, lk1, lk2 = rows
    return [jnp.concatenate([gq, gk, gv, gg, rq, rk, rv, rg, cq, ckv, lk1 + lk2], axis=-1)]


def _head_masks():
    hm = (lax.shift_right_logical(_iota((4, 1, LANES), 2), 5) == _iota((4, 1, LANES), 0)).astype(F32)
    vm = (lax.shift_right_logical(_iota((4, 1, 256), 2), 6) == _iota((4, 1, 256), 0)).astype(F32)
    bd = (lax.shift_right_logical(_iota((256, LANES), 0), 6) == lax.shift_right_logical(_iota((256, LANES), 1), 5)).astype(F32)
    return hm, vm, bd


def chunk_step(s, q, k, v, la, upper):
    hm, vm, bd = _head_masks()
    t, u = _iota((4 * CHUNK, CHUNK), 0), _iota((4 * CHUNK, CHUNK), 1)
    t = jnp.bitwise_and(t, CHUNK - 1)
    causal = (u >= t) if upper else (t >= u)
    if la.shape[0] == 1:
        pos = _iota((CHUNK, LANES), 0)
        b = ((CHUNK - pos) if upper else (pos + 1)).astype(F32) * la
        bend = float(CHUNK) * la
    else:
        b = cumsum_rows(la, upper)
        bend = jnp.sum(la, axis=0, keepdims=True)
    half = 0.5 * bend
    qd = q * jnp.exp(b - half)
    kd = k * jnp.exp(half - b)
    qe = (qd[None] * hm).reshape(4 * CHUNK, LANES)
    att = jnp.where(causal, mm_nt(qe, kd), 0.0)
    o_intra = (mm_nn(att, v).reshape(4, CHUNK, 256) * vm).sum(0)
    o = mm_nt(q * jnp.exp(b), s) + o_intra
    s_new = (s * jnp.exp(bend) + mm_tn(v, k * jnp.exp(bend - b))) * bd
    return o, s_new


def scan_step(sg, sr, q, k, v, lrk, rq, rk, rv, gw, gb, rdec, tab, upper):
    la_g = log_sigmoid(mm_nn(lrk, gw) + gb) * (1.0 / GLA_TAU)
    og, sg2 = chunk_step(sg, q * GLA_DK ** -0.5, k, v, la_g, upper)
    la_r = log_sigmoid(rdec)
    orr, sr2 = chunk_step(sr, rope(rq, tab, 16), rope(rk * RET_DK ** -0.5, tab, 16), rv, la_r, upper)
    return og, orr, sg2, sr2


def _chunk_of(n, ncc, nch, reverse):
    if not reverse:
        return n
    return jnp.where(n < ncc, ncc - 1 - n, nch - 1 + ncc - n)


def _scan_in_specs(p3, tabs, gw, gb, rdec, cidx):
    nb = p3.shape[0]

    def blk(width, cb):
        return pl.BlockSpec((nb, CHUNK, width), lambda m: (0, cidx(m), cb))

    specs = [blk(128, C_GQ // 128), blk(128, C_GK // 128), blk(256, C_GV // 256), blk(128, C_LK // 128),
             blk(128, C_RQ // 128), blk(128, C_RK // 128), blk(256, C_RV // 256)]
    args = [p3] * 7
    for t in tabs:
        specs.append(pl.BlockSpec((CHUNK, LANES), lambda m: (cidx(m), 0)))
        args.append(t)
    for w in (gw, gb, rdec):
        specs.append(pl.BlockSpec(w.shape, lambda m: (0, 0)))
        args.append(w)
    return specs, args


def scan_fwd(p, tabs, gw, gb, rdec, nb, ncc, nch, reverse, name):
    cidx = lambda n: _chunk_of(n, ncc, nch, reverse)
    t = p.shape[0]
    specs, args = _scan_in_specs(p.reshape(nb, t // nb, p.shape[1]), tabs, gw, gb, rdec, cidx)

    def body(q, k, v, lrk, rq, rk, rv, tc, ta, tb, gw_r, gb_r, rd_r, og_r, or_r, sgo_r, sro_r, sg, sr):
        @pl.when(pl.program_id(0) == 0)
        def _():
            sg[...] = jnp.zeros_like(sg)
            sr[...] = jnp.zeros_like(sr)

        sgo_r[0] = sg[...]
        sro_r[0] = sr[...]
        ld = lambda r: r[...].astype(F32)
        tab, gw_, gb_, rd_ = (ld(tc), ld(ta), ld(tb)), ld(gw_r), ld(gb_r), ld(rd_r)
        for b in range(nb):
            lb = lambda r: r[b].astype(F32)
            og, orr, sg2, sr2 = scan_step(sg[b], sr[b], lb(q), lb(k), lb(v), lb(lrk), lb(rq), lb(rk), lb(rv),
                                          gw_, gb_, rd_, tab, reverse)
            og_r[b] = og
            or_r[b] = orr
            sg[b] = sg2
            sr[b] = sr2

    row_out = pl.BlockSpec((nb, CHUNK, 256), lambda n: (0, cidx(n), 0))
    st_out = pl.BlockSpec((1, nb, 256, LANES), lambda n: (n, 0, 0, 0))
    og, orr, sgs, srs = pl.pallas_call(
        body, name=name, grid=(nch,), in_specs=specs, out_specs=[row_out, row_out, st_out, st_out],
        out_shape=[jax.ShapeDtypeStruct((nb, t // nb, 256), F32)] * 2 + [jax.ShapeDtypeStruct((nch, nb, 256, LANES), F32)] * 2,
        scratch_shapes=[pltpu.VMEM((nb, 256, LANES), F32)] * 2,
    )(*args)
    return og.reshape(t, 256), orr.reshape(t, 256), sgs, srs


def scan_bwd(p, tabs, gw, gb, rdec, sg_in, sr_in, dog, dor, prev, nb, ncc, nch, reverse, name):
    step = lambda m: nch - 1 - m
    cidx = lambda m: _chunk_of(step(m), ncc, nch, reverse)
    t = p.shape[0]
    lt = t // nb
    specs, args = _scan_in_specs(p.reshape(nb, lt, p.shape[1]), tabs, gw, gb, rdec, cidx)
    st_spec = pl.BlockSpec((1, nb, 256, LANES), lambda m: (step(m), 0, 0, 0))
    specs += [st_spec, st_spec]
    args += [sg_in, sr_in]
    row = lambda width: pl.BlockSpec((nb, CHUNK, width), lambda m: (0, cidx(m), 0))
    specs += [row(256), row(256)]
    args += [dog.reshape(nb, lt, 256), dor.reshape(nb, lt, 256)]
    widths = (128, 128, 256, 128, 128, 128, 256)
    if prev is not None:
        specs += [row(wd) for wd in widths]
        args += [a.reshape(nb, lt, a.shape[1]) for a in prev]
    n_prev = 0 if prev is None else 7

    def body(*refs):
        (q, k, v, lrk, rq, rk, rv, tc, ta, tb, gw_r, gb_r, rd_r, sgi, sri, dog_r, dor_r), rest = refs[:17], refs[17:]
        prev_r, rest = rest[:n_prev], rest[n_prev:]
        outs, (dgw_r, dgb_r, drd_r, dsg, dsr) = rest[:7], rest[7:]
        first = pl.program_id(0) == 0

        @pl.when(first)
        def _():
            dsg[...] = jnp.zeros_like(dsg)
            dsr[...] = jnp.zeros_like(dsr)

        ld = lambda r: r[...].astype(F32)
        tab, gw_, gb_, rd_ = (ld(tc), ld(ta), ld(tb)), ld(gw_r), ld(gb_r), ld(rd_r)
        wsum = None
        for b in range(nb):
            lb = lambda r: r[b].astype(F32)
            prim = (sgi[0, b], sri[0, b], lb(q), lb(k), lb(v), lb(lrk), lb(rq), lb(rk), lb(rv), gw_, gb_, rd_)
            _, vjp = jax.vjp(lambda *a: scan_step(*a, tab, reverse), *prim)
            g = vjp((lb(dog_r), lb(dor_r), dsg[b], dsr[b]))
            dsg[b] = g[0]
            dsr[b] = g[1]
            for j in range(7):
                val = g[2 + j]
                if n_prev:
                    val = val + prev_r[j][b]
                outs[j][b] = val
            wsum = g[9:12] if wsum is None else tuple(a + c for a, c in zip(wsum, g[9:12]))
        for ref, val in zip((dgw_r, dgb_r, drd_r), wsum):
            @pl.when(first)
            def _():
                ref[...] = val

            @pl.when(jnp.logical_not(first))
            def _():
                ref[...] += val

    wspec = lambda w: pl.BlockSpec(w.shape, lambda m: (0, 0))
    res = pl.pallas_call(
        body, name=name, grid=(nch,), in_specs=specs,
        out_specs=[row(wd) for wd in widths] + [wspec(gw), wspec(gb), wspec(rdec)],
        out_shape=[jax.ShapeDtypeStruct((nb, lt, wd), F32) for wd in widths]
        + [jax.ShapeDtypeStruct(w.shape, F32) for w in (gw, gb, rdec)],
        scratch_shapes=[pltpu.VMEM((nb, 256, LANES), F32)] * 2,
    )(*args)
    return tuple(a.reshape(t, a.shape[2]) for a in res[:7]), res[7], res[8], res[9]


def _attn_tiles(lc, lt):
    nct = lc // ROW_TILE
    return nct, (lt - lc) // ROW_TILE


def _attn_loop(tile, lc, lt, lat_rows):
    for i in range(lc // ROW_TILE):
        tile(i * ROW_TILE, lc, ROW_TILE)

    def lat(i, carry):
        tile(pl.multiple_of(lc + i * lat_rows, ROW_TILE), lt, lat_rows)
        return carry

    lax.fori_loop(0, (lt - lc) // lat_rows, lat, 0)


def mla_fwd(q, k, v, nb, lc, lt, name, gather=()):
    ng = len(gather)

    def body(q_ref, k_ref, v_ref, *rest):
        x_refs, (o_ref, lse_ref), out_refs, sems = rest[:ng], rest[ng:ng + 2], rest[ng + 2:2 * ng + 2], rest[2 * ng + 2:]
        if ng:
            start, finish = _gather8_steps(x_refs, out_refs, *sems)
            pl.when((pl.program_id(0) == 0) & (pl.program_id(1) == 0))(start)

        def tile(r0, nk, nrows):
            rows = pl.ds(r0, nrows)
            lane = _iota((nrows, LANES), 1)
            outs, lse = [], jnp.zeros((nrows, LANES), F32)
            for j in range(2):
                s = _dg(q_ref[rows, j * LANES:(j + 1) * LANES], k_ref[0:nk, j * LANES:(j + 1) * LANES], 1, 1)
                m = jnp.max(s, axis=-1, keepdims=True)
                p = jnp.exp(s - m)
                l = jnp.sum(p, axis=-1, keepdims=True)
                outs.append(_dg(p, v_ref[0:nk, j * 64:(j + 1) * 64], 1, 0) * (1.0 / l))
                lse = jnp.where(lane == j, m + jnp.log(l), lse)
            o_ref[rows, :] = jnp.concatenate(outs, axis=-1)
            lse_ref[rows, :] = lse

        _attn_loop(tile, lc, lt, ATTN_ROWS_FWD)
        if ng:
            pl.when((pl.program_id(0) == nb - 1) & (pl.program_id(1) == MLA_HEADS // 2 - 1))(finish)

    pair = lambda width: pl.BlockSpec((lt, width), lambda b, h: (b, h))
    res = pl.pallas_call(
        body, name=name, grid=(nb, MLA_HEADS // 2), in_specs=[pair(2 * LANES), pair(2 * LANES), pair(LANES)] + [ANY] * ng,
        out_specs=[pair(LANES), pair(LANES)] + [ANY] * ng,
        out_shape=[jax.ShapeDtypeStruct((nb * lt, MLA_HEADS * 64), F32), jax.ShapeDtypeStruct((nb * lt, MLA_HEADS // 2 * LANES), F32)]
        + [jax.ShapeDtypeStruct((8,) + g.shape, g.dtype) for g in gather],
        scratch_shapes=[_dma_sems(ng, 7), _dma_sems(ng, 7), _dma_sems(ng, 1)] if ng else [],
    )(q, k, v, *gather)
    return res[0], res[1], list(res[2:])


def mla_bwd(q, k, v, o, lse, do, nb, lc, lt, name, exchange=()):
    ne = len(exchange)

    def body(q_ref, k_ref, v_ref, o_ref, lse_ref, do_ref, *rest):
        p_refs, (dq_ref, dk_ref, dv_ref), rest = rest[:ne], rest[ne:ne + 3], rest[ne + 3:]
        out_refs, (dka, dva), sems = rest[:ne], rest[ne:ne + 2], rest[ne + 2:]
        if ne:
            start, finish = _exchange_steps(p_refs, out_refs, *sems)
            pl.when((pl.program_id(0) == 0) & (pl.program_id(1) == 0))(start)
        dka[...] = jnp.zeros_like(dka)
        dva[...] = jnp.zeros_like(dva)

        def tile(r0, nk, nrows):
            rows = pl.ds(r0, nrows)
            dqs = []
            for j in range(2):
                qj, kj = q_ref[rows, j * LANES:(j + 1) * LANES], k_ref[0:nk, j * LANES:(j + 1) * LANES]
                vj, doj = v_ref[0:nk, j * 64:(j + 1) * 64], do_ref[rows, j * 64:(j + 1) * 64]
                p = jnp.exp(_dg(qj, kj, 1, 1) - lse_ref[rows, j:j + 1])
                dsum = jnp.sum(doj * o_ref[rows, j * 64:(j + 1) * 64], axis=-1, keepdims=True)
                ds = p * (_dg(doj, vj, 1, 1) - dsum)
                dqs.append(_dg(ds, kj, 1, 0))
                dka[j, 0:nk, :] += _dg(ds, qj, 0, 0)
                dva[j, 0:nk, :] += _dg(p, doj, 0, 0)
            dq_ref[rows, :] = jnp.concatenate(dqs, axis=-1)

        _attn_loop(tile, lc, lt, ATTN_ROWS_BWD)
        dk_ref[...] = jnp.concatenate([dka[0], dka[1]], axis=-1)
        dv_ref[...] = jnp.concatenate([dva[0], dva[1]], axis=-1)
        if ne:
            pl.when((pl.program_id(0) == nb - 1) & (pl.program_id(1) == MLA_HEADS // 2 - 1))(finish)

    t = nb * lt
    pair = lambda width: pl.BlockSpec((lt, width), lambda b, h: (b, h))
    res = pl.pallas_call(
        body, name=name, grid=(nb, MLA_HEADS // 2),
        in_specs=[pair(2 * LANES), pair(2 * LANES), pair(LANES), pair(LANES), pair(LANES), pair(LANES)] + [ANY] * ne,
        out_specs=[pair(2 * LANES), pair(2 * LANES), pair(LANES)] + [ANY] * ne,
        out_shape=[jax.ShapeDtypeStruct((t, MLA_HEADS * LANES), F32), jax.ShapeDtypeStruct((t, MLA_HEADS * LANES), F32),
                   jax.ShapeDtypeStruct((t, MLA_HEADS * 64), F32)] + [jax.ShapeDtypeStruct(e.shape, e.dtype) for e in exchange],
        scratch_shapes=[pltpu.VMEM((2, lt, LANES), F32), pltpu.VMEM((2, lt, 64), F32)]
        + ([_dma_sems(ne, 3), _dma_sems(ne, 3), _dma_sems(ne, 1)] if ne else []),
    )(q, k, v, o, lse, do, *exchange)
    return res[0], res[1], res[2], list(res[3:])


HALO = 16


def _gate_specs(u, per_batch, lc):
    gh = GATE_ROWS // HALO
    nh = u.shape[0] // HALO
    width = u.shape[1]
    main = pl.BlockSpec((GATE_ROWS, width), lambda i: (i, 0))
    prev = pl.BlockSpec((HALO, width), lambda i: (jnp.maximum(i * gh - 1, 0), 0))
    nxt = pl.BlockSpec((HALO, width), lambda i: (jnp.minimum((i + 1) * gh, nh - 1), 0))
    return main, prev, nxt


def _seg_edges(per_batch, lc):
    j = pl.program_id(0) % (per_batch // GATE_ROWS)
    first = (j == 0) | (j == lc // GATE_ROWS)
    last = (j == lc // GATE_ROWS - 1) | (j == per_batch // GATE_ROWS - 1)
    return first, last


def _shifted(x, prev_ref, next_ref, first, last):
    rows = _iota(x.shape, 0)
    before = jnp.where(first, 0.0, prev_ref[HALO - 1:HALO, :].astype(F32))
    after = jnp.where(last, 0.0, next_ref[0:1, :].astype(F32))
    xm = jnp.where(rows == 0, before, pltpu.roll(x, 1, 0))
    xp = jnp.where(rows == x.shape[0] - 1, after, pltpu.roll(x, x.shape[0] - 1, 0))
    return xm, xp


def _whole(a):
    return pl.BlockSpec(a.shape, lambda i: (0,) * a.ndim)


def gate_fwd(u, cw, cb, wdown, per_batch, lc, name):
    main, prev, nxt = _gate_specs(u, per_batch, lc)
    f = u.shape[1] // 2

    def body(u_ref, p_ref, n_ref, w_ref, b_ref, wd_ref, act_ref, f_ref):
        first, last = _seg_edges(per_batch, lc)
        x = u_ref[...].astype(F32)
        xm, xp = _shifted(x, p_ref, n_ref, first, last)
        c = w_ref[0:1, :] * xm + w_ref[1:2, :] * x + w_ref[2:3, :] * xp + b_ref[...]
        act = silu(c[:, :f]) * c[:, f:]
        act_ref[...] = act.astype(act_ref.dtype)
        f_ref[...] = _dg(act, wd_ref[...], 1, 0)

    return pl.pallas_call(
        body, name=name, grid=(u.shape[0] // GATE_ROWS,),
        in_specs=[main, prev, nxt, _whole(cw), _whole(cb), _whole(wdown)],
        out_specs=[pl.BlockSpec((GATE_ROWS, f), lambda i: (i, 0)), pl.BlockSpec((GATE_ROWS, wdown.shape[1]), lambda i: (i, 0))],
        out_shape=[jax.ShapeDtypeStruct((u.shape[0], f), BF), jax.ShapeDtypeStruct((u.shape[0], wdown.shape[1]), F32)],
    )(u, u, u, cw, cb, wdown)


def gate_bwd(u, cw, cb, dact, per_batch, lc, name):
    main, prev, nxt = _gate_specs(u, per_batch, lc)
    f = u.shape[1] // 2

    def body(u_ref, p_ref, n_ref, w_ref, b_ref, da_ref, dc_ref, dw_ref):
        first, last = _seg_edges(per_batch, lc)
        x = u_ref[...].astype(F32)
        xm, xp = _shifted(x, p_ref, n_ref, first, last)
        c = w_ref[0:1, :] * xm + w_ref[1:2, :] * x + w_ref[2:3, :] * xp + b_ref[...]
        a, g = c[:, :f], c[:, f:]
        sg = jax.nn.sigmoid(a)
        da = da_ref[...]
        dc = jnp.concatenate([da * g * (sg * (1.0 + a * (1.0 - sg))), da * (a * sg)], axis=-1)
        dc_ref[...] = dc.astype(dc_ref.dtype)
        part = jnp.concatenate([jnp.sum(xm * dc, axis=0, keepdims=True), jnp.sum(x * dc, axis=0, keepdims=True),
                                jnp.sum(xp * dc, axis=0, keepdims=True), jnp.sum(dc, axis=0, keepdims=True),
                                jnp.zeros((4, 2 * f), F32)], axis=0)

        @pl.when(pl.program_id(0) == 0)
        def _():
            dw_ref[...] = part

        @pl.when(pl.program_id(0) > 0)
        def _():
            dw_ref[...] += part

    return pl.pallas_call(
        body, name=name, grid=(u.shape[0] // GATE_ROWS,),
        in_specs=[main, prev, nxt, _whole(cw), _whole(cb), pl.BlockSpec((GATE_ROWS, f), lambda i: (i, 0))],
        out_specs=[main, pl.BlockSpec((8, 2 * f), lambda i: (0, 0))],
        out_shape=[jax.ShapeDtypeStruct(u.shape, BF), jax.ShapeDtypeStruct((8, 2 * f), F32)],
    )(u, u, u, cw, cb, dact)


def conv_transpose(dc, cw, per_batch, lc, name):
    main, prev, nxt = _gate_specs(dc, per_batch, lc)

    def body(d_ref, p_ref, n_ref, w_ref, du_ref):
        first, last = _seg_edges(per_batch, lc)
        x = d_ref[...].astype(F32)
        xm, xp = _shifted(x, p_ref, n_ref, first, last)
        du_ref[...] = (w_ref[0:1, :] * xp + w_ref[1:2, :] * x + w_ref[2:3, :] * xm).astype(du_ref.dtype)

    return pl.pallas_call(
        body, name=name, grid=(dc.shape[0] // GATE_ROWS,),
        in_specs=[main, prev, nxt, _whole(cw)],
        out_specs=main, out_shape=jax.ShapeDtypeStruct(dc.shape, BF),
    )(dc, dc, dc, cw)


def loss_head(x, target, tiles_per_batch, ctx_tiles, name):
    n_tiles = x.shape[0] // ROW_TILE
    lat_tiles = tiles_per_batch - ctx_tiles

    def tgt_idx(i):
        j = i % tiles_per_batch
        return jnp.where(j < ctx_tiles, 0, (i // tiles_per_batch) * lat_tiles + j - ctx_tiles), 0

    def body(x_ref, t_ref, dx_ref, l_ref):
        lat = (pl.program_id(0) % tiles_per_batch >= ctx_tiles).astype(F32)
        err = (x_ref[...] - t_ref[...]) * lat
        dx_ref[...] = err * (1.0 / D_MODEL)
        l_ref[...] = jnp.full(l_ref.shape, 0.5 / D_MODEL * jnp.sum(err * err), F32)

    return pl.pallas_call(
        body, name=name, grid=(n_tiles,),
        in_specs=[pl.BlockSpec((ROW_TILE, D_MODEL), lambda i: (i, 0)), pl.BlockSpec((ROW_TILE, D_MODEL), tgt_idx)],
        out_specs=[pl.BlockSpec((ROW_TILE, D_MODEL), lambda i: (i, 0)), pl.BlockSpec((1, 8, LANES), lambda i: (i, 0, 0))],
        out_shape=[jax.ShapeDtypeStruct(x.shape, F32), jax.ShapeDtypeStruct((n_tiles, 8, LANES), F32)],
    )(x, target)


def adamw(w, g, m, v, name):
    rows, cols = w.shape
    tr = rows
    for cand in (512, 256, 128, 64, 32, 16, 8):
        if rows % cand == 0 and cand * cols * 4 <= (1 << 20):
            tr = cand
            break

    def body(w_ref, g_ref, m_ref, v_ref, d_ref, mo_ref, vo_ref):
        gg = g_ref[...]
        m2 = ADAM_B1 * m_ref[...] + (1.0 - ADAM_B1) * gg
        v2 = ADAM_B2 * v_ref[...] + (1.0 - ADAM_B2) * (gg * gg)
        m_hat = m2 / (1.0 - ADAM_B1 ** ADAM_STEP)
        v_hat = v2 / (1.0 - ADAM_B2 ** ADAM_STEP)
        d_ref[...] = -ADAM_LR * (m_hat / (jnp.sqrt(v_hat) + ADAM_EPS) + ADAM_WD * w_ref[...])
        mo_ref[...] = m2
        vo_ref[...] = v2

    spec = pl.BlockSpec((tr, cols), lambda i: (i, 0))
    return pl.pallas_call(body, name=name, grid=(rows // tr,), in_specs=[spec] * 4, out_specs=[spec] * 3,
                          out_shape=[jax.ShapeDtypeStruct(w.shape, F32)] * 3)(w, g, m, v)


def fn_post_mod(rows, tps, consts, ws):
    (x, a), (g, sc, sh), (lng, lnb) = rows, tps, ws
    y = layer_norm(ALPHA * x + g * a, lng, lnb)
    return [y, y * (1.0 + sc) + sh]


def _pick(n, cands):
    for c in cands:
        if n % c == 0:
            return c
    return n


def rope_tables(lc, l):
    pos = jnp.arange(l, dtype=F32)
    ret_inv = 1.0 / (ROPE_BASE ** jnp.linspace(0.0, 1.0, RET_DK // 2, dtype=F32))
    ang = pos[:, None] * ret_inv
    rc, rs = jnp.cos(ang), jnp.sin(ang)
    n_ax = MLA_D_ROPE // 4
    ax_inv = ROPE_BASE ** (-jnp.arange(n_ax, dtype=F32) / n_ax)
    rows_n = l // GRID_W
    rows = jnp.repeat(jnp.arange(rows_n, dtype=F32), GRID_W)
    cols = jnp.tile(jnp.arange(GRID_W, dtype=F32), rows_n)
    ra, ca = rows[:, None] * ax_inv, cols[:, None] * ax_inv
    rwc, rws, clc, cls = jnp.cos(ra), jnp.sin(ra), jnp.cos(ca), jnp.sin(ca)
    one = lambda n: jnp.ones((l, n), F32)
    zero = lambda n: jnp.zeros((l, n), F32)
    cat = lambda parts: jnp.concatenate(parts, axis=1)

    def with_ctx(tab, is_cos):
        head = jnp.ones((lc, LANES), F32) if is_cos else jnp.zeros((lc, LANES), F32)
        return jnp.concatenate([head, tab], axis=0)

    ret = (cat([rc, rc] * 4), cat([-rs, zero(16)] * 4), cat([zero(16), rs] * 4))
    ax_c = [rwc, rwc, clc, clc]
    ax_a = [-rws, zero(8), -cls, zero(8)]
    ax_b = [zero(8), rws, zero(8), cls]
    qt = (cat([one(64)] + ax_c + [one(32)]), cat([zero(64)] + ax_a + [zero(32)]), cat([zero(64)] + ax_b + [zero(32)]))
    kt = (cat([one(32)] + ax_c + [one(64)]), cat([zero(32)] + ax_a + [zero(64)]), cat([zero(32)] + ax_b + [zero(64)]))
    fix = lambda t3: tuple(with_ctx(t, k == 0) for k, t in enumerate(t3))
    return fix(ret), fix(qt), fix(kt)


_IN_ORDER = ((0, 128), (128, 256), (256, 512), (544, 800), (800, 928), (928, 1056), (1056, 1312), (1312, 1568),
             (1568, 1824), (1824, 1952), (512, 544), (1952, 1984))


def permute_w_in(w):
    parts = [w[:, a:b] for a, b in _IN_ORDER] + [jnp.zeros((w.shape[0], D_INP - D_IN), w.dtype)]
    return jnp.concatenate(parts, axis=1)


def unpermute_w_in(g):
    out, at = {}, 0
    for a, b in _IN_ORDER:
        out[a] = g[:, at:at + b - a]
        at += b - a
    return jnp.concatenate([out[a] for a in sorted(out)], axis=1)


def layer_weights(big, w, l):
    f = lambda a: a.astype(F32)
    r = {}
    r["ada_w"] = big["ada_w"].astype(BF)
    r["win"] = permute_w_in(big["w_in"]).astype(BF)
    r["wout"] = big["w_out"].astype(BF)
    r["wup"] = big["ffn_up"].astype(BF)
    r["wdown"] = big["ffn_down"].astype(BF)
    uq = big["mla_w_uq"].reshape(256, MLA_HEADS, 96)
    r["wuq"] = jnp.pad(uq, ((0, 0), (0, 0), (0, 32))).reshape(256, 8 * LANES).astype(BF)
    uk = big["mla_w_uk"].reshape(128, MLA_HEADS, 64)
    r["wuk"] = jnp.pad(uk, ((0, 0), (0, 0), (0, 64))).reshape(128, 8 * LANES).astype(BF)
    r["wuv"] = big["mla_w_uv"].astype(BF)
    gw = f(w["gla_gate_w"][l])
    z16 = jnp.zeros((16, LANES), F32)
    z96 = jnp.zeros((96, LANES), F32)
    r["gw"] = (jnp.concatenate([gw[0], z16, z96], axis=0), jnp.concatenate([z16, gw[1], z96], axis=0))
    r["gb"] = tuple(f(w["gla_gate_b"][l][d]).reshape(1, LANES) for d in range(2))
    r["rdec"] = tuple(jnp.repeat(f(w["ret_decay"][l][d]), 32).reshape(1, LANES) for d in range(2))
    r["gng"] = jnp.tile(f(w["gla_norm_g"][l]), 4).reshape(1, 256)
    r["gq"] = f(w["mla_q_norm_g"][l]).reshape(1, 256)
    r["gkv"] = f(w["mla_kv_norm_g"][l]).reshape(1, 128)
    for n in ("ln1_g", "ln1_b", "ln2_g", "ln2_b"):
        r[n] = f(w[n][l]).reshape(1, D_MODEL)
    r["cw"] = f(w["ffn_conv_w"][l])
    r["cb"] = f(w["ffn_conv_b"][l]).reshape(1, 2 * D_FF)
    return r


def tile_params(mod_l, nb, nct, nlt):
    m6 = mod_l.reshape(8, 6, D_MODEL)
    out = []
    for j in range(6):
        parts = []
        for b in range(nb):
            parts.append(jnp.broadcast_to(m6[4, j], (nct, 1, D_MODEL)))
            parts.append(jnp.broadcast_to(m6[b, j], (nlt, 1, D_MODEL)))
        out.append(jnp.concatenate(parts, axis=0))
    return out


def tile_param_grads(dts, nb, nct, nlt):
    cols = []
    for dt in dts:
        d = dt.reshape(nb, nct + nlt, D_MODEL)
        lat = jnp.sum(d[:, nct:], axis=1)
        ctx = jnp.sum(d[:, :nct], axis=(0, 1))
        cols.append(jnp.concatenate([lat, jnp.zeros((4 - nb, D_MODEL), F32), ctx[None], jnp.zeros((3, D_MODEL), F32)], axis=0))
    return jnp.stack(cols, axis=1).reshape(8, 6 * D_MODEL)


def layer_forward(x, h1, tp, lw, tabs, dims, nxt):
    nb, lc, lt = dims
    t = x.shape[0]
    nbt = lt // ROW_TILE
    ncc, nch = lc // CHUNK, lt // CHUNK
    tm = _pick(t, (1024, 768, 512, 256))
    ret_tab, q_tab, k_tab = tabs
    full = lambda a: (a, a.shape[1], 0)
    p = matmul(h1, lw["win"], "nn", F32, tm, D_INP, 1024, "proj_in")
    ogf, orf, sgf, srf = scan_fwd(p, ret_tab, lw["gw"][0], lw["gb"][0], lw["rdec"][0], nb, ncc, nch, False, "scan_fwd_f")
    ogb, orb, sgb, srb = scan_fwd(p, ret_tab, lw["gw"][1], lw["gb"][1], lw["rdec"][1], nb, ncc, nch, True, "scan_fwd_b")
    prep_rows = [(p, 256, C_CQ // 256), (p, 128, C_CKV // 128), (p, 128, C_LK // 128)]
    prep_consts = [(a, nbt) for a in q_tab + k_tab]
    prep_ws = [lw["gq"], lw["gkv"], lw["wuq"], lw["wuk"], lw["wuv"]]
    q, k, v = stage_fwd(fn_mla_prep, prep_rows, [], prep_consts, prep_ws, [(1024, BF), (1024, BF), (512, BF)], "mla_prep")
    if nxt is None:
        mo, lse, _ = mla_fwd(q, k, v, nb, lc, lt, "mla_attn_last")
        made, tp_next = None, None
    else:
        mo, lse, got = mla_fwd(q, k, v, nb, lc, lt, "mla_attn", gather=nxt[0])
        made = nxt[1](got)
        tp_next = (made[0][1], made[0][0])
    mix_rows = [full(ogf), full(ogb), full(orf), full(orb), (p, 256, C_GG // 256), (p, 256, C_RG // 256), full(mo)]
    m, = stage_fwd(fn_mix, mix_rows, [], [], [lw["gng"]], [(1024, BF)], "mix")
    a = matmul(m, lw["wout"], "nn", F32, tm, 1024, 1024, "proj_out")
    x1, h2 = stage_fwd(fn_post_mod, [full(x), full(a)], [tp[2], tp[4], tp[3]], [], [lw["ln1_g"], lw["ln1_b"]],
                       [(1024, F32), (1024, BF)], "post1")
    u = matmul(h2, lw["wup"], "nn", BF, tm, 1408, 1024, "ffn_up", b_outer=True)
    act, f = gate_fwd(u, lw["cw"], lw["cb"], lw["wdown"], lt, lc, "ffn_gate_down")
    if tp_next is None:
        x2, = stage_fwd(fn_post, [full(x1), full(f)], [tp[5]], [], [lw["ln2_g"], lw["ln2_b"]], [(1024, F32)], "post2_last")
        h1n = None
    else:
        x2, h1n = stage_fwd(fn_post_mod, [full(x1), full(f)], [tp[5], tp_next[0], tp_next[1]], [],
                            [lw["ln2_g"], lw["ln2_b"]], [(1024, F32), (1024, BF)], "post2")
    res = dict(x=x, h1=h1, p=p, ogf=ogf, orf=orf, sgf=sgf, srf=srf, ogb=ogb, orb=orb, sgb=sgb, srb=srb, q=q, k=k, v=v,
               mo=mo, lse=lse, m=m, a=a, x1=x1, h2=h2, u=u, act=act, f=f, mix_rows=mix_rows, prep_rows=prep_rows,
               prep_consts=prep_consts, prep_ws=prep_ws)
    return x2, h1n, res, made


def layer_backward(dx2, dh1n, res, tp, tp_next, lw, tabs, dims, exchange=()):
    nb, lc, lt = dims
    r = res
    t = dx2.shape[0]
    ncc, nch = lc // CHUNK, lt // CHUNK
    tm = _pick(t, (1024, 768, 512, 256))
    tkr = _pick(t, (2304, 1536, 1024, 768, 512))
    ret_tab = tabs[0]
    full = lambda a: (a, a.shape[1], 0)
    g = {}
    if tp_next is None:
        (dx1a, df), (dg2,), (g["ln2_g"], g["ln2_b"]) = stage_bwd(
            fn_post, [full(r["x1"]), full(r["f"])], [tp[5]], [], [lw["ln2_g"], lw["ln2_b"]], [dx2], [F32, BF], "post2_last_bwd")
        dnext = None
    else:
        (dx1a, df), (dg2, dsc1n, dsh1n), (g["ln2_g"], g["ln2_b"]) = stage_bwd(
            fn_post_mod, [full(r["x1"]), full(r["f"])], [tp[5], tp_next[0], tp_next[1]], [], [lw["ln2_g"], lw["ln2_b"]],
            [dx2, dh1n], [F32, BF], "post2_bwd")
        dnext = (dsc1n, dsh1n)
    g["ffn_down"] = matmul(r["act"], df, "tn", F32, 1408, 1024, tkr, "ffn_down_dw")
    dact = matmul(df, lw["wdown"], "nt", F32, tm, 1408, 1024, "ffn_down_dx", b_outer=True)
    dc, dcw = gate_bwd(r["u"], lw["cw"], lw["cb"], dact, lt, lc, "ffn_gate_bwd")
    g["ffn_conv_w"], g["ffn_conv_b"] = dcw[0:3], dcw[3]
    du = conv_transpose(dc, lw["cw"], lt, lc, "ffn_conv_t")
    dh2 = matmul(du, lw["wup"], "nt", F32, _pick(t, (512, 256)), 1024, 2 * D_FF, "ffn_up_dx")
    g["ffn_up"] = matmul(r["h2"], du, "tn", F32, 1024, 1408, tkr, "ffn_up_dw")
    (dxa, da), (dg1, dsc2, dsh2), (g["ln1_g"], g["ln1_b"]) = stage_bwd(
        fn_post_mod, [full(r["x"]), full(r["a"])], [tp[2], tp[4], tp[3]], [], [lw["ln1_g"], lw["ln1_b"]],
        [dx1a, dh2], [F32, BF], "post1_bwd")
    dm = matmul(da, lw["wout"], "nt", F32, tm, 1024, 1024, "proj_out_dx")
    g["w_out"] = matmul(r["m"], da, "tn", F32, 1024, 1024, tkr, "proj_out_dw")
    (dog, _, dor, _, dpg, dpr, dmo), _, (dgng,) = stage_bwd(
        fn_mix, r["mix_rows"], [], [], [lw["gng"]], [dm], [F32, None, F32, None, F32, F32, F32], "mix_bwd")
    g["gla_norm_g"] = jnp.sum(dgng.reshape(4, 64), axis=0)
    dq, dk, dv, landed = mla_bwd(r["q"], r["k"], r["v"], r["mo"], r["lse"], dmo, nb, lc, lt,
                                 "mla_attn_bwd" if exchange else "mla_attn_bwd_first", exchange=exchange)
    (dpq, dpkv, dplk), _, (dgq, dgkv, dwuq, dwuk, dwuv) = stage_bwd(
        fn_mla_prep, r["prep_rows"], [], r["prep_consts"], r["prep_ws"], [dq, dk, dv], [F32, F32, F32], "mla_prep_bwd")
    g["mla_q_norm_g"], g["mla_kv_norm_g"] = dgq.reshape(256), dgkv.reshape(128)
    g["mla_w_uq"] = dwuq.reshape(256, MLA_HEADS, LANES)[:, :, :96].reshape(256, MLA_HEADS * 96)
    g["mla_w_uk"] = dwuk.reshape(128, MLA_HEADS, LANES)[:, :, :64].reshape(128, MLA_HEADS * 64)
    g["mla_w_uv"] = dwuv
    s7, dgw0, dgb0, drd0 = scan_bwd(r["p"], ret_tab, lw["gw"][0], lw["gb"][0], lw["rdec"][0], r["sgf"], r["srf"], dog, dor,
                                    None, nb, ncc, nch, False, "scan_bwd_f")
    s7, dgw1, dgb1, drd1 = scan_bwd(r["p"], ret_tab, lw["gw"][1], lw["gb"][1], lw["rdec"][1], r["sgb"], r["srb"], dog, dor,
                                    s7, nb, ncc, nch, True, "scan_bwd_b")
    g["gla_gate_w"] = jnp.stack([dgw0[0:16], dgw1[16:32]])
    g["gla_gate_b"] = jnp.stack([dgb0[0], dgb1[0]])
    g["ret_decay"] = jnp.stack([jnp.sum(drd0.reshape(4, 32), axis=1), jnp.sum(drd1.reshape(4, 32), axis=1)])
    gq_, gk_, gv_, glrk, rq_, rk_, rv_ = s7
    pieces = [gq_, gk_, gv_, dpg, rq_, rk_, rv_, dpr, dpq, dpkv, glrk, dplk]
    dp, = stage_fwd(fn_assemble, [full(a) for a in pieces], [], [], [], [(D_INP, BF)], "dproj_assemble")
    dh1 = matmul(dp, lw["win"], "nt", F32, tm, 1024, D_INP, "proj_in_dx")
    g["w_in"] = unpermute_w_in(matmul(r["h1"], dp, "tn", F32, 1024, 1024, tkr, "proj_in_dw"))
    for n in ("ln1_g", "ln1_b", "ln2_g", "ln2_b"):
        g[n] = g[n].reshape(D_MODEL)
    dtp = [None, None, dg1, dsh2, dsc2, dg2]
    return dxa, dh1, dtp, dnext, g, landed


def local_step(x, c, ctx, c_ctx, w, loss_target, first_big, next_blocks, assemble, reduce_hooks=None):
    nb, l, _ = x.shape
    lc = ctx.shape[1]
    lt = lc + l
    dims = (nb, lc, lt)
    nct, nlt = lc // ROW_TILE, l // ROW_TILE
    tabs = rope_tables(lc, l)
    x0 = jnp.concatenate([ctx, x], axis=1).reshape(nb * lt, D_MODEL)
    s8 = jnp.concatenate([silu(c), jnp.zeros((4 - nb, D_MODEL), F32), silu(c_ctx)[None], jnp.zeros((3, D_MODEL), F32)], axis=0)

    def make_layer(i, big):
        lw = layer_weights(big, w, i)
        mod = matmul(s8, lw["ada_w"], "nn", F32, 8, 1536, 1024, "ada_mod") + w["ada_b"][i].astype(F32)[None]
        return tile_params(mod, nb, nct, nlt), lw

    made = make_layer(0, first_big)
    lws, tps = [], []
    h1, = stage_fwd(fn_modulate, [(x0, D_MODEL, 0)], [made[0][1], made[0][0]], [], [], [(D_MODEL, BF)], "mod_in")
    xs, ress = x0, []
    for i in range(DEPTH):
        tps.append(made[0])
        lws.append(made[1])
        nxt = None
        if i < DEPTH - 1:
            nxt = (next_blocks(i + 1), functools.partial(lambda got, j: make_layer(j, assemble(j, got)), j=i + 1))
        xs, h1, res, made = layer_forward(xs, h1, tps[i], lws[i], tabs, dims, nxt)
        ress.append(res)
    dx, lparts = loss_head(xs, loss_target.reshape(nb * l, D_MODEL), nct + nlt, nct, "loss_head")
    loss = jnp.sum(lparts[:, 0, 0])
    grads = [None] * DEPTH
    dtps = [None] * DEPTH
    reduced = [None] * DEPTH
    dh1, parts = None, ()
    for i in reversed(range(DEPTH)):
        tpn = None if i == DEPTH - 1 else (tps[i + 1][1], tps[i + 1][0])
        dx, dh1, dtp, dn, grads[i], landed = layer_backward(dx, dh1, ress[i], tps[i], tpn, lws[i], tabs, dims, parts)
        if dn is not None:
            dtps[i + 1][1], dtps[i + 1][0] = dn
        dtps[i] = dtp
        if reduce_hooks is not None:
            if parts:
                reduced[i + 1] = reduce_hooks[1](landed)
            parts = reduce_hooks[0]({n: grads[i].pop(n) for n in REDUCED})
    if reduce_hooks is not None:
        reduced[0] = reduce_hooks[1](exchange_chips(parts, "grad_exchange_last"))
    (dx0b,), (dsc1, dsh1), _ = stage_bwd(fn_modulate, [(x0, D_MODEL, 0)], [tps[0][1], tps[0][0]], [], [], [dh1], [F32], "mod_in_bwd")
    dtps[0][1], dtps[0][0] = dsc1, dsh1
    grad_x = (dx + dx0b).reshape(nb, lt, D_MODEL)[:, lc:]
    dmod = jnp.stack([tile_param_grads(d, nb, nct, nlt) for d in dtps])
    gw = {n: jnp.stack([grads[i][n] for i in range(DEPTH)]) for n in grads[0]}
    return loss, grad_x, gw, dmod, s8, lws, reduced


MESH_IDS = pl.DeviceIdType.MESH
ANY = pl.BlockSpec(memory_space=pl.ANY)


def _place():
    return lax.axis_index("x"), lax.axis_index("y"), lax.axis_index("c")


def _dma_sems(n, per):
    return pltpu.SemaphoreType.DMA((n, per))


def _gather8_steps(x_refs, out_refs, send_sems, recv_sems, local_sems):
    n = len(x_refs)
    x, y, c = _place()
    me, sibling = (x, y, c), (x, y, 1 - c)
    chips = [(1 - x, y), (x, 1 - y), (1 - x, 1 - y)]

    def copy(a, k, blk, to, own=False):
        slot = out_refs[a].at[4 * blk[0] + 2 * blk[1] + blk[2]]
        return pltpu.make_async_remote_copy(
            src_ref=x_refs[a] if own else slot, dst_ref=slot,
            send_sem=send_sems.at[a, k], recv_sem=recv_sems.at[a, k], device_id=to, device_id_type=MESH_IDS)

    def local(a):
        return pltpu.make_async_copy(x_refs[a], out_refs[a].at[4 * x + 2 * y + c], local_sems.at[a, 0])

    def first_copies():
        cps = []
        for a in range(n):
            cps.append(copy(a, 0, me, sibling, own=True))
            cps += [copy(a, 1 + j, me, (*chip, c), own=True) for j, chip in enumerate(chips)]
        return cps

    def start():
        for a in range(n):
            local(a).start()
        for cp in first_copies():
            cp.start()

    def finish():
        passed = []
        for j, chip in enumerate(chips):
            for a in range(n):
                copy(a, 1 + j, (*chip, c), me).wait_recv()
                passed.append(copy(a, 4 + j, (*chip, c), sibling))
                passed[-1].start()
        for a in range(n):
            copy(a, 0, sibling, me).wait_recv()
            for j, chip in enumerate(chips):
                copy(a, 4 + j, (*chip, 1 - c), me).wait_recv()
        for cp in first_copies() + passed:
            cp.wait_send()
        for a in range(n):
            local(a).wait()

    return start, finish


def all_gather8(blocks, name):
    n = len(blocks)

    def body(*refs):
        start, finish = _gather8_steps(refs[:n], refs[n:2 * n], *refs[2 * n:])
        start()
        finish()

    return pl.pallas_call(
        body, name=name, out_shape=[jax.ShapeDtypeStruct((8,) + b.shape, b.dtype) for b in blocks],
        in_specs=[ANY] * n, out_specs=[ANY] * n,
        scratch_shapes=[_dma_sems(n, 7), _dma_sems(n, 7), _dma_sems(n, 1)],
    )(*blocks)


def swap_cores(blocks, name):
    n = len(blocks)

    def body(*refs):
        x_refs, out_refs, (send_sems, recv_sems) = refs[:n], refs[n:2 * n], refs[2 * n:]
        x, y, c = _place()
        cps = [pltpu.make_async_remote_copy(src_ref=x_refs[a], dst_ref=out_refs[a], send_sem=send_sems.at[a, 0],
                                            recv_sem=recv_sems.at[a, 0], device_id=(x, y, 1 - c), device_id_type=MESH_IDS)
               for a in range(n)]
        for cp in cps:
            cp.start()
        for cp in cps:
            cp.wait()

    return pl.pallas_call(
        body, name=name, out_shape=[jax.ShapeDtypeStruct(b.shape, b.dtype) for b in blocks],
        in_specs=[ANY] * n, out_specs=[ANY] * n, scratch_shapes=[_dma_sems(n, 1), _dma_sems(n, 1)],
    )(*blocks)


def _exchange_steps(p_refs, out_refs, send_sems, recv_sems, local_sems):
    n = len(p_refs)
    x, y, c = _place()
    jm = 2 * x + y
    chips = [(1 - x, y), (x, 1 - y), (1 - x, 1 - y)]

    def local(a):
        return pltpu.make_async_copy(p_refs[a].at[jm], out_refs[a].at[jm], local_sems.at[a, 0])

    def sends():
        return [pltpu.make_async_remote_copy(
            src_ref=p_refs[a].at[2 * px + py], dst_ref=out_refs[a].at[jm], send_sem=send_sems.at[a, k],
            recv_sem=recv_sems.at[a, k], device_id=(px, py, c), device_id_type=MESH_IDS)
            for k, (px, py) in enumerate(chips) for a in range(n)]

    def start():
        for a in range(n):
            local(a).start()
        for cp in sends():
            cp.start()

    def finish():
        for k, (px, py) in enumerate(chips):
            for a in range(n):
                pltpu.make_async_remote_copy(
                    src_ref=p_refs[a].at[jm], dst_ref=out_refs[a].at[2 * px + py], send_sem=send_sems.at[a, k],
                    recv_sem=recv_sems.at[a, k], device_id=(px, py, c), device_id_type=MESH_IDS).wait_recv()
        for cp in sends():
            cp.wait_send()
        for a in range(n):
            local(a).wait()

    return start, finish


def exchange_chips(parts, name):
    n = len(parts)

    def body(*refs):
        start, finish = _exchange_steps(refs[:n], refs[n:2 * n], *refs[2 * n:])
        start()
        finish()

    return pl.pallas_call(
        body, name=name, out_shape=[jax.ShapeDtypeStruct(p.shape, p.dtype) for p in parts],
        in_specs=[ANY] * n, out_specs=[ANY] * n, scratch_shapes=[_dma_sems(n, 3), _dma_sems(n, 3), _dma_sems(n, 1)],
    )(*parts)


def _row_tile(rows, cols, itemsize=4, limit=1 << 21):
    top = min(rows, limit // (cols * itemsize)) // 16 * 16
    for cand in range(top, 0, -16):
        if rows % cand == 0:
            return cand
    return rows


def add_halves(a, b, kind, name):
    if kind == "col":
        m, k, n = a.shape
        n4 = n // 4
        tr = _row_tile(k, n4)
        in_spec = pl.BlockSpec((1, tr, n4), lambda j, h, i: (h, i, j))
        out_spec = pl.BlockSpec((1, 1, tr, n4), lambda j, h, i: (j, h, i, 0))
        grid, out_shape = (4, m, k // tr), (4, m, k, n4)
    elif kind == "row":
        m, k, n = a.shape
        k4 = k // 4
        tr = _row_tile(k4, n)
        nt = k4 // tr
        in_spec = pl.BlockSpec((1, tr, n), lambda j, h, i: (h, j * nt + i, 0))
        out_spec = pl.BlockSpec((1, 1, tr, n), lambda j, h, i: (j, h, i, 0))
        grid, out_shape = (4, m, nt), (4, m, k4, n)
    else:
        _, m, k, n = a.shape
        tr = _row_tile(k, n)
        in_spec = out_spec = pl.BlockSpec((1, 1, tr, n), lambda j, h, i: (j, h, i, 0))
        grid, out_shape = (4, m, k // tr), a.shape

    def body(a_ref, b_ref, s_ref):
        s_ref[...] = (a_ref[...] + b_ref[...].astype(F32)).astype(BF).reshape(s_ref.shape)

    return pl.pallas_call(body, name=name, grid=grid, in_specs=[in_spec, in_spec], out_specs=out_spec,
                          out_shape=jax.ShapeDtypeStruct(out_shape, BF))(a, b)


def sum_slots(a, name):
    s, m, k, n = a.shape
    tr = _row_tile(k, n, limit=(1 << 22) // s)

    def body(a_ref, o_ref):
        acc = a_ref[0].astype(F32)
        for j in range(1, s):
            acc = acc + a_ref[j].astype(F32)
        o_ref[...] = acc

    return pl.pallas_call(body, name=name, grid=(m, k // tr), in_specs=[pl.BlockSpec((s, 1, tr, n), lambda h, i: (0, h, i, 0))],
                          out_specs=pl.BlockSpec((1, tr, n), lambda h, i: (h, i, 0)),
                          out_shape=jax.ShapeDtypeStruct((m, k, n), F32))(a)


def sum_small(arrays, name):
    n = len(arrays)

    def body(*refs):
        for a_ref, o_ref in zip(refs[:n], refs[n:]):
            acc = a_ref[0]
            for j in range(1, 8):
                acc = acc + a_ref[j]
            o_ref[...] = acc

    return pl.pallas_call(body, name=name, out_shape=[jax.ShapeDtypeStruct(a.shape[1:], F32) for a in arrays])(*arrays)


COL_SHARDED = ("ada_w", "w_in", "mla_w_uq", "mla_w_uk", "mla_w_uv", "ffn_up", "ffn_conv_w")
ROW_SHARDED = ("w_out", "ffn_down")
GATHERED = ("ada_w", "w_in", "mla_w_uq", "mla_w_uk", "mla_w_uv", "w_out", "ffn_up", "ffn_down", "ffn_conv_w")
LAYER_GATHERED = GATHERED[:-1]
REDUCED = ("w_in", "mla_w_uq", "mla_w_uk", "mla_w_uv", "w_out", "ffn_up", "ffn_down")
SMALL = ("ada_b", "gla_gate_w", "gla_gate_b", "gla_norm_g", "ret_decay", "mla_q_norm_g", "mla_kv_norm_g",
         "ln1_g", "ln1_b", "ffn_conv_b", "ln2_g", "ln2_b")
WEIGHTS = ("c_ctx", "ada_w", "ada_b", "w_in", "gla_gate_w", "gla_gate_b", "gla_norm_g", "ret_decay", "mla_q_norm_g",
           "mla_kv_norm_g", "mla_w_uq", "mla_w_uk", "mla_w_uv", "w_out", "ln1_g", "ln1_b", "ffn_up", "ffn_conv_w",
           "ffn_conv_b", "ffn_down", "ln2_g", "ln2_b")
PACK = 16 * LANES
HALF_LAYERS = DEPTH // 2


def _pad_flat(v, n):
    return jnp.concatenate([v, jnp.zeros((n - v.shape[0],), v.dtype)]) if n > v.shape[0] else v


def _my_layers(a, c):
    return lax.dynamic_slice_in_dim(a, HALF_LAYERS * c, HALF_LAYERS, axis=0)


def layer_blocks(shards, l, c):
    out = []
    for n in LAYER_GATHERED:
        a = shards[n][l]
        out.append(lax.dynamic_slice_in_dim(a, c * (a.shape[0] // 2), a.shape[0] // 2, axis=0).astype(BF))
    return out


def layer_assemble(got):
    out = {}
    for n, g in zip(LAYER_GATHERED, got):
        _, k2, n4 = g.shape
        if n in ROW_SHARDED:
            out[n] = g.reshape(8 * k2, n4)
        else:
            out[n] = jnp.transpose(g.reshape(4, 2, k2, n4), (1, 2, 0, 3)).reshape(2 * k2, 4 * n4)
    return out


def gather_conv_taps(shard, c):
    got, = all_gather8([_my_layers(shard, c)], "gather_conv_taps")
    _, _, k, n4 = got.shape
    return jnp.transpose(got.reshape(4, 2, HALF_LAYERS, k, n4), (1, 2, 3, 0, 4)).reshape(DEPTH, k, 4 * n4)


def _reduce_kind(n, g):
    return "row" if n in ROW_SHARDED else ("col" if (g.shape[-1] // 4) % LANES == 0 else "pre")


def reduce_begin(g, c):
    keep, give, kinds = [], [], []
    for n in REDUCED:
        a, kind = g[n], _reduce_kind(n, g[n])
        if kind == "pre":
            a = jnp.transpose(a.reshape(a.shape[0], 4, a.shape[1] // 4), (1, 0, 2))
        axis = a.ndim - 1 if kind == "row" else a.ndim - 2
        half = a.shape[axis] // 2
        lead = (slice(None), None) if kind == "pre" else (None,)
        keep.append(lax.dynamic_slice_in_dim(a, half * c, half, axis=axis)[lead])
        give.append(lax.dynamic_slice_in_dim(a, half * (1 - c), half, axis=axis)[lead].astype(BF))
        kinds.append(kind)
    got = swap_cores(give, "grad_swap_cores")
    return [add_halves(a, b, kind, "grad_add_cores_" + n) for n, a, b, kind in zip(REDUCED, keep, got, kinds)]


def reduce_end(landed):
    return [sum_slots(a, "grad_sum_chips_" + n) for n, a in zip(REDUCED, landed)]


def reduce_assemble(layers, c):
    mine = [jnp.concatenate([layers[l][k] for l in range(DEPTH)], axis=0) for k in range(len(REDUCED))]
    theirs = swap_cores(mine, "grad_swap_back")
    out = {}
    for n, a, b in zip(REDUCED, mine, theirs):
        out[n] = jnp.concatenate([jnp.where(c == 0, a, b), jnp.where(c == 0, b, a)], axis=2 if n in ROW_SHARDED else 1)
    return out


def _pack_small(d, names):
    flat = jnp.concatenate([d[n].astype(F32).reshape(-1) for n in names])
    n = -(-flat.shape[0] // PACK) * PACK
    return _pad_flat(flat, n).reshape(n // LANES, LANES)


def _unpack_small(buf, like, names):
    flat, out, at = buf.reshape(-1), {}, 0
    for n in names:
        sz = like[n].size
        out[n] = flat[at:at + sz].reshape(like[n].shape)
        at += sz
    return out


def kernel(x, c, ctx, c_ctx, ada_w, ada_b, w_in, gla_gate_w, gla_gate_b, gla_norm_g, ret_decay, mla_q_norm_g, mla_kv_norm_g, mla_w_uq, mla_w_uk, mla_w_uv, w_out, ln1_g, ln1_b, ffn_up, ffn_conv_w, ffn_conv_b, ffn_down, ln2_g, ln2_b, loss_target, m_c_ctx, m_ada_w, m_ada_b, m_w_in, m_gla_gate_w, m_gla_gate_b, m_gla_norm_g, m_ret_decay, m_mla_q_norm_g, m_mla_kv_norm_g, m_mla_w_uq, m_mla_w_uk, m_mla_w_uv, m_w_out, m_ln1_g, m_ln1_b, m_ffn_up, m_ffn_conv_w, m_ffn_conv_b, m_ffn_down, m_ln2_g, m_ln2_b, v_c_ctx, v_ada_w, v_ada_b, v_w_in, v_gla_gate_w, v_gla_gate_b, v_gla_norm_g, v_ret_decay, v_mla_q_norm_g, v_mla_kv_norm_g, v_mla_w_uq, v_mla_w_uk, v_mla_w_uv, v_w_out, v_ln1_g, v_ln1_b, v_ffn_up, v_ffn_conv_w, v_ffn_conv_b, v_ffn_down, v_ln2_g, v_ln2_b):
    w = dict(c_ctx=c_ctx, ada_w=ada_w, ada_b=ada_b, w_in=w_in, gla_gate_w=gla_gate_w, gla_gate_b=gla_gate_b, gla_norm_g=gla_norm_g, ret_decay=ret_decay, mla_q_norm_g=mla_q_norm_g, mla_kv_norm_g=mla_kv_norm_g, mla_w_uq=mla_w_uq, mla_w_uk=mla_w_uk, mla_w_uv=mla_w_uv, w_out=w_out, ln1_g=ln1_g, ln1_b=ln1_b, ffn_up=ffn_up, ffn_conv_w=ffn_conv_w, ffn_conv_b=ffn_conv_b, ffn_down=ffn_down, ln2_g=ln2_g, ln2_b=ln2_b)
    m = dict(c_ctx=m_c_ctx, ada_w=m_ada_w, ada_b=m_ada_b, w_in=m_w_in, gla_gate_w=m_gla_gate_w, gla_gate_b=m_gla_gate_b, gla_norm_g=m_gla_norm_g, ret_decay=m_ret_decay, mla_q_norm_g=m_mla_q_norm_g, mla_kv_norm_g=m_mla_kv_norm_g, mla_w_uq=m_mla_w_uq, mla_w_uk=m_mla_w_uk, mla_w_uv=m_mla_w_uv, w_out=m_w_out, ln1_g=m_ln1_g, ln1_b=m_ln1_b, ffn_up=m_ffn_up, ffn_conv_w=m_ffn_conv_w, ffn_conv_b=m_ffn_conv_b, ffn_down=m_ffn_down, ln2_g=m_ln2_g, ln2_b=m_ln2_b)
    v = dict(c_ctx=v_c_ctx, ada_w=v_ada_w, ada_b=v_ada_b, w_in=v_w_in, gla_gate_w=v_gla_gate_w, gla_gate_b=v_gla_gate_b, gla_norm_g=v_gla_norm_g, ret_decay=v_ret_decay, mla_q_norm_g=v_mla_q_norm_g, mla_kv_norm_g=v_mla_kv_norm_g, mla_w_uq=v_mla_w_uq, mla_w_uk=v_mla_w_uk, mla_w_uv=v_mla_w_uv, w_out=v_w_out, ln1_g=v_ln1_g, ln1_b=v_ln1_b, ffn_up=v_ffn_up, ffn_conv_w=v_ffn_conv_w, ffn_conv_b=v_ffn_conv_b, ffn_down=v_ffn_down, ln2_g=v_ln2_g, ln2_b=v_ln2_b)
    xi, yi, ci = _place()
    chip = 2 * xi + yi

    whole = {n: w[n] for n in WEIGHTS if n not in GATHERED and n != "c_ctx"}
    whole["ffn_conv_w"] = gather_conv_taps(ffn_conv_w, ci)
    first_big = layer_assemble(all_gather8(layer_blocks(w, 0, ci), "gather_layer0"))
    loss, grad_x, gw, dmod, s8, lws, reduced = local_step(
        x, c, ctx, c_ctx, whole, loss_target, first_big,
        lambda l: layer_blocks(w, l, ci), lambda l, got: layer_assemble(got),
        (lambda g: reduce_begin(g, ci), reduce_end))
    loss = lax.psum(loss, ("x", "y", "c"))

    dsil = jnp.zeros((8, D_MODEL), F32)
    for i in range(DEPTH):
        dsil = dsil + matmul(dmod[i], lws[i]["ada_w"], "nt", F32, 8, 1024, 1536, "ada_dsilu")

    grads = reduce_assemble(reduced, ci)

    small = {n: gw[n] for n in SMALL if n != "ada_b"}
    small.update(dsil=dsil[4])
    names = tuple(small)
    conv_g = gw["ffn_conv_w"].reshape(DEPTH * 3, 2 * D_FF)
    ev_small, ev_dmod, ev_s8, ev_conv = all_gather8(
        [_pack_small(small, names), dmod.reshape(DEPTH * 8, 6 * D_MODEL), s8, conv_g], "gather_small")
    sm_small, sm_dmod, sm_conv = sum_small([ev_small, ev_dmod, ev_conv], "sum_small")
    summed = _unpack_small(sm_small, small, names)
    for n in SMALL:
        if n != "ada_b":
            grads[n] = summed[n]
    grads["ada_b"] = jnp.sum(sm_dmod.reshape(DEPTH, 8, 6 * D_MODEL)[:, :5], axis=1)
    sg = jax.nn.sigmoid(c_ctx)
    grads["c_ctx"] = summed["dsil"] * (sg * (1.0 + c_ctx * (1.0 - sg)))
    ccols = ffn_conv_w.shape[2]
    grads["ffn_conv_w"] = lax.dynamic_slice_in_dim(sm_conv.reshape(DEPTH, 3, 2 * D_FF), chip * ccols, ccols, axis=2)
    s_all = ev_s8.reshape(64, D_MODEL)
    d_all = jnp.transpose(ev_dmod.reshape(8, DEPTH, 8, 6 * D_MODEL), (1, 0, 2, 3)).reshape(DEPTH, 64, 6 * D_MODEL)
    cols = ada_w.shape[2]
    g_ada = []
    for i in range(DEPTH):
        d_mine = lax.dynamic_slice_in_dim(d_all[i], chip * cols, cols, axis=1)
        g_ada.append(matmul(s_all, d_mine, "tn", F32, 1024, cols, 64, "ada_dw"))
    grads["ada_w"] = jnp.stack(g_ada)

    delta, new_m, new_v = {}, {}, {}
    for n in GATHERED:
        shp = w[n].shape
        v2 = lambda a: a.reshape(-1, shp[-1])
        d_, m_, v_ = adamw(v2(w[n]), v2(grads[n]), v2(m[n]), v2(v[n]), "adamw_" + n)
        delta[n], new_m[n], new_v[n] = d_.reshape(shp), m_.reshape(shp), v_.reshape(shp)
    rep = tuple(n for n in WEIGHTS if n not in GATHERED)
    pk = lambda d: _pack_small({n: d[n] for n in rep}, rep)
    d_, m_, v_ = adamw(pk(w), pk(grads), pk(m), pk(v), "adamw_small")
    like = {n: w[n] for n in rep}
    delta.update(_unpack_small(d_, like, rep))
    new_m.update(_unpack_small(m_, like, rep))
    new_v.update(_unpack_small(v_, like, rep))
    grads = {n: grads[n].reshape(w[n].shape) for n in WEIGHTS}
    return (loss, grad_x, *[grads[n] for n in WEIGHTS], *[delta[n] for n in WEIGHTS], *[new_m[n] for n in WEIGHTS],
            *[new_v[n] for n in WEIGHTS])
```

```python
import functools

import jax
import jax.numpy as jnp
from jax import lax
from jax.experimental import pallas as pl
from jax.experimental.pallas import tpu as pltpu

F32 = jnp.float32
BF = jnp.bfloat16

D_MODEL = 1024
DEPTH = 4
GRID_W = 64
GLA_DK = 32
GLA_TAU = 16.0
RET_DK = 32
MLA_HEADS = 8
MLA_D_NOPE = 64
MLA_D_ROPE = 32
MLA_SCALE = (MLA_D_NOPE + MLA_D_ROPE) ** -0.5
D_FF = 2816
ROPE_BASE = 10000.0
EPS = 1e-6
ALPHA = (2 * DEPTH) ** 0.25
ADAM_LR, ADAM_B1, ADAM_B2, ADAM_EPS, ADAM_WD, ADAM_STEP = 0.001, 0.9, 0.999, 1e-08, 0.01, 10

ROW_TILE = 256
CHUNK = 64
GATE_ROWS = 128
ATTN_ROWS_FWD = 512
ATTN_ROWS_BWD = 512
LANES = 128

C_GQ, C_GK, C_GV, C_GG, C_RQ, C_RK, C_RV, C_RG, C_CQ, C_CKV, C_LK = 0, 128, 256, 512, 768, 896, 1024, 1280, 1536, 1792, 1920
D_INP = 2048
D_IN = 1984


def _dg(a, b, ca, cb):
    return lax.dot_general(a.astype(BF), b.astype(BF), (((ca,), (cb,)), ((), ())), preferred_element_type=F32)


@jax.custom_vjp
def mm_nn(a, b):
    return _dg(a, b, 1, 0)


@jax.custom_vjp
def mm_nt(a, b):
    return _dg(a, b, 1, 1)


@jax.custom_vjp
def mm_tn(a, b):
    return _dg(a, b, 0, 0)


mm_nn.defvjp(lambda a, b: (_dg(a, b, 1, 0), (a, b)),
             lambda r, g: (mm_nt(g, r[1]).astype(r[0].dtype), mm_tn(r[0], g).astype(r[1].dtype)))
mm_nt.defvjp(lambda a, b: (_dg(a, b, 1, 1), (a, b)),
             lambda r, g: (mm_nn(g, r[1]).astype(r[0].dtype), mm_tn(g, r[0]).astype(r[1].dtype)))
mm_tn.defvjp(lambda a, b: (_dg(a, b, 0, 0), (a, b)),
             lambda r, g: (mm_nt(r[1], g).astype(r[0].dtype), mm_nn(r[0], g).astype(r[1].dtype)))


def _split3(x):
    h = x.astype(BF)
    r = x - h.astype(F32)
    m = r.astype(BF)
    lo = (r - m.astype(F32)).astype(BF)
    return h, m, lo


def _exact(x, mat, left):
    h, m, lo = _split3(x)
    if left:
        d = lambda t: lax.dot_general(mat, t, (((1,), (0,)), ((), ())), preferred_element_type=F32)
    else:
        d = lambda t: lax.dot_general(t, mat, (((1,), (0,)), ((), ())), preferred_element_type=F32)
    return (d(lo) + d(m)) + d(h)


def _iota(shape, axis):
    return lax.broadcasted_iota(jnp.int32, shape, axis)


def _tri(n, upper):
    r, c = _iota((n, n), 0), _iota((n, n), 1)
    return jnp.where((c >= r) if upper else (r >= c), 1.0, 0.0).astype(BF)


@functools.partial(jax.custom_vjp, nondiff_argnums=(1,))
def cumsum_rows(x, upper):
    return _exact(x, _tri(x.shape[0], upper), True)


cumsum_rows.defvjp(lambda x, upper: (cumsum_rows(x, upper), None),
                   lambda upper, r, g: (cumsum_rows(g, not upper),))


def _seg(n, w):
    shift = w.bit_length() - 1
    r, c = _iota((n, n), 0), _iota((n, n), 1)
    return jnp.where(lax.shift_right_logical(r, shift) == lax.shift_right_logical(c, shift), 1.0, 0.0).astype(BF)


@functools.partial(jax.custom_vjp, nondiff_argnums=(1,))
def seg_sum(x, w):
    return _exact(x, _seg(x.shape[1], w), False)


seg_sum.defvjp(lambda x, w: (seg_sum(x, w), None), lambda w, r, g: (seg_sum(g, w),))


def _place_mat(transpose):
    shape = (8 * LANES, LANES) if transpose else (LANES, 8 * LANES)
    r, c = _iota(shape, 0), _iota(shape, 1)
    src, dst = (c, r) if transpose else (r, c)
    dl = jnp.bitwise_and(dst, LANES - 1)
    ok = (dl >= 64) & (dl < 96) & (src == dl - 32)
    return jnp.where(ok, 1.0, 0.0).astype(BF)


@jax.custom_vjp
def place_kr(x):
    return _dg(x, _place_mat(False), 1, 0)


place_kr.defvjp(lambda x: (place_kr(x), None), lambda r, g: (_dg(g, _place_mat(True), 1, 0),))


@functools.partial(jax.custom_vjp, nondiff_argnums=(1,))
def lane_roll(x, s):
    return pltpu.roll(x, s, 1)


lane_roll.defvjp(lambda x, s: (pltpu.roll(x, s, 1), None),
                 lambda s, r, g: (pltpu.roll(g, (g.shape[1] - s) % g.shape[1], 1),))


def rope(x, tab, d):
    cos, sa, sb = tab
    return x * cos + lane_roll(x, LANES - d) * sa + lane_roll(x, d) * sb


def silu(x):
    return x * jax.nn.sigmoid(x)


def log_sigmoid(z):
    return jnp.minimum(z, 0.0) - jnp.log(1.0 + jnp.exp(-jnp.abs(z)))


def layer_norm(x, g, b):
    mu = jnp.mean(x, axis=-1, keepdims=True)
    xc = x - mu
    var = jnp.mean(xc * xc, axis=-1, keepdims=True)
    return xc * lax.rsqrt(var + EPS) * g + b


def matmul(a, b, mode, out_dtype, tm, tn, tk, name, b_outer=False):
    ij = (lambda f: (lambda g0, g1, kk: f(g1, g0, kk))) if b_outer else (lambda f: f)
    if mode == "nn":
        (m, k), (k2, n) = a.shape, b.shape
        a_spec = pl.BlockSpec((tm, tk), ij(lambda i, j, kk: (i, kk)))
        b_spec = pl.BlockSpec((tk, tn), ij(lambda i, j, kk: (kk, j)))
        ca, cb = 1, 0
    elif mode == "nt":
        (m, k), (n, k2) = a.shape, b.shape
        a_spec = pl.BlockSpec((tm, tk), ij(lambda i, j, kk: (i, kk)))
        b_spec = pl.BlockSpec((tn, tk), ij(lambda i, j, kk: (j, kk)))
        ca, cb = 1, 1
    else:
        (k, m), (k2, n) = a.shape, b.shape
        a_spec = pl.BlockSpec((tk, tm), ij(lambda i, j, kk: (kk, i)))
        b_spec = pl.BlockSpec((tk, tn), ij(lambda i, j, kk: (kk, j)))
        ca, cb = 0, 0
    assert k == k2 and m % tm == 0 and n % tn == 0 and k % tk == 0, (name, a.shape, b.shape, tm, tn, tk)
    nk = k // tk
    grid = (n // tn, m // tm, nk) if b_outer else (m // tm, n // tn, nk)

    def body(a_ref, b_ref, o_ref, *acc):
        part = _dg(a_ref[...], b_ref[...], ca, cb)
        if nk == 1:
            o_ref[...] = part.astype(o_ref.dtype)
            return
        acc_ref, = acc
        kk = pl.program_id(2)

        @pl.when(kk == 0)
        def _():
            acc_ref[...] = part

        @pl.when(kk > 0)
        def _():
            acc_ref[...] += part

        @pl.when(kk == nk - 1)
        def _():
            o_ref[...] = acc_ref[...].astype(o_ref.dtype)

    return pl.pallas_call(
        body, name=name, grid=grid,
        in_specs=[a_spec, b_spec], out_specs=pl.BlockSpec((tm, tn), ij(lambda i, j, kk: (i, j))),
        out_shape=jax.ShapeDtypeStruct((m, n), out_dtype),
        scratch_shapes=[] if nk == 1 else [pltpu.VMEM((tm, tn), F32)],
    )(a, b)


def _stage_specs(rows, tps, consts, ws):
    specs, args = [], []
    for arr, width, cb in rows:
        specs.append(pl.BlockSpec((ROW_TILE, width), functools.partial(lambda i, cb: (i, cb), cb=cb)))
        args.append(arr)
    for arr in tps:
        specs.append(pl.BlockSpec((1, 1, arr.shape[2]), lambda i: (i, 0, 0)))
        args.append(arr)
    for arr, period in consts:
        specs.append(pl.BlockSpec((ROW_TILE, arr.shape[1]), functools.partial(lambda i, p: (i % p, 0), p=period)))
        args.append(arr)
    for arr in ws:
        specs.append(pl.BlockSpec(arr.shape, functools.partial(lambda i, nd: (0,) * nd, nd=arr.ndim)))
        args.append(arr)
    return specs, args


def _stage_load(refs, n_rows, n_tps, n_consts, n_ws):
    it = iter(refs)
    rows = [next(it)[...].astype(F32) for _ in range(n_rows)]
    tps = [next(it)[0].astype(F32) for _ in range(n_tps)]
    consts = [next(it)[...].astype(F32) for _ in range(n_consts)]
    ws = [next(it)[...].astype(F32) for _ in range(n_ws)]
    return rows, tps, consts, ws


def stage_fwd(fn, rows, tps, consts, ws, outs, name):
    n_tiles = rows[0][0].shape[0] // ROW_TILE
    specs, args = _stage_specs(rows, tps, consts, ws)
    counts = (len(rows), len(tps), len(consts), len(ws))

    def body(*refs):
        r, t, c, w = _stage_load(refs[:sum(counts)], *counts)
        res = fn(r, t, c, w)
        for o_ref, o in zip(refs[sum(counts):], res):
            o_ref[...] = o.astype(o_ref.dtype)

    res = pl.pallas_call(
        body, name=name, grid=(n_tiles,), in_specs=specs,
        out_specs=[pl.BlockSpec((ROW_TILE, wd), lambda i: (i, 0)) for wd, _ in outs],
        out_shape=[jax.ShapeDtypeStruct((n_tiles * ROW_TILE, wd), dt) for wd, dt in outs],
    )(*args)
    return list(res)


def stage_bwd(fn, rows, tps, consts, ws, cts, row_grads, name):
    n_tiles = rows[0][0].shape[0] // ROW_TILE
    specs, args = _stage_specs(rows, tps, consts, ws)
    counts = (len(rows), len(tps), len(consts), len(ws))
    n_in = sum(counts)
    for ct in cts:
        specs.append(pl.BlockSpec((ROW_TILE, ct.shape[1]), lambda i: (i, 0)))
        args.append(ct)
    want = [k for k, dt in enumerate(row_grads) if dt is not None]
    out_specs = [pl.BlockSpec((ROW_TILE, rows[k][1]), lambda i: (i, 0)) for k in want]
    out_shape = [jax.ShapeDtypeStruct((n_tiles * ROW_TILE, rows[k][1]), row_grads[k]) for k in want]
    out_specs += [pl.BlockSpec((1, 1, a.shape[2]), lambda i: (i, 0, 0)) for a in tps]
    out_shape += [jax.ShapeDtypeStruct((n_tiles, 1, a.shape[2]), F32) for a in tps]
    out_specs += [pl.BlockSpec(a.shape, functools.partial(lambda i, nd: (0,) * nd, nd=a.ndim)) for a in ws]
    out_shape += [jax.ShapeDtypeStruct(a.shape, F32) for a in ws]

    def body(*refs):
        r, t, c, w = _stage_load(refs[:n_in], *counts)
        g = [ref[...].astype(F32) for ref in refs[n_in:n_in + len(cts)]]
        _, vjp = jax.vjp(lambda r_, t_, w_: fn(r_, t_, c, w_), r, t, w)
        dr, dt, dw = vjp(g)
        o = iter(refs[n_in + len(cts):])
        for k in want:
            ref = next(o)
            ref[...] = dr[k].astype(ref.dtype)
        for v in dt:
            next(o)[0] = v
        first = pl.program_id(0) == 0
        for v in dw:
            ref = next(o)

            @pl.when(first)
            def _():
                ref[...] = v

            @pl.when(jnp.logical_not(first))
            def _():
                ref[...] += v

    res = pl.pallas_call(body, name=name, grid=(n_tiles,), in_specs=specs, out_specs=out_specs, out_shape=out_shape)(*args)
    res = list(res)
    drows = [None] * len(rows)
    for k in want:
        drows[k] = res.pop(0)
    dtps = [res.pop(0) for _ in tps]
    dws = [res.pop(0) for _ in ws]
    return drows, dtps, dws


def fn_modulate(rows, tps, consts, ws):
    (x,), (sc, sh) = rows, tps
    return [x * (1.0 + sc) + sh]


def fn_post(rows, tps, consts, ws):
    (x, a), (g,), (lng, lnb) = rows, tps, ws
    return [layer_norm(ALPHA * x + g * a, lng, lnb)]


def fn_mix(rows, tps, consts, ws):
    ogf, ogb, orf, orb, pg, pr, mo = rows
    gng, = ws
    og = ogf + ogb
    out_g = og * lax.rsqrt(seg_sum(og * og, 64) * (1.0 / 64) + EPS) * gng * silu(pg)
    o = orf + orb
    oc = o - seg_sum(o, 64) * (1.0 / 64)
    out_r = oc * lax.rsqrt(seg_sum(oc * oc, 64) * (1.0 / 64) + EPS) * silu(pr)
    return [jnp.concatenate([out_g, out_r, mo], axis=-1)]


def fn_mla_prep(rows, tps, consts, ws):
    pq, pkv, plk = rows
    gq, gkv, wuq, wuk, wuv = ws
    qtab, ktab = consts[0:3], consts[3:6]
    cq = pq * lax.rsqrt(jnp.mean(pq * pq, axis=-1, keepdims=True) + EPS) * gq
    qp = mm_nn(cq, wuq)
    q = jnp.concatenate([rope(qp[:, h * LANES:(h + 1) * LANES], qtab, 8) * MLA_SCALE for h in range(MLA_HEADS)], axis=-1)
    ckv = pkv * lax.rsqrt(jnp.mean(pkv * pkv, axis=-1, keepdims=True) + EPS) * gkv
    k = mm_nn(ckv, wuk) + place_kr(rope(plk, ktab, 8))
    v = mm_nn(ckv, wuv)
    return [q, k, v]


def fn_assemble(rows, tps, consts, ws):
    gq, gk, gv, gg, rq, rk, rv, rg, cq, ckv, lk1, lk2 = rows
    return [jnp.concatenate([gq, gk, gv, gg, rq, rk, rv, rg, cq, ckv, lk1 + lk2], axis=-1)]


def _head_masks():
    hm = (lax.shift_right_logical(_iota((4, 1, LANES), 2), 5) == _iota((4, 1, LANES), 0)).astype(F32)
    vm = (lax.shift_right_logical(_iota((4, 1, 256), 2), 6) == _iota((4, 1, 256), 0)).astype(F32)
    bd = (lax.shift_right_logical(_iota((256, LANES), 0), 6) == lax.shift_right_logical(_iota((256, LANES), 1), 5)).astype(F32)
    return hm, vm, bd


def chunk_step(s, q, k, v, la, upper):
    hm, vm, bd = _head_masks()
    t, u = _iota((4 * CHUNK, CHUNK), 0), _iota((4 * CHUNK, CHUNK), 1)
    t = jnp.bitwise_and(t, CHUNK - 1)
    causal = (u >= t) if upper else (t >= u)
    if la.shape[0] == 1:
        pos = _iota((CHUNK, LANES), 0)
        b = ((CHUNK - pos) if upper else (pos + 1)).astype(F32) * la
        bend = float(CHUNK) * la
    else:
        b = cumsum_rows(la, upper)
        bend = jnp.sum(la, axis=0, keepdims=True)
    half = 0.5 * bend
    qd = q * jnp.exp(b - half)
    kd = k * jnp.exp(half - b)
    qe = (qd[None] * hm).reshape(4 * CHUNK, LANES)
    att = jnp.where(causal, mm_nt(qe, kd), 0.0)
    o_intra = (mm_nn(att, v).reshape(4, CHUNK, 256) * vm).sum(0)
    o = mm_nt(q * jnp.exp(b), s) + o_intra
    s_new = (s * jnp.exp(bend) + mm_tn(v, k * jnp.exp(bend - b))) * bd
    return o, s_new


def scan_step(sg, sr, q, k, v, lrk, rq, rk, rv, gw, gb, rdec, tab, upper):
    la_g = log_sigmoid(mm_nn(lrk, gw) + gb) * (1.0 / GLA_TAU)
    og, sg2 = chunk_step(sg, q * GLA_DK ** -0.5, k, v, la_g, upper)
    la_r = log_sigmoid(rdec)
    orr, sr2 = chunk_step(sr, rope(rq, tab, 16), rope(rk * RET_DK ** -0.5, tab, 16), rv, la_r, upper)
    return og, orr, sg2, sr2


def _chunk_of(n, ncc, nch, reverse):
    if not reverse:
        return n
    return jnp.where(n < ncc, ncc - 1 - n, nch - 1 + ncc - n)


def _scan_in_specs(p3, tabs, gw, gb, rdec, cidx):
    nb = p3.shape[0]

    def blk(width, cb):
        return pl.BlockSpec((nb, CHUNK, width), lambda m: (0, cidx(m), cb))

    specs = [blk(128, C_GQ // 128), blk(128, C_GK // 128), blk(256, C_GV // 256), blk(128, C_LK // 128),
             blk(128, C_RQ // 128), blk(128, C_RK // 128), blk(256, C_RV // 256)]
    args = [p3] * 7
    for t in tabs:
        specs.append(pl.BlockSpec((CHUNK, LANES), lambda m: (cidx(m), 0)))
        args.append(t)
    for w in (gw, gb, rdec):
        specs.append(pl.BlockSpec(w.shape, lambda m: (0, 0)))
        args.append(w)
    return specs, args


def scan_fwd(p, tabs, gw, gb, rdec, nb, ncc, nch, reverse, name):
    cidx = lambda n: _chunk_of(n, ncc, nch, reverse)
    t = p.shape[0]
    specs, args = _scan_in_specs(p.reshape(nb, t // nb, p.shape[1]), tabs, gw, gb, rdec, cidx)

    def body(q, k, v, lrk, rq, rk, rv, tc, ta, tb, gw_r, gb_r, rd_r, og_r, or_r, sgo_r, sro_r, sg, sr):
        @pl.when(pl.program_id(0) == 0)
        def _():
            sg[...] = jnp.zeros_like(sg)
            sr[...] = jnp.zeros_like(sr)

        sgo_r[0] = sg[...]
        sro_r[0] = sr[...]
        ld = lambda r: r[...].astype(F32)
        tab, gw_, gb_, rd_ = (ld(tc), ld(ta), ld(tb)), ld(gw_r), ld(gb_r), ld(rd_r)
        for b in range(nb):
            lb = lambda r: r[b].astype(F32)
            og, orr, sg2, sr2 = scan_step(sg[b], sr[b], lb(q), lb(k), lb(v), lb(lrk), lb(rq), lb(rk), lb(rv),
                                          gw_, gb_, rd_, tab, reverse)
            og_r[b] = og
            or_r[b] = orr
            sg[b] = sg2
            sr[b] = sr2

    row_out = pl.BlockSpec((nb, CHUNK, 256), lambda n: (0, cidx(n), 0))
    st_out = pl.BlockSpec((1, nb, 256, LANES), lambda n: (n, 0, 0, 0))
    og, orr, sgs, srs = pl.pallas_call(
        body, name=name, grid=(nch,), in_specs=specs, out_specs=[row_out, row_out, st_out, st_out],
        out_shape=[jax.ShapeDtypeStruct((nb, t // nb, 256), F32)] * 2 + [jax.ShapeDtypeStruct((nch, nb, 256, LANES), F32)] * 2,
        scratch_shapes=[pltpu.VMEM((nb, 256, LANES), F32)] * 2,
    )(*args)
    return og.reshape(t, 256), orr.reshape(t, 256), sgs, srs


def scan_bwd(p, tabs, gw, gb, rdec, sg_in, sr_in, dog, dor, prev, nb, ncc, nch, reverse, name):
    step = lambda m: nch - 1 - m
    cidx = lambda m: _chunk_of(step(m), ncc, nch, reverse)
    t = p.shape[0]
    lt = t // nb
    specs, args = _scan_in_specs(p.reshape(nb, lt, p.shape[1]), tabs, gw, gb, rdec, cidx)
    st_spec = pl.BlockSpec((1, nb, 256, LANES), lambda m: (step(m), 0, 0, 0))
    specs += [st_spec, st_spec]
    args += [sg_in, sr_in]
    row = lambda width: pl.BlockSpec((nb, CHUNK, width), lambda m: (0, cidx(m), 0))
    specs += [row(256), row(256)]
    args += [dog.reshape(nb, lt, 256), dor.reshape(nb, lt, 256)]
    widths = (128, 128, 256, 128, 128, 128, 256)
    if prev is not None:
        specs += [row(wd) for wd in widths]
        args += [a.reshape(nb, lt, a.shape[1]) for a in prev]
    n_prev = 0 if prev is None else 7

    def body(*refs):
        (q, k, v, lrk, rq, rk, rv, tc, ta, tb, gw_r, gb_r, rd_r, sgi, sri, dog_r, dor_r), rest = refs[:17], refs[17:]
        prev_r, rest = rest[:n_prev], rest[n_prev:]
        outs, (dgw_r, dgb_r, drd_r, dsg, dsr) = rest[:7], rest[7:]
        first = pl.program_id(0) == 0

        @pl.when(first)
        def _():
            dsg[...] = jnp.zeros_like(dsg)
            dsr[...] = jnp.zeros_like(dsr)

        ld = lambda r: r[...].astype(F32)
        tab, gw_, gb_, rd_ = (ld(tc), ld(ta), ld(tb)), ld(gw_r), ld(gb_r), ld(rd_r)
        wsum = None
        for b in range(nb):
            lb = lambda r: r[b].astype(F32)
            prim = (sgi[0, b], sri[0, b], lb(q), lb(k), lb(v), lb(lrk), lb(rq), lb(rk), lb(rv), gw_, gb_, rd_)
            _, vjp = jax.vjp(lambda *a: scan_step(*a, tab, reverse), *prim)
            g = vjp((lb(dog_r), lb(dor_r), dsg[b], dsr[b]))
            dsg[b] = g[0]
            dsr[b] = g[1]
            for j in range(7):
                val = g[2 + j]
                if n_prev:
                    val = val + prev_r[j][b]
                outs[j][b] = val
            wsum = g[9:12] if wsum is None else tuple(a + c for a, c in zip(wsum, g[9:12]))
        for ref, val in zip((dgw_r, dgb_r, drd_r), wsum):
            @pl.when(first)
            def _():
                ref[...] = val

            @pl.when(jnp.logical_not(first))
            def _():
                ref[...] += val

    wspec = lambda w: pl.BlockSpec(w.shape, lambda m: (0, 0))
    res = pl.pallas_call(
        body, name=name, grid=(nch,), in_specs=specs,
        out_specs=[row(wd) for wd in widths] + [wspec(gw), wspec(gb), wspec(rdec)],
        out_shape=[jax.ShapeDtypeStruct((nb, lt, wd), F32) for wd in widths]
        + [jax.ShapeDtypeStruct(w.shape, F32) for w in (gw, gb, rdec)],
        scratch_shapes=[pltpu.VMEM((nb, 256, LANES), F32)] * 2,
    )(*args)
    return tuple(a.reshape(t, a.shape[2]) for a in res[:7]), res[7], res[8], res[9]


def _attn_tiles(lc, lt):
    nct = lc // ROW_TILE
    return nct, (lt - lc) // ROW_TILE


def _attn_loop(tile, lc, lt, lat_rows):
    for i in range(lc // ROW_TILE):
        tile(i * ROW_TILE, lc, ROW_TILE)

    def lat(i, carry):
        tile(pl.multiple_of(lc + i * lat_rows, ROW_TILE), lt, lat_rows)
        return carry

    lax.fori_loop(0, (lt - lc) // lat_rows, lat, 0)


def mla_fwd(q, k, v, nb, lc, lt, name, gather=()):
    ng = len(gather)

    def body(q_ref, k_ref, v_ref, *rest):
        x_refs, (o_ref, lse_ref), out_refs, sems = rest[:ng], rest[ng:ng + 2], rest[ng + 2:2 * ng + 2], rest[2 * ng + 2:]
        if ng:
            start, finish = _gather8_steps(x_refs, out_refs, *sems)
            pl.when((pl.program_id(0) == 0) & (pl.program_id(1) == 0))(start)

        def tile(r0, nk, nrows):
            rows = pl.ds(r0, nrows)
            lane = _iota((nrows, LANES), 1)
            outs, lse = [], jnp.zeros((nrows, LANES), F32)
            for j in range(2):
                s = _dg(q_ref[rows, j * LANES:(j + 1) * LANES], k_ref[0:nk, j * LANES:(j + 1) * LANES], 1, 1)
                m = jnp.max(s, axis=-1, keepdims=True)
                p = jnp.exp(s - m)
                l = jnp.sum(p, axis=-1, keepdims=True)
                outs.append(_dg(p, v_ref[0:nk, j * 64:(j + 1) * 64], 1, 0) * (1.0 / l))
                lse = jnp.where(lane == j, m + jnp.log(l), lse)
            o_ref[rows, :] = jnp.concatenate(outs, axis=-1)
            lse_ref[rows, :] = lse

        _attn_loop(tile, lc, lt, ATTN_ROWS_FWD)
        if ng:
            pl.when((pl.program_id(0) == nb - 1) & (pl.program_id(1) == MLA_HEADS // 2 - 1))(finish)

    pair = lambda width: pl.BlockSpec((lt, width), lambda b, h: (b, h))
    res = pl.pallas_call(
        body, name=name, grid=(nb, MLA_HEADS // 2), in_specs=[pair(2 * LANES), pair(2 * LANES), pair(LANES)] + [ANY] * ng,
        out_specs=[pair(LANES), pair(LANES)] + [ANY] * ng,
        out_shape=[jax.ShapeDtypeStruct((nb * lt, MLA_HEADS * 64), F32), jax.ShapeDtypeStruct((nb * lt, MLA_HEADS // 2 * LANES), F32)]
        + [jax.ShapeDtypeStruct((8,) + g.shape, g.dtype) for g in gather],
        scratch_shapes=[_dma_sems(ng, 7), _dma_sems(ng, 7), _dma_sems(ng, 1)] if ng else [],
    )(q, k, v, *gather)
    return res[0], res[1], list(res[2:])


def mla_bwd(q, k, v, o, lse, do, nb, lc, lt, name, exchange=()):
    ne = len(exchange)

    def body(q_ref, k_ref, v_ref, o_ref, lse_ref, do_ref, *rest):
        p_refs, (dq_ref, dk_ref, dv_ref), rest = rest[:ne], rest[ne:ne + 3], rest[ne + 3:]
        out_refs, (dka, dva), sems = rest[:ne], rest[ne:ne + 2], rest[ne + 2:]
        if ne:
            start, finish = _exchange_steps(p_refs, out_refs, *sems)
            pl.when((pl.program_id(0) == 0) & (pl.program_id(1) == 0))(start)
        dka[...] = jnp.zeros_like(dka)
        dva[...] = jnp.zeros_like(dva)

        def tile(r0, nk, nrows):
            rows = pl.ds(r0, nrows)
            dqs = []
            for j in range(2):
                qj, kj = q_ref[rows, j * LANES:(j + 1) * LANES], k_ref[0:nk, j * LANES:(j + 1) * LANES]
                vj, doj = v_ref[0:nk, j * 64:(j + 1) * 64], do_ref[rows, j * 64:(j + 1) * 64]
                p = jnp.exp(_dg(qj, kj, 1, 1) - lse_ref[rows, j:j + 1])
                dsum = jnp.sum(doj * o_ref[rows, j * 64:(j + 1) * 64], axis=-1, keepdims=True)
                ds = p * (_dg(doj, vj, 1, 1) - dsum)
                dqs.append(_dg(ds, kj, 1, 0))
                dka[j, 0:nk, :] += _dg(ds, qj, 0, 0)
                dva[j, 0:nk, :] += _dg(p, doj, 0, 0)
            dq_ref[rows, :] = jnp.concatenate(dqs, axis=-1)

        _attn_loop(tile, lc, lt, ATTN_ROWS_BWD)
        dk_ref[...] = jnp.concatenate([dka[0], dka[1]], axis=-1)
        dv_ref[...] = jnp.concatenate([dva[0], dva[1]], axis=-1)
        if ne:
            pl.when((pl.program_id(0) == nb - 1) & (pl.program_id(1) == MLA_HEADS // 2 - 1))(finish)

    t = nb * lt
    pair = lambda width: pl.BlockSpec((lt, width), lambda b, h: (b, h))
    res = pl.pallas_call(
        body, name=name, grid=(nb, MLA_HEADS // 2),
        in_specs=[pair(2 * LANES), pair(2 * LANES), pair(LANES), pair(LANES), pair(LANES), pair(LANES)] + [ANY] * ne,
        out_specs=[pair(2 * LANES), pair(2 * LANES), pair(LANES)] + [ANY] * ne,
        out_shape=[jax.ShapeDtypeStruct((t, MLA_HEADS * LANES), F32), jax.ShapeDtypeStruct((t, MLA_HEADS * LANES), F32),
                   jax.ShapeDtypeStruct((t, MLA_HEADS * 64), F32)] + [jax.ShapeDtypeStruct(e.shape, e.dtype) for e in exchange],
        scratch_shapes=[pltpu.VMEM((2, lt, LANES), F32), pltpu.VMEM((2, lt, 64), F32)]
        + ([_dma_sems(ne, 3), _dma_sems(ne, 3), _dma_sems(ne, 1)] if ne else []),
    )(q, k, v, o, lse, do, *exchange)
    return res[0], res[1], res[2], list(res[3:])


HALO = 16


def _gate_specs(u, per_batch, lc):
    gh = GATE_ROWS // HALO
    nh = u.shape[0] // HALO
    width = u.shape[1]
    main = pl.BlockSpec((GATE_ROWS, width), lambda i: (i, 0))
    prev = pl.BlockSpec((HALO, width), lambda i: (jnp.maximum(i * gh - 1, 0), 0))
    nxt = pl.BlockSpec((HALO, width), lambda i: (jnp.minimum((i + 1) * gh, nh - 1), 0))
    return main, prev, nxt


def _seg_edges(per_batch, lc):
    j = pl.program_id(0) % (per_batch // GATE_ROWS)
    first = (j == 0) | (j == lc // GATE_ROWS)
    last = (j == lc // GATE_ROWS - 1) | (j == per_batch // GATE_ROWS - 1)
    return first, last


def _shifted(x, prev_ref, next_ref, first, last):
    rows = _iota(x.shape, 0)
    before = jnp.where(first, 0.0, prev_ref[HALO - 1:HALO, :].astype(F32))
    after = jnp.where(last, 0.0, next_ref[0:1, :].astype(F32))
    xm = jnp.where(rows == 0, before, pltpu.roll(x, 1, 0))
    xp = jnp.where(rows == x.shape[0] - 1, after, pltpu.roll(x, x.shape[0] - 1, 0))
    return xm, xp


def _whole(a):
    return pl.BlockSpec(a.shape, lambda i: (0,) * a.ndim)


def gate_fwd(u, cw, cb, wdown, per_batch, lc, name):
    main, prev, nxt = _gate_specs(u, per_batch, lc)
    f = u.shape[1] // 2

    def body(u_ref, p_ref, n_ref, w_ref, b_ref, wd_ref, act_ref, f_ref):
        first, last = _seg_edges(per_batch, lc)
        x = u_ref[...].astype(F32)
        xm, xp = _shifted(x, p_ref, n_ref, first, last)
        c = w_ref[0:1, :] * xm + w_ref[1:2, :] * x + w_ref[2:3, :] * xp + b_ref[...]
        act = silu(c[:, :f]) * c[:, f:]
        act_ref[...] = act.astype(act_ref.dtype)
        f_ref[...] = _dg(act, wd_ref[...], 1, 0)

    return pl.pallas_call(
        body, name=name, grid=(u.shape[0] // GATE_ROWS,),
        in_specs=[main, prev, nxt, _whole(cw), _whole(cb), _whole(wdown)],
        out_specs=[pl.BlockSpec((GATE_ROWS, f), lambda i: (i, 0)), pl.BlockSpec((GATE_ROWS, wdown.shape[1]), lambda i: (i, 0))],
        out_shape=[jax.ShapeDtypeStruct((u.shape[0], f), BF), jax.ShapeDtypeStruct((u.shape[0], wdown.shape[1]), F32)],
    )(u, u, u, cw, cb, wdown)


def gate_bwd(u, cw, cb, dact, per_batch, lc, name):
    main, prev, nxt = _gate_specs(u, per_batch, lc)
    f = u.shape[1] // 2

    def body(u_ref, p_ref, n_ref, w_ref, b_ref, da_ref, dc_ref, dw_ref):
        first, last = _seg_edges(per_batch, lc)
        x = u_ref[...].astype(F32)
        xm, xp = _shifted(x, p_ref, n_ref, first, last)
        c = w_ref[0:1, :] * xm + w_ref[1:2, :] * x + w_ref[2:3, :] * xp + b_ref[...]
        a, g = c[:, :f], c[:, f:]
        sg = jax.nn.sigmoid(a)
        da = da_ref[...]
        dc = jnp.concatenate([da * g * (sg * (1.0 + a * (1.0 - sg))), da * (a * sg)], axis=-1)
        dc_ref[...] = dc.astype(dc_ref.dtype)
        part = jnp.concatenate([jnp.sum(xm * dc, axis=0, keepdims=True), jnp.sum(x * dc, axis=0, keepdims=True),
                                jnp.sum(xp * dc, axis=0, keepdims=True), jnp.sum(dc, axis=0, keepdims=True),
                                jnp.zeros((4, 2 * f), F32)], axis=0)

        @pl.when(pl.program_id(0) == 0)
        def _():
            dw_ref[...] = part

        @pl.when(pl.program_id(0) > 0)
        def _():
            dw_ref[...] += part

    return pl.pallas_call(
        body, name=name, grid=(u.shape[0] // GATE_ROWS,),
        in_specs=[main, prev, nxt, _whole(cw), _whole(cb), pl.BlockSpec((GATE_ROWS, f), lambda i: (i, 0))],
        out_specs=[main, pl.BlockSpec((8, 2 * f), lambda i: (0, 0))],
        out_shape=[jax.ShapeDtypeStruct(u.shape, BF), jax.ShapeDtypeStruct((8, 2 * f), F32)],
    )(u, u, u, cw, cb, dact)


def conv_transpose(dc, cw, per_batch, lc, name):
    main, prev, nxt = _gate_specs(dc, per_batch, lc)

    def body(d_ref, p_ref, n_ref, w_ref, du_ref):
        first, last = _seg_edges(per_batch, lc)
        x = d_ref[...].astype(F32)
        xm, xp = _shifted(x, p_ref, n_ref, first, last)
        du_ref[...] = (w_ref[0:1, :] * xp + w_ref[1:2, :] * x + w_ref[2:3, :] * xm).astype(du_ref.dtype)

    return pl.pallas_call(
        body, name=name, grid=(dc.shape[0] // GATE_ROWS,),
        in_specs=[main, prev, nxt, _whole(cw)],
        out_specs=main, out_shape=jax.ShapeDtypeStruct(dc.shape, BF),
    )(dc, dc, dc, cw)


def loss_head(x, target, tiles_per_batch, ctx_tiles, name):
    n_tiles = x.shape[0] // ROW_TILE
    lat_tiles = tiles_per_batch - ctx_tiles

    def tgt_idx(i):
        j = i % tiles_per_batch
        return jnp.where(j < ctx_tiles, 0, (i // tiles_per_batch) * lat_tiles + j - ctx_tiles), 0

    def body(x_ref, t_ref, dx_ref, l_ref):
        lat = (pl.program_id(0) % tiles_per_batch >= ctx_tiles).astype(F32)
        err = (x_ref[...] - t_ref[...]) * lat
        dx_ref[...] = err * (1.0 / D_MODEL)
        l_ref[...] = jnp.full(l_ref.shape, 0.5 / D_MODEL * jnp.sum(err * err), F32)

    return pl.pallas_call(
        body, name=name, grid=(n_tiles,),
        in_specs=[pl.BlockSpec((ROW_TILE, D_MODEL), lambda i: (i, 0)), pl.BlockSpec((ROW_TILE, D_MODEL), tgt_idx)],
        out_specs=[pl.BlockSpec((ROW_TILE, D_MODEL), lambda i: (i, 0)), pl.BlockSpec((1, 8, LANES), lambda i: (i, 0, 0))],
        out_shape=[jax.ShapeDtypeStruct(x.shape, F32), jax.ShapeDtypeStruct((n_tiles, 8, LANES), F32)],
    )(x, target)


def adamw(w, g, m, v, name):
    rows, cols = w.shape
    tr = rows
    for cand in (512, 256, 128, 64, 32, 16, 8):
        if rows % cand == 0 and cand * cols * 4 <= (1 << 20):
            tr = cand
            break

    def body(w_ref, g_ref, m_ref, v_ref, d_ref, mo_ref, vo_ref):
        gg = g_ref[...]
        m2 = ADAM_B1 * m_ref[...] + (1.0 - ADAM_B1) * gg
        v2 = ADAM_B2 * v_ref[...] + (1.0 - ADAM_B2) * (gg * gg)
        m_hat = m2 / (1.0 - ADAM_B1 ** ADAM_STEP)
        v_hat = v2 / (1.0 - ADAM_B2 ** ADAM_STEP)
        d_ref[...] = -ADAM_LR * (m_hat / (jnp.sqrt(v_hat) + ADAM_EPS) + ADAM_WD * w_ref[...])
        mo_ref[...] = m2
        vo_ref[...] = v2

    spec = pl.BlockSpec((tr, cols), lambda i: (i, 0))
    return pl.pallas_call(body, name=name, grid=(rows // tr,), in_specs=[spec] * 4, out_specs=[spec] * 3,
                          out_shape=[jax.ShapeDtypeStruct(w.shape, F32)] * 3)(w, g, m, v)


def fn_post_mod(rows, tps, consts, ws):
    (x, a), (g, sc, sh), (lng, lnb) = rows, tps, ws
    y = layer_norm(ALPHA * x + g * a, lng, lnb)
    return [y, y * (1.0 + sc) + sh]


def _pick(n, cands):
    for c in cands:
        if n % c == 0:
            return c
    return n


def rope_tables(lc, l):
    pos = jnp.arange(l, dtype=F32)
    ret_inv = 1.0 / (ROPE_BASE ** jnp.linspace(0.0, 1.0, RET_DK // 2, dtype=F32))
    ang = pos[:, None] * ret_inv
    rc, rs = jnp.cos(ang), jnp.sin(ang)
    n_ax = MLA_D_ROPE // 4
    ax_inv = ROPE_BASE ** (-jnp.arange(n_ax, dtype=F32) / n_ax)
    rows_n = l // GRID_W
    rows = jnp.repeat(jnp.arange(rows_n, dtype=F32), GRID_W)
    cols = jnp.tile(jnp.arange(GRID_W, dtype=F32), rows_n)
    ra, ca = rows[:, None] * ax_inv, cols[:, None] * ax_inv
    rwc, rws, clc, cls = jnp.cos(ra), jnp.sin(ra), jnp.cos(ca), jnp.sin(ca)
    one = lambda n: jnp.ones((l, n), F32)
    zero = lambda n: jnp.zeros((l, n), F32)
    cat = lambda parts: jnp.concatenate(parts, axis=1)

    def with_ctx(tab, is_cos):
        head = jnp.ones((lc, LANES), F32) if is_cos else jnp.zeros((lc, LANES), F32)
        return jnp.concatenate([head, tab], axis=0)

    ret = (cat([rc, rc] * 4), cat([-rs, zero(16)] * 4), cat([zero(16), rs] * 4))
    ax_c = [rwc, rwc, clc, clc]
    ax_a = [-rws, zero(8), -cls, zero(8)]
    ax_b = [zero(8), rws, zero(8), cls]
    qt = (cat([one(64)] + ax_c + [one(32)]), cat([zero(64)] + ax_a + [zero(32)]), cat([zero(64)] + ax_b + [zero(32)]))
    kt = (cat([one(32)] + ax_c + [one(64)]), cat([zero(32)] + ax_a + [zero(64)]), cat([zero(32)] + ax_b + [zero(64)]))
    fix = lambda t3: tuple(with_ctx(t, k == 0) for k, t in enumerate(t3))
    return fix(ret), fix(qt), fix(kt)


_IN_ORDER = ((0, 128), (128, 256), (256, 512), (544, 800), (800, 928), (928, 1056), (1056, 1312), (1312, 1568),
             (1568, 1824), (1824, 1952), (512, 544), (1952, 1984))


def permute_w_in(w):
    parts = [w[:, a:b] for a, b in _IN_ORDER] + [jnp.zeros((w.shape[0], D_INP - D_IN), w.dtype)]
    return jnp.concatenate(parts, axis=1)


def unpermute_w_in(g):
    out, at = {}, 0
    for a, b in _IN_ORDER:
        out[a] = g[:, at:at + b - a]
        at += b - a
    return jnp.concatenate([out[a] for a in sorted(out)], axis=1)


def layer_weights(big, w, l):
    f = lambda a: a.astype(F32)
    r = {}
    r["ada_w"] = big["ada_w"].astype(BF)
    r["win"] = permute_w_in(big["w_in"]).astype(BF)
    r["wout"] = big["w_out"].astype(BF)
    r["wup"] = big["ffn_up"].astype(BF)
    r["wdown"] = big["ffn_down"].astype(BF)
    uq = big["mla_w_uq"].reshape(256, MLA_HEADS, 96)
    r["wuq"] = jnp.pad(uq, ((0, 0), (0, 0), (0, 32))).reshape(256, 8 * LANES).astype(BF)
    uk = big["mla_w_uk"].reshape(128, MLA_HEADS, 64)
    r["wuk"] = jnp.pad(uk, ((0, 0), (0, 0), (0, 64))).reshape(128, 8 * LANES).astype(BF)
    r["wuv"] = big["mla_w_uv"].astype(BF)
    gw = f(w["gla_gate_w"][l])
    z16 = jnp.zeros((16, LANES), F32)
    z96 = jnp.zeros((96, LANES), F32)
    r["gw"] = (jnp.concatenate([gw[0], z16, z96], axis=0), jnp.concatenate([z16, gw[1], z96], axis=0))
    r["gb"] = tuple(f(w["gla_gate_b"][l][d]).reshape(1, LANES) for d in range(2))
    r["rdec"] = tuple(jnp.repeat(f(w["ret_decay"][l][d]), 32).reshape(1, LANES) for d in range(2))
    r["gng"] = jnp.tile(f(w["gla_norm_g"][l]), 4).reshape(1, 256)
    r["gq"] = f(w["mla_q_norm_g"][l]).reshape(1, 256)
    r["gkv"] = f(w["mla_kv_norm_g"][l]).reshape(1, 128)
    for n in ("ln1_g", "ln1_b", "ln2_g", "ln2_b"):
        r[n] = f(w[n][l]).reshape(1, D_MODEL)
    r["cw"] = f(w["ffn_conv_w"][l])
    r["cb"] = f(w["ffn_conv_b"][l]).reshape(1, 2 * D_FF)
    return r


def tile_params(mod_l, nb, nct, nlt):
    m6 = mod_l.reshape(8, 6, D_MODEL)
    out = []
    for j in range(6):
        parts = []
        for b in range(nb):
            parts.append(jnp.broadcast_to(m6[4, j], (nct, 1, D_MODEL)))
            parts.append(jnp.broadcast_to(m6[b, j], (nlt, 1, D_MODEL)))
        out.append(jnp.concatenate(parts, axis=0))
    return out


def tile_param_grads(dts, nb, nct, nlt):
    cols = []
    for dt in dts:
        d = dt.reshape(nb, nct + nlt, D_MODEL)
        lat = jnp.sum(d[:, nct:], axis=1)
        ctx = jnp.sum(d[:, :nct], axis=(0, 1))
        cols.append(jnp.concatenate([lat, jnp.zeros((4 - nb, D_MODEL), F32), ctx[None], jnp.zeros((3, D_MODEL), F32)], axis=0))
    return jnp.stack(cols, axis=1).reshape(8, 6 * D_MODEL)


def layer_forward(x, h1, tp, lw, tabs, dims, nxt):
    nb, lc, lt = dims
    t = x.shape[0]
    nbt = lt // ROW_TILE
    ncc, nch = lc // CHUNK, lt // CHUNK
    tm = _pick(t, (1024, 768, 512, 256))
    ret_tab, q_tab, k_tab = tabs
    full = lambda a: (a, a.shape[1], 0)
    p = matmul(h1, lw["win"], "nn", F32, tm, D_INP, 1024, "proj_in")
    ogf, orf, sgf, srf = scan_fwd(p, ret_tab, lw["gw"][0], lw["gb"][0], lw["rdec"][0], nb, ncc, nch, False, "scan_fwd_f")
    ogb, orb, sgb, srb = scan_fwd(p, ret_tab, lw["gw"][1], lw["gb"][1], lw["rdec"][1], nb, ncc, nch, True, "scan_fwd_b")
    prep_rows = [(p, 256, C_CQ // 256), (p, 128, C_CKV // 128), (p, 128, C_LK // 128)]
    prep_consts = [(a, nbt) for a in q_tab + k_tab]
    prep_ws = [lw["gq"], lw["gkv"], lw["wuq"], lw["wuk"], lw["wuv"]]
    q, k, v = stage_fwd(fn_mla_prep, prep_rows, [], prep_consts, prep_ws, [(1024, BF), (1024, BF), (512, BF)], "mla_prep")
    if nxt is None:
        mo, lse, _ = mla_fwd(q, k, v, nb, lc, lt, "mla_attn_last")
        made, tp_next = None, None
    else:
        mo, lse, got = mla_fwd(q, k, v, nb, lc, lt, "mla_attn", gather=nxt[0])
        made = nxt[1](got)
        tp_next = (made[0][1], made[0][0])
    mix_rows = [full(ogf), full(ogb), full(orf), full(orb), (p, 256, C_GG // 256), (p, 256, C_RG // 256), full(mo)]
    m, = stage_fwd(fn_mix, mix_rows, [], [], [lw["gng"]], [(1024, BF)], "mix")
    a = matmul(m, lw["wout"], "nn", F32, tm, 1024, 1024, "proj_out")
    x1, h2 = stage_fwd(fn_post_mod, [full(x), full(a)], [tp[2], tp[4], tp[3]], [], [lw["ln1_g"], lw["ln1_b"]],
                       [(1024, F32), (1024, BF)], "post1")
    u = matmul(h2, lw["wup"], "nn", BF, tm, 1408, 1024, "ffn_up", b_outer=True)
    act, f = gate_fwd(u, lw["cw"], lw["cb"], lw["wdown"], lt, lc, "ffn_gate_down")
    if tp_next is None:
        x2, = stage_fwd(fn_post, [full(x1), full(f)], [tp[5]], [], [lw["ln2_g"], lw["ln2_b"]], [(1024, F32)], "post2_last")
        h1n = None
    else:
        x2, h1n = stage_fwd(fn_post_mod, [full(x1), full(f)], [tp[5], tp_next[0], tp_next[1]], [],
                            [lw["ln2_g"], lw["ln2_b"]], [(1024, F32), (1024, BF)], "post2")
    res = dict(x=x, h1=h1, p=p, ogf=ogf, orf=orf, sgf=sgf, srf=srf, ogb=ogb, orb=orb, sgb=sgb, srb=srb, q=q, k=k, v=v,
               mo=mo, lse=lse, m=m, a=a, x1=x1, h2=h2, u=u, act=act, f=f, mix_rows=mix_rows, prep_rows=prep_rows,
               prep_consts=prep_consts, prep_ws=prep_ws)
    return x2, h1n, res, made


def layer_backward(dx2, dh1n, res, tp, tp_next, lw, tabs, dims, exchange=()):
    nb, lc, lt = dims
    r = res
    t = dx2.shape[0]
    ncc, nch = lc // CHUNK, lt // CHUNK
    tm = _pick(t, (1024, 768, 512, 256))
    tkr = _pick(t, (2304, 1536, 1024, 768, 512))
    ret_tab = tabs[0]
    full = lambda a: (a, a.shape[1], 0)
    g = {}
    if tp_next is None:
        (dx1a, df), (dg2,), (g["ln2_g"], g["ln2_b"]) = stage_bwd(
            fn_post, [full(r["x1"]), full(r["f"])], [tp[5]], [], [lw["ln2_g"], lw["ln2_b"]], [dx2], [F32, BF], "post2_last_bwd")
        dnext = None
    else:
        (dx1a, df), (dg2, dsc1n, dsh1n), (g["ln2_g"], g["ln2_b"]) = stage_bwd(
            fn_post_mod, [full(r["x1"]), full(r["f"])], [tp[5], tp_next[0], tp_next[1]], [], [lw["ln2_g"], lw["ln2_b"]],
            [dx2, dh1n], [F32, BF], "post2_bwd")
        dnext = (dsc1n, dsh1n)
    g["ffn_down"] = matmul(r["act"], df, "tn", F32, 1408, 1024, tkr, "ffn_down_dw")
    dact = matmul(df, lw["wdown"], "nt", F32, tm, 1408, 1024, "ffn_down_dx", b_outer=True)
    dc, dcw = gate_bwd(r["u"], lw["cw"], lw["cb"], dact, lt, lc, "ffn_gate_bwd")
    g["ffn_conv_w"], g["ffn_conv_b"] = dcw[0:3], dcw[3]
    du = conv_transpose(dc, lw["cw"], lt, lc, "ffn_conv_t")
    dh2 = matmul(du, lw["wup"], "nt", F32, _pick(t, (512, 256)), 1024, 2 * D_FF, "ffn_up_dx")
    g["ffn_up"] = matmul(r["h2"], du, "tn", F32, 1024, 1408, tkr, "ffn_up_dw")
    (dxa, da), (dg1, dsc2, dsh2), (g["ln1_g"], g["ln1_b"]) = stage_bwd(
        fn_post_mod, [full(r["x"]), full(r["a"])], [tp[2], tp[4], tp[3]], [], [lw["ln1_g"], lw["ln1_b"]],
        [dx1a, dh2], [F32, BF], "post1_bwd")
    dm = matmul(da, lw["wout"], "nt", F32, tm, 1024, 1024, "proj_out_dx")
    g["w_out"] = matmul(r["m"], da, "tn", F32, 1024, 1024, tkr, "proj_out_dw")
    (dog, _, dor, _, dpg, dpr, dmo), _, (dgng,) = stage_bwd(
        fn_mix, r["mix_rows"], [], [], [lw["gng"]], [dm], [F32, None, F32, None, F32, F32, F32], "mix_bwd")
    g["gla_norm_g"] = jnp.sum(dgng.reshape(4, 64), axis=0)
    dq, dk, dv, landed = mla_bwd(r["q"], r["k"], r["v"], r["mo"], r["lse"], dmo, nb, lc, lt,
                                 "mla_attn_bwd" if exchange else "mla_attn_bwd_first", exchange=exchange)
    (dpq, dpkv, dplk), _, (dgq, dgkv, dwuq, dwuk, dwuv) = stage_bwd(
        fn_mla_prep, r["prep_rows"], [], r["prep_consts"], r["prep_ws"], [dq, dk, dv], [F32, F32, F32], "mla_prep_bwd")
    g["mla_q_norm_g"], g["mla_kv_norm_g"] = dgq.reshape(256), dgkv.reshape(128)
    g["mla_w_uq"] = dwuq.reshape(256, MLA_HEADS, LANES)[:, :, :96].reshape(256, MLA_HEADS * 96)
    g["mla_w_uk"] = dwuk.reshape(128, MLA_HEADS, LANES)[:, :, :64].reshape(128, MLA_HEADS * 64)
    g["mla_w_uv"] = dwuv
    s7, dgw0, dgb0, drd0 = scan_bwd(r["p"], ret_tab, lw["gw"][0], lw["gb"][0], lw["rdec"][0], r["sgf"], r["srf"], dog, dor,
                                    None, nb, ncc, nch, False, "scan_bwd_f")
    s7, dgw1, dgb1, drd1 = scan_bwd(r["p"], ret_tab, lw["gw"][1], lw["gb"][1], lw["rdec"][1], r["sgb"], r["srb"], dog, dor,
                                    s7, nb, ncc, nch, True, "scan_bwd_b")
    g["gla_gate_w"] = jnp.stack([dgw0[0:16], dgw1[16:32]])
    g["gla_gate_b"] = jnp.stack([dgb0[0], dgb1[0]])
    g["ret_decay"] = jnp.stack([jnp.sum(drd0.reshape(4, 32), axis=1), jnp.sum(drd1.reshape(4, 32), axis=1)])
    gq_, gk_, gv_, glrk, rq_, rk_, rv_ = s7
    pieces = [gq_, gk_, gv_, dpg, rq_, rk_, rv_, dpr, dpq, dpkv, glrk, dplk]
    dp, = stage_fwd(fn_assemble, [full(a) for a in pieces], [], [], [], [(D_INP, BF)], "dproj_assemble")
    dh1 = matmul(dp, lw["win"], "nt", F32, tm, 1024, D_INP, "proj_in_dx")
    g["w_in"] = unpermute_w_in(matmul(r["h1"], dp, "tn", F32, 1024, 1024, tkr, "proj_in_dw"))
    for n in ("ln1_g", "ln1_b", "ln2_g", "ln2_b"):
        g[n] = g[n].reshape(D_MODEL)
    dtp = [None, None, dg1, dsh2, dsc2, dg2]
    return dxa, dh1, dtp, dnext, g, landed


def local_step(x, c, ctx, c_ctx, w, loss_target, first_big, next_blocks, assemble, reduce_hooks=None):
    nb, l, _ = x.shape
    lc = ctx.shape[1]
    lt = lc + l
    dims = (nb, lc, lt)
    nct, nlt = lc // ROW_TILE, l // ROW_TILE
    tabs = rope_tables(lc, l)
    x0 = jnp.concatenate([ctx, x], axis=1).reshape(nb * lt, D_MODEL)
    s8 = jnp.concatenate([silu(c), jnp.zeros((4 - nb, D_MODEL), F32), silu(c_ctx)[None], jnp.zeros((3, D_MODEL), F32)], axis=0)

    def make_layer(i, big):
        lw = layer_weights(big, w, i)
        mod = matmul(s8, lw["ada_w"], "nn", F32, 8, 1536, 1024, "ada_mod") + w["ada_b"][i].astype(F32)[None]
        return tile_params(mod, nb, nct, nlt), lw

    made = make_layer(0, first_big)
    lws, tps = [], []
    h1, = stage_fwd(fn_modulate, [(x0, D_MODEL, 0)], [made[0][1], made[0][0]], [], [], [(D_MODEL, BF)], "mod_in")
    xs, ress = x0, []
    for i in range(DEPTH):
        tps.append(made[0])
        lws.append(made[1])
        nxt = None
        if i < DEPTH - 1:
            nxt = (next_blocks(i + 1), functools.partial(lambda got, j: make_layer(j, assemble(j, got)), j=i + 1))
        xs, h1, res, made = layer_forward(xs, h1, tps[i], lws[i], tabs, dims, nxt)
        ress.append(res)
    dx, lparts = loss_head(xs, loss_target.reshape(nb * l, D_MODEL), nct + nlt, nct, "loss_head")
    loss = jnp.sum(lparts[:, 0, 0])
    grads = [None] * DEPTH
    dtps = [None] * DEPTH
    reduced = [None] * DEPTH
    dh1, parts = None, ()
    for i in reversed(range(DEPTH)):
        tpn = None if i == DEPTH - 1 else (tps[i + 1][1], tps[i + 1][0])
        dx, dh1, dtp, dn, grads[i], landed = layer_backward(dx, dh1, ress[i], tps[i], tpn, lws[i], tabs, dims, parts)
        if dn is not None:
            dtps[i + 1][1], dtps[i + 1][0] = dn
        dtps[i] = dtp
        if reduce_hooks is not None:
            if parts:
                reduced[i + 1] = reduce_hooks[1](landed)
            parts = reduce_hooks[0]({n: grads[i].pop(n) for n in REDUCED})
    if reduce_hooks is not None:
        reduced[0] = reduce_hooks[1](exchange_chips(parts, "grad_exchange_last"))
    (dx0b,), (dsc1, dsh1), _ = stage_bwd(fn_modulate, [(x0, D_MODEL, 0)], [tps[0][1], tps[0][0]], [], [], [dh1], [F32], "mod_in_bwd")
    dtps[0][1], dtps[0][0] = dsc1, dsh1
    grad_x = (dx + dx0b).reshape(nb, lt, D_MODEL)[:, lc:]
    dmod = jnp.stack([tile_param_grads(d, nb, nct, nlt) for d in dtps])
    gw = {n: jnp.stack([grads[i][n] for i in range(DEPTH)]) for n in grads[0]}
    return loss, grad_x, gw, dmod, s8, lws, reduced


MESH_IDS = pl.DeviceIdType.MESH
ANY = pl.BlockSpec(memory_space=pl.ANY)


def _place():
    return lax.axis_index("x"), lax.axis_index("y"), lax.axis_index("c")


def _dma_sems(n, per):
    return pltpu.SemaphoreType.DMA((n, per))


def _gather8_steps(x_refs, out_refs, send_sems, recv_sems, local_sems):
    n = len(x_refs)
    x, y, c = _place()
    me, sibling = (x, y, c), (x, y, 1 - c)
    chips = [(1 - x, y), (x, 1 - y), (1 - x, 1 - y)]

    def copy(a, k, blk, to, own=False):
        slot = out_refs[a].at[4 * blk[0] + 2 * blk[1] + blk[2]]
        return pltpu.make_async_remote_copy(
            src_ref=x_refs[a] if own else slot, dst_ref=slot,
            send_sem=send_sems.at[a, k], recv_sem=recv_sems.at[a, k], device_id=to, device_id_type=MESH_IDS)

    def local(a):
        return pltpu.make_async_copy(x_refs[a], out_refs[a].at[4 * x + 2 * y + c], local_sems.at[a, 0])

    def first_copies():
        cps = []
        for a in range(n):
            cps.append(copy(a, 0, me, sibling, own=True))
            cps += [copy(a, 1 + j, me, (*chip, c), own=True) for j, chip in enumerate(chips)]
        return cps

    def start():
        for a in range(n):
            local(a).start()
        for cp in first_copies():
            cp.start()

    def finish():
        passed = []
        for j, chip in enumerate(chips):
            for a in range(n):
                copy(a, 1 + j, (*chip, c), me).wait_recv()
                passed.append(copy(a, 4 + j, (*chip, c), sibling))
                passed[-1].start()
        for a in range(n):
            copy(a, 0, sibling, me).wait_recv()
            for j, chip in enumerate(chips):
                copy(a, 4 + j, (*chip, 1 - c), me).wait_recv()
        for cp in first_copies() + passed:
            cp.wait_send()
        for a in range(n):
            local(a).wait()

    return start, finish


def all_gather8(blocks, name):
    n = len(blocks)

    def body(*refs):
        start, finish = _gather8_steps(refs[:n], refs[n:2 * n], *refs[2 * n:])
        start()
        finish()

    return pl.pallas_call(
        body, name=name, out_shape=[jax.ShapeDtypeStruct((8,) + b.shape, b.dtype) for b in blocks],
        in_specs=[ANY] * n, out_specs=[ANY] * n,
        scratch_shapes=[_dma_sems(n, 7), _dma_sems(n, 7), _dma_sems(n, 1)],
    )(*blocks)


def swap_cores(blocks, name):
    n = len(blocks)

    def body(*refs):
        x_refs, out_refs, (send_sems, recv_sems) = refs[:n], refs[n:2 * n], refs[2 * n:]
        x, y, c = _place()
        cps = [pltpu.make_async_remote_copy(src_ref=x_refs[a], dst_ref=out_refs[a], send_sem=send_sems.at[a, 0],
                                            recv_sem=recv_sems.at[a, 0], device_id=(x, y, 1 - c), device_id_type=MESH_IDS)
               for a in range(n)]
        for cp in cps:
            cp.start()
        for cp in cps:
            cp.wait()

    return pl.pallas_call(
        body, name=name, out_shape=[jax.ShapeDtypeStruct(b.shape, b.dtype) for b in blocks],
        in_specs=[ANY] * n, out_specs=[ANY] * n, scratch_shapes=[_dma_sems(n, 1), _dma_sems(n, 1)],
    )(*blocks)


def _exchange_steps(p_refs, out_refs, send_sems, recv_sems, local_sems):
    n = len(p_refs)
    x, y, c = _place()
    jm = 2 * x + y
    chips = [(1 - x, y), (x, 1 - y), (1 - x, 1 - y)]

    def local(a):
        return pltpu.make_async_copy(p_refs[a].at[jm], out_refs[a].at[jm], local_sems.at[a, 0])

    def sends():
        return [pltpu.make_async_remote_copy(
            src_ref=p_refs[a].at[2 * px + py], dst_ref=out_refs[a].at[jm], send_sem=send_sems.at[a, k],
            recv_sem=recv_sems.at[a, k], device_id=(px, py, c), device_id_type=MESH_IDS)
            for k, (px, py) in enumerate(chips) for a in range(n)]

    def start():
        for a in range(n):
            local(a).start()
        for cp in sends():
            cp.start()

    def finish():
        for k, (px, py) in enumerate(chips):
            for a in range(n):
                pltpu.make_async_remote_copy(
                    src_ref=p_refs[a].at[jm], dst_ref=out_refs[a].at[2 * px + py], send_sem=send_sems.at[a, k],
                    recv_sem=recv_sems.at[a, k], device_id=(px, py, c), device_id_type=MESH_IDS).wait_recv()
        for cp in sends():
            cp.wait_send()
        for a in range(n):
            local(a).wait()

    return start, finish


def exchange_chips(parts, name):
    n = len(parts)

    def body(*refs):
        start, finish = _exchange_steps(refs[:n], refs[n:2 * n], *refs[2 * n:])
        start()
        finish()

    return pl.pallas_call(
        body, name=name, out_shape=[jax.ShapeDtypeStruct(p.shape, p.dtype) for p in parts],
        in_specs=[ANY] * n, out_specs=[ANY] * n, scratch_shapes=[_dma_sems(n, 3), _dma_sems(n, 3), _dma_sems(n, 1)],
    )(*parts)


def _row_tile(rows, cols, itemsize=4, limit=1 << 21):
    top = min(rows, limit // (cols * itemsize)) // 16 * 16
    for cand in range(top, 0, -16):
        if rows % cand == 0:
            return cand
    return rows


def add_halves(a, b, kind, name):
    if kind == "col":
        m, k, n = a.shape
        n4 = n // 4
        tr = _row_tile(k, n4)
        in_spec = pl.BlockSpec((1, tr, n4), lambda j, h, i: (h, i, j))
        out_spec = pl.BlockSpec((1, 1, tr, n4), lambda j, h, i: (j, h, i, 0))
        grid, out_shape = (4, m, k // tr), (4, m, k, n4)
    elif kind == "row":
        m, k, n = a.shape
        k4 = k // 4
        tr = _row_tile(k4, n)
        nt = k4 // tr
        in_spec = pl.BlockSpec((1, tr, n), lambda j, h, i: (h, j * nt + i, 0))
        out_spec = pl.BlockSpec((1, 1, tr, n), lambda j, h, i: (j, h, i, 0))
        grid, out_shape = (4, m, nt), (4, m, k4, n)
    else:
        _, m, k, n = a.shape
        tr = _row_tile(k, n)
        in_spec = out_spec = pl.BlockSpec((1, 1, tr, n), lambda j, h, i: (j, h, i, 0))
        grid, out_shape = (4, m, k // tr), a.shape

    def body(a_ref, b_ref, s_ref):
        s_ref[...] = (a_ref[...] + b_ref[...].astype(F32)).astype(BF).reshape(s_ref.shape)

    return pl.pallas_call(body, name=name, grid=grid, in_specs=[in_spec, in_spec], out_specs=out_spec,
                          out_shape=jax.ShapeDtypeStruct(out_shape, BF))(a, b)


def sum_slots(a, name):
    s, m, k, n = a.shape
    tr = _row_tile(k, n, limit=(1 << 22) // s)

    def body(a_ref, o_ref):
        acc = a_ref[0].astype(F32)
        for j in range(1, s):
            acc = acc + a_ref[j].astype(F32)
        o_ref[...] = acc

    return pl.pallas_call(body, name=name, grid=(m, k // tr), in_specs=[pl.BlockSpec((s, 1, tr, n), lambda h, i: (0, h, i, 0))],
                          out_specs=pl.BlockSpec((1, tr, n), lambda h, i: (h, i, 0)),
                          out_shape=jax.ShapeDtypeStruct((m, k, n), F32))(a)


def sum_small(arrays, name):
    n = len(arrays)

    def body(*refs):
        for a_ref, o_ref in zip(refs[:n], refs[n:]):
            acc = a_ref[0]
            for j in range(1, 8):
                acc = acc + a_ref[j]
            o_ref[...] = acc

    return pl.pallas_call(body, name=name, out_shape=[jax.ShapeDtypeStruct(a.shape[1:], F32) for a in arrays])(*arrays)


COL_SHARDED = ("ada_w", "w_in", "mla_w_uq", "mla_w_uk", "mla_w_uv", "ffn_up", "ffn_conv_w")
ROW_SHARDED = ("w_out", "ffn_down")
GATHERED = ("ada_w", "w_in", "mla_w_uq", "mla_w_uk", "mla_w_uv", "w_out", "ffn_up", "ffn_down", "ffn_conv_w")
LAYER_GATHERED = GATHERED[:-1]
REDUCED = ("w_in", "mla_w_uq", "mla_w_uk", "mla_w_uv", "w_out", "ffn_up", "ffn_down")
SMALL = ("ada_b", "gla_gate_w", "gla_gate_b", "gla_norm_g", "ret_decay", "mla_q_norm_g", "mla_kv_norm_g",
         "ln1_g", "ln1_b", "ffn_conv_b", "ln2_g", "ln2_b")
WEIGHTS = ("c_ctx", "ada_w", "ada_b", "w_in", "gla_gate_w", "gla_gate_b", "gla_norm_g", "ret_decay", "mla_q_norm_g",
           "mla_kv_norm_g", "mla_w_uq", "mla_w_uk", "mla_w_uv", "w_out", "ln1_g", "ln1_b", "ffn_up", "ffn_conv_w",
           "ffn_conv_b", "ffn_down", "ln2_g", "ln2_b")
PACK = 16 * LANES
HALF_LAYERS = DEPTH // 2


def _pad_flat(v, n):
    return jnp.concatenate([v, jnp.zeros((n - v.shape[0],), v.dtype)]) if n > v.shape[0] else v


def _my_layers(a, c):
    return lax.dynamic_slice_in_dim(a, HALF_LAYERS * c, HALF_LAYERS, axis=0)


def layer_blocks(shards, l, c):
    out = []
    for n in LAYER_GATHERED:
        a = shards[n][l]
        out.append(lax.dynamic_slice_in_dim(a, c * (a.shape[0] // 2), a.shape[0] // 2, axis=0).astype(BF))
    return out


def layer_assemble(got):
    out = {}
    for n, g in zip(LAYER_GATHERED, got):
        _, k2, n4 = g.shape
        if n in ROW_SHARDED:
            out[n] = g.reshape(8 * k2, n4)
        else:
            out[n] = jnp.transpose(g.reshape(4, 2, k2, n4), (1, 2, 0, 3)).reshape(2 * k2, 4 * n4)
    return out


def conv_taps_assemble(got):
    _, _, k, n4 = got.shape
    return jnp.transpose(got.reshape(4, 2, HALF_LAYERS, k, n4), (1, 2, 3, 0, 4)).reshape(DEPTH, k, 4 * n4)


def _reduce_kind(n, g):
    return "row" if n in ROW_SHARDED else ("col" if (g.shape[-1] // 4) % LANES == 0 else "pre")


def reduce_begin(g, c):
    keep, give, kinds = [], [], []
    for n in REDUCED:
        a, kind = g[n], _reduce_kind(n, g[n])
        if kind == "pre":
            a = jnp.transpose(a.reshape(a.shape[0], 4, a.shape[1] // 4), (1, 0, 2))
        axis = a.ndim - 1 if kind == "row" else a.ndim - 2
        half = a.shape[axis] // 2
        lead = (slice(None), None) if kind == "pre" else (None,)
        keep.append(lax.dynamic_slice_in_dim(a, half * c, half, axis=axis)[lead])
        give.append(lax.dynamic_slice_in_dim(a, half * (1 - c), half, axis=axis)[lead].astype(BF))
        kinds.append(kind)
    got = swap_cores(give, "grad_swap_cores")
    return [add_halves(a, b, kind, "grad_add_cores_" + n) for n, a, b, kind in zip(REDUCED, keep, got, kinds)]


def reduce_end(landed):
    return [sum_slots(a, "grad_sum_chips_" + n) for n, a in zip(REDUCED, landed)]


def reduce_assemble(layers, c):
    mine = [jnp.concatenate([layers[l][k] for l in range(DEPTH)], axis=0) for k in range(len(REDUCED))]
    theirs = swap_cores(mine, "grad_swap_back")
    out = {}
    for n, a, b in zip(REDUCED, mine, theirs):
        out[n] = jnp.concatenate([jnp.where(c == 0, a, b), jnp.where(c == 0, b, a)], axis=2 if n in ROW_SHARDED else 1)
    return out


def _pack_small(d, names):
    flat = jnp.concatenate([d[n].astype(F32).reshape(-1) for n in names])
    n = -(-flat.shape[0] // PACK) * PACK
    return _pad_flat(flat, n).reshape(n // LANES, LANES)


def _unpack_small(buf, like, names):
    flat, out, at = buf.reshape(-1), {}, 0
    for n in names:
        sz = like[n].size
        out[n] = flat[at:at + sz].reshape(like[n].shape)
        at += sz
    return out


def kernel(x, c, ctx, c_ctx, ada_w, ada_b, w_in, gla_gate_w, gla_gate_b, gla_norm_g, ret_decay, mla_q_norm_g, mla_kv_norm_g, mla_w_uq, mla_w_uk, mla_w_uv, w_out, ln1_g, ln1_b, ffn_up, ffn_conv_w, ffn_conv_b, ffn_down, ln2_g, ln2_b, loss_target, m_c_ctx, m_ada_w, m_ada_b, m_w_in, m_gla_gate_w, m_gla_gate_b, m_gla_norm_g, m_ret_decay, m_mla_q_norm_g, m_mla_kv_norm_g, m_mla_w_uq, m_mla_w_uk, m_mla_w_uv, m_w_out, m_ln1_g, m_ln1_b, m_ffn_up, m_ffn_conv_w, m_ffn_conv_b, m_ffn_down, m_ln2_g, m_ln2_b, v_c_ctx, v_ada_w, v_ada_b, v_w_in, v_gla_gate_w, v_gla_gate_b, v_gla_norm_g, v_ret_decay, v_mla_q_norm_g, v_mla_kv_norm_g, v_mla_w_uq, v_mla_w_uk, v_mla_w_uv, v_w_out, v_ln1_g, v_ln1_b, v_ffn_up, v_ffn_conv_w, v_ffn_conv_b, v_ffn_down, v_ln2_g, v_ln2_b):
    w = dict(c_ctx=c_ctx, ada_w=ada_w, ada_b=ada_b, w_in=w_in, gla_gate_w=gla_gate_w, gla_gate_b=gla_gate_b, gla_norm_g=gla_norm_g, ret_decay=ret_decay, mla_q_norm_g=mla_q_norm_g, mla_kv_norm_g=mla_kv_norm_g, mla_w_uq=mla_w_uq, mla_w_uk=mla_w_uk, mla_w_uv=mla_w_uv, w_out=w_out, ln1_g=ln1_g, ln1_b=ln1_b, ffn_up=ffn_up, ffn_conv_w=ffn_conv_w, ffn_conv_b=ffn_conv_b, ffn_down=ffn_down, ln2_g=ln2_g, ln2_b=ln2_b)
    m = dict(c_ctx=m_c_ctx, ada_w=m_ada_w, ada_b=m_ada_b, w_in=m_w_in, gla_gate_w=m_gla_gate_w, gla_gate_b=m_gla_gate_b, gla_norm_g=m_gla_norm_g, ret_decay=m_ret_decay, mla_q_norm_g=m_mla_q_norm_g, mla_kv_norm_g=m_mla_kv_norm_g, mla_w_uq=m_mla_w_uq, mla_w_uk=m_mla_w_uk, mla_w_uv=m_mla_w_uv, w_out=m_w_out, ln1_g=m_ln1_g, ln1_b=m_ln1_b, ffn_up=m_ffn_up, ffn_conv_w=m_ffn_conv_w, ffn_conv_b=m_ffn_conv_b, ffn_down=m_ffn_down, ln2_g=m_ln2_g, ln2_b=m_ln2_b)
    v = dict(c_ctx=v_c_ctx, ada_w=v_ada_w, ada_b=v_ada_b, w_in=v_w_in, gla_gate_w=v_gla_gate_w, gla_gate_b=v_gla_gate_b, gla_norm_g=v_gla_norm_g, ret_decay=v_ret_decay, mla_q_norm_g=v_mla_q_norm_g, mla_kv_norm_g=v_mla_kv_norm_g, mla_w_uq=v_mla_w_uq, mla_w_uk=v_mla_w_uk, mla_w_uv=v_mla_w_uv, w_out=v_w_out, ln1_g=v_ln1_g, ln1_b=v_ln1_b, ffn_up=v_ffn_up, ffn_conv_w=v_ffn_conv_w, ffn_conv_b=v_ffn_conv_b, ffn_down=v_ffn_down, ln2_g=v_ln2_g, ln2_b=v_ln2_b)
    xi, yi, ci = _place()
    chip = 2 * xi + yi

    whole = {n: w[n] for n in WEIGHTS if n not in GATHERED and n != "c_ctx"}
    got0 = all_gather8(layer_blocks(w, 0, ci) + [_my_layers(ffn_conv_w, ci)], "gather_layer0")
    whole["ffn_conv_w"] = conv_taps_assemble(got0[-1])
    first_big = layer_assemble(got0[:-1])
    loss, grad_x, gw, dmod, s8, lws, reduced = local_step(
        x, c, ctx, c_ctx, whole, loss_target, first_big,
        lambda l: layer_blocks(w, l, ci), lambda l, got: layer_assemble(got),
        (lambda g: reduce_begin(g, ci), reduce_end))
    loss = lax.psum(loss, ("x", "y", "c"))

    dsil = jnp.zeros((8, D_MODEL), F32)
    for i in range(DEPTH):
        dsil = dsil + matmul(dmod[i], lws[i]["ada_w"], "nt", F32, 8, 1024, 1536, "ada_dsilu")

    grads = reduce_assemble(reduced, ci)

    small = {n: gw[n] for n in SMALL if n != "ada_b"}
    small.update(dsil=dsil[4])
    names = tuple(small)
    conv_g = gw["ffn_conv_w"].reshape(DEPTH * 3, 2 * D_FF)
    ev_small, ev_dmod, ev_s8, ev_conv = all_gather8(
        [_pack_small(small, names), dmod.reshape(DEPTH * 8, 6 * D_MODEL), s8, conv_g], "gather_small")
    sm_small, sm_dmod, sm_conv = sum_small([ev_small, ev_dmod, ev_conv], "sum_small")
    summed = _unpack_small(sm_small, small, names)
    for n in SMALL:
        if n != "ada_b":
            grads[n] = summed[n]
    grads["ada_b"] = jnp.sum(sm_dmod.reshape(DEPTH, 8, 6 * D_MODEL)[:, :5], axis=1)
    sg = jax.nn.sigmoid(c_ctx)
    grads["c_ctx"] = summed["dsil"] * (sg * (1.0 + c_ctx * (1.0 - sg)))
    ccols = ffn_conv_w.shape[2]
    grads["ffn_conv_w"] = lax.dynamic_slice_in_dim(sm_conv.reshape(DEPTH, 3, 2 * D_FF), chip * ccols, ccols, axis=2)
    s_all = ev_s8.reshape(64, D_MODEL)
    d_all = jnp.transpose(ev_dmod.reshape(8, DEPTH, 8, 6 * D_MODEL), (1, 0, 2, 3)).reshape(DEPTH, 64, 6 * D_MODEL)
    cols = ada_w.shape[2]
    g_ada = []
    for i in range(DEPTH):
        d_mine = lax.dynamic_slice_in_dim(d_all[i], chip * cols, cols, axis=1)
        g_ada.append(matmul(s_all, d_mine, "tn", F32, 1024, cols, 64, "ada_dw"))
    grads["ada_w"] = jnp.stack(g_ada)

    delta, new_m, new_v = {}, {}, {}
    for n in GATHERED:
        shp = w[n].shape
        v2 = lambda a: a.reshape(-1, shp[-1])
        d_, m_, v_ = adamw(v2(w[n]), v2(grads[n]), v2(m[n]), v2(v[n]), "adamw_" + n)
        delta[n], new_m[n], new_v[n] = d_.reshape(shp), m_.reshape(shp), v_.reshape(shp)
    rep = tuple(n for n in WEIGHTS if n not in GATHERED)
    pk = lambda d: _pack_small({n: d[n] for n in rep}, rep)
    d_, m_, v_ = adamw(pk(w), pk(grads), pk(m), pk(v), "adamw_small")
    like = {n: w[n] for n in rep}
    delta.update(_unpack_small(d_, like, rep))
    new_m.update(_unpack_small(m_, like, rep))
    new_v.update(_unpack_small(v_, like, rep))
    grads = {n: grads[n].reshape(w[n].shape) for n in WEIGHTS}
    return (loss, grad_x, *[grads[n] for n in WEIGHTS], *[delta[n] for n in WEIGHTS], *[new_m[n] for n in WEIGHTS],
            *[new_v[n] for n in WEIGHTS])
```

```python
import functools

import jax
import jax.numpy as jnp
from jax import lax
from jax.experimental import pallas as pl
from jax.experimental.pallas import tpu as pltpu

F32 = jnp.float32
BF = jnp.bfloat16

D_MODEL = 1024
DEPTH = 4
GRID_W = 64
GLA_DK = 32
GLA_TAU = 16.0
RET_DK = 32
MLA_HEADS = 8
MLA_D_NOPE = 64
MLA_D_ROPE = 32
MLA_SCALE = (MLA_D_NOPE + MLA_D_ROPE) ** -0.5
D_FF = 2816
ROPE_BASE = 10000.0
EPS = 1e-6
ALPHA = (2 * DEPTH) ** 0.25
ADAM_LR, ADAM_B1, ADAM_B2, ADAM_EPS, ADAM_WD, ADAM_STEP = 0.001, 0.9, 0.999, 1e-08, 0.01, 10

ROW_TILE = 256
CHUNK = 64
GATE_ROWS = 128
ATTN_ROWS_FWD = 1024
ATTN_ROWS_BWD = 512
LANES = 128

C_GQ, C_GK, C_GV, C_GG, C_RQ, C_RK, C_RV, C_RG, C_CQ, C_CKV, C_LK = 0, 128, 256, 512, 768, 896, 1024, 1280, 1536, 1792, 1920
D_INP = 2048
D_IN = 1984


def _dg(a, b, ca, cb):
    return lax.dot_general(a.astype(BF), b.astype(BF), (((ca,), (cb,)), ((), ())), preferred_element_type=F32)


@jax.custom_vjp
def mm_nn(a, b):
    return _dg(a, b, 1, 0)


@jax.custom_vjp
def mm_nt(a, b):
    return _dg(a, b, 1, 1)


@jax.custom_vjp
def mm_tn(a, b):
    return _dg(a, b, 0, 0)


mm_nn.defvjp(lambda a, b: (_dg(a, b, 1, 0), (a, b)),
             lambda r, g: (mm_nt(g, r[1]).astype(r[0].dtype), mm_tn(r[0], g).astype(r[1].dtype)))
mm_nt.defvjp(lambda a, b: (_dg(a, b, 1, 1), (a, b)),
             lambda r, g: (mm_nn(g, r[1]).astype(r[0].dtype), mm_tn(g, r[0]).astype(r[1].dtype)))
mm_tn.defvjp(lambda a, b: (_dg(a, b, 0, 0), (a, b)),
             lambda r, g: (mm_nt(r[1], g).astype(r[0].dtype), mm_nn(r[0], g).astype(r[1].dtype)))


def _split3(x):
    h = x.astype(BF)
    r = x - h.astype(F32)
    m = r.astype(BF)
    lo = (r - m.astype(F32)).astype(BF)
    return h, m, lo


def _exact(x, mat, left):
    h, m, lo = _split3(x)
    if left:
        d = lambda t: lax.dot_general(mat, t, (((1,), (0,)), ((), ())), preferred_element_type=F32)
    else:
        d = lambda t: lax.dot_general(t, mat, (((1,), (0,)), ((), ())), preferred_element_type=F32)
    return (d(lo) + d(m)) + d(h)


def _iota(shape, axis):
    return lax.broadcasted_iota(jnp.int32, shape, axis)


def _tri(n, upper):
    r, c = _iota((n, n), 0), _iota((n, n), 1)
    return jnp.where((c >= r) if upper else (r >= c), 1.0, 0.0).astype(BF)


@functools.partial(jax.custom_vjp, nondiff_argnums=(1,))
def cumsum_rows(x, upper):
    return _exact(x, _tri(x.shape[0], upper), True)


cumsum_rows.defvjp(lambda x, upper: (cumsum_rows(x, upper), None),
                   lambda upper, r, g: (cumsum_rows(g, not upper),))


def _seg(n, w):
    shift = w.bit_length() - 1
    r, c = _iota((n, n), 0), _iota((n, n), 1)
    return jnp.where(lax.shift_right_logical(r, shift) == lax.shift_right_logical(c, shift), 1.0, 0.0).astype(BF)


@functools.partial(jax.custom_vjp, nondiff_argnums=(1,))
def seg_sum(x, w):
    return _exact(x, _seg(x.shape[1], w), False)


seg_sum.defvjp(lambda x, w: (seg_sum(x, w), None), lambda w, r, g: (seg_sum(g, w),))


def _place_mat(transpose):
    shape = (8 * LANES, LANES) if transpose else (LANES, 8 * LANES)
    r, c = _iota(shape, 0), _iota(shape, 1)
    src, dst = (c, r) if transpose else (r, c)
    dl = jnp.bitwise_and(dst, LANES - 1)
    ok = (dl >= 64) & (dl < 96) & (src == dl - 32)
    return jnp.where(ok, 1.0, 0.0).astype(BF)


@jax.custom_vjp
def place_kr(x):
    return _exact(x, _place_mat(False), False)


place_kr.defvjp(lambda x: (place_kr(x), None), lambda r, g: (_exact(g, _place_mat(True), False),))


@functools.partial(jax.custom_vjp, nondiff_argnums=(1,))
def lane_roll(x, s):
    return pltpu.roll(x, s, 1)


lane_roll.defvjp(lambda x, s: (pltpu.roll(x, s, 1), None),
                 lambda s, r, g: (pltpu.roll(g, (g.shape[1] - s) % g.shape[1], 1),))


def rope(x, tab, d):
    cos, sa, sb = tab
    return x * cos + lane_roll(x, LANES - d) * sa + lane_roll(x, d) * sb


def silu(x):
    return x * jax.nn.sigmoid(x)


def log_sigmoid(z):
    return jnp.minimum(z, 0.0) - jnp.log(1.0 + jnp.exp(-jnp.abs(z)))


def layer_norm(x, g, b):
    mu = jnp.mean(x, axis=-1, keepdims=True)
    xc = x - mu
    var = jnp.mean(xc * xc, axis=-1, keepdims=True)
    return xc * lax.rsqrt(var + EPS) * g + b


def matmul(a, b, mode, out_dtype, tm, tn, tk, name, b_outer=False):
    ij = (lambda f: (lambda g0, g1, kk: f(g1, g0, kk))) if b_outer else (lambda f: f)
    if mode == "nn":
        (m, k), (k2, n) = a.shape, b.shape
        a_spec = pl.BlockSpec((tm, tk), ij(lambda i, j, kk: (i, kk)))
        b_spec = pl.BlockSpec((tk, tn), ij(lambda i, j, kk: (kk, j)))
        ca, cb = 1, 0
    elif mode == "nt":
        (m, k), (n, k2) = a.shape, b.shape
        a_spec = pl.BlockSpec((tm, tk), ij(lambda i, j, kk: (i, kk)))
        b_spec = pl.BlockSpec((tn, tk), ij(lambda i, j, kk: (j, kk)))
        ca, cb = 1, 1
    else:
        (k, m), (k2, n) = a.shape, b.shape
        a_spec = pl.BlockSpec((tk, tm), ij(lambda i, j, kk: (kk, i)))
        b_spec = pl.BlockSpec((tk, tn), ij(lambda i, j, kk: (kk, j)))
        ca, cb = 0, 0
    assert k == k2 and m % tm == 0 and n % tn == 0 and k % tk == 0, (name, a.shape, b.shape, tm, tn, tk)
    nk = k // tk
    grid = (n // tn, m // tm, nk) if b_outer else (m // tm, n // tn, nk)

    def body(a_ref, b_ref, o_ref, *acc):
        part = _dg(a_ref[...], b_ref[...], ca, cb)
        if nk == 1:
            o_ref[...] = part.astype(o_ref.dtype)
            return
        acc_ref, = acc
        kk = pl.program_id(2)

        @pl.when(kk == 0)
        def _():
            acc_ref[...] = part

        @pl.when(kk > 0)
        def _():
            acc_ref[...] += part

        @pl.when(kk == nk - 1)
        def _():
            o_ref[...] = acc_ref[...].astype(o_ref.dtype)

    return pl.pallas_call(
        body, name=name, grid=grid,
        in_specs=[a_spec, b_spec], out_specs=pl.BlockSpec((tm, tn), ij(lambda i, j, kk: (i, j))),
        out_shape=jax.ShapeDtypeStruct((m, n), out_dtype),
        scratch_shapes=[] if nk == 1 else [pltpu.VMEM((tm, tn), F32)],
    )(a, b)


def _stage_specs(rows, tps, consts, ws):
    specs, args = [], []
    for arr, width, cb in rows:
        specs.append(pl.BlockSpec((ROW_TILE, width), functools.partial(lambda i, cb: (i, cb), cb=cb)))
        args.append(arr)
    for arr in tps:
        specs.append(pl.BlockSpec((1, 1, arr.shape[2]), lambda i: (i, 0, 0)))
        args.append(arr)
    for arr, period in consts:
        specs.append(pl.BlockSpec((ROW_TILE, arr.shape[1]), functools.partial(lambda i, p: (i % p, 0), p=period)))
        args.append(arr)
    for arr in ws:
        specs.append(pl.BlockSpec(arr.shape, functools.partial(lambda i, nd: (0,) * nd, nd=arr.ndim)))
        args.append(arr)
    return specs, args


def _stage_load(refs, n_rows, n_tps, n_consts, n_ws):
    it = iter(refs)
    rows = [next(it)[...].astype(F32) for _ in range(n_rows)]
    tps = [next(it)[0].astype(F32) for _ in range(n_tps)]
    consts = [next(it)[...].astype(F32) for _ in range(n_consts)]
    ws = [next(it)[...].astype(F32) for _ in range(n_ws)]
    return rows, tps, consts, ws


def stage_fwd(fn, rows, tps, consts, ws, outs, name):
    n_tiles = rows[0][0].shape[0] // ROW_TILE
    specs, args = _stage_specs(rows, tps, consts, ws)
    counts = (len(rows), len(tps), len(consts), len(ws))

    def body(*refs):
        r, t, c, w = _stage_load(refs[:sum(counts)], *counts)
        res = fn(r, t, c, w)
        for o_ref, o in zip(refs[sum(counts):], res):
            o_ref[...] = o.astype(o_ref.dtype)

    res = pl.pallas_call(
        body, name=name, grid=(n_tiles,), in_specs=specs,
        out_specs=[pl.BlockSpec((ROW_TILE, wd), lambda i: (i, 0)) for wd, _ in outs],
        out_shape=[jax.ShapeDtypeStruct((n_tiles * ROW_TILE, wd), dt) for wd, dt in outs],
    )(*args)
    return list(res)


def stage_bwd(fn, rows, tps, consts, ws, cts, row_grads, name):
    n_tiles = rows[0][0].shape[0] // ROW_TILE
    specs, args = _stage_specs(rows, tps, consts, ws)
    counts = (len(rows), len(tps), len(consts), len(ws))
    n_in = sum(counts)
    for ct in cts:
        specs.append(pl.BlockSpec((ROW_TILE, ct.shape[1]), lambda i: (i, 0)))
        args.append(ct)
    want = [k for k, dt in enumerate(row_grads) if dt is not None]
    out_specs = [pl.BlockSpec((ROW_TILE, rows[k][1]), lambda i: (i, 0)) for k in want]
    out_shape = [jax.ShapeDtypeStruct((n_tiles * ROW_TILE, rows[k][1]), row_grads[k]) for k in want]
    out_specs += [pl.BlockSpec((1, 1, a.shape[2]), lambda i: (i, 0, 0)) for a in tps]
    out_shape += [jax.ShapeDtypeStruct((n_tiles, 1, a.shape[2]), F32) for a in tps]
    out_specs += [pl.BlockSpec(a.shape, functools.partial(lambda i, nd: (0,) * nd, nd=a.ndim)) for a in ws]
    out_shape += [jax.ShapeDtypeStruct(a.shape, F32) for a in ws]

    def body(*refs):
        r, t, c, w = _stage_load(refs[:n_in], *counts)
        g = [ref[...].astype(F32) for ref in refs[n_in:n_in + len(cts)]]
        _, vjp = jax.vjp(lambda r_, t_, w_: fn(r_, t_, c, w_), r, t, w)
        dr, dt, dw = vjp(g)
        o = iter(refs[n_in + len(cts):])
        for k in want:
            ref = next(o)
            ref[...] = dr[k].astype(ref.dtype)
        for v in dt:
            next(o)[0] = v
        first = pl.program_id(0) == 0
        for v in dw:
            ref = next(o)

            @pl.when(first)
            def _():
                ref[...] = v

            @pl.when(jnp.logical_not(first))
            def _():
                ref[...] += v

    res = pl.pallas_call(body, name=name, grid=(n_tiles,), in_specs=specs, out_specs=out_specs, out_shape=out_shape)(*args)
    res = list(res)
    drows = [None] * len(rows)
    for k in want:
        drows[k] = res.pop(0)
    dtps = [res.pop(0) for _ in tps]
    dws = [res.pop(0) for _ in ws]
    return drows, dtps, dws


def fn_modulate(rows, tps, consts, ws):
    (x,), (sc, sh) = rows, tps
    return [x * (1.0 + sc) + sh]


def fn_post(rows, tps, consts, ws):
    (x, a), (g,), (lng, lnb) = rows, tps, ws
    return [layer_norm(ALPHA * x + g * a, lng, lnb)]


def fn_mix(rows, tps, consts, ws):
    ogf, ogb, orf, orb, pg, pr, mo = rows
    gng, = ws
    og = ogf + ogb
    out_g = og * lax.rsqrt(seg_sum(og * og, 64) * (1.0 / 64) + EPS) * gng * silu(pg)
    o = orf + orb
    oc = o - seg_sum(o, 64) * (1.0 / 64)
    out_r = oc * lax.rsqrt(seg_sum(oc * oc, 64) * (1.0 / 64) + EPS) * silu(pr)
    return [jnp.concatenate([out_g, out_r, mo], axis=-1)]


def fn_mla_prep(rows, tps, consts, ws):
    pq, pkv, plk = rows
    gq, gkv, wuq, wuk, wuv = ws
    qtab, ktab = consts[0:3], consts[3:6]
    cq = pq * lax.rsqrt(jnp.mean(pq * pq, axis=-1, keepdims=True) + EPS) * gq
    qp = mm_nn(cq, wuq)
    q = jnp.concatenate([rope(qp[:, h * LANES:(h + 1) * LANES], qtab, 8) * MLA_SCALE for h in range(MLA_HEADS)], axis=-1)
    ckv = pkv * lax.rsqrt(jnp.mean(pkv * pkv, axis=-1, keepdims=True) + EPS) * gkv
    k = mm_nn(ckv, wuk) + place_kr(rope(plk, ktab, 8))
    v = mm_nn(ckv, wuv)
    return [q, k, v]


def fn_assemble(rows, tps, consts, ws):
    gq, gk, gv, gg, rq, rk, rv, rg, cq, ckv, lk1, lk2 = rows
    return [jnp.concatenate([gq, gk, gv, gg, rq, rk, rv, rg, cq, ckv, lk1 + lk2], axis=-1)]


def _head_masks():
    hm = (lax.shift_right_logical(_iota((4, 1, LANES), 2), 5) == _iota((4, 1, LANES), 0)).astype(F32)
    vm = (lax.shift_right_logical(_iota((4, 1, 256), 2), 6) == _iota((4, 1, 256), 0)).astype(F32)
    bd = (lax.shift_right_logical(_iota((256, LANES), 0), 6) == lax.shift_right_logical(_iota((256, LANES), 1), 5)).astype(F32)
    return hm, vm, bd


def chunk_step(s, q, k, v, la, upper):
    hm, vm, bd = _head_masks()
    t, u = _iota((4 * CHUNK, CHUNK), 0), _iota((4 * CHUNK, CHUNK), 1)
    t = jnp.bitwise_and(t, CHUNK - 1)
    causal = (u >= t) if upper else (t >= u)
    if la.shape[0] == 1:
        pos = _iota((CHUNK, LANES), 0)
        b = ((CHUNK - pos) if upper else (pos + 1)).astype(F32) * la
        bend = float(CHUNK) * la
    else:
        b = cumsum_rows(la, upper)
        bend = jnp.sum(la, axis=0, keepdims=True)
    half = 0.5 * bend
    qd = q * jnp.exp(b - half)
    kd = k * jnp.exp(half - b)
    qe = (qd[None] * hm).reshape(4 * CHUNK, LANES)
    att = jnp.where(causal, mm_nt(qe, kd), 0.0)
    o_intra = (mm_nn(att, v).reshape(4, CHUNK, 256) * vm).sum(0)
    o = mm_nt(q * jnp.exp(b), s) + o_intra
    s_new = (s * jnp.exp(bend) + mm_tn(v, k * jnp.exp(bend - b))) * bd
    return o, s_new


def scan_step(sg, sr, q, k, v, lrk, rq, rk, rv, gw, gb, rdec, tab, upper):
    la_g = log_sigmoid(mm_nn(lrk, gw) + gb) * (1.0 / GLA_TAU)
    og, sg2 = chunk_step(sg, q * GLA_DK ** -0.5, k, v, la_g, upper)
    la_r = log_sigmoid(rdec)
    orr, sr2 = chunk_step(sr, rope(rq, tab, 16), rope(rk * RET_DK ** -0.5, tab, 16), rv, la_r, upper)
    return og, orr, sg2, sr2


def _chunk_of(n, ncc, nch, reverse):
    if not reverse:
        return n
    return jnp.where(n < ncc, ncc - 1 - n, nch - 1 + ncc - n)


def _scan_in_specs(p3, tabs, gw, gb, rdec, cidx):
    nb = p3.shape[0]

    def blk(width, cb):
        return pl.BlockSpec((nb, CHUNK, width), lambda m: (0, cidx(m), cb))

    specs = [blk(128, C_GQ // 128), blk(128, C_GK // 128), blk(256, C_GV // 256), blk(128, C_LK // 128),
             blk(128, C_RQ // 128), blk(128, C_RK // 128), blk(256, C_RV // 256)]
    args = [p3] * 7
    for t in tabs:
        specs.append(pl.BlockSpec((CHUNK, LANES), lambda m: (cidx(m), 0)))
        args.append(t)
    for w in (gw, gb, rdec):
        specs.append(pl.BlockSpec(w.shape, lambda m: (0, 0)))
        args.append(w)
    return specs, args


def scan_fwd(p, tabs, gw, gb, rdec, nb, ncc, nch, reverse, name):
    cidx = lambda n: _chunk_of(n, ncc, nch, reverse)
    t = p.shape[0]
    specs, args = _scan_in_specs(p.reshape(nb, t // nb, p.shape[1]), tabs, gw, gb, rdec, cidx)

    def body(q, k, v, lrk, rq, rk, rv, tc, ta, tb, gw_r, gb_r, rd_r, og_r, or_r, sgo_r, sro_r, sg, sr):
        @pl.when(pl.program_id(0) == 0)
        def _():
            sg[...] = jnp.zeros_like(sg)
            sr[...] = jnp.zeros_like(sr)

        sgo_r[0] = sg[...]
        sro_r[0] = sr[...]
        ld = lambda r: r[...].astype(F32)
        tab, gw_, gb_, rd_ = (ld(tc), ld(ta), ld(tb)), ld(gw_r), ld(gb_r), ld(rd_r)
        for b in range(nb):
            lb = lambda r: r[b].astype(F32)
            og, orr, sg2, sr2 = scan_step(sg[b], sr[b], lb(q), lb(k), lb(v), lb(lrk), lb(rq), lb(rk), lb(rv),
                                          gw_, gb_, rd_, tab, reverse)
            og_r[b] = og
            or_r[b] = orr
            sg[b] = sg2
            sr[b] = sr2

    row_out = pl.BlockSpec((nb, CHUNK, 256), lambda n: (0, cidx(n), 0))
    st_out = pl.BlockSpec((1, nb, 256, LANES), lambda n: (n, 0, 0, 0))
    og, orr, sgs, srs = pl.pallas_call(
        body, name=name, grid=(nch,), in_specs=specs, out_specs=[row_out, row_out, st_out, st_out],
        out_shape=[jax.ShapeDtypeStruct((nb, t // nb, 256), F32)] * 2 + [jax.ShapeDtypeStruct((nch, nb, 256, LANES), F32)] * 2,
        scratch_shapes=[pltpu.VMEM((nb, 256, LANES), F32)] * 2,
    )(*args)
    return og.reshape(t, 256), orr.reshape(t, 256), sgs, srs


def scan_bwd(p, tabs, gw, gb, rdec, sg_in, sr_in, dog, dor, prev, nb, ncc, nch, reverse, name):
    step = lambda m: nch - 1 - m
    cidx = lambda m: _chunk_of(step(m), ncc, nch, reverse)
    t = p.shape[0]
    lt = t // nb
    specs, args = _scan_in_specs(p.reshape(nb, lt, p.shape[1]), tabs, gw, gb, rdec, cidx)
    st_spec = pl.BlockSpec((1, nb, 256, LANES), lambda m: (step(m), 0, 0, 0))
    specs += [st_spec, st_spec]
    args += [sg_in, sr_in]
    row = lambda width: pl.BlockSpec((nb, CHUNK, width), lambda m: (0, cidx(m), 0))
    specs += [row(256), row(256)]
    args += [dog.reshape(nb, lt, 256), dor.reshape(nb, lt, 256)]
    widths = (128, 128, 256, 128, 128, 128, 256)
    if prev is not None:
        specs += [row(wd) for wd in widths]
        args += [a.reshape(nb, lt, a.shape[1]) for a in prev]
    n_prev = 0 if prev is None else 7

    def body(*refs):
        (q, k, v, lrk, rq, rk, rv, tc, ta, tb, gw_r, gb_r, rd_r, sgi, sri, dog_r, dor_r), rest = refs[:17], refs[17:]
        prev_r, rest = rest[:n_prev], rest[n_prev:]
        outs, (dgw_r, dgb_r, drd_r, dsg, dsr) = rest[:7], rest[7:]
        first = pl.program_id(0) == 0

        @pl.when(first)
        def _():
            dsg[...] = jnp.zeros_like(dsg)
            dsr[...] = jnp.zeros_like(dsr)

        ld = lambda r: r[...].astype(F32)
        tab, gw_, gb_, rd_ = (ld(tc), ld(ta), ld(tb)), ld(gw_r), ld(gb_r), ld(rd_r)
        wsum = None
        for b in range(nb):
            lb = lambda r: r[b].astype(F32)
            prim = (sgi[0, b], sri[0, b], lb(q), lb(k), lb(v), lb(lrk), lb(rq), lb(rk), lb(rv), gw_, gb_, rd_)
            _, vjp = jax.vjp(lambda *a: scan_step(*a, tab, reverse), *prim)
            g = vjp((lb(dog_r), lb(dor_r), dsg[b], dsr[b]))
            dsg[b] = g[0]
            dsr[b] = g[1]
            for j in range(7):
                val = g[2 + j]
                if n_prev:
                    val = val + prev_r[j][b]
                outs[j][b] = val
            wsum = g[9:12] if wsum is None else tuple(a + c for a, c in zip(wsum, g[9:12]))
        for ref, val in zip((dgw_r, dgb_r, drd_r), wsum):
            @pl.when(first)
            def _():
                ref[...] = val

            @pl.when(jnp.logical_not(first))
            def _():
                ref[...] += val

    wspec = lambda w: pl.BlockSpec(w.shape, lambda m: (0, 0))
    res = pl.pallas_call(
        body, name=name, grid=(nch,), in_specs=specs,
        out_specs=[row(wd) for wd in widths] + [wspec(gw), wspec(gb), wspec(rdec)],
        out_shape=[jax.ShapeDtypeStruct((nb, lt, wd), F32) for wd in widths]
        + [jax.ShapeDtypeStruct(w.shape, F32) for w in (gw, gb, rdec)],
        scratch_shapes=[pltpu.VMEM((nb, 256, LANES), F32)] * 2,
    )(*args)
    return tuple(a.reshape(t, a.shape[2]) for a in res[:7]), res[7], res[8], res[9]


def _attn_tiles(lc, lt):
    nct = lc // ROW_TILE
    return nct, (lt - lc) // ROW_TILE


def _attn_loop(tile, lc, lt, lat_rows):
    for i in range(lc // ROW_TILE):
        tile(i * ROW_TILE, lc, ROW_TILE)

    def lat(i, carry):
        tile(pl.multiple_of(lc + i * lat_rows, ROW_TILE), lt, lat_rows)
        return carry

    lax.fori_loop(0, (lt - lc) // lat_rows, lat, 0)


def mla_fwd(q, k, v, nb, lc, lt, name, gather=()):
    ng = len(gather)

    def body(q_ref, k_ref, v_ref, *rest):
        x_refs, (o_ref, lse_ref), out_refs, sems = rest[:ng], rest[ng:ng + 2], rest[ng + 2:2 * ng + 2], rest[2 * ng + 2:]
        if ng:
            start, finish = _gather8_steps(x_refs, out_refs, *sems)
            pl.when((pl.program_id(0) == 0) & (pl.program_id(1) == 0))(start)

        def tile(r0, nk, nrows):
            rows = pl.ds(r0, nrows)
            lane = _iota((nrows, LANES), 1)
            outs, lse = [], jnp.zeros((nrows, LANES), F32)
            for j in range(2):
                s = _dg(q_ref[rows, j * LANES:(j + 1) * LANES], k_ref[0:nk, j * LANES:(j + 1) * LANES], 1, 1)
                m = jnp.max(s, axis=-1, keepdims=True)
                p = jnp.exp(s - m)
                l = jnp.sum(p, axis=-1, keepdims=True)
                outs.append(_dg(p, v_ref[0:nk, j * 64:(j + 1) * 64], 1, 0) * (1.0 / l))
                lse = jnp.where(lane == j, m + jnp.log(l), lse)
            o_ref[rows, :] = jnp.concatenate(outs, axis=-1)
            lse_ref[rows, :] = lse

        _attn_loop(tile, lc, lt, ATTN_ROWS_FWD)
        if ng:
            pl.when((pl.program_id(0) == nb - 1) & (pl.program_id(1) == MLA_HEADS // 2 - 1))(finish)

    pair = lambda width: pl.BlockSpec((lt, width), lambda b, h: (b, h))
    res = pl.pallas_call(
        body, name=name, grid=(nb, MLA_HEADS // 2), in_specs=[pair(2 * LANES), pair(2 * LANES), pair(LANES)] + [ANY] * ng,
        out_specs=[pair(LANES), pair(LANES)] + [ANY] * ng,
        out_shape=[jax.ShapeDtypeStruct((nb * lt, MLA_HEADS * 64), F32), jax.ShapeDtypeStruct((nb * lt, MLA_HEADS // 2 * LANES), F32)]
        + [jax.ShapeDtypeStruct((8,) + g.shape, g.dtype) for g in gather],
        scratch_shapes=[_dma_sems(ng, 7), _dma_sems(ng, 7), _dma_sems(ng, 1)] if ng else [],
    )(q, k, v, *gather)
    return res[0], res[1], list(res[2:])


def mla_bwd(q, k, v, o, lse, do, nb, lc, lt, name, exchange=()):
    ne = len(exchange)

    def body(q_ref, k_ref, v_ref, o_ref, lse_ref, do_ref, *rest):
        p_refs, (dq_ref, dk_ref, dv_ref), rest = rest[:ne], rest[ne:ne + 3], rest[ne + 3:]
        out_refs, (dka, dva), sems = rest[:ne], rest[ne:ne + 2], rest[ne + 2:]
        if ne:
            start, finish = _exchange_steps(p_refs, out_refs, *sems)
            pl.when((pl.program_id(0) == 0) & (pl.program_id(1) == 0))(start)
        dka[...] = jnp.zeros_like(dka)
        dva[...] = jnp.zeros_like(dva)

        def tile(r0, nk, nrows):
            rows = pl.ds(r0, nrows)
            dqs = []
            for j in range(2):
                qj, kj = q_ref[rows, j * LANES:(j + 1) * LANES], k_ref[0:nk, j * LANES:(j + 1) * LANES]
                vj, doj = v_ref[0:nk, j * 64:(j + 1) * 64], do_ref[rows, j * 64:(j + 1) * 64]
                p = jnp.exp(_dg(qj, kj, 1, 1) - lse_ref[rows, j:j + 1])
                dsum = jnp.sum(doj * o_ref[rows, j * 64:(j + 1) * 64], axis=-1, keepdims=True)
                ds = p * (_dg(doj, vj, 1, 1) - dsum)
                dqs.append(_dg(ds, kj, 1, 0))
                dka[j, 0:nk, :] += _dg(ds, qj, 0, 0)
                dva[j, 0:nk, :] += _dg(p, doj, 0, 0)
            dq_ref[rows, :] = jnp.concatenate(dqs, axis=-1)

        _attn_loop(tile, lc, lt, ATTN_ROWS_BWD)
        dk_ref[...] = jnp.concatenate([dka[0], dka[1]], axis=-1)
        dv_ref[...] = jnp.concatenate([dva[0], dva[1]], axis=-1)
        if ne:
            pl.when((pl.program_id(0) == nb - 1) & (pl.program_id(1) == MLA_HEADS // 2 - 1))(finish)

    t = nb * lt
    pair = lambda width: pl.BlockSpec((lt, width), lambda b, h: (b, h))
    res = pl.pallas_call(
        body, name=name, grid=(nb, MLA_HEADS // 2),
        in_specs=[pair(2 * LANES), pair(2 * LANES), pair(LANES), pair(LANES), pair(LANES), pair(LANES)] + [ANY] * ne,
        out_specs=[pair(2 * LANES), pair(2 * LANES), pair(LANES)] + [ANY] * ne,
        out_shape=[jax.ShapeDtypeStruct((t, MLA_HEADS * LANES), F32), jax.ShapeDtypeStruct((t, MLA_HEADS * LANES), F32),
                   jax.ShapeDtypeStruct((t, MLA_HEADS * 64), F32)] + [jax.ShapeDtypeStruct(e.shape, e.dtype) for e in exchange],
        scratch_shapes=[pltpu.VMEM((2, lt, LANES), F32), pltpu.VMEM((2, lt, 64), F32)]
        + ([_dma_sems(ne, 3), _dma_sems(ne, 3), _dma_sems(ne, 1)] if ne else []),
    )(q, k, v, o, lse, do, *exchange)
    return res[0], res[1], res[2], list(res[3:])


HALO = 16


def _gate_specs(u, per_batch, lc):
    gh = GATE_ROWS // HALO
    nh = u.shape[0] // HALO
    width = u.shape[1]
    main = pl.BlockSpec((GATE_ROWS, width), lambda i: (i, 0))
    prev = pl.BlockSpec((HALO, width), lambda i: (jnp.maximum(i * gh - 1, 0), 0))
    nxt = pl.BlockSpec((HALO, width), lambda i: (jnp.minimum((i + 1) * gh, nh - 1), 0))
    return main, prev, nxt


def _seg_edges(per_batch, lc):
    j = pl.program_id(0) % (per_batch // GATE_ROWS)
    first = (j == 0) | (j == lc // GATE_ROWS)
    last = (j == lc // GATE_ROWS - 1) | (j == per_batch // GATE_ROWS - 1)
    return first, last


def _shifted(x, prev_ref, next_ref, first, last):
    rows = _iota(x.shape, 0)
    before = jnp.where(first, 0.0, prev_ref[HALO - 1:HALO, :].astype(F32))
    after = jnp.where(last, 0.0, next_ref[0:1, :].astype(F32))
    xm = jnp.where(rows == 0, before, pltpu.roll(x, 1, 0))
    xp = jnp.where(rows == x.shape[0] - 1, after, pltpu.roll(x, x.shape[0] - 1, 0))
    return xm, xp


def _whole(a):
    return pl.BlockSpec(a.shape, lambda i: (0,) * a.ndim)


def gate_fwd(u, cw, cb, wdown, per_batch, lc, name):
    main, prev, nxt = _gate_specs(u, per_batch, lc)
    f = u.shape[1] // 2

    def body(u_ref, p_ref, n_ref, w_ref, b_ref, wd_ref, act_ref, f_ref):
        first, last = _seg_edges(per_batch, lc)
        x = u_ref[...].astype(F32)
        xm, xp = _shifted(x, p_ref, n_ref, first, last)
        c = w_ref[0:1, :] * xm + w_ref[1:2, :] * x + w_ref[2:3, :] * xp + b_ref[...]
        act = silu(c[:, :f]) * c[:, f:]
        act_ref[...] = act.astype(act_ref.dtype)
        f_ref[...] = _dg(act, wd_ref[...], 1, 0)

    return pl.pallas_call(
        body, name=name, grid=(u.shape[0] // GATE_ROWS,),
        in_specs=[main, prev, nxt, _whole(cw), _whole(cb), _whole(wdown)],
        out_specs=[pl.BlockSpec((GATE_ROWS, f), lambda i: (i, 0)), pl.BlockSpec((GATE_ROWS, wdown.shape[1]), lambda i: (i, 0))],
        out_shape=[jax.ShapeDtypeStruct((u.shape[0], f), BF), jax.ShapeDtypeStruct((u.shape[0], wdown.shape[1]), F32)],
    )(u, u, u, cw, cb, wdown)


def gate_bwd(u, cw, cb, dact, per_batch, lc, name):
    main, prev, nxt = _gate_specs(u, per_batch, lc)
    f = u.shape[1] // 2

    def body(u_ref, p_ref, n_ref, w_ref, b_ref, da_ref, dc_ref, dw_ref):
        first, last = _seg_edges(per_batch, lc)
        x = u_ref[...].astype(F32)
        xm, xp = _shifted(x, p_ref, n_ref, first, last)
        c = w_ref[0:1, :] * xm + w_ref[1:2, :] * x + w_ref[2:3, :] * xp + b_ref[...]
        a, g = c[:, :f], c[:, f:]
        sg = jax.nn.sigmoid(a)
        da = da_ref[...]
        dc = jnp.concatenate([da * g * (sg * (1.0 + a * (1.0 - sg))), da * (a * sg)], axis=-1)
        dc_ref[...] = dc.astype(dc_ref.dtype)
        part = jnp.concatenate([jnp.sum(xm * dc, axis=0, keepdims=True), jnp.sum(x * dc, axis=0, keepdims=True),
                                jnp.sum(xp * dc, axis=0, keepdims=True), jnp.sum(dc, axis=0, keepdims=True),
                                jnp.zeros((4, 2 * f), F32)], axis=0)

        @pl.when(pl.program_id(0) == 0)
        def _():
            dw_ref[...] = part

        @pl.when(pl.program_id(0) > 0)
        def _():
            dw_ref[...] += part

    return pl.pallas_call(
        body, name=name, grid=(u.shape[0] // GATE_ROWS,),
        in_specs=[main, prev, nxt, _whole(cw), _whole(cb), pl.BlockSpec((GATE_ROWS, f), lambda i: (i, 0))],
        out_specs=[main, pl.BlockSpec((8, 2 * f), lambda i: (0, 0))],
        out_shape=[jax.ShapeDtypeStruct(u.shape, BF), jax.ShapeDtypeStruct((8, 2 * f), F32)],
    )(u, u, u, cw, cb, dact)


def conv_transpose(dc, cw, per_batch, lc, name):
    main, prev, nxt = _gate_specs(dc, per_batch, lc)

    def body(d_ref, p_ref, n_ref, w_ref, du_ref):
        first, last = _seg_edges(per_batch, lc)
        x = d_ref[...].astype(F32)
        xm, xp = _shifted(x, p_ref, n_ref, first, last)
        du_ref[...] = (w_ref[0:1, :] * xp + w_ref[1:2, :] * x + w_ref[2:3, :] * xm).astype(du_ref.dtype)

    return pl.pallas_call(
        body, name=name, grid=(dc.shape[0] // GATE_ROWS,),
        in_specs=[main, prev, nxt, _whole(cw)],
        out_specs=main, out_shape=jax.ShapeDtypeStruct(dc.shape, BF),
    )(dc, dc, dc, cw)


def loss_head(x, target, tiles_per_batch, ctx_tiles, name):
    n_tiles = x.shape[0] // ROW_TILE
    lat_tiles = tiles_per_batch - ctx_tiles

    def tgt_idx(i):
        j = i % tiles_per_batch
        return jnp.where(j < ctx_tiles, 0, (i // tiles_per_batch) * lat_tiles + j - ctx_tiles), 0

    def body(x_ref, t_ref, dx_ref, l_ref):
        lat = (pl.program_id(0) % tiles_per_batch >= ctx_tiles).astype(F32)
        err = (x_ref[...] - t_ref[...]) * lat
        dx_ref[...] = err * (1.0 / D_MODEL)
        l_ref[...] = jnp.full(l_ref.shape, 0.5 / D_MODEL * jnp.sum(err * err), F32)

    return pl.pallas_call(
        body, name=name, grid=(n_tiles,),
        in_specs=[pl.BlockSpec((ROW_TILE, D_MODEL), lambda i: (i, 0)), pl.BlockSpec((ROW_TILE, D_MODEL), tgt_idx)],
        out_specs=[pl.BlockSpec((ROW_TILE, D_MODEL), lambda i: (i, 0)), pl.BlockSpec((1, 8, LANES), lambda i: (i, 0, 0))],
        out_shape=[jax.ShapeDtypeStruct(x.shape, F32), jax.ShapeDtypeStruct((n_tiles, 8, LANES), F32)],
    )(x, target)


def adamw(w, g, m, v, name):
    rows, cols = w.shape
    tr = rows
    for cand in (512, 256, 128, 64, 32, 16, 8):
        if rows % cand == 0 and cand * cols * 4 <= (1 << 20):
            tr = cand
            break

    def body(w_ref, g_ref, m_ref, v_ref, d_ref, mo_ref, vo_ref):
        gg = g_ref[...]
        m2 = ADAM_B1 * m_ref[...] + (1.0 - ADAM_B1) * gg
        v2 = ADAM_B2 * v_ref[...] + (1.0 - ADAM_B2) * (gg * gg)
        m_hat = m2 / (1.0 - ADAM_B1 ** ADAM_STEP)
        v_hat = v2 / (1.0 - ADAM_B2 ** ADAM_STEP)
        d_ref[...] = -ADAM_LR * (m_hat / (jnp.sqrt(v_hat) + ADAM_EPS) + ADAM_WD * w_ref[...])
        mo_ref[...] = m2
        vo_ref[...] = v2

    spec = pl.BlockSpec((tr, cols), lambda i: (i, 0))
    return pl.pallas_call(body, name=name, grid=(rows // tr,), in_specs=[spec] * 4, out_specs=[spec] * 3,
                          out_shape=[jax.ShapeDtypeStruct(w.shape, F32)] * 3)(w, g, m, v)


def fn_post_mod(rows, tps, consts, ws):
    (x, a), (g, sc, sh), (lng, lnb) = rows, tps, ws
    y = layer_norm(ALPHA * x + g * a, lng, lnb)
    return [y, y * (1.0 + sc) + sh]


def _pick(n, cands):
    for c in cands:
        if n % c == 0:
            return c
    return n


def rope_tables(lc, l):
    pos = jnp.arange(l, dtype=F32)
    ret_inv = 1.0 / (ROPE_BASE ** jnp.linspace(0.0, 1.0, RET_DK // 2, dtype=F32))
    ang = pos[:, None] * ret_inv
    rc, rs = jnp.cos(ang), jnp.sin(ang)
    n_ax = MLA_D_ROPE // 4
    ax_inv = ROPE_BASE ** (-jnp.arange(n_ax, dtype=F32) / n_ax)
    rows_n = l // GRID_W
    rows = jnp.repeat(jnp.arange(rows_n, dtype=F32), GRID_W)
    cols = jnp.tile(jnp.arange(GRID_W, dtype=F32), rows_n)
    ra, ca = rows[:, None] * ax_inv, cols[:, None] * ax_inv
    rwc, rws, clc, cls = jnp.cos(ra), jnp.sin(ra), jnp.cos(ca), jnp.sin(ca)
    one = lambda n: jnp.ones((l, n), F32)
    zero = lambda n: jnp.zeros((l, n), F32)
    cat = lambda parts: jnp.concatenate(parts, axis=1)

    def with_ctx(tab, is_cos):
        head = jnp.ones((lc, LANES), F32) if is_cos else jnp.zeros((lc, LANES), F32)
        return jnp.concatenate([head, tab], axis=0)

    ret = (cat([rc, rc] * 4), cat([-rs, zero(16)] * 4), cat([zero(16), rs] * 4))
    ax_c = [rwc, rwc, clc, clc]
    ax_a = [-rws, zero(8), -cls, zero(8)]
    ax_b = [zero(8), rws, zero(8), cls]
    qt = (cat([one(64)] + ax_c + [one(32)]), cat([zero(64)] + ax_a + [zero(32)]), cat([zero(64)] + ax_b + [zero(32)]))
    kt = (cat([one(32)] + ax_c + [one(64)]), cat([zero(32)] + ax_a + [zero(64)]), cat([zero(32)] + ax_b + [zero(64)]))
    fix = lambda t3: tuple(with_ctx(t, k == 0) for k, t in enumerate(t3))
    return fix(ret), fix(qt), fix(kt)


_IN_ORDER = ((0, 128), (128, 256), (256, 512), (544, 800), (800, 928), (928, 1056), (1056, 1312), (1312, 1568),
             (1568, 1824), (1824, 1952), (512, 544), (1952, 1984))


def permute_w_in(w):
    parts = [w[:, a:b] for a, b in _IN_ORDER] + [jnp.zeros((w.shape[0], D_INP - D_IN), w.dtype)]
    return jnp.concatenate(parts, axis=1)


def unpermute_w_in(g):
    out, at = {}, 0
    for a, b in _IN_ORDER:
        out[a] = g[:, at:at + b - a]
        at += b - a
    return jnp.concatenate([out[a] for a in sorted(out)], axis=1)


def layer_weights(big, w, l):
    f = lambda a: a.astype(F32)
    r = {}
    r["ada_w"] = big["ada_w"].astype(BF)
    r["win"] = permute_w_in(big["w_in"]).astype(BF)
    r["wout"] = big["w_out"].astype(BF)
    r["wup"] = big["ffn_up"].astype(BF)
    r["wdown"] = big["ffn_down"].astype(BF)
    uq = big["mla_w_uq"].reshape(256, MLA_HEADS, 96)
    r["wuq"] = jnp.pad(uq, ((0, 0), (0, 0), (0, 32))).reshape(256, 8 * LANES).astype(BF)
    uk = big["mla_w_uk"].reshape(128, MLA_HEADS, 64)
    r["wuk"] = jnp.pad(uk, ((0, 0), (0, 0), (0, 64))).reshape(128, 8 * LANES).astype(BF)
    r["wuv"] = big["mla_w_uv"].astype(BF)
    gw = f(w["gla_gate_w"][l])
    z16 = jnp.zeros((16, LANES), F32)
    z96 = jnp.zeros((96, LANES), F32)
    r["gw"] = (jnp.concatenate([gw[0], z16, z96], axis=0), jnp.concatenate([z16, gw[1], z96], axis=0))
    r["gb"] = tuple(f(w["gla_gate_b"][l][d]).reshape(1, LANES) for d in range(2))
    r["rdec"] = tuple(jnp.repeat(f(w["ret_decay"][l][d]), 32).reshape(1, LANES) for d in range(2))
    r["gng"] = jnp.tile(f(w["gla_norm_g"][l]), 4).reshape(1, 256)
    r["gq"] = f(w["mla_q_norm_g"][l]).reshape(1, 256)
    r["gkv"] = f(w["mla_kv_norm_g"][l]).reshape(1, 128)
    for n in ("ln1_g", "ln1_b", "ln2_g", "ln2_b"):
        r[n] = f(w[n][l]).reshape(1, D_MODEL)
    r["cw"] = f(w["ffn_conv_w"][l])
    r["cb"] = f(w["ffn_conv_b"][l]).reshape(1, 2 * D_FF)
    return r


def tile_params(mod_l, nb, nct, nlt):
    m6 = mod_l.reshape(8, 6, D_MODEL)
    out = []
    for j in range(6):
        parts = []
        for b in range(nb):
            parts.append(jnp.broadcast_to(m6[4, j], (nct, 1, D_MODEL)))
            parts.append(jnp.broadcast_to(m6[b, j], (nlt, 1, D_MODEL)))
        out.append(jnp.concatenate(parts, axis=0))
    return out


def tile_param_grads(dts, nb, nct, nlt):
    cols = []
    for dt in dts:
        d = dt.reshape(nb, nct + nlt, D_MODEL)
        lat = jnp.sum(d[:, nct:], axis=1)
        ctx = jnp.sum(d[:, :nct], axis=(0, 1))
        cols.append(jnp.concatenate([lat, jnp.zeros((4 - nb, D_MODEL), F32), ctx[None], jnp.zeros((3, D_MODEL), F32)], axis=0))
    return jnp.stack(cols, axis=1).reshape(8, 6 * D_MODEL)


def layer_forward(x, h1, tp, lw, tabs, dims, nxt):
    nb, lc, lt = dims
    t = x.shape[0]
    nbt = lt // ROW_TILE
    ncc, nch = lc // CHUNK, lt // CHUNK
    tm = _pick(t, (1024, 768, 512, 256))
    ret_tab, q_tab, k_tab = tabs
    full = lambda a: (a, a.shape[1], 0)
    p = matmul(h1, lw["win"], "nn", F32, tm, D_INP, 1024, "proj_in")
    ogf, orf, sgf, srf = scan_fwd(p, ret_tab, lw["gw"][0], lw["gb"][0], lw["rdec"][0], nb, ncc, nch, False, "scan_fwd_f")
    ogb, orb, sgb, srb = scan_fwd(p, ret_tab, lw["gw"][1], lw["gb"][1], lw["rdec"][1], nb, ncc, nch, True, "scan_fwd_b")
    prep_rows = [(p, 256, C_CQ // 256), (p, 128, C_CKV // 128), (p, 128, C_LK // 128)]
    prep_consts = [(a, nbt) for a in q_tab + k_tab]
    prep_ws = [lw["gq"], lw["gkv"], lw["wuq"], lw["wuk"], lw["wuv"]]
    q, k, v = stage_fwd(fn_mla_prep, prep_rows, [], prep_consts, prep_ws, [(1024, BF), (1024, BF), (512, BF)], "mla_prep")
    if nxt is None:
        mo, lse, _ = mla_fwd(q, k, v, nb, lc, lt, "mla_attn_last")
        made, tp_next = None, None
    else:
        mo, lse, got = mla_fwd(q, k, v, nb, lc, lt, "mla_attn", gather=nxt[0])
        made = nxt[1](got)
        tp_next = (made[0][1], made[0][0])
    mix_rows = [full(ogf), full(ogb), full(orf), full(orb), (p, 256, C_GG // 256), (p, 256, C_RG // 256), full(mo)]
    m, = stage_fwd(fn_mix, mix_rows, [], [], [lw["gng"]], [(1024, BF)], "mix")
    a = matmul(m, lw["wout"], "nn", F32, tm, 1024, 1024, "proj_out")
    x1, h2 = stage_fwd(fn_post_mod, [full(x), full(a)], [tp[2], tp[4], tp[3]], [], [lw["ln1_g"], lw["ln1_b"]],
                       [(1024, F32), (1024, BF)], "post1")
    u = matmul(h2, lw["wup"], "nn", BF, tm, 1408, 1024, "ffn_up", b_outer=True)
    act, f = gate_fwd(u, lw["cw"], lw["cb"], lw["wdown"], lt, lc, "ffn_gate_down")
    if tp_next is None:
        x2, = stage_fwd(fn_post, [full(x1), full(f)], [tp[5]], [], [lw["ln2_g"], lw["ln2_b"]], [(1024, F32)], "post2_last")
        h1n = None
    else:
        x2, h1n = stage_fwd(fn_post_mod, [full(x1), full(f)], [tp[5], tp_next[0], tp_next[1]], [],
                            [lw["ln2_g"], lw["ln2_b"]], [(1024, F32), (1024, BF)], "post2")
    res = dict(x=x, h1=h1, p=p, ogf=ogf, orf=orf, sgf=sgf, srf=srf, ogb=ogb, orb=orb, sgb=sgb, srb=srb, q=q, k=k, v=v,
               mo=mo, lse=lse, m=m, a=a, x1=x1, h2=h2, u=u, act=act, f=f, mix_rows=mix_rows, prep_rows=prep_rows,
               prep_consts=prep_consts, prep_ws=prep_ws)
    return x2, h1n, res, made


def layer_backward(dx2, dh1n, res, tp, tp_next, lw, tabs, dims, exchange=()):
    nb, lc, lt = dims
    r = res
    t = dx2.shape[0]
    ncc, nch = lc // CHUNK, lt // CHUNK
    tm = _pick(t, (1024, 768, 512, 256))
    tkr = _pick(t, (2304, 1536, 1024, 768, 512))
    ret_tab = tabs[0]
    full = lambda a: (a, a.shape[1], 0)
    g = {}
    if tp_next is None:
        (dx1a, df), (dg2,), (g["ln2_g"], g["ln2_b"]) = stage_bwd(
            fn_post, [full(r["x1"]), full(r["f"])], [tp[5]], [], [lw["ln2_g"], lw["ln2_b"]], [dx2], [F32, BF], "post2_last_bwd")
        dnext = None
    else:
        (dx1a, df), (dg2, dsc1n, dsh1n), (g["ln2_g"], g["ln2_b"]) = stage_bwd(
            fn_post_mod, [full(r["x1"]), full(r["f"])], [tp[5], tp_next[0], tp_next[1]], [], [lw["ln2_g"], lw["ln2_b"]],
            [dx2, dh1n], [F32, BF], "post2_bwd")
        dnext = (dsc1n, dsh1n)
    g["ffn_down"] = matmul(r["act"], df, "tn", F32, 1408, 1024, tkr, "ffn_down_dw")
    dact = matmul(df, lw["wdown"], "nt", F32, tm, 1408, 1024, "ffn_down_dx", b_outer=True)
    dc, dcw = gate_bwd(r["u"], lw["cw"], lw["cb"], dact, lt, lc, "ffn_gate_bwd")
    g["ffn_conv_w"], g["ffn_conv_b"] = dcw[0:3], dcw[3]
    du = conv_transpose(dc, lw["cw"], lt, lc, "ffn_conv_t")
    dh2 = matmul(du, lw["wup"], "nt", F32, _pick(t, (512, 256)), 1024, 2 * D_FF, "ffn_up_dx")
    g["ffn_up"] = matmul(r["h2"], du, "tn", F32, 1024, 1408, tkr, "ffn_up_dw")
    (dxa, da), (dg1, dsc2, dsh2), (g["ln1_g"], g["ln1_b"]) = stage_bwd(
        fn_post_mod, [full(r["x"]), full(r["a"])], [tp[2], tp[4], tp[3]], [], [lw["ln1_g"], lw["ln1_b"]],
        [dx1a, dh2], [F32, BF], "post1_bwd")
    dm = matmul(da, lw["wout"], "nt", F32, tm, 1024, 1024, "proj_out_dx")
    g["w_out"] = matmul(r["m"], da, "tn", F32, 1024, 1024, tkr, "proj_out_dw")
    (dog, _, dor, _, dpg, dpr, dmo), _, (dgng,) = stage_bwd(
        fn_mix, r["mix_rows"], [], [], [lw["gng"]], [dm], [F32, None, F32, None, F32, F32, F32], "mix_bwd")
    g["gla_norm_g"] = jnp.sum(dgng.reshape(4, 64), axis=0)
    dq, dk, dv, landed = mla_bwd(r["q"], r["k"], r["v"], r["mo"], r["lse"], dmo, nb, lc, lt,
                                 "mla_attn_bwd" if exchange else "mla_attn_bwd_first", exchange=exchange)
    (dpq, dpkv, dplk), _, (dgq, dgkv, dwuq, dwuk, dwuv) = stage_bwd(
        fn_mla_prep, r["prep_rows"], [], r["prep_consts"], r["prep_ws"], [dq, dk, dv], [F32, F32, F32], "mla_prep_bwd")
    g["mla_q_norm_g"], g["mla_kv_norm_g"] = dgq.reshape(256), dgkv.reshape(128)
    g["mla_w_uq"] = dwuq.reshape(256, MLA_HEADS, LANES)[:, :, :96].reshape(256, MLA_HEADS * 96)
    g["mla_w_uk"] = dwuk.reshape(128, MLA_HEADS, LANES)[:, :, :64].reshape(128, MLA_HEADS * 64)
    g["mla_w_uv"] = dwuv
    s7, dgw0, dgb0, drd0 = scan_bwd(r["p"], ret_tab, lw["gw"][0], lw["gb"][0], lw["rdec"][0], r["sgf"], r["srf"], dog, dor,
                                    None, nb, ncc, nch, False, "scan_bwd_f")
    s7, dgw1, dgb1, drd1 = scan_bwd(r["p"], ret_tab, lw["gw"][1], lw["gb"][1], lw["rdec"][1], r["sgb"], r["srb"], dog, dor,
                                    s7, nb, ncc, nch, True, "scan_bwd_b")
    g["gla_gate_w"] = jnp.stack([dgw0[0:16], dgw1[16:32]])
    g["gla_gate_b"] = jnp.stack([dgb0[0], dgb1[0]])
    g["ret_decay"] = jnp.stack([jnp.sum(drd0.reshape(4, 32), axis=1), jnp.sum(drd1.reshape(4, 32), axis=1)])
    gq_, gk_, gv_, glrk, rq_, rk_, rv_ = s7
    pieces = [gq_, gk_, gv_, dpg, rq_, rk_, rv_, dpr, dpq, dpkv, glrk, dplk]
    dp, = stage_fwd(fn_assemble, [full(a) for a in pieces], [], [], [], [(D_INP, BF)], "dproj_assemble")
    dh1 = matmul(dp, lw["win"], "nt", F32, tm, 1024, D_INP, "proj_in_dx")
    g["w_in"] = unpermute_w_in(matmul(r["h1"], dp, "tn", F32, 1024, 1024, tkr, "proj_in_dw"))
    for n in ("ln1_g", "ln1_b", "ln2_g", "ln2_b"):
        g[n] = g[n].reshape(D_MODEL)
    dtp = [None, None, dg1, dsh2, dsc2, dg2]
    return dxa, dh1, dtp, dnext, g, landed


def local_step(x, c, ctx, c_ctx, w, loss_target, first_big, next_blocks, assemble, reduce_hooks=None):
    nb, l, _ = x.shape
    lc = ctx.shape[1]
    lt = lc + l
    dims = (nb, lc, lt)
    nct, nlt = lc // ROW_TILE, l // ROW_TILE
    tabs = rope_tables(lc, l)
    x0 = jnp.concatenate([ctx, x], axis=1).reshape(nb * lt, D_MODEL)
    s8 = jnp.concatenate([silu(c), jnp.zeros((4 - nb, D_MODEL), F32), silu(c_ctx)[None], jnp.zeros((3, D_MODEL), F32)], axis=0)

    def make_layer(i, big):
        lw = layer_weights(big, w, i)
        mod = matmul(s8, lw["ada_w"], "nn", F32, 8, 1536, 1024, "ada_mod") + w["ada_b"][i].astype(F32)[None]
        return tile_params(mod, nb, nct, nlt), lw

    made = make_layer(0, first_big)
    lws, tps = [], []
    h1, = stage_fwd(fn_modulate, [(x0, D_MODEL, 0)], [made[0][1], made[0][0]], [], [], [(D_MODEL, BF)], "mod_in")
    xs, ress = x0, []
    for i in range(DEPTH):
        tps.append(made[0])
        lws.append(made[1])
        nxt = None
        if i < DEPTH - 1:
            nxt = (next_blocks(i + 1), functools.partial(lambda got, j: make_layer(j, assemble(j, got)), j=i + 1))
        xs, h1, res, made = layer_forward(xs, h1, tps[i], lws[i], tabs, dims, nxt)
        ress.append(res)
    dx, lparts = loss_head(xs, loss_target.reshape(nb * l, D_MODEL), nct + nlt, nct, "loss_head")
    loss = jnp.sum(lparts[:, 0, 0])
    grads = [None] * DEPTH
    dtps = [None] * DEPTH
    reduced = [None] * DEPTH
    dh1, parts = None, ()
    for i in reversed(range(DEPTH)):
        tpn = None if i == DEPTH - 1 else (tps[i + 1][1], tps[i + 1][0])
        dx, dh1, dtp, dn, grads[i], landed = layer_backward(dx, dh1, ress[i], tps[i], tpn, lws[i], tabs, dims, parts)
        if dn is not None:
            dtps[i + 1][1], dtps[i + 1][0] = dn
        dtps[i] = dtp
        if reduce_hooks is not None:
            if parts:
                reduced[i + 1] = reduce_hooks[1](landed)
            parts = reduce_hooks[0]({n: grads[i].pop(n) for n in REDUCED})
    if reduce_hooks is not None:
        reduced[0] = reduce_hooks[1](exchange_chips(parts, "grad_exchange_last"))
    (dx0b,), (dsc1, dsh1), _ = stage_bwd(fn_modulate, [(x0, D_MODEL, 0)], [tps[0][1], tps[0][0]], [], [], [dh1], [F32], "mod_in_bwd")
    dtps[0][1], dtps[0][0] = dsc1, dsh1
    grad_x = (dx + dx0b).reshape(nb, lt, D_MODEL)[:, lc:]
    dmod = jnp.stack([tile_param_grads(d, nb, nct, nlt) for d in dtps])
    gw = {n: jnp.stack([grads[i][n] for i in range(DEPTH)]) for n in grads[0]}
    return loss, grad_x, gw, dmod, s8, lws, reduced


MESH_IDS = pl.DeviceIdType.MESH
ANY = pl.BlockSpec(memory_space=pl.ANY)


def _place():
    return lax.axis_index("x"), lax.axis_index("y"), lax.axis_index("c")


def _dma_sems(n, per):
    return pltpu.SemaphoreType.DMA((n, per))


def _gather8_steps(x_refs, out_refs, send_sems, recv_sems, local_sems):
    n = len(x_refs)
    x, y, c = _place()
    me, sibling = (x, y, c), (x, y, 1 - c)
    chips = [(1 - x, y), (x, 1 - y), (1 - x, 1 - y)]

    def copy(a, k, blk, to, own=False):
        slot = out_refs[a].at[4 * blk[0] + 2 * blk[1] + blk[2]]
        return pltpu.make_async_remote_copy(
            src_ref=x_refs[a] if own else slot, dst_ref=slot,
            send_sem=send_sems.at[a, k], recv_sem=recv_sems.at[a, k], device_id=to, device_id_type=MESH_IDS)

    def local(a):
        return pltpu.make_async_copy(x_refs[a], out_refs[a].at[4 * x + 2 * y + c], local_sems.at[a, 0])

    def first_copies():
        cps = []
        for a in range(n):
            cps.append(copy(a, 0, me, sibling, own=True))
            cps += [copy(a, 1 + j, me, (*chip, c), own=True) for j, chip in enumerate(chips)]
        return cps

    def start():
        for a in range(n):
            local(a).start()
        for cp in first_copies():
            cp.start()

    def finish():
        passed = []
        for j, chip in enumerate(chips):
            for a in range(n):
                copy(a, 1 + j, (*chip, c), me).wait_recv()
                passed.append(copy(a, 4 + j, (*chip, c), sibling))
                passed[-1].start()
        for a in range(n):
            copy(a, 0, sibling, me).wait_recv()
            for j, chip in enumerate(chips):
                copy(a, 4 + j, (*chip, 1 - c), me).wait_recv()
        for cp in first_copies() + passed:
            cp.wait_send()
        for a in range(n):
            local(a).wait()

    return start, finish


def all_gather8(blocks, name):
    n = len(blocks)

    def body(*refs):
        start, finish = _gather8_steps(refs[:n], refs[n:2 * n], *refs[2 * n:])
        start()
        finish()

    return pl.pallas_call(
        body, name=name, out_shape=[jax.ShapeDtypeStruct((8,) + b.shape, b.dtype) for b in blocks],
        in_specs=[ANY] * n, out_specs=[ANY] * n,
        scratch_shapes=[_dma_sems(n, 7), _dma_sems(n, 7), _dma_sems(n, 1)],
    )(*blocks)


def swap_cores(blocks, name):
    n = len(blocks)

    def body(*refs):
        x_refs, out_refs, (send_sems, recv_sems) = refs[:n], refs[n:2 * n], refs[2 * n:]
        x, y, c = _place()
        cps = [pltpu.make_async_remote_copy(src_ref=x_refs[a], dst_ref=out_refs[a], send_sem=send_sems.at[a, 0],
                                            recv_sem=recv_sems.at[a, 0], device_id=(x, y, 1 - c), device_id_type=MESH_IDS)
               for a in range(n)]
        for cp in cps:
            cp.start()
        for cp in cps:
            cp.wait()

    return pl.pallas_call(
        body, name=name, out_shape=[jax.ShapeDtypeStruct(b.shape, b.dtype) for b in blocks],
        in_specs=[ANY] * n, out_specs=[ANY] * n, scratch_shapes=[_dma_sems(n, 1), _dma_sems(n, 1)],
    )(*blocks)


def _exchange_steps(p_refs, out_refs, send_sems, recv_sems, local_sems):
    n = len(p_refs)
    x, y, c = _place()
    jm = 2 * x + y
    chips = [(1 - x, y), (x, 1 - y), (1 - x, 1 - y)]

    def local(a):
        return pltpu.make_async_copy(p_refs[a].at[jm], out_refs[a].at[jm], local_sems.at[a, 0])

    def sends():
        return [pltpu.make_async_remote_copy(
            src_ref=p_refs[a].at[2 * px + py], dst_ref=out_refs[a].at[jm], send_sem=send_sems.at[a, k],
            recv_sem=recv_sems.at[a, k], device_id=(px, py, c), device_id_type=MESH_IDS)
            for k, (px, py) in enumerate(chips) for a in range(n)]

    def start():
        for a in range(n):
            local(a).start()
        for cp in sends():
            cp.start()

    def finish():
        for k, (px, py) in enumerate(chips):
            for a in range(n):
                pltpu.make_async_remote_copy(
                    src_ref=p_refs[a].at[jm], dst_ref=out_refs[a].at[2 * px + py], send_sem=send_sems.at[a, k],
                    recv_sem=recv_sems.at[a, k], device_id=(px, py, c), device_id_type=MESH_IDS).wait_recv()
        for cp in sends():
            cp.wait_send()
        for a in range(n):
            local(a).wait()

    return start, finish


def exchange_chips(parts, name):
    n = len(parts)

    def body(*refs):
        start, finish = _exchange_steps(refs[:n], refs[n:2 * n], *refs[2 * n:])
        start()
        finish()

    return pl.pallas_call(
        body, name=name, out_shape=[jax.ShapeDtypeStruct(p.shape, p.dtype) for p in parts],
        in_specs=[ANY] * n, out_specs=[ANY] * n, scratch_shapes=[_dma_sems(n, 3), _dma_sems(n, 3), _dma_sems(n, 1)],
    )(*parts)


def _row_tile(rows, cols, itemsize=4, limit=1 << 21):
    top = min(rows, limit // (cols * itemsize)) // 16 * 16
    for cand in range(top, 0, -16):
        if rows % cand == 0:
            return cand
    return rows


def add_halves(a, b, kind, name):
    if kind == "col":
        m, k, n = a.shape
        n4 = n // 4
        tr = _row_tile(k, n4)
        in_spec = pl.BlockSpec((1, tr, n4), lambda j, h, i: (h, i, j))
        out_spec = pl.BlockSpec((1, 1, tr, n4), lambda j, h, i: (j, h, i, 0))
        grid, out_shape = (4, m, k // tr), (4, m, k, n4)
    elif kind == "row":
        m, k, n = a.shape
        k4 = k // 4
        tr = _row_tile(k4, n)
        nt = k4 // tr
        in_spec = pl.BlockSpec((1, tr, n), lambda j, h, i: (h, j * nt + i, 0))
        out_spec = pl.BlockSpec((1, 1, tr, n), lambda j, h, i: (j, h, i, 0))
        grid, out_shape = (4, m, nt), (4, m, k4, n)
    else:
        _, m, k, n = a.shape
        tr = _row_tile(k, n)
        in_spec = out_spec = pl.BlockSpec((1, 1, tr, n), lambda j, h, i: (j, h, i, 0))
        grid, out_shape = (4, m, k // tr), a.shape

    def body(a_ref, b_ref, s_ref):
        s_ref[...] = (a_ref[...] + b_ref[...].astype(F32)).astype(BF).reshape(s_ref.shape)

    return pl.pallas_call(body, name=name, grid=grid, in_specs=[in_spec, in_spec], out_specs=out_spec,
                          out_shape=jax.ShapeDtypeStruct(out_shape, BF))(a, b)


def sum_slots(a, name):
    s, m, k, n = a.shape
    tr = _row_tile(k, n, limit=(1 << 22) // s)

    def body(a_ref, o_ref):
        acc = a_ref[0].astype(F32)
        for j in range(1, s):
            acc = acc + a_ref[j].astype(F32)
        o_ref[...] = acc

    return pl.pallas_call(body, name=name, grid=(m, k // tr), in_specs=[pl.BlockSpec((s, 1, tr, n), lambda h, i: (0, h, i, 0))],
                          out_specs=pl.BlockSpec((1, tr, n), lambda h, i: (h, i, 0)),
                          out_shape=jax.ShapeDtypeStruct((m, k, n), F32))(a)


def sum_small(arrays, name):
    n = len(arrays)

    def body(*refs):
        for a_ref, o_ref in zip(refs[:n], refs[n:]):
            acc = a_ref[0]
            for j in range(1, 8):
                acc = acc + a_ref[j]
            o_ref[...] = acc

    return pl.pallas_call(body, name=name, out_shape=[jax.ShapeDtypeStruct(a.shape[1:], F32) for a in arrays])(*arrays)


COL_SHARDED = ("ada_w", "w_in", "mla_w_uq", "mla_w_uk", "mla_w_uv", "ffn_up", "ffn_conv_w")
ROW_SHARDED = ("w_out", "ffn_down")
GATHERED = ("ada_w", "w_in", "mla_w_uq", "mla_w_uk", "mla_w_uv", "w_out", "ffn_up", "ffn_down", "ffn_conv_w")
LAYER_GATHERED = GATHERED[:-1]
REDUCED = ("w_in", "mla_w_uq", "mla_w_uk", "mla_w_uv", "w_out", "ffn_up", "ffn_down")
SMALL = ("ada_b", "gla_gate_w", "gla_gate_b", "gla_norm_g", "ret_decay", "mla_q_norm_g", "mla_kv_norm_g",
         "ln1_g", "ln1_b", "ffn_conv_b", "ln2_g", "ln2_b")
WEIGHTS = ("c_ctx", "ada_w", "ada_b", "w_in", "gla_gate_w", "gla_gate_b", "gla_norm_g", "ret_decay", "mla_q_norm_g",
           "mla_kv_norm_g", "mla_w_uq", "mla_w_uk", "mla_w_uv", "w_out", "ln1_g", "ln1_b", "ffn_up", "ffn_conv_w",
           "ffn_conv_b", "ffn_down", "ln2_g", "ln2_b")
PACK = 16 * LANES
HALF_LAYERS = DEPTH // 2


def _pad_flat(v, n):
    return jnp.concatenate([v, jnp.zeros((n - v.shape[0],), v.dtype)]) if n > v.shape[0] else v


def _my_layers(a, c):
    return lax.dynamic_slice_in_dim(a, HALF_LAYERS * c, HALF_LAYERS, axis=0)


def layer_blocks(shards, l, c):
    out = []
    for n in LAYER_GATHERED:
        a = shards[n][l]
        out.append(lax.dynamic_slice_in_dim(a, c * (a.shape[0] // 2), a.shape[0] // 2, axis=0).astype(BF))
    return out


def layer_assemble(got):
    out = {}
    for n, g in zip(LAYER_GATHERED, got):
        _, k2, n4 = g.shape
        if n in ROW_SHARDED:
            out[n] = g.reshape(8 * k2, n4)
        else:
            out[n] = jnp.transpose(g.reshape(4, 2, k2, n4), (1, 2, 0, 3)).reshape(2 * k2, 4 * n4)
    return out


def gather_conv_taps(shard, c):
    got, = all_gather8([_my_layers(shard, c)], "gather_conv_taps")
    _, _, k, n4 = got.shape
    return jnp.transpose(got.reshape(4, 2, HALF_LAYERS, k, n4), (1, 2, 3, 0, 4)).reshape(DEPTH, k, 4 * n4)


def _reduce_kind(n, g):
    return "row" if n in ROW_SHARDED else ("col" if (g.shape[-1] // 4) % LANES == 0 else "pre")


def reduce_begin(g, c):
    keep, give, kinds = [], [], []
    for n in REDUCED:
        a, kind = g[n], _reduce_kind(n, g[n])
        if kind == "pre":
            a = jnp.transpose(a.reshape(a.shape[0], 4, a.shape[1] // 4), (1, 0, 2))
        axis = a.ndim - 1 if kind == "row" else a.ndim - 2
        half = a.shape[axis] // 2
        lead = (slice(None), None) if kind == "pre" else (None,)
        keep.append(lax.dynamic_slice_in_dim(a, half * c, half, axis=axis)[lead])
        give.append(lax.dynamic_slice_in_dim(a, half * (1 - c), half, axis=axis)[lead].astype(BF))
        kinds.append(kind)
    got = swap_cores(give, "grad_swap_cores")
    return [add_halves(a, b, kind, "grad_add_cores_" + n) for n, a, b, kind in zip(REDUCED, keep, got, kinds)]


def reduce_end(landed):
    return [sum_slots(a, "grad_sum_chips_" + n) for n, a in zip(REDUCED, landed)]


def reduce_assemble(layers, c):
    mine = [jnp.concatenate([layers[l][k] for l in range(DEPTH)], axis=0) for k in range(len(REDUCED))]
    theirs = swap_cores(mine, "grad_swap_back")
    out = {}
    for n, a, b in zip(REDUCED, mine, theirs):
        out[n] = jnp.concatenate([jnp.where(c == 0, a, b), jnp.where(c == 0, b, a)], axis=2 if n in ROW_SHARDED else 1)
    return out


def _pack_small(d, names):
    flat = jnp.concatenate([d[n].astype(F32).reshape(-1) for n in names])
    n = -(-flat.shape[0] // PACK) * PACK
    return _pad_flat(flat, n).reshape(n // LANES, LANES)


def _unpack_small(buf, like, names):
    flat, out, at = buf.reshape(-1), {}, 0
    for n in names:
        sz = like[n].size
        out[n] = flat[at:at + sz].reshape(like[n].shape)
        at += sz
    return out


def kernel(x, c, ctx, c_ctx, ada_w, ada_b, w_in, gla_gate_w, gla_gate_b, gla_norm_g, ret_decay, mla_q_norm_g, mla_kv_norm_g, mla_w_uq, mla_w_uk, mla_w_uv, w_out, ln1_g, ln1_b, ffn_up, ffn_conv_w, ffn_conv_b, ffn_down, ln2_g, ln2_b, loss_target, m_c_ctx, m_ada_w, m_ada_b, m_w_in, m_gla_gate_w, m_gla_gate_b, m_gla_norm_g, m_ret_decay, m_mla_q_norm_g, m_mla_kv_norm_g, m_mla_w_uq, m_mla_w_uk, m_mla_w_uv, m_w_out, m_ln1_g, m_ln1_b, m_ffn_up, m_ffn_conv_w, m_ffn_conv_b, m_ffn_down, m_ln2_g, m_ln2_b, v_c_ctx, v_ada_w, v_ada_b, v_w_in, v_gla_gate_w, v_gla_gate_b, v_gla_norm_g, v_ret_decay, v_mla_q_norm_g, v_mla_kv_norm_g, v_mla_w_uq, v_mla_w_uk, v_mla_w_uv, v_w_out, v_ln1_g, v_ln1_b, v_ffn_up, v_ffn_conv_w, v_ffn_conv_b, v_ffn_down, v_ln2_g, v_ln2_b):
    w = dict(c_ctx=c_ctx, ada_w=ada_w, ada_b=ada_b, w_in=w_in, gla_gate_w=gla_gate_w, gla_gate_b=gla_gate_b, gla_norm_g=gla_norm_g, ret_decay=ret_decay, mla_q_norm_g=mla_q_norm_g, mla_kv_norm_g=mla_kv_norm_g, mla_w_uq=mla_w_uq, mla_w_uk=mla_w_uk, mla_w_uv=mla_w_uv, w_out=w_out, ln1_g=ln1_g, ln1_b=ln1_b, ffn_up=ffn_up, ffn_conv_w=ffn_conv_w, ffn_conv_b=ffn_conv_b, ffn_down=ffn_down, ln2_g=ln2_g, ln2_b=ln2_b)
    m = dict(c_ctx=m_c_ctx, ada_w=m_ada_w, ada_b=m_ada_b, w_in=m_w_in, gla_gate_w=m_gla_gate_w, gla_gate_b=m_gla_gate_b, gla_norm_g=m_gla_norm_g, ret_decay=m_ret_decay, mla_q_norm_g=m_mla_q_norm_g, mla_kv_norm_g=m_mla_kv_norm_g, mla_w_uq=m_mla_w_uq, mla_w_uk=m_mla_w_uk, mla_w_uv=m_mla_w_uv, w_out=m_w_out, ln1_g=m_ln1_g, ln1_b=m_ln1_b, ffn_up=m_ffn_up, ffn_conv_w=m_ffn_conv_w, ffn_conv_b=m_ffn_conv_b, ffn_down=m_ffn_down, ln2_g=m_ln2_g, ln2_b=m_ln2_b)
    v = dict(c_ctx=v_c_ctx, ada_w=v_ada_w, ada_b=v_ada_b, w_in=v_w_in, gla_gate_w=v_gla_gate_w, gla_gate_b=v_gla_gate_b, gla_norm_g=v_gla_norm_g, ret_decay=v_ret_decay, mla_q_norm_g=v_mla_q_norm_g, mla_kv_norm_g=v_mla_kv_norm_g, mla_w_uq=v_mla_w_uq, mla_w_uk=v_mla_w_uk, mla_w_uv=v_mla_w_uv, w_out=v_w_out, ln1_g=v_ln1_g, ln1_b=v_ln1_b, ffn_up=v_ffn_up, ffn_conv_w=v_ffn_conv_w, ffn_conv_b=v_ffn_conv_b, ffn_down=v_ffn_down, ln2_g=v_ln2_g, ln2_b=v_ln2_b)
    xi, yi, ci = _place()
    chip = 2 * xi + yi

    whole = {n: w[n] for n in WEIGHTS if n not in GATHERED and n != "c_ctx"}
    whole["ffn_conv_w"] = gather_conv_taps(ffn_conv_w, ci)
    first_big = layer_assemble(all_gather8(layer_blocks(w, 0, ci), "gather_layer0"))
    loss, grad_x, gw, dmod, s8, lws, reduced = local_step(
        x, c, ctx, c_ctx, whole, loss_target, first_big,
        lambda l: layer_blocks(w, l, ci), lambda l, got: layer_assemble(got),
        (lambda g: reduce_begin(g, ci), reduce_end))
    loss = lax.psum(loss, ("x", "y", "c"))

    dsil = jnp.zeros((8, D_MODEL), F32)
    for i in range(DEPTH):
        dsil = dsil + matmul(dmod[i], lws[i]["ada_w"], "nt", F32, 8, 1024, 1536, "ada_dsilu")

    grads = reduce_assemble(reduced, ci)

    small = {n: gw[n] for n in SMALL if n != "ada_b"}
    small.update(dsil=dsil[4])
    names = tuple(small)
    conv_g = gw["ffn_conv_w"].reshape(DEPTH * 3, 2 * D_FF)
    ev_small, ev_dmod, ev_s8, ev_conv = all_gather8(
        [_pack_small(small, names), dmod.reshape(DEPTH * 8, 6 * D_MODEL), s8, conv_g], "gather_small")
    sm_small, sm_dmod, sm_conv = sum_small([ev_small, ev_dmod, ev_conv], "sum_small")
    summed = _unpack_small(sm_small, small, names)
    for n in SMALL:
        if n != "ada_b":
            grads[n] = summed[n]
    grads["ada_b"] = jnp.sum(sm_dmod.reshape(DEPTH, 8, 6 * D_MODEL)[:, :5], axis=1)
    sg = jax.nn.sigmoid(c_ctx)
    grads["c_ctx"] = summed["dsil"] * (sg * (1.0 + c_ctx * (1.0 - sg)))
    ccols = ffn_conv_w.shape[2]
    grads["ffn_conv_w"] = lax.dynamic_slice_in_dim(sm_conv.reshape(DEPTH, 3, 2 * D_FF), chip * ccols, ccols, axis=2)
    s_all = ev_s8.reshape(64, D_MODEL)
    d_all = jnp.transpose(ev_dmod.reshape(8, DEPTH, 8, 6 * D_MODEL), (1, 0, 2, 3)).reshape(DEPTH, 64, 6 * D_MODEL)
    cols = ada_w.shape[2]
    g_ada = []
    for i in range(DEPTH):
        d_mine = lax.dynamic_slice_in_dim(d_all[i], chip * cols, cols, axis=1)
        g_ada.append(matmul(s_all, d_mine, "tn", F32, 1024, cols, 64, "ada_dw"))
    grads["ada_w"] = jnp.stack(g_ada)

    delta, new_m, new_v = {}, {}, {}
    for n in GATHERED:
        shp = w[n].shape
        v2 = lambda a: a.reshape(-1, shp[-1])
        d_, m_, v_ = adamw(v2(w[n]), v2(grads[n]), v2(m[n]), v2(v[n]), "adamw_" + n)
        delta[n], new_m[n], new_v[n] = d_.reshape(shp), m_.reshape(shp), v_.reshape(shp)
    rep = tuple(n for n in WEIGHTS if n not in GATHERED)
    pk = lambda d: _pack_small({n: d[n] for n in rep}, rep)
    d_, m_, v_ = adamw(pk(w), pk(grads), pk(m), pk(v), "adamw_small")
    like = {n: w[n] for n in rep}
    delta.update(_unpack_small(d_, like, rep))
    new_m.update(_unpack_small(m_, like, rep))
    new_v.update(_unpack_small(v_, like, rep))
    grads = {n: grads[n].reshape(w[n].shape) for n in WEIGHTS}
    return (loss, grad_x, *[grads[n] for n in WEIGHTS], *[delta[n] for n in WEIGHTS], *[new_m[n] for n in WEIGHTS],
            *[new_v[n] for n in WEIGHTS])
```

```python
import functools

import jax
import jax.numpy as jnp
from jax import lax
from jax.experimental import pallas as pl
from jax.experimental.pallas import tpu as pltpu

F32 = jnp.float32
BF = jnp.bfloat16

D_MODEL = 1024
DEPTH = 4
GRID_W = 64
GLA_DK = 32
GLA_TAU = 16.0
RET_DK = 32
MLA_HEADS = 8
MLA_D_NOPE = 64
MLA_D_ROPE = 32
MLA_SCALE = (MLA_D_NOPE + MLA_D_ROPE) ** -0.5
D_FF = 2816
ROPE_BASE = 10000.0
EPS = 1e-6
ALPHA = (2 * DEPTH) ** 0.25
ADAM_LR, ADAM_B1, ADAM_B2, ADAM_EPS, ADAM_WD, ADAM_STEP = 0.001, 0.9, 0.999, 1e-08, 0.01, 10

ROW_TILE = 256
CHUNK = 64
GATE_ROWS = 256
ATTN_ROWS_FWD = 1024
ATTN_ROWS_BWD = 512
LANES = 128

C_GQ, C_GK, C_GV, C_GG, C_RQ, C_RK, C_RV, C_RG, C_CQ, C_CKV, C_LK = 0, 128, 256, 512, 768, 896, 1024, 1280, 1536, 1792, 1920
D_INP = 2048
D_IN = 1984


def _dg(a, b, ca, cb):
    return lax.dot_general(a.astype(BF), b.astype(BF), (((ca,), (cb,)), ((), ())), preferred_element_type=F32)


@jax.custom_vjp
def mm_nn(a, b):
    return _dg(a, b, 1, 0)


@jax.custom_vjp
def mm_nt(a, b):
    return _dg(a, b, 1, 1)


@jax.custom_vjp
def mm_tn(a, b):
    return _dg(a, b, 0, 0)


mm_nn.defvjp(lambda a, b: (_dg(a, b, 1, 0), (a, b)),
             lambda r, g: (mm_nt(g, r[1]).astype(r[0].dtype), mm_tn(r[0], g).astype(r[1].dtype)))
mm_nt.defvjp(lambda a, b: (_dg(a, b, 1, 1), (a, b)),
             lambda r, g: (mm_nn(g, r[1]).astype(r[0].dtype), mm_tn(g, r[0]).astype(r[1].dtype)))
mm_tn.defvjp(lambda a, b: (_dg(a, b, 0, 0), (a, b)),
             lambda r, g: (mm_nt(r[1], g).astype(r[0].dtype), mm_nn(r[0], g).astype(r[1].dtype)))


def _split3(x):
    h = x.astype(BF)
    r = x - h.astype(F32)
    m = r.astype(BF)
    lo = (r - m.astype(F32)).astype(BF)
    return h, m, lo


def _exact(x, mat, left):
    h, m, lo = _split3(x)
    if left:
        d = lambda t: lax.dot_general(mat, t, (((1,), (0,)), ((), ())), preferred_element_type=F32)
    else:
        d = lambda t: lax.dot_general(t, mat, (((1,), (0,)), ((), ())), preferred_element_type=F32)
    return (d(lo) + d(m)) + d(h)


def _iota(shape, axis):
    return lax.broadcasted_iota(jnp.int32, shape, axis)


def _tri(n, upper):
    r, c = _iota((n, n), 0), _iota((n, n), 1)
    return jnp.where((c >= r) if upper else (r >= c), 1.0, 0.0).astype(BF)


@functools.partial(jax.custom_vjp, nondiff_argnums=(1,))
def cumsum_rows(x, upper):
    return _exact(x, _tri(x.shape[0], upper), True)


cumsum_rows.defvjp(lambda x, upper: (cumsum_rows(x, upper), None),
                   lambda upper, r, g: (cumsum_rows(g, not upper),))


def _seg(n, w):
    shift = w.bit_length() - 1
    r, c = _iota((n, n), 0), _iota((n, n), 1)
    return jnp.where(lax.shift_right_logical(r, shift) == lax.shift_right_logical(c, shift), 1.0, 0.0).astype(BF)


@functools.partial(jax.custom_vjp, nondiff_argnums=(1,))
def seg_sum(x, w):
    return _exact(x, _seg(x.shape[1], w), False)


seg_sum.defvjp(lambda x, w: (seg_sum(x, w), None), lambda w, r, g: (seg_sum(g, w),))


def _place_mat(transpose):
    shape = (8 * LANES, LANES) if transpose else (LANES, 8 * LANES)
    r, c = _iota(shape, 0), _iota(shape, 1)
    src, dst = (c, r) if transpose else (r, c)
    dl = jnp.bitwise_and(dst, LANES - 1)
    ok = (dl >= 64) & (dl < 96) & (src == dl - 32)
    return jnp.where(ok, 1.0, 0.0).astype(BF)


@jax.custom_vjp
def place_kr(x):
    return _exact(x, _place_mat(False), False)


place_kr.defvjp(lambda x: (place_kr(x), None), lambda r, g: (_exact(g, _place_mat(True), False),))


@functools.partial(jax.custom_vjp, nondiff_argnums=(1,))
def lane_roll(x, s):
    return pltpu.roll(x, s, 1)


lane_roll.defvjp(lambda x, s: (pltpu.roll(x, s, 1), None),
                 lambda s, r, g: (pltpu.roll(g, (g.shape[1] - s) % g.shape[1], 1),))


def rope(x, tab, d):
    cos, sa, sb = tab
    return x * cos + lane_roll(x, LANES - d) * sa + lane_roll(x, d) * sb


def silu(x):
    return x * jax.nn.sigmoid(x)


def log_sigmoid(z):
    return jnp.minimum(z, 0.0) - jnp.log(1.0 + jnp.exp(-jnp.abs(z)))


def layer_norm(x, g, b):
    mu = jnp.mean(x, axis=-1, keepdims=True)
    xc = x - mu
    var = jnp.mean(xc * xc, axis=-1, keepdims=True)
    return xc * lax.rsqrt(var + EPS) * g + b


def matmul(a, b, mode, out_dtype, tm, tn, tk, name, b_outer=False):
    ij = (lambda f: (lambda g0, g1, kk: f(g1, g0, kk))) if b_outer else (lambda f: f)
    if mode == "nn":
        (m, k), (k2, n) = a.shape, b.shape
        a_spec = pl.BlockSpec((tm, tk), ij(lambda i, j, kk: (i, kk)))
        b_spec = pl.BlockSpec((tk, tn), ij(lambda i, j, kk: (kk, j)))
        ca, cb = 1, 0
    elif mode == "nt":
        (m, k), (n, k2) = a.shape, b.shape
        a_spec = pl.BlockSpec((tm, tk), ij(lambda i, j, kk: (i, kk)))
        b_spec = pl.BlockSpec((tn, tk), ij(lambda i, j, kk: (j, kk)))
        ca, cb = 1, 1
    else:
        (k, m), (k2, n) = a.shape, b.shape
        a_spec = pl.BlockSpec((tk, tm), ij(lambda i, j, kk: (kk, i)))
        b_spec = pl.BlockSpec((tk, tn), ij(lambda i, j, kk: (kk, j)))
        ca, cb = 0, 0
    assert k == k2 and m % tm == 0 and n % tn == 0 and k % tk == 0, (name, a.shape, b.shape, tm, tn, tk)
    nk = k // tk
    grid = (n // tn, m // tm, nk) if b_outer else (m // tm, n // tn, nk)

    def body(a_ref, b_ref, o_ref, *acc):
        part = _dg(a_ref[...], b_ref[...], ca, cb)
        if nk == 1:
            o_ref[...] = part.astype(o_ref.dtype)
            return
        acc_ref, = acc
        kk = pl.program_id(2)

        @pl.when(kk == 0)
        def _():
            acc_ref[...] = part

        @pl.when(kk > 0)
        def _():
            acc_ref[...] += part

        @pl.when(kk == nk - 1)
        def _():
            o_ref[...] = acc_ref[...].astype(o_ref.dtype)

    return pl.pallas_call(
        body, name=name, grid=grid,
        in_specs=[a_spec, b_spec], out_specs=pl.BlockSpec((tm, tn), ij(lambda i, j, kk: (i, j))),
        out_shape=jax.ShapeDtypeStruct((m, n), out_dtype),
        scratch_shapes=[] if nk == 1 else [pltpu.VMEM((tm, tn), F32)],
    )(a, b)


def _stage_specs(rows, tps, consts, ws):
    specs, args = [], []
    for arr, width, cb in rows:
        specs.append(pl.BlockSpec((ROW_TILE, width), functools.partial(lambda i, cb: (i, cb), cb=cb)))
        args.append(arr)
    for arr in tps:
        specs.append(pl.BlockSpec((1, 1, arr.shape[2]), lambda i: (i, 0, 0)))
        args.append(arr)
    for arr, period in consts:
        specs.append(pl.BlockSpec((ROW_TILE, arr.shape[1]), functools.partial(lambda i, p: (i % p, 0), p=period)))
        args.append(arr)
    for arr in ws:
        specs.append(pl.BlockSpec(arr.shape, functools.partial(lambda i, nd: (0,) * nd, nd=arr.ndim)))
        args.append(arr)
    return specs, args


def _stage_load(refs, n_rows, n_tps, n_consts, n_ws):
    it = iter(refs)
    rows = [next(it)[...].astype(F32) for _ in range(n_rows)]
    tps = [next(it)[0].astype(F32) for _ in range(n_tps)]
    consts = [next(it)[...].astype(F32) for _ in range(n_consts)]
    ws = [next(it)[...].astype(F32) for _ in range(n_ws)]
    return rows, tps, consts, ws


def stage_fwd(fn, rows, tps, consts, ws, outs, name):
    n_tiles = rows[0][0].shape[0] // ROW_TILE
    specs, args = _stage_specs(rows, tps, consts, ws)
    counts = (len(rows), len(tps), len(consts), len(ws))

    def body(*refs):
        r, t, c, w = _stage_load(refs[:sum(counts)], *counts)
        res = fn(r, t, c, w)
        for o_ref, o in zip(refs[sum(counts):], res):
            o_ref[...] = o.astype(o_ref.dtype)

    res = pl.pallas_call(
        body, name=name, grid=(n_tiles,), in_specs=specs,
        out_specs=[pl.BlockSpec((ROW_TILE, wd), lambda i: (i, 0)) for wd, _ in outs],
        out_shape=[jax.ShapeDtypeStruct((n_tiles * ROW_TILE, wd), dt) for wd, dt in outs],
    )(*args)
    return list(res)


def stage_bwd(fn, rows, tps, consts, ws, cts, row_grads, name):
    n_tiles = rows[0][0].shape[0] // ROW_TILE
    specs, args = _stage_specs(rows, tps, consts, ws)
    counts = (len(rows), len(tps), len(consts), len(ws))
    n_in = sum(counts)
    for ct in cts:
        specs.append(pl.BlockSpec((ROW_TILE, ct.shape[1]), lambda i: (i, 0)))
        args.append(ct)
    want = [k for k, dt in enumerate(row_grads) if dt is not None]
    out_specs = [pl.BlockSpec((ROW_TILE, rows[k][1]), lambda i: (i, 0)) for k in want]
    out_shape = [jax.ShapeDtypeStruct((n_tiles * ROW_TILE, rows[k][1]), row_grads[k]) for k in want]
    out_specs += [pl.BlockSpec((1, 1, a.shape[2]), lambda i: (i, 0, 0)) for a in tps]
    out_shape += [jax.ShapeDtypeStruct((n_tiles, 1, a.shape[2]), F32) for a in tps]
    out_specs += [pl.BlockSpec(a.shape, functools.partial(lambda i, nd: (0,) * nd, nd=a.ndim)) for a in ws]
    out_shape += [jax.ShapeDtypeStruct(a.shape, F32) for a in ws]

    def body(*refs):
        r, t, c, w = _stage_load(refs[:n_in], *counts)
        g = [ref[...].astype(F32) for ref in refs[n_in:n_in + len(cts)]]
        _, vjp = jax.vjp(lambda r_, t_, w_: fn(r_, t_, c, w_), r, t, w)
        dr, dt, dw = vjp(g)
        o = iter(refs[n_in + len(cts):])
        for k in want:
            ref = next(o)
            ref[...] = dr[k].astype(ref.dtype)
        for v in dt:
            next(o)[0] = v
        first = pl.program_id(0) == 0
        for v in dw:
            ref = next(o)

            @pl.when(first)
            def _():
                ref[...] = v

            @pl.when(jnp.logical_not(first))
            def _():
                ref[...] += v

    res = pl.pallas_call(body, name=name, grid=(n_tiles,), in_specs=specs, out_specs=out_specs, out_shape=out_shape)(*args)
    res = list(res)
    drows = [None] * len(rows)
    for k in want:
        drows[k] = res.pop(0)
    dtps = [res.pop(0) for _ in tps]
    dws = [res.pop(0) for _ in ws]
    return drows, dtps, dws


def fn_modulate(rows, tps, consts, ws):
    (x,), (sc, sh) = rows, tps
    return [x * (1.0 + sc) + sh]


def fn_post(rows, tps, consts, ws):
    (x, a), (g,), (lng, lnb) = rows, tps, ws
    return [layer_norm(ALPHA * x + g * a, lng, lnb)]


def fn_mix(rows, tps, consts, ws):
    ogf, ogb, orf, orb, pg, pr, mo = rows
    gng, = ws
    og = ogf + ogb
    out_g = og * lax.rsqrt(seg_sum(og * og, 64) * (1.0 / 64) + EPS) * gng * silu(pg)
    o = orf + orb
    oc = o - seg_sum(o, 64) * (1.0 / 64)
    out_r = oc * lax.rsqrt(seg_sum(oc * oc, 64) * (1.0 / 64) + EPS) * silu(pr)
    return [jnp.concatenate([out_g, out_r, mo], axis=-1)]


def fn_mla_prep(rows, tps, consts, ws):
    pq, pkv, plk = rows
    gq, gkv, wuq, wuk, wuv = ws
    qtab, ktab = consts[0:3], consts[3:6]
    cq = pq * lax.rsqrt(jnp.mean(pq * pq, axis=-1, keepdims=True) + EPS) * gq
    qp = mm_nn(cq, wuq)
    q = jnp.concatenate([rope(qp[:, h * LANES:(h + 1) * LANES], qtab, 8) * MLA_SCALE for h in range(MLA_HEADS)], axis=-1)
    ckv = pkv * lax.rsqrt(jnp.mean(pkv * pkv, axis=-1, keepdims=True) + EPS) * gkv
    k = mm_nn(ckv, wuk) + place_kr(rope(plk, ktab, 8))
    v = mm_nn(ckv, wuv)
    return [q, k, v]


def fn_assemble(rows, tps, consts, ws):
    gq, gk, gv, gg, rq, rk, rv, rg, cq, ckv, lk1, lk2 = rows
    return [jnp.concatenate([gq, gk, gv, gg, rq, rk, rv, rg, cq, ckv, lk1 + lk2], axis=-1)]


def _head_masks():
    hm = (lax.shift_right_logical(_iota((4, 1, LANES), 2), 5) == _iota((4, 1, LANES), 0)).astype(F32)
    vm = (lax.shift_right_logical(_iota((4, 1, 256), 2), 6) == _iota((4, 1, 256), 0)).astype(F32)
    bd = (lax.shift_right_logical(_iota((256, LANES), 0), 6) == lax.shift_right_logical(_iota((256, LANES), 1), 5)).astype(F32)
    return hm, vm, bd


def chunk_step(s, q, k, v, la, upper):
    hm, vm, bd = _head_masks()
    t, u = _iota((4 * CHUNK, CHUNK), 0), _iota((4 * CHUNK, CHUNK), 1)
    t = jnp.bitwise_and(t, CHUNK - 1)
    causal = (u >= t) if upper else (t >= u)
    if la.shape[0] == 1:
        pos = _iota((CHUNK, LANES), 0)
        b = ((CHUNK - pos) if upper else (pos + 1)).astype(F32) * la
        bend = float(CHUNK) * la
    else:
        b = cumsum_rows(la, upper)
        bend = jnp.sum(la, axis=0, keepdims=True)
    half = 0.5 * bend
    qd = q * jnp.exp(b - half)
    kd = k * jnp.exp(half - b)
    qe = (qd[None] * hm).reshape(4 * CHUNK, LANES)
    att = jnp.where(causal, mm_nt(qe, kd), 0.0)
    o_intra = (mm_nn(att, v).reshape(4, CHUNK, 256) * vm).sum(0)
    o = mm_nt(q * jnp.exp(b), s) + o_intra
    s_new = (s * jnp.exp(bend) + mm_tn(v, k * jnp.exp(bend - b))) * bd
    return o, s_new


def scan_step(sg, sr, q, k, v, lrk, rq, rk, rv, gw, gb, rdec, tab, upper):
    la_g = log_sigmoid(mm_nn(lrk, gw) + gb) * (1.0 / GLA_TAU)
    og, sg2 = chunk_step(sg, q * GLA_DK ** -0.5, k, v, la_g, upper)
    la_r = log_sigmoid(rdec)
    orr, sr2 = chunk_step(sr, rope(rq, tab, 16), rope(rk * RET_DK ** -0.5, tab, 16), rv, la_r, upper)
    return og, orr, sg2, sr2


def _chunk_of(n, ncc, nch, reverse):
    if not reverse:
        return n
    return jnp.where(n < ncc, ncc - 1 - n, nch - 1 + ncc - n)


def _scan_in_specs(p3, tabs, gw, gb, rdec, cidx):
    nb = p3.shape[0]

    def blk(width, cb):
        return pl.BlockSpec((nb, CHUNK, width), lambda m: (0, cidx(m), cb))

    specs = [blk(128, C_GQ // 128), blk(128, C_GK // 128), blk(256, C_GV // 256), blk(128, C_LK // 128),
             blk(128, C_RQ // 128), blk(128, C_RK // 128), blk(256, C_RV // 256)]
    args = [p3] * 7
    for t in tabs:
        specs.append(pl.BlockSpec((CHUNK, LANES), lambda m: (cidx(m), 0)))
        args.append(t)
    for w in (gw, gb, rdec):
        specs.append(pl.BlockSpec(w.shape, lambda m: (0, 0)))
        args.append(w)
    return specs, args


def scan_fwd(p, tabs, gw, gb, rdec, nb, ncc, nch, reverse, name):
    cidx = lambda n: _chunk_of(n, ncc, nch, reverse)
    t = p.shape[0]
    specs, args = _scan_in_specs(p.reshape(nb, t // nb, p.shape[1]), tabs, gw, gb, rdec, cidx)

    def body(q, k, v, lrk, rq, rk, rv, tc, ta, tb, gw_r, gb_r, rd_r, og_r, or_r, sgo_r, sro_r, sg, sr):
        @pl.when(pl.program_id(0) == 0)
        def _():
            sg[...] = jnp.zeros_like(sg)
            sr[...] = jnp.zeros_like(sr)

        sgo_r[0] = sg[...]
        sro_r[0] = sr[...]
        ld = lambda r: r[...].astype(F32)
        tab, gw_, gb_, rd_ = (ld(tc), ld(ta), ld(tb)), ld(gw_r), ld(gb_r), ld(rd_r)
        for b in range(nb):
            lb = lambda r: r[b].astype(F32)
            og, orr, sg2, sr2 = scan_step(sg[b], sr[b], lb(q), lb(k), lb(v), lb(lrk), lb(rq), lb(rk), lb(rv),
                                          gw_, gb_, rd_, tab, reverse)
            og_r[b] = og
            or_r[b] = orr
            sg[b] = sg2
            sr[b] = sr2

    row_out = pl.BlockSpec((nb, CHUNK, 256), lambda n: (0, cidx(n), 0))
    st_out = pl.BlockSpec((1, nb, 256, LANES), lambda n: (n, 0, 0, 0))
    og, orr, sgs, srs = pl.pallas_call(
        body, name=name, grid=(nch,), in_specs=specs, out_specs=[row_out, row_out, st_out, st_out],
        out_shape=[jax.ShapeDtypeStruct((nb, t // nb, 256), F32)] * 2 + [jax.ShapeDtypeStruct((nch, nb, 256, LANES), F32)] * 2,
        scratch_shapes=[pltpu.VMEM((nb, 256, LANES), F32)] * 2,
    )(*args)
    return og.reshape(t, 256), orr.reshape(t, 256), sgs, srs


def scan_bwd(p, tabs, gw, gb, rdec, sg_in, sr_in, dog, dor, prev, nb, ncc, nch, reverse, name):
    step = lambda m: nch - 1 - m
    cidx = lambda m: _chunk_of(step(m), ncc, nch, reverse)
    t = p.shape[0]
    lt = t // nb
    specs, args = _scan_in_specs(p.reshape(nb, lt, p.shape[1]), tabs, gw, gb, rdec, cidx)
    st_spec = pl.BlockSpec((1, nb, 256, LANES), lambda m: (step(m), 0, 0, 0))
    specs += [st_spec, st_spec]
    args += [sg_in, sr_in]
    row = lambda width: pl.BlockSpec((nb, CHUNK, width), lambda m: (0, cidx(m), 0))
    specs += [row(256), row(256)]
    args += [dog.reshape(nb, lt, 256), dor.reshape(nb, lt, 256)]
    widths = (128, 128, 256, 128, 128, 128, 256)
    if prev is not None:
        specs += [row(wd) for wd in widths]
        args += [a.reshape(nb, lt, a.shape[1]) for a in prev]
    n_prev = 0 if prev is None else 7

    def body(*refs):
        (q, k, v, lrk, rq, rk, rv, tc, ta, tb, gw_r, gb_r, rd_r, sgi, sri, dog_r, dor_r), rest = refs[:17], refs[17:]
        prev_r, rest = rest[:n_prev], rest[n_prev:]
        outs, (dgw_r, dgb_r, drd_r, dsg, dsr) = rest[:7], rest[7:]
        first = pl.program_id(0) == 0

        @pl.when(first)
        def _():
            dsg[...] = jnp.zeros_like(dsg)
            dsr[...] = jnp.zeros_like(dsr)

        ld = lambda r: r[...].astype(F32)
        tab, gw_, gb_, rd_ = (ld(tc), ld(ta), ld(tb)), ld(gw_r), ld(gb_r), ld(rd_r)
        wsum = None
        for b in range(nb):
            lb = lambda r: r[b].astype(F32)
            prim = (sgi[0, b], sri[0, b], lb(q), lb(k), lb(v), lb(lrk), lb(rq), lb(rk), lb(rv), gw_, gb_, rd_)
            _, vjp = jax.vjp(lambda *a: scan_step(*a, tab, reverse), *prim)
            g = vjp((lb(dog_r), lb(dor_r), dsg[b], dsr[b]))
            dsg[b] = g[0]
            dsr[b] = g[1]
            for j in range(7):
                val = g[2 + j]
                if n_prev:
                    val = val + prev_r[j][b]
                outs[j][b] = val
            wsum = g[9:12] if wsum is None else tuple(a + c for a, c in zip(wsum, g[9:12]))
        for ref, val in zip((dgw_r, dgb_r, drd_r), wsum):
            @pl.when(first)
            def _():
                ref[...] = val

            @pl.when(jnp.logical_not(first))
            def _():
                ref[...] += val

    wspec = lambda w: pl.BlockSpec(w.shape, lambda m: (0, 0))
    res = pl.pallas_call(
        body, name=name, grid=(nch,), in_specs=specs,
        out_specs=[row(wd) for wd in widths] + [wspec(gw), wspec(gb), wspec(rdec)],
        out_shape=[jax.ShapeDtypeStruct((nb, lt, wd), F32) for wd in widths]
        + [jax.ShapeDtypeStruct(w.shape, F32) for w in (gw, gb, rdec)],
        scratch_shapes=[pltpu.VMEM((nb, 256, LANES), F32)] * 2,
    )(*args)
    return tuple(a.reshape(t, a.shape[2]) for a in res[:7]), res[7], res[8], res[9]


def _attn_tiles(lc, lt):
    nct = lc // ROW_TILE
    return nct, (lt - lc) // ROW_TILE


def _attn_loop(tile, lc, lt, lat_rows):
    for i in range(lc // ROW_TILE):
        tile(i * ROW_TILE, lc, ROW_TILE)

    def lat(i, carry):
        tile(pl.multiple_of(lc + i * lat_rows, ROW_TILE), lt, lat_rows)
        return carry

    lax.fori_loop(0, (lt - lc) // lat_rows, lat, 0)


def mla_fwd(q, k, v, nb, lc, lt, name, gather=()):
    ng = len(gather)

    def body(q_ref, k_ref, v_ref, *rest):
        x_refs, (o_ref, lse_ref), out_refs, sems = rest[:ng], rest[ng:ng + 2], rest[ng + 2:2 * ng + 2], rest[2 * ng + 2:]
        if ng:
            start, finish = _gather8_steps(x_refs, out_refs, *sems)
            pl.when((pl.program_id(0) == 0) & (pl.program_id(1) == 0))(start)

        def tile(r0, nk, nrows):
            rows = pl.ds(r0, nrows)
            lane = _iota((nrows, LANES), 1)
            outs, lse = [], jnp.zeros((nrows, LANES), F32)
            for j in range(2):
                s = _dg(q_ref[rows, j * LANES:(j + 1) * LANES], k_ref[0:nk, j * LANES:(j + 1) * LANES], 1, 1)
                m = jnp.max(s, axis=-1, keepdims=True)
                p = jnp.exp(s - m)
                l = jnp.sum(p, axis=-1, keepdims=True)
                outs.append(_dg(p, v_ref[0:nk, j * 64:(j + 1) * 64], 1, 0) * (1.0 / l))
                lse = jnp.where(lane == j, m + jnp.log(l), lse)
            o_ref[rows, :] = jnp.concatenate(outs, axis=-1)
            lse_ref[rows, :] = lse

        _attn_loop(tile, lc, lt, ATTN_ROWS_FWD)
        if ng:
            pl.when((pl.program_id(0) == nb - 1) & (pl.program_id(1) == MLA_HEADS // 2 - 1))(finish)

    pair = lambda width: pl.BlockSpec((lt, width), lambda b, h: (b, h))
    res = pl.pallas_call(
        body, name=name, grid=(nb, MLA_HEADS // 2), in_specs=[pair(2 * LANES), pair(2 * LANES), pair(LANES)] + [ANY] * ng,
        out_specs=[pair(LANES), pair(LANES)] + [ANY] * ng,
        out_shape=[jax.ShapeDtypeStruct((nb * lt, MLA_HEADS * 64), F32), jax.ShapeDtypeStruct((nb * lt, MLA_HEADS // 2 * LANES), F32)]
        + [jax.ShapeDtypeStruct((8,) + g.shape, g.dtype) for g in gather],
        scratch_shapes=[_dma_sems(ng, 7), _dma_sems(ng, 7), _dma_sems(ng, 1)] if ng else [],
    )(q, k, v, *gather)
    return res[0], res[1], list(res[2:])


def mla_bwd(q, k, v, o, lse, do, nb, lc, lt, name, exchange=()):
    ne = len(exchange)

    def body(q_ref, k_ref, v_ref, o_ref, lse_ref, do_ref, *rest):
        p_refs, (dq_ref, dk_ref, dv_ref), rest = rest[:ne], rest[ne:ne + 3], rest[ne + 3:]
        out_refs, (dka, dva), sems = rest[:ne], rest[ne:ne + 2], rest[ne + 2:]
        if ne:
            start, finish = _exchange_steps(p_refs, out_refs, *sems)
            pl.when((pl.program_id(0) == 0) & (pl.program_id(1) == 0))(start)
        dka[...] = jnp.zeros_like(dka)
        dva[...] = jnp.zeros_like(dva)

        def tile(r0, nk, nrows):
            rows = pl.ds(r0, nrows)
            dqs = []
            for j in range(2):
                qj, kj = q_ref[rows, j * LANES:(j + 1) * LANES], k_ref[0:nk, j * LANES:(j + 1) * LANES]
                vj, doj = v_ref[0:nk, j * 64:(j + 1) * 64], do_ref[rows, j * 64:(j + 1) * 64]
                p = jnp.exp(_dg(qj, kj, 1, 1) - lse_ref[rows, j:j + 1])
                dsum = jnp.sum(doj * o_ref[rows, j * 64:(j + 1) * 64], axis=-1, keepdims=True)
                ds = p * (_dg(doj, vj, 1, 1) - dsum)
                dqs.append(_dg(ds, kj, 1, 0))
                dka[j, 0:nk, :] += _dg(ds, qj, 0, 0)
                dva[j, 0:nk, :] += _dg(p, doj, 0, 0)
            dq_ref[rows, :] = jnp.concatenate(dqs, axis=-1)

        _attn_loop(tile, lc, lt, ATTN_ROWS_BWD)
        dk_ref[...] = jnp.concatenate([dka[0], dka[1]], axis=-1)
        dv_ref[...] = jnp.concatenate([dva[0], dva[1]], axis=-1)
        if ne:
            pl.when((pl.program_id(0) == nb - 1) & (pl.program_id(1) == MLA_HEADS // 2 - 1))(finish)

    t = nb * lt
    pair = lambda width: pl.BlockSpec((lt, width), lambda b, h: (b, h))
    res = pl.pallas_call(
        body, name=name, grid=(nb, MLA_HEADS // 2),
        in_specs=[pair(2 * LANES), pair(2 * LANES), pair(LANES), pair(LANES), pair(LANES), pair(LANES)] + [ANY] * ne,
        out_specs=[pair(2 * LANES), pair(2 * LANES), pair(LANES)] + [ANY] * ne,
        out_shape=[jax.ShapeDtypeStruct((t, MLA_HEADS * LANES), F32), jax.ShapeDtypeStruct((t, MLA_HEADS * LANES), F32),
                   jax.ShapeDtypeStruct((t, MLA_HEADS * 64), F32)] + [jax.ShapeDtypeStruct(e.shape, e.dtype) for e in exchange],
        scratch_shapes=[pltpu.VMEM((2, lt, LANES), F32), pltpu.VMEM((2, lt, 64), F32)]
        + ([_dma_sems(ne, 3), _dma_sems(ne, 3), _dma_sems(ne, 1)] if ne else []),
    )(q, k, v, o, lse, do, *exchange)
    return res[0], res[1], res[2], list(res[3:])


HALO = 16


def _gate_specs(u, per_batch, lc):
    gh = GATE_ROWS // HALO
    nh = u.shape[0] // HALO
    width = u.shape[1]
    main = pl.BlockSpec((GATE_ROWS, width), lambda i: (i, 0))
    prev = pl.BlockSpec((HALO, width), lambda i: (jnp.maximum(i * gh - 1, 0), 0))
    nxt = pl.BlockSpec((HALO, width), lambda i: (jnp.minimum((i + 1) * gh, nh - 1), 0))
    return main, prev, nxt


def _seg_edges(per_batch, lc):
    j = pl.program_id(0) % (per_batch // GATE_ROWS)
    first = (j == 0) | (j == lc // GATE_ROWS)
    last = (j == lc // GATE_ROWS - 1) | (j == per_batch // GATE_ROWS - 1)
    return first, last


def _shifted(x, prev_ref, next_ref, first, last):
    rows = _iota(x.shape, 0)
    before = jnp.where(first, 0.0, prev_ref[HALO - 1:HALO, :].astype(F32))
    after = jnp.where(last, 0.0, next_ref[0:1, :].astype(F32))
    xm = jnp.where(rows == 0, before, pltpu.roll(x, 1, 0))
    xp = jnp.where(rows == x.shape[0] - 1, after, pltpu.roll(x, x.shape[0] - 1, 0))
    return xm, xp


def _whole(a):
    return pl.BlockSpec(a.shape, lambda i: (0,) * a.ndim)


def gate_fwd(u, cw, cb, wdown, per_batch, lc, name):
    main, prev, nxt = _gate_specs(u, per_batch, lc)
    f = u.shape[1] // 2

    def body(u_ref, p_ref, n_ref, w_ref, b_ref, wd_ref, act_ref, f_ref):
        first, last = _seg_edges(per_batch, lc)
        x = u_ref[...].astype(F32)
        xm, xp = _shifted(x, p_ref, n_ref, first, last)
        c = w_ref[0:1, :] * xm + w_ref[1:2, :] * x + w_ref[2:3, :] * xp + b_ref[...]
        act = silu(c[:, :f]) * c[:, f:]
        act_ref[...] = act.astype(act_ref.dtype)
        f_ref[...] = _dg(act, wd_ref[...], 1, 0)

    return pl.pallas_call(
        body, name=name, grid=(u.shape[0] // GATE_ROWS,),
        in_specs=[main, prev, nxt, _whole(cw), _whole(cb), _whole(wdown)],
        out_specs=[pl.BlockSpec((GATE_ROWS, f), lambda i: (i, 0)), pl.BlockSpec((GATE_ROWS, wdown.shape[1]), lambda i: (i, 0))],
        out_shape=[jax.ShapeDtypeStruct((u.shape[0], f), BF), jax.ShapeDtypeStruct((u.shape[0], wdown.shape[1]), F32)],
    )(u, u, u, cw, cb, wdown)


def gate_bwd(u, cw, cb, dact, per_batch, lc, name):
    main, prev, nxt = _gate_specs(u, per_batch, lc)
    f = u.shape[1] // 2

    def body(u_ref, p_ref, n_ref, w_ref, b_ref, da_ref, dc_ref, dw_ref):
        first, last = _seg_edges(per_batch, lc)
        x = u_ref[...].astype(F32)
        xm, xp = _shifted(x, p_ref, n_ref, first, last)
        c = w_ref[0:1, :] * xm + w_ref[1:2, :] * x + w_ref[2:3, :] * xp + b_ref[...]
        a, g = c[:, :f], c[:, f:]
        sg = jax.nn.sigmoid(a)
        da = da_ref[...]
        dc = jnp.concatenate([da * g * (sg * (1.0 + a * (1.0 - sg))), da * (a * sg)], axis=-1)
        dc_ref[...] = dc.astype(dc_ref.dtype)
        part = jnp.concatenate([jnp.sum(xm * dc, axis=0, keepdims=True), jnp.sum(x * dc, axis=0, keepdims=True),
                                jnp.sum(xp * dc, axis=0, keepdims=True), jnp.sum(dc, axis=0, keepdims=True),
                                jnp.zeros((4, 2 * f), F32)], axis=0)

        @pl.when(pl.program_id(0) == 0)
        def _():
            dw_ref[...] = part

        @pl.when(pl.program_id(0) > 0)
        def _():
            dw_ref[...] += part

    return pl.pallas_call(
        body, name=name, grid=(u.shape[0] // GATE_ROWS,),
        in_specs=[main, prev, nxt, _whole(cw), _whole(cb), pl.BlockSpec((GATE_ROWS, f), lambda i: (i, 0))],
        out_specs=[main, pl.BlockSpec((8, 2 * f), lambda i: (0, 0))],
        out_shape=[jax.ShapeDtypeStruct(u.shape, BF), jax.ShapeDtypeStruct((8, 2 * f), F32)],
    )(u, u, u, cw, cb, dact)


def conv_transpose(dc, cw, per_batch, lc, name):
    main, prev, nxt = _gate_specs(dc, per_batch, lc)

    def body(d_ref, p_ref, n_ref, w_ref, du_ref):
        first, last = _seg_edges(per_batch, lc)
        x = d_ref[...].astype(F32)
        xm, xp = _shifted(x, p_ref, n_ref, first, last)
        du_ref[...] = (w_ref[0:1, :] * xp + w_ref[1:2, :] * x + w_ref[2:3, :] * xm).astype(du_ref.dtype)

    return pl.pallas_call(
        body, name=name, grid=(dc.shape[0] // GATE_ROWS,),
        in_specs=[main, prev, nxt, _whole(cw)],
        out_specs=main, out_shape=jax.ShapeDtypeStruct(dc.shape, BF),
    )(dc, dc, dc, cw)


def loss_head(x, target, tiles_per_batch, ctx_tiles, name):
    n_tiles = x.shape[0] // ROW_TILE
    lat_tiles = tiles_per_batch - ctx_tiles

    def tgt_idx(i):
        j = i % tiles_per_batch
        return jnp.where(j < ctx_tiles, 0, (i // tiles_per_batch) * lat_tiles + j - ctx_tiles), 0

    def body(x_ref, t_ref, dx_ref, l_ref):
        lat = (pl.program_id(0) % tiles_per_batch >= ctx_tiles).astype(F32)
        err = (x_ref[...] - t_ref[...]) * lat
        dx_ref[...] = err * (1.0 / D_MODEL)
        l_ref[...] = jnp.full(l_ref.shape, 0.5 / D_MODEL * jnp.sum(err * err), F32)

    return pl.pallas_call(
        body, name=name, grid=(n_tiles,),
        in_specs=[pl.BlockSpec((ROW_TILE, D_MODEL), lambda i: (i, 0)), pl.BlockSpec((ROW_TILE, D_MODEL), tgt_idx)],
        out_specs=[pl.BlockSpec((ROW_TILE, D_MODEL), lambda i: (i, 0)), pl.BlockSpec((1, 8, LANES), lambda i: (i, 0, 0))],
        out_shape=[jax.ShapeDtypeStruct(x.shape, F32), jax.ShapeDtypeStruct((n_tiles, 8, LANES), F32)],
    )(x, target)


def adamw(w, g, m, v, name):
    rows, cols = w.shape
    tr = rows
    for cand in (512, 256, 128, 64, 32, 16, 8):
        if rows % cand == 0 and cand * cols * 4 <= (1 << 20):
            tr = cand
            break

    def body(w_ref, g_ref, m_ref, v_ref, d_ref, mo_ref, vo_ref):
        gg = g_ref[...]
        m2 = ADAM_B1 * m_ref[...] + (1.0 - ADAM_B1) * gg
        v2 = ADAM_B2 * v_ref[...] + (1.0 - ADAM_B2) * (gg * gg)
        m_hat = m2 / (1.0 - ADAM_B1 ** ADAM_STEP)
        v_hat = v2 / (1.0 - ADAM_B2 ** ADAM_STEP)
        d_ref[...] = -ADAM_LR * (m_hat / (jnp.sqrt(v_hat) + ADAM_EPS) + ADAM_WD * w_ref[...])
        mo_ref[...] = m2
        vo_ref[...] = v2

    spec = pl.BlockSpec((tr, cols), lambda i: (i, 0))
    return pl.pallas_call(body, name=name, grid=(rows // tr,), in_specs=[spec] * 4, out_specs=[spec] * 3,
                          out_shape=[jax.ShapeDtypeStruct(w.shape, F32)] * 3)(w, g, m, v)


def fn_post_mod(rows, tps, consts, ws):
    (x, a), (g, sc, sh), (lng, lnb) = rows, tps, ws
    y = layer_norm(ALPHA * x + g * a, lng, lnb)
    return [y, y * (1.0 + sc) + sh]


def _pick(n, cands):
    for c in cands:
        if n % c == 0:
            return c
    return n


def rope_tables(lc, l):
    pos = jnp.arange(l, dtype=F32)
    ret_inv = 1.0 / (ROPE_BASE ** jnp.linspace(0.0, 1.0, RET_DK // 2, dtype=F32))
    ang = pos[:, None] * ret_inv
    rc, rs = jnp.cos(ang), jnp.sin(ang)
    n_ax = MLA_D_ROPE // 4
    ax_inv = ROPE_BASE ** (-jnp.arange(n_ax, dtype=F32) / n_ax)
    rows_n = l // GRID_W
    rows = jnp.repeat(jnp.arange(rows_n, dtype=F32), GRID_W)
    cols = jnp.tile(jnp.arange(GRID_W, dtype=F32), rows_n)
    ra, ca = rows[:, None] * ax_inv, cols[:, None] * ax_inv
    rwc, rws, clc, cls = jnp.cos(ra), jnp.sin(ra), jnp.cos(ca), jnp.sin(ca)
    one = lambda n: jnp.ones((l, n), F32)
    zero = lambda n: jnp.zeros((l, n), F32)
    cat = lambda parts: jnp.concatenate(parts, axis=1)

    def with_ctx(tab, is_cos):
        head = jnp.ones((lc, LANES), F32) if is_cos else jnp.zeros((lc, LANES), F32)
        return jnp.concatenate([head, tab], axis=0)

    ret = (cat([rc, rc] * 4), cat([-rs, zero(16)] * 4), cat([zero(16), rs] * 4))
    ax_c = [rwc, rwc, clc, clc]
    ax_a = [-rws, zero(8), -cls, zero(8)]
    ax_b = [zero(8), rws, zero(8), cls]
    qt = (cat([one(64)] + ax_c + [one(32)]), cat([zero(64)] + ax_a + [zero(32)]), cat([zero(64)] + ax_b + [zero(32)]))
    kt = (cat([one(32)] + ax_c + [one(64)]), cat([zero(32)] + ax_a + [zero(64)]), cat([zero(32)] + ax_b + [zero(64)]))
    fix = lambda t3: tuple(with_ctx(t, k == 0) for k, t in enumerate(t3))
    return fix(ret), fix(qt), fix(kt)


_IN_ORDER = ((0, 128), (128, 256), (256, 512), (544, 800), (800, 928), (928, 1056), (1056, 1312), (1312, 1568),
             (1568, 1824), (1824, 1952), (512, 544), (1952, 1984))


def permute_w_in(w):
    parts = [w[:, a:b] for a, b in _IN_ORDER] + [jnp.zeros((w.shape[0], D_INP - D_IN), w.dtype)]
    return jnp.concatenate(parts, axis=1)


def unpermute_w_in(g):
    out, at = {}, 0
    for a, b in _IN_ORDER:
        out[a] = g[:, at:at + b - a]
        at += b - a
    return jnp.concatenate([out[a] for a in sorted(out)], axis=1)


def layer_weights(big, w, l):
    f = lambda a: a.astype(F32)
    r = {}
    r["ada_w"] = big["ada_w"].astype(BF)
    r["win"] = permute_w_in(big["w_in"]).astype(BF)
    r["wout"] = big["w_out"].astype(BF)
    r["wup"] = big["ffn_up"].astype(BF)
    r["wdown"] = big["ffn_down"].astype(BF)
    uq = big["mla_w_uq"].reshape(256, MLA_HEADS, 96)
    r["wuq"] = jnp.pad(uq, ((0, 0), (0, 0), (0, 32))).reshape(256, 8 * LANES).astype(BF)
    uk = big["mla_w_uk"].reshape(128, MLA_HEADS, 64)
    r["wuk"] = jnp.pad(uk, ((0, 0), (0, 0), (0, 64))).reshape(128, 8 * LANES).astype(BF)
    r["wuv"] = big["mla_w_uv"].astype(BF)
    gw = f(w["gla_gate_w"][l])
    z16 = jnp.zeros((16, LANES), F32)
    z96 = jnp.zeros((96, LANES), F32)
    r["gw"] = (jnp.concatenate([gw[0], z16, z96], axis=0), jnp.concatenate([z16, gw[1], z96], axis=0))
    r["gb"] = tuple(f(w["gla_gate_b"][l][d]).reshape(1, LANES) for d in range(2))
    r["rdec"] = tuple(jnp.repeat(f(w["ret_decay"][l][d]), 32).reshape(1, LANES) for d in range(2))
    r["gng"] = jnp.tile(f(w["gla_norm_g"][l]), 4).reshape(1, 256)
    r["gq"] = f(w["mla_q_norm_g"][l]).reshape(1, 256)
    r["gkv"] = f(w["mla_kv_norm_g"][l]).reshape(1, 128)
    for n in ("ln1_g", "ln1_b", "ln2_g", "ln2_b"):
        r[n] = f(w[n][l]).reshape(1, D_MODEL)
    r["cw"] = f(w["ffn_conv_w"][l])
    r["cb"] = f(w["ffn_conv_b"][l]).reshape(1, 2 * D_FF)
    return r


def tile_params(mod_l, nb, nct, nlt):
    m6 = mod_l.reshape(8, 6, D_MODEL)
    out = []
    for j in range(6):
        parts = []
        for b in range(nb):
            parts.append(jnp.broadcast_to(m6[4, j], (nct, 1, D_MODEL)))
            parts.append(jnp.broadcast_to(m6[b, j], (nlt, 1, D_MODEL)))
        out.append(jnp.concatenate(parts, axis=0))
    return out


def tile_param_grads(dts, nb, nct, nlt):
    cols = []
    for dt in dts:
        d = dt.reshape(nb, nct + nlt, D_MODEL)
        lat = jnp.sum(d[:, nct:], axis=1)
        ctx = jnp.sum(d[:, :nct], axis=(0, 1))
        cols.append(jnp.concatenate([lat, jnp.zeros((4 - nb, D_MODEL), F32), ctx[None], jnp.zeros((3, D_MODEL), F32)], axis=0))
    return jnp.stack(cols, axis=1).reshape(8, 6 * D_MODEL)


def layer_forward(x, h1, tp, lw, tabs, dims, nxt):
    nb, lc, lt = dims
    t = x.shape[0]
    nbt = lt // ROW_TILE
    ncc, nch = lc // CHUNK, lt // CHUNK
    tm = _pick(t, (1024, 768, 512, 256))
    ret_tab, q_tab, k_tab = tabs
    full = lambda a: (a, a.shape[1], 0)
    p = matmul(h1, lw["win"], "nn", F32, tm, D_INP, 1024, "proj_in")
    ogf, orf, sgf, srf = scan_fwd(p, ret_tab, lw["gw"][0], lw["gb"][0], lw["rdec"][0], nb, ncc, nch, False, "scan_fwd_f")
    ogb, orb, sgb, srb = scan_fwd(p, ret_tab, lw["gw"][1], lw["gb"][1], lw["rdec"][1], nb, ncc, nch, True, "scan_fwd_b")
    prep_rows = [(p, 256, C_CQ // 256), (p, 128, C_CKV // 128), (p, 128, C_LK // 128)]
    prep_consts = [(a, nbt) for a in q_tab + k_tab]
    prep_ws = [lw["gq"], lw["gkv"], lw["wuq"], lw["wuk"], lw["wuv"]]
    q, k, v = stage_fwd(fn_mla_prep, prep_rows, [], prep_consts, prep_ws, [(1024, BF), (1024, BF), (512, BF)], "mla_prep")
    if nxt is None:
        mo, lse, _ = mla_fwd(q, k, v, nb, lc, lt, "mla_attn_last")
        made, tp_next = None, None
    else:
        mo, lse, got = mla_fwd(q, k, v, nb, lc, lt, "mla_attn", gather=nxt[0])
        made = nxt[1](got)
        tp_next = (made[0][1], made[0][0])
    mix_rows = [full(ogf), full(ogb), full(orf), full(orb), (p, 256, C_GG // 256), (p, 256, C_RG // 256), full(mo)]
    m, = stage_fwd(fn_mix, mix_rows, [], [], [lw["gng"]], [(1024, BF)], "mix")
    a = matmul(m, lw["wout"], "nn", F32, tm, 1024, 1024, "proj_out")
    x1, h2 = stage_fwd(fn_post_mod, [full(x), full(a)], [tp[2], tp[4], tp[3]], [], [lw["ln1_g"], lw["ln1_b"]],
                       [(1024, F32), (1024, BF)], "post1")
    u = matmul(h2, lw["wup"], "nn", BF, tm, 1408, 1024, "ffn_up", b_outer=True)
    act, f = gate_fwd(u, lw["cw"], lw["cb"], lw["wdown"], lt, lc, "ffn_gate_down")
    if tp_next is None:
        x2, = stage_fwd(fn_post, [full(x1), full(f)], [tp[5]], [], [lw["ln2_g"], lw["ln2_b"]], [(1024, F32)], "post2_last")
        h1n = None
    else:
        x2, h1n = stage_fwd(fn_post_mod, [full(x1), full(f)], [tp[5], tp_next[0], tp_next[1]], [],
                            [lw["ln2_g"], lw["ln2_b"]], [(1024, F32), (1024, BF)], "post2")
    res = dict(x=x, h1=h1, p=p, ogf=ogf, orf=orf, sgf=sgf, srf=srf, ogb=ogb, orb=orb, sgb=sgb, srb=srb, q=q, k=k, v=v,
               mo=mo, lse=lse, m=m, a=a, x1=x1, h2=h2, u=u, act=act, f=f, mix_rows=mix_rows, prep_rows=prep_rows,
               prep_consts=prep_consts, prep_ws=prep_ws)
    return x2, h1n, res, made


def layer_backward(dx2, dh1n, res, tp, tp_next, lw, tabs, dims, exchange=()):
    nb, lc, lt = dims
    r = res
    t = dx2.shape[0]
    ncc, nch = lc // CHUNK, lt // CHUNK
    tm = _pick(t, (1024, 768, 512, 256))
    tkr = _pick(t, (2304, 1536, 1024, 768, 512))
    ret_tab = tabs[0]
    full = lambda a: (a, a.shape[1], 0)
    g = {}
    if tp_next is None:
        (dx1a, df), (dg2,), (g["ln2_g"], g["ln2_b"]) = stage_bwd(
            fn_post, [full(r["x1"]), full(r["f"])], [tp[5]], [], [lw["ln2_g"], lw["ln2_b"]], [dx2], [F32, BF], "post2_last_bwd")
        dnext = None
    else:
        (dx1a, df), (dg2, dsc1n, dsh1n), (g["ln2_g"], g["ln2_b"]) = stage_bwd(
            fn_post_mod, [full(r["x1"]), full(r["f"])], [tp[5], tp_next[0], tp_next[1]], [], [lw["ln2_g"], lw["ln2_b"]],
            [dx2, dh1n], [F32, BF], "post2_bwd")
        dnext = (dsc1n, dsh1n)
    g["ffn_down"] = matmul(r["act"], df, "tn", F32, 1408, 1024, tkr, "ffn_down_dw")
    dact = matmul(df, lw["wdown"], "nt", F32, tm, 1408, 1024, "ffn_down_dx", b_outer=True)
    dc, dcw = gate_bwd(r["u"], lw["cw"], lw["cb"], dact, lt, lc, "ffn_gate_bwd")
    g["ffn_conv_w"], g["ffn_conv_b"] = dcw[0:3], dcw[3]
    du = conv_transpose(dc, lw["cw"], lt, lc, "ffn_conv_t")
    dh2 = matmul(du, lw["wup"], "nt", F32, _pick(t, (512, 256)), 1024, 2 * D_FF, "ffn_up_dx")
    g["ffn_up"] = matmul(r["h2"], du, "tn", F32, 1024, 1408, tkr, "ffn_up_dw")
    (dxa, da), (dg1, dsc2, dsh2), (g["ln1_g"], g["ln1_b"]) = stage_bwd(
        fn_post_mod, [full(r["x"]), full(r["a"])], [tp[2], tp[4], tp[3]], [], [lw["ln1_g"], lw["ln1_b"]],
        [dx1a, dh2], [F32, BF], "post1_bwd")
    dm = matmul(da, lw["wout"], "nt", F32, tm, 1024, 1024, "proj_out_dx")
    g["w_out"] = matmul(r["m"], da, "tn", F32, 1024, 1024, tkr, "proj_out_dw")
    (dog, _, dor, _, dpg, dpr, dmo), _, (dgng,) = stage_bwd(
        fn_mix, r["mix_rows"], [], [], [lw["gng"]], [dm], [F32, None, F32, None, F32, F32, F32], "mix_bwd")
    g["gla_norm_g"] = jnp.sum(dgng.reshape(4, 64), axis=0)
    dq, dk, dv, landed = mla_bwd(r["q"], r["k"], r["v"], r["mo"], r["lse"], dmo, nb, lc, lt,
                                 "mla_attn_bwd" if exchange else "mla_attn_bwd_first", exchange=exchange)
    (dpq, dpkv, dplk), _, (dgq, dgkv, dwuq, dwuk, dwuv) = stage_bwd(
        fn_mla_prep, r["prep_rows"], [], r["prep_consts"], r["prep_ws"], [dq, dk, dv], [F32, F32, F32], "mla_prep_bwd")
    g["mla_q_norm_g"], g["mla_kv_norm_g"] = dgq.reshape(256), dgkv.reshape(128)
    g["mla_w_uq"] = dwuq.reshape(256, MLA_HEADS, LANES)[:, :, :96].reshape(256, MLA_HEADS * 96)
    g["mla_w_uk"] = dwuk.reshape(128, MLA_HEADS, LANES)[:, :, :64].reshape(128, MLA_HEADS * 64)
    g["mla_w_uv"] = dwuv
    s7, dgw0, dgb0, drd0 = scan_bwd(r["p"], ret_tab, lw["gw"][0], lw["gb"][0], lw["rdec"][0], r["sgf"], r["srf"], dog, dor,
                                    None, nb, ncc, nch, False, "scan_bwd_f")
    s7, dgw1, dgb1, drd1 = scan_bwd(r["p"], ret_tab, lw["gw"][1], lw["gb"][1], lw["rdec"][1], r["sgb"], r["srb"], dog, dor,
                                    s7, nb, ncc, nch, True, "scan_bwd_b")
    g["gla_gate_w"] = jnp.stack([dgw0[0:16], dgw1[16:32]])
    g["gla_gate_b"] = jnp.stack([dgb0[0], dgb1[0]])
    g["ret_decay"] = jnp.stack([jnp.sum(drd0.reshape(4, 32), axis=1), jnp.sum(drd1.reshape(4, 32), axis=1)])
    gq_, gk_, gv_, glrk, rq_, rk_, rv_ = s7
    pieces = [gq_, gk_, gv_, dpg, rq_, rk_, rv_, dpr, dpq, dpkv, glrk, dplk]
    dp, = stage_fwd(fn_assemble, [full(a) for a in pieces], [], [], [], [(D_INP, BF)], "dproj_assemble")
    dh1 = matmul(dp, lw["win"], "nt", F32, tm, 1024, D_INP, "proj_in_dx")
    g["w_in"] = unpermute_w_in(matmul(r["h1"], dp, "tn", F32, 1024, 1024, tkr, "proj_in_dw"))
    for n in ("ln1_g", "ln1_b", "ln2_g", "ln2_b"):
        g[n] = g[n].reshape(D_MODEL)
    dtp = [None, None, dg1, dsh2, dsc2, dg2]
    return dxa, dh1, dtp, dnext, g, landed


def local_step(x, c, ctx, c_ctx, w, loss_target, first_big, next_blocks, assemble, reduce_hooks=None):
    nb, l, _ = x.shape
    lc = ctx.shape[1]
    lt = lc + l
    dims = (nb, lc, lt)
    nct, nlt = lc // ROW_TILE, l // ROW_TILE
    tabs = rope_tables(lc, l)
    x0 = jnp.concatenate([ctx, x], axis=1).reshape(nb * lt, D_MODEL)
    s8 = jnp.concatenate([silu(c), jnp.zeros((4 - nb, D_MODEL), F32), silu(c_ctx)[None], jnp.zeros((3, D_MODEL), F32)], axis=0)

    def make_layer(i, big):
        lw = layer_weights(big, w, i)
        mod = matmul(s8, lw["ada_w"], "nn", F32, 8, 1536, 1024, "ada_mod") + w["ada_b"][i].astype(F32)[None]
        return tile_params(mod, nb, nct, nlt), lw

    made = make_layer(0, first_big)
    lws, tps = [], []
    h1, = stage_fwd(fn_modulate, [(x0, D_MODEL, 0)], [made[0][1], made[0][0]], [], [], [(D_MODEL, BF)], "mod_in")
    xs, ress = x0, []
    for i in range(DEPTH):
        tps.append(made[0])
        lws.append(made[1])
        nxt = None
        if i < DEPTH - 1:
            nxt = (next_blocks(i + 1), functools.partial(lambda got, j: make_layer(j, assemble(j, got)), j=i + 1))
        xs, h1, res, made = layer_forward(xs, h1, tps[i], lws[i], tabs, dims, nxt)
        ress.append(res)
    dx, lparts = loss_head(xs, loss_target.reshape(nb * l, D_MODEL), nct + nlt, nct, "loss_head")
    loss = jnp.sum(lparts[:, 0, 0])
    grads = [None] * DEPTH
    dtps = [None] * DEPTH
    reduced = [None] * DEPTH
    dh1, parts = None, ()
    for i in reversed(range(DEPTH)):
        tpn = None if i == DEPTH - 1 else (tps[i + 1][1], tps[i + 1][0])
        dx, dh1, dtp, dn, grads[i], landed = layer_backward(dx, dh1, ress[i], tps[i], tpn, lws[i], tabs, dims, parts)
        if dn is not None:
            dtps[i + 1][1], dtps[i + 1][0] = dn
        dtps[i] = dtp
        if reduce_hooks is not None:
            if parts:
                reduced[i + 1] = reduce_hooks[1](landed)
            parts = reduce_hooks[0]({n: grads[i].pop(n) for n in REDUCED})
    if reduce_hooks is not None:
        reduced[0] = reduce_hooks[1](exchange_chips(parts, "grad_exchange_last"))
    (dx0b,), (dsc1, dsh1), _ = stage_bwd(fn_modulate, [(x0, D_MODEL, 0)], [tps[0][1], tps[0][0]], [], [], [dh1], [F32], "mod_in_bwd")
    dtps[0][1], dtps[0][0] = dsc1, dsh1
    grad_x = (dx + dx0b).reshape(nb, lt, D_MODEL)[:, lc:]
    dmod = jnp.stack([tile_param_grads(d, nb, nct, nlt) for d in dtps])
    gw = {n: jnp.stack([grads[i][n] for i in range(DEPTH)]) for n in grads[0]}
    return loss, grad_x, gw, dmod, s8, lws, reduced


MESH_IDS = pl.DeviceIdType.MESH
ANY = pl.BlockSpec(memory_space=pl.ANY)


def _place():
    return lax.axis_index("x"), lax.axis_index("y"), lax.axis_index("c")


def _dma_sems(n, per):
    return pltpu.SemaphoreType.DMA((n, per))


def _gather8_steps(x_refs, out_refs, send_sems, recv_sems, local_sems):
    n = len(x_refs)
    x, y, c = _place()
    me, sibling = (x, y, c), (x, y, 1 - c)
    chips = [(1 - x, y), (x, 1 - y), (1 - x, 1 - y)]

    def copy(a, k, blk, to, own=False):
        slot = out_refs[a].at[4 * blk[0] + 2 * blk[1] + blk[2]]
        return pltpu.make_async_remote_copy(
            src_ref=x_refs[a] if own else slot, dst_ref=slot,
            send_sem=send_sems.at[a, k], recv_sem=recv_sems.at[a, k], device_id=to, device_id_type=MESH_IDS)

    def local(a):
        return pltpu.make_async_copy(x_refs[a], out_refs[a].at[4 * x + 2 * y + c], local_sems.at[a, 0])

    def first_copies():
        cps = []
        for a in range(n):
            cps.append(copy(a, 0, me, sibling, own=True))
            cps += [copy(a, 1 + j, me, (*chip, c), own=True) for j, chip in enumerate(chips)]
        return cps

    def start():
        for a in range(n):
            local(a).start()
        for cp in first_copies():
            cp.start()

    def finish():
        passed = []
        for j, chip in enumerate(chips):
            for a in range(n):
                copy(a, 1 + j, (*chip, c), me).wait_recv()
                passed.append(copy(a, 4 + j, (*chip, c), sibling))
                passed[-1].start()
        for a in range(n):
            copy(a, 0, sibling, me).wait_recv()
            for j, chip in enumerate(chips):
                copy(a, 4 + j, (*chip, 1 - c), me).wait_recv()
        for cp in first_copies() + passed:
            cp.wait_send()
        for a in range(n):
            local(a).wait()

    return start, finish


def all_gather8(blocks, name):
    n = len(blocks)

    def body(*refs):
        start, finish = _gather8_steps(refs[:n], refs[n:2 * n], *refs[2 * n:])
        start()
        finish()

    return pl.pallas_call(
        body, name=name, out_shape=[jax.ShapeDtypeStruct((8,) + b.shape, b.dtype) for b in blocks],
        in_specs=[ANY] * n, out_specs=[ANY] * n,
        scratch_shapes=[_dma_sems(n, 7), _dma_sems(n, 7), _dma_sems(n, 1)],
    )(*blocks)


def swap_cores(blocks, name):
    n = len(blocks)

    def body(*refs):
        x_refs, out_refs, (send_sems, recv_sems) = refs[:n], refs[n:2 * n], refs[2 * n:]
        x, y, c = _place()
        cps = [pltpu.make_async_remote_copy(src_ref=x_refs[a], dst_ref=out_refs[a], send_sem=send_sems.at[a, 0],
                                            recv_sem=recv_sems.at[a, 0], device_id=(x, y, 1 - c), device_id_type=MESH_IDS)
               for a in range(n)]
        for cp in cps:
            cp.start()
        for cp in cps:
            cp.wait()

    return pl.pallas_call(
        body, name=name, out_shape=[jax.ShapeDtypeStruct(b.shape, b.dtype) for b in blocks],
        in_specs=[ANY] * n, out_specs=[ANY] * n, scratch_shapes=[_dma_sems(n, 1), _dma_sems(n, 1)],
    )(*blocks)


def _exchange_steps(p_refs, out_refs, send_sems, recv_sems, local_sems):
    n = len(p_refs)
    x, y, c = _place()
    jm = 2 * x + y
    chips = [(1 - x, y), (x, 1 - y), (1 - x, 1 - y)]

    def local(a):
        return pltpu.make_async_copy(p_refs[a].at[jm], out_refs[a].at[jm], local_sems.at[a, 0])

    def sends():
        return [pltpu.make_async_remote_copy(
            src_ref=p_refs[a].at[2 * px + py], dst_ref=out_refs[a].at[jm], send_sem=send_sems.at[a, k],
            recv_sem=recv_sems.at[a, k], device_id=(px, py, c), device_id_type=MESH_IDS)
            for k, (px, py) in enumerate(chips) for a in range(n)]

    def start():
        for a in range(n):
            local(a).start()
        for cp in sends():
            cp.start()

    def finish():
        for k, (px, py) in enumerate(chips):
            for a in range(n):
                pltpu.make_async_remote_copy(
                    src_ref=p_refs[a].at[jm], dst_ref=out_refs[a].at[2 * px + py], send_sem=send_sems.at[a, k],
                    recv_sem=recv_sems.at[a, k], device_id=(px, py, c), device_id_type=MESH_IDS).wait_recv()
        for cp in sends():
            cp.wait_send()
        for a in range(n):
            local(a).wait()

    return start, finish


def exchange_chips(parts, name):
    n = len(parts)

    def body(*refs):
        start, finish = _exchange_steps(refs[:n], refs[n:2 * n], *refs[2 * n:])
        start()
        finish()

    return pl.pallas_call(
        body, name=name, out_shape=[jax.ShapeDtypeStruct(p.shape, p.dtype) for p in parts],
        in_specs=[ANY] * n, out_specs=[ANY] * n, scratch_shapes=[_dma_sems(n, 3), _dma_sems(n, 3), _dma_sems(n, 1)],
    )(*parts)


def _row_tile(rows, cols, itemsize=4, limit=1 << 21):
    top = min(rows, limit // (cols * itemsize)) // 16 * 16
    for cand in range(top, 0, -16):
        if rows % cand == 0:
            return cand
    return rows


def add_halves(a, b, kind, name):
    if kind == "col":
        m, k, n = a.shape
        n4 = n // 4
        tr = _row_tile(k, n4)
        in_spec = pl.BlockSpec((1, tr, n4), lambda j, h, i: (h, i, j))
        out_spec = pl.BlockSpec((1, 1, tr, n4), lambda j, h, i: (j, h, i, 0))
        grid, out_shape = (4, m, k // tr), (4, m, k, n4)
    elif kind == "row":
        m, k, n = a.shape
        k4 = k // 4
        tr = _row_tile(k4, n)
        nt = k4 // tr
        in_spec = pl.BlockSpec((1, tr, n), lambda j, h, i: (h, j * nt + i, 0))
        out_spec = pl.BlockSpec((1, 1, tr, n), lambda j, h, i: (j, h, i, 0))
        grid, out_shape = (4, m, nt), (4, m, k4, n)
    else:
        _, m, k, n = a.shape
        tr = _row_tile(k, n)
        in_spec = out_spec = pl.BlockSpec((1, 1, tr, n), lambda j, h, i: (j, h, i, 0))
        grid, out_shape = (4, m, k // tr), a.shape

    def body(a_ref, b_ref, s_ref):
        s_ref[...] = (a_ref[...] + b_ref[...].astype(F32)).astype(BF).reshape(s_ref.shape)

    return pl.pallas_call(body, name=name, grid=grid, in_specs=[in_spec, in_spec], out_specs=out_spec,
                          out_shape=jax.ShapeDtypeStruct(out_shape, BF))(a, b)


def sum_slots(a, name):
    s, m, k, n = a.shape
    tr = _row_tile(k, n, limit=(1 << 22) // s)

    def body(a_ref, o_ref):
        acc = a_ref[0].astype(F32)
        for j in range(1, s):
            acc = acc + a_ref[j].astype(F32)
        o_ref[...] = acc

    return pl.pallas_call(body, name=name, grid=(m, k // tr), in_specs=[pl.BlockSpec((s, 1, tr, n), lambda h, i: (0, h, i, 0))],
                          out_specs=pl.BlockSpec((1, tr, n), lambda h, i: (h, i, 0)),
                          out_shape=jax.ShapeDtypeStruct((m, k, n), F32))(a)


def sum_small(arrays, name):
    n = len(arrays)

    def body(*refs):
        for a_ref, o_ref in zip(refs[:n], refs[n:]):
            acc = a_ref[0]
            for j in range(1, 8):
                acc = acc + a_ref[j]
            o_ref[...] = acc

    return pl.pallas_call(body, name=name, out_shape=[jax.ShapeDtypeStruct(a.shape[1:], F32) for a in arrays])(*arrays)


COL_SHARDED = ("ada_w", "w_in", "mla_w_uq", "mla_w_uk", "mla_w_uv", "ffn_up", "ffn_conv_w")
ROW_SHARDED = ("w_out", "ffn_down")
GATHERED = ("ada_w", "w_in", "mla_w_uq", "mla_w_uk", "mla_w_uv", "w_out", "ffn_up", "ffn_down", "ffn_conv_w")
LAYER_GATHERED = GATHERED[:-1]
REDUCED = ("w_in", "mla_w_uq", "mla_w_uk", "mla_w_uv", "w_out", "ffn_up", "ffn_down")
SMALL = ("ada_b", "gla_gate_w", "gla_gate_b", "gla_norm_g", "ret_decay", "mla_q_norm_g", "mla_kv_norm_g",
         "ln1_g", "ln1_b", "ffn_conv_b", "ln2_g", "ln2_b")
WEIGHTS = ("c_ctx", "ada_w", "ada_b", "w_in", "gla_gate_w", "gla_gate_b", "gla_norm_g", "ret_decay", "mla_q_norm_g",
           "mla_kv_norm_g", "mla_w_uq", "mla_w_uk", "mla_w_uv", "w_out", "ln1_g", "ln1_b", "ffn_up", "ffn_conv_w",
           "ffn_conv_b", "ffn_down", "ln2_g", "ln2_b")
PACK = 16 * LANES
HALF_LAYERS = DEPTH // 2


def _pad_flat(v, n):
    return jnp.concatenate([v, jnp.zeros((n - v.shape[0],), v.dtype)]) if n > v.shape[0] else v


def _my_layers(a, c):
    return lax.dynamic_slice_in_dim(a, HALF_LAYERS * c, HALF_LAYERS, axis=0)


def layer_blocks(shards, l, c):
    out = []
    for n in LAYER_GATHERED:
        a = shards[n][l]
        out.append(lax.dynamic_slice_in_dim(a, c * (a.shape[0] // 2), a.shape[0] // 2, axis=0).astype(BF))
    return out


def layer_assemble(got):
    out = {}
    for n, g in zip(LAYER_GATHERED, got):
        _, k2, n4 = g.shape
        if n in ROW_SHARDED:
            out[n] = g.reshape(8 * k2, n4)
        else:
            out[n] = jnp.transpose(g.reshape(4, 2, k2, n4), (1, 2, 0, 3)).reshape(2 * k2, 4 * n4)
    return out


def gather_conv_taps(shard, c):
    got, = all_gather8([_my_layers(shard, c)], "gather_conv_taps")
    _, _, k, n4 = got.shape
    return jnp.transpose(got.reshape(4, 2, HALF_LAYERS, k, n4), (1, 2, 3, 0, 4)).reshape(DEPTH, k, 4 * n4)


def _reduce_kind(n, g):
    return "row" if n in ROW_SHARDED else ("col" if (g.shape[-1] // 4) % LANES == 0 else "pre")


def reduce_begin(g, c):
    keep, give, kinds = [], [], []
    for n in REDUCED:
        a, kind = g[n], _reduce_kind(n, g[n])
        if kind == "pre":
            a = jnp.transpose(a.reshape(a.shape[0], 4, a.shape[1] // 4), (1, 0, 2))
        axis = a.ndim - 1 if kind == "row" else a.ndim - 2
        half = a.shape[axis] // 2
        lead = (slice(None), None) if kind == "pre" else (None,)
        keep.append(lax.dynamic_slice_in_dim(a, half * c, half, axis=axis)[lead])
        give.append(lax.dynamic_slice_in_dim(a, half * (1 - c), half, axis=axis)[lead].astype(BF))
        kinds.append(kind)
    got = swap_cores(give, "grad_swap_cores")
    return [add_halves(a, b, kind, "grad_add_cores_" + n) for n, a, b, kind in zip(REDUCED, keep, got, kinds)]


def reduce_end(landed):
    return [sum_slots(a, "grad_sum_chips_" + n) for n, a in zip(REDUCED, landed)]


def reduce_assemble(layers, c):
    mine = [jnp.concatenate([layers[l][k] for l in range(DEPTH)], axis=0) for k in range(len(REDUCED))]
    theirs = swap_cores(mine, "grad_swap_back")
    out = {}
    for n, a, b in zip(REDUCED, mine, theirs):
        out[n] = jnp.concatenate([jnp.where(c == 0, a, b), jnp.where(c == 0, b, a)], axis=2 if n in ROW_SHARDED else 1)
    return out


def _pack_small(d, names):
    flat = jnp.concatenate([d[n].astype(F32).reshape(-1) for n in names])
    n = -(-flat.shape[0] // PACK) * PACK
    return _pad_flat(flat, n).reshape(n // LANES, LANES)


def _unpack_small(buf, like, names):
    flat, out, at = buf.reshape(-1), {}, 0
    for n in names:
        sz = like[n].size
        out[n] = flat[at:at + sz].reshape(like[n].shape)
        at += sz
    return out


def kernel(x, c, ctx, c_ctx, ada_w, ada_b, w_in, gla_gate_w, gla_gate_b, gla_norm_g, ret_decay, mla_q_norm_g, mla_kv_norm_g, mla_w_uq, mla_w_uk, mla_w_uv, w_out, ln1_g, ln1_b, ffn_up, ffn_conv_w, ffn_conv_b, ffn_down, ln2_g, ln2_b, loss_target, m_c_ctx, m_ada_w, m_ada_b, m_w_in, m_gla_gate_w, m_gla_gate_b, m_gla_norm_g, m_ret_decay, m_mla_q_norm_g, m_mla_kv_norm_g, m_mla_w_uq, m_mla_w_uk, m_mla_w_uv, m_w_out, m_ln1_g, m_ln1_b, m_ffn_up, m_ffn_conv_w, m_ffn_conv_b, m_ffn_down, m_ln2_g, m_ln2_b, v_c_ctx, v_ada_w, v_ada_b, v_w_in, v_gla_gate_w, v_gla_gate_b, v_gla_norm_g, v_ret_decay, v_mla_q_norm_g, v_mla_kv_norm_g, v_mla_w_uq, v_mla_w_uk, v_mla_w_uv, v_w_out, v_ln1_g, v_ln1_b, v_ffn_up, v_ffn_conv_w, v_ffn_conv_b, v_ffn_down, v_ln2_g, v_ln2_b):
    w = dict(c_ctx=c_ctx, ada_w=ada_w, ada_b=ada_b, w_in=w_in, gla_gate_w=gla_gate_w, gla_gate_b=gla_gate_b, gla_norm_g=gla_norm_g, ret_decay=ret_decay, mla_q_norm_g=mla_q_norm_g, mla_kv_norm_g=mla_kv_norm_g, mla_w_uq=mla_w_uq, mla_w_uk=mla_w_uk, mla_w_uv=mla_w_uv, w_out=w_out, ln1_g=ln1_g, ln1_b=ln1_b, ffn_up=ffn_up, ffn_conv_w=ffn_conv_w, ffn_conv_b=ffn_conv_b, ffn_down=ffn_down, ln2_g=ln2_g, ln2_b=ln2_b)
    m = dict(c_ctx=m_c_ctx, ada_w=m_ada_w, ada_b=m_ada_b, w_in=m_w_in, gla_gate_w=m_gla_gate_w, gla_gate_b=m_gla_gate_b, gla_norm_g=m_gla_norm_g, ret_decay=m_ret_decay, mla_q_norm_g=m_mla_q_norm_g, mla_kv_norm_g=m_mla_kv_norm_g, mla_w_uq=m_mla_w_uq, mla_w_uk=m_mla_w_uk, mla_w_uv=m_mla_w_uv, w_out=m_w_out, ln1_g=m_ln1_g, ln1_b=m_ln1_b, ffn_up=m_ffn_up, ffn_conv_w=m_ffn_conv_w, ffn_conv_b=m_ffn_conv_b, ffn_down=m_ffn_down, ln2_g=m_ln2_g, ln2_b=m_ln2_b)
    v = dict(c_ctx=v_c_ctx, ada_w=v_ada_w, ada_b=v_ada_b, w_in=v_w_in, gla_gate_w=v_gla_gate_w, gla_gate_b=v_gla_gate_b, gla_norm_g=v_gla_norm_g, ret_decay=v_ret_decay, mla_q_norm_g=v_mla_q_norm_g, mla_kv_norm_g=v_mla_kv_norm_g, mla_w_uq=v_mla_w_uq, mla_w_uk=v_mla_w_uk, mla_w_uv=v_mla_w_uv, w_out=v_w_out, ln1_g=v_ln1_g, ln1_b=v_ln1_b, ffn_up=v_ffn_up, ffn_conv_w=v_ffn_conv_w, ffn_conv_b=v_ffn_conv_b, ffn_down=v_ffn_down, ln2_g=v_ln2_g, ln2_b=v_ln2_b)
    xi, yi, ci = _place()
    chip = 2 * xi + yi

    whole = {n: w[n] for n in WEIGHTS if n not in GATHERED and n != "c_ctx"}
    whole["ffn_conv_w"] = gather_conv_taps(ffn_conv_w, ci)
    first_big = layer_assemble(all_gather8(layer_blocks(w, 0, ci), "gather_layer0"))
    loss, grad_x, gw, dmod, s8, lws, reduced = local_step(
        x, c, ctx, c_ctx, whole, loss_target, first_big,
        lambda l: layer_blocks(w, l, ci), lambda l, got: layer_assemble(got),
        (lambda g: reduce_begin(g, ci), reduce_end))
    loss = lax.psum(loss, ("x", "y", "c"))

    dsil = jnp.zeros((8, D_MODEL), F32)
    for i in range(DEPTH):
        dsil = dsil + matmul(dmod[i], lws[i]["ada_w"], "nt", F32, 8, 1024, 1536, "ada_dsilu")

    grads = reduce_assemble(reduced, ci)

    small = {n: gw[n] for n in SMALL if n != "ada_b"}
    small.update(dsil=dsil[4])
    names = tuple(small)
    conv_g = gw["ffn_conv_w"].reshape(DEPTH * 3, 2 * D_FF)
    ev_small, ev_dmod, ev_s8, ev_conv = all_gather8(
        [_pack_small(small, names), dmod.reshape(DEPTH * 8, 6 * D_MODEL), s8, conv_g], "gather_small")
    sm_small, sm_dmod, sm_conv = sum_small([ev_small, ev_dmod, ev_conv], "sum_small")
    summed = _unpack_small(sm_small, small, names)
    for n in SMALL:
        if n != "ada_b":
            grads[n] = summed[n]
    grads["ada_b"] = jnp.sum(sm_dmod.reshape(DEPTH, 8, 6 * D_MODEL)[:, :5], axis=1)
    sg = jax.nn.sigmoid(c_ctx)
    grads["c_ctx"] = summed["dsil"] * (sg * (1.0 + c_ctx * (1.0 - sg)))
    ccols = ffn_conv_w.shape[2]
    grads["ffn_conv_w"] = lax.dynamic_slice_in_dim(sm_conv.reshape(DEPTH, 3, 2 * D_FF), chip * ccols, ccols, axis=2)
    s_all = ev_s8.reshape(64, D_MODEL)
    d_all = jnp.transpose(ev_dmod.reshape(8, DEPTH, 8, 6 * D_MODEL), (1, 0, 2, 3)).reshape(DEPTH, 64, 6 * D_MODEL)
    cols = ada_w.shape[2]
    g_ada = []
    for i in range(DEPTH):
        d_mine = lax.dynamic_slice_in_dim(d_all[i], chip * cols, cols, axis=1)
        g_ada.append(matmul(s_all, d_mine, "tn", F32, 1024, cols, 64, "ada_dw"))
    grads["ada_w"] = jnp.stack(g_ada)

    delta, new_m, new_v = {}, {}, {}
    for n in GATHERED:
        shp = w[n].shape
        v2 = lambda a: a.reshape(-1, shp[-1])
        d_, m_, v_ = adamw(v2(w[n]), v2(grads[n]), v2(m[n]), v2(v[n]), "adamw_" + n)
        delta[n], new_m[n], new_v[n] = d_.reshape(shp), m_.reshape(shp), v_.reshape(shp)
    rep = tuple(n for n in WEIGHTS if n not in GATHERED)
    pk = lambda d: _pack_small({n: d[n] for n in rep}, rep)
    d_, m_, v_ = adamw(pk(w), pk(grads), pk(m), pk(v), "adamw_small")
    like = {n: w[n] for n in rep}
    delta.update(_unpack_small(d_, like, rep))
    new_m.update(_unpack_small(m_, like, rep))
    new_v.update(_unpack_small(v_, like, rep))
    grads = {n: grads[n].reshape(w[n].shape) for n in WEIGHTS}
    return (loss, grad_x, *[grads[n] for n in WEIGHTS], *[delta[n] for n in WEIGHTS], *[new_m[n] for n in WEIGHTS],
            *[new_v[n] for n in WEIGHTS])
```
